```python
import math
import jax, jax.numpy as jnp
from jax import lax
import numpy as np

D_MODEL = 1024
BATCH = 16
SEQ = 4096
DEPTH = 1

CHUNK = 64
Q_BLOCK = 128
EPS = 1e-6
M_HEADS = 4
M_DK = 128
M_DV = 256
M_QK = M_HEADS * M_DK
M_V = M_HEADS * M_DV
CONV_W = 4
D_HEADS = 8
D_DH = 64
D_QK = D_HEADS * 2 * D_DH
D_V = D_HEADS * 2 * D_DH
X_HEADS = 4
X_DH = D_MODEL // X_HEADS
N_MEM = 256
N_EXPERTS = 32
TOP_K = 4
D_FF = D_MODEL
SWIGLU_LIMIT = 7.0
SWIGLU_ALPHA = 1.702
N_BRANCHES = 2
IN_SIZES = [M_QK, M_QK, M_V, M_V, 2 * M_HEADS, D_QK, D_QK, D_V, N_BRANCHES * D_MODEL]
N_IN = sum(IN_SIZES)
IN_SPLITS = np.cumsum(IN_SIZES[:-1]).tolist()

kernel_name = "hybrid_mlstm_diffattn_moe_stream_block"


def rmsnorm(x, g):
    xf = x.astype(jnp.float32)
    y = xf * lax.rsqrt(jnp.mean(xf * xf, axis=-1, keepdims=True) + EPS)
    return (y * g.astype(jnp.float32)).astype(x.dtype)


def causal_conv(x, w):
    S = x.shape[1]
    xp = jnp.pad(x, ((0, 0), (CONV_W - 1, 0), (0, 0)))
    return sum(w[j] * xp[:, j:j + S] for j in range(CONV_W))


def mlstm_chunkwise(q, k, v, i_pre, f_pre):
    B, S, H, DK = q.shape
    DV = v.shape[-1]
    NC, L = S // CHUNK, CHUNK
    f32 = jnp.float32
    q = q.astype(f32) * (DK ** -0.5)
    qc = q.reshape(B, NC, L, H, DK).transpose(1, 0, 3, 2, 4)
    kc = k.astype(f32).reshape(B, NC, L, H, DK).transpose(1, 0, 3, 2, 4)
    vc = v.astype(f32).reshape(B, NC, L, H, DV).transpose(1, 0, 3, 2, 4)
    fc = jax.nn.log_sigmoid(f_pre.astype(f32)).reshape(B, NC, L, H).transpose(1, 0, 3, 2)
    ic = i_pre.astype(f32).reshape(B, NC, L, H).transpose(1, 0, 3, 2)
    causal = jnp.tril(jnp.ones((L, L), dtype=bool))

    def step(carry, inp):
        C, n, m = carry
        qb, kb, vb, fb, ib = inp
        b = jnp.cumsum(fb, axis=-1)
        b_last = b[..., -1]
        dm = jnp.where(causal, b[..., :, None] - b[..., None, :] + ib[..., None, :], -jnp.inf)
        inter = b + m[..., None]
        m_t = jnp.maximum(inter, jnp.max(dm, axis=-1))
        w_inter = jnp.exp(inter - m_t)
        s = jnp.einsum('bhtd,bhsd->bhts', qb, kb) * jnp.exp(dm - m_t[..., None])
        num = jnp.einsum('bhts,bhsv->bhtv', s, vb) + w_inter[..., None] * jnp.einsum('bhvd,bhtd->bhtv', C, qb)
        den = jnp.sum(s, axis=-1) + w_inter * jnp.einsum('bhd,bhtd->bht', n, qb)
        h = num / jnp.maximum(jnp.abs(den), jnp.exp(-m_t))[..., None]
        g = b_last[..., None] - b + ib
        m_new = jnp.maximum(b_last + m, jnp.max(g, axis=-1))
        wk = jnp.exp(g - m_new[..., None])
        decay = jnp.exp(b_last + m - m_new)
        C_new = decay[..., None, None] * C + jnp.einsum('bhs,bhsv,bhsd->bhvd', wk, vb, kb)
        n_new = decay[..., None] * n + jnp.einsum('bhs,bhsd->bhd', wk, kb)
        return (C_new, n_new, m_new), h

    init = (jnp.zeros((B, H, DV, DK), f32), jnp.zeros((B, H, DK), f32), jnp.zeros((B, H), f32))
    _, hs = lax.scan(step, init, (qc, kc, vc, fc, ic))
    return hs.transpose(1, 0, 3, 2, 4).reshape(B, S, H, DV).astype(v.dtype)


def diff_attention(q, k, v, lam, gain, lam_init):
    B, S, H = q.shape[:3]
    nb = S // Q_BLOCK
    q = q * (D_DH ** -0.5)
    k_chunk = jnp.arange(S) // CHUNK
    qb = q.reshape(B, nb, Q_BLOCK, H, 2, D_DH).transpose(1, 0, 2, 3, 4, 5)

    def block(args):
        qblk, idx = args
        q_chunk = (idx * Q_BLOCK + jnp.arange(Q_BLOCK)) // CHUNK
        mask = k_chunk[None, :] <= q_chunk[:, None]
        s = jnp.einsum('bqhcd,bkhcd->bhcqk', qblk, k).astype(jnp.float32)
        p = jax.nn.softmax(jnp.where(mask, s, -jnp.inf), axis=-1)
        a = p[:, :, 0] - lam * p[:, :, 1]
        return jnp.einsum('bhqk,bkhv->bqhv', a.astype(v.dtype), v)

    out = lax.map(block, (qb, jnp.arange(nb)))
    out = out.transpose(1, 0, 2, 3, 4).reshape(B, S, H, 2 * D_DH)
    return rmsnorm(out, gain) * (1.0 - lam_init)


def hybrid_mixer(h, w_in, conv_w, b_if, m_gain, lam_p, d_gain, w_bm, w_bd, b_gate, w_out, lam_init):
    B, S, _ = h.shape
    z = h @ w_in
    qm, km, vm, om, ifm, qd, kd, vd, gates = jnp.split(z, IN_SPLITS, axis=-1)
    qk = jax.nn.silu(causal_conv(jnp.concatenate([qm, km], axis=-1), conv_w))
    qm, km = qk[..., :M_QK], qk[..., M_QK:]
    hm = mlstm_chunkwise(qm.reshape(B, S, M_HEADS, M_DK), km.reshape(B, S, M_HEADS, M_DK),
                         vm.reshape(B, S, M_HEADS, M_DV),
                         ifm[..., :M_HEADS] + b_if[:M_HEADS], ifm[..., M_HEADS:] + b_if[M_HEADS:])
    hm = jax.nn.sigmoid(om) * rmsnorm(hm, m_gain).reshape(B, S, M_V)
    branch_m = hm @ w_bm
    lp = lam_p.astype(jnp.float32)
    lam = jnp.exp(jnp.sum(lp[0] * lp[1])) - jnp.exp(jnp.sum(lp[2] * lp[3])) + lam_init
    hd = diff_attention(qd.reshape(B, S, D_HEADS, 2, D_DH), kd.reshape(B, S, D_HEADS, 2, D_DH),
                        vd.reshape(B, S, D_HEADS, 2 * D_DH), lam, d_gain, lam_init)
    branch_d = hd.reshape(B, S, D_V) @ w_bd
    g = jax.nn.sigmoid(gates + b_gate)
    return (g[..., :D_MODEL] * branch_m + g[..., D_MODEL:] * branch_d) @ w_out


def cross_attention(h, memn, wq, wkv, wo):
    B, S, _ = h.shape
    M = memn.shape[1]
    q = (h @ wq).reshape(B, S, X_HEADS, X_DH) * (X_DH ** -0.5)
    kv = memn @ wkv
    k = kv[..., :D_MODEL].reshape(B, M, X_HEADS, X_DH)
    v = kv[..., D_MODEL:].reshape(B, M, X_HEADS, X_DH)
    p = jax.nn.softmax(jnp.einsum('bshd,bmhd->bhsm', q, k).astype(jnp.float32), axis=-1)
    o = jnp.einsum('bhsm,bmhd->bshd', p.astype(v.dtype), v).reshape(B, S, D_MODEL)
    return o @ wo


def moe(h, w_r, b_r, w_gu, b_gu, w_dn, b_dn):
    B, S, D = h.shape
    t = h.reshape(B * S, D)
    logits = (t @ w_r + b_r).astype(jnp.float32)
    top_v, top_i = lax.top_k(logits, TOP_K)
    top_w = jax.nn.softmax(top_v, axis=-1)
    combine = jnp.sum(jax.nn.one_hot(top_i, N_EXPERTS, dtype=jnp.float32) * top_w[..., None], axis=1)

    def expert(acc, p):
        wgu, bgu, wdn, bdn, c = p
        gu = t @ wgu + bgu
        gate = jnp.minimum(gu[:, :D_FF], SWIGLU_LIMIT)
        up = jnp.clip(gu[:, D_FF:], -SWIGLU_LIMIT, SWIGLU_LIMIT)
        y = ((up + 1.0) * (gate * jax.nn.sigmoid(SWIGLU_ALPHA * gate))) @ wdn + bdn
        return acc + c[:, None] * y, None

    acc, _ = lax.scan(expert, jnp.zeros_like(t), (w_gu, b_gu, w_dn, b_dn, combine.T.astype(t.dtype)))
    return acc.reshape(B, S, D)


def setup_inputs(seed: int = 0) -> dict:
    key = jax.random.key(seed)
    ks = jax.random.split(key, 32)
    f32 = jnp.float32
    D, E, L = D_MODEL, N_EXPERTS, DEPTH

    def nrm(k, shape, scale):
        return jax.random.normal(k, shape, f32) * scale

    def gain(k, shape):
        return 1.0 + nrm(k, shape, 0.02)

    f_bias = jnp.broadcast_to(jnp.linspace(3.0, 6.0, M_HEADS, dtype=f32), (L, M_HEADS))
    b_if = jnp.concatenate([nrm(ks[3], (L, M_HEADS), 0.1), f_bias + nrm(ks[4], (L, M_HEADS), 0.1)], axis=-1)
    return {
        "x": nrm(ks[0], (BATCH, SEQ, D), 1.0),
        "mem": nrm(ks[1], (BATCH, N_MEM, D), 1.0),
        "norm_mix": gain(ks[2], (L, D)),
        "w_in": nrm(ks[5], (L, D, N_IN), D ** -0.5),
        "conv_w": nrm(ks[6], (L, CONV_W, 2 * M_QK), CONV_W ** -0.5),
        "b_if": b_if,
        "mlstm_gain": gain(ks[7], (L, M_HEADS, M_DV)),
        "diff_lambda": nrm(ks[8], (L, 4, D_DH), 0.1),
        "diff_gain": gain(ks[9], (L, 2 * D_DH)),
        "w_branch_m": nrm(ks[10], (L, M_V, D), M_V ** -0.5),
        "w_branch_d": nrm(ks[11], (L, D_V, D), D_V ** -0.5),
        "b_gate": nrm(ks[12], (L, N_BRANCHES * D), 0.1),
        "w_out": nrm(ks[13], (L, D, D), D ** -0.5),
        "norm_xattn": gain(ks[14], (L, D)),
        "norm_mem": gain(ks[15], (L, D)),
        "wq_x": nrm(ks[16], (L, D, D), D ** -0.5),
        "wkv_x": nrm(ks[17], (L, D, 2 * D), D ** -0.5),
        "wo_x": nrm(ks[18], (L, D, D), D ** -0.5),
        "norm_ffn": gain(ks[19], (L, D)),
        "w_router": nrm(ks[20], (L, D, E), D ** -0.5),
        "b_router": nrm(ks[21], (L, E), 0.01),
        "w_gu": nrm(ks[22], (L, E, D, 2 * D_FF), D ** -0.5),
        "b_gu": nrm(ks[23], (L, E, 2 * D_FF), 0.02),
        "w_dn": nrm(ks[24], (L, E, D_FF, D), D_FF ** -0.5),
        "b_dn": nrm(ks[25], (L, E, D), 0.02),
        "norm_final": gain(ks[26], (D,)),
    }


def reference(x, mem, norm_mix, w_in, conv_w, b_if, mlstm_gain, diff_lambda, diff_gain,
              w_branch_m, w_branch_d, b_gate, w_out, norm_xattn, norm_mem, wq_x, wkv_x, wo_x,
              norm_ffn, w_router, b_router, w_gu, b_gu, w_dn, b_dn, norm_final):
    for l in range(DEPTH):
        lam_init = 0.8 - 0.6 * math.exp(-0.3 * l)
        x = x + hybrid_mixer(rmsnorm(x, norm_mix[l]), w_in[l], conv_w[l], b_if[l], mlstm_gain[l],
                             diff_lambda[l], diff_gain[l], w_branch_m[l], w_branch_d[l], b_gate[l],
                             w_out[l], lam_init)
        x = x + cross_attention(rmsnorm(x, norm_xattn[l]), rmsnorm(mem, norm_mem[l]),
                                wq_x[l], wkv_x[l], wo_x[l])
        x = x + moe(rmsnorm(x, norm_ffn[l]), w_router[l], b_router[l], w_gu[l], b_gu[l],
                    w_dn[l], b_dn[l])
    return rmsnorm(x, norm_final)
```

```python
import functools
import math

import jax
import jax.numpy as jnp
from jax import lax
from jax.experimental import pallas as pl
from jax.experimental.pallas import tpu as pltpu

F32 = jnp.float32
BF16 = jnp.bfloat16
U32 = jnp.uint32
I32 = jnp.int32

EPS = 1e-6
CHUNK = 64
D_MODEL = 1024
M_HEADS = 4
M_DK = 128
M_DV = 256
M_QK = M_HEADS * M_DK
M_V = M_HEADS * M_DV
CONV_W = 4
D_HEADS = 8
D_DH = 64
D_QK = D_HEADS * 2 * D_DH
D_V = D_HEADS * 2 * D_DH
X_HEADS = 4
X_DH = D_MODEL // X_HEADS
N_EXPERTS = 32
TOP_K = 4
D_FF = D_MODEL
SWIGLU_LIMIT = 7.0
SWIGLU_ALPHA = 1.702

LANES = 128
SUBLANES = 8
N_MAIN = 2 * M_QK + 2 * M_V + 2 * D_QK + D_V + 2 * D_MODEL
OFF_QM, OFF_KM, OFF_VM, OFF_OM = 0, M_QK, 2 * M_QK, 2 * M_QK + M_V
OFF_QD = OFF_OM + M_V
OFF_KD = OFF_QD + D_QK
OFF_VD = OFF_KD + D_QK
OFF_G = OFF_VD + D_V

VMEM_LIMIT = 56 * 1024 * 1024


def _cparams(sem, vmem=VMEM_LIMIT):
    return pltpu.CompilerParams(dimension_semantics=sem, vmem_limit_bytes=vmem)


def _rms(x, g):
    return x * lax.rsqrt(jnp.mean(x * x, axis=-1, keepdims=True) + EPS) * g


def _split_bf16(x):
    hi = x.astype(BF16)
    lo = (x - hi.astype(F32)).astype(BF16)
    return hi, lo


def _dot(a, b):
    return jnp.dot(a, b, preferred_element_type=F32)


def _dot_nt(a, b):
    return lax.dot_general(a, b, (((1,), (1,)), ((), ())), preferred_element_type=F32)


def _sigmoid(x):
    return 1.0 / (1.0 + jnp.exp(-x))


def _log_sigmoid(x):
    return jnp.minimum(x, 0.0) - jnp.log(1.0 + jnp.exp(-jnp.abs(x)))


def _pack_bf16_pairs(x):
    w = x.shape[1] // 2
    u = lax.bitcast_convert_type(x, U32)
    r = (u + jnp.uint32(0x7FFF) + ((u >> 16) & jnp.uint32(1))) >> 16
    return r[:, :w] | (r[:, w:] << 16)


def _unpack_bf16_pairs(p):
    lo = lax.bitcast_convert_type(p << 16, F32)
    hi = lax.bitcast_convert_type(p & jnp.uint32(0xFFFF0000), F32)
    return lo, hi


def _inproj_kernel(x_ref, g_ref, w_ref, wif_ref, wift_ref, z_ref, zif_ref, zift_ref, hn_ref):
    @pl.when(pl.program_id(1) == 0)
    def _():
        hn = _rms(x_ref[...], g_ref[...]).astype(BF16)
        hn_ref[...] = hn
        zif_ref[...] = _dot(hn, wif_ref[...])
        zift_ref[...] = _dot_nt(wift_ref[...], hn)

    z_ref[...] = _dot(hn_ref[...], w_ref[...]).astype(BF16)


def _inproj(x2d, g, w_main, w_if, w_ift, tm, tn):
    T = x2d.shape[0]
    return pl.pallas_call(
        _inproj_kernel,
        grid=(T // tm, N_MAIN // tn),
        in_specs=[
            pl.BlockSpec((tm, D_MODEL), lambda i, j: (i, 0)),
            pl.BlockSpec((1, D_MODEL), lambda i, j: (0, 0)),
            pl.BlockSpec((D_MODEL, tn), lambda i, j: (0, j)),
            pl.BlockSpec((D_MODEL, LANES), lambda i, j: (0, 0)),
            pl.BlockSpec((SUBLANES, D_MODEL), lambda i, j: (0, 0)),
        ],
        out_specs=[
            pl.BlockSpec((tm, tn), lambda i, j: (i, j)),
            pl.BlockSpec((tm, LANES), lambda i, j: (i, 0)),
            pl.BlockSpec((SUBLANES, tm), lambda i, j: (0, i)),
        ],
        out_shape=[
            jax.ShapeDtypeStruct((T, N_MAIN), BF16),
            jax.ShapeDtypeStruct((T, LANES), F32),
            jax.ShapeDtypeStruct((SUBLANES, T), F32),
        ],
        scratch_shapes=[pltpu.VMEM((tm, D_MODEL), BF16)],
        compiler_params=_cparams(("arbitrary", "arbitrary")),
        name="inproj",
    )(x2d, g, w_main, w_if, w_ift)


def _mlstm_kernel(q_ref, k_ref, v_ref, om_ref, zif_ref, zift_ref, cw_ref, bif_ref, bift_ref, mg_ref,
                  out_ref, qc_ref, kc_ref, carry_ref, c_ref, n_ref, m_ref, *, ts):
    nchunk = ts // CHUNK
    L = CHUNK

    @pl.when(pl.program_id(1) == 0)
    def _():
        carry_ref[...] = jnp.zeros_like(carry_ref)
        c_ref[...] = jnp.zeros_like(c_ref)
        n_ref[...] = jnp.zeros_like(n_ref)
        m_ref[...] = jnp.zeros_like(m_ref)

    row8 = lax.broadcasted_iota(I32, (SUBLANES, M_QK), 0)

    def conv_silu(x, prev8, w):
        acc = w[CONV_W - 1:CONV_W, :] * x
        for s in range(1, CONV_W):
            xs = pltpu.roll(x, s, 0)
            top = jnp.where(row8 < s, pltpu.roll(prev8, s, 0), xs[0:SUBLANES])
            xs = jnp.concatenate([top, xs[SUBLANES:]], axis=0)
            acc = acc + w[CONV_W - 1 - s:CONV_W - s, :] * xs
        return acc * _sigmoid(acc)

    def conv_body(c, carry):
        r0 = pl.multiple_of(c * L, L)
        xq = q_ref[pl.ds(r0, L), :].astype(F32)
        xk = k_ref[pl.ds(r0, L), :].astype(F32)
        yq = conv_silu(xq, carry_ref[:, 0:M_QK], cw_ref[:, 0:M_QK]) * (M_DK ** -0.5)
        yk = conv_silu(xk, carry_ref[:, M_QK:2 * M_QK], cw_ref[:, M_QK:2 * M_QK])
        qc_ref[pl.ds(r0, L), :] = yq.astype(BF16)
        kc_ref[pl.ds(r0, L), :] = yk.astype(BF16)
        carry_ref[:, 0:M_QK] = xq[L - SUBLANES:L]
        carry_ref[:, M_QK:2 * M_QK] = xk[L - SUBLANES:L]
        return carry

    lax.fori_loop(0, nchunk, conv_body, 0)

    ti = lax.broadcasted_iota(I32, (L, L), 0)
    si = lax.broadcasted_iota(I32, (L, L), 1)
    causal = si <= ti
    tril = jnp.where(causal, 1.0, 0.0).astype(BF16)
    triu = jnp.where(ti <= si, 1.0, 0.0).astype(BF16)

    def chunk_body(c, carry):
        r0 = pl.multiple_of(c * L, L)
        g_col = zif_ref[pl.ds(r0, L), :] + bif_ref[...]
        g_row = zift_ref[c] + bift_ref[:, 0:1]
        lf_col = _log_sigmoid(g_col)
        lf_row = _log_sigmoid(g_row)
        ch, cl = _split_bf16(lf_col)
        b_col_all = _dot(tril, ch) + _dot(tril, cl)
        rh, rl = _split_bf16(lf_row)
        b_row_all = _dot(rh, triu) + _dot(rl, triu)
        for h in range(M_HEADS):
            i_col = g_col[:, h:h + 1]
            b_col = b_col_all[:, M_HEADS + h:M_HEADS + h + 1]
            i_row = g_row[h:h + 1, :]
            b_row = b_row_all[M_HEADS + h:M_HEADS + h + 1, :]
            b_last = b_col[L - 1:L, :]
            m_prev = m_ref[h:h + 1, 0:1]
            dm = jnp.where(causal, b_col - b_row + i_row, -jnp.inf)
            inter = b_col + m_prev
            m_t = jnp.maximum(inter, jnp.max(dm, axis=-1, keepdims=True))
            w_inter = jnp.exp(inter - m_t)
            q = qc_ref[pl.ds(r0, L), h * M_DK:(h + 1) * M_DK]
            k = kc_ref[pl.ds(r0, L), h * M_DK:(h + 1) * M_DK]
            v = v_ref[pl.ds(r0, L), h * M_DV:(h + 1) * M_DV]
            s = _dot_nt(q, k) * jnp.exp(dm - m_t)
            c_old = c_ref[h]
            n_old = n_ref[h:h + 1, :]
            num = _dot(s.astype(BF16), v) + w_inter * _dot(q, c_old.astype(BF16))
            qn = jnp.sum(q.astype(F32) * n_old, axis=-1, keepdims=True)
            den = jnp.sum(s, axis=-1, keepdims=True) + w_inter * qn
            hv = num / jnp.maximum(jnp.abs(den), jnp.exp(-m_t))
            gk_col = b_last - b_col + i_col
            gk_row = b_last - b_row + i_row
            m_new = jnp.maximum(b_last + m_prev, jnp.max(gk_row, axis=-1, keepdims=True))
            wk = jnp.exp(gk_col - m_new)
            decay = jnp.exp(b_last + m_prev - m_new)
            kw = k.astype(F32) * wk
            c_ref[h] = decay * c_old + _dot(kw.T.astype(BF16), v)
            n_ref[h:h + 1, :] = decay * n_old + jnp.sum(kw, axis=0, keepdims=True)
            m_ref[h:h + 1, :] = jnp.broadcast_to(m_new, (1, LANES))
            hn = _rms(hv, mg_ref[:, h * M_DV:(h + 1) * M_DV])
            og = _sigmoid(om_ref[pl.ds(r0, L), h * M_DV:(h + 1) * M_DV].astype(F32))
            out_ref[pl.ds(r0, L), h * M_DV:(h + 1) * M_DV] = (og * hn).astype(BF16)
        return carry

    lax.fori_loop(0, nchunk, chunk_body, 0)


def _mlstm(z, zif, zift3, conv_w, bif, bift, m_gain, B, S, ts):
    T = B * S
    nt = S // ts
    nck = ts // CHUNK
    row = lambda b, t: b * nt + t
    return pl.pallas_call(
        functools.partial(_mlstm_kernel, ts=ts),
        grid=(B, nt),
        in_specs=[
            pl.BlockSpec((ts, M_QK), lambda b, t: (row(b, t), OFF_QM // M_QK)),
            pl.BlockSpec((ts, M_QK), lambda b, t: (row(b, t), OFF_KM // M_QK)),
            pl.BlockSpec((ts, M_V), lambda b, t: (row(b, t), OFF_VM // M_V)),
            pl.BlockSpec((ts, M_V), lambda b, t: (row(b, t), OFF_OM // M_V)),
            pl.BlockSpec((ts, LANES), lambda b, t: (row(b, t), 0)),
            pl.BlockSpec((nck, SUBLANES, CHUNK), lambda b, t: (row(b, t), 0, 0)),
            pl.BlockSpec((CONV_W, 2 * M_QK), lambda b, t: (0, 0)),
            pl.BlockSpec((1, LANES), lambda b, t: (0, 0)),
            pl.BlockSpec((SUBLANES, LANES), lambda b, t: (0, 0)),
            pl.BlockSpec((1, M_V), lambda b, t: (0, 0)),
        ],
        out_specs=pl.BlockSpec((ts, M_V), lambda b, t: (row(b, t), 0)),
        out_shape=jax.ShapeDtypeStruct((T, M_V), BF16),
        scratch_shapes=[
            pltpu.VMEM((ts, M_QK), BF16),
            pltpu.VMEM((ts, M_QK), BF16),
            pltpu.VMEM((SUBLANES, 2 * M_QK), F32),
            pltpu.VMEM((M_HEADS, M_DK, M_DV), F32),
            pltpu.VMEM((SUBLANES, M_DK), F32),
            pltpu.VMEM((SUBLANES, LANES), F32),
        ],
        compiler_params=_cparams(("arbitrary", "arbitrary")),
        name="mlstm",
    )(z, z, z, z, zif, zift3, conv_w, bif, bift, m_gain)


def _diffattn_kernel(q_ref, k_ref, v_ref, lam_ref, gain_ref, out_ref,
                     m1_ref, l1_ref, a1_ref, m2_ref, l2_ref, a2_ref, *, tq, lam_init):
    qi = pl.program_id(2)
    q = q_ref[...]
    lane = lax.broadcasted_iota(I32, (1, 2 * D_DH), 1)
    scale = jnp.asarray(D_DH ** -0.5, BF16)
    q1 = jnp.where(lane < D_DH, q, jnp.zeros_like(q)) * scale
    q2 = jnp.where(lane >= D_DH, q, jnp.zeros_like(q)) * scale

    m1_ref[...] = jnp.full_like(m1_ref, -jnp.inf)
    m2_ref[...] = jnp.full_like(m2_ref, -jnp.inf)
    l1_ref[...] = jnp.zeros_like(l1_ref)
    l2_ref[...] = jnp.zeros_like(l2_ref)
    a1_ref[...] = jnp.zeros_like(a1_ref)
    a2_ref[...] = jnp.zeros_like(a2_ref)

    def online(s, v, m_ref, l_ref, a_ref):
        m_old = m_ref[...]
        m_new = jnp.maximum(m_old, jnp.max(s, axis=-1, keepdims=True))
        p = jnp.exp(s - m_new)
        alpha = jnp.exp(m_old - m_new)
        l_ref[...] = alpha * l_ref[...] + jnp.sum(p, axis=-1, keepdims=True)
        a_ref[...] = alpha * a_ref[...] + _dot(p.astype(BF16), v)
        m_ref[...] = m_new

    def block(j, mask):
        r0 = pl.multiple_of(j * tq, tq)
        k = k_ref[pl.ds(r0, tq), :]
        v = v_ref[pl.ds(r0, tq), :]
        s1 = _dot_nt(q1, k)
        s2 = _dot_nt(q2, k)
        if mask is not None:
            s1 = jnp.where(mask, s1, -jnp.inf)
            s2 = jnp.where(mask, s2, -jnp.inf)
        online(s1, v, m1_ref, l1_ref, a1_ref)
        online(s2, v, m2_ref, l2_ref, a2_ref)

    def body(j, carry):
        block(j, None)
        return carry

    lax.fori_loop(0, qi, body, 0)
    rq = lax.broadcasted_iota(I32, (tq, tq), 0) // CHUNK
    ck = lax.broadcasted_iota(I32, (tq, tq), 1) // CHUNK
    block(qi, ck <= rq)

    lp = lam_ref[...]
    lam = (jnp.exp(jnp.sum(lp[0:1, :] * lp[1:2, :], axis=-1, keepdims=True))
           - jnp.exp(jnp.sum(lp[2:3, :] * lp[3:4, :], axis=-1, keepdims=True)) + lam_init)
    o = a1_ref[...] / l1_ref[...] - lam * (a2_ref[...] / l2_ref[...])
    out_ref[...] = (_rms(o, gain_ref[...]) * (1.0 - lam_init)).astype(BF16)


def _diffattn(z, lam_p, d_gain, B, S, tq, lam_init):
    T = B * S
    nq = S // tq
    w = 2 * D_DH
    return pl.pallas_call(
        functools.partial(_diffattn_kernel, tq=tq, lam_init=lam_init),
        grid=(B, D_HEADS, nq),
        in_specs=[
            pl.BlockSpec((tq, w), lambda b, h, i: (b * nq + i, OFF_QD // w + h)),
            pl.BlockSpec((S, w), lambda b, h, i: (b, OFF_KD // w + h)),
            pl.BlockSpec((S, w), lambda b, h, i: (b, OFF_VD // w + h)),
            pl.BlockSpec((4, D_DH), lambda b, h, i: (0, 0)),
            pl.BlockSpec((1, w), lambda b, h, i: (0, 0)),
        ],
        out_specs=pl.BlockSpec((tq, w), lambda b, h, i: (b * nq + i, h)),
        out_shape=jax.ShapeDtypeStruct((T, D_V), BF16),
        scratch_shapes=[
            pltpu.VMEM((tq, 1), F32), pltpu.VMEM((tq, 1), F32), pltpu.VMEM((tq, w), F32),
            pltpu.VMEM((tq, 1), F32), pltpu.VMEM((tq, 1), F32), pltpu.VMEM((tq, w), F32),
        ],
        compiler_params=_cparams(("arbitrary", "arbitrary", "arbitrary")),
        name="diffattn",
    )(z, z, z, lam_p, d_gain)


def _merge_kernel(hm_ref, hd_ref, gz_ref, x_ref, wbm_ref, wbd_ref, wout_ref, bg_ref, out_ref):
    bm = _dot(hm_ref[...], wbm_ref[...])
    bd = _dot(hd_ref[...], wbd_ref[...])
    g = _sigmoid(gz_ref[...].astype(F32) + bg_ref[...])
    merged = g[:, :D_MODEL] * bm + g[:, D_MODEL:] * bd
    out_ref[...] = x_ref[...] + _dot(merged.astype(BF16), wout_ref[...])


def _merge(hm, hd, z, x2d, w_bm, w_bd, w_out, b_gate, tm):
    T = x2d.shape[0]
    full = lambda i: (0, 0)
    return pl.pallas_call(
        _merge_kernel,
        grid=(T // tm,),
        in_specs=[
            pl.BlockSpec((tm, M_V), lambda i: (i, 0)),
            pl.BlockSpec((tm, D_V), lambda i: (i, 0)),
            pl.BlockSpec((tm, 2 * D_MODEL), lambda i: (i, OFF_G // (2 * D_MODEL))),
            pl.BlockSpec((tm, D_MODEL), lambda i: (i, 0)),
            pl.BlockSpec((M_V, D_MODEL), full),
            pl.BlockSpec((D_V, D_MODEL), full),
            pl.BlockSpec((D_MODEL, D_MODEL), full),
            pl.BlockSpec((1, 2 * D_MODEL), full),
        ],
        out_specs=pl.BlockSpec((tm, D_MODEL), lambda i: (i, 0)),
        out_shape=jax.ShapeDtypeStruct((T, D_MODEL), F32),
        compiler_params=_cparams(("arbitrary",)),
        name="merge",
    )(hm, hd, z, x2d, w_bm, w_bd, w_out, b_gate)


def _memkv_kernel(mem_ref, g_ref, w_ref, out_ref):
    out_ref[...] = _dot(_rms(mem_ref[...], g_ref[...]).astype(BF16), w_ref[...]).astype(BF16)


def _memkv(mem2d, g, wkv, n_mem):
    R = mem2d.shape[0]
    return pl.pallas_call(
        _memkv_kernel,
        grid=(R // n_mem,),
        in_specs=[
            pl.BlockSpec((n_mem, D_MODEL), lambda i: (i, 0)),
            pl.BlockSpec((1, D_MODEL), lambda i: (0, 0)),
            pl.BlockSpec((D_MODEL, 2 * D_MODEL), lambda i: (0, 0)),
        ],
        out_specs=pl.BlockSpec((n_mem, 2 * D_MODEL), lambda i: (i, 0)),
        out_shape=jax.ShapeDtypeStruct((R, 2 * D_MODEL), BF16),
        compiler_params=_cparams(("arbitrary",)),
        name="memkv",
    )(mem2d, g, wkv)


def _xattn_kernel(x_ref, g_ref, wq_ref, kv_ref, wo_ref, out_ref, o_ref):
    x = x_ref[...]
    h = _rms(x, g_ref[...]).astype(BF16)
    q = (_dot(h, wq_ref[...]) * (X_DH ** -0.5)).astype(BF16)
    for hd in range(X_HEADS):
        qh = q[:, hd * X_DH:(hd + 1) * X_DH]
        kh = kv_ref[:, hd * X_DH:(hd + 1) * X_DH]
        vh = kv_ref[:, D_MODEL + hd * X_DH:D_MODEL + (hd + 1) * X_DH]
        s = _dot_nt(qh, kh)
        p = jnp.exp(s - jnp.max(s, axis=-1, keepdims=True))
        p = p / jnp.sum(p, axis=-1, keepdims=True)
        o_ref[:, hd * X_DH:(hd + 1) * X_DH] = _dot(p.astype(BF16), vh).astype(BF16)
    out_ref[...] = x + _dot(o_ref[...], wo_ref[...])


def _xattn(x1, g, wq, kvmem, wo, S, n_mem, tm):
    T = x1.shape[0]
    per_b = S // tm
    full = lambda i: (0, 0)
    return pl.pallas_call(
        _xattn_kernel,
        grid=(T // tm,),
        in_specs=[
            pl.BlockSpec((tm, D_MODEL), lambda i: (i, 0)),
            pl.BlockSpec((1, D_MODEL), full),
            pl.BlockSpec((D_MODEL, D_MODEL), full),
            pl.BlockSpec((n_mem, 2 * D_MODEL), lambda i: (i // per_b, 0)),
            pl.BlockSpec((D_MODEL, D_MODEL), full),
        ],
        out_specs=pl.BlockSpec((tm, D_MODEL), lambda i: (i, 0)),
        out_shape=jax.ShapeDtypeStruct((T, D_MODEL), F32),
        scratch_shapes=[pltpu.VMEM((tm, D_MODEL), BF16)],
        compiler_params=_cparams(("arbitrary",)),
        name="xattn",
    )(x1, g, wq, kvmem, wo)


def _router_kernel(x_ref, g_ref, wrh_ref, wrl_ref, br_ref, hp_ref, ids_ref, tw_ref, cnt_ref):
    @pl.when(pl.program_id(0) == 0)
    def _():
        cnt_ref[...] = jnp.zeros_like(cnt_ref)

    hn = _rms(x_ref[...], g_ref[...])
    hp_ref[...] = _pack_bf16_pairs(hn)
    hh, hl = _split_bf16(hn)
    logits = _dot(hh, wrh_ref[...]) + _dot(hh, wrl_ref[...]) + _dot(hl, wrh_ref[...]) + br_ref[...]
    lane = lax.broadcasted_iota(I32, logits.shape, 1)
    lanef = lane.astype(F32)
    ids = jnp.zeros(logits.shape, F32)
    tw = jnp.zeros(logits.shape, F32)
    onehot = jnp.zeros(logits.shape, F32)
    v0 = None
    den = None
    for kk in range(TOP_K):
        mx = jnp.max(logits, axis=-1, keepdims=True)
        idx = jnp.min(jnp.where(logits == mx, lanef, float(LANES)), axis=-1, keepdims=True)
        sel = lanef == idx
        if kk == 0:
            v0 = mx
        e = jnp.exp(mx - v0)
        den = e if den is None else den + e
        ids = jnp.where(lane == kk, idx, ids)
        tw = jnp.where(lane == kk, e, tw)
        onehot = jnp.where(sel, 1.0, onehot)
        logits = jnp.where(sel, -jnp.inf, logits)
    ids_ref[...] = ids.astype(I32)
    tw_ref[...] = tw / den
    cnt_ref[...] += jnp.sum(onehot, axis=0, keepdims=True)


def _router(x2, g, wr_hi, wr_lo, b_r, tm):
    T = x2.shape[0]
    full = lambda i: (0, 0)
    return pl.pallas_call(
        _router_kernel,
        grid=(T // tm,),
        in_specs=[
            pl.BlockSpec((tm, D_MODEL), lambda i: (i, 0)),
            pl.BlockSpec((1, D_MODEL), full),
            pl.BlockSpec((D_MODEL, LANES), full),
            pl.BlockSpec((D_MODEL, LANES), full),
            pl.BlockSpec((1, LANES), full),
        ],
        out_specs=[
            pl.BlockSpec((tm, D_MODEL // 2), lambda i: (i, 0)),
            pl.BlockSpec((tm, LANES), lambda i: (i, 0)),
            pl.BlockSpec((tm, LANES), lambda i: (i, 0)),
            pl.BlockSpec((1, LANES), full),
        ],
        out_shape=[
            jax.ShapeDtypeStruct((T, D_MODEL // 2), U32),
            jax.ShapeDtypeStruct((T, LANES), I32),
            jax.ShapeDtypeStruct((T, LANES), F32),
            jax.ShapeDtypeStruct((1, LANES), F32),
        ],
        compiler_params=_cparams(("arbitrary",)),
        name="router",
    )(x2, g, wr_hi, wr_lo, b_r)


def _slots_kernel(ids_ref, start_ref, pos_ref, run_ref):
    @pl.when(pl.program_id(0) == 0)
    def _():
        run_ref[...] = jnp.zeros_like(run_ref)

    ids = ids_ref[...]
    tm = ids.shape[0]
    lane = lax.broadcasted_iota(I32, ids.shape, 1)
    sels = [lane == ids[:, kk:kk + 1] for kk in range(TOP_K)]
    onehot = jnp.zeros(ids.shape, F32)
    for s in sels:
        onehot = jnp.where(s, 1.0, onehot)
    r = lax.broadcasted_iota(I32, (tm, tm), 0)
    c = lax.broadcasted_iota(I32, (tm, tm), 1)
    strict = jnp.where(c < r, 1.0, 0.0).astype(BF16)
    rank = _dot(strict, onehot.astype(BF16)) + run_ref[...] + start_ref[...]
    pos = jnp.zeros(ids.shape, F32)
    for kk, s in enumerate(sels):
        pk = jnp.sum(jnp.where(s, rank, 0.0), axis=-1, keepdims=True)
        pos = jnp.where(lane == kk, pk, pos)
    pos_ref[...] = pos.astype(I32)
    run_ref[...] += jnp.sum(onehot, axis=0, keepdims=True)


def _slots(ids, starts, tm):
    T = ids.shape[0]
    return pl.pallas_call(
        _slots_kernel,
        grid=(T // tm,),
        in_specs=[
            pl.BlockSpec((tm, LANES), lambda i: (i, 0)),
            pl.BlockSpec((1, LANES), lambda i: (0, 0)),
        ],
        out_specs=pl.BlockSpec((tm, LANES), lambda i: (i, 0)),
        out_shape=jax.ShapeDtypeStruct((T, LANES), I32),
        scratch_shapes=[pltpu.VMEM((1, LANES), F32)],
        compiler_params=_cparams(("arbitrary",)),
        name="slots",
    )(ids, starts)


def _dispatch_kernel(hp_ref, pos_hbm, xs_in, xs_ref, pos_smem, sem_pos, sem_rows, *, tm):
    del xs_in
    i = pl.program_id(0)
    cp = pltpu.make_async_copy(pos_hbm.at[i], pos_smem, sem_pos)
    cp.start()
    cp.wait()

    def row_copy(t, kk):
        p = pos_smem[t * TOP_K + kk]
        return pltpu.make_async_copy(hp_ref.at[pl.ds(t, 1), :], xs_ref.at[pl.ds(p, 1), :], sem_rows)

    def start(t, carry):
        for kk in range(TOP_K):
            row_copy(t, kk).start()
        return carry

    def wait(t, carry):
        for kk in range(TOP_K):
            row_copy(t, kk).wait()
        return carry

    lax.fori_loop(0, tm, start, 0)
    lax.fori_loop(0, tm, wait, 0)


def _dispatch(hp, pos2d, xs_zero, tm):
    T = hp.shape[0]
    return pl.pallas_call(
        functools.partial(_dispatch_kernel, tm=tm),
        grid=(T // tm,),
        in_specs=[
            pl.BlockSpec((tm, D_MODEL // 2), lambda i: (i, 0)),
            pl.BlockSpec(memory_space=pl.ANY),
            pl.BlockSpec(memory_space=pl.ANY),
        ],
        out_specs=pl.BlockSpec(memory_space=pl.ANY),
        out_shape=jax.ShapeDtypeStruct(xs_zero.shape, U32),
        scratch_shapes=[
            pltpu.SMEM((tm * TOP_K,), I32),
            pltpu.SemaphoreType.DMA,
            pltpu.SemaphoreType.DMA,
        ],
        input_output_aliases={2: 0},
        compiler_params=_cparams(("arbitrary",)),
        name="dispatch",
    )(hp, pos2d, xs_zero)


def _experts_kernel(te_ref, nu_ref, xs_ref, wgu_ref, bgu_ref, wdn_ref, bdn_ref, ys_ref):
    del te_ref
    i = pl.program_id(0)
    half = D_MODEL // 2

    @pl.when(i < nu_ref[0])
    def _():
        lo, hi = _unpack_bf16_pairs(xs_ref[...])
        gu = (_dot(lo.astype(BF16), wgu_ref[0, :half, :]) + _dot(hi.astype(BF16), wgu_ref[0, half:, :])
              + bgu_ref[0])
        gate = jnp.minimum(gu[:, :D_FF], SWIGLU_LIMIT)
        up = jnp.clip(gu[:, D_FF:], -SWIGLU_LIMIT, SWIGLU_LIMIT)
        act = (up + 1.0) * (gate * _sigmoid(SWIGLU_ALPHA * gate))
        y = _dot(act.astype(BF16), wdn_ref[0]) + bdn_ref[0]
        ys_ref[...] = _pack_bf16_pairs(y)

    @pl.when(i >= nu_ref[0])
    def _():
        ys_ref[...] = jnp.zeros_like(ys_ref)


def _experts(tile_expert, n_used, xs, w_gu, b_gu, w_dn, b_dn, tg):
    P = xs.shape[0]
    half = D_MODEL // 2
    grid_spec = pltpu.PrefetchScalarGridSpec(
        num_scalar_prefetch=2,
        grid=(P // tg,),
        in_specs=[
            pl.BlockSpec((tg, half), lambda i, te, nu: (i, 0)),
            pl.BlockSpec((1, D_MODEL, 2 * D_FF), lambda i, te, nu: (te[i], 0, 0)),
            pl.BlockSpec((1, 1, 2 * D_FF), lambda i, te, nu: (te[i], 0, 0)),
            pl.BlockSpec((1, D_FF, D_MODEL), lambda i, te, nu: (te[i], 0, 0)),
            pl.BlockSpec((1, 1, D_MODEL), lambda i, te, nu: (te[i], 0, 0)),
        ],
        out_specs=pl.BlockSpec((tg, half), lambda i, te, nu: (i, 0)),
    )
    return pl.pallas_call(
        _experts_kernel,
        grid_spec=grid_spec,
        out_shape=jax.ShapeDtypeStruct((P, half), U32),
        compiler_params=_cparams(("arbitrary",)),
        name="experts",
    )(tile_expert, n_used, xs, w_gu, b_gu, w_dn, b_dn)


def _combine_kernel(pos_hbm, ys_hbm, tw_ref, x_ref, g_ref, out_ref, buf_ref, pos_smem, sem_pos, sem_rows, *,
                    tm, final_norm):
    i = pl.program_id(0)
    cp = pltpu.make_async_copy(pos_hbm.at[i], pos_smem, sem_pos)
    cp.start()
    cp.wait()

    def row_copy(t, kk):
        p = pos_smem[t * TOP_K + kk]
        return pltpu.make_async_copy(ys_hbm.at[pl.ds(p, 1), :], buf_ref.at[kk, pl.ds(t, 1), :], sem_rows)

    def start(t, carry):
        for kk in range(TOP_K):
            row_copy(t, kk).start()
        return carry

    def wait(t, carry):
        for kk in range(TOP_K):
            row_copy(t, kk).wait()
        return carry

    lax.fori_loop(0, tm, start, 0)
    lax.fori_loop(0, tm, wait, 0)

    tw = tw_ref[...]
    acc_lo = None
    acc_hi = None
    for kk in range(TOP_K):
        lo, hi = _unpack_bf16_pairs(buf_ref[kk])
        wk = tw[:, kk:kk + 1]
        acc_lo = wk * lo if acc_lo is None else acc_lo + wk * lo
        acc_hi = wk * hi if acc_hi is None else acc_hi + wk * hi
    x3 = x_ref[...] + jnp.concatenate([acc_lo, acc_hi], axis=1)
    out_ref[...] = _rms(x3, g_ref[...]) if final_norm else x3


def _combine(pos2d, ys, tw, x2, g, tm, final_norm):
    T = x2.shape[0]
    half = D_MODEL // 2
    return pl.pallas_call(
        functools.partial(_combine_kernel, tm=tm, final_norm=final_norm),
        grid=(T // tm,),
        in_specs=[
            pl.BlockSpec(memory_space=pl.ANY),
            pl.BlockSpec(memory_space=pl.ANY),
            pl.BlockSpec((tm, LANES), lambda i: (i, 0)),
            pl.BlockSpec((tm, D_MODEL), lambda i: (i, 0)),
            pl.BlockSpec((1, D_MODEL), lambda i: (0, 0)),
        ],
        out_specs=pl.BlockSpec((tm, D_MODEL), lambda i: (i, 0)),
        out_shape=jax.ShapeDtypeStruct((T, D_MODEL), F32),
        scratch_shapes=[
            pltpu.VMEM((TOP_K, tm, half), U32),
            pltpu.SMEM((tm * TOP_K,), I32),
            pltpu.SemaphoreType.DMA,
            pltpu.SemaphoreType.DMA,
        ],
        compiler_params=_cparams(("arbitrary",)),
        name="combine",
    )(pos2d, ys, tw, x2, g)


def _tiles(B, S):
    T = B * S
    return dict(
        tm_in=min(1024, T), tn_in=1024,
        ts=min(512, S),
        tq=min(512, S),
        tm_proj=min(512, S),
        tm_route=min(256, T),
        tg=512,
    )


def _pad_lanes(a, n=LANES, value=0.0):
    return jnp.pad(a, ((0, 0), (0, n - a.shape[1])), constant_values=value)


def kernel(x, mem, norm_mix, w_in, conv_w, b_if, mlstm_gain, diff_lambda, diff_gain, w_branch_m, w_branch_d,
           b_gate, w_out, norm_xattn, norm_mem, wq_x, wkv_x, wo_x, norm_ffn, w_router, b_router, w_gu, b_gu,
           w_dn, b_dn, norm_final):
    B, S, D = x.shape
    n_mem = mem.shape[1]
    T = B * S
    depth = norm_mix.shape[0]
    tl = _tiles(B, S)
    x2d = x.reshape(T, D)
    mem2d = mem.reshape(B * n_mem, D)

    for l in range(depth):
        lam_init = 0.8 - 0.6 * math.exp(-0.3 * l)
        wl = w_in[l]
        if_lo = 2 * M_QK + 2 * M_V
        w_main = jnp.concatenate([wl[:, :if_lo], wl[:, if_lo + 2 * M_HEADS:]], axis=1).astype(BF16)
        w_if = wl[:, if_lo:if_lo + 2 * M_HEADS]
        w_ifp = _pad_lanes(w_if).astype(BF16)
        w_ift = w_if.T.astype(BF16)
        bif = _pad_lanes(b_if[l][None, :])
        bift = jnp.broadcast_to(b_if[l][:, None], (SUBLANES, LANES))

        z, zif, zift = _inproj(x2d, norm_mix[l][None, :], w_main, w_ifp, w_ift, tl["tm_in"], tl["tn_in"])
        zift3 = zift.reshape(SUBLANES, T // CHUNK, CHUNK).transpose(1, 0, 2)

        hm = _mlstm(z, zif, zift3, conv_w[l], bif, bift, mlstm_gain[l].reshape(1, M_V), B, S, tl["ts"])
        hd = _diffattn(z, diff_lambda[l], diff_gain[l][None, :], B, S, tl["tq"], lam_init)
        x1 = _merge(hm, hd, z, x2d, w_branch_m[l].astype(BF16), w_branch_d[l].astype(BF16),
                    w_out[l].astype(BF16), b_gate[l][None, :], tl["tm_proj"])

        kvmem = _memkv(mem2d, norm_mem[l][None, :], wkv_x[l].astype(BF16), n_mem)
        x2 = _xattn(x1, norm_xattn[l][None, :], wq_x[l].astype(BF16), kvmem, wo_x[l].astype(BF16),
                    S, n_mem, tl["tm_proj"])

        wr = _pad_lanes(w_router[l])
        wr_hi = wr.astype(BF16)
        wr_lo = (wr - wr_hi.astype(F32)).astype(BF16)
        br = _pad_lanes(b_router[l][None, :], value=-jnp.inf)
        tm_r = tl["tm_route"]
        tg = tl["tg"]
        hp, ids, tw, cnt = _router(x2, norm_ffn[l][None, :], wr_hi, wr_lo, br, tm_r)

        counts = cnt[0, :N_EXPERTS].astype(I32)
        padded = ((counts + tg - 1) // tg) * tg
        ends = jnp.cumsum(padded)
        starts = ends - padded
        n_tiles = (T * TOP_K) // tg + N_EXPERTS
        tile_row0 = jnp.arange(n_tiles, dtype=I32) * tg
        tile_expert = jnp.minimum(jnp.searchsorted(ends, tile_row0, side="right"), N_EXPERTS - 1).astype(I32)
        n_used = (ends[-1] // tg).astype(I32).reshape(1)
        last_used = tile_expert[jnp.maximum(n_used[0] - 1, 0)]
        tile_expert = jnp.where(tile_row0 < ends[-1], tile_expert, last_used)

        pos = _slots(ids, _pad_lanes(starts.astype(F32)[None, :]), tm_r)
        pos2d = pos[:, :TOP_K].reshape(T // tm_r, tm_r * TOP_K)

        xs = _dispatch(hp, pos2d, jnp.zeros((n_tiles * tg, D // 2), U32), tm_r)
        ys = _experts(tile_expert, n_used, xs, w_gu[l].astype(BF16), b_gu[l][:, None, :],
                      w_dn[l].astype(BF16), b_dn[l][:, None, :], tg)
        x2d = _combine(pos2d, ys, tw, x2, norm_final[None, :], tm_r, final_norm=(l == depth - 1))
    return x2d.reshape(B, S, D)
```

```python
import functools
import math

import jax
import jax.numpy as jnp
from jax import lax
from jax.experimental import pallas as pl
from jax.experimental.pallas import tpu as pltpu

F32 = jnp.float32
BF16 = jnp.bfloat16
U32 = jnp.uint32
I32 = jnp.int32

EPS = 1e-6
CHUNK = 64
D_MODEL = 1024
M_HEADS = 4
M_DK = 128
M_DV = 256
M_QK = M_HEADS * M_DK
M_V = M_HEADS * M_DV
CONV_W = 4
D_HEADS = 8
D_DH = 64
D_QK = D_HEADS * 2 * D_DH
D_V = D_HEADS * 2 * D_DH
X_HEADS = 4
X_DH = D_MODEL // X_HEADS
N_EXPERTS = 32
TOP_K = 4
D_FF = D_MODEL
SWIGLU_LIMIT = 7.0
SWIGLU_ALPHA = 1.702

LANES = 128
SUBLANES = 8
N_MAIN = 2 * M_QK + 2 * M_V + 2 * D_QK + D_V + 2 * D_MODEL
OFF_QM, OFF_KM, OFF_VM, OFF_OM = 0, M_QK, 2 * M_QK, 2 * M_QK + M_V
OFF_QD = OFF_OM + M_V
OFF_KD = OFF_QD + D_QK
OFF_VD = OFF_KD + D_QK
OFF_G = OFF_VD + D_V

VMEM_LIMIT = 56 * 1024 * 1024


def _cparams(sem, vmem=VMEM_LIMIT):
    return pltpu.CompilerParams(dimension_semantics=sem, vmem_limit_bytes=vmem)


def _rms(x, g):
    return x * lax.rsqrt(jnp.mean(x * x, axis=-1, keepdims=True) + EPS) * g


def _split_bf16(x):
    hi = x.astype(BF16)
    lo = (x - hi.astype(F32)).astype(BF16)
    return hi, lo


def _dot(a, b):
    return jnp.dot(a, b, preferred_element_type=F32)


def _dot_nt(a, b):
    return lax.dot_general(a, b, (((1,), (1,)), ((), ())), preferred_element_type=F32)


def _sigmoid(x):
    return 1.0 / (1.0 + jnp.exp(-x))


def _log_sigmoid(x):
    return jnp.minimum(x, 0.0) - jnp.log(1.0 + jnp.exp(-jnp.abs(x)))


def _pack_bf16_pairs(x):
    w = x.shape[1] // 2
    u = lax.bitcast_convert_type(x, U32)
    r = (u + jnp.uint32(0x7FFF) + ((u >> 16) & jnp.uint32(1))) >> 16
    return r[:, :w] | (r[:, w:] << 16)


def _unpack_bf16_pairs(p):
    lo = lax.bitcast_convert_type(p << 16, F32)
    hi = lax.bitcast_convert_type(p & jnp.uint32(0xFFFF0000), F32)
    return lo, hi


def _inproj_kernel(x_ref, g_ref, w_ref, wif_ref, wift_ref, z_ref, zif_ref, zift_ref, hn_ref):
    @pl.when(pl.program_id(1) == 0)
    def _():
        hn = _rms(x_ref[...], g_ref[...]).astype(BF16)
        hn_ref[...] = hn
        zif_ref[...] = _dot(hn, wif_ref[...])
        zift_ref[...] = _dot_nt(wift_ref[...], hn)

    z_ref[...] = _dot(hn_ref[...], w_ref[...]).astype(BF16)


def _inproj(x2d, g, w_main, w_if, w_ift, tm, tn):
    T = x2d.shape[0]
    return pl.pallas_call(
        _inproj_kernel,
        grid=(T // tm, N_MAIN // tn),
        in_specs=[
            pl.BlockSpec((tm, D_MODEL), lambda i, j: (i, 0)),
            pl.BlockSpec((1, D_MODEL), lambda i, j: (0, 0)),
            pl.BlockSpec((D_MODEL, tn), lambda i, j: (0, j)),
            pl.BlockSpec((D_MODEL, LANES), lambda i, j: (0, 0)),
            pl.BlockSpec((SUBLANES, D_MODEL), lambda i, j: (0, 0)),
        ],
        out_specs=[
            pl.BlockSpec((tm, tn), lambda i, j: (i, j)),
            pl.BlockSpec((tm, LANES), lambda i, j: (i, 0)),
            pl.BlockSpec((SUBLANES, tm), lambda i, j: (0, i)),
        ],
        out_shape=[
            jax.ShapeDtypeStruct((T, N_MAIN), BF16),
            jax.ShapeDtypeStruct((T, LANES), F32),
            jax.ShapeDtypeStruct((SUBLANES, T), F32),
        ],
        scratch_shapes=[pltpu.VMEM((tm, D_MODEL), BF16)],
        compiler_params=_cparams(("arbitrary", "arbitrary")),
        name="inproj",
    )(x2d, g, w_main, w_if, w_ift)


def _mlstm_kernel(q_ref, k_ref, v_ref, om_ref, zif_ref, zift_ref, cw_ref, bif_ref, bift_ref, mg_ref,
                  out_ref, qc_ref, kc_ref, carry_ref, c_ref, n_ref, m_ref, *, ts):
    nchunk = ts // CHUNK
    L = CHUNK

    @pl.when(pl.program_id(1) == 0)
    def _():
        carry_ref[...] = jnp.zeros_like(carry_ref)
        c_ref[...] = jnp.zeros_like(c_ref)
        n_ref[...] = jnp.zeros_like(n_ref)
        m_ref[...] = jnp.zeros_like(m_ref)

    row8 = lax.broadcasted_iota(I32, (SUBLANES, M_QK), 0)

    def conv_silu(x, prev8, w):
        acc = w[CONV_W - 1:CONV_W, :] * x
        for s in range(1, CONV_W):
            xs = pltpu.roll(x, s, 0)
            top = jnp.where(row8 < s, pltpu.roll(prev8, s, 0), xs[0:SUBLANES])
            xs = jnp.concatenate([top, xs[SUBLANES:]], axis=0)
            acc = acc + w[CONV_W - 1 - s:CONV_W - s, :] * xs
        return acc * _sigmoid(acc)

    def conv_body(c, carry):
        r0 = pl.multiple_of(c * L, L)
        xq = q_ref[pl.ds(r0, L), :].astype(F32)
        xk = k_ref[pl.ds(r0, L), :].astype(F32)
        yq = conv_silu(xq, carry_ref[:, 0:M_QK], cw_ref[:, 0:M_QK]) * (M_DK ** -0.5)
        yk = conv_silu(xk, carry_ref[:, M_QK:2 * M_QK], cw_ref[:, M_QK:2 * M_QK])
        qc_ref[pl.ds(r0, L), :] = yq.astype(BF16)
        kc_ref[pl.ds(r0, L), :] = yk.astype(BF16)
        carry_ref[:, 0:M_QK] = xq[L - SUBLANES:L]
        carry_ref[:, M_QK:2 * M_QK] = xk[L - SUBLANES:L]
        return carry

    lax.fori_loop(0, nchunk, conv_body, 0)

    ti = lax.broadcasted_iota(I32, (L, L), 0)
    si = lax.broadcasted_iota(I32, (L, L), 1)
    causal = si <= ti
    tril = jnp.where(causal, 1.0, 0.0).astype(BF16)
    triu = jnp.where(ti <= si, 1.0, 0.0).astype(BF16)

    def chunk_body(c, carry):
        r0 = pl.multiple_of(c * L, L)
        g_col = zif_ref[pl.ds(r0, L), :] + bif_ref[...]
        g_row = zift_ref[c] + bift_ref[:, 0:1]
        lf_col = _log_sigmoid(g_col)
        lf_row = _log_sigmoid(g_row)
        ch, cl = _split_bf16(lf_col)
        b_col_all = _dot(tril, ch) + _dot(tril, cl)
        rh, rl = _split_bf16(lf_row)
        b_row_all = _dot(rh, triu) + _dot(rl, triu)
        for h in range(M_HEADS):
            i_col = g_col[:, h:h + 1]
            b_col = b_col_all[:, M_HEADS + h:M_HEADS + h + 1]
            i_row = g_row[h:h + 1, :]
            b_row = b_row_all[M_HEADS + h:M_HEADS + h + 1, :]
            b_last = b_col[L - 1:L, :]
            m_prev = m_ref[h:h + 1, 0:1]
            dm = jnp.where(causal, b_col - b_row + i_row, -jnp.inf)
            inter = b_col + m_prev
            m_t = jnp.maximum(inter, jnp.max(dm, axis=-1, keepdims=True))
            w_inter = jnp.exp(inter - m_t)
            q = qc_ref[pl.ds(r0, L), h * M_DK:(h + 1) * M_DK]
            k = kc_ref[pl.ds(r0, L), h * M_DK:(h + 1) * M_DK]
            v = v_ref[pl.ds(r0, L), h * M_DV:(h + 1) * M_DV]
            s = _dot_nt(q, k) * jnp.exp(dm - m_t)
            c_old = c_ref[h]
            n_old = n_ref[h:h + 1, :]
            num = _dot(s.astype(BF16), v) + w_inter * _dot(q, c_old.astype(BF16))
            qn = jnp.sum(q.astype(F32) * n_old, axis=-1, keepdims=True)
            den = jnp.sum(s, axis=-1, keepdims=True) + w_inter * qn
            hv = num / jnp.maximum(jnp.abs(den), jnp.exp(-m_t))
            gk_col = b_last - b_col + i_col
            gk_row = b_last - b_row + i_row
            m_new = jnp.maximum(b_last + m_prev, jnp.max(gk_row, axis=-1, keepdims=True))
            wk = jnp.exp(gk_col - m_new)
            decay = jnp.exp(b_last + m_prev - m_new)
            kw = k.astype(F32) * wk
            c_ref[h] = decay * c_old + _dot(kw.T.astype(BF16), v)
            n_ref[h:h + 1, :] = decay * n_old + jnp.sum(kw, axis=0, keepdims=True)
            m_ref[h:h + 1, :] = jnp.broadcast_to(m_new, (1, LANES))
            hn = _rms(hv, mg_ref[:, h * M_DV:(h + 1) * M_DV])
            og = _sigmoid(om_ref[pl.ds(r0, L), h * M_DV:(h + 1) * M_DV].astype(F32))
            out_ref[pl.ds(r0, L), h * M_DV:(h + 1) * M_DV] = (og * hn).astype(BF16)
        return carry

    lax.fori_loop(0, nchunk, chunk_body, 0)


def _mlstm(z, zif, zift3, conv_w, bif, bift, m_gain, B, S, ts):
    T = B * S
    nt = S // ts
    nck = ts // CHUNK
    row = lambda b, t: b * nt + t
    return pl.pallas_call(
        functools.partial(_mlstm_kernel, ts=ts),
        grid=(B, nt),
        in_specs=[
            pl.BlockSpec((ts, M_QK), lambda b, t: (row(b, t), OFF_QM // M_QK)),
            pl.BlockSpec((ts, M_QK), lambda b, t: (row(b, t), OFF_KM // M_QK)),
            pl.BlockSpec((ts, M_V), lambda b, t: (row(b, t), OFF_VM // M_V)),
            pl.BlockSpec((ts, M_V), lambda b, t: (row(b, t), OFF_OM // M_V)),
            pl.BlockSpec((ts, LANES), lambda b, t: (row(b, t), 0)),
            pl.BlockSpec((nck, SUBLANES, CHUNK), lambda b, t: (row(b, t), 0, 0)),
            pl.BlockSpec((CONV_W, 2 * M_QK), lambda b, t: (0, 0)),
            pl.BlockSpec((1, LANES), lambda b, t: (0, 0)),
            pl.BlockSpec((SUBLANES, LANES), lambda b, t: (0, 0)),
            pl.BlockSpec((1, M_V), lambda b, t: (0, 0)),
        ],
        out_specs=pl.BlockSpec((ts, M_V), lambda b, t: (row(b, t), 0)),
        out_shape=jax.ShapeDtypeStruct((T, M_V), BF16),
        scratch_shapes=[
            pltpu.VMEM((ts, M_QK), BF16),
            pltpu.VMEM((ts, M_QK), BF16),
            pltpu.VMEM((SUBLANES, 2 * M_QK), F32),
            pltpu.VMEM((M_HEADS, M_DK, M_DV), F32),
            pltpu.VMEM((SUBLANES, M_DK), F32),
            pltpu.VMEM((SUBLANES, LANES), F32),
        ],
        compiler_params=_cparams(("arbitrary", "arbitrary")),
        name="mlstm",
    )(z, z, z, z, zif, zift3, conv_w, bif, bift, m_gain)


def _diffattn_kernel(q_ref, k_ref, v_ref, lam_ref, gain_ref, out_ref,
                     m1_ref, a1_ref, m2_ref, a2_ref, *, tq, lam_init):
    qi = pl.program_id(2)
    w = 2 * D_DH
    q = q_ref[...]
    lane = lax.broadcasted_iota(I32, (1, w), 1)
    scale = jnp.asarray(D_DH ** -0.5, BF16)
    q1 = jnp.where(lane < D_DH, q, jnp.zeros_like(q)) * scale
    q2 = jnp.where(lane >= D_DH, q, jnp.zeros_like(q)) * scale
    ones = jnp.ones((tq, w), BF16)

    m1_ref[...] = jnp.full_like(m1_ref, -jnp.inf)
    m2_ref[...] = jnp.full_like(m2_ref, -jnp.inf)
    a1_ref[...] = jnp.zeros_like(a1_ref)
    a2_ref[...] = jnp.zeros_like(a2_ref)

    def online(s, vext, m_ref, a_ref):
        m_old = m_ref[...]
        m_new = jnp.maximum(m_old, jnp.max(s, axis=-1, keepdims=True))
        p = jnp.exp(s - jnp.concatenate([m_new] * (tq // w), axis=1))
        alpha = jnp.exp(m_old - m_new)
        a_ref[...] = jnp.concatenate([alpha, alpha], axis=1) * a_ref[...] + _dot(p.astype(BF16), vext)
        m_ref[...] = m_new

    def block(j, mask):
        r0 = pl.multiple_of(j * tq, tq)
        k = k_ref[pl.ds(r0, tq), :]
        vext = jnp.concatenate([v_ref[pl.ds(r0, tq), :], ones], axis=1)
        s1 = _dot_nt(q1, k)
        s2 = _dot_nt(q2, k)
        if mask is not None:
            s1 = jnp.where(mask, s1, -jnp.inf)
            s2 = jnp.where(mask, s2, -jnp.inf)
        online(s1, vext, m1_ref, a1_ref)
        online(s2, vext, m2_ref, a2_ref)

    def body(j, carry):
        block(j, None)
        return carry

    lax.fori_loop(0, qi, body, 0)
    rq = lax.broadcasted_iota(I32, (tq, tq), 0) // CHUNK
    ck = lax.broadcasted_iota(I32, (tq, tq), 1) // CHUNK
    block(qi, ck <= rq)

    lp = lam_ref[...]
    lam = (jnp.exp(jnp.sum(lp[0:1, :] * lp[1:2, :], axis=-1, keepdims=True))
           - jnp.exp(jnp.sum(lp[2:3, :] * lp[3:4, :], axis=-1, keepdims=True)) + lam_init)
    a1 = a1_ref[...]
    a2 = a2_ref[...]
    o = a1[:, :w] / a1[:, w:] - lam * (a2[:, :w] / a2[:, w:])
    out_ref[...] = (_rms(o, gain_ref[...]) * (1.0 - lam_init)).astype(BF16)


def _diffattn(z, lam_p, d_gain, B, S, tq, lam_init):
    T = B * S
    nq = S // tq
    w = 2 * D_DH
    return pl.pallas_call(
        functools.partial(_diffattn_kernel, tq=tq, lam_init=lam_init),
        grid=(B, D_HEADS, nq),
        in_specs=[
            pl.BlockSpec((tq, w), lambda b, h, i: (b * nq + i, OFF_QD // w + h)),
            pl.BlockSpec((S, w), lambda b, h, i: (b, OFF_KD // w + h)),
            pl.BlockSpec((S, w), lambda b, h, i: (b, OFF_VD // w + h)),
            pl.BlockSpec((4, D_DH), lambda b, h, i: (0, 0)),
            pl.BlockSpec((1, w), lambda b, h, i: (0, 0)),
        ],
        out_specs=pl.BlockSpec((tq, w), lambda b, h, i: (b * nq + i, h)),
        out_shape=jax.ShapeDtypeStruct((T, D_V), BF16),
        scratch_shapes=[
            pltpu.VMEM((tq, w), F32), pltpu.VMEM((tq, 2 * w), F32),
            pltpu.VMEM((tq, w), F32), pltpu.VMEM((tq, 2 * w), F32),
        ],
        compiler_params=_cparams(("arbitrary", "arbitrary", "arbitrary")),
        name="diffattn",
    )(z, z, z, lam_p, d_gain)


def _merge_kernel(hm_ref, hd_ref, gz_ref, x_ref, wbm_ref, wbd_ref, wout_ref, bg_ref, out_ref):
    bm = _dot(hm_ref[...], wbm_ref[...])
    bd = _dot(hd_ref[...], wbd_ref[...])
    g = _sigmoid(gz_ref[...].astype(F32) + bg_ref[...])
    merged = g[:, :D_MODEL] * bm + g[:, D_MODEL:] * bd
    out_ref[...] = x_ref[...] + _dot(merged.astype(BF16), wout_ref[...])


def _merge(hm, hd, z, x2d, w_bm, w_bd, w_out, b_gate, tm):
    T = x2d.shape[0]
    full = lambda i: (0, 0)
    return pl.pallas_call(
        _merge_kernel,
        grid=(T // tm,),
        in_specs=[
            pl.BlockSpec((tm, M_V), lambda i: (i, 0)),
            pl.BlockSpec((tm, D_V), lambda i: (i, 0)),
            pl.BlockSpec((tm, 2 * D_MODEL), lambda i: (i, OFF_G // (2 * D_MODEL))),
            pl.BlockSpec((tm, D_MODEL), lambda i: (i, 0)),
            pl.BlockSpec((M_V, D_MODEL), full),
            pl.BlockSpec((D_V, D_MODEL), full),
            pl.BlockSpec((D_MODEL, D_MODEL), full),
            pl.BlockSpec((1, 2 * D_MODEL), full),
        ],
        out_specs=pl.BlockSpec((tm, D_MODEL), lambda i: (i, 0)),
        out_shape=jax.ShapeDtypeStruct((T, D_MODEL), F32),
        compiler_params=_cparams(("arbitrary",)),
        name="merge",
    )(hm, hd, z, x2d, w_bm, w_bd, w_out, b_gate)


def _memkv_kernel(mem_ref, g_ref, w_ref, out_ref):
    out_ref[...] = _dot(_rms(mem_ref[...], g_ref[...]).astype(BF16), w_ref[...]).astype(BF16)


def _memkv(mem2d, g, wkv, n_mem):
    R = mem2d.shape[0]
    return pl.pallas_call(
        _memkv_kernel,
        grid=(R // n_mem,),
        in_specs=[
            pl.BlockSpec((n_mem, D_MODEL), lambda i: (i, 0)),
            pl.BlockSpec((1, D_MODEL), lambda i: (0, 0)),
            pl.BlockSpec((D_MODEL, 2 * D_MODEL), lambda i: (0, 0)),
        ],
        out_specs=pl.BlockSpec((n_mem, 2 * D_MODEL), lambda i: (i, 0)),
        out_shape=jax.ShapeDtypeStruct((R, 2 * D_MODEL), BF16),
        compiler_params=_cparams(("arbitrary",)),
        name="memkv",
    )(mem2d, g, wkv)


def _xattn_kernel(x_ref, g_ref, wq_ref, kv_ref, wo_ref, out_ref, o_ref):
    x = x_ref[...]
    h = _rms(x, g_ref[...]).astype(BF16)
    q = (_dot(h, wq_ref[...]) * (X_DH ** -0.5)).astype(BF16)
    for hd in range(X_HEADS):
        qh = q[:, hd * X_DH:(hd + 1) * X_DH]
        kh = kv_ref[:, hd * X_DH:(hd + 1) * X_DH]
        vh = kv_ref[:, D_MODEL + hd * X_DH:D_MODEL + (hd + 1) * X_DH]
        s = _dot_nt(qh, kh)
        p = jnp.exp(s - jnp.max(s, axis=-1, keepdims=True))
        p = p / jnp.sum(p, axis=-1, keepdims=True)
        o_ref[:, hd * X_DH:(hd + 1) * X_DH] = _dot(p.astype(BF16), vh).astype(BF16)
    out_ref[...] = x + _dot(o_ref[...], wo_ref[...])


def _xattn(x1, g, wq, kvmem, wo, S, n_mem, tm):
    T = x1.shape[0]
    per_b = S // tm
    full = lambda i: (0, 0)
    return pl.pallas_call(
        _xattn_kernel,
        grid=(T // tm,),
        in_specs=[
            pl.BlockSpec((tm, D_MODEL), lambda i: (i, 0)),
            pl.BlockSpec((1, D_MODEL), full),
            pl.BlockSpec((D_MODEL, D_MODEL), full),
            pl.BlockSpec((n_mem, 2 * D_MODEL), lambda i: (i // per_b, 0)),
            pl.BlockSpec((D_MODEL, D_MODEL), full),
        ],
        out_specs=pl.BlockSpec((tm, D_MODEL), lambda i: (i, 0)),
        out_shape=jax.ShapeDtypeStruct((T, D_MODEL), F32),
        scratch_shapes=[pltpu.VMEM((tm, D_MODEL), BF16)],
        compiler_params=_cparams(("arbitrary",)),
        name="xattn",
    )(x1, g, wq, kvmem, wo)


def _router_kernel(x_ref, g_ref, wrh_ref, wrl_ref, br_ref, hp_ref, ids_ref, tw_ref, cnt_ref):
    @pl.when(pl.program_id(0) == 0)
    def _():
        cnt_ref[...] = jnp.zeros_like(cnt_ref)

    hn = _rms(x_ref[...], g_ref[...])
    hp_ref[...] = _pack_bf16_pairs(hn)
    hh, hl = _split_bf16(hn)
    logits = _dot(hh, wrh_ref[...]) + _dot(hh, wrl_ref[...]) + _dot(hl, wrh_ref[...]) + br_ref[...]
    lane = lax.broadcasted_iota(I32, logits.shape, 1)
    lanef = lane.astype(F32)
    ids = jnp.zeros(logits.shape, F32)
    tw = jnp.zeros(logits.shape, F32)
    onehot = jnp.zeros(logits.shape, F32)
    v0 = None
    den = None
    for kk in range(TOP_K):
        mx = jnp.max(logits, axis=-1, keepdims=True)
        idx = jnp.min(jnp.where(logits == mx, lanef, float(LANES)), axis=-1, keepdims=True)
        sel = lanef == idx
        if kk == 0:
            v0 = mx
        e = jnp.exp(mx - v0)
        den = e if den is None else den + e
        ids = jnp.where(lane == kk, idx, ids)
        tw = jnp.where(lane == kk, e, tw)
        onehot = jnp.where(sel, 1.0, onehot)
        logits = jnp.where(sel, -jnp.inf, logits)
    ids_ref[...] = ids.astype(I32)
    tw_ref[...] = tw / den
    cnt_ref[...] += jnp.sum(onehot, axis=0, keepdims=True)


def _router(x2, g, wr_hi, wr_lo, b_r, tm):
    T = x2.shape[0]
    full = lambda i: (0, 0)
    return pl.pallas_call(
        _router_kernel,
        grid=(T // tm,),
        in_specs=[
            pl.BlockSpec((tm, D_MODEL), lambda i: (i, 0)),
            pl.BlockSpec((1, D_MODEL), full),
            pl.BlockSpec((D_MODEL, LANES), full),
            pl.BlockSpec((D_MODEL, LANES), full),
            pl.BlockSpec((1, LANES), full),
        ],
        out_specs=[
            pl.BlockSpec((tm, D_MODEL // 2), lambda i: (i, 0)),
            pl.BlockSpec((tm, LANES), lambda i: (i, 0)),
            pl.BlockSpec((tm, LANES), lambda i: (i, 0)),
            pl.BlockSpec((1, LANES), full),
        ],
        out_shape=[
            jax.ShapeDtypeStruct((T, D_MODEL // 2), U32),
            jax.ShapeDtypeStruct((T, LANES), I32),
            jax.ShapeDtypeStruct((T, LANES), F32),
            jax.ShapeDtypeStruct((1, LANES), F32),
        ],
        compiler_params=_cparams(("arbitrary",)),
        name="router",
    )(x2, g, wr_hi, wr_lo, b_r)


def _slots_kernel(ids_ref, start_ref, pos_ref, run_ref):
    @pl.when(pl.program_id(0) == 0)
    def _():
        run_ref[...] = jnp.zeros_like(run_ref)

    ids = ids_ref[...]
    tm = ids.shape[0]
    lane = lax.broadcasted_iota(I32, ids.shape, 1)
    sels = [lane == ids[:, kk:kk + 1] for kk in range(TOP_K)]
    onehot = jnp.zeros(ids.shape, F32)
    for s in sels:
        onehot = jnp.where(s, 1.0, onehot)
    r = lax.broadcasted_iota(I32, (tm, tm), 0)
    c = lax.broadcasted_iota(I32, (tm, tm), 1)
    strict = jnp.where(c < r, 1.0, 0.0).astype(BF16)
    rank = _dot(strict, onehot.astype(BF16)) + run_ref[...] + start_ref[...]
    pos = jnp.zeros(ids.shape, F32)
    for kk, s in enumerate(sels):
        pk = jnp.sum(jnp.where(s, rank, 0.0), axis=-1, keepdims=True)
        pos = jnp.where(lane == kk, pk, pos)
    pos_ref[...] = pos.astype(I32)
    run_ref[...] += jnp.sum(onehot, axis=0, keepdims=True)


def _slots(ids, starts, tm):
    T = ids.shape[0]
    return pl.pallas_call(
        _slots_kernel,
        grid=(T // tm,),
        in_specs=[
            pl.BlockSpec((tm, LANES), lambda i: (i, 0)),
            pl.BlockSpec((1, LANES), lambda i: (0, 0)),
        ],
        out_specs=pl.BlockSpec((tm, LANES), lambda i: (i, 0)),
        out_shape=jax.ShapeDtypeStruct((T, LANES), I32),
        scratch_shapes=[pltpu.VMEM((1, LANES), F32)],
        compiler_params=_cparams(("arbitrary",)),
        name="slots",
    )(ids, starts)


def _dispatch_kernel(hp_ref, pos_hbm, xs_in, xs_ref, pos_smem, sem_pos, sem_rows, *, tm):
    del xs_in
    i = pl.program_id(0)
    cp = pltpu.make_async_copy(pos_hbm.at[i], pos_smem, sem_pos)
    cp.start()
    cp.wait()

    def row_copy(t, kk):
        p = pos_smem[t * TOP_K + kk]
        return pltpu.make_async_copy(hp_ref.at[pl.ds(t, 1), :], xs_ref.at[pl.ds(p, 1), :], sem_rows)

    def start(t, carry):
        for kk in range(TOP_K):
            row_copy(t, kk).start()
        return carry

    def wait(t, carry):
        for kk in range(TOP_K):
            row_copy(t, kk).wait()
        return carry

    lax.fori_loop(0, tm, start, 0)
    lax.fori_loop(0, tm, wait, 0)


def _dispatch(hp, pos2d, xs_zero, tm):
    T = hp.shape[0]
    return pl.pallas_call(
        functools.partial(_dispatch_kernel, tm=tm),
        grid=(T // tm,),
        in_specs=[
            pl.BlockSpec((tm, D_MODEL // 2), lambda i: (i, 0)),
            pl.BlockSpec(memory_space=pl.ANY),
            pl.BlockSpec(memory_space=pl.ANY),
        ],
        out_specs=pl.BlockSpec(memory_space=pl.ANY),
        out_shape=jax.ShapeDtypeStruct(xs_zero.shape, U32),
        scratch_shapes=[
            pltpu.SMEM((tm * TOP_K,), I32),
            pltpu.SemaphoreType.DMA,
            pltpu.SemaphoreType.DMA,
        ],
        input_output_aliases={2: 0},
        compiler_params=_cparams(("arbitrary",)),
        name="dispatch",
    )(hp, pos2d, xs_zero)


def _experts_kernel(te_ref, nu_ref, xs_ref, wgu_ref, bgu_ref, wdn_ref, bdn_ref, ys_ref):
    del te_ref
    i = pl.program_id(0)
    half = D_MODEL // 2

    @pl.when(i < nu_ref[0])
    def _():
        lo, hi = _unpack_bf16_pairs(xs_ref[...])
        gu = (_dot(lo.astype(BF16), wgu_ref[0, :half, :]) + _dot(hi.astype(BF16), wgu_ref[0, half:, :])
              + bgu_ref[0])
        gate = jnp.minimum(gu[:, :D_FF], SWIGLU_LIMIT)
        up = jnp.clip(gu[:, D_FF:], -SWIGLU_LIMIT, SWIGLU_LIMIT)
        act = (up + 1.0) * (gate * _sigmoid(SWIGLU_ALPHA * gate))
        y = _dot(act.astype(BF16), wdn_ref[0]) + bdn_ref[0]
        ys_ref[...] = _pack_bf16_pairs(y)

    @pl.when(i >= nu_ref[0])
    def _():
        ys_ref[...] = jnp.zeros_like(ys_ref)


def _experts(tile_expert, n_used, xs, w_gu, b_gu, w_dn, b_dn, tg):
    P = xs.shape[0]
    half = D_MODEL // 2
    grid_spec = pltpu.PrefetchScalarGridSpec(
        num_scalar_prefetch=2,
        grid=(P // tg,),
        in_specs=[
            pl.BlockSpec((tg, half), lambda i, te, nu: (i, 0)),
            pl.BlockSpec((1, D_MODEL, 2 * D_FF), lambda i, te, nu: (te[i], 0, 0)),
            pl.BlockSpec((1, 1, 2 * D_FF), lambda i, te, nu: (te[i], 0, 0)),
            pl.BlockSpec((1, D_FF, D_MODEL), lambda i, te, nu: (te[i], 0, 0)),
            pl.BlockSpec((1, 1, D_MODEL), lambda i, te, nu: (te[i], 0, 0)),
        ],
        out_specs=pl.BlockSpec((tg, half), lambda i, te, nu: (i, 0)),
    )
    return pl.pallas_call(
        _experts_kernel,
        grid_spec=grid_spec,
        out_shape=jax.ShapeDtypeStruct((P, half), U32),
        compiler_params=_cparams(("arbitrary",)),
        name="experts",
    )(tile_expert, n_used, xs, w_gu, b_gu, w_dn, b_dn)


def _combine_kernel(pos_hbm, ys_hbm, tw_ref, x_ref, g_ref, out_ref, buf_ref, pos_smem, sem_pos, sem_rows, *,
                    tm, final_norm):
    i = pl.program_id(0)
    cp = pltpu.make_async_copy(pos_hbm.at[i], pos_smem, sem_pos)
    cp.start()
    cp.wait()

    def row_copy(t, kk):
        p = pos_smem[t * TOP_K + kk]
        return pltpu.make_async_copy(ys_hbm.at[pl.ds(p, 1), :], buf_ref.at[kk, pl.ds(t, 1), :], sem_rows)

    def start(t, carry):
        for kk in range(TOP_K):
            row_copy(t, kk).start()
        return carry

    def wait(t, carry):
        for kk in range(TOP_K):
            row_copy(t, kk).wait()
        return carry

    lax.fori_loop(0, tm, start, 0)
    lax.fori_loop(0, tm, wait, 0)

    tw = tw_ref[...]
    acc_lo = None
    acc_hi = None
    for kk in range(TOP_K):
        lo, hi = _unpack_bf16_pairs(buf_ref[kk])
        wk = tw[:, kk:kk + 1]
        acc_lo = wk * lo if acc_lo is None else acc_lo + wk * lo
        acc_hi = wk * hi if acc_hi is None else acc_hi + wk * hi
    x3 = x_ref[...] + jnp.concatenate([acc_lo, acc_hi], axis=1)
    out_ref[...] = _rms(x3, g_ref[...]) if final_norm else x3


def _combine(pos2d, ys, tw, x2, g, tm, final_norm):
    T = x2.shape[0]
    half = D_MODEL // 2
    return pl.pallas_call(
        functools.partial(_combine_kernel, tm=tm, final_norm=final_norm),
        grid=(T // tm,),
        in_specs=[
            pl.BlockSpec(memory_space=pl.ANY),
            pl.BlockSpec(memory_space=pl.ANY),
            pl.BlockSpec((tm, LANES), lambda i: (i, 0)),
            pl.BlockSpec((tm, D_MODEL), lambda i: (i, 0)),
            pl.BlockSpec((1, D_MODEL), lambda i: (0, 0)),
        ],
        out_specs=pl.BlockSpec((tm, D_MODEL), lambda i: (i, 0)),
        out_shape=jax.ShapeDtypeStruct((T, D_MODEL), F32),
        scratch_shapes=[
            pltpu.VMEM((TOP_K, tm, half), U32),
            pltpu.SMEM((tm * TOP_K,), I32),
            pltpu.SemaphoreType.DMA,
            pltpu.SemaphoreType.DMA,
        ],
        compiler_params=_cparams(("arbitrary",)),
        name="combine",
    )(pos2d, ys, tw, x2, g)


def _tiles(B, S):
    T = B * S
    return dict(
        tm_in=min(1024, T), tn_in=1024,
        ts=min(512, S),
        tq=min(512, S),
        tm_proj=min(512, S),
        tm_route=min(256, T),
        tg=512,
    )


def _pad_lanes(a, n=LANES, value=0.0):
    return jnp.pad(a, ((0, 0), (0, n - a.shape[1])), constant_values=value)


def kernel(x, mem, norm_mix, w_in, conv_w, b_if, mlstm_gain, diff_lambda, diff_gain, w_branch_m, w_branch_d,
           b_gate, w_out, norm_xattn, norm_mem, wq_x, wkv_x, wo_x, norm_ffn, w_router, b_router, w_gu, b_gu,
           w_dn, b_dn, norm_final):
    B, S, D = x.shape
    n_mem = mem.shape[1]
    T = B * S
    depth = norm_mix.shape[0]
    tl = _tiles(B, S)
    x2d = x.reshape(T, D)
    mem2d = mem.reshape(B * n_mem, D)

    for l in range(depth):
        lam_init = 0.8 - 0.6 * math.exp(-0.3 * l)
        wl = w_in[l]
        if_lo = 2 * M_QK + 2 * M_V
        w_main = jnp.concatenate([wl[:, :if_lo], wl[:, if_lo + 2 * M_HEADS:]], axis=1).astype(BF16)
        w_if = wl[:, if_lo:if_lo + 2 * M_HEADS]
        w_ifp = _pad_lanes(w_if).astype(BF16)
        w_ift = w_if.T.astype(BF16)
        bif = _pad_lanes(b_if[l][None, :])
        bift = jnp.broadcast_to(b_if[l][:, None], (SUBLANES, LANES))

        z, zif, zift = _inproj(x2d, norm_mix[l][None, :], w_main, w_ifp, w_ift, tl["tm_in"], tl["tn_in"])
        zift3 = zift.reshape(SUBLANES, T // CHUNK, CHUNK).transpose(1, 0, 2)

        hm = _mlstm(z, zif, zift3, conv_w[l], bif, bift, mlstm_gain[l].reshape(1, M_V), B, S, tl["ts"])
        hd = _diffattn(z, diff_lambda[l], diff_gain[l][None, :], B, S, tl["tq"], lam_init)
        x1 = _merge(hm, hd, z, x2d, w_branch_m[l].astype(BF16), w_branch_d[l].astype(BF16),
                    w_out[l].astype(BF16), b_gate[l][None, :], tl["tm_proj"])

        kvmem = _memkv(mem2d, norm_mem[l][None, :], wkv_x[l].astype(BF16), n_mem)
        x2 = _xattn(x1, norm_xattn[l][None, :], wq_x[l].astype(BF16), kvmem, wo_x[l].astype(BF16),
                    S, n_mem, tl["tm_proj"])

        wr = _pad_lanes(w_router[l])
        wr_hi = wr.astype(BF16)
        wr_lo = (wr - wr_hi.astype(F32)).astype(BF16)
        br = _pad_lanes(b_router[l][None, :], value=-jnp.inf)
        tm_r = tl["tm_route"]
        tg = tl["tg"]
        hp, ids, tw, cnt = _router(x2, norm_ffn[l][None, :], wr_hi, wr_lo, br, tm_r)

        counts = cnt[0, :N_EXPERTS].astype(I32)
        padded = ((counts + tg - 1) // tg) * tg
        ends = jnp.cumsum(padded)
        starts = ends - padded
        n_tiles = (T * TOP_K) // tg + N_EXPERTS
        tile_row0 = jnp.arange(n_tiles, dtype=I32) * tg
        tile_expert = jnp.minimum(jnp.sum((tile_row0[:, None] >= ends[None, :]).astype(I32), axis=1), N_EXPERTS - 1)
        n_used = (ends[-1] // tg).astype(I32).reshape(1)
        last_used = tile_expert[jnp.maximum(n_used[0] - 1, 0)]
        tile_expert = jnp.where(tile_row0 < ends[-1], tile_expert, last_used)

        pos = _slots(ids, _pad_lanes(starts.astype(F32)[None, :]), tm_r)
        pos2d = pos[:, :TOP_K].reshape(T // tm_r, tm_r * TOP_K)

        xs = _dispatch(hp, pos2d, jnp.zeros((n_tiles * tg, D // 2), U32), tm_r)
        ys = _experts(tile_expert, n_used, xs, w_gu[l].astype(BF16), b_gu[l][:, None, :],
                      w_dn[l].astype(BF16), b_dn[l][:, None, :], tg)
        x2d = _combine(pos2d, ys, tw, x2, norm_final[None, :], tm_r, final_norm=(l == depth - 1))
    return x2d.reshape(B, S, D)
```

```python
import functools
import math

import jax
import jax.numpy as jnp
from jax import lax
from jax.experimental import pallas as pl
from jax.experimental.pallas import tpu as pltpu

F32 = jnp.float32
BF16 = jnp.bfloat16
U32 = jnp.uint32
I32 = jnp.int32

EPS = 1e-6
CHUNK = 64
D_MODEL = 1024
M_HEADS = 4
M_DK = 128
M_DV = 256
M_QK = M_HEADS * M_DK
M_V = M_HEADS * M_DV
CONV_W = 4
D_HEADS = 8
D_DH = 64
D_QK = D_HEADS * 2 * D_DH
D_V = D_HEADS * 2 * D_DH
X_HEADS = 4
X_DH = D_MODEL // X_HEADS
N_EXPERTS = 32
TOP_K = 4
D_FF = D_MODEL
SWIGLU_LIMIT = 7.0
SWIGLU_ALPHA = 1.702

LANES = 128
SUBLANES = 8
N_MAIN = 2 * M_QK + 2 * M_V + 2 * D_QK + D_V + 2 * D_MODEL
OFF_QM, OFF_KM, OFF_VM, OFF_OM = 0, M_QK, 2 * M_QK, 2 * M_QK + M_V
OFF_QD = OFF_OM + M_V
OFF_KD = OFF_QD + D_QK
OFF_VD = OFF_KD + D_QK
OFF_G = OFF_VD + D_V

VMEM_LIMIT = 56 * 1024 * 1024
DMA_UNROLL = 4


def _cparams(sem, vmem=VMEM_LIMIT):
    return pltpu.CompilerParams(dimension_semantics=sem, vmem_limit_bytes=vmem)


def _rms(x, g):
    return x * lax.rsqrt(jnp.mean(x * x, axis=-1, keepdims=True) + EPS) * g


def _split_bf16(x):
    hi = x.astype(BF16)
    lo = (x - hi.astype(F32)).astype(BF16)
    return hi, lo


def _dot(a, b):
    return jnp.dot(a, b, preferred_element_type=F32)


def _dot_nt(a, b):
    return lax.dot_general(a, b, (((1,), (1,)), ((), ())), preferred_element_type=F32)


def _sigmoid(x):
    return 1.0 / (1.0 + jnp.exp(-x))


def _log_sigmoid(x):
    return jnp.minimum(x, 0.0) - jnp.log(1.0 + jnp.exp(-jnp.abs(x)))


def _pack_bf16_pairs(x):
    w = x.shape[1] // 2
    u = lax.bitcast_convert_type(x, U32)
    r = (u + jnp.uint32(0x7FFF) + ((u >> 16) & jnp.uint32(1))) >> 16
    return r[:, :w] | (r[:, w:] << 16)


def _unpack_bf16_pairs(p):
    lo = lax.bitcast_convert_type(p << 16, F32)
    hi = lax.bitcast_convert_type(p & jnp.uint32(0xFFFF0000), F32)
    return lo, hi


def _inproj_kernel(x_ref, g_ref, w_ref, wif_ref, wift_ref, z_ref, zif_ref, zift_ref, hn_ref):
    @pl.when(pl.program_id(1) == 0)
    def _():
        hn = _rms(x_ref[...], g_ref[...]).astype(BF16)
        hn_ref[...] = hn
        zif_ref[...] = _dot(hn, wif_ref[...])
        zift_ref[...] = _dot_nt(wift_ref[...], hn)

    z_ref[...] = _dot(hn_ref[...], w_ref[...]).astype(BF16)


def _inproj(x2d, g, w_main, w_if, w_ift, tm, tn):
    T = x2d.shape[0]
    return pl.pallas_call(
        _inproj_kernel,
        grid=(T // tm, N_MAIN // tn),
        in_specs=[
            pl.BlockSpec((tm, D_MODEL), lambda i, j: (i, 0)),
            pl.BlockSpec((1, D_MODEL), lambda i, j: (0, 0)),
            pl.BlockSpec((D_MODEL, tn), lambda i, j: (0, j)),
            pl.BlockSpec((D_MODEL, LANES), lambda i, j: (0, 0)),
            pl.BlockSpec((SUBLANES, D_MODEL), lambda i, j: (0, 0)),
        ],
        out_specs=[
            pl.BlockSpec((tm, tn), lambda i, j: (i, j)),
            pl.BlockSpec((tm, LANES), lambda i, j: (i, 0)),
            pl.BlockSpec((SUBLANES, tm), lambda i, j: (0, i)),
        ],
        out_shape=[
            jax.ShapeDtypeStruct((T, N_MAIN), BF16),
            jax.ShapeDtypeStruct((T, LANES), F32),
            jax.ShapeDtypeStruct((SUBLANES, T), F32),
        ],
        scratch_shapes=[pltpu.VMEM((tm, D_MODEL), BF16)],
        compiler_params=_cparams(("arbitrary", "arbitrary")),
        name="inproj",
    )(x2d, g, w_main, w_if, w_ift)


def _mlstm_kernel(q_ref, k_ref, v_ref, om_ref, zif_ref, zift_ref, cw_ref, bif_ref, bift_ref, mg_ref,
                  out_ref, qc_ref, kc_ref, carry_ref, c_ref, n_ref, m_ref, *, ts):
    nchunk = ts // CHUNK
    L = CHUNK

    @pl.when(pl.program_id(1) == 0)
    def _():
        carry_ref[...] = jnp.zeros_like(carry_ref)
        c_ref[...] = jnp.zeros_like(c_ref)
        n_ref[...] = jnp.zeros_like(n_ref)
        m_ref[...] = jnp.zeros_like(m_ref)

    row8 = lax.broadcasted_iota(I32, (SUBLANES, M_QK), 0)

    def conv_silu(x, prev8, w):
        acc = w[CONV_W - 1:CONV_W, :] * x
        for s in range(1, CONV_W):
            xs = pltpu.roll(x, s, 0)
            top = jnp.where(row8 < s, pltpu.roll(prev8, s, 0), xs[0:SUBLANES])
            xs = jnp.concatenate([top, xs[SUBLANES:]], axis=0)
            acc = acc + w[CONV_W - 1 - s:CONV_W - s, :] * xs
        return acc * _sigmoid(acc)

    def conv_body(c, carry):
        r0 = pl.multiple_of(c * L, L)
        xq = q_ref[pl.ds(r0, L), :].astype(F32)
        xk = k_ref[pl.ds(r0, L), :].astype(F32)
        yq = conv_silu(xq, carry_ref[:, 0:M_QK], cw_ref[:, 0:M_QK]) * (M_DK ** -0.5)
        yk = conv_silu(xk, carry_ref[:, M_QK:2 * M_QK], cw_ref[:, M_QK:2 * M_QK])
        qc_ref[pl.ds(r0, L), :] = yq.astype(BF16)
        kc_ref[pl.ds(r0, L), :] = yk.astype(BF16)
        carry_ref[:, 0:M_QK] = xq[L - SUBLANES:L]
        carry_ref[:, M_QK:2 * M_QK] = xk[L - SUBLANES:L]
        return carry

    lax.fori_loop(0, nchunk, conv_body, 0)

    ti = lax.broadcasted_iota(I32, (L, L), 0)
    si = lax.broadcasted_iota(I32, (L, L), 1)
    causal = si <= ti
    tril = jnp.where(causal, 1.0, 0.0).astype(BF16)
    triu = jnp.where(ti <= si, 1.0, 0.0).astype(BF16)

    def chunk_body(c, carry):
        r0 = pl.multiple_of(c * L, L)
        g_col = zif_ref[pl.ds(r0, L), :] + bif_ref[...]
        g_row = zift_ref[c] + bift_ref[:, 0:1]
        lf_col = _log_sigmoid(g_col)
        lf_row = _log_sigmoid(g_row)
        ch, cl = _split_bf16(lf_col)
        b_col_all = _dot(tril, ch) + _dot(tril, cl)
        rh, rl = _split_bf16(lf_row)
        b_row_all = _dot(rh, triu) + _dot(rl, triu)
        for h in range(M_HEADS):
            i_col = g_col[:, h:h + 1]
            b_col = b_col_all[:, M_HEADS + h:M_HEADS + h + 1]
            i_row = g_row[h:h + 1, :]
            b_row = b_row_all[M_HEADS + h:M_HEADS + h + 1, :]
            b_last = b_col[L - 1:L, :]
            m_prev = m_ref[h:h + 1, 0:1]
            dm = jnp.where(causal, b_col - b_row + i_row, -jnp.inf)
            inter = b_col + m_prev
            m_t = jnp.maximum(inter, jnp.max(dm, axis=-1, keepdims=True))
            w_inter = jnp.exp(inter - m_t)
            q = qc_ref[pl.ds(r0, L), h * M_DK:(h + 1) * M_DK]
            k = kc_ref[pl.ds(r0, L), h * M_DK:(h + 1) * M_DK]
            v = v_ref[pl.ds(r0, L), h * M_DV:(h + 1) * M_DV]
            s = _dot_nt(q, k) * jnp.exp(dm - m_t)
            c_old = c_ref[h]
            n_old = n_ref[h:h + 1, :]
            num = _dot(s.astype(BF16), v) + w_inter * _dot(q, c_old.astype(BF16))
            qn = jnp.sum(q.astype(F32) * n_old, axis=-1, keepdims=True)
            den = jnp.sum(s, axis=-1, keepdims=True) + w_inter * qn
            hv = num / jnp.maximum(jnp.abs(den), jnp.exp(-m_t))
            gk_col = b_last - b_col + i_col
            gk_row = b_last - b_row + i_row
            m_new = jnp.maximum(b_last + m_prev, jnp.max(gk_row, axis=-1, keepdims=True))
            wk = jnp.exp(gk_col - m_new)
            decay = jnp.exp(b_last + m_prev - m_new)
            kw = k.astype(F32) * wk
            c_ref[h] = decay * c_old + _dot(kw.T.astype(BF16), v)
            n_ref[h:h + 1, :] = decay * n_old + jnp.sum(kw, axis=0, keepdims=True)
            m_ref[h:h + 1, :] = jnp.broadcast_to(m_new, (1, LANES))
            hn = _rms(hv, mg_ref[:, h * M_DV:(h + 1) * M_DV])
            og = _sigmoid(om_ref[pl.ds(r0, L), h * M_DV:(h + 1) * M_DV].astype(F32))
            out_ref[pl.ds(r0, L), h * M_DV:(h + 1) * M_DV] = (og * hn).astype(BF16)
        return carry

    lax.fori_loop(0, nchunk, chunk_body, 0)


def _mlstm(z, zif, zift3, conv_w, bif, bift, m_gain, B, S, ts):
    T = B * S
    nt = S // ts
    nck = ts // CHUNK
    row = lambda b, t: b * nt + t
    return pl.pallas_call(
        functools.partial(_mlstm_kernel, ts=ts),
        grid=(B, nt),
        in_specs=[
            pl.BlockSpec((ts, M_QK), lambda b, t: (row(b, t), OFF_QM // M_QK)),
            pl.BlockSpec((ts, M_QK), lambda b, t: (row(b, t), OFF_KM // M_QK)),
            pl.BlockSpec((ts, M_V), lambda b, t: (row(b, t), OFF_VM // M_V)),
            pl.BlockSpec((ts, M_V), lambda b, t: (row(b, t), OFF_OM // M_V)),
            pl.BlockSpec((ts, LANES), lambda b, t: (row(b, t), 0)),
            pl.BlockSpec((nck, SUBLANES, CHUNK), lambda b, t: (row(b, t), 0, 0)),
            pl.BlockSpec((CONV_W, 2 * M_QK), lambda b, t: (0, 0)),
            pl.BlockSpec((1, LANES), lambda b, t: (0, 0)),
            pl.BlockSpec((SUBLANES, LANES), lambda b, t: (0, 0)),
            pl.BlockSpec((1, M_V), lambda b, t: (0, 0)),
        ],
        out_specs=pl.BlockSpec((ts, M_V), lambda b, t: (row(b, t), 0)),
        out_shape=jax.ShapeDtypeStruct((T, M_V), BF16),
        scratch_shapes=[
            pltpu.VMEM((ts, M_QK), BF16),
            pltpu.VMEM((ts, M_QK), BF16),
            pltpu.VMEM((SUBLANES, 2 * M_QK), F32),
            pltpu.VMEM((M_HEADS, M_DK, M_DV), F32),
            pltpu.VMEM((SUBLANES, M_DK), F32),
            pltpu.VMEM((SUBLANES, LANES), F32),
        ],
        compiler_params=_cparams(("arbitrary", "arbitrary")),
        name="mlstm",
    )(z, z, z, z, zif, zift3, conv_w, bif, bift, m_gain)


def _diffattn_kernel(q_ref, k_ref, v_ref, lam_ref, gain_ref, out_ref,
                     m1_ref, a1_ref, m2_ref, a2_ref, *, tq, lam_init):
    qi = pl.program_id(2)
    w = 2 * D_DH
    q = q_ref[...]
    lane = lax.broadcasted_iota(I32, (1, w), 1)
    scale = jnp.asarray(D_DH ** -0.5, BF16)
    q1 = jnp.where(lane < D_DH, q, jnp.zeros_like(q)) * scale
    q2 = jnp.where(lane >= D_DH, q, jnp.zeros_like(q)) * scale
    ones = jnp.ones((tq, w), BF16)

    m1_ref[...] = jnp.full_like(m1_ref, -jnp.inf)
    m2_ref[...] = jnp.full_like(m2_ref, -jnp.inf)
    a1_ref[...] = jnp.zeros_like(a1_ref)
    a2_ref[...] = jnp.zeros_like(a2_ref)

    def online(s, vext, m_ref, a_ref):
        m_old = m_ref[...]
        m_new = jnp.maximum(m_old, jnp.max(s, axis=-1, keepdims=True))
        p = jnp.exp(s - jnp.concatenate([m_new] * (tq // w), axis=1))
        alpha = jnp.exp(m_old - m_new)
        a_ref[...] = jnp.concatenate([alpha, alpha], axis=1) * a_ref[...] + _dot(p.astype(BF16), vext)
        m_ref[...] = m_new

    def block(j, mask):
        r0 = pl.multiple_of(j * tq, tq)
        k = k_ref[pl.ds(r0, tq), :]
        vext = jnp.concatenate([v_ref[pl.ds(r0, tq), :], ones], axis=1)
        s1 = _dot_nt(q1, k)
        s2 = _dot_nt(q2, k)
        if mask is not None:
            s1 = jnp.where(mask, s1, -jnp.inf)
            s2 = jnp.where(mask, s2, -jnp.inf)
        online(s1, vext, m1_ref, a1_ref)
        online(s2, vext, m2_ref, a2_ref)

    def body(jj, carry):
        block(2 * jj, None)
        block(2 * jj + 1, None)
        return carry

    lax.fori_loop(0, qi // 2, body, 0)

    @pl.when(qi % 2 == 1)
    def _():
        block(qi - 1, None)

    rq = lax.broadcasted_iota(I32, (tq, tq), 0) // CHUNK
    ck = lax.broadcasted_iota(I32, (tq, tq), 1) // CHUNK
    block(qi, ck <= rq)

    lp = lam_ref[...]
    lam = (jnp.exp(jnp.sum(lp[0:1, :] * lp[1:2, :], axis=-1, keepdims=True))
           - jnp.exp(jnp.sum(lp[2:3, :] * lp[3:4, :], axis=-1, keepdims=True)) + lam_init)
    a1 = a1_ref[...]
    a2 = a2_ref[...]
    o = a1[:, :w] / a1[:, w:] - lam * (a2[:, :w] / a2[:, w:])
    out_ref[...] = (_rms(o, gain_ref[...]) * (1.0 - lam_init)).astype(BF16)


def _diffattn(z, lam_p, d_gain, B, S, tq, lam_init):
    T = B * S
    nq = S // tq
    w = 2 * D_DH
    return pl.pallas_call(
        functools.partial(_diffattn_kernel, tq=tq, lam_init=lam_init),
        grid=(B, D_HEADS, nq),
        in_specs=[
            pl.BlockSpec((tq, w), lambda b, h, i: (b * nq + i, OFF_QD // w + h)),
            pl.BlockSpec((S, w), lambda b, h, i: (b, OFF_KD // w + h)),
            pl.BlockSpec((S, w), lambda b, h, i: (b, OFF_VD // w + h)),
            pl.BlockSpec((4, D_DH), lambda b, h, i: (0, 0)),
            pl.BlockSpec((1, w), lambda b, h, i: (0, 0)),
        ],
        out_specs=pl.BlockSpec((tq, w), lambda b, h, i: (b * nq + i, h)),
        out_shape=jax.ShapeDtypeStruct((T, D_V), BF16),
        scratch_shapes=[
            pltpu.VMEM((tq, w), F32), pltpu.VMEM((tq, 2 * w), F32),
            pltpu.VMEM((tq, w), F32), pltpu.VMEM((tq, 2 * w), F32),
        ],
        compiler_params=_cparams(("arbitrary", "arbitrary", "arbitrary")),
        name="diffattn",
    )(z, z, z, lam_p, d_gain)


def _merge_kernel(hm_ref, hd_ref, gz_ref, x_ref, wbm_ref, wbd_ref, wout_ref, bg_ref, out_ref):
    bm = _dot(hm_ref[...], wbm_ref[...])
    bd = _dot(hd_ref[...], wbd_ref[...])
    g = _sigmoid(gz_ref[...].astype(F32) + bg_ref[...])
    merged = g[:, :D_MODEL] * bm + g[:, D_MODEL:] * bd
    out_ref[...] = x_ref[...] + _dot(merged.astype(BF16), wout_ref[...])


def _merge(hm, hd, z, x2d, w_bm, w_bd, w_out, b_gate, tm):
    T = x2d.shape[0]
    full = lambda i: (0, 0)
    return pl.pallas_call(
        _merge_kernel,
        grid=(T // tm,),
        in_specs=[
            pl.BlockSpec((tm, M_V), lambda i: (i, 0)),
            pl.BlockSpec((tm, D_V), lambda i: (i, 0)),
            pl.BlockSpec((tm, 2 * D_MODEL), lambda i: (i, OFF_G // (2 * D_MODEL))),
            pl.BlockSpec((tm, D_MODEL), lambda i: (i, 0)),
            pl.BlockSpec((M_V, D_MODEL), full),
            pl.BlockSpec((D_V, D_MODEL), full),
            pl.BlockSpec((D_MODEL, D_MODEL), full),
            pl.BlockSpec((1, 2 * D_MODEL), full),
        ],
        out_specs=pl.BlockSpec((tm, D_MODEL), lambda i: (i, 0)),
        out_shape=jax.ShapeDtypeStruct((T, D_MODEL), F32),
        compiler_params=_cparams(("arbitrary",)),
        name="merge",
    )(hm, hd, z, x2d, w_bm, w_bd, w_out, b_gate)


def _memkv_kernel(mem_ref, g_ref, w_ref, out_ref):
    out_ref[...] = _dot(_rms(mem_ref[...], g_ref[...]).astype(BF16), w_ref[...]).astype(BF16)


def _memkv(mem2d, g, wkv, n_mem):
    R = mem2d.shape[0]
    return pl.pallas_call(
        _memkv_kernel,
        grid=(R // n_mem,),
        in_specs=[
            pl.BlockSpec((n_mem, D_MODEL), lambda i: (i, 0)),
            pl.BlockSpec((1, D_MODEL), lambda i: (0, 0)),
            pl.BlockSpec((D_MODEL, 2 * D_MODEL), lambda i: (0, 0)),
        ],
        out_specs=pl.BlockSpec((n_mem, 2 * D_MODEL), lambda i: (i, 0)),
        out_shape=jax.ShapeDtypeStruct((R, 2 * D_MODEL), BF16),
        compiler_params=_cparams(("arbitrary",)),
        name="memkv",
    )(mem2d, g, wkv)


def _xattn_kernel(x_ref, g_ref, wq_ref, kv_ref, wo_ref, out_ref, o_ref):
    x = x_ref[...]
    h = _rms(x, g_ref[...]).astype(BF16)
    q = (_dot(h, wq_ref[...]) * (X_DH ** -0.5)).astype(BF16)
    for hd in range(X_HEADS):
        qh = q[:, hd * X_DH:(hd + 1) * X_DH]
        kh = kv_ref[:, hd * X_DH:(hd + 1) * X_DH]
        vh = kv_ref[:, D_MODEL + hd * X_DH:D_MODEL + (hd + 1) * X_DH]
        s = _dot_nt(qh, kh)
        p = jnp.exp(s - jnp.max(s, axis=-1, keepdims=True))
        p = p / jnp.sum(p, axis=-1, keepdims=True)
        o_ref[:, hd * X_DH:(hd + 1) * X_DH] = _dot(p.astype(BF16), vh).astype(BF16)
    out_ref[...] = x + _dot(o_ref[...], wo_ref[...])


def _xattn(x1, g, wq, kvmem, wo, S, n_mem, tm):
    T = x1.shape[0]
    per_b = S // tm
    full = lambda i: (0, 0)
    return pl.pallas_call(
        _xattn_kernel,
        grid=(T // tm,),
        in_specs=[
            pl.BlockSpec((tm, D_MODEL), lambda i: (i, 0)),
            pl.BlockSpec((1, D_MODEL), full),
            pl.BlockSpec((D_MODEL, D_MODEL), full),
            pl.BlockSpec((n_mem, 2 * D_MODEL), lambda i: (i // per_b, 0)),
            pl.BlockSpec((D_MODEL, D_MODEL), full),
        ],
        out_specs=pl.BlockSpec((tm, D_MODEL), lambda i: (i, 0)),
        out_shape=jax.ShapeDtypeStruct((T, D_MODEL), F32),
        scratch_shapes=[pltpu.VMEM((tm, D_MODEL), BF16)],
        compiler_params=_cparams(("arbitrary",)),
        name="xattn",
    )(x1, g, wq, kvmem, wo)


def _router_kernel(x_ref, g_ref, wrh_ref, wrl_ref, br_ref, hp_ref, ids_ref, tw_ref, cnt_ref):
    @pl.when(pl.program_id(0) == 0)
    def _():
        cnt_ref[...] = jnp.zeros_like(cnt_ref)

    hn = _rms(x_ref[...], g_ref[...])
    hp_ref[...] = _pack_bf16_pairs(hn)
    hh, hl = _split_bf16(hn)
    logits = _dot(hh, wrh_ref[...]) + _dot(hh, wrl_ref[...]) + _dot(hl, wrh_ref[...]) + br_ref[...]
    lane = lax.broadcasted_iota(I32, logits.shape, 1)
    lanef = lane.astype(F32)
    ids = jnp.zeros(logits.shape, F32)
    tw = jnp.zeros(logits.shape, F32)
    onehot = jnp.zeros(logits.shape, F32)
    v0 = None
    den = None
    for kk in range(TOP_K):
        mx = jnp.max(logits, axis=-1, keepdims=True)
        idx = jnp.min(jnp.where(logits == mx, lanef, float(LANES)), axis=-1, keepdims=True)
        sel = lanef == idx
        if kk == 0:
            v0 = mx
        e = jnp.exp(mx - v0)
        den = e if den is None else den + e
        ids = jnp.where(lane == kk, idx, ids)
        tw = jnp.where(lane == kk, e, tw)
        onehot = jnp.where(sel, 1.0, onehot)
        logits = jnp.where(sel, -jnp.inf, logits)
    ids_ref[...] = ids.astype(I32)
    tw_ref[...] = tw / den
    cnt_ref[...] += jnp.sum(onehot, axis=0, keepdims=True)


def _router(x2, g, wr_hi, wr_lo, b_r, tm):
    T = x2.shape[0]
    full = lambda i: (0, 0)
    return pl.pallas_call(
        _router_kernel,
        grid=(T // tm,),
        in_specs=[
            pl.BlockSpec((tm, D_MODEL), lambda i: (i, 0)),
            pl.BlockSpec((1, D_MODEL), full),
            pl.BlockSpec((D_MODEL, LANES), full),
            pl.BlockSpec((D_MODEL, LANES), full),
            pl.BlockSpec((1, LANES), full),
        ],
        out_specs=[
            pl.BlockSpec((tm, D_MODEL // 2), lambda i: (i, 0)),
            pl.BlockSpec((tm, LANES), lambda i: (i, 0)),
            pl.BlockSpec((tm, LANES), lambda i: (i, 0)),
            pl.BlockSpec((1, LANES), full),
        ],
        out_shape=[
            jax.ShapeDtypeStruct((T, D_MODEL // 2), U32),
            jax.ShapeDtypeStruct((T, LANES), I32),
            jax.ShapeDtypeStruct((T, LANES), F32),
            jax.ShapeDtypeStruct((1, LANES), F32),
        ],
        compiler_params=_cparams(("arbitrary",)),
        name="router",
    )(x2, g, wr_hi, wr_lo, b_r)


def _slots_kernel(ids_ref, start_ref, pos_ref, run_ref):
    @pl.when(pl.program_id(0) == 0)
    def _():
        run_ref[...] = jnp.zeros_like(run_ref)

    ids = ids_ref[...]
    tm = ids.shape[0]
    lane = lax.broadcasted_iota(I32, ids.shape, 1)
    sels = [lane == ids[:, kk:kk + 1] for kk in range(TOP_K)]
    onehot = jnp.zeros(ids.shape, F32)
    for s in sels:
        onehot = jnp.where(s, 1.0, onehot)
    r = lax.broadcasted_iota(I32, (tm, tm), 0)
    c = lax.broadcasted_iota(I32, (tm, tm), 1)
    strict = jnp.where(c < r, 1.0, 0.0).astype(BF16)
    rank = _dot(strict, onehot.astype(BF16)) + run_ref[...] + start_ref[...]
    pos = jnp.zeros(ids.shape, F32)
    for kk, s in enumerate(sels):
        pk = jnp.sum(jnp.where(s, rank, 0.0), axis=-1, keepdims=True)
        pos = jnp.where(lane == kk, pk, pos)
    pos_ref[...] = pos.astype(I32)
    run_ref[...] += jnp.sum(onehot, axis=0, keepdims=True)


def _slots(ids, starts, tm):
    T = ids.shape[0]
    return pl.pallas_call(
        _slots_kernel,
        grid=(T // tm,),
        in_specs=[
            pl.BlockSpec((tm, LANES), lambda i: (i, 0)),
            pl.BlockSpec((1, LANES), lambda i: (0, 0)),
        ],
        out_specs=pl.BlockSpec((tm, LANES), lambda i: (i, 0)),
        out_shape=jax.ShapeDtypeStruct((T, LANES), I32),
        scratch_shapes=[pltpu.VMEM((1, LANES), F32)],
        compiler_params=_cparams(("arbitrary",)),
        name="slots",
    )(ids, starts)


def _dispatch_kernel(hp_ref, pos_hbm, xs_in, xs_ref, pos_smem, sem_pos, sem_rows, *, tm):
    del xs_in
    i = pl.program_id(0)
    cp = pltpu.make_async_copy(pos_hbm.at[i], pos_smem, sem_pos)
    cp.start()
    cp.wait()

    def row_copy(t, kk):
        p = pos_smem[t * TOP_K + kk]
        return pltpu.make_async_copy(hp_ref.at[pl.ds(t, 1), :], xs_ref.at[pl.ds(p, 1), :], sem_rows)

    def start(t, carry):
        for kk in range(TOP_K):
            row_copy(t, kk).start(priority=kk % 2)
        return carry

    def wait(t, carry):
        for kk in range(TOP_K):
            row_copy(t, kk).wait()
        return carry

    lax.fori_loop(0, tm, start, 0, unroll=DMA_UNROLL)
    lax.fori_loop(0, tm, wait, 0, unroll=DMA_UNROLL)


def _dispatch(hp, pos2d, xs_zero, tm):
    T = hp.shape[0]
    return pl.pallas_call(
        functools.partial(_dispatch_kernel, tm=tm),
        grid=(T // tm,),
        in_specs=[
            pl.BlockSpec((tm, D_MODEL // 2), lambda i: (i, 0)),
            pl.BlockSpec(memory_space=pl.ANY),
            pl.BlockSpec(memory_space=pl.ANY),
        ],
        out_specs=pl.BlockSpec(memory_space=pl.ANY),
        out_shape=jax.ShapeDtypeStruct(xs_zero.shape, U32),
        scratch_shapes=[
            pltpu.SMEM((tm * TOP_K,), I32),
            pltpu.SemaphoreType.DMA,
            pltpu.SemaphoreType.DMA,
        ],
        input_output_aliases={2: 0},
        compiler_params=_cparams(("arbitrary",)),
        name="dispatch",
    )(hp, pos2d, xs_zero)


def _experts_kernel(te_ref, nu_ref, xs_ref, wgu_ref, bgu_ref, wdn_ref, bdn_ref, ys_ref):
    del te_ref
    i = pl.program_id(0)
    half = D_MODEL // 2

    @pl.when(i < nu_ref[0])
    def _():
        lo, hi = _unpack_bf16_pairs(xs_ref[...])
        gu = (_dot(lo.astype(BF16), wgu_ref[0, :half, :]) + _dot(hi.astype(BF16), wgu_ref[0, half:, :])
              + bgu_ref[0])
        gate = jnp.minimum(gu[:, :D_FF], SWIGLU_LIMIT)
        up = jnp.clip(gu[:, D_FF:], -SWIGLU_LIMIT, SWIGLU_LIMIT)
        act = (up + 1.0) * (gate * _sigmoid(SWIGLU_ALPHA * gate))
        y = _dot(act.astype(BF16), wdn_ref[0]) + bdn_ref[0]
        ys_ref[...] = _pack_bf16_pairs(y)

    @pl.when(i >= nu_ref[0])
    def _():
        ys_ref[...] = jnp.zeros_like(ys_ref)


def _experts(tile_expert, n_used, xs, w_gu, b_gu, w_dn, b_dn, tg):
    P = xs.shape[0]
    half = D_MODEL // 2
    grid_spec = pltpu.PrefetchScalarGridSpec(
        num_scalar_prefetch=2,
        grid=(P // tg,),
        in_specs=[
            pl.BlockSpec((tg, half), lambda i, te, nu: (i, 0)),
            pl.BlockSpec((1, D_MODEL, 2 * D_FF), lambda i, te, nu: (te[i], 0, 0)),
            pl.BlockSpec((1, 1, 2 * D_FF), lambda i, te, nu: (te[i], 0, 0)),
            pl.BlockSpec((1, D_FF, D_MODEL), lambda i, te, nu: (te[i], 0, 0)),
            pl.BlockSpec((1, 1, D_MODEL), lambda i, te, nu: (te[i], 0, 0)),
        ],
        out_specs=pl.BlockSpec((tg, half), lambda i, te, nu: (i, 0)),
    )
    return pl.pallas_call(
        _experts_kernel,
        grid_spec=grid_spec,
        out_shape=jax.ShapeDtypeStruct((P, half), U32),
        compiler_params=_cparams(("arbitrary",)),
        name="experts",
    )(tile_expert, n_used, xs, w_gu, b_gu, w_dn, b_dn)


def _combine_kernel(pos_hbm, ys_hbm, tw_ref, x_ref, g_ref, out_ref, buf_ref, pos_smem, sem_pos, sem_rows, *,
                    tm, final_norm):
    i = pl.program_id(0)
    n = pl.num_programs(0)

    def row_copy(slot, t, kk):
        p = pos_smem[slot, t * TOP_K + kk]
        return pltpu.make_async_copy(ys_hbm.at[pl.ds(p, 1), :], buf_ref.at[slot, kk, pl.ds(t, 1), :],
                                     sem_rows.at[slot])

    def request(step, slot):
        cp = pltpu.make_async_copy(pos_hbm.at[step], pos_smem.at[slot], sem_pos)
        cp.start()
        cp.wait()

        def start(t, carry):
            for kk in range(TOP_K):
                row_copy(slot, t, kk).start(priority=kk % 2)
            return carry

        lax.fori_loop(0, tm, start, 0, unroll=DMA_UNROLL)

    def consume(slot):
        def wait(t, carry):
            for kk in range(TOP_K):
                row_copy(slot, t, kk).wait()
            return carry

        lax.fori_loop(0, tm, wait, 0, unroll=DMA_UNROLL)
        tw = tw_ref[...]
        acc_lo = None
        acc_hi = None
        for kk in range(TOP_K):
            lo, hi = _unpack_bf16_pairs(buf_ref[slot, kk])
            wk = tw[:, kk:kk + 1]
            acc_lo = wk * lo if acc_lo is None else acc_lo + wk * lo
            acc_hi = wk * hi if acc_hi is None else acc_hi + wk * hi
        x3 = x_ref[...] + jnp.concatenate([acc_lo, acc_hi], axis=1)
        out_ref[...] = _rms(x3, g_ref[...]) if final_norm else x3

    @pl.when(i == 0)
    def _():
        request(0, 0)

    for slot in range(2):
        @pl.when(i % 2 == slot)
        def _(slot=slot):
            @pl.when(i + 1 < n)
            def _():
                request(i + 1, 1 - slot)

            consume(slot)


def _combine(pos2d, ys, tw, x2, g, tm, final_norm):
    T = x2.shape[0]
    half = D_MODEL // 2
    return pl.pallas_call(
        functools.partial(_combine_kernel, tm=tm, final_norm=final_norm),
        grid=(T // tm,),
        in_specs=[
            pl.BlockSpec(memory_space=pl.ANY),
            pl.BlockSpec(memory_space=pl.ANY),
            pl.BlockSpec((tm, LANES), lambda i: (i, 0)),
            pl.BlockSpec((tm, D_MODEL), lambda i: (i, 0)),
            pl.BlockSpec((1, D_MODEL), lambda i: (0, 0)),
        ],
        out_specs=pl.BlockSpec((tm, D_MODEL), lambda i: (i, 0)),
        out_shape=jax.ShapeDtypeStruct((T, D_MODEL), F32),
        scratch_shapes=[
            pltpu.VMEM((2, TOP_K, tm, half), U32),
            pltpu.SMEM((2, tm * TOP_K), I32),
            pltpu.SemaphoreType.DMA,
            pltpu.SemaphoreType.DMA((2,)),
        ],
        compiler_params=_cparams(("arbitrary",)),
        name="combine",
    )(pos2d, ys, tw, x2, g)


def _tiles(B, S):
    T = B * S
    return dict(
        tm_in=min(1024, T), tn_in=1024,
        ts=min(512, S),
        tq=min(512, S),
        tm_proj=min(512, S),
        tm_route=min(256, T),
        tg=512,
    )


def _pad_lanes(a, n=LANES, value=0.0):
    return jnp.pad(a, ((0, 0), (0, n - a.shape[1])), constant_values=value)


def kernel(x, mem, norm_mix, w_in, conv_w, b_if, mlstm_gain, diff_lambda, diff_gain, w_branch_m, w_branch_d,
           b_gate, w_out, norm_xattn, norm_mem, wq_x, wkv_x, wo_x, norm_ffn, w_router, b_router, w_gu, b_gu,
           w_dn, b_dn, norm_final):
    B, S, D = x.shape
    n_mem = mem.shape[1]
    T = B * S
    depth = norm_mix.shape[0]
    tl = _tiles(B, S)
    x2d = x.reshape(T, D)
    mem2d = mem.reshape(B * n_mem, D)

    for l in range(depth):
        lam_init = 0.8 - 0.6 * math.exp(-0.3 * l)
        wl = w_in[l]
        if_lo = 2 * M_QK + 2 * M_V
        w_main = jnp.concatenate([wl[:, :if_lo], wl[:, if_lo + 2 * M_HEADS:]], axis=1).astype(BF16)
        w_if = wl[:, if_lo:if_lo + 2 * M_HEADS]
        w_ifp = _pad_lanes(w_if).astype(BF16)
        w_ift = w_if.T.astype(BF16)
        bif = _pad_lanes(b_if[l][None, :])
        bift = jnp.broadcast_to(b_if[l][:, None], (SUBLANES, LANES))

        z, zif, zift = _inproj(x2d, norm_mix[l][None, :], w_main, w_ifp, w_ift, tl["tm_in"], tl["tn_in"])
        zift3 = zift.reshape(SUBLANES, T // CHUNK, CHUNK).transpose(1, 0, 2)

        hm = _mlstm(z, zif, zift3, conv_w[l], bif, bift, mlstm_gain[l].reshape(1, M_V), B, S, tl["ts"])
        hd = _diffattn(z, diff_lambda[l], diff_gain[l][None, :], B, S, tl["tq"], lam_init)
        x1 = _merge(hm, hd, z, x2d, w_branch_m[l].astype(BF16), w_branch_d[l].astype(BF16),
                    w_out[l].astype(BF16), b_gate[l][None, :], tl["tm_proj"])

        kvmem = _memkv(mem2d, norm_mem[l][None, :], wkv_x[l].astype(BF16), n_mem)
        x2 = _xattn(x1, norm_xattn[l][None, :], wq_x[l].astype(BF16), kvmem, wo_x[l].astype(BF16),
                    S, n_mem, tl["tm_proj"])

        wr = _pad_lanes(w_router[l])
        wr_hi = wr.astype(BF16)
        wr_lo = (wr - wr_hi.astype(F32)).astype(BF16)
        br = _pad_lanes(b_router[l][None, :], value=-jnp.inf)
        tm_r = tl["tm_route"]
        tg = tl["tg"]
        hp, ids, tw, cnt = _router(x2, norm_ffn[l][None, :], wr_hi, wr_lo, br, tm_r)

        counts = cnt[0, :N_EXPERTS].astype(I32)
        padded = ((counts + tg - 1) // tg) * tg
        ends = jnp.cumsum(padded)
        starts = ends - padded
        n_tiles = (T * TOP_K) // tg + N_EXPERTS
        tile_row0 = jnp.arange(n_tiles, dtype=I32) * tg
        tile_expert = jnp.minimum(jnp.sum((tile_row0[:, None] >= ends[None, :]).astype(I32), axis=1), N_EXPERTS - 1)
        n_used = (ends[-1] // tg).astype(I32).reshape(1)
        last_used = tile_expert[jnp.maximum(n_used[0] - 1, 0)]
        tile_expert = jnp.where(tile_row0 < ends[-1], tile_expert, last_used)

        pos = _slots(ids, _pad_lanes(starts.astype(F32)[None, :]), tm_r)
        pos2d = pos[:, :TOP_K].reshape(T // tm_r, tm_r * TOP_K)

        xs = _dispatch(hp, pos2d, jnp.zeros((n_tiles * tg, D // 2), U32), tm_r)
        ys = _experts(tile_expert, n_used, xs, w_gu[l].astype(BF16), b_gu[l][:, None, :],
                      w_dn[l].astype(BF16), b_dn[l][:, None, :], tg)
        x2d = _combine(pos2d, ys, tw, x2, norm_final[None, :], tm_r, final_norm=(l == depth - 1))
    return x2d.reshape(B, S, D)
```

```python
import functools
import math

import jax
import jax.numpy as jnp
from jax import lax
from jax.experimental import pallas as pl
from jax.experimental.pallas import tpu as pltpu

F32 = jnp.float32
BF16 = jnp.bfloat16
U32 = jnp.uint32
I32 = jnp.int32

EPS = 1e-6
CHUNK = 64
D_MODEL = 1024
M_HEADS = 4
M_DK = 128
M_DV = 256
M_QK = M_HEADS * M_DK
M_V = M_HEADS * M_DV
CONV_W = 4
D_HEADS = 8
D_DH = 64
D_QK = D_HEADS * 2 * D_DH
D_V = D_HEADS * 2 * D_DH
X_HEADS = 4
X_DH = D_MODEL // X_HEADS
N_EXPERTS = 32
TOP_K = 4
D_FF = D_MODEL
SWIGLU_LIMIT = 7.0
SWIGLU_ALPHA = 1.702

LANES = 128
SUBLANES = 8
N_MAIN = 2 * M_QK + 2 * M_V + 2 * D_QK + D_V + 2 * D_MODEL
OFF_QM, OFF_KM, OFF_VM, OFF_OM = 0, M_QK, 2 * M_QK, 2 * M_QK + M_V
OFF_QD = OFF_OM + M_V
OFF_KD = OFF_QD + D_QK
OFF_VD = OFF_KD + D_QK
OFF_G = OFF_VD + D_V

VMEM_LIMIT = 56 * 1024 * 1024
DMA_UNROLL = 4


def _cparams(sem, vmem=VMEM_LIMIT):
    return pltpu.CompilerParams(dimension_semantics=sem, vmem_limit_bytes=vmem)


def _rms(x, g):
    return x * lax.rsqrt(jnp.mean(x * x, axis=-1, keepdims=True) + EPS) * g


def _split_bf16(x):
    hi = x.astype(BF16)
    lo = (x - hi.astype(F32)).astype(BF16)
    return hi, lo


def _dot(a, b):
    return jnp.dot(a, b, preferred_element_type=F32)


def _dot_nt(a, b):
    return lax.dot_general(a, b, (((1,), (1,)), ((), ())), preferred_element_type=F32)


def _sigmoid(x):
    return 1.0 / (1.0 + jnp.exp(-x))


def _log_sigmoid(x):
    return jnp.minimum(x, 0.0) - jnp.log(1.0 + jnp.exp(-jnp.abs(x)))


def _pack_bf16_pairs(x):
    w = x.shape[1] // 2
    u = lax.bitcast_convert_type(x, U32)
    r = (u + jnp.uint32(0x7FFF) + ((u >> 16) & jnp.uint32(1))) >> 16
    return r[:, :w] | (r[:, w:] << 16)


def _unpack_bf16_pairs(p):
    lo = lax.bitcast_convert_type(p << 16, F32)
    hi = lax.bitcast_convert_type(p & jnp.uint32(0xFFFF0000), F32)
    return lo, hi


def _inproj_kernel(x_ref, g_ref, w_ref, wif_ref, wift_ref, z_ref, zif_ref, zift_ref, hn_ref):
    @pl.when(pl.program_id(1) == 0)
    def _():
        hn = _rms(x_ref[...], g_ref[...]).astype(BF16)
        hn_ref[...] = hn
        zif_ref[...] = _dot(hn, wif_ref[...])
        zift_ref[...] = _dot_nt(wift_ref[...], hn)

    z_ref[...] = _dot(hn_ref[...], w_ref[...]).astype(BF16)


def _inproj(x2d, g, w_main, w_if, w_ift, tm, tn):
    T = x2d.shape[0]
    return pl.pallas_call(
        _inproj_kernel,
        grid=(T // tm, N_MAIN // tn),
        in_specs=[
            pl.BlockSpec((tm, D_MODEL), lambda i, j: (i, 0)),
            pl.BlockSpec((1, D_MODEL), lambda i, j: (0, 0)),
            pl.BlockSpec((D_MODEL, tn), lambda i, j: (0, j)),
            pl.BlockSpec((D_MODEL, LANES), lambda i, j: (0, 0)),
            pl.BlockSpec((SUBLANES, D_MODEL), lambda i, j: (0, 0)),
        ],
        out_specs=[
            pl.BlockSpec((tm, tn), lambda i, j: (i, j)),
            pl.BlockSpec((tm, LANES), lambda i, j: (i, 0)),
            pl.BlockSpec((SUBLANES, tm), lambda i, j: (0, i)),
        ],
        out_shape=[
            jax.ShapeDtypeStruct((T, N_MAIN), BF16),
            jax.ShapeDtypeStruct((T, LANES), F32),
            jax.ShapeDtypeStruct((SUBLANES, T), F32),
        ],
        scratch_shapes=[pltpu.VMEM((tm, D_MODEL), BF16)],
        compiler_params=_cparams(("arbitrary", "arbitrary")),
        name="inproj",
    )(x2d, g, w_main, w_if, w_ift)


def _mlstm_kernel(q_ref, k_ref, v_ref, om_ref, zif_ref, zift_ref, cw_ref, bif_ref, bift_ref, mg_ref,
                  out_ref, qc_ref, kc_ref, kt_ref, carry_ref, c_ref, n_ref, m_ref,
                  bd_ref, bdt_ref, brep_ref, grow_ref, brow_ref, *, ts):
    nchunk = ts // CHUNK
    L = CHUNK

    @pl.when(pl.program_id(1) == 0)
    def _():
        carry_ref[...] = jnp.zeros_like(carry_ref)
        c_ref[...] = jnp.zeros_like(c_ref)
        n_ref[...] = jnp.zeros_like(n_ref)
        m_ref[...] = jnp.zeros_like(m_ref)
        rt = lax.broadcasted_iota(I32, (ts, ts), 0)
        ct = lax.broadcasted_iota(I32, (ts, ts), 1)
        same = (rt // L) == (ct // L)
        bd_ref[...] = jnp.where(same, jnp.where(ct <= rt, 1.0, 0.0), 0.0).astype(BF16)
        bdt_ref[...] = jnp.where(same, jnp.where(rt <= ct, 1.0, 0.0), 0.0).astype(BF16)

    row8 = lax.broadcasted_iota(I32, (SUBLANES, M_QK), 0)

    def conv_silu(x, prev8, w):
        acc = w[CONV_W - 1:CONV_W, :] * x
        for s in range(1, CONV_W):
            xs = pltpu.roll(x, s, 0)
            top = jnp.where(row8 < s, pltpu.roll(prev8, s, 0), xs[0:SUBLANES])
            xs = jnp.concatenate([top, xs[SUBLANES:]], axis=0)
            acc = acc + w[CONV_W - 1 - s:CONV_W - s, :] * xs
        return acc * _sigmoid(acc)

    def conv_body(c, carry):
        r0 = pl.multiple_of(c * L, L)
        xq = q_ref[pl.ds(r0, L), :].astype(F32)
        xk = k_ref[pl.ds(r0, L), :].astype(F32)
        yq = conv_silu(xq, carry_ref[:, 0:M_QK], cw_ref[:, 0:M_QK]) * (M_DK ** -0.5)
        yk = conv_silu(xk, carry_ref[:, M_QK:2 * M_QK], cw_ref[:, M_QK:2 * M_QK])
        qc_ref[pl.ds(r0, L), :] = yq.astype(BF16)
        kc_ref[pl.ds(r0, L), :] = yk.astype(BF16)
        for h in range(M_HEADS):
            kt_ref[c, h] = yk[:, h * M_DK:(h + 1) * M_DK].T
        carry_ref[:, 0:M_QK] = xq[L - SUBLANES:L]
        carry_ref[:, M_QK:2 * M_QK] = xk[L - SUBLANES:L]
        return carry

    lax.fori_loop(0, nchunk, conv_body, 0)

    ti = lax.broadcasted_iota(I32, (L, L), 0)
    si = lax.broadcasted_iota(I32, (L, L), 1)
    causal = si <= ti
    lane_row = lax.broadcasted_iota(I32, (LANES, LANES), 0)
    ones_l = jnp.ones((L, LANES), BF16)

    lf_col = _log_sigmoid(zif_ref[...] + bif_ref[...])
    ch, cl = _split_bf16(lf_col)
    b_col_all = _dot(bd_ref[...], ch) + _dot(bd_ref[...], cl)
    bh, bl = _split_bf16(b_col_all)
    for h in range(M_HEADS):
        sel_f = jnp.where(lane_row == M_HEADS + h, 1.0, 0.0).astype(BF16)
        brep_ref[h] = _dot(bh, sel_f) + _dot(bl, sel_f)
    g_row_all = zift_ref[...] + bift_ref[:, 0:1]
    rh, rl = _split_bf16(_log_sigmoid(g_row_all))
    b_row_tile = _dot(rh, bdt_ref[...]) + _dot(rl, bdt_ref[...])
    for cc in range(nchunk):
        grow_ref[cc] = g_row_all[:, cc * L:(cc + 1) * L]
        brow_ref[cc] = b_row_tile[:, cc * L:(cc + 1) * L]

    def chunk_body(c, carry):
        r0 = pl.multiple_of(c * L, L)
        g_row = grow_ref[c]
        b_row_all = brow_ref[c]
        for h in range(M_HEADS):
            b_rep = brep_ref[h, pl.ds(r0, L), :]
            i_row = g_row[h:h + 1, :]
            b_row = b_row_all[M_HEADS + h:M_HEADS + h + 1, :]
            b_last = b_rep[L - 1:L, :]
            q = qc_ref[pl.ds(r0, L), h * M_DK:(h + 1) * M_DK]
            k = kc_ref[pl.ds(r0, L), h * M_DK:(h + 1) * M_DK]
            vext = jnp.concatenate([v_ref[pl.ds(r0, L), h * M_DV:(h + 1) * M_DV], ones_l], axis=1)
            dm = jnp.where(causal, b_rep[:, :L] - b_row + i_row, -jnp.inf)
            m_loc = jnp.max(dm, axis=-1, keepdims=True)
            s_loc = _dot_nt(q, k) * jnp.exp(dm - m_loc)
            pv = _dot(s_loc.astype(BF16), vext)
            gk_row = b_last[:, :L] - b_row + i_row
            g_max = jnp.max(gk_row, axis=-1, keepdims=True)
            kwt = (kt_ref[c, h] * jnp.exp(gk_row - g_max)).astype(BF16)
            kv = _dot(kwt, vext)
            m_prev = m_ref[h:h + 1, :]
            c_old = c_ref[h]
            n_old = n_ref[h]
            qcn = _dot(q, jnp.concatenate([c_old, n_old], axis=1).astype(BF16))
            inter = b_rep + m_prev
            m_t = jnp.maximum(inter, m_loc)
            w_inter = jnp.exp(inter - m_t)
            r_loc = jnp.exp(m_loc - m_t)
            den = r_loc * pv[:, M_DV:] + w_inter * qcn[:, M_DV:]
            inv = 1.0 / jnp.maximum(jnp.abs(den), jnp.exp(-m_t))
            hv = (jnp.concatenate([r_loc * inv] * 2, axis=1) * pv[:, :M_DV]
                  + jnp.concatenate([w_inter * inv] * 2, axis=1) * qcn[:, :M_DV])
            m_new = jnp.maximum(b_last + m_prev, g_max)
            decay = jnp.exp(b_last + m_prev - m_new)
            sc_loc = jnp.exp(g_max - m_new)
            c_ref[h] = (jnp.concatenate([decay] * 2, axis=1) * c_old
                        + jnp.concatenate([sc_loc] * 2, axis=1) * kv[:, :M_DV])
            n_ref[h] = decay * n_old + sc_loc * kv[:, M_DV:]
            m_ref[h:h + 1, :] = m_new
            hn = _rms(hv, mg_ref[:, h * M_DV:(h + 1) * M_DV])
            og = _sigmoid(om_ref[pl.ds(r0, L), h * M_DV:(h + 1) * M_DV].astype(F32))
            out_ref[pl.ds(r0, L), h * M_DV:(h + 1) * M_DV] = (og * hn).astype(BF16)
        return carry

    lax.fori_loop(0, nchunk, chunk_body, 0, unroll=4)


def _mlstm(z, zif, zift, conv_w, bif, bift, m_gain, B, S, ts):
    T = B * S
    nt = S // ts
    nck = ts // CHUNK
    row = lambda b, t: b * nt + t
    return pl.pallas_call(
        functools.partial(_mlstm_kernel, ts=ts),
        grid=(B, nt),
        in_specs=[
            pl.BlockSpec((ts, M_QK), lambda b, t: (row(b, t), OFF_QM // M_QK)),
            pl.BlockSpec((ts, M_QK), lambda b, t: (row(b, t), OFF_KM // M_QK)),
            pl.BlockSpec((ts, M_V), lambda b, t: (row(b, t), OFF_VM // M_V)),
            pl.BlockSpec((ts, M_V), lambda b, t: (row(b, t), OFF_OM // M_V)),
            pl.BlockSpec((ts, LANES), lambda b, t: (row(b, t), 0)),
            pl.BlockSpec((SUBLANES, ts), lambda b, t: (0, row(b, t))),
            pl.BlockSpec((CONV_W, 2 * M_QK), lambda b, t: (0, 0)),
            pl.BlockSpec((1, LANES), lambda b, t: (0, 0)),
            pl.BlockSpec((SUBLANES, LANES), lambda b, t: (0, 0)),
            pl.BlockSpec((1, M_V), lambda b, t: (0, 0)),
        ],
        out_specs=pl.BlockSpec((ts, M_V), lambda b, t: (row(b, t), 0)),
        out_shape=jax.ShapeDtypeStruct((T, M_V), BF16),
        scratch_shapes=[
            pltpu.VMEM((ts, M_QK), BF16),
            pltpu.VMEM((ts, M_QK), BF16),
            pltpu.VMEM((nck, M_HEADS, M_DK, CHUNK), F32),
            pltpu.VMEM((SUBLANES, 2 * M_QK), F32),
            pltpu.VMEM((M_HEADS, M_DK, M_DV), F32),
            pltpu.VMEM((M_HEADS, M_DK, LANES), F32),
            pltpu.VMEM((SUBLANES, LANES), F32),
            pltpu.VMEM((ts, ts), BF16),
            pltpu.VMEM((ts, ts), BF16),
            pltpu.VMEM((M_HEADS, ts, LANES), F32),
            pltpu.VMEM((nck, SUBLANES, CHUNK), F32),
            pltpu.VMEM((nck, SUBLANES, CHUNK), F32),
        ],
        compiler_params=_cparams(("arbitrary", "arbitrary")),
        name="mlstm",
    )(z, z, z, z, zif, zift, conv_w, bif, bift, m_gain)


def _diffattn_kernel(q_ref, k_ref, v_ref, lam_ref, gain_ref, out_ref,
                     m1_ref, a1_ref, m2_ref, a2_ref, *, tq, lam_init):
    qi = pl.program_id(2)
    w = 2 * D_DH
    q = q_ref[...]
    lane = lax.broadcasted_iota(I32, (1, w), 1)
    scale = jnp.asarray(D_DH ** -0.5, BF16)
    q1 = jnp.where(lane < D_DH, q, jnp.zeros_like(q)) * scale
    q2 = jnp.where(lane >= D_DH, q, jnp.zeros_like(q)) * scale
    ones = jnp.ones((tq, w), BF16)

    m1_ref[...] = jnp.full_like(m1_ref, -jnp.inf)
    m2_ref[...] = jnp.full_like(m2_ref, -jnp.inf)
    a1_ref[...] = jnp.zeros_like(a1_ref)
    a2_ref[...] = jnp.zeros_like(a2_ref)

    def online(s, vext, m_ref, a_ref):
        m_old = m_ref[...]
        m_new = jnp.maximum(m_old, jnp.max(s, axis=-1, keepdims=True))
        p = jnp.exp(s - jnp.concatenate([m_new] * (tq // w), axis=1))
        alpha = jnp.exp(m_old - m_new)
        a_ref[...] = jnp.concatenate([alpha, alpha], axis=1) * a_ref[...] + _dot(p.astype(BF16), vext)
        m_ref[...] = m_new

    def block(j, mask):
        r0 = pl.multiple_of(j * tq, tq)
        k = k_ref[pl.ds(r0, tq), :]
        vext = jnp.concatenate([v_ref[pl.ds(r0, tq), :], ones], axis=1)
        s1 = _dot_nt(q1, k)
        s2 = _dot_nt(q2, k)
        if mask is not None:
            s1 = jnp.where(mask, s1, -jnp.inf)
            s2 = jnp.where(mask, s2, -jnp.inf)
        online(s1, vext, m1_ref, a1_ref)
        online(s2, vext, m2_ref, a2_ref)

    def body(jj, carry):
        block(2 * jj, None)
        block(2 * jj + 1, None)
        return carry

    lax.fori_loop(0, qi // 2, body, 0)

    @pl.when(qi % 2 == 1)
    def _():
        block(qi - 1, None)

    rq = lax.broadcasted_iota(I32, (tq, tq), 0) // CHUNK
    ck = lax.broadcasted_iota(I32, (tq, tq), 1) // CHUNK
    block(qi, ck <= rq)

    lp = lam_ref[...]
    lam = (jnp.exp(jnp.sum(lp[0:1, :] * lp[1:2, :], axis=-1, keepdims=True))
           - jnp.exp(jnp.sum(lp[2:3, :] * lp[3:4, :], axis=-1, keepdims=True)) + lam_init)
    a1 = a1_ref[...]
    a2 = a2_ref[...]
    o = a1[:, :w] / a1[:, w:] - lam * (a2[:, :w] / a2[:, w:])
    out_ref[...] = (_rms(o, gain_ref[...]) * (1.0 - lam_init)).astype(BF16)


def _diffattn(z, lam_p, d_gain, B, S, tq, lam_init):
    T = B * S
    nq = S // tq
    w = 2 * D_DH
    return pl.pallas_call(
        functools.partial(_diffattn_kernel, tq=tq, lam_init=lam_init),
        grid=(B, D_HEADS, nq),
        in_specs=[
            pl.BlockSpec((tq, w), lambda b, h, i: (b * nq + i, OFF_QD // w + h)),
            pl.BlockSpec((S, w), lambda b, h, i: (b, OFF_KD // w + h)),
            pl.BlockSpec((S, w), lambda b, h, i: (b, OFF_VD // w + h)),
            pl.BlockSpec((4, D_DH), lambda b, h, i: (0, 0)),
            pl.BlockSpec((1, w), lambda b, h, i: (0, 0)),
        ],
        out_specs=pl.BlockSpec((tq, w), lambda b, h, i: (b * nq + i, h)),
        out_shape=jax.ShapeDtypeStruct((T, D_V), BF16),
        scratch_shapes=[
            pltpu.VMEM((tq, w), F32), pltpu.VMEM((tq, 2 * w), F32),
            pltpu.VMEM((tq, w), F32), pltpu.VMEM((tq, 2 * w), F32),
        ],
        compiler_params=_cparams(("arbitrary", "arbitrary", "arbitrary")),
        name="diffattn",
    )(z, z, z, lam_p, d_gain)


def _merge_kernel(hm_ref, hd_ref, gz_ref, x_ref, wbm_ref, wbd_ref, wout_ref, bg_ref, out_ref):
    bm = _dot(hm_ref[...], wbm_ref[...])
    bd = _dot(hd_ref[...], wbd_ref[...])
    g = _sigmoid(gz_ref[...].astype(F32) + bg_ref[...])
    merged = g[:, :D_MODEL] * bm + g[:, D_MODEL:] * bd
    out_ref[...] = x_ref[...] + _dot(merged.astype(BF16), wout_ref[...])


def _merge(hm, hd, z, x2d, w_bm, w_bd, w_out, b_gate, tm):
    T = x2d.shape[0]
    full = lambda i: (0, 0)
    return pl.pallas_call(
        _merge_kernel,
        grid=(T // tm,),
        in_specs=[
            pl.BlockSpec((tm, M_V), lambda i: (i, 0)),
            pl.BlockSpec((tm, D_V), lambda i: (i, 0)),
            pl.BlockSpec((tm, 2 * D_MODEL), lambda i: (i, OFF_G // (2 * D_MODEL))),
            pl.BlockSpec((tm, D_MODEL), lambda i: (i, 0)),
            pl.BlockSpec((M_V, D_MODEL), full),
            pl.BlockSpec((D_V, D_MODEL), full),
            pl.BlockSpec((D_MODEL, D_MODEL), full),
            pl.BlockSpec((1, 2 * D_MODEL), full),
        ],
        out_specs=pl.BlockSpec((tm, D_MODEL), lambda i: (i, 0)),
        out_shape=jax.ShapeDtypeStruct((T, D_MODEL), F32),
        compiler_params=_cparams(("arbitrary",)),
        name="merge",
    )(hm, hd, z, x2d, w_bm, w_bd, w_out, b_gate)


def _memkv_kernel(mem_ref, g_ref, w_ref, out_ref):
    out_ref[...] = _dot(_rms(mem_ref[...], g_ref[...]).astype(BF16), w_ref[...]).astype(BF16)


def _memkv(mem2d, g, wkv, n_mem):
    R = mem2d.shape[0]
    return pl.pallas_call(
        _memkv_kernel,
        grid=(R // n_mem,),
        in_specs=[
            pl.BlockSpec((n_mem, D_MODEL), lambda i: (i, 0)),
            pl.BlockSpec((1, D_MODEL), lambda i: (0, 0)),
            pl.BlockSpec((D_MODEL, 2 * D_MODEL), lambda i: (0, 0)),
        ],
        out_specs=pl.BlockSpec((n_mem, 2 * D_MODEL), lambda i: (i, 0)),
        out_shape=jax.ShapeDtypeStruct((R, 2 * D_MODEL), BF16),
        compiler_params=_cparams(("arbitrary",)),
        name="memkv",
    )(mem2d, g, wkv)


def _xattn_kernel(x_ref, g_ref, wq_ref, kv_ref, wo_ref, out_ref, o_ref):
    x = x_ref[...]
    h = _rms(x, g_ref[...]).astype(BF16)
    q = (_dot(h, wq_ref[...]) * (X_DH ** -0.5)).astype(BF16)
    for hd in range(X_HEADS):
        qh = q[:, hd * X_DH:(hd + 1) * X_DH]
        kh = kv_ref[:, hd * X_DH:(hd + 1) * X_DH]
        vh = kv_ref[:, D_MODEL + hd * X_DH:D_MODEL + (hd + 1) * X_DH]
        s = _dot_nt(qh, kh)
        p = jnp.exp(s - jnp.max(s, axis=-1, keepdims=True))
        p = p / jnp.sum(p, axis=-1, keepdims=True)
        o_ref[:, hd * X_DH:(hd + 1) * X_DH] = _dot(p.astype(BF16), vh).astype(BF16)
    out_ref[...] = x + _dot(o_ref[...], wo_ref[...])


def _xattn(x1, g, wq, kvmem, wo, S, n_mem, tm):
    T = x1.shape[0]
    per_b = S // tm
    full = lambda i: (0, 0)
    return pl.pallas_call(
        _xattn_kernel,
        grid=(T // tm,),
        in_specs=[
            pl.BlockSpec((tm, D_MODEL), lambda i: (i, 0)),
            pl.BlockSpec((1, D_MODEL), full),
            pl.BlockSpec((D_MODEL, D_MODEL), full),
            pl.BlockSpec((n_mem, 2 * D_MODEL), lambda i: (i // per_b, 0)),
            pl.BlockSpec((D_MODEL, D_MODEL), full),
        ],
        out_specs=pl.BlockSpec((tm, D_MODEL), lambda i: (i, 0)),
        out_shape=jax.ShapeDtypeStruct((T, D_MODEL), F32),
        scratch_shapes=[pltpu.VMEM((tm, D_MODEL), BF16)],
        compiler_params=_cparams(("arbitrary",)),
        name="xattn",
    )(x1, g, wq, kvmem, wo)


def _router_kernel(x_ref, g_ref, wrh_ref, wrl_ref, br_ref, hp_ref, ids_ref, tw_ref, cnt_ref):
    @pl.when(pl.program_id(0) == 0)
    def _():
        cnt_ref[...] = jnp.zeros_like(cnt_ref)

    hn = _rms(x_ref[...], g_ref[...])
    hp_ref[...] = _pack_bf16_pairs(hn)
    hh, hl = _split_bf16(hn)
    logits = _dot(hh, wrh_ref[...]) + _dot(hh, wrl_ref[...]) + _dot(hl, wrh_ref[...]) + br_ref[...]
    lane = lax.broadcasted_iota(I32, logits.shape, 1)
    lanef = lane.astype(F32)
    ids = jnp.zeros(logits.shape, F32)
    tw = jnp.zeros(logits.shape, F32)
    onehot = jnp.zeros(logits.shape, F32)
    v0 = None
    den = None
    for kk in range(TOP_K):
        mx = jnp.max(logits, axis=-1, keepdims=True)
        idx = jnp.min(jnp.where(logits == mx, lanef, float(LANES)), axis=-1, keepdims=True)
        sel = lanef == idx
        if kk == 0:
            v0 = mx
        e = jnp.exp(mx - v0)
        den = e if den is None else den + e
        ids = jnp.where(lane == kk, idx, ids)
        tw = jnp.where(lane == kk, e, tw)
        onehot = jnp.where(sel, 1.0, onehot)
        logits = jnp.where(sel, -jnp.inf, logits)
    ids_ref[...] = ids.astype(I32)
    tw_ref[...] = tw / den
    cnt_ref[...] += jnp.sum(onehot, axis=0, keepdims=True)


def _router(x2, g, wr_hi, wr_lo, b_r, tm):
    T = x2.shape[0]
    full = lambda i: (0, 0)
    return pl.pallas_call(
        _router_kernel,
        grid=(T // tm,),
        in_specs=[
            pl.BlockSpec((tm, D_MODEL), lambda i: (i, 0)),
            pl.BlockSpec((1, D_MODEL), full),
            pl.BlockSpec((D_MODEL, LANES), full),
            pl.BlockSpec((D_MODEL, LANES), full),
            pl.BlockSpec((1, LANES), full),
        ],
        out_specs=[
            pl.BlockSpec((tm, D_MODEL // 2), lambda i: (i, 0)),
            pl.BlockSpec((tm, LANES), lambda i: (i, 0)),
            pl.BlockSpec((tm, LANES), lambda i: (i, 0)),
            pl.BlockSpec((1, LANES), full),
        ],
        out_shape=[
            jax.ShapeDtypeStruct((T, D_MODEL // 2), U32),
            jax.ShapeDtypeStruct((T, LANES), I32),
            jax.ShapeDtypeStruct((T, LANES), F32),
            jax.ShapeDtypeStruct((1, LANES), F32),
        ],
        compiler_params=_cparams(("arbitrary",)),
        name="router",
    )(x2, g, wr_hi, wr_lo, b_r)


def _slots_kernel(ids_ref, start_ref, pos_ref, run_ref):
    @pl.when(pl.program_id(0) == 0)
    def _():
        run_ref[...] = jnp.zeros_like(run_ref)

    ids = ids_ref[...]
    tm = ids.shape[0]
    lane = lax.broadcasted_iota(I32, ids.shape, 1)
    sels = [lane == ids[:, kk:kk + 1] for kk in range(TOP_K)]
    onehot = jnp.zeros(ids.shape, F32)
    for s in sels:
        onehot = jnp.where(s, 1.0, onehot)
    r = lax.broadcasted_iota(I32, (tm, tm), 0)
    c = lax.broadcasted_iota(I32, (tm, tm), 1)
    strict = jnp.where(c < r, 1.0, 0.0).astype(BF16)
    rank = _dot(strict, onehot.astype(BF16)) + run_ref[...] + start_ref[...]
    pos = jnp.zeros(ids.shape, F32)
    for kk, s in enumerate(sels):
        pk = jnp.sum(jnp.where(s, rank, 0.0), axis=-1, keepdims=True)
        pos = jnp.where(lane == kk, pk, pos)
    pos_ref[...] = pos.astype(I32)
    run_ref[...] += jnp.sum(onehot, axis=0, keepdims=True)


def _slots(ids, starts, tm):
    T = ids.shape[0]
    return pl.pallas_call(
        _slots_kernel,
        grid=(T // tm,),
        in_specs=[
            pl.BlockSpec((tm, LANES), lambda i: (i, 0)),
            pl.BlockSpec((1, LANES), lambda i: (0, 0)),
        ],
        out_specs=pl.BlockSpec((tm, LANES), lambda i: (i, 0)),
        out_shape=jax.ShapeDtypeStruct((T, LANES), I32),
        scratch_shapes=[pltpu.VMEM((1, LANES), F32)],
        compiler_params=_cparams(("arbitrary",)),
        name="slots",
    )(ids, starts)


def _dispatch_kernel(hp_ref, pos_hbm, xs_in, xs_ref, pos_smem, sem_pos, sem_rows, *, tm):
    del xs_in
    i = pl.program_id(0)
    cp = pltpu.make_async_copy(pos_hbm.at[i], pos_smem, sem_pos)
    cp.start()
    cp.wait()

    def row_copy(t, kk):
        p = pos_smem[t * TOP_K + kk]
        return pltpu.make_async_copy(hp_ref.at[pl.ds(t, 1), :], xs_ref.at[pl.ds(p, 1), :], sem_rows)

    def start(t, carry):
        for kk in range(TOP_K):
            row_copy(t, kk).start(priority=kk % 2)
        return carry

    def wait(t, carry):
        for kk in range(TOP_K):
            row_copy(t, kk).wait()
        return carry

    lax.fori_loop(0, tm, start, 0, unroll=DMA_UNROLL)
    lax.fori_loop(0, tm, wait, 0, unroll=DMA_UNROLL)


def _dispatch(hp, pos2d, xs_zero, tm):
    T = hp.shape[0]
    return pl.pallas_call(
        functools.partial(_dispatch_kernel, tm=tm),
        grid=(T // tm,),
        in_specs=[
            pl.BlockSpec((tm, D_MODEL // 2), lambda i: (i, 0)),
            pl.BlockSpec(memory_space=pl.ANY),
            pl.BlockSpec(memory_space=pl.ANY),
        ],
        out_specs=pl.BlockSpec(memory_space=pl.ANY),
        out_shape=jax.ShapeDtypeStruct(xs_zero.shape, U32),
        scratch_shapes=[
            pltpu.SMEM((tm * TOP_K,), I32),
            pltpu.SemaphoreType.DMA,
            pltpu.SemaphoreType.DMA,
        ],
        input_output_aliases={2: 0},
        compiler_params=_cparams(("arbitrary",)),
        name="dispatch",
    )(hp, pos2d, xs_zero)


def _experts_kernel(te_ref, nu_ref, xs_ref, wgu_ref, bgu_ref, wdn_ref, bdn_ref, ys_ref, wgu_bf, wdn_bf):
    i = pl.program_id(0)
    half = D_MODEL // 2

    @pl.when((i == 0) | (te_ref[i] != te_ref[jnp.maximum(i - 1, 0)]))
    def _():
        wgu_bf[...] = wgu_ref[0].astype(BF16)
        wdn_bf[...] = wdn_ref[0].astype(BF16)

    @pl.when(i < nu_ref[0])
    def _():
        lo, hi = _unpack_bf16_pairs(xs_ref[...])
        gu = (_dot(lo.astype(BF16), wgu_bf[:half, :]) + _dot(hi.astype(BF16), wgu_bf[half:, :])
              + bgu_ref[0])
        gate = jnp.minimum(gu[:, :D_FF], SWIGLU_LIMIT)
        up = jnp.clip(gu[:, D_FF:], -SWIGLU_LIMIT, SWIGLU_LIMIT)
        act = (up + 1.0) * (gate * _sigmoid(SWIGLU_ALPHA * gate))
        y = _dot(act.astype(BF16), wdn_bf[...]) + bdn_ref[0]
        ys_ref[...] = _pack_bf16_pairs(y)

    @pl.when(i >= nu_ref[0])
    def _():
        ys_ref[...] = jnp.zeros_like(ys_ref)


def _experts(tile_expert, n_used, xs, w_gu, b_gu, w_dn, b_dn, tg):
    P = xs.shape[0]
    half = D_MODEL // 2
    grid_spec = pltpu.PrefetchScalarGridSpec(
        num_scalar_prefetch=2,
        grid=(P // tg,),
        in_specs=[
            pl.BlockSpec((tg, half), lambda i, te, nu: (i, 0)),
            pl.BlockSpec((1, D_MODEL, 2 * D_FF), lambda i, te, nu: (te[i], 0, 0)),
            pl.BlockSpec((1, 1, 2 * D_FF), lambda i, te, nu: (te[i], 0, 0)),
            pl.BlockSpec((1, D_FF, D_MODEL), lambda i, te, nu: (te[i], 0, 0)),
            pl.BlockSpec((1, 1, D_MODEL), lambda i, te, nu: (te[i], 0, 0)),
        ],
        out_specs=pl.BlockSpec((tg, half), lambda i, te, nu: (i, 0)),
        scratch_shapes=[pltpu.VMEM((D_MODEL, 2 * D_FF), BF16), pltpu.VMEM((D_FF, D_MODEL), BF16)],
    )
    return pl.pallas_call(
        _experts_kernel,
        grid_spec=grid_spec,
        out_shape=jax.ShapeDtypeStruct((P, half), U32),
        compiler_params=_cparams(("arbitrary",)),
        name="experts",
    )(tile_expert, n_used, xs, w_gu, b_gu, w_dn, b_dn)


def _combine_kernel(pos_hbm, ys_hbm, tw_ref, x_ref, g_ref, out_ref, buf_ref, pos_smem, sem_pos, sem_rows, *,
                    tm, final_norm):
    i = pl.program_id(0)
    n = pl.num_programs(0)

    def row_copy(slot, t, kk):
        p = pos_smem[slot, t * TOP_K + kk]
        return pltpu.make_async_copy(ys_hbm.at[pl.ds(p, 1), :], buf_ref.at[slot, kk, pl.ds(t, 1), :],
                                     sem_rows.at[slot])

    def request(step, slot):
        cp = pltpu.make_async_copy(pos_hbm.at[step], pos_smem.at[slot], sem_pos)
        cp.start()
        cp.wait()

        def start(t, carry):
            for kk in range(TOP_K):
                row_copy(slot, t, kk).start(priority=kk % 2)
            return carry

        lax.fori_loop(0, tm, start, 0, unroll=DMA_UNROLL)

    def consume(slot):
        def wait(t, carry):
            for kk in range(TOP_K):
                row_copy(slot, t, kk).wait()
            return carry

        lax.fori_loop(0, tm, wait, 0, unroll=DMA_UNROLL)
        tw = tw_ref[...]
        acc_lo = None
        acc_hi = None
        for kk in range(TOP_K):
            lo, hi = _unpack_bf16_pairs(buf_ref[slot, kk])
            wk = tw[:, kk:kk + 1]
            acc_lo = wk * lo if acc_lo is None else acc_lo + wk * lo
            acc_hi = wk * hi if acc_hi is None else acc_hi + wk * hi
        x3 = x_ref[...] + jnp.concatenate([acc_lo, acc_hi], axis=1)
        out_ref[...] = _rms(x3, g_ref[...]) if final_norm else x3

    @pl.when(i == 0)
    def _():
        request(0, 0)

    for slot in range(2):
        @pl.when(i % 2 == slot)
        def _(slot=slot):
            @pl.when(i + 1 < n)
            def _():
                request(i + 1, 1 - slot)

            consume(slot)


def _combine(pos2d, ys, tw, x2, g, tm, final_norm):
    T = x2.shape[0]
    half = D_MODEL // 2
    return pl.pallas_call(
        functools.partial(_combine_kernel, tm=tm, final_norm=final_norm),
        grid=(T // tm,),
        in_specs=[
            pl.BlockSpec(memory_space=pl.ANY),
            pl.BlockSpec(memory_space=pl.ANY),
            pl.BlockSpec((tm, LANES), lambda i: (i, 0)),
            pl.BlockSpec((tm, D_MODEL), lambda i: (i, 0)),
            pl.BlockSpec((1, D_MODEL), lambda i: (0, 0)),
        ],
        out_specs=pl.BlockSpec((tm, D_MODEL), lambda i: (i, 0)),
        out_shape=jax.ShapeDtypeStruct((T, D_MODEL), F32),
        scratch_shapes=[
            pltpu.VMEM((2, TOP_K, tm, half), U32),
            pltpu.SMEM((2, tm * TOP_K), I32),
            pltpu.SemaphoreType.DMA,
            pltpu.SemaphoreType.DMA((2,)),
        ],
        compiler_params=_cparams(("arbitrary",)),
        name="combine",
    )(pos2d, ys, tw, x2, g)


def _tiles(B, S):
    T = B * S
    return dict(
        tm_in=min(1024, T), tn_in=1024,
        ts=min(512, S),
        tq=min(512, S),
        tm_proj=min(512, S),
        tm_route=min(256, T),
        tg=512,
    )


def _pad_lanes(a, n=LANES, value=0.0):
    return jnp.pad(a, ((0, 0), (0, n - a.shape[1])), constant_values=value)


def kernel(x, mem, norm_mix, w_in, conv_w, b_if, mlstm_gain, diff_lambda, diff_gain, w_branch_m, w_branch_d,
           b_gate, w_out, norm_xattn, norm_mem, wq_x, wkv_x, wo_x, norm_ffn, w_router, b_router, w_gu, b_gu,
           w_dn, b_dn, norm_final):
    B, S, D = x.shape
    n_mem = mem.shape[1]
    T = B * S
    depth = norm_mix.shape[0]
    tl = _tiles(B, S)
    x2d = x.reshape(T, D)
    mem2d = mem.reshape(B * n_mem, D)

    for l in range(depth):
        lam_init = 0.8 - 0.6 * math.exp(-0.3 * l)
        wl = w_in[l]
        if_lo = 2 * M_QK + 2 * M_V
        w_main = jnp.concatenate([wl[:, :if_lo], wl[:, if_lo + 2 * M_HEADS:]], axis=1).astype(BF16)
        w_if = wl[:, if_lo:if_lo + 2 * M_HEADS]
        w_ifp = _pad_lanes(w_if).astype(BF16)
        w_ift = w_if.T.astype(BF16)
        bif = _pad_lanes(b_if[l][None, :])
        bift = jnp.broadcast_to(b_if[l][:, None], (SUBLANES, LANES))

        z, zif, zift = _inproj(x2d, norm_mix[l][None, :], w_main, w_ifp, w_ift, tl["tm_in"], tl["tn_in"])
        hm = _mlstm(z, zif, zift, conv_w[l], bif, bift, mlstm_gain[l].reshape(1, M_V), B, S, tl["ts"])
        hd = _diffattn(z, diff_lambda[l], diff_gain[l][None, :], B, S, tl["tq"], lam_init)
        x1 = _merge(hm, hd, z, x2d, w_branch_m[l].astype(BF16), w_branch_d[l].astype(BF16),
                    w_out[l].astype(BF16), b_gate[l][None, :], tl["tm_proj"])

        kvmem = _memkv(mem2d, norm_mem[l][None, :], wkv_x[l].astype(BF16), n_mem)
        x2 = _xattn(x1, norm_xattn[l][None, :], wq_x[l].astype(BF16), kvmem, wo_x[l].astype(BF16),
                    S, n_mem, tl["tm_proj"])

        wr = _pad_lanes(w_router[l])
        wr_hi = wr.astype(BF16)
        wr_lo = (wr - wr_hi.astype(F32)).astype(BF16)
        br = _pad_lanes(b_router[l][None, :], value=-jnp.inf)
        tm_r = tl["tm_route"]
        tg = tl["tg"]
        hp, ids, tw, cnt = _router(x2, norm_ffn[l][None, :], wr_hi, wr_lo, br, tm_r)

        counts = cnt[0, :N_EXPERTS].astype(I32)
        padded = ((counts + tg - 1) // tg) * tg
        ends = jnp.cumsum(padded)
        starts = ends - padded
        n_tiles = (T * TOP_K) // tg + N_EXPERTS
        tile_row0 = jnp.arange(n_tiles, dtype=I32) * tg
        tile_expert = jnp.minimum(jnp.sum((tile_row0[:, None] >= ends[None, :]).astype(I32), axis=1), N_EXPERTS - 1)
        n_used = (ends[-1] // tg).astype(I32).reshape(1)
        last_used = tile_expert[jnp.maximum(n_used[0] - 1, 0)]
        tile_expert = jnp.where(tile_row0 < ends[-1], tile_expert, last_used)

        pos = _slots(ids, _pad_lanes(starts.astype(F32)[None, :]), tm_r)
        pos2d = pos[:, :TOP_K].reshape(T // tm_r, tm_r * TOP_K)

        xs = _dispatch(hp, pos2d, jnp.zeros((n_tiles * tg, D // 2), U32), tm_r)
        ys = _experts(tile_expert, n_used, xs, w_gu[l], b_gu[l][:, None, :], w_dn[l], b_dn[l][:, None, :], tg)
        x2d = _combine(pos2d, ys, tw, x2, norm_final[None, :], tm_r, final_norm=(l == depth - 1))
    return x2d.reshape(B, S, D)
```

```python
import functools
import math

import jax
import jax.numpy as jnp
from jax import lax
from jax.experimental import pallas as pl
from jax.experimental.pallas import tpu as pltpu

F32 = jnp.float32
BF16 = jnp.bfloat16
U32 = jnp.uint32
I32 = jnp.int32

EPS = 1e-6
CHUNK = 64
D_MODEL = 1024
M_HEADS = 4
M_DK = 128
M_DV = 256
M_QK = M_HEADS * M_DK
M_V = M_HEADS * M_DV
CONV_W = 4
D_HEADS = 8
D_DH = 64
D_QK = D_HEADS * 2 * D_DH
D_V = D_HEADS * 2 * D_DH
X_HEADS = 4
X_DH = D_MODEL // X_HEADS
N_EXPERTS = 32
TOP_K = 4
D_FF = D_MODEL
SWIGLU_LIMIT = 7.0
SWIGLU_ALPHA = 1.702

LANES = 128
SUBLANES = 8
N_MAIN = 2 * M_QK + 2 * M_V + 2 * D_QK + D_V + 2 * D_MODEL
OFF_QM, OFF_KM, OFF_VM, OFF_OM = 0, M_QK, 2 * M_QK, 2 * M_QK + M_V
OFF_QD = OFF_OM + M_V
OFF_KD = OFF_QD + D_QK
OFF_VD = OFF_KD + D_QK
OFF_G = OFF_VD + D_V

VMEM_LIMIT = 56 * 1024 * 1024
DMA_UNROLL = 4


def _cparams(sem, vmem=VMEM_LIMIT):
    return pltpu.CompilerParams(dimension_semantics=sem, vmem_limit_bytes=vmem)


def _rms(x, g):
    return x * lax.rsqrt(jnp.mean(x * x, axis=-1, keepdims=True) + EPS) * g


def _split_bf16(x):
    hi = x.astype(BF16)
    lo = (x - hi.astype(F32)).astype(BF16)
    return hi, lo


def _dot(a, b):
    return jnp.dot(a, b, preferred_element_type=F32)


def _dot_nt(a, b):
    return lax.dot_general(a, b, (((1,), (1,)), ((), ())), preferred_element_type=F32)


def _sigmoid(x):
    return 1.0 / (1.0 + jnp.exp(-x))


def _log_sigmoid(x):
    return jnp.minimum(x, 0.0) - jnp.log(1.0 + jnp.exp(-jnp.abs(x)))


def _pack_bf16_pairs(x):
    w = x.shape[1] // 2
    u = lax.bitcast_convert_type(x, U32)
    r = (u + jnp.uint32(0x7FFF) + ((u >> 16) & jnp.uint32(1))) >> 16
    return r[:, :w] | (r[:, w:] << 16)


def _ceil_rows(x):
    return jnp.floor((x + (SUBLANES - 1)) * (1.0 / SUBLANES)) * SUBLANES


def _unpack_bf16_pairs(p):
    lo = lax.bitcast_convert_type(p << 16, F32)
    hi = lax.bitcast_convert_type(p & jnp.uint32(0xFFFF0000), F32)
    return lo, hi


def _inproj_kernel(x_ref, g_ref, w_ref, wif_ref, wift_ref, z_ref, zif_ref, zift_ref, hn_ref):
    @pl.when(pl.program_id(1) == 0)
    def _():
        hn = _rms(x_ref[...], g_ref[...]).astype(BF16)
        hn_ref[...] = hn
        zif_ref[...] = _dot(hn, wif_ref[...])
        zift_ref[...] = _dot_nt(wift_ref[...], hn)

    z_ref[...] = _dot(hn_ref[...], w_ref[...]).astype(BF16)


def _inproj(x2d, g, w_main, w_if, w_ift, tm, tn):
    T = x2d.shape[0]
    return pl.pallas_call(
        _inproj_kernel,
        grid=(T // tm, N_MAIN // tn),
        in_specs=[
            pl.BlockSpec((tm, D_MODEL), lambda i, j: (i, 0)),
            pl.BlockSpec((1, D_MODEL), lambda i, j: (0, 0)),
            pl.BlockSpec((D_MODEL, tn), lambda i, j: (0, j)),
            pl.BlockSpec((D_MODEL, LANES), lambda i, j: (0, 0)),
            pl.BlockSpec((SUBLANES, D_MODEL), lambda i, j: (0, 0)),
        ],
        out_specs=[
            pl.BlockSpec((tm, tn), lambda i, j: (i, j)),
            pl.BlockSpec((tm, LANES), lambda i, j: (i, 0)),
            pl.BlockSpec((SUBLANES, tm), lambda i, j: (0, i)),
        ],
        out_shape=[
            jax.ShapeDtypeStruct((T, N_MAIN), BF16),
            jax.ShapeDtypeStruct((T, LANES), F32),
            jax.ShapeDtypeStruct((SUBLANES, T), F32),
        ],
        scratch_shapes=[pltpu.VMEM((tm, D_MODEL), BF16)],
        compiler_params=_cparams(("arbitrary", "arbitrary")),
        name="inproj",
    )(x2d, g, w_main, w_if, w_ift)


def _mlstm_kernel(q_ref, k_ref, v_ref, om_ref, zif_ref, zift_ref, cw_ref, bif_ref, bift_ref, mg_ref,
                  out_ref, qc_ref, kc_ref, kt_ref, carry_ref, c_ref, n_ref, m_ref,
                  bd_ref, bdt_ref, brep_ref, grow_ref, brow_ref, *, ts):
    nchunk = ts // CHUNK
    L = CHUNK

    @pl.when(pl.program_id(1) == 0)
    def _():
        carry_ref[...] = jnp.zeros_like(carry_ref)
        c_ref[...] = jnp.zeros_like(c_ref)
        n_ref[...] = jnp.zeros_like(n_ref)
        m_ref[...] = jnp.zeros_like(m_ref)
        rt = lax.broadcasted_iota(I32, (ts, ts), 0)
        ct = lax.broadcasted_iota(I32, (ts, ts), 1)
        same = (rt // L) == (ct // L)
        bd_ref[...] = jnp.where(same, jnp.where(ct <= rt, 1.0, 0.0), 0.0).astype(BF16)
        bdt_ref[...] = jnp.where(same, jnp.where(rt <= ct, 1.0, 0.0), 0.0).astype(BF16)

    row8 = lax.broadcasted_iota(I32, (SUBLANES, M_QK), 0)

    def conv_silu(x, prev8, w):
        acc = w[CONV_W - 1:CONV_W, :] * x
        for s in range(1, CONV_W):
            xs = pltpu.roll(x, s, 0)
            top = jnp.where(row8 < s, pltpu.roll(prev8, s, 0), xs[0:SUBLANES])
            xs = jnp.concatenate([top, xs[SUBLANES:]], axis=0)
            acc = acc + w[CONV_W - 1 - s:CONV_W - s, :] * xs
        return acc * _sigmoid(acc)

    def conv_body(c, carry):
        r0 = pl.multiple_of(c * L, L)
        xq = q_ref[pl.ds(r0, L), :].astype(F32)
        xk = k_ref[pl.ds(r0, L), :].astype(F32)
        yq = conv_silu(xq, carry_ref[:, 0:M_QK], cw_ref[:, 0:M_QK]) * (M_DK ** -0.5)
        yk = conv_silu(xk, carry_ref[:, M_QK:2 * M_QK], cw_ref[:, M_QK:2 * M_QK])
        qc_ref[pl.ds(r0, L), :] = yq.astype(BF16)
        kc_ref[pl.ds(r0, L), :] = yk.astype(BF16)
        for h in range(M_HEADS):
            kt_ref[c, h] = yk[:, h * M_DK:(h + 1) * M_DK].T
        carry_ref[:, 0:M_QK] = xq[L - SUBLANES:L]
        carry_ref[:, M_QK:2 * M_QK] = xk[L - SUBLANES:L]
        return carry

    lax.fori_loop(0, nchunk, conv_body, 0)

    ti = lax.broadcasted_iota(I32, (L, L), 0)
    si = lax.broadcasted_iota(I32, (L, L), 1)
    causal = si <= ti
    lane_row = lax.broadcasted_iota(I32, (LANES, LANES), 0)
    ones_l = jnp.ones((L, LANES), BF16)

    lf_col = _log_sigmoid(zif_ref[...] + bif_ref[...])
    ch, cl = _split_bf16(lf_col)
    b_col_all = _dot(bd_ref[...], ch) + _dot(bd_ref[...], cl)
    bh, bl = _split_bf16(b_col_all)
    for h in range(M_HEADS):
        sel_f = jnp.where(lane_row == M_HEADS + h, 1.0, 0.0).astype(BF16)
        brep_ref[h] = _dot(bh, sel_f) + _dot(bl, sel_f)
    g_row_all = zift_ref[...] + bift_ref[:, 0:1]
    rh, rl = _split_bf16(_log_sigmoid(g_row_all))
    b_row_tile = _dot(rh, bdt_ref[...]) + _dot(rl, bdt_ref[...])
    for cc in range(nchunk):
        grow_ref[cc] = g_row_all[:, cc * L:(cc + 1) * L]
        brow_ref[cc] = b_row_tile[:, cc * L:(cc + 1) * L]

    def chunk_body(c, carry):
        r0 = pl.multiple_of(c * L, L)
        g_row = grow_ref[c]
        b_row_all = brow_ref[c]
        for h in range(M_HEADS):
            b_rep = brep_ref[h, pl.ds(r0, L), :]
            i_row = g_row[h:h + 1, :]
            b_row = b_row_all[M_HEADS + h:M_HEADS + h + 1, :]
            b_last = b_rep[L - 1:L, :]
            q = qc_ref[pl.ds(r0, L), h * M_DK:(h + 1) * M_DK]
            k = kc_ref[pl.ds(r0, L), h * M_DK:(h + 1) * M_DK]
            vext = jnp.concatenate([v_ref[pl.ds(r0, L), h * M_DV:(h + 1) * M_DV], ones_l], axis=1)
            dm = jnp.where(causal, b_rep[:, :L] - b_row + i_row, -jnp.inf)
            m_loc = jnp.max(dm, axis=-1, keepdims=True)
            s_loc = _dot_nt(q, k) * jnp.exp(dm - m_loc)
            pv = _dot(s_loc.astype(BF16), vext)
            gk_row = b_last[:, :L] - b_row + i_row
            g_max = jnp.max(gk_row, axis=-1, keepdims=True)
            kwt = (kt_ref[c, h] * jnp.exp(gk_row - g_max)).astype(BF16)
            kv = _dot(kwt, vext)
            m_prev = m_ref[h:h + 1, :]
            c_old = c_ref[h]
            n_old = n_ref[h]
            qcn = _dot(q, jnp.concatenate([c_old, n_old], axis=1).astype(BF16))
            inter = b_rep + m_prev
            m_t = jnp.maximum(inter, m_loc)
            w_inter = jnp.exp(inter - m_t)
            r_loc = jnp.exp(m_loc - m_t)
            den = r_loc * pv[:, M_DV:] + w_inter * qcn[:, M_DV:]
            inv = 1.0 / jnp.maximum(jnp.abs(den), jnp.exp(-m_t))
            hv = (jnp.concatenate([r_loc * inv] * 2, axis=1) * pv[:, :M_DV]
                  + jnp.concatenate([w_inter * inv] * 2, axis=1) * qcn[:, :M_DV])
            m_new = jnp.maximum(b_last + m_prev, g_max)
            decay = jnp.exp(b_last + m_prev - m_new)
            sc_loc = jnp.exp(g_max - m_new)
            c_ref[h] = (jnp.concatenate([decay] * 2, axis=1) * c_old
                        + jnp.concatenate([sc_loc] * 2, axis=1) * kv[:, :M_DV])
            n_ref[h] = decay * n_old + sc_loc * kv[:, M_DV:]
            m_ref[h:h + 1, :] = m_new
            hn = _rms(hv, mg_ref[:, h * M_DV:(h + 1) * M_DV])
            og = _sigmoid(om_ref[pl.ds(r0, L), h * M_DV:(h + 1) * M_DV].astype(F32))
            out_ref[pl.ds(r0, L), h * M_DV:(h + 1) * M_DV] = (og * hn).astype(BF16)
        return carry

    lax.fori_loop(0, nchunk, chunk_body, 0, unroll=4)


def _mlstm(z, zif, zift, conv_w, bif, bift, m_gain, B, S, ts):
    T = B * S
    nt = S // ts
    nck = ts // CHUNK
    row = lambda b, t: b * nt + t
    return pl.pallas_call(
        functools.partial(_mlstm_kernel, ts=ts),
        grid=(B, nt),
        in_specs=[
            pl.BlockSpec((ts, M_QK), lambda b, t: (row(b, t), OFF_QM // M_QK)),
            pl.BlockSpec((ts, M_QK), lambda b, t: (row(b, t), OFF_KM // M_QK)),
            pl.BlockSpec((ts, M_V), lambda b, t: (row(b, t), OFF_VM // M_V)),
            pl.BlockSpec((ts, M_V), lambda b, t: (row(b, t), OFF_OM // M_V)),
            pl.BlockSpec((ts, LANES), lambda b, t: (row(b, t), 0)),
            pl.BlockSpec((SUBLANES, ts), lambda b, t: (0, row(b, t))),
            pl.BlockSpec((CONV_W, 2 * M_QK), lambda b, t: (0, 0)),
            pl.BlockSpec((1, LANES), lambda b, t: (0, 0)),
            pl.BlockSpec((SUBLANES, LANES), lambda b, t: (0, 0)),
            pl.BlockSpec((1, M_V), lambda b, t: (0, 0)),
        ],
        out_specs=pl.BlockSpec((ts, M_V), lambda b, t: (row(b, t), 0)),
        out_shape=jax.ShapeDtypeStruct((T, M_V), BF16),
        scratch_shapes=[
            pltpu.VMEM((ts, M_QK), BF16),
            pltpu.VMEM((ts, M_QK), BF16),
            pltpu.VMEM((nck, M_HEADS, M_DK, CHUNK), F32),
            pltpu.VMEM((SUBLANES, 2 * M_QK), F32),
            pltpu.VMEM((M_HEADS, M_DK, M_DV), F32),
            pltpu.VMEM((M_HEADS, M_DK, LANES), F32),
            pltpu.VMEM((SUBLANES, LANES), F32),
            pltpu.VMEM((ts, ts), BF16),
            pltpu.VMEM((ts, ts), BF16),
            pltpu.VMEM((M_HEADS, ts, LANES), F32),
            pltpu.VMEM((nck, SUBLANES, CHUNK), F32),
            pltpu.VMEM((nck, SUBLANES, CHUNK), F32),
        ],
        compiler_params=_cparams(("arbitrary", "arbitrary")),
        name="mlstm",
    )(z, z, z, z, zif, zift, conv_w, bif, bift, m_gain)


def _diffattn_kernel(q_ref, k_ref, v_ref, lam_ref, gain_ref, out_ref,
                     m1_ref, a1_ref, m2_ref, a2_ref, *, tq, lam_init):
    qi = pl.program_id(2)
    w = 2 * D_DH
    q = q_ref[...]
    lane = lax.broadcasted_iota(I32, (1, w), 1)
    scale = jnp.asarray(D_DH ** -0.5, BF16)
    q1 = jnp.where(lane < D_DH, q, jnp.zeros_like(q)) * scale
    q2 = jnp.where(lane >= D_DH, q, jnp.zeros_like(q)) * scale
    ones = jnp.ones((tq, w), BF16)

    m1_ref[...] = jnp.full_like(m1_ref, -jnp.inf)
    m2_ref[...] = jnp.full_like(m2_ref, -jnp.inf)
    a1_ref[...] = jnp.zeros_like(a1_ref)
    a2_ref[...] = jnp.zeros_like(a2_ref)

    def online(s, vext, m_ref, a_ref):
        m_old = m_ref[...]
        m_new = jnp.maximum(m_old, jnp.max(s, axis=-1, keepdims=True))
        p = jnp.exp(s - jnp.concatenate([m_new] * (tq // w), axis=1))
        alpha = jnp.exp(m_old - m_new)
        a_ref[...] = jnp.concatenate([alpha, alpha], axis=1) * a_ref[...] + _dot(p.astype(BF16), vext)
        m_ref[...] = m_new

    def block(j, mask):
        r0 = pl.multiple_of(j * tq, tq)
        k = k_ref[pl.ds(r0, tq), :]
        vext = jnp.concatenate([v_ref[pl.ds(r0, tq), :], ones], axis=1)
        s1 = _dot_nt(q1, k)
        s2 = _dot_nt(q2, k)
        if mask is not None:
            s1 = jnp.where(mask, s1, -jnp.inf)
            s2 = jnp.where(mask, s2, -jnp.inf)
        online(s1, vext, m1_ref, a1_ref)
        online(s2, vext, m2_ref, a2_ref)

    def body(jj, carry):
        block(2 * jj, None)
        block(2 * jj + 1, None)
        return carry

    lax.fori_loop(0, qi // 2, body, 0)

    @pl.when(qi % 2 == 1)
    def _():
        block(qi - 1, None)

    rq = lax.broadcasted_iota(I32, (tq, tq), 0) // CHUNK
    ck = lax.broadcasted_iota(I32, (tq, tq), 1) // CHUNK
    block(qi, ck <= rq)

    lp = lam_ref[...]
    lam = (jnp.exp(jnp.sum(lp[0:1, :] * lp[1:2, :], axis=-1, keepdims=True))
           - jnp.exp(jnp.sum(lp[2:3, :] * lp[3:4, :], axis=-1, keepdims=True)) + lam_init)
    a1 = a1_ref[...]
    a2 = a2_ref[...]
    o = a1[:, :w] / a1[:, w:] - lam * (a2[:, :w] / a2[:, w:])
    out_ref[...] = (_rms(o, gain_ref[...]) * (1.0 - lam_init)).astype(BF16)


def _diffattn(z, lam_p, d_gain, B, S, tq, lam_init):
    T = B * S
    nq = S // tq
    w = 2 * D_DH
    return pl.pallas_call(
        functools.partial(_diffattn_kernel, tq=tq, lam_init=lam_init),
        grid=(B, D_HEADS, nq),
        in_specs=[
            pl.BlockSpec((tq, w), lambda b, h, i: (b * nq + i, OFF_QD // w + h)),
            pl.BlockSpec((S, w), lambda b, h, i: (b, OFF_KD // w + h)),
            pl.BlockSpec((S, w), lambda b, h, i: (b, OFF_VD // w + h)),
            pl.BlockSpec((4, D_DH), lambda b, h, i: (0, 0)),
            pl.BlockSpec((1, w), lambda b, h, i: (0, 0)),
        ],
        out_specs=pl.BlockSpec((tq, w), lambda b, h, i: (b * nq + i, h)),
        out_shape=jax.ShapeDtypeStruct((T, D_V), BF16),
        scratch_shapes=[
            pltpu.VMEM((tq, w), F32), pltpu.VMEM((tq, 2 * w), F32),
            pltpu.VMEM((tq, w), F32), pltpu.VMEM((tq, 2 * w), F32),
        ],
        compiler_params=_cparams(("arbitrary", "arbitrary", "arbitrary")),
        name="diffattn",
    )(z, z, z, lam_p, d_gain)


def _merge_kernel(hm_ref, hd_ref, gz_ref, x_ref, wbm_ref, wbd_ref, wout_ref, bg_ref, out_ref):
    bm = _dot(hm_ref[...], wbm_ref[...])
    bd = _dot(hd_ref[...], wbd_ref[...])
    g = _sigmoid(gz_ref[...].astype(F32) + bg_ref[...])
    merged = g[:, :D_MODEL] * bm + g[:, D_MODEL:] * bd
    out_ref[...] = x_ref[...] + _dot(merged.astype(BF16), wout_ref[...])


def _merge(hm, hd, z, x2d, w_bm, w_bd, w_out, b_gate, tm):
    T = x2d.shape[0]
    full = lambda i: (0, 0)
    return pl.pallas_call(
        _merge_kernel,
        grid=(T // tm,),
        in_specs=[
            pl.BlockSpec((tm, M_V), lambda i: (i, 0)),
            pl.BlockSpec((tm, D_V), lambda i: (i, 0)),
            pl.BlockSpec((tm, 2 * D_MODEL), lambda i: (i, OFF_G // (2 * D_MODEL))),
            pl.BlockSpec((tm, D_MODEL), lambda i: (i, 0)),
            pl.BlockSpec((M_V, D_MODEL), full),
            pl.BlockSpec((D_V, D_MODEL), full),
            pl.BlockSpec((D_MODEL, D_MODEL), full),
            pl.BlockSpec((1, 2 * D_MODEL), full),
        ],
        out_specs=pl.BlockSpec((tm, D_MODEL), lambda i: (i, 0)),
        out_shape=jax.ShapeDtypeStruct((T, D_MODEL), F32),
        compiler_params=_cparams(("arbitrary",)),
        name="merge",
    )(hm, hd, z, x2d, w_bm, w_bd, w_out, b_gate)


def _memkv_kernel(mem_ref, g_ref, w_ref, out_ref):
    out_ref[...] = _dot(_rms(mem_ref[...], g_ref[...]).astype(BF16), w_ref[...]).astype(BF16)


def _memkv(mem2d, g, wkv, n_mem):
    R = mem2d.shape[0]
    return pl.pallas_call(
        _memkv_kernel,
        grid=(R // n_mem,),
        in_specs=[
            pl.BlockSpec((n_mem, D_MODEL), lambda i: (i, 0)),
            pl.BlockSpec((1, D_MODEL), lambda i: (0, 0)),
            pl.BlockSpec((D_MODEL, 2 * D_MODEL), lambda i: (0, 0)),
        ],
        out_specs=pl.BlockSpec((n_mem, 2 * D_MODEL), lambda i: (i, 0)),
        out_shape=jax.ShapeDtypeStruct((R, 2 * D_MODEL), BF16),
        compiler_params=_cparams(("arbitrary",)),
        name="memkv",
    )(mem2d, g, wkv)


def _xattn_kernel(x_ref, g_ref, wq_ref, kv_ref, wo_ref, out_ref, o_ref):
    x = x_ref[...]
    h = _rms(x, g_ref[...]).astype(BF16)
    q = (_dot(h, wq_ref[...]) * (X_DH ** -0.5)).astype(BF16)
    for hd in range(X_HEADS):
        qh = q[:, hd * X_DH:(hd + 1) * X_DH]
        kh = kv_ref[:, hd * X_DH:(hd + 1) * X_DH]
        vh = kv_ref[:, D_MODEL + hd * X_DH:D_MODEL + (hd + 1) * X_DH]
        s = _dot_nt(qh, kh)
        p = jnp.exp(s - jnp.max(s, axis=-1, keepdims=True))
        p = p / jnp.sum(p, axis=-1, keepdims=True)
        o_ref[:, hd * X_DH:(hd + 1) * X_DH] = _dot(p.astype(BF16), vh).astype(BF16)
    out_ref[...] = x + _dot(o_ref[...], wo_ref[...])


def _xattn(x1, g, wq, kvmem, wo, S, n_mem, tm):
    T = x1.shape[0]
    per_b = S // tm
    full = lambda i: (0, 0)
    return pl.pallas_call(
        _xattn_kernel,
        grid=(T // tm,),
        in_specs=[
            pl.BlockSpec((tm, D_MODEL), lambda i: (i, 0)),
            pl.BlockSpec((1, D_MODEL), full),
            pl.BlockSpec((D_MODEL, D_MODEL), full),
            pl.BlockSpec((n_mem, 2 * D_MODEL), lambda i: (i // per_b, 0)),
            pl.BlockSpec((D_MODEL, D_MODEL), full),
        ],
        out_specs=pl.BlockSpec((tm, D_MODEL), lambda i: (i, 0)),
        out_shape=jax.ShapeDtypeStruct((T, D_MODEL), F32),
        scratch_shapes=[pltpu.VMEM((tm, D_MODEL), BF16)],
        compiler_params=_cparams(("arbitrary",)),
        name="xattn",
    )(x1, g, wq, kvmem, wo)


def _router_kernel(x_ref, g_ref, wrh_ref, wrl_ref, br_ref, hp_ref, ids_ref, tw_ref, cnt_ref):
    @pl.when(pl.program_id(0) == 0)
    def _():
        cnt_ref[...] = jnp.zeros_like(cnt_ref)

    hn = _rms(x_ref[...], g_ref[...])
    hh, hl = _split_bf16(hn)
    hp_ref[...] = hh
    logits = _dot(hh, wrh_ref[...]) + _dot(hh, wrl_ref[...]) + _dot(hl, wrh_ref[...]) + br_ref[...]
    lane = lax.broadcasted_iota(I32, logits.shape, 1)
    lanef = lane.astype(F32)
    ids = jnp.zeros(logits.shape, F32)
    tw = jnp.zeros(logits.shape, F32)
    onehot = jnp.zeros(logits.shape, F32)
    v0 = None
    den = None
    for kk in range(TOP_K):
        mx = jnp.max(logits, axis=-1, keepdims=True)
        idx = jnp.min(jnp.where(logits == mx, lanef, float(LANES)), axis=-1, keepdims=True)
        sel = lanef == idx
        if kk == 0:
            v0 = mx
        e = jnp.exp(mx - v0)
        den = e if den is None else den + e
        ids = jnp.where(lane == kk, idx, ids)
        tw = jnp.where(lane == kk, e, tw)
        onehot = jnp.where(sel, 1.0, onehot)
        logits = jnp.where(sel, -jnp.inf, logits)
    ids_ref[...] = ids.astype(I32)
    tw_ref[...] = tw / den
    cnt_ref[...] += _ceil_rows(jnp.sum(onehot, axis=0, keepdims=True))


def _router(x2, g, wr_hi, wr_lo, b_r, tm):
    T = x2.shape[0]
    full = lambda i: (0, 0)
    return pl.pallas_call(
        _router_kernel,
        grid=(T // tm,),
        in_specs=[
            pl.BlockSpec((tm, D_MODEL), lambda i: (i, 0)),
            pl.BlockSpec((1, D_MODEL), full),
            pl.BlockSpec((D_MODEL, LANES), full),
            pl.BlockSpec((D_MODEL, LANES), full),
            pl.BlockSpec((1, LANES), full),
        ],
        out_specs=[
            pl.BlockSpec((tm, D_MODEL), lambda i: (i, 0)),
            pl.BlockSpec((tm, LANES), lambda i: (i, 0)),
            pl.BlockSpec((tm, LANES), lambda i: (i, 0)),
            pl.BlockSpec((1, LANES), full),
        ],
        out_shape=[
            jax.ShapeDtypeStruct((T, D_MODEL), BF16),
            jax.ShapeDtypeStruct((T, LANES), I32),
            jax.ShapeDtypeStruct((T, LANES), F32),
            jax.ShapeDtypeStruct((1, LANES), F32),
        ],
        compiler_params=_cparams(("arbitrary",)),
        name="router",
    )(x2, g, wr_hi, wr_lo, b_r)


SEG_WORDS = SUBLANES * LANES


def _slots_kernel(ids_ref, start_ref, ls_ref, lst_ref, seg_ref, run_ref):
    @pl.when(pl.program_id(0) == 0)
    def _():
        run_ref[...] = jnp.zeros_like(run_ref)

    ids = ids_ref[...]
    tm = ids.shape[0]
    lane = lax.broadcasted_iota(I32, ids.shape, 1)
    sels = [lane == ids[:, kk:kk + 1] for kk in range(TOP_K)]
    onehot = jnp.zeros(ids.shape, F32)
    for s in sels:
        onehot = jnp.where(s, 1.0, onehot)
    c8 = _ceil_rows(jnp.sum(onehot, axis=0, keepdims=True))
    er = lax.broadcasted_iota(I32, (LANES, LANES), 0)
    ec = lax.broadcasted_iota(I32, (LANES, LANES), 1)
    before = jnp.where(er < ec, 1.0, 0.0).astype(BF16)
    pieces = jnp.broadcast_to(c8 * (1.0 / SUBLANES), (SUBLANES, LANES)).astype(BF16)
    lo = _dot(pieces, before)[0:1, :] * SUBLANES
    r = lax.broadcasted_iota(I32, (tm, tm), 0)
    c = lax.broadcasted_iota(I32, (tm, tm), 1)
    strict = jnp.where(c < r, 1.0, 0.0).astype(BF16)
    slot = _dot(strict, onehot.astype(BF16)) + lo
    ls = jnp.zeros(ids.shape, F32)
    for kk, s in enumerate(sels):
        pk = jnp.sum(jnp.where(s, slot, 0.0), axis=-1, keepdims=True)
        ls = jnp.where(lane == kk, pk, ls)
    ls_ref[...] = ls
    hi = jnp.floor(ls * (1.0 / 32.0))
    rem = ls - 32.0 * hi
    pick = jnp.where(lax.broadcasted_iota(I32, (SUBLANES, LANES), 0) == lax.broadcasted_iota(I32, (SUBLANES, LANES), 1),
                     1.0, 0.0).astype(BF16)
    lst_ref[...] = 32.0 * _dot_nt(pick, hi.astype(BF16)) + _dot_nt(pick, rem.astype(BF16))
    row = lax.broadcasted_iota(I32, (SUBLANES, LANES), 0)
    off = start_ref[...] + run_ref[...]
    seg = jnp.where(row == 0, c8, jnp.where(row == 1, lo, jnp.where(row == 2, off, 0.0)))
    seg_ref[...] = seg.astype(I32)
    run_ref[...] += c8


def _slots(ids, starts, tm):
    T = ids.shape[0]
    nt = T // tm
    return pl.pallas_call(
        _slots_kernel,
        grid=(nt,),
        in_specs=[
            pl.BlockSpec((tm, LANES), lambda i: (i, 0)),
            pl.BlockSpec((1, LANES), lambda i: (0, 0)),
        ],
        out_specs=[
            pl.BlockSpec((tm, LANES), lambda i: (i, 0)),
            pl.BlockSpec((SUBLANES, tm), lambda i: (0, i)),
            pl.BlockSpec((SUBLANES, LANES), lambda i: (i, 0)),
        ],
        out_shape=[
            jax.ShapeDtypeStruct((T, LANES), F32),
            jax.ShapeDtypeStruct((SUBLANES, T), F32),
            jax.ShapeDtypeStruct((nt * SUBLANES, LANES), I32),
        ],
        scratch_shapes=[pltpu.VMEM((1, LANES), F32)],
        compiler_params=_cparams(("arbitrary",)),
        name="slots",
    )(ids, starts)


def _local_rows(tm):
    need = tm * TOP_K + N_EXPERTS * (SUBLANES - 1)
    return ((need + LANES - 1) // LANES) * LANES


def _segment_copies(seg_smem, make_copy, act):
    def expert(e, carry):
        n_piece = seg_smem[e] // SUBLANES
        lo = seg_smem[LANES + e]
        off = seg_smem[2 * LANES + e]

        def piece(j, carry2):
            act(make_copy(pl.multiple_of(lo + j * SUBLANES, SUBLANES), pl.multiple_of(off + j * SUBLANES, SUBLANES)))
            return carry2

        lax.fori_loop(0, n_piece, piece, 0)
        return carry

    lax.fori_loop(0, N_EXPERTS, expert, 0)


def _dispatch_kernel(h_ref, lst_ref, seg_hbm, xs_in, xs_ref, sbuf_ref, seg_smem, sem_seg, sem_rows, *, tm):
    del xs_in
    i = pl.program_id(0)
    rows = sbuf_ref.shape[0]
    cp = pltpu.make_async_copy(seg_hbm.at[i], seg_smem, sem_seg)
    cp.start()
    lst = lst_ref[...].astype(I32)
    rid = lax.broadcasted_iota(I32, (rows, tm), 0)
    perm = jnp.zeros((rows, tm), F32)
    for kk in range(TOP_K):
        perm = perm + jnp.where(rid == lst[kk:kk + 1, :], 1.0, 0.0)
    srt = _dot(perm.astype(BF16), h_ref[...])
    sbuf_ref[...] = _pack_bf16_pairs(srt)
    cp.wait()

    def make_copy(lo, off):
        return pltpu.make_async_copy(sbuf_ref.at[pl.ds(lo, SUBLANES), :], xs_ref.at[pl.ds(off, SUBLANES), :], sem_rows)

    _segment_copies(seg_smem, make_copy, lambda c: c.start())
    _segment_copies(seg_smem, make_copy, lambda c: c.wait())


def _dispatch(h, lst, seg2d, xs_zero, tm):
    T = h.shape[0]
    rows = _local_rows(tm)
    return pl.pallas_call(
        functools.partial(_dispatch_kernel, tm=tm),
        grid=(T // tm,),
        in_specs=[
            pl.BlockSpec((tm, D_MODEL), lambda i: (i, 0)),
            pl.BlockSpec((SUBLANES, tm), lambda i: (0, i)),
            pl.BlockSpec(memory_space=pl.ANY),
            pl.BlockSpec(memory_space=pl.ANY),
        ],
        out_specs=pl.BlockSpec(memory_space=pl.ANY),
        out_shape=jax.ShapeDtypeStruct(xs_zero.shape, U32),
        scratch_shapes=[
            pltpu.VMEM((rows, D_MODEL // 2), U32),
            pltpu.SMEM((SEG_WORDS,), I32),
            pltpu.SemaphoreType.DMA,
            pltpu.SemaphoreType.DMA,
        ],
        input_output_aliases={3: 0},
        compiler_params=_cparams(("arbitrary",)),
        name="dispatch",
    )(h, lst, seg2d, xs_zero)


def _experts_kernel(te_ref, nu_ref, xs_ref, wgu_ref, bgu_ref, wdn_ref, bdn_ref, ys_ref, wgu_bf, wdn_bf):
    i = pl.program_id(0)
    half = D_MODEL // 2

    @pl.when((i == 0) | (te_ref[i] != te_ref[jnp.maximum(i - 1, 0)]))
    def _():
        wgu_bf[...] = wgu_ref[0].astype(BF16)
        wdn_bf[...] = wdn_ref[0].astype(BF16)

    @pl.when(i < nu_ref[0])
    def _():
        lo, hi = _unpack_bf16_pairs(xs_ref[...])
        gu = (_dot(lo.astype(BF16), wgu_bf[:half, :]) + _dot(hi.astype(BF16), wgu_bf[half:, :])
              + bgu_ref[0])
        gate = jnp.minimum(gu[:, :D_FF], SWIGLU_LIMIT)
        up = jnp.clip(gu[:, D_FF:], -SWIGLU_LIMIT, SWIGLU_LIMIT)
        act = (up + 1.0) * (gate * _sigmoid(SWIGLU_ALPHA * gate))
        y = _dot(act.astype(BF16), wdn_bf[...]) + bdn_ref[0]
        ys_ref[...] = _pack_bf16_pairs(y)

    @pl.when(i >= nu_ref[0])
    def _():
        ys_ref[...] = jnp.zeros_like(ys_ref)


def _experts(tile_expert, n_used, xs, w_gu, b_gu, w_dn, b_dn, tg):
    P = xs.shape[0]
    half = D_MODEL // 2
    grid_spec = pltpu.PrefetchScalarGridSpec(
        num_scalar_prefetch=2,
        grid=(P // tg,),
        in_specs=[
            pl.BlockSpec((tg, half), lambda i, te, nu: (jnp.minimum(i, jnp.maximum(nu[0] - 1, 0)), 0)),
            pl.BlockSpec((1, D_MODEL, 2 * D_FF), lambda i, te, nu: (te[i], 0, 0)),
            pl.BlockSpec((1, 1, 2 * D_FF), lambda i, te, nu: (te[i], 0, 0)),
            pl.BlockSpec((1, D_FF, D_MODEL), lambda i, te, nu: (te[i], 0, 0)),
            pl.BlockSpec((1, 1, D_MODEL), lambda i, te, nu: (te[i], 0, 0)),
        ],
        out_specs=pl.BlockSpec((tg, half), lambda i, te, nu: (i, 0)),
        scratch_shapes=[pltpu.VMEM((D_MODEL, 2 * D_FF), BF16), pltpu.VMEM((D_FF, D_MODEL), BF16)],
    )
    return pl.pallas_call(
        _experts_kernel,
        grid_spec=grid_spec,
        out_shape=jax.ShapeDtypeStruct((P, half), U32),
        compiler_params=_cparams(("arbitrary",)),
        name="experts",
    )(tile_expert, n_used, xs, w_gu, b_gu, w_dn, b_dn)


def _combine_kernel(seg_hbm, ys_hbm, ls_ref, tw_ref, x_ref, g_ref, out_ref, ybuf_ref, seg_smem, sem_seg, sem_rows, *,
                    tm, final_norm):
    i = pl.program_id(0)
    rows = ybuf_ref.shape[0]

    @pl.when(i == 0)
    def _():
        ybuf_ref[...] = jnp.zeros_like(ybuf_ref)

    cp = pltpu.make_async_copy(seg_hbm.at[i], seg_smem, sem_seg)
    cp.start()
    cp.wait()

    def make_copy(lo, off):
        return pltpu.make_async_copy(ys_hbm.at[pl.ds(off, SUBLANES), :], ybuf_ref.at[pl.ds(lo, SUBLANES), :], sem_rows)

    _segment_copies(seg_smem, make_copy, lambda c: c.start())
    ls = ls_ref[...].astype(I32)
    tw = tw_ref[...]
    cid = lax.broadcasted_iota(I32, (tm, rows), 1)
    wmat = jnp.zeros((tm, rows), F32)
    for kk in range(TOP_K):
        wmat = wmat + jnp.where(cid == ls[:, kk:kk + 1], tw[:, kk:kk + 1], 0.0)
    wh, wl = _split_bf16(wmat)
    _segment_copies(seg_smem, make_copy, lambda c: c.wait())
    lo, hi = _unpack_bf16_pairs(ybuf_ref[...])
    lo = lo.astype(BF16)
    hi = hi.astype(BF16)
    moe = jnp.concatenate([_dot(wh, lo) + _dot(wl, lo), _dot(wh, hi) + _dot(wl, hi)], axis=1)
    x3 = x_ref[...] + moe
    out_ref[...] = _rms(x3, g_ref[...]) if final_norm else x3


def _combine(seg2d, ys, ls, tw, x2, g, tm, final_norm):
    T = x2.shape[0]
    rows = _local_rows(tm)
    return pl.pallas_call(
        functools.partial(_combine_kernel, tm=tm, final_norm=final_norm),
        grid=(T // tm,),
        in_specs=[
            pl.BlockSpec(memory_space=pl.ANY),
            pl.BlockSpec(memory_space=pl.ANY),
            pl.BlockSpec((tm, LANES), lambda i: (i, 0)),
            pl.BlockSpec((tm, LANES), lambda i: (i, 0)),
            pl.BlockSpec((tm, D_MODEL), lambda i: (i, 0)),
            pl.BlockSpec((1, D_MODEL), lambda i: (0, 0)),
        ],
        out_specs=pl.BlockSpec((tm, D_MODEL), lambda i: (i, 0)),
        out_shape=jax.ShapeDtypeStruct((T, D_MODEL), F32),
        scratch_shapes=[
            pltpu.VMEM((rows, D_MODEL // 2), U32),
            pltpu.SMEM((SEG_WORDS,), I32),
            pltpu.SemaphoreType.DMA,
            pltpu.SemaphoreType.DMA,
        ],
        compiler_params=_cparams(("arbitrary",)),
        name="combine",
    )(seg2d, ys, ls, tw, x2, g)


def _tiles(B, S):
    T = B * S
    return dict(
        tm_in=min(1024, T), tn_in=1024,
        ts=min(512, S),
        tq=min(512, S),
        tm_proj=min(512, S),
        tm_route=min(256, T),
        tg=512,
    )


def _pad_lanes(a, n=LANES, value=0.0):
    return jnp.pad(a, ((0, 0), (0, n - a.shape[1])), constant_values=value)


def kernel(x, mem, norm_mix, w_in, conv_w, b_if, mlstm_gain, diff_lambda, diff_gain, w_branch_m, w_branch_d,
           b_gate, w_out, norm_xattn, norm_mem, wq_x, wkv_x, wo_x, norm_ffn, w_router, b_router, w_gu, b_gu,
           w_dn, b_dn, norm_final):
    B, S, D = x.shape
    n_mem = mem.shape[1]
    T = B * S
    depth = norm_mix.shape[0]
    tl = _tiles(B, S)
    x2d = x.reshape(T, D)
    mem2d = mem.reshape(B * n_mem, D)

    for l in range(depth):
        lam_init = 0.8 - 0.6 * math.exp(-0.3 * l)
        wl = w_in[l]
        if_lo = 2 * M_QK + 2 * M_V
        w_main = jnp.concatenate([wl[:, :if_lo], wl[:, if_lo + 2 * M_HEADS:]], axis=1).astype(BF16)
        w_if = wl[:, if_lo:if_lo + 2 * M_HEADS]
        w_ifp = _pad_lanes(w_if).astype(BF16)
        w_ift = w_if.T.astype(BF16)
        bif = _pad_lanes(b_if[l][None, :])
        bift = jnp.broadcast_to(b_if[l][:, None], (SUBLANES, LANES))

        z, zif, zift = _inproj(x2d, norm_mix[l][None, :], w_main, w_ifp, w_ift, tl["tm_in"], tl["tn_in"])
        hm = _mlstm(z, zif, zift, conv_w[l], bif, bift, mlstm_gain[l].reshape(1, M_V), B, S, tl["ts"])
        hd = _diffattn(z, diff_lambda[l], diff_gain[l][None, :], B, S, tl["tq"], lam_init)
        x1 = _merge(hm, hd, z, x2d, w_branch_m[l].astype(BF16), w_branch_d[l].astype(BF16),
                    w_out[l].astype(BF16), b_gate[l][None, :], tl["tm_proj"])

        kvmem = _memkv(mem2d, norm_mem[l][None, :], wkv_x[l].astype(BF16), n_mem)
        x2 = _xattn(x1, norm_xattn[l][None, :], wq_x[l].astype(BF16), kvmem, wo_x[l].astype(BF16),
                    S, n_mem, tl["tm_proj"])

        wr = _pad_lanes(w_router[l])
        wr_hi = wr.astype(BF16)
        wr_lo = (wr - wr_hi.astype(F32)).astype(BF16)
        br = _pad_lanes(b_router[l][None, :], value=-jnp.inf)
        tm_r = tl["tm_route"]
        tg = tl["tg"]
        hp, ids, tw, cnt = _router(x2, norm_ffn[l][None, :], wr_hi, wr_lo, br, tm_r)

        counts = cnt[0, :N_EXPERTS].astype(I32)
        padded = ((counts + tg - 1) // tg) * tg
        ends = jnp.cumsum(padded)
        starts = ends - padded
        max_rows = T * TOP_K + (T // tm_r) * N_EXPERTS * (SUBLANES - 1)
        n_tiles = -(-max_rows // tg) + N_EXPERTS
        tile_row0 = jnp.arange(n_tiles, dtype=I32) * tg
        tile_expert = jnp.minimum(jnp.sum((tile_row0[:, None] >= ends[None, :]).astype(I32), axis=1), N_EXPERTS - 1)
        n_used = (ends[-1] // tg).astype(I32).reshape(1)
        last_used = tile_expert[jnp.maximum(n_used[0] - 1, 0)]
        tile_expert = jnp.where(tile_row0 < ends[-1], tile_expert, last_used)

        ls, lst, seg = _slots(ids, _pad_lanes(starts.astype(F32)[None, :]), tm_r)
        seg2d = seg.reshape(T // tm_r, SEG_WORDS)

        xs = _dispatch(hp, lst, seg2d, jnp.zeros((n_tiles * tg, D // 2), U32), tm_r)
        ys = _experts(tile_expert, n_used, xs, w_gu[l], b_gu[l][:, None, :], w_dn[l], b_dn[l][:, None, :], tg)
        x2d = _combine(seg2d, ys, ls, tw, x2, norm_final[None, :], tm_r, final_norm=(l == depth - 1))
    return x2d.reshape(B, S, D)
```

```python
import functools
import math

import jax
import jax.numpy as jnp
from jax import lax
from jax.experimental import pallas as pl
from jax.experimental.pallas import tpu as pltpu

F32 = jnp.float32
BF16 = jnp.bfloat16
U32 = jnp.uint32
I32 = jnp.int32

EPS = 1e-6
CHUNK = 64
D_MODEL = 1024
M_HEADS = 4
M_DK = 128
M_DV = 256
M_QK = M_HEADS * M_DK
M_V = M_HEADS * M_DV
CONV_W = 4
D_HEADS = 8
D_DH = 64
D_QK = D_HEADS * 2 * D_DH
D_V = D_HEADS * 2 * D_DH
X_HEADS = 4
X_DH = D_MODEL // X_HEADS
N_EXPERTS = 32
TOP_K = 4
D_FF = D_MODEL
SWIGLU_LIMIT = 7.0
SWIGLU_ALPHA = 1.702

LANES = 128
SUBLANES = 8
N_MAIN = 2 * M_QK + 2 * M_V + 2 * D_QK + D_V + 2 * D_MODEL
OFF_QM, OFF_KM, OFF_VM, OFF_OM = 0, M_QK, 2 * M_QK, 2 * M_QK + M_V
OFF_QD = OFF_OM + M_V
OFF_KD = OFF_QD + D_QK
OFF_VD = OFF_KD + D_QK
OFF_G = OFF_VD + D_V

VMEM_LIMIT = 56 * 1024 * 1024
DMA_UNROLL = 4


def _cparams(sem, vmem=VMEM_LIMIT):
    return pltpu.CompilerParams(dimension_semantics=sem, vmem_limit_bytes=vmem)


def _rms(x, g):
    return x * lax.rsqrt(jnp.mean(x * x, axis=-1, keepdims=True) + EPS) * g


def _split_bf16(x):
    hi = x.astype(BF16)
    lo = (x - hi.astype(F32)).astype(BF16)
    return hi, lo


def _dot(a, b):
    return jnp.dot(a, b, preferred_element_type=F32)


def _dot_nt(a, b):
    return lax.dot_general(a, b, (((1,), (1,)), ((), ())), preferred_element_type=F32)


def _sigmoid(x):
    return 1.0 / (1.0 + jnp.exp(-x))


def _log_sigmoid(x):
    return jnp.minimum(x, 0.0) - jnp.log(1.0 + jnp.exp(-jnp.abs(x)))


def _pack_bf16_pairs(x):
    w = x.shape[1] // 2
    u = lax.bitcast_convert_type(x, U32)
    r = (u + jnp.uint32(0x7FFF) + ((u >> 16) & jnp.uint32(1))) >> 16
    return r[:, :w] | (r[:, w:] << 16)


def _ceil_rows(x):
    return jnp.floor((x + (SUBLANES - 1)) * (1.0 / SUBLANES)) * SUBLANES


def _unpack_bf16_pairs(p):
    lo = lax.bitcast_convert_type(p << 16, F32)
    hi = lax.bitcast_convert_type(p & jnp.uint32(0xFFFF0000), F32)
    return lo, hi


def _inproj_kernel(x_ref, g_ref, w_ref, wif_ref, wift_ref, z_ref, zif_ref, zift_ref, hn_ref):
    @pl.when(pl.program_id(1) == 0)
    def _():
        hn = _rms(x_ref[...], g_ref[...]).astype(BF16)
        hn_ref[...] = hn
        zif_ref[...] = _dot(hn, wif_ref[...])
        zift_ref[...] = _dot_nt(wift_ref[...], hn)

    z_ref[...] = _dot(hn_ref[...], w_ref[...]).astype(BF16)


def _inproj(x2d, g, w_main, w_if, w_ift, tm, tn):
    T = x2d.shape[0]
    return pl.pallas_call(
        _inproj_kernel,
        grid=(T // tm, N_MAIN // tn),
        in_specs=[
            pl.BlockSpec((tm, D_MODEL), lambda i, j: (i, 0)),
            pl.BlockSpec((1, D_MODEL), lambda i, j: (0, 0)),
            pl.BlockSpec((D_MODEL, tn), lambda i, j: (0, j)),
            pl.BlockSpec((D_MODEL, LANES), lambda i, j: (0, 0)),
            pl.BlockSpec((SUBLANES, D_MODEL), lambda i, j: (0, 0)),
        ],
        out_specs=[
            pl.BlockSpec((tm, tn), lambda i, j: (i, j)),
            pl.BlockSpec((tm, LANES), lambda i, j: (i, 0)),
            pl.BlockSpec((SUBLANES, tm), lambda i, j: (0, i)),
        ],
        out_shape=[
            jax.ShapeDtypeStruct((T, N_MAIN), BF16),
            jax.ShapeDtypeStruct((T, LANES), F32),
            jax.ShapeDtypeStruct((SUBLANES, T), F32),
        ],
        scratch_shapes=[pltpu.VMEM((tm, D_MODEL), BF16)],
        compiler_params=_cparams(("arbitrary", "arbitrary")),
        name="inproj",
    )(x2d, g, w_main, w_if, w_ift)


def _mlstm_kernel(q_ref, k_ref, v_ref, om_ref, zif_ref, zift_ref, cw_ref, bif_ref, bift_ref, mg_ref,
                  out_ref, qc_ref, kc_ref, kt_ref, carry_ref, c_ref, n_ref, m_ref,
                  bd_ref, bdt_ref, brep_ref, grow_ref, brow_ref, *, ts):
    nchunk = ts // CHUNK
    L = CHUNK

    @pl.when(pl.program_id(1) == 0)
    def _():
        carry_ref[...] = jnp.zeros_like(carry_ref)
        c_ref[...] = jnp.zeros_like(c_ref)
        n_ref[...] = jnp.zeros_like(n_ref)
        m_ref[...] = jnp.zeros_like(m_ref)
        rt = lax.broadcasted_iota(I32, (ts, ts), 0)
        ct = lax.broadcasted_iota(I32, (ts, ts), 1)
        same = (rt // L) == (ct // L)
        bd_ref[...] = jnp.where(same, jnp.where(ct <= rt, 1.0, 0.0), 0.0).astype(BF16)
        bdt_ref[...] = jnp.where(same, jnp.where(rt <= ct, 1.0, 0.0), 0.0).astype(BF16)

    row8 = lax.broadcasted_iota(I32, (SUBLANES, M_QK), 0)

    def conv_silu(x, prev8, w):
        acc = w[CONV_W - 1:CONV_W, :] * x
        for s in range(1, CONV_W):
            xs = pltpu.roll(x, s, 0)
            top = jnp.where(row8 < s, pltpu.roll(prev8, s, 0), xs[0:SUBLANES])
            xs = jnp.concatenate([top, xs[SUBLANES:]], axis=0)
            acc = acc + w[CONV_W - 1 - s:CONV_W - s, :] * xs
        return acc * _sigmoid(acc)

    def conv_body(c, carry):
        r0 = pl.multiple_of(c * L, L)
        xq = q_ref[pl.ds(r0, L), :].astype(F32)
        xk = k_ref[pl.ds(r0, L), :].astype(F32)
        yq = conv_silu(xq, carry_ref[:, 0:M_QK], cw_ref[:, 0:M_QK]) * (M_DK ** -0.5)
        yk = conv_silu(xk, carry_ref[:, M_QK:2 * M_QK], cw_ref[:, M_QK:2 * M_QK])
        qc_ref[pl.ds(r0, L), :] = yq.astype(BF16)
        kc_ref[pl.ds(r0, L), :] = yk.astype(BF16)
        for h in range(M_HEADS):
            kt_ref[c, h] = yk[:, h * M_DK:(h + 1) * M_DK].T
        carry_ref[:, 0:M_QK] = xq[L - SUBLANES:L]
        carry_ref[:, M_QK:2 * M_QK] = xk[L - SUBLANES:L]
        return carry

    lax.fori_loop(0, nchunk, conv_body, 0)

    ti = lax.broadcasted_iota(I32, (L, L), 0)
    si = lax.broadcasted_iota(I32, (L, L), 1)
    causal = si <= ti
    lane_row = lax.broadcasted_iota(I32, (LANES, LANES), 0)
    ones_l = jnp.ones((L, LANES), BF16)

    lf_col = _log_sigmoid(zif_ref[...] + bif_ref[...])
    ch, cl = _split_bf16(lf_col)
    b_col_all = _dot(bd_ref[...], ch) + _dot(bd_ref[...], cl)
    bh, bl = _split_bf16(b_col_all)
    for h in range(M_HEADS):
        sel_f = jnp.where(lane_row == M_HEADS + h, 1.0, 0.0).astype(BF16)
        brep_ref[h] = _dot(bh, sel_f) + _dot(bl, sel_f)
    g_row_all = zift_ref[...] + bift_ref[:, 0:1]
    rh, rl = _split_bf16(_log_sigmoid(g_row_all))
    b_row_tile = _dot(rh, bdt_ref[...]) + _dot(rl, bdt_ref[...])
    for cc in range(nchunk):
        grow_ref[cc] = g_row_all[:, cc * L:(cc + 1) * L]
        brow_ref[cc] = b_row_tile[:, cc * L:(cc + 1) * L]

    def chunk_body(c, carry):
        r0 = pl.multiple_of(c * L, L)
        g_row = grow_ref[c]
        b_row_all = brow_ref[c]
        for h in range(M_HEADS):
            b_rep = brep_ref[h, pl.ds(r0, L), :]
            i_row = g_row[h:h + 1, :]
            b_row = b_row_all[M_HEADS + h:M_HEADS + h + 1, :]
            b_last = b_rep[L - 1:L, :]
            q = qc_ref[pl.ds(r0, L), h * M_DK:(h + 1) * M_DK]
            k = kc_ref[pl.ds(r0, L), h * M_DK:(h + 1) * M_DK]
            vext = jnp.concatenate([v_ref[pl.ds(r0, L), h * M_DV:(h + 1) * M_DV], ones_l], axis=1)
            dm = jnp.where(causal, b_rep[:, :L] - b_row + i_row, -jnp.inf)
            m_loc = jnp.max(dm, axis=-1, keepdims=True)
            s_loc = _dot_nt(q, k) * jnp.exp(dm - m_loc)
            pv = _dot(s_loc.astype(BF16), vext)
            gk_row = b_last[:, :L] - b_row + i_row
            g_max = jnp.max(gk_row, axis=-1, keepdims=True)
            kwt = (kt_ref[c, h] * jnp.exp(gk_row - g_max)).astype(BF16)
            kv = _dot(kwt, vext)
            m_prev = m_ref[h:h + 1, :]
            c_old = c_ref[h]
            n_old = n_ref[h]
            qcn = _dot(q, jnp.concatenate([c_old, n_old], axis=1).astype(BF16))
            inter = b_rep + m_prev
            m_t = jnp.maximum(inter, m_loc)
            w_inter = jnp.exp(inter - m_t)
            r_loc = jnp.exp(m_loc - m_t)
            den = r_loc * pv[:, M_DV:] + w_inter * qcn[:, M_DV:]
            inv = 1.0 / jnp.maximum(jnp.abs(den), jnp.exp(-m_t))
            hv = (jnp.concatenate([r_loc * inv] * 2, axis=1) * pv[:, :M_DV]
                  + jnp.concatenate([w_inter * inv] * 2, axis=1) * qcn[:, :M_DV])
            m_new = jnp.maximum(b_last + m_prev, g_max)
            decay = jnp.exp(b_last + m_prev - m_new)
            sc_loc = jnp.exp(g_max - m_new)
            c_ref[h] = (jnp.concatenate([decay] * 2, axis=1) * c_old
                        + jnp.concatenate([sc_loc] * 2, axis=1) * kv[:, :M_DV])
            n_ref[h] = decay * n_old + sc_loc * kv[:, M_DV:]
            m_ref[h:h + 1, :] = m_new
            hn = _rms(hv, mg_ref[:, h * M_DV:(h + 1) * M_DV])
            og = _sigmoid(om_ref[pl.ds(r0, L), h * M_DV:(h + 1) * M_DV].astype(F32))
            out_ref[pl.ds(r0, L), h * M_DV:(h + 1) * M_DV] = (og * hn).astype(BF16)
        return carry

    lax.fori_loop(0, nchunk, chunk_body, 0, unroll=4)


def _mlstm(z, zif, zift, conv_w, bif, bift, m_gain, B, S, ts):
    T = B * S
    nt = S // ts
    nck = ts // CHUNK
    row = lambda b, t: b * nt + t
    return pl.pallas_call(
        functools.partial(_mlstm_kernel, ts=ts),
        grid=(B, nt),
        in_specs=[
            pl.BlockSpec((ts, M_QK), lambda b, t: (row(b, t), OFF_QM // M_QK)),
            pl.BlockSpec((ts, M_QK), lambda b, t: (row(b, t), OFF_KM // M_QK)),
            pl.BlockSpec((ts, M_V), lambda b, t: (row(b, t), OFF_VM // M_V)),
            pl.BlockSpec((ts, M_V), lambda b, t: (row(b, t), OFF_OM // M_V)),
            pl.BlockSpec((ts, LANES), lambda b, t: (row(b, t), 0)),
            pl.BlockSpec((SUBLANES, ts), lambda b, t: (0, row(b, t))),
            pl.BlockSpec((CONV_W, 2 * M_QK), lambda b, t: (0, 0)),
            pl.BlockSpec((1, LANES), lambda b, t: (0, 0)),
            pl.BlockSpec((SUBLANES, LANES), lambda b, t: (0, 0)),
            pl.BlockSpec((1, M_V), lambda b, t: (0, 0)),
        ],
        out_specs=pl.BlockSpec((ts, M_V), lambda b, t: (row(b, t), 0)),
        out_shape=jax.ShapeDtypeStruct((T, M_V), BF16),
        scratch_shapes=[
            pltpu.VMEM((ts, M_QK), BF16),
            pltpu.VMEM((ts, M_QK), BF16),
            pltpu.VMEM((nck, M_HEADS, M_DK, CHUNK), F32),
            pltpu.VMEM((SUBLANES, 2 * M_QK), F32),
            pltpu.VMEM((M_HEADS, M_DK, M_DV), F32),
            pltpu.VMEM((M_HEADS, M_DK, LANES), F32),
            pltpu.VMEM((SUBLANES, LANES), F32),
            pltpu.VMEM((ts, ts), BF16),
            pltpu.VMEM((ts, ts), BF16),
            pltpu.VMEM((M_HEADS, ts, LANES), F32),
            pltpu.VMEM((nck, SUBLANES, CHUNK), F32),
            pltpu.VMEM((nck, SUBLANES, CHUNK), F32),
        ],
        compiler_params=_cparams(("arbitrary", "arbitrary")),
        name="mlstm",
    )(z, z, z, z, zif, zift, conv_w, bif, bift, m_gain)


def _diffattn_kernel(q_ref, k_ref, v_ref, lam_ref, gain_ref, out_ref,
                     m1_ref, a1_ref, m2_ref, a2_ref, *, tq, lam_init):
    qi = pl.program_id(2)
    w = 2 * D_DH
    q = q_ref[...]
    lane = lax.broadcasted_iota(I32, (1, w), 1)
    scale = jnp.asarray(D_DH ** -0.5, BF16)
    q1 = jnp.where(lane < D_DH, q, jnp.zeros_like(q)) * scale
    q2 = jnp.where(lane >= D_DH, q, jnp.zeros_like(q)) * scale
    ones = jnp.ones((tq, w), BF16)

    m1_ref[...] = jnp.full_like(m1_ref, -jnp.inf)
    m2_ref[...] = jnp.full_like(m2_ref, -jnp.inf)
    a1_ref[...] = jnp.zeros_like(a1_ref)
    a2_ref[...] = jnp.zeros_like(a2_ref)

    def online(s, vext, m_ref, a_ref):
        m_old = m_ref[...]
        m_new = jnp.maximum(m_old, jnp.max(s, axis=-1, keepdims=True))
        p = jnp.exp(s - jnp.concatenate([m_new] * (tq // w), axis=1))
        alpha = jnp.exp(m_old - m_new)
        a_ref[...] = jnp.concatenate([alpha, alpha], axis=1) * a_ref[...] + _dot(p.astype(BF16), vext)
        m_ref[...] = m_new

    def block(j, mask):
        r0 = pl.multiple_of(j * tq, tq)
        k = k_ref[pl.ds(r0, tq), :]
        vext = jnp.concatenate([v_ref[pl.ds(r0, tq), :], ones], axis=1)
        s1 = _dot_nt(q1, k)
        s2 = _dot_nt(q2, k)
        if mask is not None:
            s1 = jnp.where(mask, s1, -jnp.inf)
            s2 = jnp.where(mask, s2, -jnp.inf)
        online(s1, vext, m1_ref, a1_ref)
        online(s2, vext, m2_ref, a2_ref)

    def body(jj, carry):
        block(2 * jj, None)
        block(2 * jj + 1, None)
        return carry

    lax.fori_loop(0, qi // 2, body, 0)

    @pl.when(qi % 2 == 1)
    def _():
        block(qi - 1, None)

    rq = lax.broadcasted_iota(I32, (tq, tq), 0) // CHUNK
    ck = lax.broadcasted_iota(I32, (tq, tq), 1) // CHUNK
    block(qi, ck <= rq)

    lp = lam_ref[...]
    lam = (jnp.exp(jnp.sum(lp[0:1, :] * lp[1:2, :], axis=-1, keepdims=True))
           - jnp.exp(jnp.sum(lp[2:3, :] * lp[3:4, :], axis=-1, keepdims=True)) + lam_init)
    a1 = a1_ref[...]
    a2 = a2_ref[...]
    o = a1[:, :w] / a1[:, w:] - lam * (a2[:, :w] / a2[:, w:])
    out_ref[...] = (_rms(o, gain_ref[...]) * (1.0 - lam_init)).astype(BF16)


def _diffattn(z, lam_p, d_gain, B, S, tq, lam_init):
    T = B * S
    nq = S // tq
    w = 2 * D_DH
    return pl.pallas_call(
        functools.partial(_diffattn_kernel, tq=tq, lam_init=lam_init),
        grid=(B, D_HEADS, nq),
        in_specs=[
            pl.BlockSpec((tq, w), lambda b, h, i: (b * nq + i, OFF_QD // w + h)),
            pl.BlockSpec((S, w), lambda b, h, i: (b, OFF_KD // w + h)),
            pl.BlockSpec((S, w), lambda b, h, i: (b, OFF_VD // w + h)),
            pl.BlockSpec((4, D_DH), lambda b, h, i: (0, 0)),
            pl.BlockSpec((1, w), lambda b, h, i: (0, 0)),
        ],
        out_specs=pl.BlockSpec((tq, w), lambda b, h, i: (b * nq + i, h)),
        out_shape=jax.ShapeDtypeStruct((T, D_V), BF16),
        scratch_shapes=[
            pltpu.VMEM((tq, w), F32), pltpu.VMEM((tq, 2 * w), F32),
            pltpu.VMEM((tq, w), F32), pltpu.VMEM((tq, 2 * w), F32),
        ],
        compiler_params=_cparams(("arbitrary", "arbitrary", "arbitrary")),
        name="diffattn",
    )(z, z, z, lam_p, d_gain)


def _merge_kernel(hm_ref, hd_ref, gz_ref, x_ref, wbm_ref, wbd_ref, wout_ref, bg_ref, out_ref):
    bm = _dot(hm_ref[...], wbm_ref[...])
    bd = _dot(hd_ref[...], wbd_ref[...])
    g = _sigmoid(gz_ref[...].astype(F32) + bg_ref[...])
    merged = g[:, :D_MODEL] * bm + g[:, D_MODEL:] * bd
    out_ref[...] = x_ref[...] + _dot(merged.astype(BF16), wout_ref[...])


def _merge(hm, hd, z, x2d, w_bm, w_bd, w_out, b_gate, tm):
    T = x2d.shape[0]
    full = lambda i: (0, 0)
    return pl.pallas_call(
        _merge_kernel,
        grid=(T // tm,),
        in_specs=[
            pl.BlockSpec((tm, M_V), lambda i: (i, 0)),
            pl.BlockSpec((tm, D_V), lambda i: (i, 0)),
            pl.BlockSpec((tm, 2 * D_MODEL), lambda i: (i, OFF_G // (2 * D_MODEL))),
            pl.BlockSpec((tm, D_MODEL), lambda i: (i, 0)),
            pl.BlockSpec((M_V, D_MODEL), full),
            pl.BlockSpec((D_V, D_MODEL), full),
            pl.BlockSpec((D_MODEL, D_MODEL), full),
            pl.BlockSpec((1, 2 * D_MODEL), full),
        ],
        out_specs=pl.BlockSpec((tm, D_MODEL), lambda i: (i, 0)),
        out_shape=jax.ShapeDtypeStruct((T, D_MODEL), F32),
        compiler_params=_cparams(("arbitrary",)),
        name="merge",
    )(hm, hd, z, x2d, w_bm, w_bd, w_out, b_gate)


def _memkv_kernel(mem_ref, g_ref, w_ref, out_ref):
    out_ref[...] = _dot(_rms(mem_ref[...], g_ref[...]).astype(BF16), w_ref[...]).astype(BF16)


def _memkv(mem2d, g, wkv, n_mem):
    R = mem2d.shape[0]
    return pl.pallas_call(
        _memkv_kernel,
        grid=(R // n_mem,),
        in_specs=[
            pl.BlockSpec((n_mem, D_MODEL), lambda i: (i, 0)),
            pl.BlockSpec((1, D_MODEL), lambda i: (0, 0)),
            pl.BlockSpec((D_MODEL, 2 * D_MODEL), lambda i: (0, 0)),
        ],
        out_specs=pl.BlockSpec((n_mem, 2 * D_MODEL), lambda i: (i, 0)),
        out_shape=jax.ShapeDtypeStruct((R, 2 * D_MODEL), BF16),
        compiler_params=_cparams(("arbitrary",)),
        name="memkv",
    )(mem2d, g, wkv)


def _xattn_kernel(x_ref, g_ref, wq_ref, kv_ref, wo_ref, out_ref, o_ref):
    x = x_ref[...]
    h = _rms(x, g_ref[...]).astype(BF16)
    q = (_dot(h, wq_ref[...]) * (X_DH ** -0.5)).astype(BF16)
    for hd in range(X_HEADS):
        qh = q[:, hd * X_DH:(hd + 1) * X_DH]
        kh = kv_ref[:, hd * X_DH:(hd + 1) * X_DH]
        vh = kv_ref[:, D_MODEL + hd * X_DH:D_MODEL + (hd + 1) * X_DH]
        s = _dot_nt(qh, kh)
        p = jnp.exp(s - jnp.max(s, axis=-1, keepdims=True))
        p = p / jnp.sum(p, axis=-1, keepdims=True)
        o_ref[:, hd * X_DH:(hd + 1) * X_DH] = _dot(p.astype(BF16), vh).astype(BF16)
    out_ref[...] = x + _dot(o_ref[...], wo_ref[...])


def _xattn(x1, g, wq, kvmem, wo, S, n_mem, tm):
    T = x1.shape[0]
    per_b = S // tm
    full = lambda i: (0, 0)
    return pl.pallas_call(
        _xattn_kernel,
        grid=(T // tm,),
        in_specs=[
            pl.BlockSpec((tm, D_MODEL), lambda i: (i, 0)),
            pl.BlockSpec((1, D_MODEL), full),
            pl.BlockSpec((D_MODEL, D_MODEL), full),
            pl.BlockSpec((n_mem, 2 * D_MODEL), lambda i: (i // per_b, 0)),
            pl.BlockSpec((D_MODEL, D_MODEL), full),
        ],
        out_specs=pl.BlockSpec((tm, D_MODEL), lambda i: (i, 0)),
        out_shape=jax.ShapeDtypeStruct((T, D_MODEL), F32),
        scratch_shapes=[pltpu.VMEM((tm, D_MODEL), BF16)],
        compiler_params=_cparams(("arbitrary",)),
        name="xattn",
    )(x1, g, wq, kvmem, wo)


def _router_kernel(x_ref, g_ref, wrh_ref, wrl_ref, br_ref, hp_ref, ids_ref, tw_ref, cnt_ref):
    @pl.when(pl.program_id(0) == 0)
    def _():
        cnt_ref[...] = jnp.zeros_like(cnt_ref)

    hn = _rms(x_ref[...], g_ref[...])
    hh, hl = _split_bf16(hn)
    hp_ref[...] = hh
    logits = _dot(hh, wrh_ref[...]) + _dot(hh, wrl_ref[...]) + _dot(hl, wrh_ref[...]) + br_ref[...]
    lane = lax.broadcasted_iota(I32, logits.shape, 1)
    lanef = lane.astype(F32)
    ids = jnp.zeros(logits.shape, F32)
    tw = jnp.zeros(logits.shape, F32)
    onehot = jnp.zeros(logits.shape, F32)
    v0 = None
    den = None
    for kk in range(TOP_K):
        mx = jnp.max(logits, axis=-1, keepdims=True)
        idx = jnp.min(jnp.where(logits == mx, lanef, float(LANES)), axis=-1, keepdims=True)
        sel = lanef == idx
        if kk == 0:
            v0 = mx
        e = jnp.exp(mx - v0)
        den = e if den is None else den + e
        ids = jnp.where(lane == kk, idx, ids)
        tw = jnp.where(lane == kk, e, tw)
        onehot = jnp.where(sel, 1.0, onehot)
        logits = jnp.where(sel, -jnp.inf, logits)
    ids_ref[...] = ids.astype(I32)
    tw_ref[...] = tw / den
    cnt_ref[...] += _ceil_rows(jnp.sum(onehot, axis=0, keepdims=True))


def _router(x2, g, wr_hi, wr_lo, b_r, tm):
    T = x2.shape[0]
    full = lambda i: (0, 0)
    return pl.pallas_call(
        _router_kernel,
        grid=(T // tm,),
        in_specs=[
            pl.BlockSpec((tm, D_MODEL), lambda i: (i, 0)),
            pl.BlockSpec((1, D_MODEL), full),
            pl.BlockSpec((D_MODEL, LANES), full),
            pl.BlockSpec((D_MODEL, LANES), full),
            pl.BlockSpec((1, LANES), full),
        ],
        out_specs=[
            pl.BlockSpec((tm, D_MODEL), lambda i: (i, 0)),
            pl.BlockSpec((tm, LANES), lambda i: (i, 0)),
            pl.BlockSpec((tm, LANES), lambda i: (i, 0)),
            pl.BlockSpec((1, LANES), full),
        ],
        out_shape=[
            jax.ShapeDtypeStruct((T, D_MODEL), BF16),
            jax.ShapeDtypeStruct((T, LANES), I32),
            jax.ShapeDtypeStruct((T, LANES), F32),
            jax.ShapeDtypeStruct((1, LANES), F32),
        ],
        compiler_params=_cparams(("arbitrary",)),
        name="router",
    )(x2, g, wr_hi, wr_lo, b_r)


SEG_WORDS = SUBLANES * LANES


def _slots_kernel(ids_ref, start_ref, ls_ref, lst_ref, seg_ref, run_ref):
    @pl.when(pl.program_id(0) == 0)
    def _():
        run_ref[...] = jnp.zeros_like(run_ref)

    ids = ids_ref[...]
    tm = ids.shape[0]
    lane = lax.broadcasted_iota(I32, ids.shape, 1)
    sels = [lane == ids[:, kk:kk + 1] for kk in range(TOP_K)]
    onehot = jnp.zeros(ids.shape, F32)
    for s in sels:
        onehot = jnp.where(s, 1.0, onehot)
    c8 = _ceil_rows(jnp.sum(onehot, axis=0, keepdims=True))
    er = lax.broadcasted_iota(I32, (LANES, LANES), 0)
    ec = lax.broadcasted_iota(I32, (LANES, LANES), 1)
    before = jnp.where(er < ec, 1.0, 0.0).astype(BF16)
    pieces = jnp.broadcast_to(c8 * (1.0 / SUBLANES), (SUBLANES, LANES)).astype(BF16)
    lo = _dot(pieces, before)[0:1, :] * SUBLANES
    r = lax.broadcasted_iota(I32, (tm, tm), 0)
    c = lax.broadcasted_iota(I32, (tm, tm), 1)
    strict = jnp.where(c < r, 1.0, 0.0).astype(BF16)
    slot = _dot(strict, onehot.astype(BF16)) + lo
    ls = jnp.zeros(ids.shape, F32)
    for kk, s in enumerate(sels):
        pk = jnp.sum(jnp.where(s, slot, 0.0), axis=-1, keepdims=True)
        ls = jnp.where(lane == kk, pk, ls)
    ls_ref[...] = ls
    hi = jnp.floor(ls * (1.0 / 32.0))
    rem = ls - 32.0 * hi
    pick = jnp.where(lax.broadcasted_iota(I32, (SUBLANES, LANES), 0) == lax.broadcasted_iota(I32, (SUBLANES, LANES), 1),
                     1.0, 0.0).astype(BF16)
    lst_ref[...] = 32.0 * _dot_nt(pick, hi.astype(BF16)) + _dot_nt(pick, rem.astype(BF16))
    row = lax.broadcasted_iota(I32, (SUBLANES, LANES), 0)
    off = start_ref[...] + run_ref[...]
    total = jnp.sum(c8, axis=-1, keepdims=True)
    seg = jnp.where(row == 0, c8, jnp.where(row == 1, lo, jnp.where(row == 2, off, jnp.where(row == 3, total, 0.0))))
    seg_ref[...] = seg.astype(I32)
    run_ref[...] += c8


def _slots(ids, starts, tm):
    T = ids.shape[0]
    nt = T // tm
    return pl.pallas_call(
        _slots_kernel,
        grid=(nt,),
        in_specs=[
            pl.BlockSpec((tm, LANES), lambda i: (i, 0)),
            pl.BlockSpec((1, LANES), lambda i: (0, 0)),
        ],
        out_specs=[
            pl.BlockSpec((tm, LANES), lambda i: (i, 0)),
            pl.BlockSpec((SUBLANES, tm), lambda i: (0, i)),
            pl.BlockSpec((SUBLANES, LANES), lambda i: (i, 0)),
        ],
        out_shape=[
            jax.ShapeDtypeStruct((T, LANES), F32),
            jax.ShapeDtypeStruct((SUBLANES, T), F32),
            jax.ShapeDtypeStruct((nt * SUBLANES, LANES), I32),
        ],
        scratch_shapes=[pltpu.VMEM((1, LANES), F32)],
        compiler_params=_cparams(("arbitrary",)),
        name="slots",
    )(ids, starts)


def _local_rows(tm):
    need = tm * TOP_K + N_EXPERTS * (SUBLANES - 1)
    return ((need + LANES - 1) // LANES) * LANES


BIG_PIECE = 4 * SUBLANES


def _segment_starts(seg, make_copy):
    def expert(e, carry):
        cnt = seg(e)
        lo = seg(LANES + e)
        off = seg(2 * LANES + e)
        n_big = lax.shift_right_logical(cnt, 5)
        n_small = lax.shift_right_logical(cnt & (BIG_PIECE - 1), 3)

        def big(j, carry2):
            d = j * BIG_PIECE
            make_copy(pl.multiple_of(lo + d, SUBLANES), pl.multiple_of(off + d, SUBLANES), BIG_PIECE).start()
            return carry2

        def small(j, carry2):
            d = n_big * BIG_PIECE + j * SUBLANES
            make_copy(pl.multiple_of(lo + d, SUBLANES), pl.multiple_of(off + d, SUBLANES), SUBLANES).start()
            return carry2

        lax.fori_loop(0, n_big, big, 0)
        lax.fori_loop(0, n_small, small, 0)
        return carry

    lax.fori_loop(0, N_EXPERTS, expert, 0)


def _segment_waits(total_rows, make_copy):
    def big(j, carry):
        make_copy(0, 0, BIG_PIECE).wait()
        return carry

    def small(j, carry):
        make_copy(0, 0, SUBLANES).wait()
        return carry

    lax.fori_loop(0, lax.shift_right_logical(total_rows, 5), big, 0)
    lax.fori_loop(0, lax.shift_right_logical(total_rows & (BIG_PIECE - 1), 3), small, 0)


def _dispatch_kernel(h_ref, lst_ref, seg_hbm, xs_ref, sbuf_ref, seg_smem, prev_smem, sem_seg, sem_rows, *, tm):
    i = pl.program_id(0)
    n = pl.num_programs(0)
    slot = i % 2
    rows = sbuf_ref.shape[1]
    cp = pltpu.make_async_copy(seg_hbm.at[i], seg_smem, sem_seg)
    cp.start()
    lst = lst_ref[...].astype(I32)
    rid = lax.broadcasted_iota(I32, (rows, tm), 0)
    perm = jnp.zeros((rows, tm), F32)
    for kk in range(TOP_K):
        perm = perm + jnp.where(rid == lst[kk:kk + 1, :], 1.0, 0.0)
    srt = _dot(perm.astype(BF16), h_ref[...])
    sbuf_ref[slot] = _pack_bf16_pairs(srt)
    cp.wait()

    def copy_from(s):
        def make_copy(lo, off, nrows):
            return pltpu.make_async_copy(sbuf_ref.at[s, pl.ds(lo, nrows), :], xs_ref.at[pl.ds(off, nrows), :],
                                         sem_rows.at[s])
        return make_copy

    _segment_starts(lambda k: seg_smem[k], copy_from(slot))

    @pl.when(i > 0)
    def _():
        _segment_waits(prev_smem[0], copy_from(1 - slot))

    prev_smem[0] = seg_smem[3 * LANES]

    @pl.when(i == n - 1)
    def _():
        _segment_waits(prev_smem[0], copy_from(slot))


def _dispatch(h, lst, seg2d, n_rows, tm):
    T = h.shape[0]
    rows = _local_rows(tm)
    return pl.pallas_call(
        functools.partial(_dispatch_kernel, tm=tm),
        grid=(T // tm,),
        in_specs=[
            pl.BlockSpec((tm, D_MODEL), lambda i: (i, 0)),
            pl.BlockSpec((SUBLANES, tm), lambda i: (0, i)),
            pl.BlockSpec(memory_space=pl.ANY),
        ],
        out_specs=pl.BlockSpec(memory_space=pl.ANY),
        out_shape=jax.ShapeDtypeStruct((n_rows, D_MODEL // 2), U32),
        scratch_shapes=[
            pltpu.VMEM((2, rows, D_MODEL // 2), U32),
            pltpu.SMEM((SEG_WORDS,), I32),
            pltpu.SMEM((1,), I32),
            pltpu.SemaphoreType.DMA,
            pltpu.SemaphoreType.DMA((2,)),
        ],
        compiler_params=_cparams(("arbitrary",)),
        name="dispatch",
    )(h, lst, seg2d)


def _experts_kernel(te_ref, nu_ref, nv_ref, xs_ref, wgu_ref, bgu_ref, wdn_ref, bdn_ref, ys_ref, wgu_bf, wdn_bf):
    i = pl.program_id(0)
    half = D_MODEL // 2

    @pl.when((i == 0) | (te_ref[i] != te_ref[jnp.maximum(i - 1, 0)]))
    def _():
        wgu_bf[...] = wgu_ref[0].astype(BF16)
        wdn_bf[...] = wdn_ref[0].astype(BF16)

    @pl.when(i < nu_ref[0])
    def _():
        live = lax.broadcasted_iota(I32, xs_ref.shape, 0) < nv_ref[i]
        lo, hi = _unpack_bf16_pairs(jnp.where(live, xs_ref[...], jnp.uint32(0)))
        gu = (_dot(lo.astype(BF16), wgu_bf[:half, :]) + _dot(hi.astype(BF16), wgu_bf[half:, :])
              + bgu_ref[0])
        gate = jnp.minimum(gu[:, :D_FF], SWIGLU_LIMIT)
        up = jnp.clip(gu[:, D_FF:], -SWIGLU_LIMIT, SWIGLU_LIMIT)
        act = (up + 1.0) * (gate * _sigmoid(SWIGLU_ALPHA * gate))
        y = _dot(act.astype(BF16), wdn_bf[...]) + bdn_ref[0]
        ys_ref[...] = _pack_bf16_pairs(y)

    @pl.when(i >= nu_ref[0])
    def _():
        ys_ref[...] = jnp.zeros_like(ys_ref)


def _experts(tile_expert, n_used, tile_valid, xs, w_gu, b_gu, w_dn, b_dn, tg):
    P = xs.shape[0]
    half = D_MODEL // 2
    grid_spec = pltpu.PrefetchScalarGridSpec(
        num_scalar_prefetch=3,
        grid=(P // tg,),
        in_specs=[
            pl.BlockSpec((tg, half), lambda i, te, nu, nv: (jnp.minimum(i, jnp.maximum(nu[0] - 1, 0)), 0)),
            pl.BlockSpec((1, D_MODEL, 2 * D_FF), lambda i, te, nu, nv: (te[i], 0, 0)),
            pl.BlockSpec((1, 1, 2 * D_FF), lambda i, te, nu, nv: (te[i], 0, 0)),
            pl.BlockSpec((1, D_FF, D_MODEL), lambda i, te, nu, nv: (te[i], 0, 0)),
            pl.BlockSpec((1, 1, D_MODEL), lambda i, te, nu, nv: (te[i], 0, 0)),
        ],
        out_specs=pl.BlockSpec((tg, half), lambda i, te, nu, nv: (i, 0)),
        scratch_shapes=[pltpu.VMEM((D_MODEL, 2 * D_FF), BF16), pltpu.VMEM((D_FF, D_MODEL), BF16)],
    )
    return pl.pallas_call(
        _experts_kernel,
        grid_spec=grid_spec,
        out_shape=jax.ShapeDtypeStruct((P, half), U32),
        compiler_params=_cparams(("arbitrary",)),
        name="experts",
    )(tile_expert, n_used, tile_valid, xs, w_gu, b_gu, w_dn, b_dn)


def _combine_kernel(seg_hbm, ys_hbm, ls_ref, tw_ref, x_ref, g_ref, out_ref, ybuf_ref, seg_smem, sem_seg, sem_rows, *,
                    tm, final_norm):
    i = pl.program_id(0)
    n = pl.num_programs(0)
    slot = i % 2
    rows = ybuf_ref.shape[1]

    def copy_into(s):
        def make_copy(lo, off, nrows):
            return pltpu.make_async_copy(ys_hbm.at[pl.ds(off, nrows), :], ybuf_ref.at[s, pl.ds(lo, nrows), :],
                                         sem_rows.at[s])
        return make_copy

    def request(step, s):
        cp = pltpu.make_async_copy(seg_hbm.at[step], seg_smem.at[s], sem_seg)
        cp.start()
        cp.wait()
        _segment_starts(lambda k: seg_smem[s, k], copy_into(s))

    @pl.when(i == 0)
    def _():
        ybuf_ref[...] = jnp.zeros_like(ybuf_ref)
        request(0, 0)

    @pl.when(i + 1 < n)
    def _():
        request(i + 1, 1 - slot)

    ls = ls_ref[...].astype(I32)
    tw = tw_ref[...]
    cid = lax.broadcasted_iota(I32, (tm, rows), 1)
    wmat = jnp.zeros((tm, rows), F32)
    for kk in range(TOP_K):
        wmat = wmat + jnp.where(cid == ls[:, kk:kk + 1], tw[:, kk:kk + 1], 0.0)
    wh, wl = _split_bf16(wmat)
    _segment_waits(seg_smem[slot, 3 * LANES], copy_into(slot))
    lo, hi = _unpack_bf16_pairs(ybuf_ref[slot])
    lo = lo.astype(BF16)
    hi = hi.astype(BF16)
    moe = jnp.concatenate([_dot(wh, lo) + _dot(wl, lo), _dot(wh, hi) + _dot(wl, hi)], axis=1)
    x3 = x_ref[...] + moe
    out_ref[...] = _rms(x3, g_ref[...]) if final_norm else x3


def _combine(seg2d, ys, ls, tw, x2, g, tm, final_norm):
    T = x2.shape[0]
    rows = _local_rows(tm)
    return pl.pallas_call(
        functools.partial(_combine_kernel, tm=tm, final_norm=final_norm),
        grid=(T // tm,),
        in_specs=[
            pl.BlockSpec(memory_space=pl.ANY),
            pl.BlockSpec(memory_space=pl.ANY),
            pl.BlockSpec((tm, LANES), lambda i: (i, 0)),
            pl.BlockSpec((tm, LANES), lambda i: (i, 0)),
            pl.BlockSpec((tm, D_MODEL), lambda i: (i, 0)),
            pl.BlockSpec((1, D_MODEL), lambda i: (0, 0)),
        ],
        out_specs=pl.BlockSpec((tm, D_MODEL), lambda i: (i, 0)),
        out_shape=jax.ShapeDtypeStruct((T, D_MODEL), F32),
        scratch_shapes=[
            pltpu.VMEM((2, rows, D_MODEL // 2), U32),
            pltpu.SMEM((2, SEG_WORDS), I32),
            pltpu.SemaphoreType.DMA,
            pltpu.SemaphoreType.DMA((2,)),
        ],
        compiler_params=_cparams(("arbitrary",)),
        name="combine",
    )(seg2d, ys, ls, tw, x2, g)


def _tiles(B, S):
    T = B * S
    return dict(
        tm_in=min(1024, T), tn_in=1024,
        ts=min(512, S),
        tq=min(512, S),
        tm_proj=min(512, S),
        tm_route=min(256, T),
        tg=512,
    )


def _pad_lanes(a, n=LANES, value=0.0):
    return jnp.pad(a, ((0, 0), (0, n - a.shape[1])), constant_values=value)


def kernel(x, mem, norm_mix, w_in, conv_w, b_if, mlstm_gain, diff_lambda, diff_gain, w_branch_m, w_branch_d,
           b_gate, w_out, norm_xattn, norm_mem, wq_x, wkv_x, wo_x, norm_ffn, w_router, b_router, w_gu, b_gu,
           w_dn, b_dn, norm_final):
    B, S, D = x.shape
    n_mem = mem.shape[1]
    T = B * S
    depth = norm_mix.shape[0]
    tl = _tiles(B, S)
    x2d = x.reshape(T, D)
    mem2d = mem.reshape(B * n_mem, D)

    for l in range(depth):
        lam_init = 0.8 - 0.6 * math.exp(-0.3 * l)
        wl = w_in[l]
        if_lo = 2 * M_QK + 2 * M_V
        w_main = jnp.concatenate([wl[:, :if_lo], wl[:, if_lo + 2 * M_HEADS:]], axis=1).astype(BF16)
        w_if = wl[:, if_lo:if_lo + 2 * M_HEADS]
        w_ifp = _pad_lanes(w_if).astype(BF16)
        w_ift = w_if.T.astype(BF16)
        bif = _pad_lanes(b_if[l][None, :])
        bift = jnp.broadcast_to(b_if[l][:, None], (SUBLANES, LANES))

        z, zif, zift = _inproj(x2d, norm_mix[l][None, :], w_main, w_ifp, w_ift, tl["tm_in"], tl["tn_in"])
        hm = _mlstm(z, zif, zift, conv_w[l], bif, bift, mlstm_gain[l].reshape(1, M_V), B, S, tl["ts"])
        hd = _diffattn(z, diff_lambda[l], diff_gain[l][None, :], B, S, tl["tq"], lam_init)
        x1 = _merge(hm, hd, z, x2d, w_branch_m[l].astype(BF16), w_branch_d[l].astype(BF16),
                    w_out[l].astype(BF16), b_gate[l][None, :], tl["tm_proj"])

        kvmem = _memkv(mem2d, norm_mem[l][None, :], wkv_x[l].astype(BF16), n_mem)
        x2 = _xattn(x1, norm_xattn[l][None, :], wq_x[l].astype(BF16), kvmem, wo_x[l].astype(BF16),
                    S, n_mem, tl["tm_proj"])

        wr = _pad_lanes(w_router[l])
        wr_hi = wr.astype(BF16)
        wr_lo = (wr - wr_hi.astype(F32)).astype(BF16)
        br = _pad_lanes(b_router[l][None, :], value=-jnp.inf)
        tm_r = tl["tm_route"]
        tg = tl["tg"]
        hp, ids, tw, cnt = _router(x2, norm_ffn[l][None, :], wr_hi, wr_lo, br, tm_r)

        counts = cnt[0, :N_EXPERTS].astype(I32)
        padded = ((counts + tg - 1) // tg) * tg
        ends = jnp.cumsum(padded)
        starts = ends - padded
        max_rows = T * TOP_K + (T // tm_r) * N_EXPERTS * (SUBLANES - 1)
        n_tiles = -(-max_rows // tg) + N_EXPERTS
        tile_row0 = jnp.arange(n_tiles, dtype=I32) * tg
        tile_expert = jnp.minimum(jnp.sum((tile_row0[:, None] >= ends[None, :]).astype(I32), axis=1), N_EXPERTS - 1)
        n_used = (ends[-1] // tg).astype(I32).reshape(1)
        last_used = tile_expert[jnp.maximum(n_used[0] - 1, 0)]
        tile_expert = jnp.where(tile_row0 < ends[-1], tile_expert, last_used)
        tile_valid = jnp.clip((starts + counts)[tile_expert] - tile_row0, 0, tg).astype(I32)

        ls, lst, seg = _slots(ids, _pad_lanes(starts.astype(F32)[None, :]), tm_r)
        seg2d = seg.reshape(T // tm_r, SEG_WORDS)

        xs = _dispatch(hp, lst, seg2d, n_tiles * tg, tm_r)
        ys = _experts(tile_expert, n_used, tile_valid, xs, w_gu[l], b_gu[l][:, None, :], w_dn[l], b_dn[l][:, None, :],
                      tg)
        x2d = _combine(seg2d, ys, ls, tw, x2, norm_final[None, :], tm_r, final_norm=(l == depth - 1))
    return x2d.reshape(B, S, D)
```

```python
import functools
import math

import jax
import jax.numpy as jnp
from jax import lax
from jax.experimental import pallas as pl
from jax.experimental.pallas import tpu as pltpu

F32 = jnp.float32
BF16 = jnp.bfloat16
U32 = jnp.uint32
I32 = jnp.int32

EPS = 1e-6
CHUNK = 64
D_MODEL = 1024
M_HEADS = 4
M_DK = 128
M_DV = 256
M_QK = M_HEADS * M_DK
M_V = M_HEADS * M_DV
CONV_W = 4
D_HEADS = 8
D_DH = 64
D_QK = D_HEADS * 2 * D_DH
D_V = D_HEADS * 2 * D_DH
X_HEADS = 4
X_DH = D_MODEL // X_HEADS
N_EXPERTS = 32
TOP_K = 4
D_FF = D_MODEL
SWIGLU_LIMIT = 7.0
SWIGLU_ALPHA = 1.702

LANES = 128
SUBLANES = 8
N_MAIN = 2 * M_QK + 2 * M_V + 2 * D_QK + D_V + 2 * D_MODEL
OFF_QM, OFF_KM, OFF_VM, OFF_OM = 0, M_QK, 2 * M_QK, 2 * M_QK + M_V
OFF_QD = OFF_OM + M_V
OFF_KD = OFF_QD + D_QK
OFF_VD = OFF_KD + D_QK
OFF_G = OFF_VD + D_V

VMEM_LIMIT = 56 * 1024 * 1024
DMA_UNROLL = 4


def _cparams(sem, vmem=VMEM_LIMIT):
    return pltpu.CompilerParams(dimension_semantics=sem, vmem_limit_bytes=vmem)


def _rms(x, g):
    return x * lax.rsqrt(jnp.mean(x * x, axis=-1, keepdims=True) + EPS) * g


def _split_bf16(x):
    hi = x.astype(BF16)
    lo = (x - hi.astype(F32)).astype(BF16)
    return hi, lo


def _dot(a, b):
    return jnp.dot(a, b, preferred_element_type=F32)


def _dot_nt(a, b):
    return lax.dot_general(a, b, (((1,), (1,)), ((), ())), preferred_element_type=F32)


def _sigmoid(x):
    return 1.0 / (1.0 + jnp.exp(-x))


def _log_sigmoid(x):
    return jnp.minimum(x, 0.0) - jnp.log(1.0 + jnp.exp(-jnp.abs(x)))


def _pack_bf16_pairs(x):
    w = x.shape[1] // 2
    u = lax.bitcast_convert_type(x, U32)
    r = (u + jnp.uint32(0x7FFF) + ((u >> 16) & jnp.uint32(1))) >> 16
    return r[:, :w] | (r[:, w:] << 16)


def _ceil_rows(x):
    return jnp.floor((x + (SUBLANES - 1)) * (1.0 / SUBLANES)) * SUBLANES


def _unpack_bf16_pairs(p):
    lo = lax.bitcast_convert_type(p << 16, F32)
    hi = lax.bitcast_convert_type(p & jnp.uint32(0xFFFF0000), F32)
    return lo, hi


def _inproj_kernel(x_ref, g_ref, w_ref, wif_ref, wift_ref, z_ref, zif_ref, zift_ref, hn_ref):
    @pl.when(pl.program_id(1) == 0)
    def _():
        hn = _rms(x_ref[...], g_ref[...]).astype(BF16)
        hn_ref[...] = hn
        zif_ref[...] = _dot(hn, wif_ref[...])
        zift_ref[...] = _dot_nt(wift_ref[...], hn)

    z_ref[...] = _dot(hn_ref[...], w_ref[...]).astype(BF16)


def _inproj(x2d, g, w_main, w_if, w_ift, tm, tn):
    T = x2d.shape[0]
    return pl.pallas_call(
        _inproj_kernel,
        grid=(T // tm, N_MAIN // tn),
        in_specs=[
            pl.BlockSpec((tm, D_MODEL), lambda i, j: (i, 0)),
            pl.BlockSpec((1, D_MODEL), lambda i, j: (0, 0)),
            pl.BlockSpec((D_MODEL, tn), lambda i, j: (0, j)),
            pl.BlockSpec((D_MODEL, LANES), lambda i, j: (0, 0)),
            pl.BlockSpec((SUBLANES, D_MODEL), lambda i, j: (0, 0)),
        ],
        out_specs=[
            pl.BlockSpec((tm, tn), lambda i, j: (i, j)),
            pl.BlockSpec((tm, LANES), lambda i, j: (i, 0)),
            pl.BlockSpec((SUBLANES, tm), lambda i, j: (0, i)),
        ],
        out_shape=[
            jax.ShapeDtypeStruct((T, N_MAIN), BF16),
            jax.ShapeDtypeStruct((T, LANES), F32),
            jax.ShapeDtypeStruct((SUBLANES, T), F32),
        ],
        scratch_shapes=[pltpu.VMEM((tm, D_MODEL), BF16)],
        compiler_params=_cparams(("arbitrary", "arbitrary")),
        name="inproj",
    )(x2d, g, w_main, w_if, w_ift)


def _mlstm_kernel(q_ref, k_ref, v_ref, om_ref, zif_ref, zift_ref, cw_ref, bif_ref, bift_ref, mg_ref,
                  out_ref, qc_ref, kc_ref, kt_ref, carry_ref, c_ref, n_ref, m_ref,
                  bd_ref, bdt_ref, brep_ref, grow_ref, brow_ref, *, ts):
    nchunk = ts // CHUNK
    L = CHUNK

    @pl.when(pl.program_id(1) == 0)
    def _():
        carry_ref[...] = jnp.zeros_like(carry_ref)
        c_ref[...] = jnp.zeros_like(c_ref)
        n_ref[...] = jnp.zeros_like(n_ref)
        m_ref[...] = jnp.zeros_like(m_ref)
        rt = lax.broadcasted_iota(I32, (ts, ts), 0)
        ct = lax.broadcasted_iota(I32, (ts, ts), 1)
        same = (rt // L) == (ct // L)
        bd_ref[...] = jnp.where(same, jnp.where(ct <= rt, 1.0, 0.0), 0.0).astype(BF16)
        bdt_ref[...] = jnp.where(same, jnp.where(rt <= ct, 1.0, 0.0), 0.0).astype(BF16)

    row8 = lax.broadcasted_iota(I32, (SUBLANES, M_QK), 0)

    def conv_silu(x, prev8, w):
        acc = w[CONV_W - 1:CONV_W, :] * x
        for s in range(1, CONV_W):
            xs = pltpu.roll(x, s, 0)
            top = jnp.where(row8 < s, pltpu.roll(prev8, s, 0), xs[0:SUBLANES])
            xs = jnp.concatenate([top, xs[SUBLANES:]], axis=0)
            acc = acc + w[CONV_W - 1 - s:CONV_W - s, :] * xs
        return acc * _sigmoid(acc)

    def conv_body(c, carry):
        r0 = pl.multiple_of(c * L, L)
        xq = q_ref[pl.ds(r0, L), :].astype(F32)
        xk = k_ref[pl.ds(r0, L), :].astype(F32)
        yq = conv_silu(xq, carry_ref[:, 0:M_QK], cw_ref[:, 0:M_QK]) * (M_DK ** -0.5)
        yk = conv_silu(xk, carry_ref[:, M_QK:2 * M_QK], cw_ref[:, M_QK:2 * M_QK])
        qc_ref[pl.ds(r0, L), :] = yq.astype(BF16)
        kc_ref[pl.ds(r0, L), :] = yk.astype(BF16)
        for h in range(M_HEADS):
            kt_ref[c, h] = yk[:, h * M_DK:(h + 1) * M_DK].T
        carry_ref[:, 0:M_QK] = xq[L - SUBLANES:L]
        carry_ref[:, M_QK:2 * M_QK] = xk[L - SUBLANES:L]
        return carry

    lax.fori_loop(0, nchunk, conv_body, 0)

    ti = lax.broadcasted_iota(I32, (L, L), 0)
    si = lax.broadcasted_iota(I32, (L, L), 1)
    causal = si <= ti
    lane_row = lax.broadcasted_iota(I32, (LANES, LANES), 0)
    ones_l = jnp.ones((L, LANES), BF16)

    lf_col = _log_sigmoid(zif_ref[...] + bif_ref[...])
    ch, cl = _split_bf16(lf_col)
    b_col_all = _dot(bd_ref[...], ch) + _dot(bd_ref[...], cl)
    bh, bl = _split_bf16(b_col_all)
    for h in range(M_HEADS):
        sel_f = jnp.where(lane_row == M_HEADS + h, 1.0, 0.0).astype(BF16)
        brep_ref[h] = _dot(bh, sel_f) + _dot(bl, sel_f)
    g_row_all = zift_ref[...] + bift_ref[:, 0:1]
    rh, rl = _split_bf16(_log_sigmoid(g_row_all))
    b_row_tile = _dot(rh, bdt_ref[...]) + _dot(rl, bdt_ref[...])
    for cc in range(nchunk):
        grow_ref[cc] = g_row_all[:, cc * L:(cc + 1) * L]
        brow_ref[cc] = b_row_tile[:, cc * L:(cc + 1) * L]

    def chunk_body(c, carry):
        r0 = pl.multiple_of(c * L, L)
        g_row = grow_ref[c]
        b_row_all = brow_ref[c]
        for h in range(M_HEADS):
            b_rep = brep_ref[h, pl.ds(r0, L), :]
            i_row = g_row[h:h + 1, :]
            b_row = b_row_all[M_HEADS + h:M_HEADS + h + 1, :]
            b_last = b_rep[L - 1:L, :]
            q = qc_ref[pl.ds(r0, L), h * M_DK:(h + 1) * M_DK]
            k = kc_ref[pl.ds(r0, L), h * M_DK:(h + 1) * M_DK]
            vext = jnp.concatenate([v_ref[pl.ds(r0, L), h * M_DV:(h + 1) * M_DV], ones_l], axis=1)
            dm = jnp.where(causal, b_rep[:, :L] - b_row + i_row, -jnp.inf)
            m_loc = jnp.max(dm, axis=-1, keepdims=True)
            s_loc = _dot_nt(q, k) * jnp.exp(dm - m_loc)
            pv = _dot(s_loc.astype(BF16), vext)
            gk_row = b_last[:, :L] - b_row + i_row
            g_max = jnp.max(gk_row, axis=-1, keepdims=True)
            kwt = (kt_ref[c, h] * jnp.exp(gk_row - g_max)).astype(BF16)
            kv = _dot(kwt, vext)
            m_prev = m_ref[h:h + 1, :]
            c_old = c_ref[h]
            n_old = n_ref[h]
            qcn = _dot(q, jnp.concatenate([c_old, n_old], axis=1).astype(BF16))
            inter = b_rep + m_prev
            m_t = jnp.maximum(inter, m_loc)
            w_inter = jnp.exp(inter - m_t)
            r_loc = jnp.exp(m_loc - m_t)
            den = r_loc * pv[:, M_DV:] + w_inter * qcn[:, M_DV:]
            inv = 1.0 / jnp.maximum(jnp.abs(den), jnp.exp(-m_t))
            hv = (jnp.concatenate([r_loc * inv] * 2, axis=1) * pv[:, :M_DV]
                  + jnp.concatenate([w_inter * inv] * 2, axis=1) * qcn[:, :M_DV])
            m_new = jnp.maximum(b_last + m_prev, g_max)
            decay = jnp.exp(b_last + m_prev - m_new)
            sc_loc = jnp.exp(g_max - m_new)
            c_ref[h] = (jnp.concatenate([decay] * 2, axis=1) * c_old
                        + jnp.concatenate([sc_loc] * 2, axis=1) * kv[:, :M_DV])
            n_ref[h] = decay * n_old + sc_loc * kv[:, M_DV:]
            m_ref[h:h + 1, :] = m_new
            hn = _rms(hv, mg_ref[:, h * M_DV:(h + 1) * M_DV])
            og = _sigmoid(om_ref[pl.ds(r0, L), h * M_DV:(h + 1) * M_DV].astype(F32))
            out_ref[pl.ds(r0, L), h * M_DV:(h + 1) * M_DV] = (og * hn).astype(BF16)
        return carry

    lax.fori_loop(0, nchunk, chunk_body, 0, unroll=4)


def _mlstm(z, zif, zift, conv_w, bif, bift, m_gain, B, S, ts):
    T = B * S
    nt = S // ts
    nck = ts // CHUNK
    row = lambda b, t: b * nt + t
    return pl.pallas_call(
        functools.partial(_mlstm_kernel, ts=ts),
        grid=(B, nt),
        in_specs=[
            pl.BlockSpec((ts, M_QK), lambda b, t: (row(b, t), OFF_QM // M_QK)),
            pl.BlockSpec((ts, M_QK), lambda b, t: (row(b, t), OFF_KM // M_QK)),
            pl.BlockSpec((ts, M_V), lambda b, t: (row(b, t), OFF_VM // M_V)),
            pl.BlockSpec((ts, M_V), lambda b, t: (row(b, t), OFF_OM // M_V)),
            pl.BlockSpec((ts, LANES), lambda b, t: (row(b, t), 0)),
            pl.BlockSpec((SUBLANES, ts), lambda b, t: (0, row(b, t))),
            pl.BlockSpec((CONV_W, 2 * M_QK), lambda b, t: (0, 0)),
            pl.BlockSpec((1, LANES), lambda b, t: (0, 0)),
            pl.BlockSpec((SUBLANES, LANES), lambda b, t: (0, 0)),
            pl.BlockSpec((1, M_V), lambda b, t: (0, 0)),
        ],
        out_specs=pl.BlockSpec((ts, M_V), lambda b, t: (row(b, t), 0)),
        out_shape=jax.ShapeDtypeStruct((T, M_V), BF16),
        scratch_shapes=[
            pltpu.VMEM((ts, M_QK), BF16),
            pltpu.VMEM((ts, M_QK), BF16),
            pltpu.VMEM((nck, M_HEADS, M_DK, CHUNK), F32),
            pltpu.VMEM((SUBLANES, 2 * M_QK), F32),
            pltpu.VMEM((M_HEADS, M_DK, M_DV), F32),
            pltpu.VMEM((M_HEADS, M_DK, LANES), F32),
            pltpu.VMEM((SUBLANES, LANES), F32),
            pltpu.VMEM((ts, ts), BF16),
            pltpu.VMEM((ts, ts), BF16),
            pltpu.VMEM((M_HEADS, ts, LANES), F32),
            pltpu.VMEM((nck, SUBLANES, CHUNK), F32),
            pltpu.VMEM((nck, SUBLANES, CHUNK), F32),
        ],
        compiler_params=_cparams(("arbitrary", "arbitrary")),
        name="mlstm",
    )(z, z, z, z, zif, zift, conv_w, bif, bift, m_gain)


def _diffattn_kernel(q_ref, k_ref, v_ref, lam_ref, gain_ref, out_ref,
                     m1_ref, a1_ref, m2_ref, a2_ref, *, tq, lam_init):
    qi = pl.program_id(2)
    w = 2 * D_DH
    q = q_ref[...]
    lane = lax.broadcasted_iota(I32, (1, w), 1)
    scale = jnp.asarray(D_DH ** -0.5, BF16)
    q1 = jnp.where(lane < D_DH, q, jnp.zeros_like(q)) * scale
    q2 = jnp.where(lane >= D_DH, q, jnp.zeros_like(q)) * scale
    n_keys = k_ref.shape[0]
    wide = min(2 * tq, n_keys)
    nper = wide // tq
    ones = jnp.ones((wide, w), BF16)

    m1_ref[...] = jnp.full_like(m1_ref, -jnp.inf)
    m2_ref[...] = jnp.full_like(m2_ref, -jnp.inf)
    a1_ref[...] = jnp.zeros_like(a1_ref)
    a2_ref[...] = jnp.zeros_like(a2_ref)

    def online(s, vext, m_ref, a_ref, rows):
        m_old = m_ref[rows, :]
        m_new = jnp.maximum(m_old, jnp.max(s, axis=-1, keepdims=True))
        p = jnp.exp(s - jnp.concatenate([m_new] * (s.shape[1] // w), axis=1))
        alpha = jnp.exp(m_old - m_new)
        a_ref[rows, :] = jnp.concatenate([alpha, alpha], axis=1) * a_ref[rows, :] + _dot(p.astype(BF16), vext)
        m_ref[rows, :] = m_new

    def block(k0, nk, rows=slice(None), mask=None):
        k = k_ref[pl.ds(k0, nk), :]
        vext = jnp.concatenate([v_ref[pl.ds(k0, nk), :], ones[:nk]], axis=1)
        s1 = _dot_nt(q1[rows], k)
        s2 = _dot_nt(q2[rows], k)
        if mask is not None:
            s1 = jnp.where(mask, s1, -jnp.inf)
            s2 = jnp.where(mask, s2, -jnp.inf)
        online(s1, vext, m1_ref, a1_ref, rows)
        online(s2, vext, m2_ref, a2_ref, rows)

    def body(jj, carry):
        block(pl.multiple_of(jj * wide, wide), wide)
        return carry

    lax.fori_loop(0, qi // nper, body, 0)

    if nper == 2:
        @pl.when(qi % 2 == 1)
        def _():
            block(pl.multiple_of((qi - 1) * tq, tq), tq)

    hq = tq // 2
    d0 = pl.multiple_of(qi * tq, tq)
    def chunk_mask(nq):
        rq = lax.broadcasted_iota(I32, (nq, hq), 0) // CHUNK
        ck = lax.broadcasted_iota(I32, (nq, hq), 1) // CHUNK
        return ck <= rq

    block(d0, hq, mask=chunk_mask(tq))
    block(pl.multiple_of(d0 + hq, hq), hq, rows=slice(hq, tq), mask=chunk_mask(hq))

    lp = lam_ref[...]
    lam = (jnp.exp(jnp.sum(lp[0:1, :] * lp[1:2, :], axis=-1, keepdims=True))
           - jnp.exp(jnp.sum(lp[2:3, :] * lp[3:4, :], axis=-1, keepdims=True)) + lam_init)
    a1 = a1_ref[...]
    a2 = a2_ref[...]
    o = a1[:, :w] / a1[:, w:] - lam * (a2[:, :w] / a2[:, w:])
    out_ref[...] = (_rms(o, gain_ref[...]) * (1.0 - lam_init)).astype(BF16)


def _diffattn(z, lam_p, d_gain, B, S, tq, lam_init):
    T = B * S
    nq = S // tq
    w = 2 * D_DH
    return pl.pallas_call(
        functools.partial(_diffattn_kernel, tq=tq, lam_init=lam_init),
        grid=(B, D_HEADS, nq),
        in_specs=[
            pl.BlockSpec((tq, w), lambda b, h, i: (b * nq + i, OFF_QD // w + h)),
            pl.BlockSpec((S, w), lambda b, h, i: (b, OFF_KD // w + h)),
            pl.BlockSpec((S, w), lambda b, h, i: (b, OFF_VD // w + h)),
            pl.BlockSpec((4, D_DH), lambda b, h, i: (0, 0)),
            pl.BlockSpec((1, w), lambda b, h, i: (0, 0)),
        ],
        out_specs=pl.BlockSpec((tq, w), lambda b, h, i: (b * nq + i, h)),
        out_shape=jax.ShapeDtypeStruct((T, D_V), BF16),
        scratch_shapes=[
            pltpu.VMEM((tq, w), F32), pltpu.VMEM((tq, 2 * w), F32),
            pltpu.VMEM((tq, w), F32), pltpu.VMEM((tq, 2 * w), F32),
        ],
        compiler_params=_cparams(("arbitrary", "arbitrary", "arbitrary")),
        name="diffattn",
    )(z, z, z, lam_p, d_gain)


def _merge_kernel(hm_ref, hd_ref, gz_ref, x_ref, wbm_ref, wbd_ref, wout_ref, bg_ref, out_ref):
    bm = _dot(hm_ref[...], wbm_ref[...])
    bd = _dot(hd_ref[...], wbd_ref[...])
    g = _sigmoid(gz_ref[...].astype(F32) + bg_ref[...])
    merged = g[:, :D_MODEL] * bm + g[:, D_MODEL:] * bd
    out_ref[...] = x_ref[...] + _dot(merged.astype(BF16), wout_ref[...])


def _merge(hm, hd, z, x2d, w_bm, w_bd, w_out, b_gate, tm):
    T = x2d.shape[0]
    full = lambda i: (0, 0)
    return pl.pallas_call(
        _merge_kernel,
        grid=(T // tm,),
        in_specs=[
            pl.BlockSpec((tm, M_V), lambda i: (i, 0)),
            pl.BlockSpec((tm, D_V), lambda i: (i, 0)),
            pl.BlockSpec((tm, 2 * D_MODEL), lambda i: (i, OFF_G // (2 * D_MODEL))),
            pl.BlockSpec((tm, D_MODEL), lambda i: (i, 0)),
            pl.BlockSpec((M_V, D_MODEL), full),
            pl.BlockSpec((D_V, D_MODEL), full),
            pl.BlockSpec((D_MODEL, D_MODEL), full),
            pl.BlockSpec((1, 2 * D_MODEL), full),
        ],
        out_specs=pl.BlockSpec((tm, D_MODEL), lambda i: (i, 0)),
        out_shape=jax.ShapeDtypeStruct((T, D_MODEL), F32),
        compiler_params=_cparams(("arbitrary",)),
        name="merge",
    )(hm, hd, z, x2d, w_bm, w_bd, w_out, b_gate)


def _memkv_kernel(mem_ref, g_ref, w_ref, out_ref):
    out_ref[...] = _dot(_rms(mem_ref[...], g_ref[...]).astype(BF16), w_ref[...]).astype(BF16)


def _memkv(mem2d, g, wkv, n_mem):
    R = mem2d.shape[0]
    return pl.pallas_call(
        _memkv_kernel,
        grid=(R // n_mem,),
        in_specs=[
            pl.BlockSpec((n_mem, D_MODEL), lambda i: (i, 0)),
            pl.BlockSpec((1, D_MODEL), lambda i: (0, 0)),
            pl.BlockSpec((D_MODEL, 2 * D_MODEL), lambda i: (0, 0)),
        ],
        out_specs=pl.BlockSpec((n_mem, 2 * D_MODEL), lambda i: (i, 0)),
        out_shape=jax.ShapeDtypeStruct((R, 2 * D_MODEL), BF16),
        compiler_params=_cparams(("arbitrary",)),
        name="memkv",
    )(mem2d, g, wkv)


def _xattn_kernel(x_ref, g_ref, wq_ref, kv_ref, wo_ref, out_ref, o_ref):
    x = x_ref[...]
    h = _rms(x, g_ref[...]).astype(BF16)
    q = (_dot(h, wq_ref[...]) * (X_DH ** -0.5)).astype(BF16)
    for hd in range(X_HEADS):
        qh = q[:, hd * X_DH:(hd + 1) * X_DH]
        kh = kv_ref[:, hd * X_DH:(hd + 1) * X_DH]
        vh = kv_ref[:, D_MODEL + hd * X_DH:D_MODEL + (hd + 1) * X_DH]
        s = _dot_nt(qh, kh)
        p = jnp.exp(s - jnp.max(s, axis=-1, keepdims=True))
        p = p / jnp.sum(p, axis=-1, keepdims=True)
        o_ref[:, hd * X_DH:(hd + 1) * X_DH] = _dot(p.astype(BF16), vh).astype(BF16)
    out_ref[...] = x + _dot(o_ref[...], wo_ref[...])


def _xattn(x1, g, wq, kvmem, wo, S, n_mem, tm):
    T = x1.shape[0]
    per_b = S // tm
    full = lambda i: (0, 0)
    return pl.pallas_call(
        _xattn_kernel,
        grid=(T // tm,),
        in_specs=[
            pl.BlockSpec((tm, D_MODEL), lambda i: (i, 0)),
            pl.BlockSpec((1, D_MODEL), full),
            pl.BlockSpec((D_MODEL, D_MODEL), full),
            pl.BlockSpec((n_mem, 2 * D_MODEL), lambda i: (i // per_b, 0)),
            pl.BlockSpec((D_MODEL, D_MODEL), full),
        ],
        out_specs=pl.BlockSpec((tm, D_MODEL), lambda i: (i, 0)),
        out_shape=jax.ShapeDtypeStruct((T, D_MODEL), F32),
        scratch_shapes=[pltpu.VMEM((tm, D_MODEL), BF16)],
        compiler_params=_cparams(("arbitrary",)),
        name="xattn",
    )(x1, g, wq, kvmem, wo)


def _router_kernel(x_ref, g_ref, wrh_ref, wrl_ref, br_ref, hp_ref, ids_ref, tw_ref, cnt_ref):
    @pl.when(pl.program_id(0) == 0)
    def _():
        cnt_ref[...] = jnp.zeros_like(cnt_ref)

    hn = _rms(x_ref[...], g_ref[...])
    hh, hl = _split_bf16(hn)
    hp_ref[...] = hh
    logits = _dot(hh, wrh_ref[...]) + _dot(hh, wrl_ref[...]) + _dot(hl, wrh_ref[...]) + br_ref[...]
    lane = lax.broadcasted_iota(I32, logits.shape, 1)
    lanef = lane.astype(F32)
    ids = jnp.zeros(logits.shape, F32)
    tw = jnp.zeros(logits.shape, F32)
    onehot = jnp.zeros(logits.shape, F32)
    v0 = None
    den = None
    for kk in range(TOP_K):
        mx = jnp.max(logits, axis=-1, keepdims=True)
        idx = jnp.min(jnp.where(logits == mx, lanef, float(LANES)), axis=-1, keepdims=True)
        sel = lanef == idx
        if kk == 0:
            v0 = mx
        e = jnp.exp(mx - v0)
        den = e if den is None else den + e
        ids = jnp.where(lane == kk, idx, ids)
        tw = jnp.where(lane == kk, e, tw)
        onehot = jnp.where(sel, 1.0, onehot)
        logits = jnp.where(sel, -jnp.inf, logits)
    ids_ref[...] = ids.astype(I32)
    tw_ref[...] = tw / den
    cnt_ref[...] += _ceil_rows(jnp.sum(onehot, axis=0, keepdims=True))


def _router(x2, g, wr_hi, wr_lo, b_r, tm):
    T = x2.shape[0]
    full = lambda i: (0, 0)
    return pl.pallas_call(
        _router_kernel,
        grid=(T // tm,),
        in_specs=[
            pl.BlockSpec((tm, D_MODEL), lambda i: (i, 0)),
            pl.BlockSpec((1, D_MODEL), full),
            pl.BlockSpec((D_MODEL, LANES), full),
            pl.BlockSpec((D_MODEL, LANES), full),
            pl.BlockSpec((1, LANES), full),
        ],
        out_specs=[
            pl.BlockSpec((tm, D_MODEL), lambda i: (i, 0)),
            pl.BlockSpec((tm, LANES), lambda i: (i, 0)),
            pl.BlockSpec((tm, LANES), lambda i: (i, 0)),
            pl.BlockSpec((1, LANES), full),
        ],
        out_shape=[
            jax.ShapeDtypeStruct((T, D_MODEL), BF16),
            jax.ShapeDtypeStruct((T, LANES), I32),
            jax.ShapeDtypeStruct((T, LANES), F32),
            jax.ShapeDtypeStruct((1, LANES), F32),
        ],
        compiler_params=_cparams(("arbitrary",)),
        name="router",
    )(x2, g, wr_hi, wr_lo, b_r)


SEG_WORDS = SUBLANES * LANES


def _slots_kernel(ids_ref, start_ref, ls_ref, lst_ref, seg_ref, run_ref):
    @pl.when(pl.program_id(0) == 0)
    def _():
        run_ref[...] = jnp.zeros_like(run_ref)

    ids = ids_ref[...]
    tm = ids.shape[0]
    lane = lax.broadcasted_iota(I32, ids.shape, 1)
    sels = [lane == ids[:, kk:kk + 1] for kk in range(TOP_K)]
    onehot = jnp.zeros(ids.shape, F32)
    for s in sels:
        onehot = jnp.where(s, 1.0, onehot)
    c8 = _ceil_rows(jnp.sum(onehot, axis=0, keepdims=True))
    er = lax.broadcasted_iota(I32, (LANES, LANES), 0)
    ec = lax.broadcasted_iota(I32, (LANES, LANES), 1)
    before = jnp.where(er < ec, 1.0, 0.0).astype(BF16)
    pieces = jnp.broadcast_to(c8 * (1.0 / SUBLANES), (SUBLANES, LANES)).astype(BF16)
    lo = _dot(pieces, before)[0:1, :] * SUBLANES
    r = lax.broadcasted_iota(I32, (tm, tm), 0)
    c = lax.broadcasted_iota(I32, (tm, tm), 1)
    strict = jnp.where(c < r, 1.0, 0.0).astype(BF16)
    slot = _dot(strict, onehot.astype(BF16)) + lo
    ls = jnp.zeros(ids.shape, F32)
    for kk, s in enumerate(sels):
        pk = jnp.sum(jnp.where(s, slot, 0.0), axis=-1, keepdims=True)
        ls = jnp.where(lane == kk, pk, ls)
    ls_ref[...] = ls
    hi = jnp.floor(ls * (1.0 / 32.0))
    rem = ls - 32.0 * hi
    pick = jnp.where(lax.broadcasted_iota(I32, (SUBLANES, LANES), 0) == lax.broadcasted_iota(I32, (SUBLANES, LANES), 1),
                     1.0, 0.0).astype(BF16)
    lst_ref[...] = 32.0 * _dot_nt(pick, hi.astype(BF16)) + _dot_nt(pick, rem.astype(BF16))
    row = lax.broadcasted_iota(I32, (SUBLANES, LANES), 0)
    off = start_ref[...] + run_ref[...]
    total = jnp.sum(c8, axis=-1, keepdims=True)
    seg = jnp.where(row == 0, c8, jnp.where(row == 1, lo, jnp.where(row == 2, off, jnp.where(row == 3, total, 0.0))))
    seg_ref[...] = seg.astype(I32)
    run_ref[...] += c8


def _slots(ids, starts, tm):
    T = ids.shape[0]
    nt = T // tm
    return pl.pallas_call(
        _slots_kernel,
        grid=(nt,),
        in_specs=[
            pl.BlockSpec((tm, LANES), lambda i: (i, 0)),
            pl.BlockSpec((1, LANES), lambda i: (0, 0)),
        ],
        out_specs=[
            pl.BlockSpec((tm, LANES), lambda i: (i, 0)),
            pl.BlockSpec((SUBLANES, tm), lambda i: (0, i)),
            pl.BlockSpec((SUBLANES, LANES), lambda i: (i, 0)),
        ],
        out_shape=[
            jax.ShapeDtypeStruct((T, LANES), F32),
            jax.ShapeDtypeStruct((SUBLANES, T), F32),
            jax.ShapeDtypeStruct((nt * SUBLANES, LANES), I32),
        ],
        scratch_shapes=[pltpu.VMEM((1, LANES), F32)],
        compiler_params=_cparams(("arbitrary",)),
        name="slots",
    )(ids, starts)


def _local_rows(tm):
    need = tm * TOP_K + N_EXPERTS * (SUBLANES - 1)
    return ((need + LANES - 1) // LANES) * LANES


BIG_PIECE = 4 * SUBLANES


def _segment_starts(seg, make_copy):
    def expert(e, carry):
        cnt = seg(e)
        lo = seg(LANES + e)
        off = seg(2 * LANES + e)
        n_big = lax.shift_right_logical(cnt, 5)
        n_small = lax.shift_right_logical(cnt & (BIG_PIECE - 1), 3)

        def big(j, carry2):
            d = j * BIG_PIECE
            make_copy(pl.multiple_of(lo + d, SUBLANES), pl.multiple_of(off + d, SUBLANES), BIG_PIECE).start()
            return carry2

        def small(j, carry2):
            d = n_big * BIG_PIECE + j * SUBLANES
            make_copy(pl.multiple_of(lo + d, SUBLANES), pl.multiple_of(off + d, SUBLANES), SUBLANES).start()
            return carry2

        lax.fori_loop(0, n_big, big, 0)
        lax.fori_loop(0, n_small, small, 0)
        return carry

    lax.fori_loop(0, N_EXPERTS, expert, 0)


def _segment_waits(total_rows, make_copy):
    def big(j, carry):
        make_copy(0, 0, BIG_PIECE).wait()
        return carry

    def small(j, carry):
        make_copy(0, 0, SUBLANES).wait()
        return carry

    lax.fori_loop(0, lax.shift_right_logical(total_rows, 5), big, 0)
    lax.fori_loop(0, lax.shift_right_logical(total_rows & (BIG_PIECE - 1), 3), small, 0)


def _dispatch_kernel(h_ref, lst_ref, seg_hbm, xs_ref, sbuf_ref, seg_smem, prev_smem, sem_seg, sem_rows, *, tm):
    i = pl.program_id(0)
    n = pl.num_programs(0)
    slot = i % 2
    rows = sbuf_ref.shape[1]
    cp = pltpu.make_async_copy(seg_hbm.at[i], seg_smem, sem_seg)
    cp.start()
    lst = lst_ref[...].astype(I32)
    rid = lax.broadcasted_iota(I32, (rows, tm), 0)
    perm = jnp.zeros((rows, tm), F32)
    for kk in range(TOP_K):
        perm = perm + jnp.where(rid == lst[kk:kk + 1, :], 1.0, 0.0)
    srt = _dot(perm.astype(BF16), h_ref[...])
    sbuf_ref[slot] = _pack_bf16_pairs(srt)
    cp.wait()

    def copy_from(s):
        def make_copy(lo, off, nrows):
            return pltpu.make_async_copy(sbuf_ref.at[s, pl.ds(lo, nrows), :], xs_ref.at[pl.ds(off, nrows), :],
                                         sem_rows.at[s])
        return make_copy

    _segment_starts(lambda k: seg_smem[k], copy_from(slot))

    @pl.when(i > 0)
    def _():
        _segment_waits(prev_smem[0], copy_from(1 - slot))

    prev_smem[0] = seg_smem[3 * LANES]

    @pl.when(i == n - 1)
    def _():
        _segment_waits(prev_smem[0], copy_from(slot))


def _dispatch(h, lst, seg2d, n_rows, tm):
    T = h.shape[0]
    rows = _local_rows(tm)
    return pl.pallas_call(
        functools.partial(_dispatch_kernel, tm=tm),
        grid=(T // tm,),
        in_specs=[
            pl.BlockSpec((tm, D_MODEL), lambda i: (i, 0)),
            pl.BlockSpec((SUBLANES, tm), lambda i: (0, i)),
            pl.BlockSpec(memory_space=pl.ANY),
        ],
        out_specs=pl.BlockSpec(memory_space=pl.ANY),
        out_shape=jax.ShapeDtypeStruct((n_rows, D_MODEL // 2), U32),
        scratch_shapes=[
            pltpu.VMEM((2, rows, D_MODEL // 2), U32),
            pltpu.SMEM((SEG_WORDS,), I32),
            pltpu.SMEM((1,), I32),
            pltpu.SemaphoreType.DMA,
            pltpu.SemaphoreType.DMA((2,)),
        ],
        compiler_params=_cparams(("arbitrary",)),
        name="dispatch",
    )(h, lst, seg2d)


def _experts_kernel(te_ref, nu_ref, nv_ref, xs_ref, wgu_ref, bgu_ref, wdn_ref, bdn_ref, ys_ref, wgu_bf, wdn_bf):
    i = pl.program_id(0)
    half = D_MODEL // 2

    @pl.when((i == 0) | (te_ref[i] != te_ref[jnp.maximum(i - 1, 0)]))
    def _():
        wgu_bf[...] = wgu_ref[0].astype(BF16)
        wdn_bf[...] = wdn_ref[0].astype(BF16)

    @pl.when(i < nu_ref[0])
    def _():
        live = lax.broadcasted_iota(I32, xs_ref.shape, 0) < nv_ref[i]
        lo, hi = _unpack_bf16_pairs(jnp.where(live, xs_ref[...], jnp.uint32(0)))
        gu = (_dot(lo.astype(BF16), wgu_bf[:half, :]) + _dot(hi.astype(BF16), wgu_bf[half:, :])
              + bgu_ref[0])
        gate = jnp.minimum(gu[:, :D_FF], SWIGLU_LIMIT)
        up = jnp.clip(gu[:, D_FF:], -SWIGLU_LIMIT, SWIGLU_LIMIT)
        act = (up + 1.0) * (gate * _sigmoid(SWIGLU_ALPHA * gate))
        y = _dot(act.astype(BF16), wdn_bf[...]) + bdn_ref[0]
        ys_ref[...] = _pack_bf16_pairs(y)

    @pl.when(i >= nu_ref[0])
    def _():
        ys_ref[...] = jnp.zeros_like(ys_ref)


def _experts(tile_expert, n_used, tile_valid, xs, w_gu, b_gu, w_dn, b_dn, tg):
    P = xs.shape[0]
    half = D_MODEL // 2
    grid_spec = pltpu.PrefetchScalarGridSpec(
        num_scalar_prefetch=3,
        grid=(P // tg,),
        in_specs=[
            pl.BlockSpec((tg, half), lambda i, te, nu, nv: (jnp.minimum(i, jnp.maximum(nu[0] - 1, 0)), 0)),
            pl.BlockSpec((1, D_MODEL, 2 * D_FF), lambda i, te, nu, nv: (te[i], 0, 0)),
            pl.BlockSpec((1, 1, 2 * D_FF), lambda i, te, nu, nv: (te[i], 0, 0)),
            pl.BlockSpec((1, D_FF, D_MODEL), lambda i, te, nu, nv: (te[i], 0, 0)),
            pl.BlockSpec((1, 1, D_MODEL), lambda i, te, nu, nv: (te[i], 0, 0)),
        ],
        out_specs=pl.BlockSpec((tg, half), lambda i, te, nu, nv: (i, 0)),
        scratch_shapes=[pltpu.VMEM((D_MODEL, 2 * D_FF), BF16), pltpu.VMEM((D_FF, D_MODEL), BF16)],
    )
    return pl.pallas_call(
        _experts_kernel,
        grid_spec=grid_spec,
        out_shape=jax.ShapeDtypeStruct((P, half), U32),
        compiler_params=_cparams(("arbitrary",)),
        name="experts",
    )(tile_expert, n_used, tile_valid, xs, w_gu, b_gu, w_dn, b_dn)


def _combine_kernel(seg_hbm, ys_hbm, ls_ref, tw_ref, x_ref, g_ref, out_ref, ybuf_ref, seg_smem, sem_seg, sem_rows, *,
                    tm, final_norm):
    i = pl.program_id(0)
    n = pl.num_programs(0)
    slot = i % 2
    rows = ybuf_ref.shape[1]

    def copy_into(s):
        def make_copy(lo, off, nrows):
            return pltpu.make_async_copy(ys_hbm.at[pl.ds(off, nrows), :], ybuf_ref.at[s, pl.ds(lo, nrows), :],
                                         sem_rows.at[s])
        return make_copy

    def request(step, s):
        cp = pltpu.make_async_copy(seg_hbm.at[step], seg_smem.at[s], sem_seg)
        cp.start()
        cp.wait()
        _segment_starts(lambda k: seg_smem[s, k], copy_into(s))

    @pl.when(i == 0)
    def _():
        ybuf_ref[...] = jnp.zeros_like(ybuf_ref)
        request(0, 0)

    @pl.when(i + 1 < n)
    def _():
        request(i + 1, 1 - slot)

    ls = ls_ref[...].astype(I32)
    tw = tw_ref[...]
    cid = lax.broadcasted_iota(I32, (tm, rows), 1)
    wmat = jnp.zeros((tm, rows), F32)
    for kk in range(TOP_K):
        wmat = wmat + jnp.where(cid == ls[:, kk:kk + 1], tw[:, kk:kk + 1], 0.0)
    wh, wl = _split_bf16(wmat)
    _segment_waits(seg_smem[slot, 3 * LANES], copy_into(slot))
    lo, hi = _unpack_bf16_pairs(ybuf_ref[slot])
    lo = lo.astype(BF16)
    hi = hi.astype(BF16)
    moe = jnp.concatenate([_dot(wh, lo) + _dot(wl, lo), _dot(wh, hi) + _dot(wl, hi)], axis=1)
    x3 = x_ref[...] + moe
    out_ref[...] = _rms(x3, g_ref[...]) if final_norm else x3


def _combine(seg2d, ys, ls, tw, x2, g, tm, final_norm):
    T = x2.shape[0]
    rows = _local_rows(tm)
    return pl.pallas_call(
        functools.partial(_combine_kernel, tm=tm, final_norm=final_norm),
        grid=(T // tm,),
        in_specs=[
            pl.BlockSpec(memory_space=pl.ANY),
            pl.BlockSpec(memory_space=pl.ANY),
            pl.BlockSpec((tm, LANES), lambda i: (i, 0)),
            pl.BlockSpec((tm, LANES), lambda i: (i, 0)),
            pl.BlockSpec((tm, D_MODEL), lambda i: (i, 0)),
            pl.BlockSpec((1, D_MODEL), lambda i: (0, 0)),
        ],
        out_specs=pl.BlockSpec((tm, D_MODEL), lambda i: (i, 0)),
        out_shape=jax.ShapeDtypeStruct((T, D_MODEL), F32),
        scratch_shapes=[
            pltpu.VMEM((2, rows, D_MODEL // 2), U32),
            pltpu.SMEM((2, SEG_WORDS), I32),
            pltpu.SemaphoreType.DMA,
            pltpu.SemaphoreType.DMA((2,)),
        ],
        compiler_params=_cparams(("arbitrary",)),
        name="combine",
    )(seg2d, ys, ls, tw, x2, g)


def _tiles(B, S):
    T = B * S
    return dict(
        tm_in=min(1024, T), tn_in=1024,
        ts=min(512, S),
        tq=min(512, S),
        tm_proj=min(512, S),
        tm_route=min(256, T),
        tg=512,
    )


def _pad_lanes(a, n=LANES, value=0.0):
    return jnp.pad(a, ((0, 0), (0, n - a.shape[1])), constant_values=value)


def kernel(x, mem, norm_mix, w_in, conv_w, b_if, mlstm_gain, diff_lambda, diff_gain, w_branch_m, w_branch_d,
           b_gate, w_out, norm_xattn, norm_mem, wq_x, wkv_x, wo_x, norm_ffn, w_router, b_router, w_gu, b_gu,
           w_dn, b_dn, norm_final):
    B, S, D = x.shape
    n_mem = mem.shape[1]
    T = B * S
    depth = norm_mix.shape[0]
    tl = _tiles(B, S)
    x2d = x.reshape(T, D)
    mem2d = mem.reshape(B * n_mem, D)

    for l in range(depth):
        lam_init = 0.8 - 0.6 * math.exp(-0.3 * l)
        wl = w_in[l]
        if_lo = 2 * M_QK + 2 * M_V
        w_main = jnp.concatenate([wl[:, :if_lo], wl[:, if_lo + 2 * M_HEADS:]], axis=1).astype(BF16)
        w_if = wl[:, if_lo:if_lo + 2 * M_HEADS]
        w_ifp = _pad_lanes(w_if).astype(BF16)
        w_ift = w_if.T.astype(BF16)
        bif = _pad_lanes(b_if[l][None, :])
        bift = jnp.broadcast_to(b_if[l][:, None], (SUBLANES, LANES))

        z, zif, zift = _inproj(x2d, norm_mix[l][None, :], w_main, w_ifp, w_ift, tl["tm_in"], tl["tn_in"])
        hm = _mlstm(z, zif, zift, conv_w[l], bif, bift, mlstm_gain[l].reshape(1, M_V), B, S, tl["ts"])
        hd = _diffattn(z, diff_lambda[l], diff_gain[l][None, :], B, S, tl["tq"], lam_init)
        x1 = _merge(hm, hd, z, x2d, w_branch_m[l].astype(BF16), w_branch_d[l].astype(BF16),
                    w_out[l].astype(BF16), b_gate[l][None, :], tl["tm_proj"])

        kvmem = _memkv(mem2d, norm_mem[l][None, :], wkv_x[l].astype(BF16), n_mem)
        x2 = _xattn(x1, norm_xattn[l][None, :], wq_x[l].astype(BF16), kvmem, wo_x[l].astype(BF16),
                    S, n_mem, tl["tm_proj"])

        wr = _pad_lanes(w_router[l])
        wr_hi = wr.astype(BF16)
        wr_lo = (wr - wr_hi.astype(F32)).astype(BF16)
        br = _pad_lanes(b_router[l][None, :], value=-jnp.inf)
        tm_r = tl["tm_route"]
        tg = tl["tg"]
        hp, ids, tw, cnt = _router(x2, norm_ffn[l][None, :], wr_hi, wr_lo, br, tm_r)

        counts = cnt[0, :N_EXPERTS].astype(I32)
        padded = ((counts + tg - 1) // tg) * tg
        ends = jnp.cumsum(padded)
        starts = ends - padded
        max_rows = T * TOP_K + (T // tm_r) * N_EXPERTS * (SUBLANES - 1)
        n_tiles = -(-max_rows // tg) + N_EXPERTS
        tile_row0 = jnp.arange(n_tiles, dtype=I32) * tg
        tile_expert = jnp.minimum(jnp.sum((tile_row0[:, None] >= ends[None, :]).astype(I32), axis=1), N_EXPERTS - 1)
        n_used = (ends[-1] // tg).astype(I32).reshape(1)
        last_used = tile_expert[jnp.maximum(n_used[0] - 1, 0)]
        tile_expert = jnp.where(tile_row0 < ends[-1], tile_expert, last_used)
        tile_valid = jnp.clip((starts + counts)[tile_expert] - tile_row0, 0, tg).astype(I32)

        ls, lst, seg = _slots(ids, _pad_lanes(starts.astype(F32)[None, :]), tm_r)
        seg2d = seg.reshape(T // tm_r, SEG_WORDS)

        xs = _dispatch(hp, lst, seg2d, n_tiles * tg, tm_r)
        ys = _experts(tile_expert, n_used, tile_valid, xs, w_gu[l], b_gu[l][:, None, :], w_dn[l], b_dn[l][:, None, :],
                      tg)
        x2d = _combine(seg2d, ys, ls, tw, x2, norm_final[None, :], tm_r, final_norm=(l == depth - 1))
    return x2d.reshape(B, S, D)
```

```python
import functools
import math

import jax
import jax.numpy as jnp
from jax import lax
from jax.experimental import pallas as pl
from jax.experimental.pallas import tpu as pltpu

F32 = jnp.float32
BF16 = jnp.bfloat16
U32 = jnp.uint32
I32 = jnp.int32

EPS = 1e-6
CHUNK = 64
D_MODEL = 1024
M_HEADS = 4
M_DK = 128
M_DV = 256
M_QK = M_HEADS * M_DK
M_V = M_HEADS * M_DV
CONV_W = 4
D_HEADS = 8
D_DH = 64
D_QK = D_HEADS * 2 * D_DH
D_V = D_HEADS * 2 * D_DH
X_HEADS = 4
X_DH = D_MODEL // X_HEADS
N_EXPERTS = 32
TOP_K = 4
D_FF = D_MODEL
SWIGLU_LIMIT = 7.0
SWIGLU_ALPHA = 1.702

LANES = 128
SUBLANES = 8
N_MAIN = 2 * M_QK + 2 * M_V + 2 * D_QK + D_V + 2 * D_MODEL
OFF_QM, OFF_KM, OFF_VM, OFF_OM = 0, M_QK, 2 * M_QK, 2 * M_QK + M_V
OFF_QD = OFF_OM + M_V
OFF_KD = OFF_QD + D_QK
OFF_VD = OFF_KD + D_QK
OFF_G = OFF_VD + D_V

VMEM_LIMIT = 56 * 1024 * 1024
DMA_UNROLL = 4


def _cparams(sem, vmem=VMEM_LIMIT):
    return pltpu.CompilerParams(dimension_semantics=sem, vmem_limit_bytes=vmem)


def _rms(x, g):
    return x * lax.rsqrt(jnp.mean(x * x, axis=-1, keepdims=True) + EPS) * g


def _split_bf16(x):
    hi = x.astype(BF16)
    lo = (x - hi.astype(F32)).astype(BF16)
    return hi, lo


def _dot(a, b):
    return jnp.dot(a, b, preferred_element_type=F32)


def _dot_nt(a, b):
    return lax.dot_general(a, b, (((1,), (1,)), ((), ())), preferred_element_type=F32)


def _sigmoid(x):
    return 1.0 / (1.0 + jnp.exp(-x))


def _log_sigmoid(x):
    return jnp.minimum(x, 0.0) - jnp.log(1.0 + jnp.exp(-jnp.abs(x)))


def _pack_bf16_pairs(x):
    w = x.shape[1] // 2
    u = lax.bitcast_convert_type(x, U32)
    r = (u + jnp.uint32(0x7FFF) + ((u >> 16) & jnp.uint32(1))) >> 16
    return r[:, :w] | (r[:, w:] << 16)


def _ceil_rows(x):
    return jnp.floor((x + (SUBLANES - 1)) * (1.0 / SUBLANES)) * SUBLANES


def _unpack_bf16_pairs(p):
    lo = lax.bitcast_convert_type(p << 16, F32)
    hi = lax.bitcast_convert_type(p & jnp.uint32(0xFFFF0000), F32)
    return lo, hi


def _inproj_kernel(x_ref, g_ref, w_ref, wif_ref, wift_ref, z_ref, zif_ref, zift_ref, hn_ref):
    @pl.when(pl.program_id(1) == 0)
    def _():
        hn = _rms(x_ref[...], g_ref[...]).astype(BF16)
        hn_ref[...] = hn
        zif_ref[...] = _dot(hn, wif_ref[...])
        zift_ref[...] = _dot_nt(wift_ref[...], hn)

    z_ref[...] = _dot(hn_ref[...], w_ref[...]).astype(BF16)


def _inproj(x2d, g, w_main, w_if, w_ift, tm, tn):
    T = x2d.shape[0]
    return pl.pallas_call(
        _inproj_kernel,
        grid=(T // tm, N_MAIN // tn),
        in_specs=[
            pl.BlockSpec((tm, D_MODEL), lambda i, j: (i, 0)),
            pl.BlockSpec((1, D_MODEL), lambda i, j: (0, 0)),
            pl.BlockSpec((D_MODEL, tn), lambda i, j: (0, j)),
            pl.BlockSpec((D_MODEL, LANES), lambda i, j: (0, 0)),
            pl.BlockSpec((SUBLANES, D_MODEL), lambda i, j: (0, 0)),
        ],
        out_specs=[
            pl.BlockSpec((tm, tn), lambda i, j: (i, j)),
            pl.BlockSpec((tm, LANES), lambda i, j: (i, 0)),
            pl.BlockSpec((SUBLANES, tm), lambda i, j: (0, i)),
        ],
        out_shape=[
            jax.ShapeDtypeStruct((T, N_MAIN), BF16),
            jax.ShapeDtypeStruct((T, LANES), F32),
            jax.ShapeDtypeStruct((SUBLANES, T), F32),
        ],
        scratch_shapes=[pltpu.VMEM((tm, D_MODEL), BF16)],
        compiler_params=_cparams(("arbitrary", "arbitrary")),
        name="inproj",
    )(x2d, g, w_main, w_if, w_ift)


def _mlstm_kernel(q_ref, k_ref, v_ref, om_ref, zif_ref, zift_ref, cw_ref, bif_ref, bift_ref, mg_ref,
                  out_ref, qc_ref, kc_ref, kt_ref, carry_ref, c_ref, n_ref, m_ref,
                  bd_ref, bdt_ref, brep_ref, grow_ref, brow_ref, *, ts):
    nchunk = ts // CHUNK
    L = CHUNK

    @pl.when(pl.program_id(1) == 0)
    def _():
        carry_ref[...] = jnp.zeros_like(carry_ref)
        c_ref[...] = jnp.zeros_like(c_ref)
        n_ref[...] = jnp.zeros_like(n_ref)
        m_ref[...] = jnp.zeros_like(m_ref)
        rt = lax.broadcasted_iota(I32, (ts, ts), 0)
        ct = lax.broadcasted_iota(I32, (ts, ts), 1)
        same = (rt // L) == (ct // L)
        bd_ref[...] = jnp.where(same, jnp.where(ct <= rt, 1.0, 0.0), 0.0).astype(BF16)
        bdt_ref[...] = jnp.where(same, jnp.where(rt <= ct, 1.0, 0.0), 0.0).astype(BF16)

    row8 = lax.broadcasted_iota(I32, (SUBLANES, M_QK), 0)

    def conv_silu(x, prev8, w):
        acc = w[CONV_W - 1:CONV_W, :] * x
        for s in range(1, CONV_W):
            xs = pltpu.roll(x, s, 0)
            top = jnp.where(row8 < s, pltpu.roll(prev8, s, 0), xs[0:SUBLANES])
            xs = jnp.concatenate([top, xs[SUBLANES:]], axis=0)
            acc = acc + w[CONV_W - 1 - s:CONV_W - s, :] * xs
        return acc * _sigmoid(acc)

    def conv_body(c, carry):
        r0 = pl.multiple_of(c * L, L)
        xq = q_ref[pl.ds(r0, L), :].astype(F32)
        xk = k_ref[pl.ds(r0, L), :].astype(F32)
        yq = conv_silu(xq, carry_ref[:, 0:M_QK], cw_ref[:, 0:M_QK]) * (M_DK ** -0.5)
        yk = conv_silu(xk, carry_ref[:, M_QK:2 * M_QK], cw_ref[:, M_QK:2 * M_QK])
        qc_ref[pl.ds(r0, L), :] = yq.astype(BF16)
        kc_ref[pl.ds(r0, L), :] = yk.astype(BF16)
        for h in range(M_HEADS):
            kt_ref[c, h] = yk[:, h * M_DK:(h + 1) * M_DK].T
        carry_ref[:, 0:M_QK] = xq[L - SUBLANES:L]
        carry_ref[:, M_QK:2 * M_QK] = xk[L - SUBLANES:L]
        return carry

    lax.fori_loop(0, nchunk, conv_body, 0)

    ti = lax.broadcasted_iota(I32, (L, L), 0)
    si = lax.broadcasted_iota(I32, (L, L), 1)
    causal = si <= ti
    lane_row = lax.broadcasted_iota(I32, (LANES, LANES), 0)
    ones_l = jnp.ones((L, LANES), BF16)

    lf_col = _log_sigmoid(zif_ref[...] + bif_ref[...])
    ch, cl = _split_bf16(lf_col)
    b_col_all = _dot(bd_ref[...], ch) + _dot(bd_ref[...], cl)
    bh, bl = _split_bf16(b_col_all)
    for h in range(M_HEADS):
        sel_f = jnp.where(lane_row == M_HEADS + h, 1.0, 0.0).astype(BF16)
        brep_ref[h] = _dot(bh, sel_f) + _dot(bl, sel_f)
    g_row_all = zift_ref[...] + bift_ref[:, 0:1]
    rh, rl = _split_bf16(_log_sigmoid(g_row_all))
    b_row_tile = _dot(rh, bdt_ref[...]) + _dot(rl, bdt_ref[...])
    for cc in range(nchunk):
        grow_ref[cc] = g_row_all[:, cc * L:(cc + 1) * L]
        brow_ref[cc] = b_row_tile[:, cc * L:(cc + 1) * L]

    def chunk_body(c, carry):
        r0 = pl.multiple_of(c * L, L)
        g_row = grow_ref[c]
        b_row_all = brow_ref[c]
        for h in range(M_HEADS):
            b_rep = brep_ref[h, pl.ds(r0, L), :]
            i_row = g_row[h:h + 1, :]
            b_row = b_row_all[M_HEADS + h:M_HEADS + h + 1, :]
            b_last = b_rep[L - 1:L, :]
            q = qc_ref[pl.ds(r0, L), h * M_DK:(h + 1) * M_DK]
            k = kc_ref[pl.ds(r0, L), h * M_DK:(h + 1) * M_DK]
            vext = jnp.concatenate([v_ref[pl.ds(r0, L), h * M_DV:(h + 1) * M_DV], ones_l], axis=1)
            dm = jnp.where(causal, b_rep[:, :L] - b_row + i_row, -jnp.inf)
            m_loc = jnp.max(dm, axis=-1, keepdims=True)
            s_loc = _dot_nt(q, k) * jnp.exp(dm - m_loc)
            pv = _dot(s_loc.astype(BF16), vext)
            gk_row = b_last[:, :L] - b_row + i_row
            g_max = jnp.max(gk_row, axis=-1, keepdims=True)
            kwt = (kt_ref[c, h] * jnp.exp(gk_row - g_max)).astype(BF16)
            kv = _dot(kwt, vext)
            m_prev = m_ref[h:h + 1, :]
            c_old = c_ref[h]
            n_old = n_ref[h]
            qcn = _dot(q, jnp.concatenate([c_old, n_old], axis=1).astype(BF16))
            inter = b_rep + m_prev
            m_t = jnp.maximum(inter, m_loc)
            w_inter = jnp.exp(inter - m_t)
            r_loc = jnp.exp(m_loc - m_t)
            den = r_loc * pv[:, M_DV:] + w_inter * qcn[:, M_DV:]
            inv = 1.0 / jnp.maximum(jnp.abs(den), jnp.exp(-m_t))
            hv = (jnp.concatenate([r_loc * inv] * 2, axis=1) * pv[:, :M_DV]
                  + jnp.concatenate([w_inter * inv] * 2, axis=1) * qcn[:, :M_DV])
            m_new = jnp.maximum(b_last + m_prev, g_max)
            decay = jnp.exp(b_last + m_prev - m_new)
            sc_loc = jnp.exp(g_max - m_new)
            c_ref[h] = (jnp.concatenate([decay] * 2, axis=1) * c_old
                        + jnp.concatenate([sc_loc] * 2, axis=1) * kv[:, :M_DV])
            n_ref[h] = decay * n_old + sc_loc * kv[:, M_DV:]
            m_ref[h:h + 1, :] = m_new
            hn = _rms(hv, mg_ref[:, h * M_DV:(h + 1) * M_DV])
            og = _sigmoid(om_ref[pl.ds(r0, L), h * M_DV:(h + 1) * M_DV].astype(F32))
            out_ref[pl.ds(r0, L), h * M_DV:(h + 1) * M_DV] = (og * hn).astype(BF16)
        return carry

    lax.fori_loop(0, nchunk, chunk_body, 0, unroll=4)


def _mlstm(z, zif, zift, conv_w, bif, bift, m_gain, B, S, ts):
    T = B * S
    nt = S // ts
    nck = ts // CHUNK
    row = lambda b, t: b * nt + t
    return pl.pallas_call(
        functools.partial(_mlstm_kernel, ts=ts),
        grid=(B, nt),
        in_specs=[
            pl.BlockSpec((ts, M_QK), lambda b, t: (row(b, t), OFF_QM // M_QK)),
            pl.BlockSpec((ts, M_QK), lambda b, t: (row(b, t), OFF_KM // M_QK)),
            pl.BlockSpec((ts, M_V), lambda b, t: (row(b, t), OFF_VM // M_V)),
            pl.BlockSpec((ts, M_V), lambda b, t: (row(b, t), OFF_OM // M_V)),
            pl.BlockSpec((ts, LANES), lambda b, t: (row(b, t), 0)),
            pl.BlockSpec((SUBLANES, ts), lambda b, t: (0, row(b, t))),
            pl.BlockSpec((CONV_W, 2 * M_QK), lambda b, t: (0, 0)),
            pl.BlockSpec((1, LANES), lambda b, t: (0, 0)),
            pl.BlockSpec((SUBLANES, LANES), lambda b, t: (0, 0)),
            pl.BlockSpec((1, M_V), lambda b, t: (0, 0)),
        ],
        out_specs=pl.BlockSpec((ts, M_V), lambda b, t: (row(b, t), 0)),
        out_shape=jax.ShapeDtypeStruct((T, M_V), BF16),
        scratch_shapes=[
            pltpu.VMEM((ts, M_QK), BF16),
            pltpu.VMEM((ts, M_QK), BF16),
            pltpu.VMEM((nck, M_HEADS, M_DK, CHUNK), F32),
            pltpu.VMEM((SUBLANES, 2 * M_QK), F32),
            pltpu.VMEM((M_HEADS, M_DK, M_DV), F32),
            pltpu.VMEM((M_HEADS, M_DK, LANES), F32),
            pltpu.VMEM((SUBLANES, LANES), F32),
            pltpu.VMEM((ts, ts), BF16),
            pltpu.VMEM((ts, ts), BF16),
            pltpu.VMEM((M_HEADS, ts, LANES), F32),
            pltpu.VMEM((nck, SUBLANES, CHUNK), F32),
            pltpu.VMEM((nck, SUBLANES, CHUNK), F32),
        ],
        compiler_params=_cparams(("arbitrary", "arbitrary")),
        name="mlstm",
    )(z, z, z, z, zif, zift, conv_w, bif, bift, m_gain)


def _diffattn_kernel(q_ref, k_ref, v_ref, lam_ref, gain_ref, out_ref,
                     m1_ref, a1_ref, m2_ref, a2_ref, *, tq, lam_init):
    qi = pl.program_id(2)
    w = 2 * D_DH
    q = q_ref[...]
    lane = lax.broadcasted_iota(I32, (1, w), 1)
    scale = jnp.asarray(D_DH ** -0.5, BF16)
    q1 = jnp.where(lane < D_DH, q, jnp.zeros_like(q)) * scale
    q2 = jnp.where(lane >= D_DH, q, jnp.zeros_like(q)) * scale
    ones = jnp.ones((tq, w), BF16)

    def online(s, vext, m_ref, a_ref, rows, first):
        s_max = jnp.max(s, axis=-1, keepdims=True)
        if first:
            m_new = jnp.broadcast_to(s_max, (s.shape[0], w))
        else:
            m_old = m_ref[rows, :]
            m_new = jnp.maximum(m_old, s_max)
        p = jnp.exp(s - jnp.concatenate([m_new] * (s.shape[1] // w), axis=1))
        pv = _dot(p.astype(BF16), vext)
        if first:
            a_ref[rows, :] = pv
        else:
            alpha = jnp.exp(m_old - m_new)
            a_ref[rows, :] = jnp.concatenate([alpha, alpha], axis=1) * a_ref[rows, :] + pv
        m_ref[rows, :] = m_new

    def block(k0, nk, rows=slice(None), mask=None, first=False):
        k = k_ref[pl.ds(k0, nk), :]
        vext = jnp.concatenate([v_ref[pl.ds(k0, nk), :], ones[:nk]], axis=1)
        s1 = _dot_nt(q1[rows], k)
        s2 = _dot_nt(q2[rows], k)
        if mask is not None:
            s1 = jnp.where(mask, s1, -jnp.inf)
            s2 = jnp.where(mask, s2, -jnp.inf)
        online(s1, vext, m1_ref, a1_ref, rows, first)
        online(s2, vext, m2_ref, a2_ref, rows, first)

    hq = tq // 2
    d0 = pl.multiple_of(qi * tq, tq)

    def chunk_mask(nq, q0):
        rq = (lax.broadcasted_iota(I32, (nq, hq), 0) + q0) // CHUNK
        ck = lax.broadcasted_iota(I32, (nq, hq), 1) // CHUNK
        return ck <= rq

    block(d0, hq, mask=chunk_mask(tq, 0), first=True)
    block(pl.multiple_of(d0 + hq, hq), hq, rows=slice(hq, tq), mask=chunk_mask(hq, 0))

    def body(jj, carry):
        block(pl.multiple_of(jj * tq, tq), tq)
        return carry

    lax.fori_loop(0, qi, body, 0)

    lp = lam_ref[...]
    lam = (jnp.exp(jnp.sum(lp[0:1, :] * lp[1:2, :], axis=-1, keepdims=True))
           - jnp.exp(jnp.sum(lp[2:3, :] * lp[3:4, :], axis=-1, keepdims=True)) + lam_init)
    a1 = a1_ref[...]
    a2 = a2_ref[...]
    o = a1[:, :w] / a1[:, w:] - lam * (a2[:, :w] / a2[:, w:])
    out_ref[...] = (_rms(o, gain_ref[...]) * (1.0 - lam_init)).astype(BF16)


def _diffattn(z, lam_p, d_gain, B, S, tq, lam_init):
    T = B * S
    nq = S // tq
    w = 2 * D_DH
    return pl.pallas_call(
        functools.partial(_diffattn_kernel, tq=tq, lam_init=lam_init),
        grid=(B, D_HEADS, nq),
        in_specs=[
            pl.BlockSpec((tq, w), lambda b, h, i: (b * nq + i, OFF_QD // w + h)),
            pl.BlockSpec((S, w), lambda b, h, i: (b, OFF_KD // w + h)),
            pl.BlockSpec((S, w), lambda b, h, i: (b, OFF_VD // w + h)),
            pl.BlockSpec((4, D_DH), lambda b, h, i: (0, 0)),
            pl.BlockSpec((1, w), lambda b, h, i: (0, 0)),
        ],
        out_specs=pl.BlockSpec((tq, w), lambda b, h, i: (b * nq + i, h)),
        out_shape=jax.ShapeDtypeStruct((T, D_V), BF16),
        scratch_shapes=[
            pltpu.VMEM((tq, w), F32), pltpu.VMEM((tq, 2 * w), F32),
            pltpu.VMEM((tq, w), F32), pltpu.VMEM((tq, 2 * w), F32),
        ],
        compiler_params=_cparams(("arbitrary", "arbitrary", "arbitrary")),
        name="diffattn",
    )(z, z, z, lam_p, d_gain)


def _merge_kernel(hm_ref, hd_ref, gz_ref, x_ref, wbm_ref, wbd_ref, wout_ref, bg_ref, out_ref):
    bm = _dot(hm_ref[...], wbm_ref[...])
    bd = _dot(hd_ref[...], wbd_ref[...])
    g = _sigmoid(gz_ref[...].astype(F32) + bg_ref[...])
    merged = g[:, :D_MODEL] * bm + g[:, D_MODEL:] * bd
    out_ref[...] = x_ref[...] + _dot(merged.astype(BF16), wout_ref[...])


def _merge(hm, hd, z, x2d, w_bm, w_bd, w_out, b_gate, tm):
    T = x2d.shape[0]
    full = lambda i: (0, 0)
    return pl.pallas_call(
        _merge_kernel,
        grid=(T // tm,),
        in_specs=[
            pl.BlockSpec((tm, M_V), lambda i: (i, 0)),
            pl.BlockSpec((tm, D_V), lambda i: (i, 0)),
            pl.BlockSpec((tm, 2 * D_MODEL), lambda i: (i, OFF_G // (2 * D_MODEL))),
            pl.BlockSpec((tm, D_MODEL), lambda i: (i, 0)),
            pl.BlockSpec((M_V, D_MODEL), full),
            pl.BlockSpec((D_V, D_MODEL), full),
            pl.BlockSpec((D_MODEL, D_MODEL), full),
            pl.BlockSpec((1, 2 * D_MODEL), full),
        ],
        out_specs=pl.BlockSpec((tm, D_MODEL), lambda i: (i, 0)),
        out_shape=jax.ShapeDtypeStruct((T, D_MODEL), F32),
        compiler_params=_cparams(("arbitrary",)),
        name="merge",
    )(hm, hd, z, x2d, w_bm, w_bd, w_out, b_gate)


def _memkv_kernel(mem_ref, g_ref, w_ref, out_ref):
    out_ref[...] = _dot(_rms(mem_ref[...], g_ref[...]).astype(BF16), w_ref[...]).astype(BF16)


def _memkv(mem2d, g, wkv, n_mem):
    R = mem2d.shape[0]
    return pl.pallas_call(
        _memkv_kernel,
        grid=(R // n_mem,),
        in_specs=[
            pl.BlockSpec((n_mem, D_MODEL), lambda i: (i, 0)),
            pl.BlockSpec((1, D_MODEL), lambda i: (0, 0)),
            pl.BlockSpec((D_MODEL, 2 * D_MODEL), lambda i: (0, 0)),
        ],
        out_specs=pl.BlockSpec((n_mem, 2 * D_MODEL), lambda i: (i, 0)),
        out_shape=jax.ShapeDtypeStruct((R, 2 * D_MODEL), BF16),
        compiler_params=_cparams(("arbitrary",)),
        name="memkv",
    )(mem2d, g, wkv)


def _xattn_kernel(x_ref, g_ref, wq_ref, kv_ref, wo_ref, out_ref, o_ref):
    x = x_ref[...]
    h = _rms(x, g_ref[...]).astype(BF16)
    q = (_dot(h, wq_ref[...]) * (X_DH ** -0.5)).astype(BF16)
    for hd in range(X_HEADS):
        qh = q[:, hd * X_DH:(hd + 1) * X_DH]
        kh = kv_ref[:, hd * X_DH:(hd + 1) * X_DH]
        vh = kv_ref[:, D_MODEL + hd * X_DH:D_MODEL + (hd + 1) * X_DH]
        s = _dot_nt(qh, kh)
        p = jnp.exp(s - jnp.max(s, axis=-1, keepdims=True))
        p = p / jnp.sum(p, axis=-1, keepdims=True)
        o_ref[:, hd * X_DH:(hd + 1) * X_DH] = _dot(p.astype(BF16), vh).astype(BF16)
    out_ref[...] = x + _dot(o_ref[...], wo_ref[...])


def _xattn(x1, g, wq, kvmem, wo, S, n_mem, tm):
    T = x1.shape[0]
    per_b = S // tm
    full = lambda i: (0, 0)
    return pl.pallas_call(
        _xattn_kernel,
        grid=(T // tm,),
        in_specs=[
            pl.BlockSpec((tm, D_MODEL), lambda i: (i, 0)),
            pl.BlockSpec((1, D_MODEL), full),
            pl.BlockSpec((D_MODEL, D_MODEL), full),
            pl.BlockSpec((n_mem, 2 * D_MODEL), lambda i: (i // per_b, 0)),
            pl.BlockSpec((D_MODEL, D_MODEL), full),
        ],
        out_specs=pl.BlockSpec((tm, D_MODEL), lambda i: (i, 0)),
        out_shape=jax.ShapeDtypeStruct((T, D_MODEL), F32),
        scratch_shapes=[pltpu.VMEM((tm, D_MODEL), BF16)],
        compiler_params=_cparams(("arbitrary",)),
        name="xattn",
    )(x1, g, wq, kvmem, wo)


def _router_kernel(x_ref, g_ref, wrh_ref, wrl_ref, br_ref, hp_ref, ids_ref, tw_ref, cnt_ref):
    @pl.when(pl.program_id(0) == 0)
    def _():
        cnt_ref[...] = jnp.zeros_like(cnt_ref)

    hn = _rms(x_ref[...], g_ref[...])
    hh, hl = _split_bf16(hn)
    hp_ref[...] = hh
    logits = _dot(hh, wrh_ref[...]) + _dot(hh, wrl_ref[...]) + _dot(hl, wrh_ref[...]) + br_ref[...]
    lane = lax.broadcasted_iota(I32, logits.shape, 1)
    lanef = lane.astype(F32)
    ids = jnp.zeros(logits.shape, F32)
    tw = jnp.zeros(logits.shape, F32)
    onehot = jnp.zeros(logits.shape, F32)
    v0 = None
    den = None
    for kk in range(TOP_K):
        mx = jnp.max(logits, axis=-1, keepdims=True)
        idx = jnp.min(jnp.where(logits == mx, lanef, float(LANES)), axis=-1, keepdims=True)
        sel = lanef == idx
        if kk == 0:
            v0 = mx
        e = jnp.exp(mx - v0)
        den = e if den is None else den + e
        ids = jnp.where(lane == kk, idx, ids)
        tw = jnp.where(lane == kk, e, tw)
        onehot = jnp.where(sel, 1.0, onehot)
        logits = jnp.where(sel, -jnp.inf, logits)
    ids_ref[...] = ids.astype(I32)
    tw_ref[...] = tw / den
    cnt_ref[...] += _ceil_rows(jnp.sum(onehot, axis=0, keepdims=True))


def _router(x2, g, wr_hi, wr_lo, b_r, tm):
    T = x2.shape[0]
    full = lambda i: (0, 0)
    return pl.pallas_call(
        _router_kernel,
        grid=(T // tm,),
        in_specs=[
            pl.BlockSpec((tm, D_MODEL), lambda i: (i, 0)),
            pl.BlockSpec((1, D_MODEL), full),
            pl.BlockSpec((D_MODEL, LANES), full),
            pl.BlockSpec((D_MODEL, LANES), full),
            pl.BlockSpec((1, LANES), full),
        ],
        out_specs=[
            pl.BlockSpec((tm, D_MODEL), lambda i: (i, 0)),
            pl.BlockSpec((tm, LANES), lambda i: (i, 0)),
            pl.BlockSpec((tm, LANES), lambda i: (i, 0)),
            pl.BlockSpec((1, LANES), full),
        ],
        out_shape=[
            jax.ShapeDtypeStruct((T, D_MODEL), BF16),
            jax.ShapeDtypeStruct((T, LANES), I32),
            jax.ShapeDtypeStruct((T, LANES), F32),
            jax.ShapeDtypeStruct((1, LANES), F32),
        ],
        compiler_params=_cparams(("arbitrary",)),
        name="router",
    )(x2, g, wr_hi, wr_lo, b_r)


SEG_WORDS = SUBLANES * LANES


def _slots_kernel(ids_ref, start_ref, ls_ref, lst_ref, seg_ref, run_ref):
    @pl.when(pl.program_id(0) == 0)
    def _():
        run_ref[...] = jnp.zeros_like(run_ref)

    ids = ids_ref[...]
    tm = ids.shape[0]
    lane = lax.broadcasted_iota(I32, ids.shape, 1)
    sels = [lane == ids[:, kk:kk + 1] for kk in range(TOP_K)]
    onehot = jnp.zeros(ids.shape, F32)
    for s in sels:
        onehot = jnp.where(s, 1.0, onehot)
    c8 = _ceil_rows(jnp.sum(onehot, axis=0, keepdims=True))
    er = lax.broadcasted_iota(I32, (LANES, LANES), 0)
    ec = lax.broadcasted_iota(I32, (LANES, LANES), 1)
    before = jnp.where(er < ec, 1.0, 0.0).astype(BF16)
    pieces = jnp.broadcast_to(c8 * (1.0 / SUBLANES), (SUBLANES, LANES)).astype(BF16)
    lo = _dot(pieces, before)[0:1, :] * SUBLANES
    r = lax.broadcasted_iota(I32, (tm, tm), 0)
    c = lax.broadcasted_iota(I32, (tm, tm), 1)
    strict = jnp.where(c < r, 1.0, 0.0).astype(BF16)
    slot = _dot(strict, onehot.astype(BF16)) + lo
    ls = jnp.zeros(ids.shape, F32)
    for kk, s in enumerate(sels):
        pk = jnp.sum(jnp.where(s, slot, 0.0), axis=-1, keepdims=True)
        ls = jnp.where(lane == kk, pk, ls)
    ls_ref[...] = ls
    hi = jnp.floor(ls * (1.0 / 32.0))
    rem = ls - 32.0 * hi
    pick = jnp.where(lax.broadcasted_iota(I32, (SUBLANES, LANES), 0) == lax.broadcasted_iota(I32, (SUBLANES, LANES), 1),
                     1.0, 0.0).astype(BF16)
    lst_ref[...] = 32.0 * _dot_nt(pick, hi.astype(BF16)) + _dot_nt(pick, rem.astype(BF16))
    row = lax.broadcasted_iota(I32, (SUBLANES, LANES), 0)
    off = start_ref[...] + run_ref[...]
    total = jnp.sum(c8, axis=-1, keepdims=True)
    seg = jnp.where(row == 0, c8, jnp.where(row == 1, lo, jnp.where(row == 2, off, jnp.where(row == 3, total, 0.0))))
    seg_ref[...] = seg.astype(I32)
    run_ref[...] += c8


def _slots(ids, starts, tm):
    T = ids.shape[0]
    nt = T // tm
    return pl.pallas_call(
        _slots_kernel,
        grid=(nt,),
        in_specs=[
            pl.BlockSpec((tm, LANES), lambda i: (i, 0)),
            pl.BlockSpec((1, LANES), lambda i: (0, 0)),
        ],
        out_specs=[
            pl.BlockSpec((tm, LANES), lambda i: (i, 0)),
            pl.BlockSpec((SUBLANES, tm), lambda i: (0, i)),
            pl.BlockSpec((SUBLANES, LANES), lambda i: (i, 0)),
        ],
        out_shape=[
            jax.ShapeDtypeStruct((T, LANES), F32),
            jax.ShapeDtypeStruct((SUBLANES, T), F32),
            jax.ShapeDtypeStruct((nt * SUBLANES, LANES), I32),
        ],
        scratch_shapes=[pltpu.VMEM((1, LANES), F32)],
        compiler_params=_cparams(("arbitrary",)),
        name="slots",
    )(ids, starts)


def _local_rows(tm):
    need = tm * TOP_K + N_EXPERTS * (SUBLANES - 1)
    return ((need + LANES - 1) // LANES) * LANES


BIG_PIECE = 4 * SUBLANES


def _segment_starts(seg, make_copy):
    def expert(e, carry):
        cnt = seg(e)
        lo = seg(LANES + e)
        off = seg(2 * LANES + e)
        n_big = lax.shift_right_logical(cnt, 5)
        n_small = lax.shift_right_logical(cnt & (BIG_PIECE - 1), 3)

        def big(j, carry2):
            d = j * BIG_PIECE
            make_copy(pl.multiple_of(lo + d, SUBLANES), pl.multiple_of(off + d, SUBLANES), BIG_PIECE).start()
            return carry2

        def small(j, carry2):
            d = n_big * BIG_PIECE + j * SUBLANES
            make_copy(pl.multiple_of(lo + d, SUBLANES), pl.multiple_of(off + d, SUBLANES), SUBLANES).start()
            return carry2

        lax.fori_loop(0, n_big, big, 0)
        lax.fori_loop(0, n_small, small, 0)
        return carry

    lax.fori_loop(0, N_EXPERTS, expert, 0)


def _segment_waits(total_rows, make_copy):
    def big(j, carry):
        make_copy(0, 0, BIG_PIECE).wait()
        return carry

    def small(j, carry):
        make_copy(0, 0, SUBLANES).wait()
        return carry

    lax.fori_loop(0, lax.shift_right_logical(total_rows, 5), big, 0)
    lax.fori_loop(0, lax.shift_right_logical(total_rows & (BIG_PIECE - 1), 3), small, 0)


def _dispatch_kernel(h_ref, lst_ref, seg_hbm, xs_ref, sbuf_ref, seg_smem, prev_smem, sem_seg, sem_rows, *, tm):
    i = pl.program_id(0)
    n = pl.num_programs(0)
    slot = i % 2
    rows = sbuf_ref.shape[1]
    cp = pltpu.make_async_copy(seg_hbm.at[i], seg_smem, sem_seg)
    cp.start()
    lst = lst_ref[...].astype(I32)
    rid = lax.broadcasted_iota(I32, (rows, tm), 0)
    perm = jnp.zeros((rows, tm), F32)
    for kk in range(TOP_K):
        perm = perm + jnp.where(rid == lst[kk:kk + 1, :], 1.0, 0.0)
    srt = _dot(perm.astype(BF16), h_ref[...])
    sbuf_ref[slot] = _pack_bf16_pairs(srt)
    cp.wait()

    def copy_from(s):
        def make_copy(lo, off, nrows):
            return pltpu.make_async_copy(sbuf_ref.at[s, pl.ds(lo, nrows), :], xs_ref.at[pl.ds(off, nrows), :],
                                         sem_rows.at[s])
        return make_copy

    _segment_starts(lambda k: seg_smem[k], copy_from(slot))

    @pl.when(i > 0)
    def _():
        _segment_waits(prev_smem[0], copy_from(1 - slot))

    prev_smem[0] = seg_smem[3 * LANES]

    @pl.when(i == n - 1)
    def _():
        _segment_waits(prev_smem[0], copy_from(slot))


def _dispatch(h, lst, seg2d, n_rows, tm):
    T = h.shape[0]
    rows = _local_rows(tm)
    return pl.pallas_call(
        functools.partial(_dispatch_kernel, tm=tm),
        grid=(T // tm,),
        in_specs=[
            pl.BlockSpec((tm, D_MODEL), lambda i: (i, 0)),
            pl.BlockSpec((SUBLANES, tm), lambda i: (0, i)),
            pl.BlockSpec(memory_space=pl.ANY),
        ],
        out_specs=pl.BlockSpec(memory_space=pl.ANY),
        out_shape=jax.ShapeDtypeStruct((n_rows, D_MODEL // 2), U32),
        scratch_shapes=[
            pltpu.VMEM((2, rows, D_MODEL // 2), U32),
            pltpu.SMEM((SEG_WORDS,), I32),
            pltpu.SMEM((1,), I32),
            pltpu.SemaphoreType.DMA,
            pltpu.SemaphoreType.DMA((2,)),
        ],
        compiler_params=_cparams(("arbitrary",)),
        name="dispatch",
    )(h, lst, seg2d)


def _experts_kernel(te_ref, nu_ref, nv_ref, xs_ref, wgu_ref, bgu_ref, wdn_ref, bdn_ref, ys_ref, wgu_bf, wdn_bf):
    i = pl.program_id(0)
    half = D_MODEL // 2

    @pl.when((i == 0) | (te_ref[i] != te_ref[jnp.maximum(i - 1, 0)]))
    def _():
        wgu_bf[...] = wgu_ref[0].astype(BF16)
        wdn_bf[...] = wdn_ref[0].astype(BF16)

    @pl.when(i < nu_ref[0])
    def _():
        live = lax.broadcasted_iota(I32, xs_ref.shape, 0) < nv_ref[i]
        lo, hi = _unpack_bf16_pairs(jnp.where(live, xs_ref[...], jnp.uint32(0)))
        gu = (_dot(lo.astype(BF16), wgu_bf[:half, :]) + _dot(hi.astype(BF16), wgu_bf[half:, :])
              + bgu_ref[0])
        gate = jnp.minimum(gu[:, :D_FF], SWIGLU_LIMIT)
        up = jnp.clip(gu[:, D_FF:], -SWIGLU_LIMIT, SWIGLU_LIMIT)
        act = (up + 1.0) * (gate * _sigmoid(SWIGLU_ALPHA * gate))
        y = _dot(act.astype(BF16), wdn_bf[...]) + bdn_ref[0]
        ys_ref[...] = _pack_bf16_pairs(y)

    @pl.when(i >= nu_ref[0])
    def _():
        ys_ref[...] = jnp.zeros_like(ys_ref)


def _experts(tile_expert, n_used, tile_valid, xs, w_gu, b_gu, w_dn, b_dn, tg):
    P = xs.shape[0]
    half = D_MODEL // 2
    grid_spec = pltpu.PrefetchScalarGridSpec(
        num_scalar_prefetch=3,
        grid=(P // tg,),
        in_specs=[
            pl.BlockSpec((tg, half), lambda i, te, nu, nv: (jnp.minimum(i, jnp.maximum(nu[0] - 1, 0)), 0)),
            pl.BlockSpec((1, D_MODEL, 2 * D_FF), lambda i, te, nu, nv: (te[i], 0, 0)),
            pl.BlockSpec((1, 1, 2 * D_FF), lambda i, te, nu, nv: (te[i], 0, 0)),
            pl.BlockSpec((1, D_FF, D_MODEL), lambda i, te, nu, nv: (te[i], 0, 0)),
            pl.BlockSpec((1, 1, D_MODEL), lambda i, te, nu, nv: (te[i], 0, 0)),
        ],
        out_specs=pl.BlockSpec((tg, half), lambda i, te, nu, nv: (i, 0)),
        scratch_shapes=[pltpu.VMEM((D_MODEL, 2 * D_FF), BF16), pltpu.VMEM((D_FF, D_MODEL), BF16)],
    )
    return pl.pallas_call(
        _experts_kernel,
        grid_spec=grid_spec,
        out_shape=jax.ShapeDtypeStruct((P, half), U32),
        compiler_params=_cparams(("arbitrary",)),
        name="experts",
    )(tile_expert, n_used, tile_valid, xs, w_gu, b_gu, w_dn, b_dn)


def _combine_kernel(seg_hbm, ys_hbm, ls_ref, tw_ref, x_ref, g_ref, out_ref, ybuf_ref, seg_smem, sem_seg, sem_rows, *,
                    tm, final_norm):
    i = pl.program_id(0)
    n = pl.num_programs(0)
    slot = i % 2
    rows = ybuf_ref.shape[1]

    def copy_into(s):
        def make_copy(lo, off, nrows):
            return pltpu.make_async_copy(ys_hbm.at[pl.ds(off, nrows), :], ybuf_ref.at[s, pl.ds(lo, nrows), :],
                                         sem_rows.at[s])
        return make_copy

    def request(step, s):
        cp = pltpu.make_async_copy(seg_hbm.at[step], seg_smem.at[s], sem_seg)
        cp.start()
        cp.wait()
        _segment_starts(lambda k: seg_smem[s, k], copy_into(s))

    @pl.when(i == 0)
    def _():
        ybuf_ref[...] = jnp.zeros_like(ybuf_ref)
        request(0, 0)

    @pl.when(i + 1 < n)
    def _():
        request(i + 1, 1 - slot)

    ls = ls_ref[...].astype(I32)
    tw = tw_ref[...]
    cid = lax.broadcasted_iota(I32, (tm, rows), 1)
    wmat = jnp.zeros((tm, rows), F32)
    for kk in range(TOP_K):
        wmat = wmat + jnp.where(cid == ls[:, kk:kk + 1], tw[:, kk:kk + 1], 0.0)
    wh, wl = _split_bf16(wmat)
    _segment_waits(seg_smem[slot, 3 * LANES], copy_into(slot))
    lo, hi = _unpack_bf16_pairs(ybuf_ref[slot])
    lo = lo.astype(BF16)
    hi = hi.astype(BF16)
    moe = jnp.concatenate([_dot(wh, lo) + _dot(wl, lo), _dot(wh, hi) + _dot(wl, hi)], axis=1)
    x3 = x_ref[...] + moe
    out_ref[...] = _rms(x3, g_ref[...]) if final_norm else x3


def _combine(seg2d, ys, ls, tw, x2, g, tm, final_norm):
    T = x2.shape[0]
    rows = _local_rows(tm)
    return pl.pallas_call(
        functools.partial(_combine_kernel, tm=tm, final_norm=final_norm),
        grid=(T // tm,),
        in_specs=[
            pl.BlockSpec(memory_space=pl.ANY),
            pl.BlockSpec(memory_space=pl.ANY),
            pl.BlockSpec((tm, LANES), lambda i: (i, 0)),
            pl.BlockSpec((tm, LANES), lambda i: (i, 0)),
            pl.BlockSpec((tm, D_MODEL), lambda i: (i, 0)),
            pl.BlockSpec((1, D_MODEL), lambda i: (0, 0)),
        ],
        out_specs=pl.BlockSpec((tm, D_MODEL), lambda i: (i, 0)),
        out_shape=jax.ShapeDtypeStruct((T, D_MODEL), F32),
        scratch_shapes=[
            pltpu.VMEM((2, rows, D_MODEL // 2), U32),
            pltpu.SMEM((2, SEG_WORDS), I32),
            pltpu.SemaphoreType.DMA,
            pltpu.SemaphoreType.DMA((2,)),
        ],
        compiler_params=_cparams(("arbitrary",)),
        name="combine",
    )(seg2d, ys, ls, tw, x2, g)


def _tiles(B, S):
    T = B * S
    return dict(
        tm_in=min(1024, T), tn_in=2048,
        ts=min(512, S),
        tq=min(1024, S),
        tm_proj=min(512, S),
        tm_route=min(256, T),
        tg=512,
    )


def _pad_lanes(a, n=LANES, value=0.0):
    return jnp.pad(a, ((0, 0), (0, n - a.shape[1])), constant_values=value)


def kernel(x, mem, norm_mix, w_in, conv_w, b_if, mlstm_gain, diff_lambda, diff_gain, w_branch_m, w_branch_d,
           b_gate, w_out, norm_xattn, norm_mem, wq_x, wkv_x, wo_x, norm_ffn, w_router, b_router, w_gu, b_gu,
           w_dn, b_dn, norm_final):
    B, S, D = x.shape
    n_mem = mem.shape[1]
    T = B * S
    depth = norm_mix.shape[0]
    tl = _tiles(B, S)
    x2d = x.reshape(T, D)
    mem2d = mem.reshape(B * n_mem, D)

    for l in range(depth):
        lam_init = 0.8 - 0.6 * math.exp(-0.3 * l)
        wl = w_in[l]
        if_lo = 2 * M_QK + 2 * M_V
        w_main = jnp.concatenate([wl[:, :if_lo], wl[:, if_lo + 2 * M_HEADS:]], axis=1).astype(BF16)
        w_if = wl[:, if_lo:if_lo + 2 * M_HEADS]
        w_ifp = _pad_lanes(w_if).astype(BF16)
        w_ift = w_if.T.astype(BF16)
        bif = _pad_lanes(b_if[l][None, :])
        bift = jnp.broadcast_to(b_if[l][:, None], (SUBLANES, LANES))

        z, zif, zift = _inproj(x2d, norm_mix[l][None, :], w_main, w_ifp, w_ift, tl["tm_in"], tl["tn_in"])
        hm = _mlstm(z, zif, zift, conv_w[l], bif, bift, mlstm_gain[l].reshape(1, M_V), B, S, tl["ts"])
        hd = _diffattn(z, diff_lambda[l], diff_gain[l][None, :], B, S, tl["tq"], lam_init)
        x1 = _merge(hm, hd, z, x2d, w_branch_m[l].astype(BF16), w_branch_d[l].astype(BF16),
                    w_out[l].astype(BF16), b_gate[l][None, :], tl["tm_proj"])

        kvmem = _memkv(mem2d, norm_mem[l][None, :], wkv_x[l].astype(BF16), n_mem)
        x2 = _xattn(x1, norm_xattn[l][None, :], wq_x[l].astype(BF16), kvmem, wo_x[l].astype(BF16),
                    S, n_mem, tl["tm_proj"])

        wr = _pad_lanes(w_router[l])
        wr_hi = wr.astype(BF16)
        wr_lo = (wr - wr_hi.astype(F32)).astype(BF16)
        br = _pad_lanes(b_router[l][None, :], value=-jnp.inf)
        tm_r = tl["tm_route"]
        tg = tl["tg"]
        hp, ids, tw, cnt = _router(x2, norm_ffn[l][None, :], wr_hi, wr_lo, br, tm_r)

        counts = cnt[0, :N_EXPERTS].astype(I32)
        padded = ((counts + tg - 1) // tg) * tg
        ends = jnp.cumsum(padded)
        starts = ends - padded
        max_rows = T * TOP_K + (T // tm_r) * N_EXPERTS * (SUBLANES - 1)
        n_tiles = -(-max_rows // tg) + N_EXPERTS
        tile_row0 = jnp.arange(n_tiles, dtype=I32) * tg
        tile_expert = jnp.minimum(jnp.sum((tile_row0[:, None] >= ends[None, :]).astype(I32), axis=1), N_EXPERTS - 1)
        n_used = (ends[-1] // tg).astype(I32).reshape(1)
        last_used = tile_expert[jnp.maximum(n_used[0] - 1, 0)]
        tile_expert = jnp.where(tile_row0 < ends[-1], tile_expert, last_used)
        tile_valid = jnp.clip((starts + counts)[tile_expert] - tile_row0, 0, tg).astype(I32)

        ls, lst, seg = _slots(ids, _pad_lanes(starts.astype(F32)[None, :]), tm_r)
        seg2d = seg.reshape(T // tm_r, SEG_WORDS)

        xs = _dispatch(hp, lst, seg2d, n_tiles * tg, tm_r)
        ys = _experts(tile_expert, n_used, tile_valid, xs, w_gu[l], b_gu[l][:, None, :], w_dn[l], b_dn[l][:, None, :],
                      tg)
        x2d = _combine(seg2d, ys, ls, tw, x2, norm_final[None, :], tm_r, final_norm=(l == depth - 1))
    return x2d.reshape(B, S, D)
```

```python
import functools
import math

import jax
import jax.numpy as jnp
from jax import lax
from jax.experimental import pallas as pl
from jax.experimental.pallas import tpu as pltpu

F32 = jnp.float32
BF16 = jnp.bfloat16
U32 = jnp.uint32
I32 = jnp.int32

EPS = 1e-6
CHUNK = 64
D_MODEL = 1024
M_HEADS = 4
M_DK = 128
M_DV = 256
M_QK = M_HEADS * M_DK
M_V = M_HEADS * M_DV
CONV_W = 4
D_HEADS = 8
D_DH = 64
D_QK = D_HEADS * 2 * D_DH
D_V = D_HEADS * 2 * D_DH
X_HEADS = 4
X_DH = D_MODEL // X_HEADS
N_EXPERTS = 32
TOP_K = 4
D_FF = D_MODEL
SWIGLU_LIMIT = 7.0
SWIGLU_ALPHA = 1.702

LANES = 128
SUBLANES = 8
N_MAIN = 2 * M_QK + 2 * M_V + 2 * D_QK + D_V + 2 * D_MODEL
OFF_QM, OFF_KM, OFF_VM, OFF_OM = 0, M_QK, 2 * M_QK, 2 * M_QK + M_V
OFF_QD = OFF_OM + M_V
OFF_KD = OFF_QD + D_QK
OFF_VD = OFF_KD + D_QK
OFF_G = OFF_VD + D_V

VMEM_LIMIT = 56 * 1024 * 1024


def _cparams(sem, vmem=VMEM_LIMIT):
    return pltpu.CompilerParams(dimension_semantics=sem, vmem_limit_bytes=vmem)


def _rms(x, g):
    return x * lax.rsqrt(jnp.mean(x * x, axis=-1, keepdims=True) + EPS) * g


def _split_bf16(x):
    hi = x.astype(BF16)
    lo = (x - hi.astype(F32)).astype(BF16)
    return hi, lo


def _dot(a, b):
    return jnp.dot(a, b, preferred_element_type=F32)


def _dot_nt(a, b):
    return lax.dot_general(a, b, (((1,), (1,)), ((), ())), preferred_element_type=F32)


def _sigmoid(x):
    return 1.0 / (1.0 + jnp.exp(-x))


def _log_sigmoid(x):
    return jnp.minimum(x, 0.0) - jnp.log(1.0 + jnp.exp(-jnp.abs(x)))


def _pack_bf16_pairs(x):
    w = x.shape[1] // 2
    u = lax.bitcast_convert_type(x, U32)
    r = (u + jnp.uint32(0x7FFF) + ((u >> 16) & jnp.uint32(1))) >> 16
    return r[:, :w] | (r[:, w:] << 16)


def _ceil_rows(x):
    return jnp.floor((x + (SUBLANES - 1)) * (1.0 / SUBLANES)) * SUBLANES


def _unpack_bf16_pairs(p):
    lo = lax.bitcast_convert_type(p << 16, F32)
    hi = lax.bitcast_convert_type(p & jnp.uint32(0xFFFF0000), F32)
    return lo, hi


def _inproj_kernel(x_ref, g_ref, w_ref, wif_ref, wift_ref, z_ref, zif_ref, zift_ref, hn_ref):
    @pl.when(pl.program_id(1) == 0)
    def _():
        hn = _rms(x_ref[...], g_ref[...]).astype(BF16)
        hn_ref[...] = hn
        zif_ref[...] = _dot(hn, wif_ref[...])
        zift_ref[...] = _dot_nt(wift_ref[...], hn)

    z_ref[...] = _dot(hn_ref[...], w_ref[...]).astype(BF16)


def _inproj(x2d, g, w_main, w_if, w_ift, tm, tn):
    T = x2d.shape[0]
    return pl.pallas_call(
        _inproj_kernel,
        grid=(T // tm, N_MAIN // tn),
        in_specs=[
            pl.BlockSpec((tm, D_MODEL), lambda i, j: (i, 0)),
            pl.BlockSpec((1, D_MODEL), lambda i, j: (0, 0)),
            pl.BlockSpec((D_MODEL, tn), lambda i, j: (0, j)),
            pl.BlockSpec((D_MODEL, LANES), lambda i, j: (0, 0)),
            pl.BlockSpec((SUBLANES, D_MODEL), lambda i, j: (0, 0)),
        ],
        out_specs=[
            pl.BlockSpec((tm, tn), lambda i, j: (i, j)),
            pl.BlockSpec((tm, LANES), lambda i, j: (i, 0)),
            pl.BlockSpec((SUBLANES, tm), lambda i, j: (0, i)),
        ],
        out_shape=[
            jax.ShapeDtypeStruct((T, N_MAIN), BF16),
            jax.ShapeDtypeStruct((T, LANES), F32),
            jax.ShapeDtypeStruct((SUBLANES, T), F32),
        ],
        scratch_shapes=[pltpu.VMEM((tm, D_MODEL), BF16)],
        compiler_params=_cparams(("arbitrary", "arbitrary")),
        name="inproj",
    )(x2d, g, w_main, w_if, w_ift)


def _mlstm_kernel(q_ref, k_ref, v_ref, om_ref, zif_ref, zift_ref, cw_ref, bif_ref, bift_ref, mg_ref,
                  out_ref, qc_ref, kc_ref, kt_ref, carry_ref, c_ref, n_ref, m_ref,
                  bd_ref, bdt_ref, brep_ref, grow_ref, brow_ref, *, ts):
    nchunk = ts // CHUNK
    L = CHUNK

    @pl.when(pl.program_id(1) == 0)
    def _():
        carry_ref[...] = jnp.zeros_like(carry_ref)
        c_ref[...] = jnp.zeros_like(c_ref)
        n_ref[...] = jnp.zeros_like(n_ref)
        m_ref[...] = jnp.zeros_like(m_ref)
        rt = lax.broadcasted_iota(I32, (ts, ts), 0)
        ct = lax.broadcasted_iota(I32, (ts, ts), 1)
        same = (rt // L) == (ct // L)
        bd_ref[...] = jnp.where(same, jnp.where(ct <= rt, 1.0, 0.0), 0.0).astype(BF16)
        bdt_ref[...] = jnp.where(same, jnp.where(rt <= ct, 1.0, 0.0), 0.0).astype(BF16)

    row8 = lax.broadcasted_iota(I32, (SUBLANES, M_QK), 0)

    def conv_silu(x, prev8, w):
        acc = w[CONV_W - 1:CONV_W, :] * x
        for s in range(1, CONV_W):
            xs = pltpu.roll(x, s, 0)
            top = jnp.where(row8 < s, pltpu.roll(prev8, s, 0), xs[0:SUBLANES])
            xs = jnp.concatenate([top, xs[SUBLANES:]], axis=0)
            acc = acc + w[CONV_W - 1 - s:CONV_W - s, :] * xs
        return acc * _sigmoid(acc)

    def conv_body(c, carry):
        r0 = pl.multiple_of(c * L, L)
        xq = q_ref[pl.ds(r0, L), :].astype(F32)
        xk = k_ref[pl.ds(r0, L), :].astype(F32)
        yq = conv_silu(xq, carry_ref[:, 0:M_QK], cw_ref[:, 0:M_QK]) * (M_DK ** -0.5)
        yk = conv_silu(xk, carry_ref[:, M_QK:2 * M_QK], cw_ref[:, M_QK:2 * M_QK])
        qc_ref[pl.ds(r0, L), :] = yq.astype(BF16)
        kc_ref[pl.ds(r0, L), :] = yk.astype(BF16)
        for h in range(M_HEADS):
            kt_ref[c, h] = yk[:, h * M_DK:(h + 1) * M_DK].T
        carry_ref[:, 0:M_QK] = xq[L - SUBLANES:L]
        carry_ref[:, M_QK:2 * M_QK] = xk[L - SUBLANES:L]
        return carry

    lax.fori_loop(0, nchunk, conv_body, 0)

    ti = lax.broadcasted_iota(I32, (L, L), 0)
    si = lax.broadcasted_iota(I32, (L, L), 1)
    causal = si <= ti
    lane_row = lax.broadcasted_iota(I32, (LANES, LANES), 0)
    ones_l = jnp.ones((L, LANES), BF16)

    lf_col = _log_sigmoid(zif_ref[...] + bif_ref[...])
    ch, cl = _split_bf16(lf_col)
    b_col_all = _dot(bd_ref[...], ch) + _dot(bd_ref[...], cl)
    bh, bl = _split_bf16(b_col_all)
    for h in range(M_HEADS):
        sel_f = jnp.where(lane_row == M_HEADS + h, 1.0, 0.0).astype(BF16)
        brep_ref[h] = _dot(bh, sel_f) + _dot(bl, sel_f)
    g_row_all = zift_ref[...] + bift_ref[:, 0:1]
    rh, rl = _split_bf16(_log_sigmoid(g_row_all))
    b_row_tile = _dot(rh, bdt_ref[...]) + _dot(rl, bdt_ref[...])
    for cc in range(nchunk):
        grow_ref[cc] = g_row_all[:, cc * L:(cc + 1) * L]
        brow_ref[cc] = b_row_tile[:, cc * L:(cc + 1) * L]

    def chunk_body(c, carry):
        r0 = pl.multiple_of(c * L, L)
        g_row = grow_ref[c]
        b_row_all = brow_ref[c]
        for h in range(M_HEADS):
            b_rep = brep_ref[h, pl.ds(r0, L), :]
            i_row = g_row[h:h + 1, :]
            b_row = b_row_all[M_HEADS + h:M_HEADS + h + 1, :]
            b_last = b_rep[L - 1:L, :]
            q = qc_ref[pl.ds(r0, L), h * M_DK:(h + 1) * M_DK]
            k = kc_ref[pl.ds(r0, L), h * M_DK:(h + 1) * M_DK]
            vext = jnp.concatenate([v_ref[pl.ds(r0, L), h * M_DV:(h + 1) * M_DV], ones_l], axis=1)
            dm = jnp.where(causal, b_rep[:, :L] - b_row + i_row, -jnp.inf)
            m_loc = jnp.max(dm, axis=-1, keepdims=True)
            s_loc = _dot_nt(q, k) * jnp.exp(dm - m_loc)
            pv = _dot(s_loc.astype(BF16), vext)
            gk_row = b_last[:, :L] - b_row + i_row
            g_max = jnp.max(gk_row, axis=-1, keepdims=True)
            kwt = (kt_ref[c, h] * jnp.exp(gk_row - g_max)).astype(BF16)
            kv = _dot(kwt, vext)
            m_prev = m_ref[h:h + 1, :]
            c_old = c_ref[h]
            n_old = n_ref[h]
            qcn = _dot(q, jnp.concatenate([c_old, n_old], axis=1).astype(BF16))
            inter = b_rep + m_prev
            m_t = jnp.maximum(inter, m_loc)
            w_inter = jnp.exp(inter - m_t)
            r_loc = jnp.exp(m_loc - m_t)
            den = r_loc * pv[:, M_DV:] + w_inter * qcn[:, M_DV:]
            inv = 1.0 / jnp.maximum(jnp.abs(den), jnp.exp(-m_t))
            hv = (jnp.concatenate([r_loc * inv] * 2, axis=1) * pv[:, :M_DV]
                  + jnp.concatenate([w_inter * inv] * 2, axis=1) * qcn[:, :M_DV])
            m_new = jnp.maximum(b_last + m_prev, g_max)
            decay = jnp.exp(b_last + m_prev - m_new)
            sc_loc = jnp.exp(g_max - m_new)
            c_ref[h] = (jnp.concatenate([decay] * 2, axis=1) * c_old
                        + jnp.concatenate([sc_loc] * 2, axis=1) * kv[:, :M_DV])
            n_ref[h] = decay * n_old + sc_loc * kv[:, M_DV:]
            m_ref[h:h + 1, :] = m_new
            hn = _rms(hv, mg_ref[:, h * M_DV:(h + 1) * M_DV])
            og = _sigmoid(om_ref[pl.ds(r0, L), h * M_DV:(h + 1) * M_DV].astype(F32))
            out_ref[pl.ds(r0, L), h * M_DV:(h + 1) * M_DV] = (og * hn).astype(BF16)
        return carry

    lax.fori_loop(0, nchunk, chunk_body, 0, unroll=4)


def _mlstm(z, zif, zift, conv_w, bif, bift, m_gain, B, S, ts):
    T = B * S
    nt = S // ts
    nck = ts // CHUNK
    row = lambda b, t: b * nt + t
    return pl.pallas_call(
        functools.partial(_mlstm_kernel, ts=ts),
        grid=(B, nt),
        in_specs=[
            pl.BlockSpec((ts, M_QK), lambda b, t: (row(b, t), OFF_QM // M_QK)),
            pl.BlockSpec((ts, M_QK), lambda b, t: (row(b, t), OFF_KM // M_QK)),
            pl.BlockSpec((ts, M_V), lambda b, t: (row(b, t), OFF_VM // M_V)),
            pl.BlockSpec((ts, M_V), lambda b, t: (row(b, t), OFF_OM // M_V)),
            pl.BlockSpec((ts, LANES), lambda b, t: (row(b, t), 0)),
            pl.BlockSpec((SUBLANES, ts), lambda b, t: (0, row(b, t))),
            pl.BlockSpec((CONV_W, 2 * M_QK), lambda b, t: (0, 0)),
            pl.BlockSpec((1, LANES), lambda b, t: (0, 0)),
            pl.BlockSpec((SUBLANES, LANES), lambda b, t: (0, 0)),
            pl.BlockSpec((1, M_V), lambda b, t: (0, 0)),
        ],
        out_specs=pl.BlockSpec((ts, M_V), lambda b, t: (row(b, t), 0)),
        out_shape=jax.ShapeDtypeStruct((T, M_V), BF16),
        scratch_shapes=[
            pltpu.VMEM((ts, M_QK), BF16),
            pltpu.VMEM((ts, M_QK), BF16),
            pltpu.VMEM((nck, M_HEADS, M_DK, CHUNK), F32),
            pltpu.VMEM((SUBLANES, 2 * M_QK), F32),
            pltpu.VMEM((M_HEADS, M_DK, M_DV), F32),
            pltpu.VMEM((M_HEADS, M_DK, LANES), F32),
            pltpu.VMEM((SUBLANES, LANES), F32),
            pltpu.VMEM((ts, ts), BF16),
            pltpu.VMEM((ts, ts), BF16),
            pltpu.VMEM((M_HEADS, ts, LANES), F32),
            pltpu.VMEM((nck, SUBLANES, CHUNK), F32),
            pltpu.VMEM((nck, SUBLANES, CHUNK), F32),
        ],
        compiler_params=_cparams(("arbitrary", "arbitrary")),
        name="mlstm",
    )(z, z, z, z, zif, zift, conv_w, bif, bift, m_gain)


def _diffattn_kernel(q_ref, k_ref, v_ref, lam_ref, gain_ref, out_ref,
                     m1_ref, a1_ref, m2_ref, a2_ref, *, tq, lam_init):
    qi = pl.program_id(2)
    w = 2 * D_DH
    q = q_ref[...]
    lane = lax.broadcasted_iota(I32, (1, w), 1)
    scale = jnp.asarray(D_DH ** -0.5, BF16)
    q1 = jnp.where(lane < D_DH, q, jnp.zeros_like(q)) * scale
    q2 = jnp.where(lane >= D_DH, q, jnp.zeros_like(q)) * scale
    ones = jnp.ones((tq, w), BF16)

    def online(s, vext, m_ref, a_ref, rows, first):
        s_max = jnp.max(s, axis=-1, keepdims=True)
        if first:
            m_new = jnp.broadcast_to(s_max, (s.shape[0], w))
        else:
            m_old = m_ref[rows, :]
            m_new = jnp.maximum(m_old, s_max)
        p = jnp.exp(s - jnp.concatenate([m_new] * (s.shape[1] // w), axis=1))
        pv = _dot(p.astype(BF16), vext)
        if first:
            a_ref[rows, :] = pv
        else:
            alpha = jnp.exp(m_old - m_new)
            a_ref[rows, :] = jnp.concatenate([alpha, alpha], axis=1) * a_ref[rows, :] + pv
        m_ref[rows, :] = m_new

    def block(k0, nk, rows=slice(None), mask=None, first=False):
        k = k_ref[pl.ds(k0, nk), :]
        vext = jnp.concatenate([v_ref[pl.ds(k0, nk), :], ones[:nk]], axis=1)
        s1 = _dot_nt(q1[rows], k)
        s2 = _dot_nt(q2[rows], k)
        if mask is not None:
            s1 = jnp.where(mask, s1, -jnp.inf)
            s2 = jnp.where(mask, s2, -jnp.inf)
        online(s1, vext, m1_ref, a1_ref, rows, first)
        online(s2, vext, m2_ref, a2_ref, rows, first)

    hq = tq // 2
    d0 = pl.multiple_of(qi * tq, tq)

    def chunk_mask(nq, q0):
        rq = (lax.broadcasted_iota(I32, (nq, hq), 0) + q0) // CHUNK
        ck = lax.broadcasted_iota(I32, (nq, hq), 1) // CHUNK
        return ck <= rq

    block(d0, hq, mask=chunk_mask(tq, 0), first=True)
    block(pl.multiple_of(d0 + hq, hq), hq, rows=slice(hq, tq), mask=chunk_mask(hq, 0))

    def body(jj, carry):
        block(pl.multiple_of(jj * tq, tq), tq)
        return carry

    lax.fori_loop(0, qi, body, 0)

    lp = lam_ref[...]
    lam = (jnp.exp(jnp.sum(lp[0:1, :] * lp[1:2, :], axis=-1, keepdims=True))
           - jnp.exp(jnp.sum(lp[2:3, :] * lp[3:4, :], axis=-1, keepdims=True)) + lam_init)
    a1 = a1_ref[...]
    a2 = a2_ref[...]
    o = a1[:, :w] / a1[:, w:] - lam * (a2[:, :w] / a2[:, w:])
    out_ref[...] = (_rms(o, gain_ref[...]) * (1.0 - lam_init)).astype(BF16)


def _diffattn(z, lam_p, d_gain, B, S, tq, lam_init):
    T = B * S
    nq = S // tq
    w = 2 * D_DH
    return pl.pallas_call(
        functools.partial(_diffattn_kernel, tq=tq, lam_init=lam_init),
        grid=(B, D_HEADS, nq),
        in_specs=[
            pl.BlockSpec((tq, w), lambda b, h, i: (b * nq + i, OFF_QD // w + h)),
            pl.BlockSpec((S, w), lambda b, h, i: (b, OFF_KD // w + h)),
            pl.BlockSpec((S, w), lambda b, h, i: (b, OFF_VD // w + h)),
            pl.BlockSpec((4, D_DH), lambda b, h, i: (0, 0)),
            pl.BlockSpec((1, w), lambda b, h, i: (0, 0)),
        ],
        out_specs=pl.BlockSpec((tq, w), lambda b, h, i: (b * nq + i, h)),
        out_shape=jax.ShapeDtypeStruct((T, D_V), BF16),
        scratch_shapes=[
            pltpu.VMEM((tq, w), F32), pltpu.VMEM((tq, 2 * w), F32),
            pltpu.VMEM((tq, w), F32), pltpu.VMEM((tq, 2 * w), F32),
        ],
        compiler_params=_cparams(("arbitrary", "arbitrary", "arbitrary")),
        name="diffattn",
    )(z, z, z, lam_p, d_gain)


def _merge_kernel(hm_ref, hd_ref, gz_ref, x_ref, wbm_ref, wbd_ref, wout_ref, bg_ref, out_ref):
    bm = _dot(hm_ref[...], wbm_ref[...])
    bd = _dot(hd_ref[...], wbd_ref[...])
    g = _sigmoid(gz_ref[...].astype(F32) + bg_ref[...])
    merged = g[:, :D_MODEL] * bm + g[:, D_MODEL:] * bd
    out_ref[...] = x_ref[...] + _dot(merged.astype(BF16), wout_ref[...])


def _merge(hm, hd, z, x2d, w_bm, w_bd, w_out, b_gate, tm):
    T = x2d.shape[0]
    full = lambda i: (0, 0)
    return pl.pallas_call(
        _merge_kernel,
        grid=(T // tm,),
        in_specs=[
            pl.BlockSpec((tm, M_V), lambda i: (i, 0)),
            pl.BlockSpec((tm, D_V), lambda i: (i, 0)),
            pl.BlockSpec((tm, 2 * D_MODEL), lambda i: (i, OFF_G // (2 * D_MODEL))),
            pl.BlockSpec((tm, D_MODEL), lambda i: (i, 0)),
            pl.BlockSpec((M_V, D_MODEL), full),
            pl.BlockSpec((D_V, D_MODEL), full),
            pl.BlockSpec((D_MODEL, D_MODEL), full),
            pl.BlockSpec((1, 2 * D_MODEL), full),
        ],
        out_specs=pl.BlockSpec((tm, D_MODEL), lambda i: (i, 0)),
        out_shape=jax.ShapeDtypeStruct((T, D_MODEL), F32),
        compiler_params=_cparams(("arbitrary",)),
        name="merge",
    )(hm, hd, z, x2d, w_bm, w_bd, w_out, b_gate)


def _memkv_kernel(mem_ref, g_ref, w_ref, out_ref):
    out_ref[...] = _dot(_rms(mem_ref[...], g_ref[...]).astype(BF16), w_ref[...]).astype(BF16)


def _memkv(mem2d, g, wkv, n_mem):
    R = mem2d.shape[0]
    return pl.pallas_call(
        _memkv_kernel,
        grid=(R // n_mem,),
        in_specs=[
            pl.BlockSpec((n_mem, D_MODEL), lambda i: (i, 0)),
            pl.BlockSpec((1, D_MODEL), lambda i: (0, 0)),
            pl.BlockSpec((D_MODEL, 2 * D_MODEL), lambda i: (0, 0)),
        ],
        out_specs=pl.BlockSpec((n_mem, 2 * D_MODEL), lambda i: (i, 0)),
        out_shape=jax.ShapeDtypeStruct((R, 2 * D_MODEL), BF16),
        compiler_params=_cparams(("arbitrary",)),
        name="memkv",
    )(mem2d, g, wkv)


def _xattn_kernel(x_ref, g_ref, wq_ref, kv_ref, wo_ref, out_ref, o_ref):
    x = x_ref[...]
    h = _rms(x, g_ref[...]).astype(BF16)
    q = (_dot(h, wq_ref[...]) * (X_DH ** -0.5)).astype(BF16)
    for hd in range(X_HEADS):
        qh = q[:, hd * X_DH:(hd + 1) * X_DH]
        kh = kv_ref[:, hd * X_DH:(hd + 1) * X_DH]
        vh = kv_ref[:, D_MODEL + hd * X_DH:D_MODEL + (hd + 1) * X_DH]
        s = _dot_nt(qh, kh)
        p = jnp.exp(s - jnp.max(s, axis=-1, keepdims=True))
        p = p / jnp.sum(p, axis=-1, keepdims=True)
        o_ref[:, hd * X_DH:(hd + 1) * X_DH] = _dot(p.astype(BF16), vh).astype(BF16)
    out_ref[...] = x + _dot(o_ref[...], wo_ref[...])


def _xattn(x1, g, wq, kvmem, wo, S, n_mem, tm):
    T = x1.shape[0]
    per_b = S // tm
    full = lambda i: (0, 0)
    return pl.pallas_call(
        _xattn_kernel,
        grid=(T // tm,),
        in_specs=[
            pl.BlockSpec((tm, D_MODEL), lambda i: (i, 0)),
            pl.BlockSpec((1, D_MODEL), full),
            pl.BlockSpec((D_MODEL, D_MODEL), full),
            pl.BlockSpec((n_mem, 2 * D_MODEL), lambda i: (i // per_b, 0)),
            pl.BlockSpec((D_MODEL, D_MODEL), full),
        ],
        out_specs=pl.BlockSpec((tm, D_MODEL), lambda i: (i, 0)),
        out_shape=jax.ShapeDtypeStruct((T, D_MODEL), F32),
        scratch_shapes=[pltpu.VMEM((tm, D_MODEL), BF16)],
        compiler_params=_cparams(("arbitrary",)),
        name="xattn",
    )(x1, g, wq, kvmem, wo)


def _router_kernel(x_ref, g_ref, wrh_ref, wrl_ref, br_ref, hp_ref, ids_ref, tw_ref, cnt_ref):
    @pl.when(pl.program_id(0) == 0)
    def _():
        cnt_ref[...] = jnp.zeros_like(cnt_ref)

    hn = _rms(x_ref[...], g_ref[...])
    hh, hl = _split_bf16(hn)
    hp_ref[...] = hh
    logits = _dot(hh, wrh_ref[...]) + _dot(hh, wrl_ref[...]) + _dot(hl, wrh_ref[...]) + br_ref[...]
    lane = lax.broadcasted_iota(I32, logits.shape, 1)
    lanef = lane.astype(F32)
    ids = jnp.zeros(logits.shape, F32)
    tw = jnp.zeros(logits.shape, F32)
    onehot = jnp.zeros(logits.shape, F32)
    v0 = None
    den = None
    for kk in range(TOP_K):
        mx = jnp.max(logits, axis=-1, keepdims=True)
        idx = jnp.min(jnp.where(logits == mx, lanef, float(LANES)), axis=-1, keepdims=True)
        sel = lanef == idx
        if kk == 0:
            v0 = mx
        e = jnp.exp(mx - v0)
        den = e if den is None else den + e
        ids = jnp.where(lane == kk, idx, ids)
        tw = jnp.where(lane == kk, e, tw)
        onehot = jnp.where(sel, 1.0, onehot)
        logits = jnp.where(sel, -jnp.inf, logits)
    ids_ref[...] = ids.astype(I32)
    tw_ref[...] = tw / den
    cnt_ref[...] += _ceil_rows(jnp.sum(onehot, axis=0, keepdims=True))


def _router(x2, g, wr_hi, wr_lo, b_r, tm):
    T = x2.shape[0]
    full = lambda i: (0, 0)
    return pl.pallas_call(
        _router_kernel,
        grid=(T // tm,),
        in_specs=[
            pl.BlockSpec((tm, D_MODEL), lambda i: (i, 0)),
            pl.BlockSpec((1, D_MODEL), full),
            pl.BlockSpec((D_MODEL, LANES), full),
            pl.BlockSpec((D_MODEL, LANES), full),
            pl.BlockSpec((1, LANES), full),
        ],
        out_specs=[
            pl.BlockSpec((tm, D_MODEL), lambda i: (i, 0)),
            pl.BlockSpec((tm, LANES), lambda i: (i, 0)),
            pl.BlockSpec((tm, LANES), lambda i: (i, 0)),
            pl.BlockSpec((1, LANES), full),
        ],
        out_shape=[
            jax.ShapeDtypeStruct((T, D_MODEL), BF16),
            jax.ShapeDtypeStruct((T, LANES), I32),
            jax.ShapeDtypeStruct((T, LANES), F32),
            jax.ShapeDtypeStruct((1, LANES), F32),
        ],
        compiler_params=_cparams(("arbitrary",)),
        name="router",
    )(x2, g, wr_hi, wr_lo, b_r)


SEG_WORDS = SUBLANES * LANES


def _slots_kernel(ids_ref, start_ref, ls_ref, lst_ref, seg_ref, run_ref):
    @pl.when(pl.program_id(0) == 0)
    def _():
        run_ref[...] = jnp.zeros_like(run_ref)

    ids = ids_ref[...]
    tm = ids.shape[0]
    lane = lax.broadcasted_iota(I32, ids.shape, 1)
    sels = [lane == ids[:, kk:kk + 1] for kk in range(TOP_K)]
    onehot = jnp.zeros(ids.shape, F32)
    for s in sels:
        onehot = jnp.where(s, 1.0, onehot)
    c8 = _ceil_rows(jnp.sum(onehot, axis=0, keepdims=True))
    er = lax.broadcasted_iota(I32, (LANES, LANES), 0)
    ec = lax.broadcasted_iota(I32, (LANES, LANES), 1)
    before = jnp.where(er < ec, 1.0, 0.0).astype(BF16)
    pieces = jnp.broadcast_to(c8 * (1.0 / SUBLANES), (SUBLANES, LANES)).astype(BF16)
    lo = _dot(pieces, before)[0:1, :] * SUBLANES
    r = lax.broadcasted_iota(I32, (tm, tm), 0)
    c = lax.broadcasted_iota(I32, (tm, tm), 1)
    strict = jnp.where(c < r, 1.0, 0.0).astype(BF16)
    slot = _dot(strict, onehot.astype(BF16)) + lo
    ls = jnp.zeros(ids.shape, F32)
    for kk, s in enumerate(sels):
        pk = jnp.sum(jnp.where(s, slot, 0.0), axis=-1, keepdims=True)
        ls = jnp.where(lane == kk, pk, ls)
    ls_ref[...] = ls
    hi = jnp.floor(ls * (1.0 / 32.0))
    rem = ls - 32.0 * hi
    pick = jnp.where(lax.broadcasted_iota(I32, (SUBLANES, LANES), 0) == lax.broadcasted_iota(I32, (SUBLANES, LANES), 1),
                     1.0, 0.0).astype(BF16)
    lst_ref[...] = 32.0 * _dot_nt(pick, hi.astype(BF16)) + _dot_nt(pick, rem.astype(BF16))
    row = lax.broadcasted_iota(I32, (SUBLANES, LANES), 0)
    off = start_ref[...] + run_ref[...]
    total = jnp.sum(c8, axis=-1, keepdims=True)
    seg = jnp.where(row == 0, c8, jnp.where(row == 1, lo, jnp.where(row == 2, off, jnp.where(row == 3, total, 0.0))))
    seg_ref[...] = seg.astype(I32)
    run_ref[...] += c8


def _slots(ids, starts, tm):
    T = ids.shape[0]
    nt = T // tm
    return pl.pallas_call(
        _slots_kernel,
        grid=(nt,),
        in_specs=[
            pl.BlockSpec((tm, LANES), lambda i: (i, 0)),
            pl.BlockSpec((1, LANES), lambda i: (0, 0)),
        ],
        out_specs=[
            pl.BlockSpec((tm, LANES), lambda i: (i, 0)),
            pl.BlockSpec((SUBLANES, tm), lambda i: (0, i)),
            pl.BlockSpec((SUBLANES, LANES), lambda i: (i, 0)),
        ],
        out_shape=[
            jax.ShapeDtypeStruct((T, LANES), F32),
            jax.ShapeDtypeStruct((SUBLANES, T), F32),
            jax.ShapeDtypeStruct((nt * SUBLANES, LANES), I32),
        ],
        scratch_shapes=[pltpu.VMEM((1, LANES), F32)],
        compiler_params=_cparams(("arbitrary",)),
        name="slots",
    )(ids, starts)


def _local_rows(tm):
    need = tm * TOP_K + N_EXPERTS * (SUBLANES - 1)
    return ((need + LANES - 1) // LANES) * LANES


BIG_PIECE = 4 * SUBLANES


def _segment_starts(seg, make_copy):
    def expert(e, carry):
        cnt = seg(e)
        lo = seg(LANES + e)
        off = seg(2 * LANES + e)
        n_big = lax.shift_right_logical(cnt, 5)
        n_small = lax.shift_right_logical(cnt & (BIG_PIECE - 1), 3)

        def big(j, carry2):
            d = j * BIG_PIECE
            make_copy(pl.multiple_of(lo + d, SUBLANES), pl.multiple_of(off + d, SUBLANES), BIG_PIECE).start()
            return carry2

        def small(j, carry2):
            d = n_big * BIG_PIECE + j * SUBLANES
            make_copy(pl.multiple_of(lo + d, SUBLANES), pl.multiple_of(off + d, SUBLANES), SUBLANES).start()
            return carry2

        lax.fori_loop(0, n_big, big, 0)
        lax.fori_loop(0, n_small, small, 0)
        return carry

    for e in range(N_EXPERTS):
        expert(e, 0)


def _segment_waits(total_rows, make_copy):
    def big(j, carry):
        make_copy(0, 0, BIG_PIECE).wait()
        return carry

    def small(j, carry):
        make_copy(0, 0, SUBLANES).wait()
        return carry

    lax.fori_loop(0, lax.shift_right_logical(total_rows, 5), big, 0)
    lax.fori_loop(0, lax.shift_right_logical(total_rows & (BIG_PIECE - 1), 3), small, 0)


def _dispatch_kernel(h_ref, lst_ref, seg_hbm, xs_ref, sbuf_ref, seg_smem, prev_smem, sem_seg, sem_rows, *, tm):
    i = pl.program_id(0)
    n = pl.num_programs(0)
    slot = i % 2
    rows = sbuf_ref.shape[1]
    cp = pltpu.make_async_copy(seg_hbm.at[i], seg_smem, sem_seg)
    cp.start()
    lst = lst_ref[...].astype(I32)
    rid = lax.broadcasted_iota(I32, (rows, tm), 0)
    perm = jnp.zeros((rows, tm), F32)
    for kk in range(TOP_K):
        perm = perm + jnp.where(rid == lst[kk:kk + 1, :], 1.0, 0.0)
    srt = _dot(perm.astype(BF16), h_ref[...])
    sbuf_ref[slot] = _pack_bf16_pairs(srt)
    cp.wait()

    def copy_from(s):
        def make_copy(lo, off, nrows):
            return pltpu.make_async_copy(sbuf_ref.at[s, pl.ds(lo, nrows), :], xs_ref.at[pl.ds(off, nrows), :],
                                         sem_rows.at[s])
        return make_copy

    _segment_starts(lambda k: seg_smem[k], copy_from(slot))

    @pl.when(i > 0)
    def _():
        _segment_waits(prev_smem[0], copy_from(1 - slot))

    prev_smem[0] = seg_smem[3 * LANES]

    @pl.when(i == n - 1)
    def _():
        _segment_waits(prev_smem[0], copy_from(slot))


def _dispatch(h, lst, seg2d, n_rows, tm):
    T = h.shape[0]
    rows = _local_rows(tm)
    return pl.pallas_call(
        functools.partial(_dispatch_kernel, tm=tm),
        grid=(T // tm,),
        in_specs=[
            pl.BlockSpec((tm, D_MODEL), lambda i: (i, 0)),
            pl.BlockSpec((SUBLANES, tm), lambda i: (0, i)),
            pl.BlockSpec(memory_space=pl.ANY),
        ],
        out_specs=pl.BlockSpec(memory_space=pl.ANY),
        out_shape=jax.ShapeDtypeStruct((n_rows, D_MODEL // 2), U32),
        scratch_shapes=[
            pltpu.VMEM((2, rows, D_MODEL // 2), U32),
            pltpu.SMEM((SEG_WORDS,), I32),
            pltpu.SMEM((1,), I32),
            pltpu.SemaphoreType.DMA,
            pltpu.SemaphoreType.DMA((2,)),
        ],
        compiler_params=_cparams(("arbitrary",)),
        name="dispatch",
    )(h, lst, seg2d)


def _experts_kernel(te_ref, nu_ref, nv_ref, xs_ref, wgu_ref, bgu_ref, wdn_ref, bdn_ref, ys_ref, wgu_bf, wdn_bf):
    i = pl.program_id(0)
    half = D_MODEL // 2

    @pl.when((i == 0) | (te_ref[i] != te_ref[jnp.maximum(i - 1, 0)]))
    def _():
        wgu_bf[...] = wgu_ref[0].astype(BF16)
        wdn_bf[...] = wdn_ref[0].astype(BF16)

    @pl.when(i < nu_ref[0])
    def _():
        live = lax.broadcasted_iota(I32, xs_ref.shape, 0) < nv_ref[i]
        lo, hi = _unpack_bf16_pairs(jnp.where(live, xs_ref[...], jnp.uint32(0)))
        gu = (_dot(lo.astype(BF16), wgu_bf[:half, :]) + _dot(hi.astype(BF16), wgu_bf[half:, :])
              + bgu_ref[0])
        gate = jnp.minimum(gu[:, :D_FF], SWIGLU_LIMIT)
        up = jnp.clip(gu[:, D_FF:], -SWIGLU_LIMIT, SWIGLU_LIMIT)
        act = (up + 1.0) * (gate * _sigmoid(SWIGLU_ALPHA * gate))
        y = _dot(act.astype(BF16), wdn_bf[...]) + bdn_ref[0]
        ys_ref[...] = _pack_bf16_pairs(y)

    @pl.when(i >= nu_ref[0])
    def _():
        ys_ref[...] = jnp.zeros_like(ys_ref)


def _experts(tile_expert, n_used, tile_valid, xs, w_gu, b_gu, w_dn, b_dn, tg):
    P = xs.shape[0]
    half = D_MODEL // 2
    grid_spec = pltpu.PrefetchScalarGridSpec(
        num_scalar_prefetch=3,
        grid=(P // tg,),
        in_specs=[
            pl.BlockSpec((tg, half), lambda i, te, nu, nv: (jnp.minimum(i, jnp.maximum(nu[0] - 1, 0)), 0)),
            pl.BlockSpec((1, D_MODEL, 2 * D_FF), lambda i, te, nu, nv: (te[i], 0, 0)),
            pl.BlockSpec((1, 1, 2 * D_FF), lambda i, te, nu, nv: (te[i], 0, 0)),
            pl.BlockSpec((1, D_FF, D_MODEL), lambda i, te, nu, nv: (te[i], 0, 0)),
            pl.BlockSpec((1, 1, D_MODEL), lambda i, te, nu, nv: (te[i], 0, 0)),
        ],
        out_specs=pl.BlockSpec((tg, half), lambda i, te, nu, nv: (i, 0)),
        scratch_shapes=[pltpu.VMEM((D_MODEL, 2 * D_FF), BF16), pltpu.VMEM((D_FF, D_MODEL), BF16)],
    )
    return pl.pallas_call(
        _experts_kernel,
        grid_spec=grid_spec,
        out_shape=jax.ShapeDtypeStruct((P, half), U32),
        compiler_params=_cparams(("arbitrary",)),
        name="experts",
    )(tile_expert, n_used, tile_valid, xs, w_gu, b_gu, w_dn, b_dn)


def _combine_kernel(seg_hbm, ys_hbm, ls_ref, tw_ref, x_ref, g_ref, out_ref, ybuf_ref, seg_smem, sem_seg, sem_rows, *,
                    tm, final_norm):
    i = pl.program_id(0)
    n = pl.num_programs(0)
    slot = i % 2
    rows = ybuf_ref.shape[1]

    def copy_into(s):
        def make_copy(lo, off, nrows):
            return pltpu.make_async_copy(ys_hbm.at[pl.ds(off, nrows), :], ybuf_ref.at[s, pl.ds(lo, nrows), :],
                                         sem_rows.at[s])
        return make_copy

    def request(step, s):
        cp = pltpu.make_async_copy(seg_hbm.at[step], seg_smem.at[s], sem_seg)
        cp.start()
        cp.wait()
        _segment_starts(lambda k: seg_smem[s, k], copy_into(s))

    @pl.when(i == 0)
    def _():
        ybuf_ref[...] = jnp.zeros_like(ybuf_ref)
        request(0, 0)

    @pl.when(i + 1 < n)
    def _():
        request(i + 1, 1 - slot)

    ls = ls_ref[...].astype(I32)
    tw = tw_ref[...]
    cid = lax.broadcasted_iota(I32, (tm, rows), 1)
    wmat = jnp.zeros((tm, rows), F32)
    for kk in range(TOP_K):
        wmat = wmat + jnp.where(cid == ls[:, kk:kk + 1], tw[:, kk:kk + 1], 0.0)
    wmat = wmat.astype(BF16)
    _segment_waits(seg_smem[slot, 3 * LANES], copy_into(slot))
    lo, hi = _unpack_bf16_pairs(ybuf_ref[slot])
    moe = jnp.concatenate([_dot(wmat, lo.astype(BF16)), _dot(wmat, hi.astype(BF16))], axis=1)
    x3 = x_ref[...] + moe
    out_ref[...] = _rms(x3, g_ref[...]) if final_norm else x3


def _combine(seg2d, ys, ls, tw, x2, g, tm, final_norm):
    T = x2.shape[0]
    rows = _local_rows(tm)
    return pl.pallas_call(
        functools.partial(_combine_kernel, tm=tm, final_norm=final_norm),
        grid=(T // tm,),
        in_specs=[
            pl.BlockSpec(memory_space=pl.ANY),
            pl.BlockSpec(memory_space=pl.ANY),
            pl.BlockSpec((tm, LANES), lambda i: (i, 0)),
            pl.BlockSpec((tm, LANES), lambda i: (i, 0)),
            pl.BlockSpec((tm, D_MODEL), lambda i: (i, 0)),
            pl.BlockSpec((1, D_MODEL), lambda i: (0, 0)),
        ],
        out_specs=pl.BlockSpec((tm, D_MODEL), lambda i: (i, 0)),
        out_shape=jax.ShapeDtypeStruct((T, D_MODEL), F32),
        scratch_shapes=[
            pltpu.VMEM((2, rows, D_MODEL // 2), U32),
            pltpu.SMEM((2, SEG_WORDS), I32),
            pltpu.SemaphoreType.DMA,
            pltpu.SemaphoreType.DMA((2,)),
        ],
        compiler_params=_cparams(("arbitrary",)),
        name="combine",
    )(seg2d, ys, ls, tw, x2, g)


def _tiles(B, S):
    T = B * S
    return dict(
        tm_in=min(1024, T), tn_in=2048,
        ts=min(512, S),
        tq=min(1024, S),
        tm_proj=min(512, S),
        tm_route=min(256, T),
        tg=512,
    )


def _pad_lanes(a, n=LANES, value=0.0):
    return jnp.pad(a, ((0, 0), (0, n - a.shape[1])), constant_values=value)


def kernel(x, mem, norm_mix, w_in, conv_w, b_if, mlstm_gain, diff_lambda, diff_gain, w_branch_m, w_branch_d,
           b_gate, w_out, norm_xattn, norm_mem, wq_x, wkv_x, wo_x, norm_ffn, w_router, b_router, w_gu, b_gu,
           w_dn, b_dn, norm_final):
    B, S, D = x.shape
    n_mem = mem.shape[1]
    T = B * S
    depth = norm_mix.shape[0]
    tl = _tiles(B, S)
    x2d = x.reshape(T, D)
    mem2d = mem.reshape(B * n_mem, D)

    for l in range(depth):
        lam_init = 0.8 - 0.6 * math.exp(-0.3 * l)
        wl = w_in[l]
        if_lo = 2 * M_QK + 2 * M_V
        w_main = jnp.concatenate([wl[:, :if_lo], wl[:, if_lo + 2 * M_HEADS:]], axis=1).astype(BF16)
        w_if = wl[:, if_lo:if_lo + 2 * M_HEADS]
        w_ifp = _pad_lanes(w_if).astype(BF16)
        w_ift = w_if.T.astype(BF16)
        bif = _pad_lanes(b_if[l][None, :])
        bift = jnp.broadcast_to(b_if[l][:, None], (SUBLANES, LANES))

        z, zif, zift = _inproj(x2d, norm_mix[l][None, :], w_main, w_ifp, w_ift, tl["tm_in"], tl["tn_in"])
        hm = _mlstm(z, zif, zift, conv_w[l], bif, bift, mlstm_gain[l].reshape(1, M_V), B, S, tl["ts"])
        hd = _diffattn(z, diff_lambda[l], diff_gain[l][None, :], B, S, tl["tq"], lam_init)
        x1 = _merge(hm, hd, z, x2d, w_branch_m[l].astype(BF16), w_branch_d[l].astype(BF16),
                    w_out[l].astype(BF16), b_gate[l][None, :], tl["tm_proj"])

        kvmem = _memkv(mem2d, norm_mem[l][None, :], wkv_x[l].astype(BF16), n_mem)
        x2 = _xattn(x1, norm_xattn[l][None, :], wq_x[l].astype(BF16), kvmem, wo_x[l].astype(BF16),
                    S, n_mem, tl["tm_proj"])

        wr = _pad_lanes(w_router[l])
        wr_hi = wr.astype(BF16)
        wr_lo = (wr - wr_hi.astype(F32)).astype(BF16)
        br = _pad_lanes(b_router[l][None, :], value=-jnp.inf)
        tm_r = tl["tm_route"]
        tg = tl["tg"]
        hp, ids, tw, cnt = _router(x2, norm_ffn[l][None, :], wr_hi, wr_lo, br, tm_r)

        counts = cnt[0, :N_EXPERTS].astype(I32)
        padded = ((counts + tg - 1) // tg) * tg
        ends = jnp.cumsum(padded)
        starts = ends - padded
        max_rows = T * TOP_K + (T // tm_r) * N_EXPERTS * (SUBLANES - 1)
        n_tiles = -(-max_rows // tg) + N_EXPERTS
        tile_row0 = jnp.arange(n_tiles, dtype=I32) * tg
        tile_expert = jnp.minimum(jnp.sum((tile_row0[:, None] >= ends[None, :]).astype(I32), axis=1), N_EXPERTS - 1)
        n_used = (ends[-1] // tg).astype(I32).reshape(1)
        last_used = tile_expert[jnp.maximum(n_used[0] - 1, 0)]
        tile_expert = jnp.where(tile_row0 < ends[-1], tile_expert, last_used)
        tile_valid = jnp.clip((starts + counts)[tile_expert] - tile_row0, 0, tg).astype(I32)

        ls, lst, seg = _slots(ids, _pad_lanes(starts.astype(F32)[None, :]), tm_r)
        seg2d = seg.reshape(T // tm_r, SEG_WORDS)

        xs = _dispatch(hp, lst, seg2d, n_tiles * tg, tm_r)
        ys = _experts(tile_expert, n_used, tile_valid, xs, w_gu[l], b_gu[l][:, None, :], w_dn[l], b_dn[l][:, None, :],
                      tg)
        x2d = _combine(seg2d, ys, ls, tw, x2, norm_final[None, :], tm_r, final_norm=(l == depth - 1))
    return x2d.reshape(B, S, D)
```

```python
import functools
import math

import jax
import jax.numpy as jnp
from jax import lax
from jax.experimental import pallas as pl
from jax.experimental.pallas import tpu as pltpu

F32 = jnp.float32
BF16 = jnp.bfloat16
U32 = jnp.uint32
I32 = jnp.int32

EPS = 1e-6
CHUNK = 64
D_MODEL = 1024
M_HEADS = 4
M_DK = 128
M_DV = 256
M_QK = M_HEADS * M_DK
M_V = M_HEADS * M_DV
CONV_W = 4
D_HEADS = 8
D_DH = 64
D_HP = 2
D_QK = D_HEADS * 2 * D_DH
D_V = D_HEADS * 2 * D_DH
X_HEADS = 4
X_DH = D_MODEL // X_HEADS
N_EXPERTS = 32
TOP_K = 4
D_FF = D_MODEL
SWIGLU_LIMIT = 7.0
SWIGLU_ALPHA = 1.702

LANES = 128
SUBLANES = 8
N_MAIN = 2 * M_QK + 2 * M_V + 2 * D_QK + D_V + 2 * D_MODEL
OFF_QM, OFF_KM, OFF_VM, OFF_OM = 0, M_QK, 2 * M_QK, 2 * M_QK + M_V
OFF_QD = OFF_OM + M_V
OFF_KD = OFF_QD + D_QK
OFF_VD = OFF_KD + D_QK
OFF_G = OFF_VD + D_V

VMEM_LIMIT = 56 * 1024 * 1024


def _cparams(sem, vmem=VMEM_LIMIT):
    return pltpu.CompilerParams(dimension_semantics=sem, vmem_limit_bytes=vmem)


def _rms(x, g):
    return x * lax.rsqrt(jnp.mean(x * x, axis=-1, keepdims=True) + EPS) * g


def _split_bf16(x):
    hi = x.astype(BF16)
    lo = (x - hi.astype(F32)).astype(BF16)
    return hi, lo


def _dot(a, b):
    return jnp.dot(a, b, preferred_element_type=F32)


def _dot_nt(a, b):
    return lax.dot_general(a, b, (((1,), (1,)), ((), ())), preferred_element_type=F32)


def _sigmoid(x):
    return 1.0 / (1.0 + jnp.exp(-x))


def _log_sigmoid(x):
    return jnp.minimum(x, 0.0) - jnp.log(1.0 + jnp.exp(-jnp.abs(x)))


def _pack_bf16_pairs(x):
    w = x.shape[1] // 2
    u = lax.bitcast_convert_type(x, U32)
    r = (u + jnp.uint32(0x7FFF) + ((u >> 16) & jnp.uint32(1))) >> 16
    return r[:, :w] | (r[:, w:] << 16)


def _ceil_rows(x):
    return jnp.floor((x + (SUBLANES - 1)) * (1.0 / SUBLANES)) * SUBLANES


def _unpack_bf16_pairs(p):
    lo = lax.bitcast_convert_type(p << 16, F32)
    hi = lax.bitcast_convert_type(p & jnp.uint32(0xFFFF0000), F32)
    return lo, hi


def _inproj_kernel(x_ref, g_ref, w_ref, wif_ref, wift_ref, z_ref, zif_ref, zift_ref, hn_ref):
    @pl.when(pl.program_id(1) == 0)
    def _():
        hn = _rms(x_ref[...], g_ref[...]).astype(BF16)
        hn_ref[...] = hn
        zif_ref[...] = _dot(hn, wif_ref[...])
        zift_ref[...] = _dot_nt(wift_ref[...], hn)

    z_ref[...] = _dot(hn_ref[...], w_ref[...]).astype(BF16)


def _inproj(x2d, g, w_main, w_if, w_ift, tm, tn):
    T = x2d.shape[0]
    return pl.pallas_call(
        _inproj_kernel,
        grid=(T // tm, N_MAIN // tn),
        in_specs=[
            pl.BlockSpec((tm, D_MODEL), lambda i, j: (i, 0)),
            pl.BlockSpec((1, D_MODEL), lambda i, j: (0, 0)),
            pl.BlockSpec((D_MODEL, tn), lambda i, j: (0, j)),
            pl.BlockSpec((D_MODEL, LANES), lambda i, j: (0, 0)),
            pl.BlockSpec((SUBLANES, D_MODEL), lambda i, j: (0, 0)),
        ],
        out_specs=[
            pl.BlockSpec((tm, tn), lambda i, j: (i, j)),
            pl.BlockSpec((tm, LANES), lambda i, j: (i, 0)),
            pl.BlockSpec((SUBLANES, tm), lambda i, j: (0, i)),
        ],
        out_shape=[
            jax.ShapeDtypeStruct((T, N_MAIN), BF16),
            jax.ShapeDtypeStruct((T, LANES), F32),
            jax.ShapeDtypeStruct((SUBLANES, T), F32),
        ],
        scratch_shapes=[pltpu.VMEM((tm, D_MODEL), BF16)],
        compiler_params=_cparams(("arbitrary", "arbitrary")),
        name="inproj",
    )(x2d, g, w_main, w_if, w_ift)


def _mlstm_kernel(q_ref, k_ref, v_ref, om_ref, zif_ref, zift_ref, cw_ref, bif_ref, bift_ref, mg_ref,
                  out_ref, qc_ref, kc_ref, kt_ref, carry_ref, c_ref, n_ref, m_ref,
                  bd_ref, bdt_ref, brep_ref, grow_ref, brow_ref, *, ts):
    nchunk = ts // CHUNK
    L = CHUNK

    @pl.when(pl.program_id(1) == 0)
    def _():
        carry_ref[...] = jnp.zeros_like(carry_ref)
        c_ref[...] = jnp.zeros_like(c_ref)
        n_ref[...] = jnp.zeros_like(n_ref)
        m_ref[...] = jnp.zeros_like(m_ref)
        rt = lax.broadcasted_iota(I32, (ts, ts), 0)
        ct = lax.broadcasted_iota(I32, (ts, ts), 1)
        same = (rt // L) == (ct // L)
        bd_ref[...] = jnp.where(same, jnp.where(ct <= rt, 1.0, 0.0), 0.0).astype(BF16)
        bdt_ref[...] = jnp.where(same, jnp.where(rt <= ct, 1.0, 0.0), 0.0).astype(BF16)

    row8 = lax.broadcasted_iota(I32, (SUBLANES, M_QK), 0)

    def conv_silu(x, prev8, w):
        acc = w[CONV_W - 1:CONV_W, :] * x
        for s in range(1, CONV_W):
            xs = pltpu.roll(x, s, 0)
            top = jnp.where(row8 < s, pltpu.roll(prev8, s, 0), xs[0:SUBLANES])
            xs = jnp.concatenate([top, xs[SUBLANES:]], axis=0)
            acc = acc + w[CONV_W - 1 - s:CONV_W - s, :] * xs
        return acc * _sigmoid(acc)

    def conv_body(c, carry):
        r0 = pl.multiple_of(c * L, L)
        xq = q_ref[pl.ds(r0, L), :].astype(F32)
        xk = k_ref[pl.ds(r0, L), :].astype(F32)
        yq = conv_silu(xq, carry_ref[:, 0:M_QK], cw_ref[:, 0:M_QK]) * (M_DK ** -0.5)
        yk = conv_silu(xk, carry_ref[:, M_QK:2 * M_QK], cw_ref[:, M_QK:2 * M_QK])
        qc_ref[pl.ds(r0, L), :] = yq.astype(BF16)
        kc_ref[pl.ds(r0, L), :] = yk.astype(BF16)
        for h in range(M_HEADS):
            kt_ref[c, h] = yk[:, h * M_DK:(h + 1) * M_DK].T
        carry_ref[:, 0:M_QK] = xq[L - SUBLANES:L]
        carry_ref[:, M_QK:2 * M_QK] = xk[L - SUBLANES:L]
        return carry

    lax.fori_loop(0, nchunk, conv_body, 0)

    ti = lax.broadcasted_iota(I32, (L, L), 0)
    si = lax.broadcasted_iota(I32, (L, L), 1)
    causal = si <= ti
    lane_row = lax.broadcasted_iota(I32, (LANES, LANES), 0)
    ones_l = jnp.ones((L, LANES), BF16)

    lf_col = _log_sigmoid(zif_ref[...] + bif_ref[...])
    ch, cl = _split_bf16(lf_col)
    b_col_all = _dot(bd_ref[...], ch) + _dot(bd_ref[...], cl)
    bh, bl = _split_bf16(b_col_all)
    for h in range(M_HEADS):
        sel_f = jnp.where(lane_row == M_HEADS + h, 1.0, 0.0).astype(BF16)
        brep_ref[h] = _dot(bh, sel_f) + _dot(bl, sel_f)
    g_row_all = zift_ref[...] + bift_ref[:, 0:1]
    rh, rl = _split_bf16(_log_sigmoid(g_row_all))
    b_row_tile = _dot(rh, bdt_ref[...]) + _dot(rl, bdt_ref[...])
    for cc in range(nchunk):
        grow_ref[cc] = g_row_all[:, cc * L:(cc + 1) * L]
        brow_ref[cc] = b_row_tile[:, cc * L:(cc + 1) * L]

    def chunk_body(c, carry):
        r0 = pl.multiple_of(c * L, L)
        g_row = grow_ref[c]
        b_row_all = brow_ref[c]
        for h in range(M_HEADS):
            b_rep = brep_ref[h, pl.ds(r0, L), :]
            i_row = g_row[h:h + 1, :]
            b_row = b_row_all[M_HEADS + h:M_HEADS + h + 1, :]
            b_last = b_rep[L - 1:L, :]
            q = qc_ref[pl.ds(r0, L), h * M_DK:(h + 1) * M_DK]
            k = kc_ref[pl.ds(r0, L), h * M_DK:(h + 1) * M_DK]
            vext = jnp.concatenate([v_ref[pl.ds(r0, L), h * M_DV:(h + 1) * M_DV], ones_l], axis=1)
            dm = jnp.where(causal, b_rep[:, :L] - b_row + i_row, -jnp.inf)
            m_loc = jnp.max(dm, axis=-1, keepdims=True)
            s_loc = _dot_nt(q, k) * jnp.exp(dm - m_loc)
            pv = _dot(s_loc.astype(BF16), vext)
            gk_row = b_last[:, :L] - b_row + i_row
            g_max = jnp.max(gk_row, axis=-1, keepdims=True)
            kwt = (kt_ref[c, h] * jnp.exp(gk_row - g_max)).astype(BF16)
            kv = _dot(kwt, vext)
            m_prev = m_ref[h:h + 1, :]
            c_old = c_ref[h]
            n_old = n_ref[h]
            qcn = _dot(q, jnp.concatenate([c_old, n_old], axis=1).astype(BF16))
            inter = b_rep + m_prev
            m_t = jnp.maximum(inter, m_loc)
            w_inter = jnp.exp(inter - m_t)
            r_loc = jnp.exp(m_loc - m_t)
            den = r_loc * pv[:, M_DV:] + w_inter * qcn[:, M_DV:]
            inv = 1.0 / jnp.maximum(jnp.abs(den), jnp.exp(-m_t))
            hv = (jnp.concatenate([r_loc * inv] * 2, axis=1) * pv[:, :M_DV]
                  + jnp.concatenate([w_inter * inv] * 2, axis=1) * qcn[:, :M_DV])
            m_new = jnp.maximum(b_last + m_prev, g_max)
            decay = jnp.exp(b_last + m_prev - m_new)
            sc_loc = jnp.exp(g_max - m_new)
            c_ref[h] = (jnp.concatenate([decay] * 2, axis=1) * c_old
                        + jnp.concatenate([sc_loc] * 2, axis=1) * kv[:, :M_DV])
            n_ref[h] = decay * n_old + sc_loc * kv[:, M_DV:]
            m_ref[h:h + 1, :] = m_new
            hn = _rms(hv, mg_ref[:, h * M_DV:(h + 1) * M_DV])
            og = _sigmoid(om_ref[pl.ds(r0, L), h * M_DV:(h + 1) * M_DV].astype(F32))
            out_ref[pl.ds(r0, L), h * M_DV:(h + 1) * M_DV] = (og * hn).astype(BF16)
        return carry

    lax.fori_loop(0, nchunk, chunk_body, 0, unroll=4)


def _mlstm(z, zif, zift, conv_w, bif, bift, m_gain, B, S, ts):
    T = B * S
    nt = S // ts
    nck = ts // CHUNK
    row = lambda b, t: b * nt + t
    return pl.pallas_call(
        functools.partial(_mlstm_kernel, ts=ts),
        grid=(B, nt),
        in_specs=[
            pl.BlockSpec((ts, M_QK), lambda b, t: (row(b, t), OFF_QM // M_QK)),
            pl.BlockSpec((ts, M_QK), lambda b, t: (row(b, t), OFF_KM // M_QK)),
            pl.BlockSpec((ts, M_V), lambda b, t: (row(b, t), OFF_VM // M_V)),
            pl.BlockSpec((ts, M_V), lambda b, t: (row(b, t), OFF_OM // M_V)),
            pl.BlockSpec((ts, LANES), lambda b, t: (row(b, t), 0)),
            pl.BlockSpec((SUBLANES, ts), lambda b, t: (0, row(b, t))),
            pl.BlockSpec((CONV_W, 2 * M_QK), lambda b, t: (0, 0)),
            pl.BlockSpec((1, LANES), lambda b, t: (0, 0)),
            pl.BlockSpec((SUBLANES, LANES), lambda b, t: (0, 0)),
            pl.BlockSpec((1, M_V), lambda b, t: (0, 0)),
        ],
        out_specs=pl.BlockSpec((ts, M_V), lambda b, t: (row(b, t), 0)),
        out_shape=jax.ShapeDtypeStruct((T, M_V), BF16),
        scratch_shapes=[
            pltpu.VMEM((ts, M_QK), BF16),
            pltpu.VMEM((ts, M_QK), BF16),
            pltpu.VMEM((nck, M_HEADS, M_DK, CHUNK), F32),
            pltpu.VMEM((SUBLANES, 2 * M_QK), F32),
            pltpu.VMEM((M_HEADS, M_DK, M_DV), F32),
            pltpu.VMEM((M_HEADS, M_DK, LANES), F32),
            pltpu.VMEM((SUBLANES, LANES), F32),
            pltpu.VMEM((ts, ts), BF16),
            pltpu.VMEM((ts, ts), BF16),
            pltpu.VMEM((M_HEADS, ts, LANES), F32),
            pltpu.VMEM((nck, SUBLANES, CHUNK), F32),
            pltpu.VMEM((nck, SUBLANES, CHUNK), F32),
        ],
        compiler_params=_cparams(("arbitrary", "arbitrary")),
        name="mlstm",
    )(z, z, z, z, zif, zift, conv_w, bif, bift, m_gain)


def _diffattn_kernel(q_ref, k_ref, v_ref, lam_ref, gain_ref, out_ref, m_ref, a_ref, *, tq, lam_init):
    qi = pl.program_id(2)
    w = 2 * D_DH
    lane = lax.broadcasted_iota(I32, (1, w), 1)
    scale = jnp.asarray(D_DH ** -0.5, BF16)
    qs = []
    for hh in range(D_HP):
        q = q_ref[:, hh * w:(hh + 1) * w]
        qs.append((jnp.where(lane < D_DH, q, jnp.zeros_like(q)) * scale,
                   jnp.where(lane >= D_DH, q, jnp.zeros_like(q)) * scale))
    ones = jnp.ones((tq, w), BF16)

    def online(s, vext, idx, rows, first):
        s_max = jnp.max(s, axis=-1, keepdims=True)
        if first:
            m_new = jnp.broadcast_to(s_max, (s.shape[0], w))
        else:
            m_old = m_ref[idx, rows, :]
            m_new = jnp.maximum(m_old, s_max)
        p = jnp.exp(s - jnp.concatenate([m_new] * (s.shape[1] // w), axis=1))
        pv = _dot(p.astype(BF16), vext)
        if first:
            a_ref[idx, rows, :] = pv
        else:
            alpha = jnp.exp(m_old - m_new)
            a_ref[idx, rows, :] = jnp.concatenate([alpha, alpha], axis=1) * a_ref[idx, rows, :] + pv
        m_ref[idx, rows, :] = m_new

    def block(k0, nk, rows=slice(None), mask=None, first=False):
        scores = []
        for hh in range(D_HP):
            k = k_ref[pl.ds(k0, nk), hh * w:(hh + 1) * w]
            for comp in range(2):
                s = _dot_nt(qs[hh][comp][rows], k)
                scores.append(s if mask is None else jnp.where(mask, s, -jnp.inf))
        for hh in range(D_HP):
            vext = jnp.concatenate([v_ref[pl.ds(k0, nk), hh * w:(hh + 1) * w], ones[:nk]], axis=1)
            for comp in range(2):
                online(scores[2 * hh + comp], vext, 2 * hh + comp, rows, first)

    hq = tq // 2
    d0 = pl.multiple_of(qi * tq, tq)

    def chunk_mask(nq, q0):
        rq = (lax.broadcasted_iota(I32, (nq, hq), 0) + q0) // CHUNK
        ck = lax.broadcasted_iota(I32, (nq, hq), 1) // CHUNK
        return ck <= rq

    block(d0, hq, mask=chunk_mask(tq, 0), first=True)
    block(pl.multiple_of(d0 + hq, hq), hq, rows=slice(hq, tq), mask=chunk_mask(hq, 0))

    def body(jj, carry):
        block(pl.multiple_of(jj * tq, tq), tq)
        return carry

    lax.fori_loop(0, qi, body, 0)

    lp = lam_ref[...]
    lam = (jnp.exp(jnp.sum(lp[0:1, :] * lp[1:2, :], axis=-1, keepdims=True))
           - jnp.exp(jnp.sum(lp[2:3, :] * lp[3:4, :], axis=-1, keepdims=True)) + lam_init)
    for hh in range(D_HP):
        a1 = a_ref[2 * hh]
        a2 = a_ref[2 * hh + 1]
        o = a1[:, :w] / a1[:, w:] - lam * (a2[:, :w] / a2[:, w:])
        out_ref[:, hh * w:(hh + 1) * w] = (_rms(o, gain_ref[...]) * (1.0 - lam_init)).astype(BF16)


def _diffattn(z, lam_p, d_gain, B, S, tq, lam_init):
    T = B * S
    nq = S // tq
    w = 2 * D_DH
    wp = D_HP * w
    return pl.pallas_call(
        functools.partial(_diffattn_kernel, tq=tq, lam_init=lam_init),
        grid=(B, D_HEADS // D_HP, nq),
        in_specs=[
            pl.BlockSpec((tq, wp), lambda b, h, i: (b * nq + i, OFF_QD // wp + h)),
            pl.BlockSpec((S, wp), lambda b, h, i: (b, OFF_KD // wp + h)),
            pl.BlockSpec((S, wp), lambda b, h, i: (b, OFF_VD // wp + h)),
            pl.BlockSpec((4, D_DH), lambda b, h, i: (0, 0)),
            pl.BlockSpec((1, w), lambda b, h, i: (0, 0)),
        ],
        out_specs=pl.BlockSpec((tq, wp), lambda b, h, i: (b * nq + i, h)),
        out_shape=jax.ShapeDtypeStruct((T, D_V), BF16),
        scratch_shapes=[
            pltpu.VMEM((2 * D_HP, tq, w), F32), pltpu.VMEM((2 * D_HP, tq, 2 * w), F32),
        ],
        compiler_params=_cparams(("arbitrary", "arbitrary", "arbitrary")),
        name="diffattn",
    )(z, z, z, lam_p, d_gain)


def _merge_kernel(hm_ref, hd_ref, gz_ref, x_ref, wbm_ref, wbd_ref, wout_ref, bg_ref, out_ref):
    bm = _dot(hm_ref[...], wbm_ref[...])
    bd = _dot(hd_ref[...], wbd_ref[...])
    g = _sigmoid(gz_ref[...].astype(F32) + bg_ref[...])
    merged = g[:, :D_MODEL] * bm + g[:, D_MODEL:] * bd
    out_ref[...] = x_ref[...] + _dot(merged.astype(BF16), wout_ref[...])


def _merge(hm, hd, z, x2d, w_bm, w_bd, w_out, b_gate, tm):
    T = x2d.shape[0]
    full = lambda i: (0, 0)
    return pl.pallas_call(
        _merge_kernel,
        grid=(T // tm,),
        in_specs=[
            pl.BlockSpec((tm, M_V), lambda i: (i, 0)),
            pl.BlockSpec((tm, D_V), lambda i: (i, 0)),
            pl.BlockSpec((tm, 2 * D_MODEL), lambda i: (i, OFF_G // (2 * D_MODEL))),
            pl.BlockSpec((tm, D_MODEL), lambda i: (i, 0)),
            pl.BlockSpec((M_V, D_MODEL), full),
            pl.BlockSpec((D_V, D_MODEL), full),
            pl.BlockSpec((D_MODEL, D_MODEL), full),
            pl.BlockSpec((1, 2 * D_MODEL), full),
        ],
        out_specs=pl.BlockSpec((tm, D_MODEL), lambda i: (i, 0)),
        out_shape=jax.ShapeDtypeStruct((T, D_MODEL), F32),
        compiler_params=_cparams(("arbitrary",)),
        name="merge",
    )(hm, hd, z, x2d, w_bm, w_bd, w_out, b_gate)


def _memkv_kernel(mem_ref, g_ref, w_ref, out_ref):
    out_ref[...] = _dot(_rms(mem_ref[...], g_ref[...]).astype(BF16), w_ref[...]).astype(BF16)


def _memkv(mem2d, g, wkv, n_mem):
    R = mem2d.shape[0]
    return pl.pallas_call(
        _memkv_kernel,
        grid=(R // n_mem,),
        in_specs=[
            pl.BlockSpec((n_mem, D_MODEL), lambda i: (i, 0)),
            pl.BlockSpec((1, D_MODEL), lambda i: (0, 0)),
            pl.BlockSpec((D_MODEL, 2 * D_MODEL), lambda i: (0, 0)),
        ],
        out_specs=pl.BlockSpec((n_mem, 2 * D_MODEL), lambda i: (i, 0)),
        out_shape=jax.ShapeDtypeStruct((R, 2 * D_MODEL), BF16),
        compiler_params=_cparams(("arbitrary",)),
        name="memkv",
    )(mem2d, g, wkv)


def _xattn_kernel(x_ref, g_ref, wq_ref, kv_ref, wo_ref, out_ref, o_ref):
    x = x_ref[...]
    h = _rms(x, g_ref[...]).astype(BF16)
    q = (_dot(h, wq_ref[...]) * (X_DH ** -0.5)).astype(BF16)
    for hd in range(X_HEADS):
        qh = q[:, hd * X_DH:(hd + 1) * X_DH]
        kh = kv_ref[:, hd * X_DH:(hd + 1) * X_DH]
        vh = kv_ref[:, D_MODEL + hd * X_DH:D_MODEL + (hd + 1) * X_DH]
        s = _dot_nt(qh, kh)
        p = jnp.exp(s - jnp.max(s, axis=-1, keepdims=True))
        p = p / jnp.sum(p, axis=-1, keepdims=True)
        o_ref[:, hd * X_DH:(hd + 1) * X_DH] = _dot(p.astype(BF16), vh).astype(BF16)
    out_ref[...] = x + _dot(o_ref[...], wo_ref[...])


def _xattn(x1, g, wq, kvmem, wo, S, n_mem, tm):
    T = x1.shape[0]
    per_b = S // tm
    full = lambda i: (0, 0)
    return pl.pallas_call(
        _xattn_kernel,
        grid=(T // tm,),
        in_specs=[
            pl.BlockSpec((tm, D_MODEL), lambda i: (i, 0)),
            pl.BlockSpec((1, D_MODEL), full),
            pl.BlockSpec((D_MODEL, D_MODEL), full),
            pl.BlockSpec((n_mem, 2 * D_MODEL), lambda i: (i // per_b, 0)),
            pl.BlockSpec((D_MODEL, D_MODEL), full),
        ],
        out_specs=pl.BlockSpec((tm, D_MODEL), lambda i: (i, 0)),
        out_shape=jax.ShapeDtypeStruct((T, D_MODEL), F32),
        scratch_shapes=[pltpu.VMEM((tm, D_MODEL), BF16)],
        compiler_params=_cparams(("arbitrary",)),
        name="xattn",
    )(x1, g, wq, kvmem, wo)


def _router_kernel(x_ref, g_ref, wrh_ref, wrl_ref, br_ref, hp_ref, ids_ref, tw_ref, cnt_ref):
    @pl.when(pl.program_id(0) == 0)
    def _():
        cnt_ref[...] = jnp.zeros_like(cnt_ref)

    hn = _rms(x_ref[...], g_ref[...])
    hh, hl = _split_bf16(hn)
    hp_ref[...] = hh
    logits = _dot(hh, wrh_ref[...]) + _dot(hh, wrl_ref[...]) + _dot(hl, wrh_ref[...]) + br_ref[...]
    lane = lax.broadcasted_iota(I32, logits.shape, 1)
    lanef = lane.astype(F32)
    ids = jnp.zeros(logits.shape, F32)
    tw = jnp.zeros(logits.shape, F32)
    onehot = jnp.zeros(logits.shape, F32)
    v0 = None
    den = None
    for kk in range(TOP_K):
        mx = jnp.max(logits, axis=-1, keepdims=True)
        idx = jnp.min(jnp.where(logits == mx, lanef, float(LANES)), axis=-1, keepdims=True)
        sel = lanef == idx
        if kk == 0:
            v0 = mx
        e = jnp.exp(mx - v0)
        den = e if den is None else den + e
        ids = jnp.where(lane == kk, idx, ids)
        tw = jnp.where(lane == kk, e, tw)
        onehot = jnp.where(sel, 1.0, onehot)
        logits = jnp.where(sel, -jnp.inf, logits)
    ids_ref[...] = ids.astype(I32)
    tw_ref[...] = tw / den
    cnt_ref[...] += _ceil_rows(jnp.sum(onehot, axis=0, keepdims=True))


def _router(x2, g, wr_hi, wr_lo, b_r, tm):
    T = x2.shape[0]
    full = lambda i: (0, 0)
    return pl.pallas_call(
        _router_kernel,
        grid=(T // tm,),
        in_specs=[
            pl.BlockSpec((tm, D_MODEL), lambda i: (i, 0)),
            pl.BlockSpec((1, D_MODEL), full),
            pl.BlockSpec((D_MODEL, LANES), full),
            pl.BlockSpec((D_MODEL, LANES), full),
            pl.BlockSpec((1, LANES), full),
        ],
        out_specs=[
            pl.BlockSpec((tm, D_MODEL), lambda i: (i, 0)),
            pl.BlockSpec((tm, LANES), lambda i: (i, 0)),
            pl.BlockSpec((tm, LANES), lambda i: (i, 0)),
            pl.BlockSpec((1, LANES), full),
        ],
        out_shape=[
            jax.ShapeDtypeStruct((T, D_MODEL), BF16),
            jax.ShapeDtypeStruct((T, LANES), I32),
            jax.ShapeDtypeStruct((T, LANES), F32),
            jax.ShapeDtypeStruct((1, LANES), F32),
        ],
        compiler_params=_cparams(("arbitrary",)),
        name="router",
    )(x2, g, wr_hi, wr_lo, b_r)


SEG_WORDS = SUBLANES * LANES


def _slots_kernel(ids_ref, start_ref, ls_ref, lst_ref, seg_ref, run_ref):
    @pl.when(pl.program_id(0) == 0)
    def _():
        run_ref[...] = jnp.zeros_like(run_ref)

    ids = ids_ref[...]
    tm = ids.shape[0]
    lane = lax.broadcasted_iota(I32, ids.shape, 1)
    sels = [lane == ids[:, kk:kk + 1] for kk in range(TOP_K)]
    onehot = jnp.zeros(ids.shape, F32)
    for s in sels:
        onehot = jnp.where(s, 1.0, onehot)
    c8 = _ceil_rows(jnp.sum(onehot, axis=0, keepdims=True))
    er = lax.broadcasted_iota(I32, (LANES, LANES), 0)
    ec = lax.broadcasted_iota(I32, (LANES, LANES), 1)
    before = jnp.where(er < ec, 1.0, 0.0).astype(BF16)
    pieces = jnp.broadcast_to(c8 * (1.0 / SUBLANES), (SUBLANES, LANES)).astype(BF16)
    lo = _dot(pieces, before)[0:1, :] * SUBLANES
    r = lax.broadcasted_iota(I32, (tm, tm), 0)
    c = lax.broadcasted_iota(I32, (tm, tm), 1)
    strict = jnp.where(c < r, 1.0, 0.0).astype(BF16)
    slot = _dot(strict, onehot.astype(BF16)) + lo
    ls = jnp.zeros(ids.shape, F32)
    for kk, s in enumerate(sels):
        pk = jnp.sum(jnp.where(s, slot, 0.0), axis=-1, keepdims=True)
        ls = jnp.where(lane == kk, pk, ls)
    ls_ref[...] = ls
    hi = jnp.floor(ls * (1.0 / 32.0))
    rem = ls - 32.0 * hi
    pick = jnp.where(lax.broadcasted_iota(I32, (SUBLANES, LANES), 0) == lax.broadcasted_iota(I32, (SUBLANES, LANES), 1),
                     1.0, 0.0).astype(BF16)
    lst_ref[...] = 32.0 * _dot_nt(pick, hi.astype(BF16)) + _dot_nt(pick, rem.astype(BF16))
    row = lax.broadcasted_iota(I32, (SUBLANES, LANES), 0)
    off = start_ref[...] + run_ref[...]
    total = jnp.sum(c8, axis=-1, keepdims=True)
    seg = jnp.where(row == 0, c8, jnp.where(row == 1, lo, jnp.where(row == 2, off, jnp.where(row == 3, total, 0.0))))
    seg_ref[...] = seg.astype(I32)
    run_ref[...] += c8


def _slots(ids, starts, tm):
    T = ids.shape[0]
    nt = T // tm
    return pl.pallas_call(
        _slots_kernel,
        grid=(nt,),
        in_specs=[
            pl.BlockSpec((tm, LANES), lambda i: (i, 0)),
            pl.BlockSpec((1, LANES), lambda i: (0, 0)),
        ],
        out_specs=[
            pl.BlockSpec((tm, LANES), lambda i: (i, 0)),
            pl.BlockSpec((SUBLANES, tm), lambda i: (0, i)),
            pl.BlockSpec((SUBLANES, LANES), lambda i: (i, 0)),
        ],
        out_shape=[
            jax.ShapeDtypeStruct((T, LANES), F32),
            jax.ShapeDtypeStruct((SUBLANES, T), F32),
            jax.ShapeDtypeStruct((nt * SUBLANES, LANES), I32),
        ],
        scratch_shapes=[pltpu.VMEM((1, LANES), F32)],
        compiler_params=_cparams(("arbitrary",)),
        name="slots",
    )(ids, starts)


def _local_rows(tm):
    need = tm * TOP_K + N_EXPERTS * (SUBLANES - 1)
    return ((need + LANES - 1) // LANES) * LANES


BIG_PIECE = 4 * SUBLANES


def _segment_starts(seg, make_copy):
    def expert(e, carry):
        cnt = seg(e)
        lo = seg(LANES + e)
        off = seg(2 * LANES + e)
        n_big = lax.shift_right_logical(cnt, 5)
        n_small = lax.shift_right_logical(cnt & (BIG_PIECE - 1), 3)

        def big(j, carry2):
            d = j * BIG_PIECE
            make_copy(pl.multiple_of(lo + d, SUBLANES), pl.multiple_of(off + d, SUBLANES), BIG_PIECE).start()
            return carry2

        def small(j, carry2):
            d = n_big * BIG_PIECE + j * SUBLANES
            make_copy(pl.multiple_of(lo + d, SUBLANES), pl.multiple_of(off + d, SUBLANES), SUBLANES).start()
            return carry2

        lax.fori_loop(0, n_big, big, 0)
        lax.fori_loop(0, n_small, small, 0)
        return carry

    for e in range(N_EXPERTS):
        expert(e, 0)


def _segment_waits(total_rows, make_copy):
    def big(j, carry):
        make_copy(0, 0, BIG_PIECE).wait()
        return carry

    def small(j, carry):
        make_copy(0, 0, SUBLANES).wait()
        return carry

    lax.fori_loop(0, lax.shift_right_logical(total_rows, 5), big, 0)
    lax.fori_loop(0, lax.shift_right_logical(total_rows & (BIG_PIECE - 1), 3), small, 0)


def _dispatch_kernel(h_ref, lst_ref, seg_hbm, xs_ref, sbuf_ref, seg_smem, prev_smem, sem_seg, sem_rows, *, tm):
    i = pl.program_id(0)
    n = pl.num_programs(0)
    slot = i % 2
    rows = sbuf_ref.shape[1]
    cp = pltpu.make_async_copy(seg_hbm.at[i], seg_smem, sem_seg)
    cp.start()
    lst = lst_ref[...].astype(I32)
    rid = lax.broadcasted_iota(I32, (rows, tm), 0)
    perm = jnp.zeros((rows, tm), F32)
    for kk in range(TOP_K):
        perm = perm + jnp.where(rid == lst[kk:kk + 1, :], 1.0, 0.0)
    srt = _dot(perm.astype(BF16), h_ref[...])
    sbuf_ref[slot] = _pack_bf16_pairs(srt)
    cp.wait()

    def copy_from(s):
        def make_copy(lo, off, nrows):
            return pltpu.make_async_copy(sbuf_ref.at[s, pl.ds(lo, nrows), :], xs_ref.at[pl.ds(off, nrows), :],
                                         sem_rows.at[s])
        return make_copy

    _segment_starts(lambda k: seg_smem[k], copy_from(slot))

    @pl.when(i > 0)
    def _():
        _segment_waits(prev_smem[0], copy_from(1 - slot))

    prev_smem[0] = seg_smem[3 * LANES]

    @pl.when(i == n - 1)
    def _():
        _segment_waits(prev_smem[0], copy_from(slot))


def _dispatch(h, lst, seg2d, n_rows, tm):
    T = h.shape[0]
    rows = _local_rows(tm)
    return pl.pallas_call(
        functools.partial(_dispatch_kernel, tm=tm),
        grid=(T // tm,),
        in_specs=[
            pl.BlockSpec((tm, D_MODEL), lambda i: (i, 0)),
            pl.BlockSpec((SUBLANES, tm), lambda i: (0, i)),
            pl.BlockSpec(memory_space=pl.ANY),
        ],
        out_specs=pl.BlockSpec(memory_space=pl.ANY),
        out_shape=jax.ShapeDtypeStruct((n_rows, D_MODEL // 2), U32),
        scratch_shapes=[
            pltpu.VMEM((2, rows, D_MODEL // 2), U32),
            pltpu.SMEM((SEG_WORDS,), I32),
            pltpu.SMEM((1,), I32),
            pltpu.SemaphoreType.DMA,
            pltpu.SemaphoreType.DMA((2,)),
        ],
        compiler_params=_cparams(("arbitrary",)),
        name="dispatch",
    )(h, lst, seg2d)


def _experts_kernel(te_ref, nu_ref, nv_ref, xs_ref, wgu_ref, bgu_ref, wdn_ref, bdn_ref, ys_ref, wgu_bf, wdn_bf):
    i = pl.program_id(0)
    half = D_MODEL // 2

    @pl.when((i == 0) | (te_ref[i] != te_ref[jnp.maximum(i - 1, 0)]))
    def _():
        wgu_bf[...] = wgu_ref[0].astype(BF16)
        wdn_bf[...] = wdn_ref[0].astype(BF16)

    @pl.when(i < nu_ref[0])
    def _():
        live = lax.broadcasted_iota(I32, xs_ref.shape, 0) < nv_ref[i]
        lo, hi = _unpack_bf16_pairs(jnp.where(live, xs_ref[...], jnp.uint32(0)))
        gu = (_dot(lo.astype(BF16), wgu_bf[:half, :]) + _dot(hi.astype(BF16), wgu_bf[half:, :])
              + bgu_ref[0])
        gate = jnp.minimum(gu[:, :D_FF], SWIGLU_LIMIT)
        up = jnp.clip(gu[:, D_FF:], -SWIGLU_LIMIT, SWIGLU_LIMIT)
        act = (up + 1.0) * (gate * _sigmoid(SWIGLU_ALPHA * gate))
        y = _dot(act.astype(BF16), wdn_bf[...]) + bdn_ref[0]
        ys_ref[...] = _pack_bf16_pairs(y)

    @pl.when(i >= nu_ref[0])
    def _():
        ys_ref[...] = jnp.zeros_like(ys_ref)


def _experts(tile_expert, n_used, tile_valid, xs, w_gu, b_gu, w_dn, b_dn, tg):
    P = xs.shape[0]
    half = D_MODEL // 2
    grid_spec = pltpu.PrefetchScalarGridSpec(
        num_scalar_prefetch=3,
        grid=(P // tg,),
        in_specs=[
            pl.BlockSpec((tg, half), lambda i, te, nu, nv: (jnp.minimum(i, jnp.maximum(nu[0] - 1, 0)), 0)),
            pl.BlockSpec((1, D_MODEL, 2 * D_FF), lambda i, te, nu, nv: (te[i], 0, 0)),
            pl.BlockSpec((1, 1, 2 * D_FF), lambda i, te, nu, nv: (te[i], 0, 0)),
            pl.BlockSpec((1, D_FF, D_MODEL), lambda i, te, nu, nv: (te[i], 0, 0)),
            pl.BlockSpec((1, 1, D_MODEL), lambda i, te, nu, nv: (te[i], 0, 0)),
        ],
        out_specs=pl.BlockSpec((tg, half), lambda i, te, nu, nv: (i, 0)),
        scratch_shapes=[pltpu.VMEM((D_MODEL, 2 * D_FF), BF16), pltpu.VMEM((D_FF, D_MODEL), BF16)],
    )
    return pl.pallas_call(
        _experts_kernel,
        grid_spec=grid_spec,
        out_shape=jax.ShapeDtypeStruct((P, half), U32),
        compiler_params=_cparams(("arbitrary",)),
        name="experts",
    )(tile_expert, n_used, tile_valid, xs, w_gu, b_gu, w_dn, b_dn)


def _combine_kernel(seg_hbm, ys_hbm, ls_ref, tw_ref, x_ref, g_ref, out_ref, ybuf_ref, seg_smem, sem_seg, sem_rows, *,
                    tm, final_norm):
    i = pl.program_id(0)
    n = pl.num_programs(0)
    slot = i % 2
    rows = ybuf_ref.shape[1]

    def copy_into(s):
        def make_copy(lo, off, nrows):
            return pltpu.make_async_copy(ys_hbm.at[pl.ds(off, nrows), :], ybuf_ref.at[s, pl.ds(lo, nrows), :],
                                         sem_rows.at[s])
        return make_copy

    def request(step, s):
        cp = pltpu.make_async_copy(seg_hbm.at[step], seg_smem.at[s], sem_seg)
        cp.start()
        cp.wait()
        _segment_starts(lambda k: seg_smem[s, k], copy_into(s))

    @pl.when(i == 0)
    def _():
        ybuf_ref[...] = jnp.zeros_like(ybuf_ref)
        request(0, 0)

    @pl.when(i + 1 < n)
    def _():
        request(i + 1, 1 - slot)

    ls = ls_ref[...].astype(I32)
    tw = tw_ref[...]
    cid = lax.broadcasted_iota(I32, (tm, rows), 1)
    wmat = jnp.zeros((tm, rows), F32)
    for kk in range(TOP_K):
        wmat = wmat + jnp.where(cid == ls[:, kk:kk + 1], tw[:, kk:kk + 1], 0.0)
    wmat = wmat.astype(BF16)
    _segment_waits(seg_smem[slot, 3 * LANES], copy_into(slot))
    lo, hi = _unpack_bf16_pairs(ybuf_ref[slot])
    moe = jnp.concatenate([_dot(wmat, lo.astype(BF16)), _dot(wmat, hi.astype(BF16))], axis=1)
    x3 = x_ref[...] + moe
    out_ref[...] = _rms(x3, g_ref[...]) if final_norm else x3


def _combine(seg2d, ys, ls, tw, x2, g, tm, final_norm):
    T = x2.shape[0]
    rows = _local_rows(tm)
    return pl.pallas_call(
        functools.partial(_combine_kernel, tm=tm, final_norm=final_norm),
        grid=(T // tm,),
        in_specs=[
            pl.BlockSpec(memory_space=pl.ANY),
            pl.BlockSpec(memory_space=pl.ANY),
            pl.BlockSpec((tm, LANES), lambda i: (i, 0)),
            pl.BlockSpec((tm, LANES), lambda i: (i, 0)),
            pl.BlockSpec((tm, D_MODEL), lambda i: (i, 0)),
            pl.BlockSpec((1, D_MODEL), lambda i: (0, 0)),
        ],
        out_specs=pl.BlockSpec((tm, D_MODEL), lambda i: (i, 0)),
        out_shape=jax.ShapeDtypeStruct((T, D_MODEL), F32),
        scratch_shapes=[
            pltpu.VMEM((2, rows, D_MODEL // 2), U32),
            pltpu.SMEM((2, SEG_WORDS), I32),
            pltpu.SemaphoreType.DMA,
            pltpu.SemaphoreType.DMA((2,)),
        ],
        compiler_params=_cparams(("arbitrary",)),
        name="combine",
    )(seg2d, ys, ls, tw, x2, g)


def _tiles(B, S):
    T = B * S
    return dict(
        tm_in=min(1024, T), tn_in=2048,
        ts=min(512, S),
        tq=min(1024, S),
        tm_proj=min(512, S),
        tm_route=min(256, T),
        tg=512,
    )


def _pad_lanes(a, n=LANES, value=0.0):
    return jnp.pad(a, ((0, 0), (0, n - a.shape[1])), constant_values=value)


def kernel(x, mem, norm_mix, w_in, conv_w, b_if, mlstm_gain, diff_lambda, diff_gain, w_branch_m, w_branch_d,
           b_gate, w_out, norm_xattn, norm_mem, wq_x, wkv_x, wo_x, norm_ffn, w_router, b_router, w_gu, b_gu,
           w_dn, b_dn, norm_final):
    B, S, D = x.shape
    n_mem = mem.shape[1]
    T = B * S
    depth = norm_mix.shape[0]
    tl = _tiles(B, S)
    x2d = x.reshape(T, D)
    mem2d = mem.reshape(B * n_mem, D)

    for l in range(depth):
        lam_init = 0.8 - 0.6 * math.exp(-0.3 * l)
        wl = w_in[l]
        if_lo = 2 * M_QK + 2 * M_V
        w_main = jnp.concatenate([wl[:, :if_lo], wl[:, if_lo + 2 * M_HEADS:]], axis=1).astype(BF16)
        w_if = wl[:, if_lo:if_lo + 2 * M_HEADS]
        w_ifp = _pad_lanes(w_if).astype(BF16)
        w_ift = w_if.T.astype(BF16)
        bif = _pad_lanes(b_if[l][None, :])
        bift = jnp.broadcast_to(b_if[l][:, None], (SUBLANES, LANES))

        z, zif, zift = _inproj(x2d, norm_mix[l][None, :], w_main, w_ifp, w_ift, tl["tm_in"], tl["tn_in"])
        hm = _mlstm(z, zif, zift, conv_w[l], bif, bift, mlstm_gain[l].reshape(1, M_V), B, S, tl["ts"])
        hd = _diffattn(z, diff_lambda[l], diff_gain[l][None, :], B, S, tl["tq"], lam_init)
        x1 = _merge(hm, hd, z, x2d, w_branch_m[l].astype(BF16), w_branch_d[l].astype(BF16),
                    w_out[l].astype(BF16), b_gate[l][None, :], tl["tm_proj"])

        kvmem = _memkv(mem2d, norm_mem[l][None, :], wkv_x[l].astype(BF16), n_mem)
        x2 = _xattn(x1, norm_xattn[l][None, :], wq_x[l].astype(BF16), kvmem, wo_x[l].astype(BF16),
                    S, n_mem, tl["tm_proj"])

        wr = _pad_lanes(w_router[l])
        wr_hi = wr.astype(BF16)
        wr_lo = (wr - wr_hi.astype(F32)).astype(BF16)
        br = _pad_lanes(b_router[l][None, :], value=-jnp.inf)
        tm_r = tl["tm_route"]
        tg = tl["tg"]
        hp, ids, tw, cnt = _router(x2, norm_ffn[l][None, :], wr_hi, wr_lo, br, tm_r)

        counts = cnt[0, :N_EXPERTS].astype(I32)
        padded = ((counts + tg - 1) // tg) * tg
        ends = jnp.cumsum(padded)
        starts = ends - padded
        max_rows = T * TOP_K + (T // tm_r) * N_EXPERTS * (SUBLANES - 1)
        n_tiles = -(-max_rows // tg) + N_EXPERTS
        tile_row0 = jnp.arange(n_tiles, dtype=I32) * tg
        tile_expert = jnp.minimum(jnp.sum((tile_row0[:, None] >= ends[None, :]).astype(I32), axis=1), N_EXPERTS - 1)
        n_used = (ends[-1] // tg).astype(I32).reshape(1)
        last_used = tile_expert[jnp.maximum(n_used[0] - 1, 0)]
        tile_expert = jnp.where(tile_row0 < ends[-1], tile_expert, last_used)
        tile_valid = jnp.clip((starts + counts)[tile_expert] - tile_row0, 0, tg).astype(I32)

        ls, lst, seg = _slots(ids, _pad_lanes(starts.astype(F32)[None, :]), tm_r)
        seg2d = seg.reshape(T // tm_r, SEG_WORDS)

        xs = _dispatch(hp, lst, seg2d, n_tiles * tg, tm_r)
        ys = _experts(tile_expert, n_used, tile_valid, xs, w_gu[l], b_gu[l][:, None, :], w_dn[l], b_dn[l][:, None, :],
                      tg)
        x2d = _combine(seg2d, ys, ls, tw, x2, norm_final[None, :], tm_r, final_norm=(l == depth - 1))
    return x2d.reshape(B, S, D)
```

```python
import functools
import math

import jax
import jax.numpy as jnp
from jax import lax
from jax.experimental import pallas as pl
from jax.experimental.pallas import tpu as pltpu

F32 = jnp.float32
BF16 = jnp.bfloat16
U32 = jnp.uint32
I32 = jnp.int32

EPS = 1e-6
CHUNK = 64
D_MODEL = 1024
M_HEADS = 4
M_DK = 128
M_DV = 256
M_QK = M_HEADS * M_DK
M_V = M_HEADS * M_DV
CONV_W = 4
D_HEADS = 8
D_DH = 64
D_HP = 2
D_QK = D_HEADS * 2 * D_DH
D_V = D_HEADS * 2 * D_DH
X_HEADS = 4
X_DH = D_MODEL // X_HEADS
N_EXPERTS = 32
TOP_K = 4
D_FF = D_MODEL
SWIGLU_LIMIT = 7.0
SWIGLU_ALPHA = 1.702

LANES = 128
SUBLANES = 8
N_MAIN = 2 * M_QK + 2 * M_V + 2 * D_QK + D_V + 2 * D_MODEL
OFF_QM, OFF_KM, OFF_VM, OFF_OM = 0, M_QK, 2 * M_QK, 2 * M_QK + M_V
OFF_QD = OFF_OM + M_V
OFF_KD = OFF_QD + D_QK
OFF_VD = OFF_KD + D_QK
OFF_G = OFF_VD + D_V

VMEM_LIMIT = 56 * 1024 * 1024


def _cparams(sem, vmem=VMEM_LIMIT):
    return pltpu.CompilerParams(dimension_semantics=sem, vmem_limit_bytes=vmem)


def _rms(x, g):
    return x * lax.rsqrt(jnp.mean(x * x, axis=-1, keepdims=True) + EPS) * g


def _split_bf16(x):
    hi = x.astype(BF16)
    lo = (x - hi.astype(F32)).astype(BF16)
    return hi, lo


def _dot(a, b):
    return jnp.dot(a, b, preferred_element_type=F32)


def _dot_nt(a, b):
    return lax.dot_general(a, b, (((1,), (1,)), ((), ())), preferred_element_type=F32)


def _sigmoid(x):
    return 1.0 / (1.0 + jnp.exp(-x))


def _log_sigmoid(x):
    return jnp.minimum(x, 0.0) - jnp.log(1.0 + jnp.exp(-jnp.abs(x)))


def _pack_bf16_pairs(x):
    w = x.shape[1] // 2
    u = lax.bitcast_convert_type(x, U32)
    r = (u + jnp.uint32(0x7FFF) + ((u >> 16) & jnp.uint32(1))) >> 16
    return r[:, :w] | (r[:, w:] << 16)


def _ceil_rows(x):
    return jnp.floor((x + (SUBLANES - 1)) * (1.0 / SUBLANES)) * SUBLANES


def _unpack_bf16_pairs(p):
    lo = lax.bitcast_convert_type(p << 16, F32)
    hi = lax.bitcast_convert_type(p & jnp.uint32(0xFFFF0000), F32)
    return lo, hi


def _inproj_kernel(x_ref, g_ref, w_ref, wif_ref, wift_ref, z_ref, zif_ref, zift_ref, hn_ref):
    @pl.when(pl.program_id(1) == 0)
    def _():
        hn = _rms(x_ref[...], g_ref[...]).astype(BF16)
        hn_ref[...] = hn
        zif_ref[...] = _dot(hn, wif_ref[...])
        zift_ref[...] = _dot_nt(wift_ref[...], hn)

    z_ref[...] = _dot(hn_ref[...], w_ref[...]).astype(BF16)


def _inproj(x2d, g, w_main, w_if, w_ift, tm, tn):
    T = x2d.shape[0]
    return pl.pallas_call(
        _inproj_kernel,
        grid=(T // tm, N_MAIN // tn),
        in_specs=[
            pl.BlockSpec((tm, D_MODEL), lambda i, j: (i, 0)),
            pl.BlockSpec((1, D_MODEL), lambda i, j: (0, 0)),
            pl.BlockSpec((D_MODEL, tn), lambda i, j: (0, j)),
            pl.BlockSpec((D_MODEL, LANES), lambda i, j: (0, 0)),
            pl.BlockSpec((SUBLANES, D_MODEL), lambda i, j: (0, 0)),
        ],
        out_specs=[
            pl.BlockSpec((tm, tn), lambda i, j: (i, j)),
            pl.BlockSpec((tm, LANES), lambda i, j: (i, 0)),
            pl.BlockSpec((SUBLANES, tm), lambda i, j: (0, i)),
        ],
        out_shape=[
            jax.ShapeDtypeStruct((T, N_MAIN), BF16),
            jax.ShapeDtypeStruct((T, LANES), F32),
            jax.ShapeDtypeStruct((SUBLANES, T), F32),
        ],
        scratch_shapes=[pltpu.VMEM((tm, D_MODEL), BF16)],
        compiler_params=_cparams(("arbitrary", "arbitrary")),
        name="inproj",
    )(x2d, g, w_main, w_if, w_ift)


def _mlstm_kernel(q_ref, k_ref, v_ref, om_ref, zif_ref, zift_ref, cw_ref, bif_ref, bift_ref, mg_ref,
                  out_ref, qc_ref, kc_ref, kt_ref, carry_ref, c_ref, n_ref, m_ref,
                  bd_ref, bdt_ref, brep_ref, grow_ref, brow_ref, *, ts):
    nchunk = ts // CHUNK
    L = CHUNK

    @pl.when(pl.program_id(1) == 0)
    def _():
        carry_ref[...] = jnp.zeros_like(carry_ref)
        c_ref[...] = jnp.zeros_like(c_ref)
        n_ref[...] = jnp.zeros_like(n_ref)
        m_ref[...] = jnp.zeros_like(m_ref)
        rt = lax.broadcasted_iota(I32, (ts, ts), 0)
        ct = lax.broadcasted_iota(I32, (ts, ts), 1)
        same = (rt // L) == (ct // L)
        bd_ref[...] = jnp.where(same, jnp.where(ct <= rt, 1.0, 0.0), 0.0).astype(BF16)
        bdt_ref[...] = jnp.where(same, jnp.where(rt <= ct, 1.0, 0.0), 0.0).astype(BF16)

    row8 = lax.broadcasted_iota(I32, (SUBLANES, M_QK), 0)

    def conv_silu(x, prev8, w):
        acc = w[CONV_W - 1:CONV_W, :] * x
        for s in range(1, CONV_W):
            xs = pltpu.roll(x, s, 0)
            top = jnp.where(row8 < s, pltpu.roll(prev8, s, 0), xs[0:SUBLANES])
            xs = jnp.concatenate([top, xs[SUBLANES:]], axis=0)
            acc = acc + w[CONV_W - 1 - s:CONV_W - s, :] * xs
        return acc * _sigmoid(acc)

    def conv_body(c, carry):
        r0 = pl.multiple_of(c * L, L)
        xq = q_ref[pl.ds(r0, L), :].astype(F32)
        xk = k_ref[pl.ds(r0, L), :].astype(F32)
        yq = conv_silu(xq, carry_ref[:, 0:M_QK], cw_ref[:, 0:M_QK]) * (M_DK ** -0.5)
        yk = conv_silu(xk, carry_ref[:, M_QK:2 * M_QK], cw_ref[:, M_QK:2 * M_QK])
        qc_ref[pl.ds(r0, L), :] = yq.astype(BF16)
        kc_ref[pl.ds(r0, L), :] = yk.astype(BF16)
        for h in range(M_HEADS):
            kt_ref[c, h] = yk[:, h * M_DK:(h + 1) * M_DK].T
        carry_ref[:, 0:M_QK] = xq[L - SUBLANES:L]
        carry_ref[:, M_QK:2 * M_QK] = xk[L - SUBLANES:L]
        return carry

    lax.fori_loop(0, nchunk, conv_body, 0, unroll=2)

    ti = lax.broadcasted_iota(I32, (L, L), 0)
    si = lax.broadcasted_iota(I32, (L, L), 1)
    causal = si <= ti
    lane_row = lax.broadcasted_iota(I32, (LANES, LANES), 0)
    ones_l = jnp.ones((L, LANES), BF16)

    lf_col = _log_sigmoid(zif_ref[...] + bif_ref[...])
    ch, cl = _split_bf16(lf_col)
    b_col_all = _dot(bd_ref[...], ch) + _dot(bd_ref[...], cl)
    bh, bl = _split_bf16(b_col_all)
    for h in range(M_HEADS):
        sel_f = jnp.where(lane_row == M_HEADS + h, 1.0, 0.0).astype(BF16)
        brep_ref[h] = _dot(bh, sel_f) + _dot(bl, sel_f)
    g_row_all = zift_ref[...] + bift_ref[:, 0:1]
    rh, rl = _split_bf16(_log_sigmoid(g_row_all))
    b_row_tile = _dot(rh, bdt_ref[...]) + _dot(rl, bdt_ref[...])
    for cc in range(nchunk):
        grow_ref[cc] = g_row_all[:, cc * L:(cc + 1) * L]
        brow_ref[cc] = b_row_tile[:, cc * L:(cc + 1) * L]

    def chunk_body(c, carry):
        r0 = pl.multiple_of(c * L, L)
        g_row = grow_ref[c]
        b_row_all = brow_ref[c]
        early = []
        for h in range(M_HEADS):
            b_rep = brep_ref[h, pl.ds(r0, L), :]
            i_row = g_row[h:h + 1, :]
            b_row = b_row_all[M_HEADS + h:M_HEADS + h + 1, :]
            b_last = b_rep[L - 1:L, :]
            q = qc_ref[pl.ds(r0, L), h * M_DK:(h + 1) * M_DK]
            k = kc_ref[pl.ds(r0, L), h * M_DK:(h + 1) * M_DK]
            vext = jnp.concatenate([v_ref[pl.ds(r0, L), h * M_DV:(h + 1) * M_DV], ones_l], axis=1)
            dm = jnp.where(causal, b_rep[:, :L] - b_row + i_row, -jnp.inf)
            m_loc = jnp.max(dm, axis=-1, keepdims=True)
            qk = _dot_nt(q, k)
            gk_row = b_last[:, :L] - b_row + i_row
            g_max = jnp.max(gk_row, axis=-1, keepdims=True)
            kwt = (kt_ref[c, h] * jnp.exp(gk_row - g_max)).astype(BF16)
            kv = _dot(kwt, vext)
            c_old = c_ref[h]
            n_old = n_ref[h]
            qcn = _dot(q, jnp.concatenate([c_old, n_old], axis=1).astype(BF16))
            early.append((b_rep, b_last, vext, dm, m_loc, qk, g_max, kv, c_old, n_old, qcn))
        pvs = []
        for h in range(M_HEADS):
            b_rep, b_last, vext, dm, m_loc, qk, g_max, kv, c_old, n_old, qcn = early[h]
            s_loc = qk * jnp.exp(dm - m_loc)
            pvs.append(_dot(s_loc.astype(BF16), vext))
        for h in range(M_HEADS):
            b_rep, b_last, vext, dm, m_loc, qk, g_max, kv, c_old, n_old, qcn = early[h]
            pv = pvs[h]
            m_prev = m_ref[h:h + 1, :]
            inter = b_rep + m_prev
            m_t = jnp.maximum(inter, m_loc)
            w_inter = jnp.exp(inter - m_t)
            r_loc = jnp.exp(m_loc - m_t)
            den = r_loc * pv[:, M_DV:] + w_inter * qcn[:, M_DV:]
            inv = 1.0 / jnp.maximum(jnp.abs(den), jnp.exp(-m_t))
            hv = (jnp.concatenate([r_loc * inv] * 2, axis=1) * pv[:, :M_DV]
                  + jnp.concatenate([w_inter * inv] * 2, axis=1) * qcn[:, :M_DV])
            m_new = jnp.maximum(b_last + m_prev, g_max)
            decay = jnp.exp(b_last + m_prev - m_new)
            sc_loc = jnp.exp(g_max - m_new)
            c_ref[h] = (jnp.concatenate([decay] * 2, axis=1) * c_old
                        + jnp.concatenate([sc_loc] * 2, axis=1) * kv[:, :M_DV])
            n_ref[h] = decay * n_old + sc_loc * kv[:, M_DV:]
            m_ref[h:h + 1, :] = m_new
            hn = _rms(hv, mg_ref[:, h * M_DV:(h + 1) * M_DV])
            og = _sigmoid(om_ref[pl.ds(r0, L), h * M_DV:(h + 1) * M_DV].astype(F32))
            out_ref[pl.ds(r0, L), h * M_DV:(h + 1) * M_DV] = (og * hn).astype(BF16)
        return carry

    lax.fori_loop(0, nchunk, chunk_body, 0, unroll=4)


def _mlstm(z, zif, zift, conv_w, bif, bift, m_gain, B, S, ts):
    T = B * S
    nt = S // ts
    nck = ts // CHUNK
    row = lambda b, t: b * nt + t
    return pl.pallas_call(
        functools.partial(_mlstm_kernel, ts=ts),
        grid=(B, nt),
        in_specs=[
            pl.BlockSpec((ts, M_QK), lambda b, t: (row(b, t), OFF_QM // M_QK)),
            pl.BlockSpec((ts, M_QK), lambda b, t: (row(b, t), OFF_KM // M_QK)),
            pl.BlockSpec((ts, M_V), lambda b, t: (row(b, t), OFF_VM // M_V)),
            pl.BlockSpec((ts, M_V), lambda b, t: (row(b, t), OFF_OM // M_V)),
            pl.BlockSpec((ts, LANES), lambda b, t: (row(b, t), 0)),
            pl.BlockSpec((SUBLANES, ts), lambda b, t: (0, row(b, t))),
            pl.BlockSpec((CONV_W, 2 * M_QK), lambda b, t: (0, 0)),
            pl.BlockSpec((1, LANES), lambda b, t: (0, 0)),
            pl.BlockSpec((SUBLANES, LANES), lambda b, t: (0, 0)),
            pl.BlockSpec((1, M_V), lambda b, t: (0, 0)),
        ],
        out_specs=pl.BlockSpec((ts, M_V), lambda b, t: (row(b, t), 0)),
        out_shape=jax.ShapeDtypeStruct((T, M_V), BF16),
        scratch_shapes=[
            pltpu.VMEM((ts, M_QK), BF16),
            pltpu.VMEM((ts, M_QK), BF16),
            pltpu.VMEM((nck, M_HEADS, M_DK, CHUNK), F32),
            pltpu.VMEM((SUBLANES, 2 * M_QK), F32),
            pltpu.VMEM((M_HEADS, M_DK, M_DV), F32),
            pltpu.VMEM((M_HEADS, M_DK, LANES), F32),
            pltpu.VMEM((SUBLANES, LANES), F32),
            pltpu.VMEM((ts, ts), BF16),
            pltpu.VMEM((ts, ts), BF16),
            pltpu.VMEM((M_HEADS, ts, LANES), F32),
            pltpu.VMEM((nck, SUBLANES, CHUNK), F32),
            pltpu.VMEM((nck, SUBLANES, CHUNK), F32),
        ],
        compiler_params=_cparams(("arbitrary", "arbitrary")),
        name="mlstm",
    )(z, z, z, z, zif, zift, conv_w, bif, bift, m_gain)


def _diffattn_kernel(q_ref, k_ref, v_ref, lam_ref, gain_ref, out_ref, m_ref, a_ref, *, tq, lam_init):
    qi = pl.program_id(2)
    w = 2 * D_DH
    lane = lax.broadcasted_iota(I32, (1, w), 1)
    scale = jnp.asarray(D_DH ** -0.5, BF16)
    qs = []
    for hh in range(D_HP):
        q = q_ref[:, hh * w:(hh + 1) * w]
        qs.append((jnp.where(lane < D_DH, q, jnp.zeros_like(q)) * scale,
                   jnp.where(lane >= D_DH, q, jnp.zeros_like(q)) * scale))
    ones = jnp.ones((tq, w), BF16)

    def online(s, vext, idx, rows, first):
        s_max = jnp.max(s, axis=-1, keepdims=True)
        if first:
            m_new = jnp.broadcast_to(s_max, (s.shape[0], w))
        else:
            m_old = m_ref[idx, rows, :]
            m_new = jnp.maximum(m_old, s_max)
        p = jnp.exp(s - jnp.concatenate([m_new] * (s.shape[1] // w), axis=1))
        pv = _dot(p.astype(BF16), vext)
        if first:
            a_ref[idx, rows, :] = pv
        else:
            alpha = jnp.exp(m_old - m_new)
            a_ref[idx, rows, :] = jnp.concatenate([alpha, alpha], axis=1) * a_ref[idx, rows, :] + pv
        m_ref[idx, rows, :] = m_new

    def block(k0, nk, rows=slice(None), mask=None, first=False):
        scores = []
        for hh in range(D_HP):
            k = k_ref[pl.ds(k0, nk), hh * w:(hh + 1) * w]
            for comp in range(2):
                s = _dot_nt(qs[hh][comp][rows], k)
                scores.append(s if mask is None else jnp.where(mask, s, -jnp.inf))
        for hh in range(D_HP):
            vext = jnp.concatenate([v_ref[pl.ds(k0, nk), hh * w:(hh + 1) * w], ones[:nk]], axis=1)
            for comp in range(2):
                online(scores[2 * hh + comp], vext, 2 * hh + comp, rows, first)

    hq = tq // 2
    d0 = pl.multiple_of(qi * tq, tq)

    def chunk_mask(nq, q0):
        rq = (lax.broadcasted_iota(I32, (nq, hq), 0) + q0) // CHUNK
        ck = lax.broadcasted_iota(I32, (nq, hq), 1) // CHUNK
        return ck <= rq

    block(d0, hq, mask=chunk_mask(tq, 0), first=True)
    block(pl.multiple_of(d0 + hq, hq), hq, rows=slice(hq, tq), mask=chunk_mask(hq, 0))

    def body(jj, carry):
        block(pl.multiple_of(jj * tq, tq), tq)
        return carry

    lax.fori_loop(0, qi, body, 0)

    lp = lam_ref[...]
    lam = (jnp.exp(jnp.sum(lp[0:1, :] * lp[1:2, :], axis=-1, keepdims=True))
           - jnp.exp(jnp.sum(lp[2:3, :] * lp[3:4, :], axis=-1, keepdims=True)) + lam_init)
    for hh in range(D_HP):
        a1 = a_ref[2 * hh]
        a2 = a_ref[2 * hh + 1]
        o = a1[:, :w] / a1[:, w:] - lam * (a2[:, :w] / a2[:, w:])
        out_ref[:, hh * w:(hh + 1) * w] = (_rms(o, gain_ref[...]) * (1.0 - lam_init)).astype(BF16)


def _diffattn(z, lam_p, d_gain, B, S, tq, lam_init):
    T = B * S
    nq = S // tq
    w = 2 * D_DH
    wp = D_HP * w
    return pl.pallas_call(
        functools.partial(_diffattn_kernel, tq=tq, lam_init=lam_init),
        grid=(B, D_HEADS // D_HP, nq),
        in_specs=[
            pl.BlockSpec((tq, wp), lambda b, h, i: (b * nq + i, OFF_QD // wp + h)),
            pl.BlockSpec((S, wp), lambda b, h, i: (b, OFF_KD // wp + h)),
            pl.BlockSpec((S, wp), lambda b, h, i: (b, OFF_VD // wp + h)),
            pl.BlockSpec((4, D_DH), lambda b, h, i: (0, 0)),
            pl.BlockSpec((1, w), lambda b, h, i: (0, 0)),
        ],
        out_specs=pl.BlockSpec((tq, wp), lambda b, h, i: (b * nq + i, h)),
        out_shape=jax.ShapeDtypeStruct((T, D_V), BF16),
        scratch_shapes=[
            pltpu.VMEM((2 * D_HP, tq, w), F32), pltpu.VMEM((2 * D_HP, tq, 2 * w), F32),
        ],
        compiler_params=_cparams(("arbitrary", "arbitrary", "arbitrary")),
        name="diffattn",
    )(z, z, z, lam_p, d_gain)


def _merge_kernel(hm_ref, hd_ref, gz_ref, x_ref, wbm_ref, wbd_ref, wout_ref, bg_ref, out_ref):
    bm = _dot(hm_ref[...], wbm_ref[...])
    bd = _dot(hd_ref[...], wbd_ref[...])
    g = _sigmoid(gz_ref[...].astype(F32) + bg_ref[...])
    merged = g[:, :D_MODEL] * bm + g[:, D_MODEL:] * bd
    out_ref[...] = x_ref[...] + _dot(merged.astype(BF16), wout_ref[...])


def _merge(hm, hd, z, x2d, w_bm, w_bd, w_out, b_gate, tm):
    T = x2d.shape[0]
    full = lambda i: (0, 0)
    return pl.pallas_call(
        _merge_kernel,
        grid=(T // tm,),
        in_specs=[
            pl.BlockSpec((tm, M_V), lambda i: (i, 0)),
            pl.BlockSpec((tm, D_V), lambda i: (i, 0)),
            pl.BlockSpec((tm, 2 * D_MODEL), lambda i: (i, OFF_G // (2 * D_MODEL))),
            pl.BlockSpec((tm, D_MODEL), lambda i: (i, 0)),
            pl.BlockSpec((M_V, D_MODEL), full),
            pl.BlockSpec((D_V, D_MODEL), full),
            pl.BlockSpec((D_MODEL, D_MODEL), full),
            pl.BlockSpec((1, 2 * D_MODEL), full),
        ],
        out_specs=pl.BlockSpec((tm, D_MODEL), lambda i: (i, 0)),
        out_shape=jax.ShapeDtypeStruct((T, D_MODEL), F32),
        compiler_params=_cparams(("arbitrary",)),
        name="merge",
    )(hm, hd, z, x2d, w_bm, w_bd, w_out, b_gate)


def _memkv_kernel(mem_ref, g_ref, w_ref, out_ref):
    out_ref[...] = _dot(_rms(mem_ref[...], g_ref[...]).astype(BF16), w_ref[...]).astype(BF16)


def _memkv(mem2d, g, wkv, n_mem):
    R = mem2d.shape[0]
    return pl.pallas_call(
        _memkv_kernel,
        grid=(R // n_mem,),
        in_specs=[
            pl.BlockSpec((n_mem, D_MODEL), lambda i: (i, 0)),
            pl.BlockSpec((1, D_MODEL), lambda i: (0, 0)),
            pl.BlockSpec((D_MODEL, 2 * D_MODEL), lambda i: (0, 0)),
        ],
        out_specs=pl.BlockSpec((n_mem, 2 * D_MODEL), lambda i: (i, 0)),
        out_shape=jax.ShapeDtypeStruct((R, 2 * D_MODEL), BF16),
        compiler_params=_cparams(("arbitrary",)),
        name="memkv",
    )(mem2d, g, wkv)


def _xattn_kernel(x_ref, g_ref, wq_ref, kv_ref, wo_ref, out_ref, o_ref):
    x = x_ref[...]
    h = _rms(x, g_ref[...]).astype(BF16)
    q = (_dot(h, wq_ref[...]) * (X_DH ** -0.5)).astype(BF16)
    scores = [_dot_nt(q[:, hd * X_DH:(hd + 1) * X_DH], kv_ref[:, hd * X_DH:(hd + 1) * X_DH]) for hd in range(X_HEADS)]
    for hd in range(X_HEADS):
        vh = kv_ref[:, D_MODEL + hd * X_DH:D_MODEL + (hd + 1) * X_DH]
        s = scores[hd]
        p = jnp.exp(s - jnp.max(s, axis=-1, keepdims=True))
        p = p / jnp.sum(p, axis=-1, keepdims=True)
        o_ref[:, hd * X_DH:(hd + 1) * X_DH] = _dot(p.astype(BF16), vh).astype(BF16)
    out_ref[...] = x + _dot(o_ref[...], wo_ref[...])


def _xattn(x1, g, wq, kvmem, wo, S, n_mem, tm):
    T = x1.shape[0]
    per_b = S // tm
    full = lambda i: (0, 0)
    return pl.pallas_call(
        _xattn_kernel,
        grid=(T // tm,),
        in_specs=[
            pl.BlockSpec((tm, D_MODEL), lambda i: (i, 0)),
            pl.BlockSpec((1, D_MODEL), full),
            pl.BlockSpec((D_MODEL, D_MODEL), full),
            pl.BlockSpec((n_mem, 2 * D_MODEL), lambda i: (i // per_b, 0)),
            pl.BlockSpec((D_MODEL, D_MODEL), full),
        ],
        out_specs=pl.BlockSpec((tm, D_MODEL), lambda i: (i, 0)),
        out_shape=jax.ShapeDtypeStruct((T, D_MODEL), F32),
        scratch_shapes=[pltpu.VMEM((tm, D_MODEL), BF16)],
        compiler_params=_cparams(("arbitrary",)),
        name="xattn",
    )(x1, g, wq, kvmem, wo)


def _router_kernel(x_ref, g_ref, wrh_ref, wrl_ref, br_ref, hp_ref, ids_ref, tw_ref, cnt_ref):
    @pl.when(pl.program_id(0) == 0)
    def _():
        cnt_ref[...] = jnp.zeros_like(cnt_ref)

    hn = _rms(x_ref[...], g_ref[...])
    hh, hl = _split_bf16(hn)
    hp_ref[...] = hh
    logits = _dot(hh, wrh_ref[...]) + _dot(hh, wrl_ref[...]) + _dot(hl, wrh_ref[...]) + br_ref[...]
    lane = lax.broadcasted_iota(I32, logits.shape, 1)
    lanef = lane.astype(F32)
    ids = jnp.zeros(logits.shape, F32)
    tw = jnp.zeros(logits.shape, F32)
    onehot = jnp.zeros(logits.shape, F32)
    v0 = None
    den = None
    for kk in range(TOP_K):
        mx = jnp.max(logits, axis=-1, keepdims=True)
        idx = jnp.min(jnp.where(logits == mx, lanef, float(LANES)), axis=-1, keepdims=True)
        sel = lanef == idx
        if kk == 0:
            v0 = mx
        e = jnp.exp(mx - v0)
        den = e if den is None else den + e
        ids = jnp.where(lane == kk, idx, ids)
        tw = jnp.where(lane == kk, e, tw)
        onehot = jnp.where(sel, 1.0, onehot)
        logits = jnp.where(sel, -jnp.inf, logits)
    ids_ref[...] = ids.astype(I32)
    tw_ref[...] = tw / den
    cnt_ref[...] += _ceil_rows(jnp.sum(onehot, axis=0, keepdims=True))


def _router(x2, g, wr_hi, wr_lo, b_r, tm):
    T = x2.shape[0]
    full = lambda i: (0, 0)
    return pl.pallas_call(
        _router_kernel,
        grid=(T // tm,),
        in_specs=[
            pl.BlockSpec((tm, D_MODEL), lambda i: (i, 0)),
            pl.BlockSpec((1, D_MODEL), full),
            pl.BlockSpec((D_MODEL, LANES), full),
            pl.BlockSpec((D_MODEL, LANES), full),
            pl.BlockSpec((1, LANES), full),
        ],
        out_specs=[
            pl.BlockSpec((tm, D_MODEL), lambda i: (i, 0)),
            pl.BlockSpec((tm, LANES), lambda i: (i, 0)),
            pl.BlockSpec((tm, LANES), lambda i: (i, 0)),
            pl.BlockSpec((1, LANES), full),
        ],
        out_shape=[
            jax.ShapeDtypeStruct((T, D_MODEL), BF16),
            jax.ShapeDtypeStruct((T, LANES), I32),
            jax.ShapeDtypeStruct((T, LANES), F32),
            jax.ShapeDtypeStruct((1, LANES), F32),
        ],
        compiler_params=_cparams(("arbitrary",)),
        name="router",
    )(x2, g, wr_hi, wr_lo, b_r)


SEG_WORDS = SUBLANES * LANES


def _slots_kernel(ids_ref, start_ref, ls_ref, lst_ref, seg_ref, run_ref):
    @pl.when(pl.program_id(0) == 0)
    def _():
        run_ref[...] = jnp.zeros_like(run_ref)

    ids = ids_ref[...]
    tm = ids.shape[0]
    lane = lax.broadcasted_iota(I32, ids.shape, 1)
    sels = [lane == ids[:, kk:kk + 1] for kk in range(TOP_K)]
    onehot = jnp.zeros(ids.shape, F32)
    for s in sels:
        onehot = jnp.where(s, 1.0, onehot)
    c8 = _ceil_rows(jnp.sum(onehot, axis=0, keepdims=True))
    er = lax.broadcasted_iota(I32, (LANES, LANES), 0)
    ec = lax.broadcasted_iota(I32, (LANES, LANES), 1)
    before = jnp.where(er < ec, 1.0, 0.0).astype(BF16)
    pieces = jnp.broadcast_to(c8 * (1.0 / SUBLANES), (SUBLANES, LANES)).astype(BF16)
    lo = _dot(pieces, before)[0:1, :] * SUBLANES
    r = lax.broadcasted_iota(I32, (tm, tm), 0)
    c = lax.broadcasted_iota(I32, (tm, tm), 1)
    strict = jnp.where(c < r, 1.0, 0.0).astype(BF16)
    slot = _dot(strict, onehot.astype(BF16)) + lo
    ls = jnp.zeros(ids.shape, F32)
    for kk, s in enumerate(sels):
        pk = jnp.sum(jnp.where(s, slot, 0.0), axis=-1, keepdims=True)
        ls = jnp.where(lane == kk, pk, ls)
    ls_ref[...] = ls
    hi = jnp.floor(ls * (1.0 / 32.0))
    rem = ls - 32.0 * hi
    pick = jnp.where(lax.broadcasted_iota(I32, (SUBLANES, LANES), 0) == lax.broadcasted_iota(I32, (SUBLANES, LANES), 1),
                     1.0, 0.0).astype(BF16)
    lst_ref[...] = 32.0 * _dot_nt(pick, hi.astype(BF16)) + _dot_nt(pick, rem.astype(BF16))
    row = lax.broadcasted_iota(I32, (SUBLANES, LANES), 0)
    off = start_ref[...] + run_ref[...]
    total = jnp.sum(c8, axis=-1, keepdims=True)
    seg = jnp.where(row == 0, c8, jnp.where(row == 1, lo, jnp.where(row == 2, off, jnp.where(row == 3, total, 0.0))))
    seg_ref[...] = seg.astype(I32)
    run_ref[...] += c8


def _slots(ids, starts, tm):
    T = ids.shape[0]
    nt = T // tm
    return pl.pallas_call(
        _slots_kernel,
        grid=(nt,),
        in_specs=[
            pl.BlockSpec((tm, LANES), lambda i: (i, 0)),
            pl.BlockSpec((1, LANES), lambda i: (0, 0)),
        ],
        out_specs=[
            pl.BlockSpec((tm, LANES), lambda i: (i, 0)),
            pl.BlockSpec((SUBLANES, tm), lambda i: (0, i)),
            pl.BlockSpec((SUBLANES, LANES), lambda i: (i, 0)),
        ],
        out_shape=[
            jax.ShapeDtypeStruct((T, LANES), F32),
            jax.ShapeDtypeStruct((SUBLANES, T), F32),
            jax.ShapeDtypeStruct((nt * SUBLANES, LANES), I32),
        ],
        scratch_shapes=[pltpu.VMEM((1, LANES), F32)],
        compiler_params=_cparams(("arbitrary",)),
        name="slots",
    )(ids, starts)


def _local_rows(tm):
    need = tm * TOP_K + N_EXPERTS * (SUBLANES - 1)
    return ((need + LANES - 1) // LANES) * LANES


BIG_PIECE = 4 * SUBLANES


def _segment_starts(seg, make_copy):
    def expert(e, carry):
        cnt = seg(e)
        lo = seg(LANES + e)
        off = seg(2 * LANES + e)
        n_big = lax.shift_right_logical(cnt, 5)
        n_small = lax.shift_right_logical(cnt & (BIG_PIECE - 1), 3)

        def big(j, carry2):
            d = j * BIG_PIECE
            make_copy(pl.multiple_of(lo + d, SUBLANES), pl.multiple_of(off + d, SUBLANES), BIG_PIECE).start()
            return carry2

        def small(j, carry2):
            d = n_big * BIG_PIECE + j * SUBLANES
            make_copy(pl.multiple_of(lo + d, SUBLANES), pl.multiple_of(off + d, SUBLANES), SUBLANES).start()
            return carry2

        lax.fori_loop(0, n_big, big, 0)
        lax.fori_loop(0, n_small, small, 0)
        return carry

    for e in range(N_EXPERTS):
        expert(e, 0)


def _segment_waits(total_rows, make_copy):
    def big(j, carry):
        make_copy(0, 0, BIG_PIECE).wait()
        return carry

    def small(j, carry):
        make_copy(0, 0, SUBLANES).wait()
        return carry

    lax.fori_loop(0, lax.shift_right_logical(total_rows, 5), big, 0)
    lax.fori_loop(0, lax.shift_right_logical(total_rows & (BIG_PIECE - 1), 3), small, 0)


def _dispatch_kernel(h_ref, lst_ref, seg_hbm, xs_ref, sbuf_ref, seg_smem, prev_smem, sem_seg, sem_rows, *, tm):
    i = pl.program_id(0)
    n = pl.num_programs(0)
    slot = i % 2
    rows = sbuf_ref.shape[1]
    cp = pltpu.make_async_copy(seg_hbm.at[i], seg_smem, sem_seg)
    cp.start()
    lst = lst_ref[...].astype(I32)
    rid = lax.broadcasted_iota(I32, (rows, tm), 0)
    perm = jnp.zeros((rows, tm), F32)
    for kk in range(TOP_K):
        perm = perm + jnp.where(rid == lst[kk:kk + 1, :], 1.0, 0.0)
    srt = _dot(perm.astype(BF16), h_ref[...])
    sbuf_ref[slot] = _pack_bf16_pairs(srt)
    cp.wait()

    def copy_from(s):
        def make_copy(lo, off, nrows):
            return pltpu.make_async_copy(sbuf_ref.at[s, pl.ds(lo, nrows), :], xs_ref.at[pl.ds(off, nrows), :],
                                         sem_rows.at[s])
        return make_copy

    _segment_starts(lambda k: seg_smem[k], copy_from(slot))

    @pl.when(i > 0)
    def _():
        _segment_waits(prev_smem[0], copy_from(1 - slot))

    prev_smem[0] = seg_smem[3 * LANES]

    @pl.when(i == n - 1)
    def _():
        _segment_waits(prev_smem[0], copy_from(slot))


def _dispatch(h, lst, seg2d, n_rows, tm):
    T = h.shape[0]
    rows = _local_rows(tm)
    return pl.pallas_call(
        functools.partial(_dispatch_kernel, tm=tm),
        grid=(T // tm,),
        in_specs=[
            pl.BlockSpec((tm, D_MODEL), lambda i: (i, 0)),
            pl.BlockSpec((SUBLANES, tm), lambda i: (0, i)),
            pl.BlockSpec(memory_space=pl.ANY),
        ],
        out_specs=pl.BlockSpec(memory_space=pl.ANY),
        out_shape=jax.ShapeDtypeStruct((n_rows, D_MODEL // 2), U32),
        scratch_shapes=[
            pltpu.VMEM((2, rows, D_MODEL // 2), U32),
            pltpu.SMEM((SEG_WORDS,), I32),
            pltpu.SMEM((1,), I32),
            pltpu.SemaphoreType.DMA,
            pltpu.SemaphoreType.DMA((2,)),
        ],
        compiler_params=_cparams(("arbitrary",)),
        name="dispatch",
    )(h, lst, seg2d)


def _experts_kernel(te_ref, nu_ref, nv_ref, xs_ref, wgu_ref, bgu_ref, wdn_ref, bdn_ref, ys_ref, wgu_bf, wdn_bf):
    i = pl.program_id(0)
    half = D_MODEL // 2

    @pl.when((i == 0) | (te_ref[i] != te_ref[jnp.maximum(i - 1, 0)]))
    def _():
        wgu_bf[...] = wgu_ref[0].astype(BF16)
        wdn_bf[...] = wdn_ref[0].astype(BF16)

    @pl.when(i < nu_ref[0])
    def _():
        live = lax.broadcasted_iota(I32, xs_ref.shape, 0) < nv_ref[i]
        lo, hi = _unpack_bf16_pairs(jnp.where(live, xs_ref[...], jnp.uint32(0)))
        gu = (_dot(lo.astype(BF16), wgu_bf[:half, :]) + _dot(hi.astype(BF16), wgu_bf[half:, :])
              + bgu_ref[0])
        gate = jnp.minimum(gu[:, :D_FF], SWIGLU_LIMIT)
        up = jnp.clip(gu[:, D_FF:], -SWIGLU_LIMIT, SWIGLU_LIMIT)
        act = (up + 1.0) * (gate * _sigmoid(SWIGLU_ALPHA * gate))
        y = _dot(act.astype(BF16), wdn_bf[...]) + bdn_ref[0]
        ys_ref[...] = _pack_bf16_pairs(y)

    @pl.when(i >= nu_ref[0])
    def _():
        ys_ref[...] = jnp.zeros_like(ys_ref)


def _experts(tile_expert, n_used, tile_valid, xs, w_gu, b_gu, w_dn, b_dn, tg):
    P = xs.shape[0]
    half = D_MODEL // 2
    grid_spec = pltpu.PrefetchScalarGridSpec(
        num_scalar_prefetch=3,
        grid=(P // tg,),
        in_specs=[
            pl.BlockSpec((tg, half), lambda i, te, nu, nv: (jnp.minimum(i, jnp.maximum(nu[0] - 1, 0)), 0)),
            pl.BlockSpec((1, D_MODEL, 2 * D_FF), lambda i, te, nu, nv: (te[i], 0, 0)),
            pl.BlockSpec((1, 1, 2 * D_FF), lambda i, te, nu, nv: (te[i], 0, 0)),
            pl.BlockSpec((1, D_FF, D_MODEL), lambda i, te, nu, nv: (te[i], 0, 0)),
            pl.BlockSpec((1, 1, D_MODEL), lambda i, te, nu, nv: (te[i], 0, 0)),
        ],
        out_specs=pl.BlockSpec((tg, half), lambda i, te, nu, nv: (i, 0)),
        scratch_shapes=[pltpu.VMEM((D_MODEL, 2 * D_FF), BF16), pltpu.VMEM((D_FF, D_MODEL), BF16)],
    )
    return pl.pallas_call(
        _experts_kernel,
        grid_spec=grid_spec,
        out_shape=jax.ShapeDtypeStruct((P, half), U32),
        compiler_params=_cparams(("arbitrary",)),
        name="experts",
    )(tile_expert, n_used, tile_valid, xs, w_gu, b_gu, w_dn, b_dn)


def _combine_kernel(seg_hbm, ys_hbm, ls_ref, tw_ref, x_ref, g_ref, out_ref, ybuf_ref, seg_smem, sem_seg, sem_rows, *,
                    tm, final_norm):
    i = pl.program_id(0)
    n = pl.num_programs(0)
    slot = i % 2
    rows = ybuf_ref.shape[1]

    def copy_into(s):
        def make_copy(lo, off, nrows):
            return pltpu.make_async_copy(ys_hbm.at[pl.ds(off, nrows), :], ybuf_ref.at[s, pl.ds(lo, nrows), :],
                                         sem_rows.at[s])
        return make_copy

    def request(step, s):
        cp = pltpu.make_async_copy(seg_hbm.at[step], seg_smem.at[s], sem_seg)
        cp.start()
        cp.wait()
        _segment_starts(lambda k: seg_smem[s, k], copy_into(s))

    @pl.when(i == 0)
    def _():
        ybuf_ref[...] = jnp.zeros_like(ybuf_ref)
        request(0, 0)

    @pl.when(i + 1 < n)
    def _():
        request(i + 1, 1 - slot)

    ls = ls_ref[...].astype(I32)
    tw = tw_ref[...]
    cid = lax.broadcasted_iota(I32, (tm, rows), 1)
    wmat = jnp.zeros((tm, rows), F32)
    for kk in range(TOP_K):
        wmat = wmat + jnp.where(cid == ls[:, kk:kk + 1], tw[:, kk:kk + 1], 0.0)
    wmat = wmat.astype(BF16)
    _segment_waits(seg_smem[slot, 3 * LANES], copy_into(slot))
    lo, hi = _unpack_bf16_pairs(ybuf_ref[slot])
    moe = jnp.concatenate([_dot(wmat, lo.astype(BF16)), _dot(wmat, hi.astype(BF16))], axis=1)
    x3 = x_ref[...] + moe
    out_ref[...] = _rms(x3, g_ref[...]) if final_norm else x3


def _combine(seg2d, ys, ls, tw, x2, g, tm, final_norm):
    T = x2.shape[0]
    rows = _local_rows(tm)
    return pl.pallas_call(
        functools.partial(_combine_kernel, tm=tm, final_norm=final_norm),
        grid=(T // tm,),
        in_specs=[
            pl.BlockSpec(memory_space=pl.ANY),
            pl.BlockSpec(memory_space=pl.ANY),
            pl.BlockSpec((tm, LANES), lambda i: (i, 0)),
            pl.BlockSpec((tm, LANES), lambda i: (i, 0)),
            pl.BlockSpec((tm, D_MODEL), lambda i: (i, 0)),
            pl.BlockSpec((1, D_MODEL), lambda i: (0, 0)),
        ],
        out_specs=pl.BlockSpec((tm, D_MODEL), lambda i: (i, 0)),
        out_shape=jax.ShapeDtypeStruct((T, D_MODEL), F32),
        scratch_shapes=[
            pltpu.VMEM((2, rows, D_MODEL // 2), U32),
            pltpu.SMEM((2, SEG_WORDS), I32),
            pltpu.SemaphoreType.DMA,
            pltpu.SemaphoreType.DMA((2,)),
        ],
        compiler_params=_cparams(("arbitrary",)),
        name="combine",
    )(seg2d, ys, ls, tw, x2, g)


def _tiles(B, S):
    T = B * S
    return dict(
        tm_in=min(1024, T), tn_in=2048,
        ts=min(512, S),
        tq=min(1024, S),
        tm_proj=min(512, S),
        tm_route=min(256, T),
        tg=512,
    )


def _pad_lanes(a, n=LANES, value=0.0):
    return jnp.pad(a, ((0, 0), (0, n - a.shape[1])), constant_values=value)


def kernel(x, mem, norm_mix, w_in, conv_w, b_if, mlstm_gain, diff_lambda, diff_gain, w_branch_m, w_branch_d,
           b_gate, w_out, norm_xattn, norm_mem, wq_x, wkv_x, wo_x, norm_ffn, w_router, b_router, w_gu, b_gu,
           w_dn, b_dn, norm_final):
    B, S, D = x.shape
    n_mem = mem.shape[1]
    T = B * S
    depth = norm_mix.shape[0]
    tl = _tiles(B, S)
    x2d = x.reshape(T, D)
    mem2d = mem.reshape(B * n_mem, D)

    for l in range(depth):
        lam_init = 0.8 - 0.6 * math.exp(-0.3 * l)
        wl = w_in[l]
        if_lo = 2 * M_QK + 2 * M_V
        w_main = jnp.concatenate([wl[:, :if_lo], wl[:, if_lo + 2 * M_HEADS:]], axis=1).astype(BF16)
        w_if = wl[:, if_lo:if_lo + 2 * M_HEADS]
        w_ifp = _pad_lanes(w_if).astype(BF16)
        w_ift = w_if.T.astype(BF16)
        bif = _pad_lanes(b_if[l][None, :])
        bift = jnp.broadcast_to(b_if[l][:, None], (SUBLANES, LANES))

        z, zif, zift = _inproj(x2d, norm_mix[l][None, :], w_main, w_ifp, w_ift, tl["tm_in"], tl["tn_in"])
        hm = _mlstm(z, zif, zift, conv_w[l], bif, bift, mlstm_gain[l].reshape(1, M_V), B, S, tl["ts"])
        hd = _diffattn(z, diff_lambda[l], diff_gain[l][None, :], B, S, tl["tq"], lam_init)
        x1 = _merge(hm, hd, z, x2d, w_branch_m[l].astype(BF16), w_branch_d[l].astype(BF16),
                    w_out[l].astype(BF16), b_gate[l][None, :], tl["tm_proj"])

        kvmem = _memkv(mem2d, norm_mem[l][None, :], wkv_x[l].astype(BF16), n_mem)
        x2 = _xattn(x1, norm_xattn[l][None, :], wq_x[l].astype(BF16), kvmem, wo_x[l].astype(BF16),
                    S, n_mem, tl["tm_proj"])

        wr = _pad_lanes(w_router[l])
        wr_hi = wr.astype(BF16)
        wr_lo = (wr - wr_hi.astype(F32)).astype(BF16)
        br = _pad_lanes(b_router[l][None, :], value=-jnp.inf)
        tm_r = tl["tm_route"]
        tg = tl["tg"]
        hp, ids, tw, cnt = _router(x2, norm_ffn[l][None, :], wr_hi, wr_lo, br, tm_r)

        counts = cnt[0, :N_EXPERTS].astype(I32)
        padded = ((counts + tg - 1) // tg) * tg
        ends = jnp.cumsum(padded)
        starts = ends - padded
        max_rows = T * TOP_K + (T // tm_r) * N_EXPERTS * (SUBLANES - 1)
        n_tiles = -(-max_rows // tg) + N_EXPERTS
        tile_row0 = jnp.arange(n_tiles, dtype=I32) * tg
        tile_expert = jnp.minimum(jnp.sum((tile_row0[:, None] >= ends[None, :]).astype(I32), axis=1), N_EXPERTS - 1)
        n_used = (ends[-1] // tg).astype(I32).reshape(1)
        last_used = tile_expert[jnp.maximum(n_used[0] - 1, 0)]
        tile_expert = jnp.where(tile_row0 < ends[-1], tile_expert, last_used)
        tile_valid = jnp.clip((starts + counts)[tile_expert] - tile_row0, 0, tg).astype(I32)

        ls, lst, seg = _slots(ids, _pad_lanes(starts.astype(F32)[None, :]), tm_r)
        seg2d = seg.reshape(T // tm_r, SEG_WORDS)

        xs = _dispatch(hp, lst, seg2d, n_tiles * tg, tm_r)
        ys = _experts(tile_expert, n_used, tile_valid, xs, w_gu[l], b_gu[l][:, None, :], w_dn[l], b_dn[l][:, None, :],
                      tg)
        x2d = _combine(seg2d, ys, ls, tw, x2, norm_final[None, :], tm_r, final_norm=(l == depth - 1))
    return x2d.reshape(B, S, D)
```

```python
import functools
import math

import jax
import jax.numpy as jnp
from jax import lax
from jax.experimental import pallas as pl
from jax.experimental.pallas import tpu as pltpu

F32 = jnp.float32
BF16 = jnp.bfloat16
U32 = jnp.uint32
I32 = jnp.int32

EPS = 1e-6
CHUNK = 64
D_MODEL = 1024
M_HEADS = 4
M_DK = 128
M_DV = 256
M_QK = M_HEADS * M_DK
M_V = M_HEADS * M_DV
CONV_W = 4
D_HEADS = 8
D_DH = 64
D_HP = 2
D_QK = D_HEADS * 2 * D_DH
D_V = D_HEADS * 2 * D_DH
X_HEADS = 4
X_DH = D_MODEL // X_HEADS
N_EXPERTS = 32
TOP_K = 4
D_FF = D_MODEL
SWIGLU_LIMIT = 7.0
SWIGLU_ALPHA = 1.702

LANES = 128
SUBLANES = 8
N_MAIN = 2 * M_QK + 2 * M_V + 2 * D_QK + D_V + 2 * D_MODEL
OFF_QM, OFF_KM, OFF_VM, OFF_OM = 0, M_QK, 2 * M_QK, 2 * M_QK + M_V
OFF_QD = OFF_OM + M_V
OFF_KD = OFF_QD + D_QK
OFF_VD = OFF_KD + D_QK
OFF_G = OFF_VD + D_V

VMEM_LIMIT = 56 * 1024 * 1024


def _cparams(sem, vmem=VMEM_LIMIT):
    return pltpu.CompilerParams(dimension_semantics=sem, vmem_limit_bytes=vmem)


def _rms(x, g):
    return x * lax.rsqrt(jnp.mean(x * x, axis=-1, keepdims=True) + EPS) * g


def _split_bf16(x):
    hi = x.astype(BF16)
    lo = (x - hi.astype(F32)).astype(BF16)
    return hi, lo


def _dot(a, b):
    return jnp.dot(a, b, preferred_element_type=F32)


def _dot_nt(a, b):
    return lax.dot_general(a, b, (((1,), (1,)), ((), ())), preferred_element_type=F32)


def _sigmoid(x):
    return 1.0 / (1.0 + jnp.exp(-x))


def _log_sigmoid(x):
    return jnp.minimum(x, 0.0) - jnp.log(1.0 + jnp.exp(-jnp.abs(x)))


def _pack_bf16_pairs(x):
    w = x.shape[1] // 2
    u = lax.bitcast_convert_type(x, U32)
    r = (u + jnp.uint32(0x7FFF) + ((u >> 16) & jnp.uint32(1))) >> 16
    return r[:, :w] | (r[:, w:] << 16)


def _ceil_rows(x):
    return jnp.floor((x + (SUBLANES - 1)) * (1.0 / SUBLANES)) * SUBLANES


def _unpack_bf16_pairs(p):
    lo = lax.bitcast_convert_type(p << 16, F32)
    hi = lax.bitcast_convert_type(p & jnp.uint32(0xFFFF0000), F32)
    return lo, hi


def _inproj_kernel(x_ref, g_ref, w_ref, wif_ref, wift_ref, z_ref, zif_ref, zift_ref, hn_ref):
    @pl.when(pl.program_id(1) == 0)
    def _():
        hn = _rms(x_ref[...], g_ref[...]).astype(BF16)
        hn_ref[...] = hn
        zif_ref[...] = _dot(hn, wif_ref[...])
        zift_ref[...] = _dot_nt(wift_ref[...], hn)

    z_ref[...] = _dot(hn_ref[...], w_ref[...]).astype(BF16)


def _inproj(x2d, g, w_main, w_if, w_ift, tm, tn):
    T = x2d.shape[0]
    return pl.pallas_call(
        _inproj_kernel,
        grid=(T // tm, N_MAIN // tn),
        in_specs=[
            pl.BlockSpec((tm, D_MODEL), lambda i, j: (i, 0)),
            pl.BlockSpec((1, D_MODEL), lambda i, j: (0, 0)),
            pl.BlockSpec((D_MODEL, tn), lambda i, j: (0, j)),
            pl.BlockSpec((D_MODEL, LANES), lambda i, j: (0, 0)),
            pl.BlockSpec((SUBLANES, D_MODEL), lambda i, j: (0, 0)),
        ],
        out_specs=[
            pl.BlockSpec((tm, tn), lambda i, j: (i, j)),
            pl.BlockSpec((tm, LANES), lambda i, j: (i, 0)),
            pl.BlockSpec((SUBLANES, tm), lambda i, j: (0, i)),
        ],
        out_shape=[
            jax.ShapeDtypeStruct((T, N_MAIN), BF16),
            jax.ShapeDtypeStruct((T, LANES), F32),
            jax.ShapeDtypeStruct((SUBLANES, T), F32),
        ],
        scratch_shapes=[pltpu.VMEM((tm, D_MODEL), BF16)],
        compiler_params=_cparams(("arbitrary", "arbitrary")),
        name="inproj",
    )(x2d, g, w_main, w_if, w_ift)


def _mlstm_kernel(q_ref, k_ref, v_ref, om_ref, zif_ref, zift_ref, cw_ref, bif_ref, bift_ref, mg_ref,
                  out_ref, qc_ref, kc_ref, kt_ref, carry_ref, c_ref, n_ref, m_ref,
                  bd_ref, bdt_ref, brep_ref, grow_ref, brow_ref, *, ts):
    nchunk = ts // CHUNK
    L = CHUNK

    @pl.when(pl.program_id(1) == 0)
    def _():
        carry_ref[...] = jnp.zeros_like(carry_ref)
        c_ref[...] = jnp.zeros_like(c_ref)
        n_ref[...] = jnp.zeros_like(n_ref)
        m_ref[...] = jnp.zeros_like(m_ref)
        rt = lax.broadcasted_iota(I32, (ts, ts), 0)
        ct = lax.broadcasted_iota(I32, (ts, ts), 1)
        same = (rt // L) == (ct // L)
        bd_ref[...] = jnp.where(same, jnp.where(ct <= rt, 1.0, 0.0), 0.0).astype(BF16)
        bdt_ref[...] = jnp.where(same, jnp.where(rt <= ct, 1.0, 0.0), 0.0).astype(BF16)

    row8 = lax.broadcasted_iota(I32, (SUBLANES, M_QK), 0)

    def conv_silu(x, prev8, w):
        acc = w[CONV_W - 1:CONV_W, :] * x
        for s in range(1, CONV_W):
            xs = pltpu.roll(x, s, 0)
            top = jnp.where(row8 < s, pltpu.roll(prev8, s, 0), xs[0:SUBLANES])
            xs = jnp.concatenate([top, xs[SUBLANES:]], axis=0)
            acc = acc + w[CONV_W - 1 - s:CONV_W - s, :] * xs
        return acc * _sigmoid(acc)

    def conv_body(c, carry):
        r0 = pl.multiple_of(c * L, L)
        xq = q_ref[pl.ds(r0, L), :].astype(F32)
        xk = k_ref[pl.ds(r0, L), :].astype(F32)
        yq = conv_silu(xq, carry_ref[:, 0:M_QK], cw_ref[:, 0:M_QK]) * (M_DK ** -0.5)
        yk = conv_silu(xk, carry_ref[:, M_QK:2 * M_QK], cw_ref[:, M_QK:2 * M_QK])
        qc_ref[pl.ds(r0, L), :] = yq.astype(BF16)
        kc_ref[pl.ds(r0, L), :] = yk.astype(BF16)
        for h in range(M_HEADS):
            kt_ref[c, h] = yk[:, h * M_DK:(h + 1) * M_DK].T
        carry_ref[:, 0:M_QK] = xq[L - SUBLANES:L]
        carry_ref[:, M_QK:2 * M_QK] = xk[L - SUBLANES:L]
        return carry

    lax.fori_loop(0, nchunk, conv_body, 0, unroll=2)

    ti = lax.broadcasted_iota(I32, (L, L), 0)
    si = lax.broadcasted_iota(I32, (L, L), 1)
    causal = si <= ti
    lane_row = lax.broadcasted_iota(I32, (LANES, LANES), 0)
    ones_l = jnp.ones((L, LANES), BF16)

    lf_col = _log_sigmoid(zif_ref[...] + bif_ref[...])
    ch, cl = _split_bf16(lf_col)
    b_col_all = _dot(bd_ref[...], ch) + _dot(bd_ref[...], cl)
    bh, bl = _split_bf16(b_col_all)
    for h in range(M_HEADS):
        sel_f = jnp.where(lane_row == M_HEADS + h, 1.0, 0.0).astype(BF16)
        brep_ref[h] = _dot(bh, sel_f) + _dot(bl, sel_f)
    g_row_all = zift_ref[...] + bift_ref[:, 0:1]
    rh, rl = _split_bf16(_log_sigmoid(g_row_all))
    b_row_tile = _dot(rh, bdt_ref[...]) + _dot(rl, bdt_ref[...])
    for cc in range(nchunk):
        grow_ref[cc] = g_row_all[:, cc * L:(cc + 1) * L]
        brow_ref[cc] = b_row_tile[:, cc * L:(cc + 1) * L]

    def chunk_body(c, carry):
        r0 = pl.multiple_of(c * L, L)
        g_row = grow_ref[c]
        b_row_all = brow_ref[c]
        early = []
        for h in range(M_HEADS):
            b_rep = brep_ref[h, pl.ds(r0, L), :]
            i_row = g_row[h:h + 1, :]
            b_row = b_row_all[M_HEADS + h:M_HEADS + h + 1, :]
            b_last = b_rep[L - 1:L, :]
            q = qc_ref[pl.ds(r0, L), h * M_DK:(h + 1) * M_DK]
            k = kc_ref[pl.ds(r0, L), h * M_DK:(h + 1) * M_DK]
            vext = jnp.concatenate([v_ref[pl.ds(r0, L), h * M_DV:(h + 1) * M_DV], ones_l], axis=1)
            dm = jnp.where(causal, b_rep[:, :L] - b_row + i_row, -jnp.inf)
            m_loc = jnp.max(dm, axis=-1, keepdims=True)
            qk = _dot_nt(q, k)
            gk_row = b_last[:, :L] - b_row + i_row
            g_max = jnp.max(gk_row, axis=-1, keepdims=True)
            kwt = (kt_ref[c, h] * jnp.exp(gk_row - g_max)).astype(BF16)
            kv = _dot(kwt, vext)
            c_old = c_ref[h]
            n_old = n_ref[h]
            qcn = _dot(q, jnp.concatenate([c_old, n_old], axis=1).astype(BF16))
            early.append((b_rep, b_last, vext, dm, m_loc, qk, g_max, kv, c_old, n_old, qcn))
        pvs = []
        for h in range(M_HEADS):
            b_rep, b_last, vext, dm, m_loc, qk, g_max, kv, c_old, n_old, qcn = early[h]
            s_loc = qk * jnp.exp(dm - m_loc)
            pvs.append(_dot(s_loc.astype(BF16), vext))
        for h in range(M_HEADS):
            b_rep, b_last, vext, dm, m_loc, qk, g_max, kv, c_old, n_old, qcn = early[h]
            pv = pvs[h]
            m_prev = m_ref[h:h + 1, :]
            inter = b_rep + m_prev
            m_t = jnp.maximum(inter, m_loc)
            w_inter = jnp.exp(inter - m_t)
            r_loc = jnp.exp(m_loc - m_t)
            den = r_loc * pv[:, M_DV:] + w_inter * qcn[:, M_DV:]
            inv = 1.0 / jnp.maximum(jnp.abs(den), jnp.exp(-m_t))
            hv = (jnp.concatenate([r_loc * inv] * 2, axis=1) * pv[:, :M_DV]
                  + jnp.concatenate([w_inter * inv] * 2, axis=1) * qcn[:, :M_DV])
            m_new = jnp.maximum(b_last + m_prev, g_max)
            decay = jnp.exp(b_last + m_prev - m_new)
            sc_loc = jnp.exp(g_max - m_new)
            c_ref[h] = (jnp.concatenate([decay] * 2, axis=1) * c_old
                        + jnp.concatenate([sc_loc] * 2, axis=1) * kv[:, :M_DV])
            n_ref[h] = decay * n_old + sc_loc * kv[:, M_DV:]
            m_ref[h:h + 1, :] = m_new
            hn = _rms(hv, mg_ref[:, h * M_DV:(h + 1) * M_DV])
            og = _sigmoid(om_ref[pl.ds(r0, L), h * M_DV:(h + 1) * M_DV].astype(F32))
            out_ref[pl.ds(r0, L), h * M_DV:(h + 1) * M_DV] = (og * hn).astype(BF16)
        return carry

    lax.fori_loop(0, nchunk, chunk_body, 0, unroll=4)


def _mlstm(z, zif, zift, conv_w, bif, bift, m_gain, B, S, ts):
    T = B * S
    nt = S // ts
    nck = ts // CHUNK
    row = lambda b, t: b * nt + t
    return pl.pallas_call(
        functools.partial(_mlstm_kernel, ts=ts),
        grid=(B, nt),
        in_specs=[
            pl.BlockSpec((ts, M_QK), lambda b, t: (row(b, t), OFF_QM // M_QK)),
            pl.BlockSpec((ts, M_QK), lambda b, t: (row(b, t), OFF_KM // M_QK)),
            pl.BlockSpec((ts, M_V), lambda b, t: (row(b, t), OFF_VM // M_V)),
            pl.BlockSpec((ts, M_V), lambda b, t: (row(b, t), OFF_OM // M_V)),
            pl.BlockSpec((ts, LANES), lambda b, t: (row(b, t), 0)),
            pl.BlockSpec((SUBLANES, ts), lambda b, t: (0, row(b, t))),
            pl.BlockSpec((CONV_W, 2 * M_QK), lambda b, t: (0, 0)),
            pl.BlockSpec((1, LANES), lambda b, t: (0, 0)),
            pl.BlockSpec((SUBLANES, LANES), lambda b, t: (0, 0)),
            pl.BlockSpec((1, M_V), lambda b, t: (0, 0)),
        ],
        out_specs=pl.BlockSpec((ts, M_V), lambda b, t: (row(b, t), 0)),
        out_shape=jax.ShapeDtypeStruct((T, M_V), BF16),
        scratch_shapes=[
            pltpu.VMEM((ts, M_QK), BF16),
            pltpu.VMEM((ts, M_QK), BF16),
            pltpu.VMEM((nck, M_HEADS, M_DK, CHUNK), F32),
            pltpu.VMEM((SUBLANES, 2 * M_QK), F32),
            pltpu.VMEM((M_HEADS, M_DK, M_DV), F32),
            pltpu.VMEM((M_HEADS, M_DK, LANES), F32),
            pltpu.VMEM((SUBLANES, LANES), F32),
            pltpu.VMEM((ts, ts), BF16),
            pltpu.VMEM((ts, ts), BF16),
            pltpu.VMEM((M_HEADS, ts, LANES), F32),
            pltpu.VMEM((nck, SUBLANES, CHUNK), F32),
            pltpu.VMEM((nck, SUBLANES, CHUNK), F32),
        ],
        compiler_params=_cparams(("arbitrary", "arbitrary")),
        name="mlstm",
    )(z, z, z, z, zif, zift, conv_w, bif, bift, m_gain)


def _diffattn_kernel(q_ref, k_ref, v_ref, lam_ref, gain_ref, out_ref, m_ref, a_ref, *, tq, lam_init):
    qi = pl.program_id(2)
    w = 2 * D_DH
    lane = lax.broadcasted_iota(I32, (1, w), 1)
    scale = jnp.asarray(D_DH ** -0.5, BF16)
    qs = []
    for hh in range(D_HP):
        q = q_ref[:, hh * w:(hh + 1) * w]
        qs.append((jnp.where(lane < D_DH, q, jnp.zeros_like(q)) * scale,
                   jnp.where(lane >= D_DH, q, jnp.zeros_like(q)) * scale))
    ones = jnp.ones((tq, w), BF16)

    def block(k0, nk, rows=slice(None), mask=None, first=False):
        n_stream = 2 * D_HP
        scores = []
        for hh in range(D_HP):
            k = k_ref[pl.ds(k0, nk), hh * w:(hh + 1) * w]
            for comp in range(2):
                s = _dot_nt(qs[hh][comp][rows], k)
                scores.append(s if mask is None else jnp.where(mask, s, -jnp.inf))
        m_news, alphas, probs = [], [], []
        for i in range(n_stream):
            s = scores[i]
            s_max = jnp.max(s, axis=-1, keepdims=True)
            if first:
                m_new = jnp.broadcast_to(s_max, (s.shape[0], w))
                alphas.append(None)
            else:
                m_old = m_ref[i, rows, :]
                m_new = jnp.maximum(m_old, s_max)
                alphas.append(jnp.exp(m_old - m_new))
            m_news.append(m_new)
            probs.append(jnp.exp(s - jnp.concatenate([m_new] * (s.shape[1] // w), axis=1)).astype(BF16))
        pvs = []
        for hh in range(D_HP):
            vext = jnp.concatenate([v_ref[pl.ds(k0, nk), hh * w:(hh + 1) * w], ones[:nk]], axis=1)
            for comp in range(2):
                pvs.append(_dot(probs[2 * hh + comp], vext))
        for i in range(n_stream):
            if first:
                a_ref[i, rows, :] = pvs[i]
            else:
                a_ref[i, rows, :] = jnp.concatenate([alphas[i]] * 2, axis=1) * a_ref[i, rows, :] + pvs[i]
            m_ref[i, rows, :] = m_news[i]

    hq = tq // 2
    d0 = pl.multiple_of(qi * tq, tq)

    def chunk_mask(q0, nk):
        rq = (lax.broadcasted_iota(I32, (hq, nk), 0) + q0) // CHUNK
        ck = lax.broadcasted_iota(I32, (hq, nk), 1) // CHUNK
        return ck <= rq

    block(d0, hq, rows=slice(0, hq), mask=chunk_mask(0, hq), first=True)
    block(d0, tq, rows=slice(hq, tq), mask=chunk_mask(hq, tq), first=True)

    def body(jj, carry):
        block(pl.multiple_of(jj * tq, tq), tq)
        return carry

    lax.fori_loop(0, qi, body, 0)

    lp = lam_ref[...]
    lam = (jnp.exp(jnp.sum(lp[0:1, :] * lp[1:2, :], axis=-1, keepdims=True))
           - jnp.exp(jnp.sum(lp[2:3, :] * lp[3:4, :], axis=-1, keepdims=True)) + lam_init)
    for hh in range(D_HP):
        a1 = a_ref[2 * hh]
        a2 = a_ref[2 * hh + 1]
        o = a1[:, :w] / a1[:, w:] - lam * (a2[:, :w] / a2[:, w:])
        out_ref[:, hh * w:(hh + 1) * w] = (_rms(o, gain_ref[...]) * (1.0 - lam_init)).astype(BF16)


def _diffattn(z, lam_p, d_gain, B, S, tq, lam_init):
    T = B * S
    nq = S // tq
    w = 2 * D_DH
    wp = D_HP * w
    return pl.pallas_call(
        functools.partial(_diffattn_kernel, tq=tq, lam_init=lam_init),
        grid=(B, D_HEADS // D_HP, nq),
        in_specs=[
            pl.BlockSpec((tq, wp), lambda b, h, i: (b * nq + i, OFF_QD // wp + h)),
            pl.BlockSpec((S, wp), lambda b, h, i: (b, OFF_KD // wp + h)),
            pl.BlockSpec((S, wp), lambda b, h, i: (b, OFF_VD // wp + h)),
            pl.BlockSpec((4, D_DH), lambda b, h, i: (0, 0)),
            pl.BlockSpec((1, w), lambda b, h, i: (0, 0)),
        ],
        out_specs=pl.BlockSpec((tq, wp), lambda b, h, i: (b * nq + i, h)),
        out_shape=jax.ShapeDtypeStruct((T, D_V), BF16),
        scratch_shapes=[
            pltpu.VMEM((2 * D_HP, tq, w), F32), pltpu.VMEM((2 * D_HP, tq, 2 * w), F32),
        ],
        compiler_params=_cparams(("arbitrary", "arbitrary", "arbitrary")),
        name="diffattn",
    )(z, z, z, lam_p, d_gain)


def _merge_kernel(hm_ref, hd_ref, gz_ref, x_ref, wbm_ref, wbd_ref, wout_ref, bg_ref, out_ref):
    bm = _dot(hm_ref[...], wbm_ref[...])
    bd = _dot(hd_ref[...], wbd_ref[...])
    g = _sigmoid(gz_ref[...].astype(F32) + bg_ref[...])
    merged = g[:, :D_MODEL] * bm + g[:, D_MODEL:] * bd
    out_ref[...] = x_ref[...] + _dot(merged.astype(BF16), wout_ref[...])


def _merge(hm, hd, z, x2d, w_bm, w_bd, w_out, b_gate, tm):
    T = x2d.shape[0]
    full = lambda i: (0, 0)
    return pl.pallas_call(
        _merge_kernel,
        grid=(T // tm,),
        in_specs=[
            pl.BlockSpec((tm, M_V), lambda i: (i, 0)),
            pl.BlockSpec((tm, D_V), lambda i: (i, 0)),
            pl.BlockSpec((tm, 2 * D_MODEL), lambda i: (i, OFF_G // (2 * D_MODEL))),
            pl.BlockSpec((tm, D_MODEL), lambda i: (i, 0)),
            pl.BlockSpec((M_V, D_MODEL), full),
            pl.BlockSpec((D_V, D_MODEL), full),
            pl.BlockSpec((D_MODEL, D_MODEL), full),
            pl.BlockSpec((1, 2 * D_MODEL), full),
        ],
        out_specs=pl.BlockSpec((tm, D_MODEL), lambda i: (i, 0)),
        out_shape=jax.ShapeDtypeStruct((T, D_MODEL), F32),
        compiler_params=_cparams(("arbitrary",)),
        name="merge",
    )(hm, hd, z, x2d, w_bm, w_bd, w_out, b_gate)


def _memkv_kernel(mem_ref, g_ref, w_ref, out_ref):
    out_ref[...] = _dot(_rms(mem_ref[...], g_ref[...]).astype(BF16), w_ref[...]).astype(BF16)


def _memkv(mem2d, g, wkv, n_mem):
    R = mem2d.shape[0]
    return pl.pallas_call(
        _memkv_kernel,
        grid=(R // n_mem,),
        in_specs=[
            pl.BlockSpec((n_mem, D_MODEL), lambda i: (i, 0)),
            pl.BlockSpec((1, D_MODEL), lambda i: (0, 0)),
            pl.BlockSpec((D_MODEL, 2 * D_MODEL), lambda i: (0, 0)),
        ],
        out_specs=pl.BlockSpec((n_mem, 2 * D_MODEL), lambda i: (i, 0)),
        out_shape=jax.ShapeDtypeStruct((R, 2 * D_MODEL), BF16),
        compiler_params=_cparams(("arbitrary",)),
        name="memkv",
    )(mem2d, g, wkv)


def _xattn_kernel(x_ref, g_ref, wq_ref, kv_ref, wo_ref, out_ref, o_ref):
    x = x_ref[...]
    h = _rms(x, g_ref[...]).astype(BF16)
    q = (_dot(h, wq_ref[...]) * (X_DH ** -0.5)).astype(BF16)
    scores = [_dot_nt(q[:, hd * X_DH:(hd + 1) * X_DH], kv_ref[:, hd * X_DH:(hd + 1) * X_DH]) for hd in range(X_HEADS)]
    for hd in range(X_HEADS):
        vh = kv_ref[:, D_MODEL + hd * X_DH:D_MODEL + (hd + 1) * X_DH]
        s = scores[hd]
        p = jnp.exp(s - jnp.max(s, axis=-1, keepdims=True))
        p = p / jnp.sum(p, axis=-1, keepdims=True)
        o_ref[:, hd * X_DH:(hd + 1) * X_DH] = _dot(p.astype(BF16), vh).astype(BF16)
    out_ref[...] = x + _dot(o_ref[...], wo_ref[...])


def _xattn(x1, g, wq, kvmem, wo, S, n_mem, tm):
    T = x1.shape[0]
    per_b = S // tm
    full = lambda i: (0, 0)
    return pl.pallas_call(
        _xattn_kernel,
        grid=(T // tm,),
        in_specs=[
            pl.BlockSpec((tm, D_MODEL), lambda i: (i, 0)),
            pl.BlockSpec((1, D_MODEL), full),
            pl.BlockSpec((D_MODEL, D_MODEL), full),
            pl.BlockSpec((n_mem, 2 * D_MODEL), lambda i: (i // per_b, 0)),
            pl.BlockSpec((D_MODEL, D_MODEL), full),
        ],
        out_specs=pl.BlockSpec((tm, D_MODEL), lambda i: (i, 0)),
        out_shape=jax.ShapeDtypeStruct((T, D_MODEL), F32),
        scratch_shapes=[pltpu.VMEM((tm, D_MODEL), BF16)],
        compiler_params=_cparams(("arbitrary",)),
        name="xattn",
    )(x1, g, wq, kvmem, wo)


def _router_kernel(x_ref, g_ref, wrh_ref, wrl_ref, br_ref, hp_ref, ids_ref, tw_ref, cnt_ref):
    @pl.when(pl.program_id(0) == 0)
    def _():
        cnt_ref[...] = jnp.zeros_like(cnt_ref)

    hn = _rms(x_ref[...], g_ref[...])
    hh, hl = _split_bf16(hn)
    hp_ref[...] = hh
    logits = _dot(hh, wrh_ref[...]) + _dot(hh, wrl_ref[...]) + _dot(hl, wrh_ref[...]) + br_ref[...]
    lane = lax.broadcasted_iota(I32, logits.shape, 1)
    lanef = lane.astype(F32)
    ids = jnp.zeros(logits.shape, F32)
    tw = jnp.zeros(logits.shape, F32)
    onehot = jnp.zeros(logits.shape, F32)
    v0 = None
    den = None
    for kk in range(TOP_K):
        mx = jnp.max(logits, axis=-1, keepdims=True)
        idx = jnp.min(jnp.where(logits == mx, lanef, float(LANES)), axis=-1, keepdims=True)
        sel = lanef == idx
        if kk == 0:
            v0 = mx
        e = jnp.exp(mx - v0)
        den = e if den is None else den + e
        ids = jnp.where(lane == kk, idx, ids)
        tw = jnp.where(lane == kk, e, tw)
        onehot = jnp.where(sel, 1.0, onehot)
        logits = jnp.where(sel, -jnp.inf, logits)
    ids_ref[...] = ids.astype(I32)
    tw_ref[...] = tw / den
    cnt_ref[...] += _ceil_rows(jnp.sum(onehot, axis=0, keepdims=True))


def _router(x2, g, wr_hi, wr_lo, b_r, tm):
    T = x2.shape[0]
    full = lambda i: (0, 0)
    return pl.pallas_call(
        _router_kernel,
        grid=(T // tm,),
        in_specs=[
            pl.BlockSpec((tm, D_MODEL), lambda i: (i, 0)),
            pl.BlockSpec((1, D_MODEL), full),
            pl.BlockSpec((D_MODEL, LANES), full),
            pl.BlockSpec((D_MODEL, LANES), full),
            pl.BlockSpec((1, LANES), full),
        ],
        out_specs=[
            pl.BlockSpec((tm, D_MODEL), lambda i: (i, 0)),
            pl.BlockSpec((tm, LANES), lambda i: (i, 0)),
            pl.BlockSpec((tm, LANES), lambda i: (i, 0)),
            pl.BlockSpec((1, LANES), full),
        ],
        out_shape=[
            jax.ShapeDtypeStruct((T, D_MODEL), BF16),
            jax.ShapeDtypeStruct((T, LANES), I32),
            jax.ShapeDtypeStruct((T, LANES), F32),
            jax.ShapeDtypeStruct((1, LANES), F32),
        ],
        compiler_params=_cparams(("arbitrary",)),
        name="router",
    )(x2, g, wr_hi, wr_lo, b_r)


SEG_WORDS = SUBLANES * LANES


def _slots_kernel(ids_ref, start_ref, ls_ref, lst_ref, seg_ref, run_ref):
    @pl.when(pl.program_id(0) == 0)
    def _():
        run_ref[...] = jnp.zeros_like(run_ref)

    ids = ids_ref[...]
    tm = ids.shape[0]
    lane = lax.broadcasted_iota(I32, ids.shape, 1)
    sels = [lane == ids[:, kk:kk + 1] for kk in range(TOP_K)]
    onehot = jnp.zeros(ids.shape, F32)
    for s in sels:
        onehot = jnp.where(s, 1.0, onehot)
    c8 = _ceil_rows(jnp.sum(onehot, axis=0, keepdims=True))
    er = lax.broadcasted_iota(I32, (LANES, LANES), 0)
    ec = lax.broadcasted_iota(I32, (LANES, LANES), 1)
    before = jnp.where(er < ec, 1.0, 0.0).astype(BF16)
    pieces = jnp.broadcast_to(c8 * (1.0 / SUBLANES), (SUBLANES, LANES)).astype(BF16)
    lo = _dot(pieces, before)[0:1, :] * SUBLANES
    r = lax.broadcasted_iota(I32, (tm, tm), 0)
    c = lax.broadcasted_iota(I32, (tm, tm), 1)
    strict = jnp.where(c < r, 1.0, 0.0).astype(BF16)
    slot = _dot(strict, onehot.astype(BF16)) + lo
    ls = jnp.zeros(ids.shape, F32)
    for kk, s in enumerate(sels):
        pk = jnp.sum(jnp.where(s, slot, 0.0), axis=-1, keepdims=True)
        ls = jnp.where(lane == kk, pk, ls)
    ls_ref[...] = ls
    hi = jnp.floor(ls * (1.0 / 32.0))
    rem = ls - 32.0 * hi
    pick = jnp.where(lax.broadcasted_iota(I32, (SUBLANES, LANES), 0) == lax.broadcasted_iota(I32, (SUBLANES, LANES), 1),
                     1.0, 0.0).astype(BF16)
    lst_ref[...] = 32.0 * _dot_nt(pick, hi.astype(BF16)) + _dot_nt(pick, rem.astype(BF16))
    row = lax.broadcasted_iota(I32, (SUBLANES, LANES), 0)
    off = start_ref[...] + run_ref[...]
    total = jnp.sum(c8, axis=-1, keepdims=True)
    seg = jnp.where(row == 0, c8, jnp.where(row == 1, lo, jnp.where(row == 2, off, jnp.where(row == 3, total, 0.0))))
    seg_ref[...] = seg.astype(I32)
    run_ref[...] += c8


def _slots(ids, starts, tm):
    T = ids.shape[0]
    nt = T // tm
    return pl.pallas_call(
        _slots_kernel,
        grid=(nt,),
        in_specs=[
            pl.BlockSpec((tm, LANES), lambda i: (i, 0)),
            pl.BlockSpec((1, LANES), lambda i: (0, 0)),
        ],
        out_specs=[
            pl.BlockSpec((tm, LANES), lambda i: (i, 0)),
            pl.BlockSpec((SUBLANES, tm), lambda i: (0, i)),
            pl.BlockSpec((SUBLANES, LANES), lambda i: (i, 0)),
        ],
        out_shape=[
            jax.ShapeDtypeStruct((T, LANES), F32),
            jax.ShapeDtypeStruct((SUBLANES, T), F32),
            jax.ShapeDtypeStruct((nt * SUBLANES, LANES), I32),
        ],
        scratch_shapes=[pltpu.VMEM((1, LANES), F32)],
        compiler_params=_cparams(("arbitrary",)),
        name="slots",
    )(ids, starts)


def _local_rows(tm):
    need = tm * TOP_K + N_EXPERTS * (SUBLANES - 1)
    return ((need + LANES - 1) // LANES) * LANES


BIG_PIECE = 4 * SUBLANES


def _segment_starts(seg, make_copy):
    def expert(e, carry):
        cnt = seg(e)
        lo = seg(LANES + e)
        off = seg(2 * LANES + e)
        n_big = lax.shift_right_logical(cnt, 5)
        n_small = lax.shift_right_logical(cnt & (BIG_PIECE - 1), 3)

        def big(j, carry2):
            d = j * BIG_PIECE
            make_copy(pl.multiple_of(lo + d, SUBLANES), pl.multiple_of(off + d, SUBLANES), BIG_PIECE).start()
            return carry2

        def small(j, carry2):
            d = n_big * BIG_PIECE + j * SUBLANES
            make_copy(pl.multiple_of(lo + d, SUBLANES), pl.multiple_of(off + d, SUBLANES), SUBLANES).start()
            return carry2

        lax.fori_loop(0, n_big, big, 0)
        lax.fori_loop(0, n_small, small, 0)
        return carry

    for e in range(N_EXPERTS):
        expert(e, 0)


def _segment_waits(total_rows, make_copy):
    def big(j, carry):
        make_copy(0, 0, BIG_PIECE).wait()
        return carry

    def small(j, carry):
        make_copy(0, 0, SUBLANES).wait()
        return carry

    lax.fori_loop(0, lax.shift_right_logical(total_rows, 5), big, 0)
    lax.fori_loop(0, lax.shift_right_logical(total_rows & (BIG_PIECE - 1), 3), small, 0)


def _dispatch_kernel(h_ref, lst_ref, seg_hbm, xs_ref, sbuf_ref, seg_smem, prev_smem, sem_seg, sem_rows, *, tm):
    i = pl.program_id(0)
    n = pl.num_programs(0)
    slot = i % 2
    rows = sbuf_ref.shape[1]
    cp = pltpu.make_async_copy(seg_hbm.at[i], seg_smem, sem_seg)
    cp.start()
    lst = lst_ref[...].astype(I32)
    rid = lax.broadcasted_iota(I32, (rows, tm), 0)
    perm = jnp.zeros((rows, tm), F32)
    for kk in range(TOP_K):
        perm = perm + jnp.where(rid == lst[kk:kk + 1, :], 1.0, 0.0)
    srt = _dot(perm.astype(BF16), h_ref[...])
    sbuf_ref[slot] = _pack_bf16_pairs(srt)
    cp.wait()

    def copy_from(s):
        def make_copy(lo, off, nrows):
            return pltpu.make_async_copy(sbuf_ref.at[s, pl.ds(lo, nrows), :], xs_ref.at[pl.ds(off, nrows), :],
                                         sem_rows.at[s])
        return make_copy

    _segment_starts(lambda k: seg_smem[k], copy_from(slot))

    @pl.when(i > 0)
    def _():
        _segment_waits(prev_smem[0], copy_from(1 - slot))

    prev_smem[0] = seg_smem[3 * LANES]

    @pl.when(i == n - 1)
    def _():
        _segment_waits(prev_smem[0], copy_from(slot))


def _dispatch(h, lst, seg2d, n_rows, tm):
    T = h.shape[0]
    rows = _local_rows(tm)
    return pl.pallas_call(
        functools.partial(_dispatch_kernel, tm=tm),
        grid=(T // tm,),
        in_specs=[
            pl.BlockSpec((tm, D_MODEL), lambda i: (i, 0)),
            pl.BlockSpec((SUBLANES, tm), lambda i: (0, i)),
            pl.BlockSpec(memory_space=pl.ANY),
        ],
        out_specs=pl.BlockSpec(memory_space=pl.ANY),
        out_shape=jax.ShapeDtypeStruct((n_rows, D_MODEL // 2), U32),
        scratch_shapes=[
            pltpu.VMEM((2, rows, D_MODEL // 2), U32),
            pltpu.SMEM((SEG_WORDS,), I32),
            pltpu.SMEM((1,), I32),
            pltpu.SemaphoreType.DMA,
            pltpu.SemaphoreType.DMA((2,)),
        ],
        compiler_params=_cparams(("arbitrary",)),
        name="dispatch",
    )(h, lst, seg2d)


def _experts_kernel(te_ref, nu_ref, nv_ref, xs_ref, wgu_ref, bgu_ref, wdn_ref, bdn_ref, ys_ref, wgu_bf, wdn_bf):
    i = pl.program_id(0)
    half = D_MODEL // 2

    @pl.when((i == 0) | (te_ref[i] != te_ref[jnp.maximum(i - 1, 0)]))
    def _():
        wgu_bf[...] = wgu_ref[0].astype(BF16)
        wdn_bf[...] = wdn_ref[0].astype(BF16)

    @pl.when(i < nu_ref[0])
    def _():
        live = lax.broadcasted_iota(I32, xs_ref.shape, 0) < nv_ref[i]
        lo, hi = _unpack_bf16_pairs(jnp.where(live, xs_ref[...], jnp.uint32(0)))
        gu = (_dot(lo.astype(BF16), wgu_bf[:half, :]) + _dot(hi.astype(BF16), wgu_bf[half:, :])
              + bgu_ref[0])
        gate = jnp.minimum(gu[:, :D_FF], SWIGLU_LIMIT)
        up = jnp.clip(gu[:, D_FF:], -SWIGLU_LIMIT, SWIGLU_LIMIT)
        act = (up + 1.0) * (gate * _sigmoid(SWIGLU_ALPHA * gate))
        y = _dot(act.astype(BF16), wdn_bf[...]) + bdn_ref[0]
        ys_ref[...] = _pack_bf16_pairs(y)

    @pl.when(i >= nu_ref[0])
    def _():
        ys_ref[...] = jnp.zeros_like(ys_ref)


def _experts(tile_expert, n_used, tile_valid, xs, w_gu, b_gu, w_dn, b_dn, tg):
    P = xs.shape[0]
    half = D_MODEL // 2
    grid_spec = pltpu.PrefetchScalarGridSpec(
        num_scalar_prefetch=3,
        grid=(P // tg,),
        in_specs=[
            pl.BlockSpec((tg, half), lambda i, te, nu, nv: (jnp.minimum(i, jnp.maximum(nu[0] - 1, 0)), 0)),
            pl.BlockSpec((1, D_MODEL, 2 * D_FF), lambda i, te, nu, nv: (te[i], 0, 0)),
            pl.BlockSpec((1, 1, 2 * D_FF), lambda i, te, nu, nv: (te[i], 0, 0)),
            pl.BlockSpec((1, D_FF, D_MODEL), lambda i, te, nu, nv: (te[i], 0, 0)),
            pl.BlockSpec((1, 1, D_MODEL), lambda i, te, nu, nv: (te[i], 0, 0)),
        ],
        out_specs=pl.BlockSpec((tg, half), lambda i, te, nu, nv: (i, 0)),
        scratch_shapes=[pltpu.VMEM((D_MODEL, 2 * D_FF), BF16), pltpu.VMEM((D_FF, D_MODEL), BF16)],
    )
    return pl.pallas_call(
        _experts_kernel,
        grid_spec=grid_spec,
        out_shape=jax.ShapeDtypeStruct((P, half), U32),
        compiler_params=_cparams(("arbitrary",)),
        name="experts",
    )(tile_expert, n_used, tile_valid, xs, w_gu, b_gu, w_dn, b_dn)


def _combine_kernel(seg_hbm, ys_hbm, ls_ref, tw_ref, x_ref, g_ref, out_ref, ybuf_ref, seg_smem, sem_seg, sem_rows, *,
                    tm, final_norm):
    i = pl.program_id(0)
    n = pl.num_programs(0)
    slot = i % 2
    rows = ybuf_ref.shape[1]

    def copy_into(s):
        def make_copy(lo, off, nrows):
            return pltpu.make_async_copy(ys_hbm.at[pl.ds(off, nrows), :], ybuf_ref.at[s, pl.ds(lo, nrows), :],
                                         sem_rows.at[s])
        return make_copy

    def request(step, s):
        cp = pltpu.make_async_copy(seg_hbm.at[step], seg_smem.at[s], sem_seg)
        cp.start()
        cp.wait()
        _segment_starts(lambda k: seg_smem[s, k], copy_into(s))

    @pl.when(i == 0)
    def _():
        ybuf_ref[...] = jnp.zeros_like(ybuf_ref)
        request(0, 0)

    @pl.when(i + 1 < n)
    def _():
        request(i + 1, 1 - slot)

    ls = ls_ref[...].astype(I32)
    tw = tw_ref[...]
    cid = lax.broadcasted_iota(I32, (tm, rows), 1)
    wmat = jnp.zeros((tm, rows), F32)
    for kk in range(TOP_K):
        wmat = wmat + jnp.where(cid == ls[:, kk:kk + 1], tw[:, kk:kk + 1], 0.0)
    wmat = wmat.astype(BF16)
    _segment_waits(seg_smem[slot, 3 * LANES], copy_into(slot))
    lo, hi = _unpack_bf16_pairs(ybuf_ref[slot])
    moe = jnp.concatenate([_dot(wmat, lo.astype(BF16)), _dot(wmat, hi.astype(BF16))], axis=1)
    x3 = x_ref[...] + moe
    out_ref[...] = _rms(x3, g_ref[...]) if final_norm else x3


def _combine(seg2d, ys, ls, tw, x2, g, tm, final_norm):
    T = x2.shape[0]
    rows = _local_rows(tm)
    return pl.pallas_call(
        functools.partial(_combine_kernel, tm=tm, final_norm=final_norm),
        grid=(T // tm,),
        in_specs=[
            pl.BlockSpec(memory_space=pl.ANY),
            pl.BlockSpec(memory_space=pl.ANY),
            pl.BlockSpec((tm, LANES), lambda i: (i, 0)),
            pl.BlockSpec((tm, LANES), lambda i: (i, 0)),
            pl.BlockSpec((tm, D_MODEL), lambda i: (i, 0)),
            pl.BlockSpec((1, D_MODEL), lambda i: (0, 0)),
        ],
        out_specs=pl.BlockSpec((tm, D_MODEL), lambda i: (i, 0)),
        out_shape=jax.ShapeDtypeStruct((T, D_MODEL), F32),
        scratch_shapes=[
            pltpu.VMEM((2, rows, D_MODEL // 2), U32),
            pltpu.SMEM((2, SEG_WORDS), I32),
            pltpu.SemaphoreType.DMA,
            pltpu.SemaphoreType.DMA((2,)),
        ],
        compiler_params=_cparams(("arbitrary",)),
        name="combine",
    )(seg2d, ys, ls, tw, x2, g)


def _tiles(B, S):
    T = B * S
    return dict(
        tm_in=min(2048, T), tn_in=1024,
        ts=min(512, S),
        tq=min(1024, S),
        tm_proj=min(512, S),
        tm_route=min(256, T),
        tg=512,
    )


def _pad_lanes(a, n=LANES, value=0.0):
    return jnp.pad(a, ((0, 0), (0, n - a.shape[1])), constant_values=value)


def kernel(x, mem, norm_mix, w_in, conv_w, b_if, mlstm_gain, diff_lambda, diff_gain, w_branch_m, w_branch_d,
           b_gate, w_out, norm_xattn, norm_mem, wq_x, wkv_x, wo_x, norm_ffn, w_router, b_router, w_gu, b_gu,
           w_dn, b_dn, norm_final):
    B, S, D = x.shape
    n_mem = mem.shape[1]
    T = B * S
    depth = norm_mix.shape[0]
    tl = _tiles(B, S)
    x2d = x.reshape(T, D)
    mem2d = mem.reshape(B * n_mem, D)

    for l in range(depth):
        lam_init = 0.8 - 0.6 * math.exp(-0.3 * l)
        wl = w_in[l]
        if_lo = 2 * M_QK + 2 * M_V
        w_main = jnp.concatenate([wl[:, :if_lo], wl[:, if_lo + 2 * M_HEADS:]], axis=1).astype(BF16)
        w_if = wl[:, if_lo:if_lo + 2 * M_HEADS]
        w_ifp = _pad_lanes(w_if).astype(BF16)
        w_ift = w_if.T.astype(BF16)
        bif = _pad_lanes(b_if[l][None, :])
        bift = jnp.broadcast_to(b_if[l][:, None], (SUBLANES, LANES))

        z, zif, zift = _inproj(x2d, norm_mix[l][None, :], w_main, w_ifp, w_ift, tl["tm_in"], tl["tn_in"])
        hm = _mlstm(z, zif, zift, conv_w[l], bif, bift, mlstm_gain[l].reshape(1, M_V), B, S, tl["ts"])
        hd = _diffattn(z, diff_lambda[l], diff_gain[l][None, :], B, S, tl["tq"], lam_init)
        x1 = _merge(hm, hd, z, x2d, w_branch_m[l].astype(BF16), w_branch_d[l].astype(BF16),
                    w_out[l].astype(BF16), b_gate[l][None, :], tl["tm_proj"])

        kvmem = _memkv(mem2d, norm_mem[l][None, :], wkv_x[l].astype(BF16), n_mem)
        x2 = _xattn(x1, norm_xattn[l][None, :], wq_x[l].astype(BF16), kvmem, wo_x[l].astype(BF16),
                    S, n_mem, tl["tm_proj"])

        wr = _pad_lanes(w_router[l])
        wr_hi = wr.astype(BF16)
        wr_lo = (wr - wr_hi.astype(F32)).astype(BF16)
        br = _pad_lanes(b_router[l][None, :], value=-jnp.inf)
        tm_r = tl["tm_route"]
        tg = tl["tg"]
        hp, ids, tw, cnt = _router(x2, norm_ffn[l][None, :], wr_hi, wr_lo, br, tm_r)

        counts = cnt[0, :N_EXPERTS].astype(I32)
        padded = ((counts + tg - 1) // tg) * tg
        ends = jnp.cumsum(padded)
        starts = ends - padded
        max_rows = T * TOP_K + (T // tm_r) * N_EXPERTS * (SUBLANES - 1)
        n_tiles = -(-max_rows // tg) + N_EXPERTS
        tile_row0 = jnp.arange(n_tiles, dtype=I32) * tg
        tile_expert = jnp.minimum(jnp.sum((tile_row0[:, None] >= ends[None, :]).astype(I32), axis=1), N_EXPERTS - 1)
        n_used = (ends[-1] // tg).astype(I32).reshape(1)
        last_used = tile_expert[jnp.maximum(n_used[0] - 1, 0)]
        tile_expert = jnp.where(tile_row0 < ends[-1], tile_expert, last_used)
        tile_valid = jnp.clip((starts + counts)[tile_expert] - tile_row0, 0, tg).astype(I32)

        ls, lst, seg = _slots(ids, _pad_lanes(starts.astype(F32)[None, :]), tm_r)
        seg2d = seg.reshape(T // tm_r, SEG_WORDS)

        xs = _dispatch(hp, lst, seg2d, n_tiles * tg, tm_r)
        ys = _experts(tile_expert, n_used, tile_valid, xs, w_gu[l], b_gu[l][:, None, :], w_dn[l], b_dn[l][:, None, :],
                      tg)
        x2d = _combine(seg2d, ys, ls, tw, x2, norm_final[None, :], tm_r, final_norm=(l == depth - 1))
    return x2d.reshape(B, S, D)
```

```python
import functools
import math

import jax
import jax.numpy as jnp
from jax import lax
from jax.experimental import pallas as pl
from jax.experimental.pallas import tpu as pltpu

F32 = jnp.float32
BF16 = jnp.bfloat16
U32 = jnp.uint32
I32 = jnp.int32

EPS = 1e-6
CHUNK = 64
D_MODEL = 1024
M_HEADS = 4
M_DK = 128
M_DV = 256
M_QK = M_HEADS * M_DK
M_V = M_HEADS * M_DV
CONV_W = 4
D_HEADS = 8
D_DH = 64
D_HP = 2
D_QK = D_HEADS * 2 * D_DH
D_V = D_HEADS * 2 * D_DH
X_HEADS = 4
X_DH = D_MODEL // X_HEADS
N_EXPERTS = 32
TOP_K = 4
D_FF = D_MODEL
SWIGLU_LIMIT = 7.0
SWIGLU_ALPHA = 1.702

LANES = 128
SUBLANES = 8
N_MAIN = 2 * M_QK + 2 * M_V + 2 * D_QK + D_V + 2 * D_MODEL
OFF_QM, OFF_KM, OFF_VM, OFF_OM = 0, M_QK, 2 * M_QK, 2 * M_QK + M_V
OFF_QD = OFF_OM + M_V
OFF_KD = OFF_QD + D_QK
OFF_VD = OFF_KD + D_QK
OFF_G = OFF_VD + D_V

VMEM_LIMIT = 56 * 1024 * 1024


def _cparams(sem, vmem=VMEM_LIMIT):
    return pltpu.CompilerParams(dimension_semantics=sem, vmem_limit_bytes=vmem)


def _rms(x, g):
    return x * lax.rsqrt(jnp.mean(x * x, axis=-1, keepdims=True) + EPS) * g


def _split_bf16(x):
    hi = x.astype(BF16)
    lo = (x - hi.astype(F32)).astype(BF16)
    return hi, lo


def _dot(a, b):
    return jnp.dot(a, b, preferred_element_type=F32)


def _dot_nt(a, b):
    return lax.dot_general(a, b, (((1,), (1,)), ((), ())), preferred_element_type=F32)


def _sigmoid(x):
    return 1.0 / (1.0 + jnp.exp(-x))


def _log_sigmoid(x):
    return jnp.minimum(x, 0.0) - jnp.log(1.0 + jnp.exp(-jnp.abs(x)))


def _pack_bf16_pairs(x):
    w = x.shape[1] // 2
    u = lax.bitcast_convert_type(x, U32)
    r = (u + jnp.uint32(0x7FFF) + ((u >> 16) & jnp.uint32(1))) >> 16
    return r[:, :w] | (r[:, w:] << 16)


def _ceil_rows(x):
    return jnp.floor((x + (SUBLANES - 1)) * (1.0 / SUBLANES)) * SUBLANES


def _unpack_bf16_pairs(p):
    lo = lax.bitcast_convert_type(p << 16, F32)
    hi = lax.bitcast_convert_type(p & jnp.uint32(0xFFFF0000), F32)
    return lo, hi


def _conv_silu(x, prev8, w):
    row8 = lax.broadcasted_iota(I32, prev8.shape, 0)
    acc = w[CONV_W - 1:CONV_W, :] * x
    for s in range(1, CONV_W):
        xs = pltpu.roll(x, s, 0)
        top = jnp.where(row8 < s, pltpu.roll(prev8, s, 0), xs[0:SUBLANES])
        xs = jnp.concatenate([top, xs[SUBLANES:]], axis=0)
        acc = acc + w[CONV_W - 1 - s:CONV_W - s, :] * xs
    return acc * _sigmoid(acc)


def _inproj_kernel(x_ref, g_ref, w_ref, wif_ref, wift_ref, cw_ref, z_ref, zif_ref, zift_ref, hn_ref, carry_ref, *,
                   tiles_per_seq):
    i = pl.program_id(0)
    j = pl.program_id(1)
    tm = x_ref.shape[0]

    @pl.when(j == 0)
    def _():
        hn = _rms(x_ref[...], g_ref[...]).astype(BF16)
        hn_ref[...] = hn
        zif_ref[...] = _dot(hn, wif_ref[...])
        zift_ref[...] = _dot_nt(wift_ref[...], hn)

    zt = _dot(hn_ref[...], w_ref[...])

    @pl.when(j == 0)
    def _():
        @pl.when(i % tiles_per_seq == 0)
        def _():
            carry_ref[...] = jnp.zeros_like(carry_ref)

        qk = zt[:, :2 * M_QK]
        y = _conv_silu(qk, carry_ref[...], cw_ref[...])
        col = lax.broadcasted_iota(I32, (1, 2 * M_QK), 1)
        y = y * jnp.where(col < M_QK, M_DK ** -0.5, 1.0)
        z_ref[:, :2 * M_QK] = y.astype(BF16)
        z_ref[:, 2 * M_QK:] = zt[:, 2 * M_QK:].astype(BF16)
        carry_ref[...] = qk[tm - SUBLANES:tm]

    @pl.when(j != 0)
    def _():
        z_ref[...] = zt.astype(BF16)


def _inproj(x2d, g, w_main, w_if, w_ift, conv_w, tm, tn, tiles_per_seq):
    T = x2d.shape[0]
    assert tn >= 2 * M_QK
    return pl.pallas_call(
        functools.partial(_inproj_kernel, tiles_per_seq=tiles_per_seq),
        grid=(T // tm, N_MAIN // tn),
        in_specs=[
            pl.BlockSpec((tm, D_MODEL), lambda i, j: (i, 0)),
            pl.BlockSpec((1, D_MODEL), lambda i, j: (0, 0)),
            pl.BlockSpec((D_MODEL, tn), lambda i, j: (0, j)),
            pl.BlockSpec((D_MODEL, LANES), lambda i, j: (0, 0)),
            pl.BlockSpec((SUBLANES, D_MODEL), lambda i, j: (0, 0)),
            pl.BlockSpec((CONV_W, 2 * M_QK), lambda i, j: (0, 0)),
        ],
        out_specs=[
            pl.BlockSpec((tm, tn), lambda i, j: (i, j)),
            pl.BlockSpec((tm, LANES), lambda i, j: (i, 0)),
            pl.BlockSpec((SUBLANES, tm), lambda i, j: (0, i)),
        ],
        out_shape=[
            jax.ShapeDtypeStruct((T, N_MAIN), BF16),
            jax.ShapeDtypeStruct((T, LANES), F32),
            jax.ShapeDtypeStruct((SUBLANES, T), F32),
        ],
        scratch_shapes=[pltpu.VMEM((tm, D_MODEL), BF16), pltpu.VMEM((SUBLANES, 2 * M_QK), F32)],
        compiler_params=_cparams(("arbitrary", "arbitrary")),
        name="inproj",
    )(x2d, g, w_main, w_if, w_ift, conv_w)


def _mlstm_kernel(q_ref, k_ref, v_ref, om_ref, zif_ref, zift_ref, bif_ref, bift_ref, mg_ref,
                  out_ref, c_ref, n_ref, m_ref, bd_ref, bdt_ref, brep_ref, grow_ref, brow_ref, *, ts):
    nchunk = ts // CHUNK
    L = CHUNK

    @pl.when(pl.program_id(1) == 0)
    def _():
        c_ref[...] = jnp.zeros_like(c_ref)
        n_ref[...] = jnp.zeros_like(n_ref)
        m_ref[...] = jnp.zeros_like(m_ref)
        rt = lax.broadcasted_iota(I32, (ts, ts), 0)
        ct = lax.broadcasted_iota(I32, (ts, ts), 1)
        same = (rt // L) == (ct // L)
        bd_ref[...] = jnp.where(same, jnp.where(ct <= rt, 1.0, 0.0), 0.0).astype(BF16)
        bdt_ref[...] = jnp.where(same, jnp.where(rt <= ct, 1.0, 0.0), 0.0).astype(BF16)

    ti = lax.broadcasted_iota(I32, (L, L), 0)
    si = lax.broadcasted_iota(I32, (L, L), 1)
    causal = si <= ti
    lane_row = lax.broadcasted_iota(I32, (LANES, LANES), 0)
    ones_l = jnp.ones((L, LANES), BF16)

    lf_col = _log_sigmoid(zif_ref[...] + bif_ref[...])
    ch, cl = _split_bf16(lf_col)
    b_col_all = _dot(bd_ref[...], ch) + _dot(bd_ref[...], cl)
    bh, bl = _split_bf16(b_col_all)
    for h in range(M_HEADS):
        sel_f = jnp.where(lane_row == M_HEADS + h, 1.0, 0.0).astype(BF16)
        brep_ref[h] = _dot(bh, sel_f) + _dot(bl, sel_f)
    g_row_all = zift_ref[...] + bift_ref[:, 0:1]
    rh, rl = _split_bf16(_log_sigmoid(g_row_all))
    b_row_tile = _dot(rh, bdt_ref[...]) + _dot(rl, bdt_ref[...])
    for cc in range(nchunk):
        grow_ref[cc] = g_row_all[:, cc * L:(cc + 1) * L]
        brow_ref[cc] = b_row_tile[:, cc * L:(cc + 1) * L]

    def chunk_body(c, carry):
        r0 = pl.multiple_of(c * L, L)
        g_row = grow_ref[c]
        b_row_all = brow_ref[c]
        early = []
        for h in range(M_HEADS):
            b_rep = brep_ref[h, pl.ds(r0, L), :]
            i_row = g_row[h:h + 1, :]
            b_row = b_row_all[M_HEADS + h:M_HEADS + h + 1, :]
            b_last = b_rep[L - 1:L, :]
            q = q_ref[pl.ds(r0, L), h * M_DK:(h + 1) * M_DK]
            k = k_ref[pl.ds(r0, L), h * M_DK:(h + 1) * M_DK]
            vext = jnp.concatenate([v_ref[pl.ds(r0, L), h * M_DV:(h + 1) * M_DV], ones_l], axis=1)
            dm = jnp.where(causal, b_rep[:, :L] - b_row + i_row, -jnp.inf)
            m_loc = jnp.max(dm, axis=-1, keepdims=True)
            qk = _dot_nt(q, k)
            gk_row = b_last[:, :L] - b_row + i_row
            g_max = jnp.max(gk_row, axis=-1, keepdims=True)
            kwt = (k.astype(F32).T * jnp.exp(gk_row - g_max)).astype(BF16)
            kv = _dot(kwt, vext)
            c_old = c_ref[h]
            n_old = n_ref[h]
            qcn = _dot(q, jnp.concatenate([c_old, n_old], axis=1).astype(BF16))
            early.append((b_rep, b_last, vext, dm, m_loc, qk, g_max, kv, c_old, n_old, qcn))
        pvs = []
        for h in range(M_HEADS):
            b_rep, b_last, vext, dm, m_loc, qk, g_max, kv, c_old, n_old, qcn = early[h]
            s_loc = qk * jnp.exp(dm - m_loc)
            pvs.append(_dot(s_loc.astype(BF16), vext))
        for h in range(M_HEADS):
            b_rep, b_last, vext, dm, m_loc, qk, g_max, kv, c_old, n_old, qcn = early[h]
            pv = pvs[h]
            m_prev = m_ref[h:h + 1, :]
            inter = b_rep + m_prev
            m_t = jnp.maximum(inter, m_loc)
            w_inter = jnp.exp(inter - m_t)
            r_loc = jnp.exp(m_loc - m_t)
            den = r_loc * pv[:, M_DV:] + w_inter * qcn[:, M_DV:]
            inv = 1.0 / jnp.maximum(jnp.abs(den), jnp.exp(-m_t))
            hv = (jnp.concatenate([r_loc * inv] * 2, axis=1) * pv[:, :M_DV]
                  + jnp.concatenate([w_inter * inv] * 2, axis=1) * qcn[:, :M_DV])
            m_new = jnp.maximum(b_last + m_prev, g_max)
            decay = jnp.exp(b_last + m_prev - m_new)
            sc_loc = jnp.exp(g_max - m_new)
            c_ref[h] = (jnp.concatenate([decay] * 2, axis=1) * c_old
                        + jnp.concatenate([sc_loc] * 2, axis=1) * kv[:, :M_DV])
            n_ref[h] = decay * n_old + sc_loc * kv[:, M_DV:]
            m_ref[h:h + 1, :] = m_new
            hn = _rms(hv, mg_ref[:, h * M_DV:(h + 1) * M_DV])
            og = _sigmoid(om_ref[pl.ds(r0, L), h * M_DV:(h + 1) * M_DV].astype(F32))
            out_ref[pl.ds(r0, L), h * M_DV:(h + 1) * M_DV] = (og * hn).astype(BF16)
        return carry

    lax.fori_loop(0, nchunk, chunk_body, 0, unroll=4)


def _mlstm(z, zif, zift, bif, bift, m_gain, B, S, ts):
    T = B * S
    nt = S // ts
    nck = ts // CHUNK
    row = lambda b, t: b * nt + t
    return pl.pallas_call(
        functools.partial(_mlstm_kernel, ts=ts),
        grid=(B, nt),
        in_specs=[
            pl.BlockSpec((ts, M_QK), lambda b, t: (row(b, t), OFF_QM // M_QK)),
            pl.BlockSpec((ts, M_QK), lambda b, t: (row(b, t), OFF_KM // M_QK)),
            pl.BlockSpec((ts, M_V), lambda b, t: (row(b, t), OFF_VM // M_V)),
            pl.BlockSpec((ts, M_V), lambda b, t: (row(b, t), OFF_OM // M_V)),
            pl.BlockSpec((ts, LANES), lambda b, t: (row(b, t), 0)),
            pl.BlockSpec((SUBLANES, ts), lambda b, t: (0, row(b, t))),
            pl.BlockSpec((1, LANES), lambda b, t: (0, 0)),
            pl.BlockSpec((SUBLANES, LANES), lambda b, t: (0, 0)),
            pl.BlockSpec((1, M_V), lambda b, t: (0, 0)),
        ],
        out_specs=pl.BlockSpec((ts, M_V), lambda b, t: (row(b, t), 0)),
        out_shape=jax.ShapeDtypeStruct((T, M_V), BF16),
        scratch_shapes=[
            pltpu.VMEM((M_HEADS, M_DK, M_DV), F32),
            pltpu.VMEM((M_HEADS, M_DK, LANES), F32),
            pltpu.VMEM((SUBLANES, LANES), F32),
            pltpu.VMEM((ts, ts), BF16),
            pltpu.VMEM((ts, ts), BF16),
            pltpu.VMEM((M_HEADS, ts, LANES), F32),
            pltpu.VMEM((nck, SUBLANES, CHUNK), F32),
            pltpu.VMEM((nck, SUBLANES, CHUNK), F32),
        ],
        compiler_params=_cparams(("arbitrary", "arbitrary")),
        name="mlstm",
    )(z, z, z, z, zif, zift, bif, bift, m_gain)


def _diffattn_kernel(q_ref, k_ref, v_ref, lam_ref, gain_ref, out_ref, m_ref, a_ref, *, tq, lam_init):
    qi = pl.program_id(2)
    w = 2 * D_DH
    lane = lax.broadcasted_iota(I32, (1, w), 1)
    scale = jnp.asarray(D_DH ** -0.5, BF16)
    qs = []
    for hh in range(D_HP):
        q = q_ref[:, hh * w:(hh + 1) * w]
        qs.append((jnp.where(lane < D_DH, q, jnp.zeros_like(q)) * scale,
                   jnp.where(lane >= D_DH, q, jnp.zeros_like(q)) * scale))
    ones = jnp.ones((tq, w), BF16)

    def block(k0, nk, rows=slice(None), mask=None, first=False):
        n_stream = 2 * D_HP
        scores = []
        for hh in range(D_HP):
            k = k_ref[pl.ds(k0, nk), hh * w:(hh + 1) * w]
            for comp in range(2):
                s = _dot_nt(qs[hh][comp][rows], k)
                scores.append(s if mask is None else jnp.where(mask, s, -jnp.inf))
        m_news, alphas, probs = [], [], []
        for i in range(n_stream):
            s = scores[i]
            s_max = jnp.max(s, axis=-1, keepdims=True)
            if first:
                m_new = jnp.broadcast_to(s_max, (s.shape[0], w))
                alphas.append(None)
            else:
                m_old = m_ref[i, rows, :]
                m_new = jnp.maximum(m_old, s_max)
                alphas.append(jnp.exp(m_old - m_new))
            m_news.append(m_new)
            probs.append(jnp.exp(s - jnp.concatenate([m_new] * (s.shape[1] // w), axis=1)).astype(BF16))
        pvs = []
        for hh in range(D_HP):
            vext = jnp.concatenate([v_ref[pl.ds(k0, nk), hh * w:(hh + 1) * w], ones[:nk]], axis=1)
            for comp in range(2):
                pvs.append(_dot(probs[2 * hh + comp], vext))
        for i in range(n_stream):
            if first:
                a_ref[i, rows, :] = pvs[i]
            else:
                a_ref[i, rows, :] = jnp.concatenate([alphas[i]] * 2, axis=1) * a_ref[i, rows, :] + pvs[i]
            m_ref[i, rows, :] = m_news[i]

    hq = tq // 2
    d0 = pl.multiple_of(qi * tq, tq)

    def chunk_mask(q0, nk):
        rq = (lax.broadcasted_iota(I32, (hq, nk), 0) + q0) // CHUNK
        ck = lax.broadcasted_iota(I32, (hq, nk), 1) // CHUNK
        return ck <= rq

    block(d0, hq, rows=slice(0, hq), mask=chunk_mask(0, hq), first=True)
    block(d0, tq, rows=slice(hq, tq), mask=chunk_mask(hq, tq), first=True)

    def body(jj, carry):
        block(pl.multiple_of(jj * tq, tq), tq)
        return carry

    lax.fori_loop(0, qi, body, 0)

    lp = lam_ref[...]
    lam = (jnp.exp(jnp.sum(lp[0:1, :] * lp[1:2, :], axis=-1, keepdims=True))
           - jnp.exp(jnp.sum(lp[2:3, :] * lp[3:4, :], axis=-1, keepdims=True)) + lam_init)
    for hh in range(D_HP):
        a1 = a_ref[2 * hh]
        a2 = a_ref[2 * hh + 1]
        o = a1[:, :w] / a1[:, w:] - lam * (a2[:, :w] / a2[:, w:])
        out_ref[:, hh * w:(hh + 1) * w] = (_rms(o, gain_ref[...]) * (1.0 - lam_init)).astype(BF16)


def _diffattn(z, lam_p, d_gain, B, S, tq, lam_init):
    T = B * S
    nq = S // tq
    w = 2 * D_DH
    wp = D_HP * w
    return pl.pallas_call(
        functools.partial(_diffattn_kernel, tq=tq, lam_init=lam_init),
        grid=(B, D_HEADS // D_HP, nq),
        in_specs=[
            pl.BlockSpec((tq, wp), lambda b, h, i: (b * nq + i, OFF_QD // wp + h)),
            pl.BlockSpec((S, wp), lambda b, h, i: (b, OFF_KD // wp + h)),
            pl.BlockSpec((S, wp), lambda b, h, i: (b, OFF_VD // wp + h)),
            pl.BlockSpec((4, D_DH), lambda b, h, i: (0, 0)),
            pl.BlockSpec((1, w), lambda b, h, i: (0, 0)),
        ],
        out_specs=pl.BlockSpec((tq, wp), lambda b, h, i: (b * nq + i, h)),
        out_shape=jax.ShapeDtypeStruct((T, D_V), BF16),
        scratch_shapes=[
            pltpu.VMEM((2 * D_HP, tq, w), F32), pltpu.VMEM((2 * D_HP, tq, 2 * w), F32),
        ],
        compiler_params=_cparams(("arbitrary", "arbitrary", "arbitrary")),
        name="diffattn",
    )(z, z, z, lam_p, d_gain)


def _merge_kernel(hm_ref, hd_ref, gz_ref, x_ref, wbm_ref, wbd_ref, wout_ref, bg_ref, out_ref):
    bm = _dot(hm_ref[...], wbm_ref[...])
    bd = _dot(hd_ref[...], wbd_ref[...])
    g = _sigmoid(gz_ref[...].astype(F32) + bg_ref[...])
    merged = g[:, :D_MODEL] * bm + g[:, D_MODEL:] * bd
    out_ref[...] = x_ref[...] + _dot(merged.astype(BF16), wout_ref[...])


def _merge(hm, hd, z, x2d, w_bm, w_bd, w_out, b_gate, tm):
    T = x2d.shape[0]
    full = lambda i: (0, 0)
    return pl.pallas_call(
        _merge_kernel,
        grid=(T // tm,),
        in_specs=[
            pl.BlockSpec((tm, M_V), lambda i: (i, 0)),
            pl.BlockSpec((tm, D_V), lambda i: (i, 0)),
            pl.BlockSpec((tm, 2 * D_MODEL), lambda i: (i, OFF_G // (2 * D_MODEL))),
            pl.BlockSpec((tm, D_MODEL), lambda i: (i, 0)),
            pl.BlockSpec((M_V, D_MODEL), full),
            pl.BlockSpec((D_V, D_MODEL), full),
            pl.BlockSpec((D_MODEL, D_MODEL), full),
            pl.BlockSpec((1, 2 * D_MODEL), full),
        ],
        out_specs=pl.BlockSpec((tm, D_MODEL), lambda i: (i, 0)),
        out_shape=jax.ShapeDtypeStruct((T, D_MODEL), F32),
        compiler_params=_cparams(("arbitrary",)),
        name="merge",
    )(hm, hd, z, x2d, w_bm, w_bd, w_out, b_gate)


def _memkv_kernel(mem_ref, g_ref, w_ref, out_ref):
    out_ref[...] = _dot(_rms(mem_ref[...], g_ref[...]).astype(BF16), w_ref[...]).astype(BF16)


def _memkv(mem2d, g, wkv, n_mem):
    R = mem2d.shape[0]
    return pl.pallas_call(
        _memkv_kernel,
        grid=(R // n_mem,),
        in_specs=[
            pl.BlockSpec((n_mem, D_MODEL), lambda i: (i, 0)),
            pl.BlockSpec((1, D_MODEL), lambda i: (0, 0)),
            pl.BlockSpec((D_MODEL, 2 * D_MODEL), lambda i: (0, 0)),
        ],
        out_specs=pl.BlockSpec((n_mem, 2 * D_MODEL), lambda i: (i, 0)),
        out_shape=jax.ShapeDtypeStruct((R, 2 * D_MODEL), BF16),
        compiler_params=_cparams(("arbitrary",)),
        name="memkv",
    )(mem2d, g, wkv)


def _xattn_kernel(x_ref, g_ref, wq_ref, kv_ref, wo_ref, out_ref, o_ref):
    x = x_ref[...]
    h = _rms(x, g_ref[...]).astype(BF16)
    q = (_dot(h, wq_ref[...]) * (X_DH ** -0.5)).astype(BF16)
    scores = [_dot_nt(q[:, hd * X_DH:(hd + 1) * X_DH], kv_ref[:, hd * X_DH:(hd + 1) * X_DH]) for hd in range(X_HEADS)]
    for hd in range(X_HEADS):
        vh = kv_ref[:, D_MODEL + hd * X_DH:D_MODEL + (hd + 1) * X_DH]
        s = scores[hd]
        p = jnp.exp(s - jnp.max(s, axis=-1, keepdims=True))
        p = p / jnp.sum(p, axis=-1, keepdims=True)
        o_ref[:, hd * X_DH:(hd + 1) * X_DH] = _dot(p.astype(BF16), vh).astype(BF16)
    out_ref[...] = x + _dot(o_ref[...], wo_ref[...])


def _xattn(x1, g, wq, kvmem, wo, S, n_mem, tm):
    T = x1.shape[0]
    per_b = S // tm
    full = lambda i: (0, 0)
    return pl.pallas_call(
        _xattn_kernel,
        grid=(T // tm,),
        in_specs=[
            pl.BlockSpec((tm, D_MODEL), lambda i: (i, 0)),
            pl.BlockSpec((1, D_MODEL), full),
            pl.BlockSpec((D_MODEL, D_MODEL), full),
            pl.BlockSpec((n_mem, 2 * D_MODEL), lambda i: (i // per_b, 0)),
            pl.BlockSpec((D_MODEL, D_MODEL), full),
        ],
        out_specs=pl.BlockSpec((tm, D_MODEL), lambda i: (i, 0)),
        out_shape=jax.ShapeDtypeStruct((T, D_MODEL), F32),
        scratch_shapes=[pltpu.VMEM((tm, D_MODEL), BF16)],
        compiler_params=_cparams(("arbitrary",)),
        name="xattn",
    )(x1, g, wq, kvmem, wo)


def _router_kernel(x_ref, g_ref, wrh_ref, wrl_ref, br_ref, hp_ref, ids_ref, tw_ref, cnt_ref):
    @pl.when(pl.program_id(0) == 0)
    def _():
        cnt_ref[...] = jnp.zeros_like(cnt_ref)

    hn = _rms(x_ref[...], g_ref[...])
    hh, hl = _split_bf16(hn)
    hp_ref[...] = hh
    logits = _dot(hh, wrh_ref[...]) + _dot(hh, wrl_ref[...]) + _dot(hl, wrh_ref[...]) + br_ref[...]
    lane = lax.broadcasted_iota(I32, logits.shape, 1)
    lanef = lane.astype(F32)
    ids = jnp.zeros(logits.shape, F32)
    tw = jnp.zeros(logits.shape, F32)
    onehot = jnp.zeros(logits.shape, F32)
    v0 = None
    den = None
    for kk in range(TOP_K):
        mx = jnp.max(logits, axis=-1, keepdims=True)
        idx = jnp.min(jnp.where(logits == mx, lanef, float(LANES)), axis=-1, keepdims=True)
        sel = lanef == idx
        if kk == 0:
            v0 = mx
        e = jnp.exp(mx - v0)
        den = e if den is None else den + e
        ids = jnp.where(lane == kk, idx, ids)
        tw = jnp.where(lane == kk, e, tw)
        onehot = jnp.where(sel, 1.0, onehot)
        logits = jnp.where(sel, -jnp.inf, logits)
    ids_ref[...] = ids.astype(I32)
    tw_ref[...] = tw / den
    cnt_ref[...] += _ceil_rows(jnp.sum(onehot, axis=0, keepdims=True))


def _router(x2, g, wr_hi, wr_lo, b_r, tm):
    T = x2.shape[0]
    full = lambda i: (0, 0)
    return pl.pallas_call(
        _router_kernel,
        grid=(T // tm,),
        in_specs=[
            pl.BlockSpec((tm, D_MODEL), lambda i: (i, 0)),
            pl.BlockSpec((1, D_MODEL), full),
            pl.BlockSpec((D_MODEL, LANES), full),
            pl.BlockSpec((D_MODEL, LANES), full),
            pl.BlockSpec((1, LANES), full),
        ],
        out_specs=[
            pl.BlockSpec((tm, D_MODEL), lambda i: (i, 0)),
            pl.BlockSpec((tm, LANES), lambda i: (i, 0)),
            pl.BlockSpec((tm, LANES), lambda i: (i, 0)),
            pl.BlockSpec((1, LANES), full),
        ],
        out_shape=[
            jax.ShapeDtypeStruct((T, D_MODEL), BF16),
            jax.ShapeDtypeStruct((T, LANES), I32),
            jax.ShapeDtypeStruct((T, LANES), F32),
            jax.ShapeDtypeStruct((1, LANES), F32),
        ],
        compiler_params=_cparams(("arbitrary",)),
        name="router",
    )(x2, g, wr_hi, wr_lo, b_r)


SEG_WORDS = SUBLANES * LANES


def _slots_kernel(ids_ref, start_ref, ls_ref, lst_ref, seg_ref, run_ref):
    @pl.when(pl.program_id(0) == 0)
    def _():
        run_ref[...] = jnp.zeros_like(run_ref)

    ids = ids_ref[...]
    tm = ids.shape[0]
    lane = lax.broadcasted_iota(I32, ids.shape, 1)
    sels = [lane == ids[:, kk:kk + 1] for kk in range(TOP_K)]
    onehot = jnp.zeros(ids.shape, F32)
    for s in sels:
        onehot = jnp.where(s, 1.0, onehot)
    c8 = _ceil_rows(jnp.sum(onehot, axis=0, keepdims=True))
    er = lax.broadcasted_iota(I32, (LANES, LANES), 0)
    ec = lax.broadcasted_iota(I32, (LANES, LANES), 1)
    before = jnp.where(er < ec, 1.0, 0.0).astype(BF16)
    pieces = jnp.broadcast_to(c8 * (1.0 / SUBLANES), (SUBLANES, LANES)).astype(BF16)
    lo = _dot(pieces, before)[0:1, :] * SUBLANES
    r = lax.broadcasted_iota(I32, (tm, tm), 0)
    c = lax.broadcasted_iota(I32, (tm, tm), 1)
    strict = jnp.where(c < r, 1.0, 0.0).astype(BF16)
    slot = _dot(strict, onehot.astype(BF16)) + lo
    ls = jnp.zeros(ids.shape, F32)
    for kk, s in enumerate(sels):
        pk = jnp.sum(jnp.where(s, slot, 0.0), axis=-1, keepdims=True)
        ls = jnp.where(lane == kk, pk, ls)
    ls_ref[...] = ls
    hi = jnp.floor(ls * (1.0 / 32.0))
    rem = ls - 32.0 * hi
    pick = jnp.where(lax.broadcasted_iota(I32, (SUBLANES, LANES), 0) == lax.broadcasted_iota(I32, (SUBLANES, LANES), 1),
                     1.0, 0.0).astype(BF16)
    lst_ref[...] = 32.0 * _dot_nt(pick, hi.astype(BF16)) + _dot_nt(pick, rem.astype(BF16))
    row = lax.broadcasted_iota(I32, (SUBLANES, LANES), 0)
    off = start_ref[...] + run_ref[...]
    total = jnp.sum(c8, axis=-1, keepdims=True)
    seg = jnp.where(row == 0, c8, jnp.where(row == 1, lo, jnp.where(row == 2, off, jnp.where(row == 3, total, 0.0))))
    seg_ref[...] = seg.astype(I32)
    run_ref[...] += c8


def _slots(ids, starts, tm):
    T = ids.shape[0]
    nt = T // tm
    return pl.pallas_call(
        _slots_kernel,
        grid=(nt,),
        in_specs=[
            pl.BlockSpec((tm, LANES), lambda i: (i, 0)),
            pl.BlockSpec((1, LANES), lambda i: (0, 0)),
        ],
        out_specs=[
            pl.BlockSpec((tm, LANES), lambda i: (i, 0)),
            pl.BlockSpec((SUBLANES, tm), lambda i: (0, i)),
            pl.BlockSpec((SUBLANES, LANES), lambda i: (i, 0)),
        ],
        out_shape=[
            jax.ShapeDtypeStruct((T, LANES), F32),
            jax.ShapeDtypeStruct((SUBLANES, T), F32),
            jax.ShapeDtypeStruct((nt * SUBLANES, LANES), I32),
        ],
        scratch_shapes=[pltpu.VMEM((1, LANES), F32)],
        compiler_params=_cparams(("arbitrary",)),
        name="slots",
    )(ids, starts)


def _local_rows(tm):
    need = tm * TOP_K + N_EXPERTS * (SUBLANES - 1)
    return ((need + LANES - 1) // LANES) * LANES


BIG_PIECE = 4 * SUBLANES


def _segment_starts(seg, make_copy):
    def expert(e, carry):
        cnt = seg(e)
        lo = seg(LANES + e)
        off = seg(2 * LANES + e)
        n_big = lax.shift_right_logical(cnt, 5)
        n_small = lax.shift_right_logical(cnt & (BIG_PIECE - 1), 3)

        def big(j, carry2):
            d = j * BIG_PIECE
            make_copy(pl.multiple_of(lo + d, SUBLANES), pl.multiple_of(off + d, SUBLANES), BIG_PIECE).start()
            return carry2

        def small(j, carry2):
            d = n_big * BIG_PIECE + j * SUBLANES
            make_copy(pl.multiple_of(lo + d, SUBLANES), pl.multiple_of(off + d, SUBLANES), SUBLANES).start()
            return carry2

        lax.fori_loop(0, n_big, big, 0)
        lax.fori_loop(0, n_small, small, 0)
        return carry

    for e in range(N_EXPERTS):
        expert(e, 0)


def _segment_waits(total_rows, make_copy):
    def big(j, carry):
        make_copy(0, 0, BIG_PIECE).wait()
        return carry

    def small(j, carry):
        make_copy(0, 0, SUBLANES).wait()
        return carry

    lax.fori_loop(0, lax.shift_right_logical(total_rows, 5), big, 0)
    lax.fori_loop(0, lax.shift_right_logical(total_rows & (BIG_PIECE - 1), 3), small, 0)


def _dispatch_kernel(h_ref, lst_ref, seg_hbm, xs_ref, sbuf_ref, seg_smem, prev_smem, sem_seg, sem_rows, *, tm):
    i = pl.program_id(0)
    n = pl.num_programs(0)
    slot = i % 2
    rows = sbuf_ref.shape[1]
    cp = pltpu.make_async_copy(seg_hbm.at[i], seg_smem, sem_seg)
    cp.start()
    lst = lst_ref[...].astype(I32)
    rid = lax.broadcasted_iota(I32, (rows, tm), 0)
    perm = jnp.zeros((rows, tm), F32)
    for kk in range(TOP_K):
        perm = perm + jnp.where(rid == lst[kk:kk + 1, :], 1.0, 0.0)
    srt = _dot(perm.astype(BF16), h_ref[...])
    sbuf_ref[slot] = _pack_bf16_pairs(srt)
    cp.wait()

    def copy_from(s):
        def make_copy(lo, off, nrows):
            return pltpu.make_async_copy(sbuf_ref.at[s, pl.ds(lo, nrows), :], xs_ref.at[pl.ds(off, nrows), :],
                                         sem_rows.at[s])
        return make_copy

    _segment_starts(lambda k: seg_smem[k], copy_from(slot))

    @pl.when(i > 0)
    def _():
        _segment_waits(prev_smem[0], copy_from(1 - slot))

    prev_smem[0] = seg_smem[3 * LANES]

    @pl.when(i == n - 1)
    def _():
        _segment_waits(prev_smem[0], copy_from(slot))


def _dispatch(h, lst, seg2d, n_rows, tm):
    T = h.shape[0]
    rows = _local_rows(tm)
    return pl.pallas_call(
        functools.partial(_dispatch_kernel, tm=tm),
        grid=(T // tm,),
        in_specs=[
            pl.BlockSpec((tm, D_MODEL), lambda i: (i, 0)),
            pl.BlockSpec((SUBLANES, tm), lambda i: (0, i)),
            pl.BlockSpec(memory_space=pl.ANY),
        ],
        out_specs=pl.BlockSpec(memory_space=pl.ANY),
        out_shape=jax.ShapeDtypeStruct((n_rows, D_MODEL // 2), U32),
        scratch_shapes=[
            pltpu.VMEM((2, rows, D_MODEL // 2), U32),
            pltpu.SMEM((SEG_WORDS,), I32),
            pltpu.SMEM((1,), I32),
            pltpu.SemaphoreType.DMA,
            pltpu.SemaphoreType.DMA((2,)),
        ],
        compiler_params=_cparams(("arbitrary",)),
        name="dispatch",
    )(h, lst, seg2d)


def _experts_kernel(te_ref, nu_ref, nv_ref, xs_ref, wgu_ref, bgu_ref, wdn_ref, bdn_ref, ys_ref, wgu_bf, wdn_bf):
    i = pl.program_id(0)
    half = D_MODEL // 2

    @pl.when((i == 0) | (te_ref[i] != te_ref[jnp.maximum(i - 1, 0)]))
    def _():
        wgu_bf[...] = wgu_ref[0].astype(BF16)
        wdn_bf[...] = wdn_ref[0].astype(BF16)

    @pl.when(i < nu_ref[0])
    def _():
        live = lax.broadcasted_iota(I32, xs_ref.shape, 0) < nv_ref[i]
        lo, hi = _unpack_bf16_pairs(jnp.where(live, xs_ref[...], jnp.uint32(0)))
        gu = (_dot(lo.astype(BF16), wgu_bf[:half, :]) + _dot(hi.astype(BF16), wgu_bf[half:, :])
              + bgu_ref[0])
        gate = jnp.minimum(gu[:, :D_FF], SWIGLU_LIMIT)
        up = jnp.clip(gu[:, D_FF:], -SWIGLU_LIMIT, SWIGLU_LIMIT)
        act = (up + 1.0) * (gate * _sigmoid(SWIGLU_ALPHA * gate))
        y = _dot(act.astype(BF16), wdn_bf[...]) + bdn_ref[0]
        ys_ref[...] = _pack_bf16_pairs(y)

    @pl.when(i >= nu_ref[0])
    def _():
        ys_ref[...] = jnp.zeros_like(ys_ref)


def _experts(tile_expert, n_used, tile_valid, xs, w_gu, b_gu, w_dn, b_dn, tg):
    P = xs.shape[0]
    half = D_MODEL // 2
    grid_spec = pltpu.PrefetchScalarGridSpec(
        num_scalar_prefetch=3,
        grid=(P // tg,),
        in_specs=[
            pl.BlockSpec((tg, half), lambda i, te, nu, nv: (jnp.minimum(i, jnp.maximum(nu[0] - 1, 0)), 0)),
            pl.BlockSpec((1, D_MODEL, 2 * D_FF), lambda i, te, nu, nv: (te[i], 0, 0)),
            pl.BlockSpec((1, 1, 2 * D_FF), lambda i, te, nu, nv: (te[i], 0, 0)),
            pl.BlockSpec((1, D_FF, D_MODEL), lambda i, te, nu, nv: (te[i], 0, 0)),
            pl.BlockSpec((1, 1, D_MODEL), lambda i, te, nu, nv: (te[i], 0, 0)),
        ],
        out_specs=pl.BlockSpec((tg, half), lambda i, te, nu, nv: (i, 0)),
        scratch_shapes=[pltpu.VMEM((D_MODEL, 2 * D_FF), BF16), pltpu.VMEM((D_FF, D_MODEL), BF16)],
    )
    return pl.pallas_call(
        _experts_kernel,
        grid_spec=grid_spec,
        out_shape=jax.ShapeDtypeStruct((P, half), U32),
        compiler_params=_cparams(("arbitrary",)),
        name="experts",
    )(tile_expert, n_used, tile_valid, xs, w_gu, b_gu, w_dn, b_dn)


def _combine_kernel(seg_hbm, ys_hbm, ls_ref, tw_ref, x_ref, g_ref, out_ref, ybuf_ref, seg_smem, sem_seg, sem_rows, *,
                    tm, final_norm):
    i = pl.program_id(0)
    n = pl.num_programs(0)
    slot = i % 2
    rows = ybuf_ref.shape[1]

    def copy_into(s):
        def make_copy(lo, off, nrows):
            return pltpu.make_async_copy(ys_hbm.at[pl.ds(off, nrows), :], ybuf_ref.at[s, pl.ds(lo, nrows), :],
                                         sem_rows.at[s])
        return make_copy

    def request(step, s):
        cp = pltpu.make_async_copy(seg_hbm.at[step], seg_smem.at[s], sem_seg)
        cp.start()
        cp.wait()
        _segment_starts(lambda k: seg_smem[s, k], copy_into(s))

    @pl.when(i == 0)
    def _():
        ybuf_ref[...] = jnp.zeros_like(ybuf_ref)
        request(0, 0)

    @pl.when(i + 1 < n)
    def _():
        request(i + 1, 1 - slot)

    ls = ls_ref[...].astype(I32)
    tw = tw_ref[...]
    cid = lax.broadcasted_iota(I32, (tm, rows), 1)
    wmat = jnp.zeros((tm, rows), F32)
    for kk in range(TOP_K):
        wmat = wmat + jnp.where(cid == ls[:, kk:kk + 1], tw[:, kk:kk + 1], 0.0)
    wmat = wmat.astype(BF16)
    _segment_waits(seg_smem[slot, 3 * LANES], copy_into(slot))
    lo, hi = _unpack_bf16_pairs(ybuf_ref[slot])
    moe = jnp.concatenate([_dot(wmat, lo.astype(BF16)), _dot(wmat, hi.astype(BF16))], axis=1)
    x3 = x_ref[...] + moe
    out_ref[...] = _rms(x3, g_ref[...]) if final_norm else x3


def _combine(seg2d, ys, ls, tw, x2, g, tm, final_norm):
    T = x2.shape[0]
    rows = _local_rows(tm)
    return pl.pallas_call(
        functools.partial(_combine_kernel, tm=tm, final_norm=final_norm),
        grid=(T // tm,),
        in_specs=[
            pl.BlockSpec(memory_space=pl.ANY),
            pl.BlockSpec(memory_space=pl.ANY),
            pl.BlockSpec((tm, LANES), lambda i: (i, 0)),
            pl.BlockSpec((tm, LANES), lambda i: (i, 0)),
            pl.BlockSpec((tm, D_MODEL), lambda i: (i, 0)),
            pl.BlockSpec((1, D_MODEL), lambda i: (0, 0)),
        ],
        out_specs=pl.BlockSpec((tm, D_MODEL), lambda i: (i, 0)),
        out_shape=jax.ShapeDtypeStruct((T, D_MODEL), F32),
        scratch_shapes=[
            pltpu.VMEM((2, rows, D_MODEL // 2), U32),
            pltpu.SMEM((2, SEG_WORDS), I32),
            pltpu.SemaphoreType.DMA,
            pltpu.SemaphoreType.DMA((2,)),
        ],
        compiler_params=_cparams(("arbitrary",)),
        name="combine",
    )(seg2d, ys, ls, tw, x2, g)


def _tiles(B, S):
    T = B * S
    return dict(
        tm_in=min(1024, S), tn_in=2048,
        ts=min(512, S),
        tq=min(1024, S),
        tm_proj=min(512, S),
        tm_route=min(256, T),
        tg=512,
    )


def _pad_lanes(a, n=LANES, value=0.0):
    return jnp.pad(a, ((0, 0), (0, n - a.shape[1])), constant_values=value)


def kernel(x, mem, norm_mix, w_in, conv_w, b_if, mlstm_gain, diff_lambda, diff_gain, w_branch_m, w_branch_d,
           b_gate, w_out, norm_xattn, norm_mem, wq_x, wkv_x, wo_x, norm_ffn, w_router, b_router, w_gu, b_gu,
           w_dn, b_dn, norm_final):
    B, S, D = x.shape
    n_mem = mem.shape[1]
    T = B * S
    depth = norm_mix.shape[0]
    tl = _tiles(B, S)
    x2d = x.reshape(T, D)
    mem2d = mem.reshape(B * n_mem, D)

    for l in range(depth):
        lam_init = 0.8 - 0.6 * math.exp(-0.3 * l)
        wl = w_in[l]
        if_lo = 2 * M_QK + 2 * M_V
        w_main = jnp.concatenate([wl[:, :if_lo], wl[:, if_lo + 2 * M_HEADS:]], axis=1).astype(BF16)
        w_if = wl[:, if_lo:if_lo + 2 * M_HEADS]
        w_ifp = _pad_lanes(w_if).astype(BF16)
        w_ift = w_if.T.astype(BF16)
        bif = _pad_lanes(b_if[l][None, :])
        bift = jnp.broadcast_to(b_if[l][:, None], (SUBLANES, LANES))

        z, zif, zift = _inproj(x2d, norm_mix[l][None, :], w_main, w_ifp, w_ift, conv_w[l], tl["tm_in"], tl["tn_in"],
                               S // tl["tm_in"])
        hm = _mlstm(z, zif, zift, bif, bift, mlstm_gain[l].reshape(1, M_V), B, S, tl["ts"])
        hd = _diffattn(z, diff_lambda[l], diff_gain[l][None, :], B, S, tl["tq"], lam_init)
        x1 = _merge(hm, hd, z, x2d, w_branch_m[l].astype(BF16), w_branch_d[l].astype(BF16),
                    w_out[l].astype(BF16), b_gate[l][None, :], tl["tm_proj"])

        kvmem = _memkv(mem2d, norm_mem[l][None, :], wkv_x[l].astype(BF16), n_mem)
        x2 = _xattn(x1, norm_xattn[l][None, :], wq_x[l].astype(BF16), kvmem, wo_x[l].astype(BF16),
                    S, n_mem, tl["tm_proj"])

        wr = _pad_lanes(w_router[l])
        wr_hi = wr.astype(BF16)
        wr_lo = (wr - wr_hi.astype(F32)).astype(BF16)
        br = _pad_lanes(b_router[l][None, :], value=-jnp.inf)
        tm_r = tl["tm_route"]
        tg = tl["tg"]
        hp, ids, tw, cnt = _router(x2, norm_ffn[l][None, :], wr_hi, wr_lo, br, tm_r)

        counts = cnt[0, :N_EXPERTS].astype(I32)
        padded = ((counts + tg - 1) // tg) * tg
        ends = jnp.cumsum(padded)
        starts = ends - padded
        max_rows = T * TOP_K + (T // tm_r) * N_EXPERTS * (SUBLANES - 1)
        n_tiles = -(-max_rows // tg) + N_EXPERTS
        tile_row0 = jnp.arange(n_tiles, dtype=I32) * tg
        tile_expert = jnp.minimum(jnp.sum((tile_row0[:, None] >= ends[None, :]).astype(I32), axis=1), N_EXPERTS - 1)
        n_used = (ends[-1] // tg).astype(I32).reshape(1)
        last_used = tile_expert[jnp.maximum(n_used[0] - 1, 0)]
        tile_expert = jnp.where(tile_row0 < ends[-1], tile_expert, last_used)
        tile_valid = jnp.clip((starts + counts)[tile_expert] - tile_row0, 0, tg).astype(I32)

        ls, lst, seg = _slots(ids, _pad_lanes(starts.astype(F32)[None, :]), tm_r)
        seg2d = seg.reshape(T // tm_r, SEG_WORDS)

        xs = _dispatch(hp, lst, seg2d, n_tiles * tg, tm_r)
        ys = _experts(tile_expert, n_used, tile_valid, xs, w_gu[l], b_gu[l][:, None, :], w_dn[l], b_dn[l][:, None, :],
                      tg)
        x2d = _combine(seg2d, ys, ls, tw, x2, norm_final[None, :], tm_r, final_norm=(l == depth - 1))
    return x2d.reshape(B, S, D)
```

```python
import functools
import math

import jax
import jax.numpy as jnp
from jax import lax
from jax.experimental import pallas as pl
from jax.experimental.pallas import tpu as pltpu

F32 = jnp.float32
BF16 = jnp.bfloat16
U32 = jnp.uint32
I32 = jnp.int32

EPS = 1e-6
CHUNK = 64
D_MODEL = 1024
M_HEADS = 4
M_DK = 128
M_DV = 256
M_QK = M_HEADS * M_DK
M_V = M_HEADS * M_DV
CONV_W = 4
D_HEADS = 8
D_DH = 64
D_HP = 2
D_QK = D_HEADS * 2 * D_DH
D_V = D_HEADS * 2 * D_DH
X_HEADS = 4
X_DH = D_MODEL // X_HEADS
N_EXPERTS = 32
TOP_K = 4
D_FF = D_MODEL
SWIGLU_LIMIT = 7.0
SWIGLU_ALPHA = 1.702

LANES = 128
SUBLANES = 8
N_MAIN = 2 * M_QK + 2 * M_V + 2 * D_QK + D_V + 2 * D_MODEL
OFF_QM, OFF_KM, OFF_VM, OFF_OM = 0, M_QK, 2 * M_QK, 2 * M_QK + M_V
OFF_QD = OFF_OM + M_V
OFF_KD = OFF_QD + D_QK
OFF_VD = OFF_KD + D_QK
OFF_G = OFF_VD + D_V

VMEM_LIMIT = 56 * 1024 * 1024


def _cparams(sem, vmem=VMEM_LIMIT):
    return pltpu.CompilerParams(dimension_semantics=sem, vmem_limit_bytes=vmem)


def _rms(x, g):
    return x * lax.rsqrt(jnp.mean(x * x, axis=-1, keepdims=True) + EPS) * g


def _split_bf16(x):
    hi = x.astype(BF16)
    lo = (x - hi.astype(F32)).astype(BF16)
    return hi, lo


def _dot(a, b):
    return jnp.dot(a, b, preferred_element_type=F32)


def _dot_nt(a, b):
    return lax.dot_general(a, b, (((1,), (1,)), ((), ())), preferred_element_type=F32)


def _sigmoid(x):
    return 1.0 / (1.0 + jnp.exp(-x))


def _log_sigmoid(x):
    return jnp.minimum(x, 0.0) - jnp.log(1.0 + jnp.exp(-jnp.abs(x)))


def _pack_bf16_pairs(x):
    w = x.shape[1] // 2
    u = lax.bitcast_convert_type(x, U32)
    r = (u + jnp.uint32(0x7FFF) + ((u >> 16) & jnp.uint32(1))) >> 16
    return r[:, :w] | (r[:, w:] << 16)


def _ceil_rows(x):
    return jnp.floor((x + (SUBLANES - 1)) * (1.0 / SUBLANES)) * SUBLANES


def _unpack_bf16_pairs(p):
    lo = lax.bitcast_convert_type(p << 16, F32)
    hi = lax.bitcast_convert_type(p & jnp.uint32(0xFFFF0000), F32)
    return lo, hi


def _conv_silu(x, prev8, w):
    row8 = lax.broadcasted_iota(I32, prev8.shape, 0)
    acc = w[CONV_W - 1:CONV_W, :] * x
    for s in range(1, CONV_W):
        xs = pltpu.roll(x, s, 0)
        top = jnp.where(row8 < s, pltpu.roll(prev8, s, 0), xs[0:SUBLANES])
        xs = jnp.concatenate([top, xs[SUBLANES:]], axis=0)
        acc = acc + w[CONV_W - 1 - s:CONV_W - s, :] * xs
    return acc * _sigmoid(acc)


def _inproj_kernel(x_ref, g_ref, w_ref, wif_ref, wift_ref, cw_ref, z_ref, zif_ref, zift_ref, zqk_ref,
                   hn_ref, qkraw_ref, carry_ref, *, tiles_per_seq):
    i = pl.program_id(0)
    j = pl.program_id(1)
    tm = x_ref.shape[0]

    @pl.when(j == 0)
    def _():
        hn = _rms(x_ref[...], g_ref[...]).astype(BF16)
        hn_ref[...] = hn
        zif_ref[...] = _dot(hn, wif_ref[...])
        zift_ref[...] = _dot_nt(wift_ref[...], hn)
        zt = _dot(hn, w_ref[...])
        z_ref[...] = zt.astype(BF16)
        qkraw_ref[...] = zt[:, :2 * M_QK]

        @pl.when(i % tiles_per_seq == 0)
        def _():
            carry_ref[...] = jnp.zeros_like(carry_ref)

    @pl.when(j == 1)
    def _():
        z_ref[...] = _dot(hn_ref[...], w_ref[...]).astype(BF16)
        qk = qkraw_ref[...]
        y = _conv_silu(qk, carry_ref[...], cw_ref[...])
        col = lax.broadcasted_iota(I32, (1, 2 * M_QK), 1)
        zqk_ref[...] = (y * jnp.where(col < M_QK, M_DK ** -0.5, 1.0)).astype(BF16)
        carry_ref[...] = qk[tm - SUBLANES:tm]

    @pl.when(j > 1)
    def _():
        z_ref[...] = _dot(hn_ref[...], w_ref[...]).astype(BF16)


def _inproj(x2d, g, w_main, w_if, w_ift, conv_w, tm, tn, tiles_per_seq):
    T = x2d.shape[0]
    assert tn >= 2 * M_QK and N_MAIN // tn >= 2
    return pl.pallas_call(
        functools.partial(_inproj_kernel, tiles_per_seq=tiles_per_seq),
        grid=(T // tm, N_MAIN // tn),
        in_specs=[
            pl.BlockSpec((tm, D_MODEL), lambda i, j: (i, 0)),
            pl.BlockSpec((1, D_MODEL), lambda i, j: (0, 0)),
            pl.BlockSpec((D_MODEL, tn), lambda i, j: (0, j)),
            pl.BlockSpec((D_MODEL, LANES), lambda i, j: (0, 0)),
            pl.BlockSpec((SUBLANES, D_MODEL), lambda i, j: (0, 0)),
            pl.BlockSpec((CONV_W, 2 * M_QK), lambda i, j: (0, 0)),
        ],
        out_specs=[
            pl.BlockSpec((tm, tn), lambda i, j: (i, j)),
            pl.BlockSpec((tm, LANES), lambda i, j: (i, 0)),
            pl.BlockSpec((SUBLANES, tm), lambda i, j: (0, i)),
            pl.BlockSpec((tm, 2 * M_QK), lambda i, j: (i, 0)),
        ],
        out_shape=[
            jax.ShapeDtypeStruct((T, N_MAIN), BF16),
            jax.ShapeDtypeStruct((T, LANES), F32),
            jax.ShapeDtypeStruct((SUBLANES, T), F32),
            jax.ShapeDtypeStruct((T, 2 * M_QK), BF16),
        ],
        scratch_shapes=[pltpu.VMEM((tm, D_MODEL), BF16), pltpu.VMEM((tm, 2 * M_QK), F32),
                        pltpu.VMEM((SUBLANES, 2 * M_QK), F32)],
        compiler_params=_cparams(("arbitrary", "arbitrary")),
        name="inproj",
    )(x2d, g, w_main, w_if, w_ift, conv_w)


def _mlstm_kernel(q_ref, k_ref, v_ref, om_ref, zif_ref, zift_ref, bif_ref, bift_ref, mg_ref,
                  out_ref, c_ref, n_ref, m_ref, bd_ref, bdt_ref, brep_ref, grow_ref, brow_ref, *, ts):
    nchunk = ts // CHUNK
    L = CHUNK

    @pl.when(pl.program_id(1) == 0)
    def _():
        c_ref[...] = jnp.zeros_like(c_ref)
        n_ref[...] = jnp.zeros_like(n_ref)
        m_ref[...] = jnp.zeros_like(m_ref)
        rt = lax.broadcasted_iota(I32, (ts, ts), 0)
        ct = lax.broadcasted_iota(I32, (ts, ts), 1)
        same = (rt // L) == (ct // L)
        bd_ref[...] = jnp.where(same, jnp.where(ct <= rt, 1.0, 0.0), 0.0).astype(BF16)
        bdt_ref[...] = jnp.where(same, jnp.where(rt <= ct, 1.0, 0.0), 0.0).astype(BF16)

    ti = lax.broadcasted_iota(I32, (L, L), 0)
    si = lax.broadcasted_iota(I32, (L, L), 1)
    causal = si <= ti
    lane_row = lax.broadcasted_iota(I32, (LANES, LANES), 0)
    ones_l = jnp.ones((L, LANES), BF16)

    lf_col = _log_sigmoid(zif_ref[...] + bif_ref[...])
    ch, cl = _split_bf16(lf_col)
    b_col_all = _dot(bd_ref[...], ch) + _dot(bd_ref[...], cl)
    bh, bl = _split_bf16(b_col_all)
    for h in range(M_HEADS):
        sel_f = jnp.where(lane_row == M_HEADS + h, 1.0, 0.0).astype(BF16)
        brep_ref[h] = _dot(bh, sel_f) + _dot(bl, sel_f)
    g_row_all = zift_ref[...] + bift_ref[:, 0:1]
    rh, rl = _split_bf16(_log_sigmoid(g_row_all))
    b_row_tile = _dot(rh, bdt_ref[...]) + _dot(rl, bdt_ref[...])
    for cc in range(nchunk):
        grow_ref[cc] = g_row_all[:, cc * L:(cc + 1) * L]
        brow_ref[cc] = b_row_tile[:, cc * L:(cc + 1) * L]

    def chunk_body(c, carry):
        r0 = pl.multiple_of(c * L, L)
        g_row = grow_ref[c]
        b_row_all = brow_ref[c]
        early = []
        for h in range(M_HEADS):
            b_rep = brep_ref[h, pl.ds(r0, L), :]
            i_row = g_row[h:h + 1, :]
            b_row = b_row_all[M_HEADS + h:M_HEADS + h + 1, :]
            b_last = b_rep[L - 1:L, :]
            q = q_ref[pl.ds(r0, L), h * M_DK:(h + 1) * M_DK]
            k = k_ref[pl.ds(r0, L), h * M_DK:(h + 1) * M_DK]
            vext = jnp.concatenate([v_ref[pl.ds(r0, L), h * M_DV:(h + 1) * M_DV], ones_l], axis=1)
            dm = jnp.where(causal, b_rep[:, :L] - b_row + i_row, -jnp.inf)
            m_loc = jnp.max(dm, axis=-1, keepdims=True)
            qk = _dot_nt(q, k)
            gk_row = b_last[:, :L] - b_row + i_row
            g_max = jnp.max(gk_row, axis=-1, keepdims=True)
            kwt = (k.astype(F32).T * jnp.exp(gk_row - g_max)).astype(BF16)
            kv = _dot(kwt, vext)
            c_old = c_ref[h]
            n_old = n_ref[h]
            qcn = _dot(q, jnp.concatenate([c_old, n_old], axis=1).astype(BF16))
            early.append((b_rep, b_last, vext, dm, m_loc, qk, g_max, kv, c_old, n_old, qcn))
        pvs = []
        for h in range(M_HEADS):
            b_rep, b_last, vext, dm, m_loc, qk, g_max, kv, c_old, n_old, qcn = early[h]
            s_loc = qk * jnp.exp(dm - m_loc)
            pvs.append(_dot(s_loc.astype(BF16), vext))
        for h in range(M_HEADS):
            b_rep, b_last, vext, dm, m_loc, qk, g_max, kv, c_old, n_old, qcn = early[h]
            pv = pvs[h]
            m_prev = m_ref[h:h + 1, :]
            inter = b_rep + m_prev
            m_t = jnp.maximum(inter, m_loc)
            w_inter = jnp.exp(inter - m_t)
            r_loc = jnp.exp(m_loc - m_t)
            den = r_loc * pv[:, M_DV:] + w_inter * qcn[:, M_DV:]
            inv = 1.0 / jnp.maximum(jnp.abs(den), jnp.exp(-m_t))
            hv = (jnp.concatenate([r_loc * inv] * 2, axis=1) * pv[:, :M_DV]
                  + jnp.concatenate([w_inter * inv] * 2, axis=1) * qcn[:, :M_DV])
            m_new = jnp.maximum(b_last + m_prev, g_max)
            decay = jnp.exp(b_last + m_prev - m_new)
            sc_loc = jnp.exp(g_max - m_new)
            c_ref[h] = (jnp.concatenate([decay] * 2, axis=1) * c_old
                        + jnp.concatenate([sc_loc] * 2, axis=1) * kv[:, :M_DV])
            n_ref[h] = decay * n_old + sc_loc * kv[:, M_DV:]
            m_ref[h:h + 1, :] = m_new
            hn = _rms(hv, mg_ref[:, h * M_DV:(h + 1) * M_DV])
            og = _sigmoid(om_ref[pl.ds(r0, L), h * M_DV:(h + 1) * M_DV].astype(F32))
            out_ref[pl.ds(r0, L), h * M_DV:(h + 1) * M_DV] = (og * hn).astype(BF16)
        return carry

    lax.fori_loop(0, nchunk, chunk_body, 0, unroll=4)


def _mlstm(zqk, z, zif, zift, bif, bift, m_gain, B, S, ts):
    T = B * S
    nt = S // ts
    nck = ts // CHUNK
    row = lambda b, t: b * nt + t
    return pl.pallas_call(
        functools.partial(_mlstm_kernel, ts=ts),
        grid=(B, nt),
        in_specs=[
            pl.BlockSpec((ts, M_QK), lambda b, t: (row(b, t), OFF_QM // M_QK)),
            pl.BlockSpec((ts, M_QK), lambda b, t: (row(b, t), OFF_KM // M_QK)),
            pl.BlockSpec((ts, M_V), lambda b, t: (row(b, t), OFF_VM // M_V)),
            pl.BlockSpec((ts, M_V), lambda b, t: (row(b, t), OFF_OM // M_V)),
            pl.BlockSpec((ts, LANES), lambda b, t: (row(b, t), 0)),
            pl.BlockSpec((SUBLANES, ts), lambda b, t: (0, row(b, t))),
            pl.BlockSpec((1, LANES), lambda b, t: (0, 0)),
            pl.BlockSpec((SUBLANES, LANES), lambda b, t: (0, 0)),
            pl.BlockSpec((1, M_V), lambda b, t: (0, 0)),
        ],
        out_specs=pl.BlockSpec((ts, M_V), lambda b, t: (row(b, t), 0)),
        out_shape=jax.ShapeDtypeStruct((T, M_V), BF16),
        scratch_shapes=[
            pltpu.VMEM((M_HEADS, M_DK, M_DV), F32),
            pltpu.VMEM((M_HEADS, M_DK, LANES), F32),
            pltpu.VMEM((SUBLANES, LANES), F32),
            pltpu.VMEM((ts, ts), BF16),
            pltpu.VMEM((ts, ts), BF16),
            pltpu.VMEM((M_HEADS, ts, LANES), F32),
            pltpu.VMEM((nck, SUBLANES, CHUNK), F32),
            pltpu.VMEM((nck, SUBLANES, CHUNK), F32),
        ],
        compiler_params=_cparams(("arbitrary", "arbitrary")),
        name="mlstm",
    )(zqk, zqk, z, z, zif, zift, bif, bift, m_gain)


def _diffattn_kernel(q_ref, k_ref, v_ref, lam_ref, gain_ref, out_ref, m_ref, a_ref, *, tq, lam_init):
    qi = pl.program_id(2)
    w = 2 * D_DH
    lane = lax.broadcasted_iota(I32, (1, w), 1)
    scale = jnp.asarray(D_DH ** -0.5, BF16)
    qs = []
    for hh in range(D_HP):
        q = q_ref[:, hh * w:(hh + 1) * w]
        qs.append((jnp.where(lane < D_DH, q, jnp.zeros_like(q)) * scale,
                   jnp.where(lane >= D_DH, q, jnp.zeros_like(q)) * scale))
    ones = jnp.ones((tq, w), BF16)

    def block(k0, nk, rows=slice(None), mask=None, first=False):
        n_stream = 2 * D_HP
        scores = []
        for hh in range(D_HP):
            k = k_ref[pl.ds(k0, nk), hh * w:(hh + 1) * w]
            for comp in range(2):
                s = _dot_nt(qs[hh][comp][rows], k)
                scores.append(s if mask is None else jnp.where(mask, s, -jnp.inf))
        m_news, alphas, probs = [], [], []
        for i in range(n_stream):
            s = scores[i]
            s_max = jnp.max(s, axis=-1, keepdims=True)
            if first:
                m_new = jnp.broadcast_to(s_max, (s.shape[0], w))
                alphas.append(None)
            else:
                m_old = m_ref[i, rows, :]
                m_new = jnp.maximum(m_old, s_max)
                alphas.append(jnp.exp(m_old - m_new))
            m_news.append(m_new)
            probs.append(jnp.exp(s - jnp.concatenate([m_new] * (s.shape[1] // w), axis=1)).astype(BF16))
        pvs = []
        for hh in range(D_HP):
            vext = jnp.concatenate([v_ref[pl.ds(k0, nk), hh * w:(hh + 1) * w], ones[:nk]], axis=1)
            for comp in range(2):
                pvs.append(_dot(probs[2 * hh + comp], vext))
        for i in range(n_stream):
            if first:
                a_ref[i, rows, :] = pvs[i]
            else:
                a_ref[i, rows, :] = jnp.concatenate([alphas[i]] * 2, axis=1) * a_ref[i, rows, :] + pvs[i]
            m_ref[i, rows, :] = m_news[i]

    hq = tq // 2
    d0 = pl.multiple_of(qi * tq, tq)

    def chunk_mask(q0, nk):
        rq = (lax.broadcasted_iota(I32, (hq, nk), 0) + q0) // CHUNK
        ck = lax.broadcasted_iota(I32, (hq, nk), 1) // CHUNK
        return ck <= rq

    block(d0, hq, rows=slice(0, hq), mask=chunk_mask(0, hq), first=True)
    block(d0, tq, rows=slice(hq, tq), mask=chunk_mask(hq, tq), first=True)

    def body(jj, carry):
        block(pl.multiple_of(jj * tq, tq), tq)
        return carry

    lax.fori_loop(0, qi, body, 0)

    lp = lam_ref[...]
    lam = (jnp.exp(jnp.sum(lp[0:1, :] * lp[1:2, :], axis=-1, keepdims=True))
           - jnp.exp(jnp.sum(lp[2:3, :] * lp[3:4, :], axis=-1, keepdims=True)) + lam_init)
    for hh in range(D_HP):
        a1 = a_ref[2 * hh]
        a2 = a_ref[2 * hh + 1]
        o = a1[:, :w] / a1[:, w:] - lam * (a2[:, :w] / a2[:, w:])
        out_ref[:, hh * w:(hh + 1) * w] = (_rms(o, gain_ref[...]) * (1.0 - lam_init)).astype(BF16)


def _diffattn(z, lam_p, d_gain, B, S, tq, lam_init):
    T = B * S
    nq = S // tq
    w = 2 * D_DH
    wp = D_HP * w
    return pl.pallas_call(
        functools.partial(_diffattn_kernel, tq=tq, lam_init=lam_init),
        grid=(B, D_HEADS // D_HP, nq),
        in_specs=[
            pl.BlockSpec((tq, wp), lambda b, h, i: (b * nq + i, OFF_QD // wp + h)),
            pl.BlockSpec((S, wp), lambda b, h, i: (b, OFF_KD // wp + h)),
            pl.BlockSpec((S, wp), lambda b, h, i: (b, OFF_VD // wp + h)),
            pl.BlockSpec((4, D_DH), lambda b, h, i: (0, 0)),
            pl.BlockSpec((1, w), lambda b, h, i: (0, 0)),
        ],
        out_specs=pl.BlockSpec((tq, wp), lambda b, h, i: (b * nq + i, h)),
        out_shape=jax.ShapeDtypeStruct((T, D_V), BF16),
        scratch_shapes=[
            pltpu.VMEM((2 * D_HP, tq, w), F32), pltpu.VMEM((2 * D_HP, tq, 2 * w), F32),
        ],
        compiler_params=_cparams(("arbitrary", "arbitrary", "arbitrary")),
        name="diffattn",
    )(z, z, z, lam_p, d_gain)


def _merge_kernel(hm_ref, hd_ref, gz_ref, x_ref, wbm_ref, wbd_ref, wout_ref, bg_ref, out_ref):
    bm = _dot(hm_ref[...], wbm_ref[...])
    bd = _dot(hd_ref[...], wbd_ref[...])
    g = _sigmoid(gz_ref[...].astype(F32) + bg_ref[...])
    merged = g[:, :D_MODEL] * bm + g[:, D_MODEL:] * bd
    out_ref[...] = x_ref[...] + _dot(merged.astype(BF16), wout_ref[...])


def _merge(hm, hd, z, x2d, w_bm, w_bd, w_out, b_gate, tm):
    T = x2d.shape[0]
    full = lambda i: (0, 0)
    return pl.pallas_call(
        _merge_kernel,
        grid=(T // tm,),
        in_specs=[
            pl.BlockSpec((tm, M_V), lambda i: (i, 0)),
            pl.BlockSpec((tm, D_V), lambda i: (i, 0)),
            pl.BlockSpec((tm, 2 * D_MODEL), lambda i: (i, OFF_G // (2 * D_MODEL))),
            pl.BlockSpec((tm, D_MODEL), lambda i: (i, 0)),
            pl.BlockSpec((M_V, D_MODEL), full),
            pl.BlockSpec((D_V, D_MODEL), full),
            pl.BlockSpec((D_MODEL, D_MODEL), full),
            pl.BlockSpec((1, 2 * D_MODEL), full),
        ],
        out_specs=pl.BlockSpec((tm, D_MODEL), lambda i: (i, 0)),
        out_shape=jax.ShapeDtypeStruct((T, D_MODEL), F32),
        compiler_params=_cparams(("arbitrary",)),
        name="merge",
    )(hm, hd, z, x2d, w_bm, w_bd, w_out, b_gate)


def _memkv_kernel(mem_ref, g_ref, w_ref, out_ref):
    out_ref[...] = _dot(_rms(mem_ref[...], g_ref[...]).astype(BF16), w_ref[...]).astype(BF16)


def _memkv(mem2d, g, wkv, n_mem):
    R = mem2d.shape[0]
    return pl.pallas_call(
        _memkv_kernel,
        grid=(R // n_mem,),
        in_specs=[
            pl.BlockSpec((n_mem, D_MODEL), lambda i: (i, 0)),
            pl.BlockSpec((1, D_MODEL), lambda i: (0, 0)),
            pl.BlockSpec((D_MODEL, 2 * D_MODEL), lambda i: (0, 0)),
        ],
        out_specs=pl.BlockSpec((n_mem, 2 * D_MODEL), lambda i: (i, 0)),
        out_shape=jax.ShapeDtypeStruct((R, 2 * D_MODEL), BF16),
        compiler_params=_cparams(("arbitrary",)),
        name="memkv",
    )(mem2d, g, wkv)


def _xattn_kernel(x_ref, g_ref, wq_ref, kv_ref, wo_ref, out_ref, o_ref):
    x = x_ref[...]
    h = _rms(x, g_ref[...]).astype(BF16)
    q = (_dot(h, wq_ref[...]) * (X_DH ** -0.5)).astype(BF16)
    scores = [_dot_nt(q[:, hd * X_DH:(hd + 1) * X_DH], kv_ref[:, hd * X_DH:(hd + 1) * X_DH]) for hd in range(X_HEADS)]
    for hd in range(X_HEADS):
        vh = kv_ref[:, D_MODEL + hd * X_DH:D_MODEL + (hd + 1) * X_DH]
        s = scores[hd]
        p = jnp.exp(s - jnp.max(s, axis=-1, keepdims=True))
        p = p / jnp.sum(p, axis=-1, keepdims=True)
        o_ref[:, hd * X_DH:(hd + 1) * X_DH] = _dot(p.astype(BF16), vh).astype(BF16)
    out_ref[...] = x + _dot(o_ref[...], wo_ref[...])


def _xattn(x1, g, wq, kvmem, wo, S, n_mem, tm):
    T = x1.shape[0]
    per_b = S // tm
    full = lambda i: (0, 0)
    return pl.pallas_call(
        _xattn_kernel,
        grid=(T // tm,),
        in_specs=[
            pl.BlockSpec((tm, D_MODEL), lambda i: (i, 0)),
            pl.BlockSpec((1, D_MODEL), full),
            pl.BlockSpec((D_MODEL, D_MODEL), full),
            pl.BlockSpec((n_mem, 2 * D_MODEL), lambda i: (i // per_b, 0)),
            pl.BlockSpec((D_MODEL, D_MODEL), full),
        ],
        out_specs=pl.BlockSpec((tm, D_MODEL), lambda i: (i, 0)),
        out_shape=jax.ShapeDtypeStruct((T, D_MODEL), F32),
        scratch_shapes=[pltpu.VMEM((tm, D_MODEL), BF16)],
        compiler_params=_cparams(("arbitrary",)),
        name="xattn",
    )(x1, g, wq, kvmem, wo)


def _router_kernel(x_ref, g_ref, wrh_ref, wrl_ref, br_ref, hp_ref, ids_ref, tw_ref, cnt_ref):
    @pl.when(pl.program_id(0) == 0)
    def _():
        cnt_ref[...] = jnp.zeros_like(cnt_ref)

    hn = _rms(x_ref[...], g_ref[...])
    hh, hl = _split_bf16(hn)
    hp_ref[...] = hh
    logits = _dot(hh, wrh_ref[...]) + _dot(hh, wrl_ref[...]) + _dot(hl, wrh_ref[...]) + br_ref[...]
    lane = lax.broadcasted_iota(I32, logits.shape, 1)
    lanef = lane.astype(F32)
    ids = jnp.zeros(logits.shape, F32)
    tw = jnp.zeros(logits.shape, F32)
    onehot = jnp.zeros(logits.shape, F32)
    v0 = None
    den = None
    for kk in range(TOP_K):
        mx = jnp.max(logits, axis=-1, keepdims=True)
        idx = jnp.min(jnp.where(logits == mx, lanef, float(LANES)), axis=-1, keepdims=True)
        sel = lanef == idx
        if kk == 0:
            v0 = mx
        e = jnp.exp(mx - v0)
        den = e if den is None else den + e
        ids = jnp.where(lane == kk, idx, ids)
        tw = jnp.where(lane == kk, e, tw)
        onehot = jnp.where(sel, 1.0, onehot)
        logits = jnp.where(sel, -jnp.inf, logits)
    ids_ref[...] = ids.astype(I32)
    tw_ref[...] = tw / den
    cnt_ref[...] += _ceil_rows(jnp.sum(onehot, axis=0, keepdims=True))


def _router(x2, g, wr_hi, wr_lo, b_r, tm):
    T = x2.shape[0]
    full = lambda i: (0, 0)
    return pl.pallas_call(
        _router_kernel,
        grid=(T // tm,),
        in_specs=[
            pl.BlockSpec((tm, D_MODEL), lambda i: (i, 0)),
            pl.BlockSpec((1, D_MODEL), full),
            pl.BlockSpec((D_MODEL, LANES), full),
            pl.BlockSpec((D_MODEL, LANES), full),
            pl.BlockSpec((1, LANES), full),
        ],
        out_specs=[
            pl.BlockSpec((tm, D_MODEL), lambda i: (i, 0)),
            pl.BlockSpec((tm, LANES), lambda i: (i, 0)),
            pl.BlockSpec((tm, LANES), lambda i: (i, 0)),
            pl.BlockSpec((1, LANES), full),
        ],
        out_shape=[
            jax.ShapeDtypeStruct((T, D_MODEL), BF16),
            jax.ShapeDtypeStruct((T, LANES), I32),
            jax.ShapeDtypeStruct((T, LANES), F32),
            jax.ShapeDtypeStruct((1, LANES), F32),
        ],
        compiler_params=_cparams(("arbitrary",)),
        name="router",
    )(x2, g, wr_hi, wr_lo, b_r)


SEG_WORDS = SUBLANES * LANES


def _slots_kernel(ids_ref, start_ref, ls_ref, lst_ref, seg_ref, run_ref):
    @pl.when(pl.program_id(0) == 0)
    def _():
        run_ref[...] = jnp.zeros_like(run_ref)

    ids = ids_ref[...]
    tm = ids.shape[0]
    lane = lax.broadcasted_iota(I32, ids.shape, 1)
    sels = [lane == ids[:, kk:kk + 1] for kk in range(TOP_K)]
    onehot = jnp.zeros(ids.shape, F32)
    for s in sels:
        onehot = jnp.where(s, 1.0, onehot)
    c8 = _ceil_rows(jnp.sum(onehot, axis=0, keepdims=True))
    er = lax.broadcasted_iota(I32, (LANES, LANES), 0)
    ec = lax.broadcasted_iota(I32, (LANES, LANES), 1)
    before = jnp.where(er < ec, 1.0, 0.0).astype(BF16)
    pieces = jnp.broadcast_to(c8 * (1.0 / SUBLANES), (SUBLANES, LANES)).astype(BF16)
    lo = _dot(pieces, before)[0:1, :] * SUBLANES
    r = lax.broadcasted_iota(I32, (tm, tm), 0)
    c = lax.broadcasted_iota(I32, (tm, tm), 1)
    strict = jnp.where(c < r, 1.0, 0.0).astype(BF16)
    slot = _dot(strict, onehot.astype(BF16)) + lo
    ls = jnp.zeros(ids.shape, F32)
    for kk, s in enumerate(sels):
        pk = jnp.sum(jnp.where(s, slot, 0.0), axis=-1, keepdims=True)
        ls = jnp.where(lane == kk, pk, ls)
    ls_ref[...] = ls
    hi = jnp.floor(ls * (1.0 / 32.0))
    rem = ls - 32.0 * hi
    pick = jnp.where(lax.broadcasted_iota(I32, (SUBLANES, LANES), 0) == lax.broadcasted_iota(I32, (SUBLANES, LANES), 1),
                     1.0, 0.0).astype(BF16)
    lst_ref[...] = 32.0 * _dot_nt(pick, hi.astype(BF16)) + _dot_nt(pick, rem.astype(BF16))
    row = lax.broadcasted_iota(I32, (SUBLANES, LANES), 0)
    off = start_ref[...] + run_ref[...]
    total = jnp.sum(c8, axis=-1, keepdims=True)
    seg = jnp.where(row == 0, c8, jnp.where(row == 1, lo, jnp.where(row == 2, off, jnp.where(row == 3, total, 0.0))))
    seg_ref[...] = seg.astype(I32)
    run_ref[...] += c8


def _slots(ids, starts, tm):
    T = ids.shape[0]
    nt = T // tm
    return pl.pallas_call(
        _slots_kernel,
        grid=(nt,),
        in_specs=[
            pl.BlockSpec((tm, LANES), lambda i: (i, 0)),
            pl.BlockSpec((1, LANES), lambda i: (0, 0)),
        ],
        out_specs=[
            pl.BlockSpec((tm, LANES), lambda i: (i, 0)),
            pl.BlockSpec((SUBLANES, tm), lambda i: (0, i)),
            pl.BlockSpec((SUBLANES, LANES), lambda i: (i, 0)),
        ],
        out_shape=[
            jax.ShapeDtypeStruct((T, LANES), F32),
            jax.ShapeDtypeStruct((SUBLANES, T), F32),
            jax.ShapeDtypeStruct((nt * SUBLANES, LANES), I32),
        ],
        scratch_shapes=[pltpu.VMEM((1, LANES), F32)],
        compiler_params=_cparams(("arbitrary",)),
        name="slots",
    )(ids, starts)


def _local_rows(tm):
    need = tm * TOP_K + N_EXPERTS * (SUBLANES - 1)
    return ((need + LANES - 1) // LANES) * LANES


BIG_PIECE = 4 * SUBLANES


def _segment_starts(seg, make_copy):
    def expert(e, carry):
        cnt = seg(e)
        lo = seg(LANES + e)
        off = seg(2 * LANES + e)
        n_big = lax.shift_right_logical(cnt, 5)
        n_small = lax.shift_right_logical(cnt & (BIG_PIECE - 1), 3)

        def big(j, carry2):
            d = j * BIG_PIECE
            make_copy(pl.multiple_of(lo + d, SUBLANES), pl.multiple_of(off + d, SUBLANES), BIG_PIECE).start()
            return carry2

        def small(j, carry2):
            d = n_big * BIG_PIECE + j * SUBLANES
            make_copy(pl.multiple_of(lo + d, SUBLANES), pl.multiple_of(off + d, SUBLANES), SUBLANES).start()
            return carry2

        lax.fori_loop(0, n_big, big, 0)
        lax.fori_loop(0, n_small, small, 0)
        return carry

    for e in range(N_EXPERTS):
        expert(e, 0)


def _segment_waits(total_rows, make_copy):
    def big(j, carry):
        make_copy(0, 0, BIG_PIECE).wait()
        return carry

    def small(j, carry):
        make_copy(0, 0, SUBLANES).wait()
        return carry

    lax.fori_loop(0, lax.shift_right_logical(total_rows, 5), big, 0)
    lax.fori_loop(0, lax.shift_right_logical(total_rows & (BIG_PIECE - 1), 3), small, 0)


def _dispatch_kernel(h_ref, lst_ref, seg_hbm, xs_ref, sbuf_ref, seg_smem, prev_smem, sem_seg, sem_rows, *, tm):
    i = pl.program_id(0)
    n = pl.num_programs(0)
    slot = i % 2
    rows = sbuf_ref.shape[1]
    cp = pltpu.make_async_copy(seg_hbm.at[i], seg_smem, sem_seg)
    cp.start()
    lst = lst_ref[...].astype(I32)
    rid = lax.broadcasted_iota(I32, (rows, tm), 0)
    perm = jnp.zeros((rows, tm), F32)
    for kk in range(TOP_K):
        perm = perm + jnp.where(rid == lst[kk:kk + 1, :], 1.0, 0.0)
    srt = _dot(perm.astype(BF16), h_ref[...])
    sbuf_ref[slot] = _pack_bf16_pairs(srt)
    cp.wait()

    def copy_from(s):
        def make_copy(lo, off, nrows):
            return pltpu.make_async_copy(sbuf_ref.at[s, pl.ds(lo, nrows), :], xs_ref.at[pl.ds(off, nrows), :],
                                         sem_rows.at[s])
        return make_copy

    _segment_starts(lambda k: seg_smem[k], copy_from(slot))

    @pl.when(i > 0)
    def _():
        _segment_waits(prev_smem[0], copy_from(1 - slot))

    prev_smem[0] = seg_smem[3 * LANES]

    @pl.when(i == n - 1)
    def _():
        _segment_waits(prev_smem[0], copy_from(slot))


def _dispatch(h, lst, seg2d, n_rows, tm):
    T = h.shape[0]
    rows = _local_rows(tm)
    return pl.pallas_call(
        functools.partial(_dispatch_kernel, tm=tm),
        grid=(T // tm,),
        in_specs=[
            pl.BlockSpec((tm, D_MODEL), lambda i: (i, 0)),
            pl.BlockSpec((SUBLANES, tm), lambda i: (0, i)),
            pl.BlockSpec(memory_space=pl.ANY),
        ],
        out_specs=pl.BlockSpec(memory_space=pl.ANY),
        out_shape=jax.ShapeDtypeStruct((n_rows, D_MODEL // 2), U32),
        scratch_shapes=[
            pltpu.VMEM((2, rows, D_MODEL // 2), U32),
            pltpu.SMEM((SEG_WORDS,), I32),
            pltpu.SMEM((1,), I32),
            pltpu.SemaphoreType.DMA,
            pltpu.SemaphoreType.DMA((2,)),
        ],
        compiler_params=_cparams(("arbitrary",)),
        name="dispatch",
    )(h, lst, seg2d)


def _experts_kernel(te_ref, nu_ref, nv_ref, xs_ref, wgu_ref, bgu_ref, wdn_ref, bdn_ref, ys_ref, wgu_bf, wdn_bf):
    i = pl.program_id(0)
    half = D_MODEL // 2

    @pl.when((i == 0) | (te_ref[i] != te_ref[jnp.maximum(i - 1, 0)]))
    def _():
        wgu_bf[...] = wgu_ref[0].astype(BF16)
        wdn_bf[...] = wdn_ref[0].astype(BF16)

    @pl.when(i < nu_ref[0])
    def _():
        live = lax.broadcasted_iota(I32, xs_ref.shape, 0) < nv_ref[i]
        lo, hi = _unpack_bf16_pairs(jnp.where(live, xs_ref[...], jnp.uint32(0)))
        gu = (_dot(lo.astype(BF16), wgu_bf[:half, :]) + _dot(hi.astype(BF16), wgu_bf[half:, :])
              + bgu_ref[0])
        gate = jnp.minimum(gu[:, :D_FF], SWIGLU_LIMIT)
        up = jnp.clip(gu[:, D_FF:], -SWIGLU_LIMIT, SWIGLU_LIMIT)
        act = (up + 1.0) * (gate * _sigmoid(SWIGLU_ALPHA * gate))
        y = _dot(act.astype(BF16), wdn_bf[...]) + bdn_ref[0]
        ys_ref[...] = _pack_bf16_pairs(y)

    @pl.when(i >= nu_ref[0])
    def _():
        ys_ref[...] = jnp.zeros_like(ys_ref)


def _experts(tile_expert, n_used, tile_valid, xs, w_gu, b_gu, w_dn, b_dn, tg):
    P = xs.shape[0]
    half = D_MODEL // 2
    grid_spec = pltpu.PrefetchScalarGridSpec(
        num_scalar_prefetch=3,
        grid=(P // tg,),
        in_specs=[
            pl.BlockSpec((tg, half), lambda i, te, nu, nv: (jnp.minimum(i, jnp.maximum(nu[0] - 1, 0)), 0)),
            pl.BlockSpec((1, D_MODEL, 2 * D_FF), lambda i, te, nu, nv: (te[i], 0, 0)),
            pl.BlockSpec((1, 1, 2 * D_FF), lambda i, te, nu, nv: (te[i], 0, 0)),
            pl.BlockSpec((1, D_FF, D_MODEL), lambda i, te, nu, nv: (te[i], 0, 0)),
            pl.BlockSpec((1, 1, D_MODEL), lambda i, te, nu, nv: (te[i], 0, 0)),
        ],
        out_specs=pl.BlockSpec((tg, half), lambda i, te, nu, nv: (i, 0)),
        scratch_shapes=[pltpu.VMEM((D_MODEL, 2 * D_FF), BF16), pltpu.VMEM((D_FF, D_MODEL), BF16)],
    )
    return pl.pallas_call(
        _experts_kernel,
        grid_spec=grid_spec,
        out_shape=jax.ShapeDtypeStruct((P, half), U32),
        compiler_params=_cparams(("arbitrary",)),
        name="experts",
    )(tile_expert, n_used, tile_valid, xs, w_gu, b_gu, w_dn, b_dn)


def _combine_kernel(seg_hbm, ys_hbm, ls_ref, tw_ref, x_ref, g_ref, out_ref, ybuf_ref, seg_smem, sem_seg, sem_rows, *,
                    tm, final_norm):
    i = pl.program_id(0)
    n = pl.num_programs(0)
    slot = i % 2
    rows = ybuf_ref.shape[1]

    def copy_into(s):
        def make_copy(lo, off, nrows):
            return pltpu.make_async_copy(ys_hbm.at[pl.ds(off, nrows), :], ybuf_ref.at[s, pl.ds(lo, nrows), :],
                                         sem_rows.at[s])
        return make_copy

    def request(step, s):
        cp = pltpu.make_async_copy(seg_hbm.at[step], seg_smem.at[s], sem_seg)
        cp.start()
        cp.wait()
        _segment_starts(lambda k: seg_smem[s, k], copy_into(s))

    @pl.when(i == 0)
    def _():
        ybuf_ref[...] = jnp.zeros_like(ybuf_ref)
        request(0, 0)

    @pl.when(i + 1 < n)
    def _():
        request(i + 1, 1 - slot)

    ls = ls_ref[...].astype(I32)
    tw = tw_ref[...]
    cid = lax.broadcasted_iota(I32, (tm, rows), 1)
    wmat = jnp.zeros((tm, rows), F32)
    for kk in range(TOP_K):
        wmat = wmat + jnp.where(cid == ls[:, kk:kk + 1], tw[:, kk:kk + 1], 0.0)
    wmat = wmat.astype(BF16)
    _segment_waits(seg_smem[slot, 3 * LANES], copy_into(slot))
    lo, hi = _unpack_bf16_pairs(ybuf_ref[slot])
    moe = jnp.concatenate([_dot(wmat, lo.astype(BF16)), _dot(wmat, hi.astype(BF16))], axis=1)
    x3 = x_ref[...] + moe
    out_ref[...] = _rms(x3, g_ref[...]) if final_norm else x3


def _combine(seg2d, ys, ls, tw, x2, g, tm, final_norm):
    T = x2.shape[0]
    rows = _local_rows(tm)
    return pl.pallas_call(
        functools.partial(_combine_kernel, tm=tm, final_norm=final_norm),
        grid=(T // tm,),
        in_specs=[
            pl.BlockSpec(memory_space=pl.ANY),
            pl.BlockSpec(memory_space=pl.ANY),
            pl.BlockSpec((tm, LANES), lambda i: (i, 0)),
            pl.BlockSpec((tm, LANES), lambda i: (i, 0)),
            pl.BlockSpec((tm, D_MODEL), lambda i: (i, 0)),
            pl.BlockSpec((1, D_MODEL), lambda i: (0, 0)),
        ],
        out_specs=pl.BlockSpec((tm, D_MODEL), lambda i: (i, 0)),
        out_shape=jax.ShapeDtypeStruct((T, D_MODEL), F32),
        scratch_shapes=[
            pltpu.VMEM((2, rows, D_MODEL // 2), U32),
            pltpu.SMEM((2, SEG_WORDS), I32),
            pltpu.SemaphoreType.DMA,
            pltpu.SemaphoreType.DMA((2,)),
        ],
        compiler_params=_cparams(("arbitrary",)),
        name="combine",
    )(seg2d, ys, ls, tw, x2, g)


def _tiles(B, S):
    T = B * S
    return dict(
        tm_in=min(1024, S), tn_in=2048,
        ts=min(512, S),
        tq=min(1024, S),
        tm_proj=min(512, S),
        tm_route=min(256, T),
        tg=512,
    )


def _pad_lanes(a, n=LANES, value=0.0):
    return jnp.pad(a, ((0, 0), (0, n - a.shape[1])), constant_values=value)


def kernel(x, mem, norm_mix, w_in, conv_w, b_if, mlstm_gain, diff_lambda, diff_gain, w_branch_m, w_branch_d,
           b_gate, w_out, norm_xattn, norm_mem, wq_x, wkv_x, wo_x, norm_ffn, w_router, b_router, w_gu, b_gu,
           w_dn, b_dn, norm_final):
    B, S, D = x.shape
    n_mem = mem.shape[1]
    T = B * S
    depth = norm_mix.shape[0]
    tl = _tiles(B, S)
    x2d = x.reshape(T, D)
    mem2d = mem.reshape(B * n_mem, D)

    for l in range(depth):
        lam_init = 0.8 - 0.6 * math.exp(-0.3 * l)
        wl = w_in[l]
        if_lo = 2 * M_QK + 2 * M_V
        w_main = jnp.concatenate([wl[:, :if_lo], wl[:, if_lo + 2 * M_HEADS:]], axis=1).astype(BF16)
        w_if = wl[:, if_lo:if_lo + 2 * M_HEADS]
        w_ifp = _pad_lanes(w_if).astype(BF16)
        w_ift = w_if.T.astype(BF16)
        bif = _pad_lanes(b_if[l][None, :])
        bift = jnp.broadcast_to(b_if[l][:, None], (SUBLANES, LANES))

        z, zif, zift, zqk = _inproj(x2d, norm_mix[l][None, :], w_main, w_ifp, w_ift, conv_w[l], tl["tm_in"],
                                    tl["tn_in"], S // tl["tm_in"])
        hm = _mlstm(zqk, z, zif, zift, bif, bift, mlstm_gain[l].reshape(1, M_V), B, S, tl["ts"])
        hd = _diffattn(z, diff_lambda[l], diff_gain[l][None, :], B, S, tl["tq"], lam_init)
        x1 = _merge(hm, hd, z, x2d, w_branch_m[l].astype(BF16), w_branch_d[l].astype(BF16),
                    w_out[l].astype(BF16), b_gate[l][None, :], tl["tm_proj"])

        kvmem = _memkv(mem2d, norm_mem[l][None, :], wkv_x[l].astype(BF16), n_mem)
        x2 = _xattn(x1, norm_xattn[l][None, :], wq_x[l].astype(BF16), kvmem, wo_x[l].astype(BF16),
                    S, n_mem, tl["tm_proj"])

        wr = _pad_lanes(w_router[l])
        wr_hi = wr.astype(BF16)
        wr_lo = (wr - wr_hi.astype(F32)).astype(BF16)
        br = _pad_lanes(b_router[l][None, :], value=-jnp.inf)
        tm_r = tl["tm_route"]
        tg = tl["tg"]
        hp, ids, tw, cnt = _router(x2, norm_ffn[l][None, :], wr_hi, wr_lo, br, tm_r)

        counts = cnt[0, :N_EXPERTS].astype(I32)
        padded = ((counts + tg - 1) // tg) * tg
        ends = jnp.cumsum(padded)
        starts = ends - padded
        max_rows = T * TOP_K + (T // tm_r) * N_EXPERTS * (SUBLANES - 1)
        n_tiles = -(-max_rows // tg) + N_EXPERTS
        tile_row0 = jnp.arange(n_tiles, dtype=I32) * tg
        tile_expert = jnp.minimum(jnp.sum((tile_row0[:, None] >= ends[None, :]).astype(I32), axis=1), N_EXPERTS - 1)
        n_used = (ends[-1] // tg).astype(I32).reshape(1)
        last_used = tile_expert[jnp.maximum(n_used[0] - 1, 0)]
        tile_expert = jnp.where(tile_row0 < ends[-1], tile_expert, last_used)
        tile_valid = jnp.clip((starts + counts)[tile_expert] - tile_row0, 0, tg).astype(I32)

        ls, lst, seg = _slots(ids, _pad_lanes(starts.astype(F32)[None, :]), tm_r)
        seg2d = seg.reshape(T // tm_r, SEG_WORDS)

        xs = _dispatch(hp, lst, seg2d, n_tiles * tg, tm_r)
        ys = _experts(tile_expert, n_used, tile_valid, xs, w_gu[l], b_gu[l][:, None, :], w_dn[l], b_dn[l][:, None, :],
                      tg)
        x2d = _combine(seg2d, ys, ls, tw, x2, norm_final[None, :], tm_r, final_norm=(l == depth - 1))
    return x2d.reshape(B, S, D)
```

```python
import functools
import math

import jax
import jax.numpy as jnp
from jax import lax
from jax.experimental import pallas as pl
from jax.experimental.pallas import tpu as pltpu

F32 = jnp.float32
BF16 = jnp.bfloat16
U32 = jnp.uint32
I32 = jnp.int32

EPS = 1e-6
CHUNK = 64
D_MODEL = 1024
M_HEADS = 4
M_DK = 128
M_DV = 256
M_QK = M_HEADS * M_DK
M_V = M_HEADS * M_DV
CONV_W = 4
D_HEADS = 8
D_DH = 64
D_HP = 2
D_QK = D_HEADS * 2 * D_DH
D_V = D_HEADS * 2 * D_DH
X_HEADS = 4
X_DH = D_MODEL // X_HEADS
N_EXPERTS = 32
TOP_K = 4
D_FF = D_MODEL
SWIGLU_LIMIT = 7.0
SWIGLU_ALPHA = 1.702

LANES = 128
SUBLANES = 8
N_MAIN = 2 * M_QK + 2 * M_V + 2 * D_QK + D_V + 2 * D_MODEL
OFF_QM, OFF_KM, OFF_VM, OFF_OM = 0, M_QK, 2 * M_QK, 2 * M_QK + M_V
OFF_QD = OFF_OM + M_V
OFF_KD = OFF_QD + D_QK
OFF_VD = OFF_KD + D_QK
OFF_G = OFF_VD + D_V

VMEM_LIMIT = 56 * 1024 * 1024


def _cparams(sem, vmem=VMEM_LIMIT):
    return pltpu.CompilerParams(dimension_semantics=sem, vmem_limit_bytes=vmem)


def _rms(x, g):
    return x * lax.rsqrt(jnp.mean(x * x, axis=-1, keepdims=True) + EPS) * g


def _split_bf16(x):
    hi = x.astype(BF16)
    lo = (x - hi.astype(F32)).astype(BF16)
    return hi, lo


def _dot(a, b):
    return jnp.dot(a, b, preferred_element_type=F32)


def _dot_nt(a, b):
    return lax.dot_general(a, b, (((1,), (1,)), ((), ())), preferred_element_type=F32)


def _sigmoid(x):
    return 1.0 / (1.0 + jnp.exp(-x))


def _log_sigmoid(x):
    return jnp.minimum(x, 0.0) - jnp.log(1.0 + jnp.exp(-jnp.abs(x)))


def _pack_bf16_pairs(x):
    w = x.shape[1] // 2
    u = lax.bitcast_convert_type(x, U32)
    r = (u + jnp.uint32(0x7FFF) + ((u >> 16) & jnp.uint32(1))) >> 16
    return r[:, :w] | (r[:, w:] << 16)


def _ceil_rows(x):
    return jnp.floor((x + (SUBLANES - 1)) * (1.0 / SUBLANES)) * SUBLANES


def _unpack_bf16_pairs(p):
    lo = lax.bitcast_convert_type(p << 16, F32)
    hi = lax.bitcast_convert_type(p & jnp.uint32(0xFFFF0000), F32)
    return lo, hi


def _inproj_kernel(x_ref, g_ref, w_ref, wif_ref, wift_ref, z_ref, zif_ref, zift_ref, hn_ref):
    @pl.when(pl.program_id(1) == 0)
    def _():
        hn = _rms(x_ref[...], g_ref[...]).astype(BF16)
        hn_ref[...] = hn
        zif_ref[...] = _dot(hn, wif_ref[...])
        zift_ref[...] = _dot_nt(wift_ref[...], hn)

    z_ref[...] = _dot(hn_ref[...], w_ref[...]).astype(BF16)


def _inproj(x2d, g, w_main, w_if, w_ift, tm, tn):
    T = x2d.shape[0]
    return pl.pallas_call(
        _inproj_kernel,
        grid=(T // tm, N_MAIN // tn),
        in_specs=[
            pl.BlockSpec((tm, D_MODEL), lambda i, j: (i, 0)),
            pl.BlockSpec((1, D_MODEL), lambda i, j: (0, 0)),
            pl.BlockSpec((D_MODEL, tn), lambda i, j: (0, j)),
            pl.BlockSpec((D_MODEL, LANES), lambda i, j: (0, 0)),
            pl.BlockSpec((SUBLANES, D_MODEL), lambda i, j: (0, 0)),
        ],
        out_specs=[
            pl.BlockSpec((tm, tn), lambda i, j: (i, j)),
            pl.BlockSpec((tm, LANES), lambda i, j: (i, 0)),
            pl.BlockSpec((SUBLANES, tm), lambda i, j: (0, i)),
        ],
        out_shape=[
            jax.ShapeDtypeStruct((T, N_MAIN), BF16),
            jax.ShapeDtypeStruct((T, LANES), F32),
            jax.ShapeDtypeStruct((SUBLANES, T), F32),
        ],
        scratch_shapes=[pltpu.VMEM((tm, D_MODEL), BF16)],
        compiler_params=_cparams(("arbitrary", "arbitrary")),
        name="inproj",
    )(x2d, g, w_main, w_if, w_ift)


def _mlstm_kernel(q_ref, k_ref, v_ref, om_ref, zif_ref, zift_ref, cw_ref, bif_ref, bift_ref, mg_ref,
                  out_ref, qc_ref, kc_ref, kt_ref, carry_ref, c_ref, n_ref, m_ref,
                  bd_ref, bdt_ref, brep_ref, grow_ref, brow_ref, *, ts):
    nchunk = ts // CHUNK
    L = CHUNK

    @pl.when(pl.program_id(1) == 0)
    def _():
        carry_ref[...] = jnp.zeros_like(carry_ref)
        c_ref[...] = jnp.zeros_like(c_ref)
        n_ref[...] = jnp.zeros_like(n_ref)
        m_ref[...] = jnp.zeros_like(m_ref)
        rt = lax.broadcasted_iota(I32, (ts, ts), 0)
        ct = lax.broadcasted_iota(I32, (ts, ts), 1)
        same = (rt // L) == (ct // L)
        bd_ref[...] = jnp.where(same, jnp.where(ct <= rt, 1.0, 0.0), 0.0).astype(BF16)
        bdt_ref[...] = jnp.where(same, jnp.where(rt <= ct, 1.0, 0.0), 0.0).astype(BF16)

    row8 = lax.broadcasted_iota(I32, (SUBLANES, M_QK), 0)

    def conv_silu(x, prev8, w):
        acc = w[CONV_W - 1:CONV_W, :] * x
        for s in range(1, CONV_W):
            xs = pltpu.roll(x, s, 0)
            top = jnp.where(row8 < s, pltpu.roll(prev8, s, 0), xs[0:SUBLANES])
            xs = jnp.concatenate([top, xs[SUBLANES:]], axis=0)
            acc = acc + w[CONV_W - 1 - s:CONV_W - s, :] * xs
        return acc * _sigmoid(acc)

    def conv_body(c, carry):
        r0 = pl.multiple_of(c * L, L)
        xq = q_ref[pl.ds(r0, L), :].astype(F32)
        xk = k_ref[pl.ds(r0, L), :].astype(F32)
        yq = conv_silu(xq, carry_ref[:, 0:M_QK], cw_ref[:, 0:M_QK]) * (M_DK ** -0.5)
        yk = conv_silu(xk, carry_ref[:, M_QK:2 * M_QK], cw_ref[:, M_QK:2 * M_QK])
        qc_ref[pl.ds(r0, L), :] = yq.astype(BF16)
        kc_ref[pl.ds(r0, L), :] = yk.astype(BF16)
        for h in range(M_HEADS):
            kt_ref[c, h] = yk[:, h * M_DK:(h + 1) * M_DK].T
        carry_ref[:, 0:M_QK] = xq[L - SUBLANES:L]
        carry_ref[:, M_QK:2 * M_QK] = xk[L - SUBLANES:L]
        return carry

    lax.fori_loop(0, nchunk, conv_body, 0, unroll=2)

    ti = lax.broadcasted_iota(I32, (L, L), 0)
    si = lax.broadcasted_iota(I32, (L, L), 1)
    causal = si <= ti
    lane_row = lax.broadcasted_iota(I32, (LANES, LANES), 0)
    ones_l = jnp.ones((L, LANES), BF16)

    lf_col = _log_sigmoid(zif_ref[...] + bif_ref[...])
    ch, cl = _split_bf16(lf_col)
    b_col_all = _dot(bd_ref[...], ch) + _dot(bd_ref[...], cl)
    bh, bl = _split_bf16(b_col_all)
    for h in range(M_HEADS):
        sel_f = jnp.where(lane_row == M_HEADS + h, 1.0, 0.0).astype(BF16)
        brep_ref[h] = _dot(bh, sel_f) + _dot(bl, sel_f)
    g_row_all = zift_ref[...] + bift_ref[:, 0:1]
    rh, rl = _split_bf16(_log_sigmoid(g_row_all))
    b_row_tile = _dot(rh, bdt_ref[...]) + _dot(rl, bdt_ref[...])
    for cc in range(nchunk):
        grow_ref[cc] = g_row_all[:, cc * L:(cc + 1) * L]
        brow_ref[cc] = b_row_tile[:, cc * L:(cc + 1) * L]

    def chunk_body(c, carry):
        r0 = pl.multiple_of(c * L, L)
        g_row = grow_ref[c]
        b_row_all = brow_ref[c]
        early = []
        for h in range(M_HEADS):
            b_rep = brep_ref[h, pl.ds(r0, L), :]
            i_row = g_row[h:h + 1, :]
            b_row = b_row_all[M_HEADS + h:M_HEADS + h + 1, :]
            b_last = b_rep[L - 1:L, :]
            q = qc_ref[pl.ds(r0, L), h * M_DK:(h + 1) * M_DK]
            k = kc_ref[pl.ds(r0, L), h * M_DK:(h + 1) * M_DK]
            vext = jnp.concatenate([v_ref[pl.ds(r0, L), h * M_DV:(h + 1) * M_DV], ones_l], axis=1)
            dm = jnp.where(causal, b_rep[:, :L] - b_row + i_row, -jnp.inf)
            m_loc = jnp.max(dm, axis=-1, keepdims=True)
            qk = _dot_nt(q, k)
            gk_row = b_last[:, :L] - b_row + i_row
            g_max = jnp.max(gk_row, axis=-1, keepdims=True)
            kwt = (kt_ref[c, h] * jnp.exp(gk_row - g_max)).astype(BF16)
            kv = _dot(kwt, vext)
            c_old = c_ref[h]
            n_old = n_ref[h]
            qcn = _dot(q, jnp.concatenate([c_old, n_old], axis=1).astype(BF16))
            early.append((b_rep, b_last, vext, dm, m_loc, qk, g_max, kv, c_old, n_old, qcn))
        pvs = []
        for h in range(M_HEADS):
            b_rep, b_last, vext, dm, m_loc, qk, g_max, kv, c_old, n_old, qcn = early[h]
            s_loc = qk * jnp.exp(dm - m_loc)
            pvs.append(_dot(s_loc.astype(BF16), vext))
        for h in range(M_HEADS):
            b_rep, b_last, vext, dm, m_loc, qk, g_max, kv, c_old, n_old, qcn = early[h]
            pv = pvs[h]
            m_prev = m_ref[h:h + 1, :]
            inter = b_rep + m_prev
            m_t = jnp.maximum(inter, m_loc)
            w_inter = jnp.exp(inter - m_t)
            r_loc = jnp.exp(m_loc - m_t)
            den = r_loc * pv[:, M_DV:] + w_inter * qcn[:, M_DV:]
            inv = 1.0 / jnp.maximum(jnp.abs(den), jnp.exp(-m_t))
            hv = (jnp.concatenate([r_loc * inv] * 2, axis=1) * pv[:, :M_DV]
                  + jnp.concatenate([w_inter * inv] * 2, axis=1) * qcn[:, :M_DV])
            m_new = jnp.maximum(b_last + m_prev, g_max)
            decay = jnp.exp(b_last + m_prev - m_new)
            sc_loc = jnp.exp(g_max - m_new)
            c_ref[h] = (jnp.concatenate([decay] * 2, axis=1) * c_old
                        + jnp.concatenate([sc_loc] * 2, axis=1) * kv[:, :M_DV])
            n_ref[h] = decay * n_old + sc_loc * kv[:, M_DV:]
            m_ref[h:h + 1, :] = m_new
            hn = _rms(hv, mg_ref[:, h * M_DV:(h + 1) * M_DV])
            og = _sigmoid(om_ref[pl.ds(r0, L), h * M_DV:(h + 1) * M_DV].astype(F32))
            out_ref[pl.ds(r0, L), h * M_DV:(h + 1) * M_DV] = (og * hn).astype(BF16)
        return carry

    lax.fori_loop(0, nchunk, chunk_body, 0, unroll=4)


def _mlstm(z, zif, zift, conv_w, bif, bift, m_gain, B, S, ts):
    T = B * S
    nt = S // ts
    nck = ts // CHUNK
    row = lambda b, t: b * nt + t
    return pl.pallas_call(
        functools.partial(_mlstm_kernel, ts=ts),
        grid=(B, nt),
        in_specs=[
            pl.BlockSpec((ts, M_QK), lambda b, t: (row(b, t), OFF_QM // M_QK)),
            pl.BlockSpec((ts, M_QK), lambda b, t: (row(b, t), OFF_KM // M_QK)),
            pl.BlockSpec((ts, M_V), lambda b, t: (row(b, t), OFF_VM // M_V)),
            pl.BlockSpec((ts, M_V), lambda b, t: (row(b, t), OFF_OM // M_V)),
            pl.BlockSpec((ts, LANES), lambda b, t: (row(b, t), 0)),
            pl.BlockSpec((SUBLANES, ts), lambda b, t: (0, row(b, t))),
            pl.BlockSpec((CONV_W, 2 * M_QK), lambda b, t: (0, 0)),
            pl.BlockSpec((1, LANES), lambda b, t: (0, 0)),
            pl.BlockSpec((SUBLANES, LANES), lambda b, t: (0, 0)),
            pl.BlockSpec((1, M_V), lambda b, t: (0, 0)),
        ],
        out_specs=pl.BlockSpec((ts, M_V), lambda b, t: (row(b, t), 0)),
        out_shape=jax.ShapeDtypeStruct((T, M_V), BF16),
        scratch_shapes=[
            pltpu.VMEM((ts, M_QK), BF16),
            pltpu.VMEM((ts, M_QK), BF16),
            pltpu.VMEM((nck, M_HEADS, M_DK, CHUNK), F32),
            pltpu.VMEM((SUBLANES, 2 * M_QK), F32),
            pltpu.VMEM((M_HEADS, M_DK, M_DV), F32),
            pltpu.VMEM((M_HEADS, M_DK, LANES), F32),
            pltpu.VMEM((SUBLANES, LANES), F32),
            pltpu.VMEM((ts, ts), BF16),
            pltpu.VMEM((ts, ts), BF16),
            pltpu.VMEM((M_HEADS, ts, LANES), F32),
            pltpu.VMEM((nck, SUBLANES, CHUNK), F32),
            pltpu.VMEM((nck, SUBLANES, CHUNK), F32),
        ],
        compiler_params=_cparams(("arbitrary", "arbitrary")),
        name="mlstm",
    )(z, z, z, z, zif, zift, conv_w, bif, bift, m_gain)


def _diffattn_kernel(q_ref, k_ref, v_ref, lam_ref, gain_ref, out_ref, m_ref, a_ref, *, tq, lam_init):
    qi = pl.program_id(2)
    w = 2 * D_DH
    lane = lax.broadcasted_iota(I32, (1, w), 1)
    scale = jnp.asarray(D_DH ** -0.5, BF16)
    qs = []
    for hh in range(D_HP):
        q = q_ref[:, hh * w:(hh + 1) * w]
        qs.append((jnp.where(lane < D_DH, q, jnp.zeros_like(q)) * scale,
                   jnp.where(lane >= D_DH, q, jnp.zeros_like(q)) * scale))
    ones = jnp.ones((tq, w), BF16)

    def block(k0, nk, rows=slice(None), mask=None, first=False):
        n_stream = 2 * D_HP
        scores = []
        for hh in range(D_HP):
            k = k_ref[pl.ds(k0, nk), hh * w:(hh + 1) * w]
            for comp in range(2):
                s = _dot_nt(qs[hh][comp][rows], k)
                scores.append(s if mask is None else jnp.where(mask, s, -jnp.inf))
        m_news, alphas, probs = [], [], []
        for i in range(n_stream):
            s = scores[i]
            s_max = jnp.max(s, axis=-1, keepdims=True)
            if first:
                m_new = jnp.broadcast_to(s_max, (s.shape[0], w))
                alphas.append(None)
            else:
                m_old = m_ref[i, rows, :]
                m_new = jnp.maximum(m_old, s_max)
                alphas.append(jnp.exp(m_old - m_new))
            m_news.append(m_new)
            probs.append(jnp.exp(s - jnp.concatenate([m_new] * (s.shape[1] // w), axis=1)).astype(BF16))
        pvs = []
        for hh in range(D_HP):
            vext = jnp.concatenate([v_ref[pl.ds(k0, nk), hh * w:(hh + 1) * w], ones[:nk]], axis=1)
            for comp in range(2):
                pvs.append(_dot(probs[2 * hh + comp], vext))
        for i in range(n_stream):
            if first:
                a_ref[i, rows, :] = pvs[i]
            else:
                a_ref[i, rows, :] = jnp.concatenate([alphas[i]] * 2, axis=1) * a_ref[i, rows, :] + pvs[i]
            m_ref[i, rows, :] = m_news[i]

    hq = tq // 2
    d0 = pl.multiple_of(qi * tq, tq)

    def chunk_mask(q0, nk):
        rq = (lax.broadcasted_iota(I32, (hq, nk), 0) + q0) // CHUNK
        ck = lax.broadcasted_iota(I32, (hq, nk), 1) // CHUNK
        return ck <= rq

    block(d0, hq, rows=slice(0, hq), mask=chunk_mask(0, hq), first=True)
    block(d0, tq, rows=slice(hq, tq), mask=chunk_mask(hq, tq), first=True)

    def body(jj, carry):
        block(pl.multiple_of(jj * tq, tq), tq)
        return carry

    lax.fori_loop(0, qi, body, 0)

    lp = lam_ref[...]
    lam = (jnp.exp(jnp.sum(lp[0:1, :] * lp[1:2, :], axis=-1, keepdims=True))
           - jnp.exp(jnp.sum(lp[2:3, :] * lp[3:4, :], axis=-1, keepdims=True)) + lam_init)
    for hh in range(D_HP):
        a1 = a_ref[2 * hh]
        a2 = a_ref[2 * hh + 1]
        o = a1[:, :w] / a1[:, w:] - lam * (a2[:, :w] / a2[:, w:])
        out_ref[:, hh * w:(hh + 1) * w] = (_rms(o, gain_ref[...]) * (1.0 - lam_init)).astype(BF16)


def _diffattn(z, lam_p, d_gain, B, S, tq, lam_init):
    T = B * S
    nq = S // tq
    w = 2 * D_DH
    wp = D_HP * w
    return pl.pallas_call(
        functools.partial(_diffattn_kernel, tq=tq, lam_init=lam_init),
        grid=(B, D_HEADS // D_HP, nq),
        in_specs=[
            pl.BlockSpec((tq, wp), lambda b, h, i: (b * nq + i, OFF_QD // wp + h)),
            pl.BlockSpec((S, wp), lambda b, h, i: (b, OFF_KD // wp + h)),
            pl.BlockSpec((S, wp), lambda b, h, i: (b, OFF_VD // wp + h)),
            pl.BlockSpec((4, D_DH), lambda b, h, i: (0, 0)),
            pl.BlockSpec((1, w), lambda b, h, i: (0, 0)),
        ],
        out_specs=pl.BlockSpec((tq, wp), lambda b, h, i: (b * nq + i, h)),
        out_shape=jax.ShapeDtypeStruct((T, D_V), BF16),
        scratch_shapes=[
            pltpu.VMEM((2 * D_HP, tq, w), F32), pltpu.VMEM((2 * D_HP, tq, 2 * w), F32),
        ],
        compiler_params=_cparams(("arbitrary", "arbitrary", "arbitrary")),
        name="diffattn",
    )(z, z, z, lam_p, d_gain)


def _merge_kernel(hm_ref, hd_ref, gz_ref, x_ref, wbm_ref, wbd_ref, wout_ref, bg_ref, out_ref):
    bm = _dot(hm_ref[...], wbm_ref[...])
    bd = _dot(hd_ref[...], wbd_ref[...])
    g = _sigmoid(gz_ref[...].astype(F32) + bg_ref[...])
    merged = g[:, :D_MODEL] * bm + g[:, D_MODEL:] * bd
    out_ref[...] = x_ref[...] + _dot(merged.astype(BF16), wout_ref[...])


def _merge(hm, hd, z, x2d, w_bm, w_bd, w_out, b_gate, tm):
    T = x2d.shape[0]
    full = lambda i: (0, 0)
    return pl.pallas_call(
        _merge_kernel,
        grid=(T // tm,),
        in_specs=[
            pl.BlockSpec((tm, M_V), lambda i: (i, 0)),
            pl.BlockSpec((tm, D_V), lambda i: (i, 0)),
            pl.BlockSpec((tm, 2 * D_MODEL), lambda i: (i, OFF_G // (2 * D_MODEL))),
            pl.BlockSpec((tm, D_MODEL), lambda i: (i, 0)),
            pl.BlockSpec((M_V, D_MODEL), full),
            pl.BlockSpec((D_V, D_MODEL), full),
            pl.BlockSpec((D_MODEL, D_MODEL), full),
            pl.BlockSpec((1, 2 * D_MODEL), full),
        ],
        out_specs=pl.BlockSpec((tm, D_MODEL), lambda i: (i, 0)),
        out_shape=jax.ShapeDtypeStruct((T, D_MODEL), F32),
        compiler_params=_cparams(("arbitrary",)),
        name="merge",
    )(hm, hd, z, x2d, w_bm, w_bd, w_out, b_gate)


def _memkv_kernel(mem_ref, g_ref, w_ref, out_ref):
    out_ref[...] = _dot(_rms(mem_ref[...], g_ref[...]).astype(BF16), w_ref[...]).astype(BF16)


def _memkv(mem2d, g, wkv, n_mem):
    R = mem2d.shape[0]
    return pl.pallas_call(
        _memkv_kernel,
        grid=(R // n_mem,),
        in_specs=[
            pl.BlockSpec((n_mem, D_MODEL), lambda i: (i, 0)),
            pl.BlockSpec((1, D_MODEL), lambda i: (0, 0)),
            pl.BlockSpec((D_MODEL, 2 * D_MODEL), lambda i: (0, 0)),
        ],
        out_specs=pl.BlockSpec((n_mem, 2 * D_MODEL), lambda i: (i, 0)),
        out_shape=jax.ShapeDtypeStruct((R, 2 * D_MODEL), BF16),
        compiler_params=_cparams(("arbitrary",)),
        name="memkv",
    )(mem2d, g, wkv)


def _xattn_kernel(x_ref, g_ref, wq_ref, kv_ref, wo_ref, out_ref, o_ref):
    x = x_ref[...]
    h = _rms(x, g_ref[...]).astype(BF16)
    q = (_dot(h, wq_ref[...]) * (X_DH ** -0.5)).astype(BF16)
    scores = [_dot_nt(q[:, hd * X_DH:(hd + 1) * X_DH], kv_ref[:, hd * X_DH:(hd + 1) * X_DH]) for hd in range(X_HEADS)]
    for hd in range(X_HEADS):
        vh = kv_ref[:, D_MODEL + hd * X_DH:D_MODEL + (hd + 1) * X_DH]
        s = scores[hd]
        p = jnp.exp(s - jnp.max(s, axis=-1, keepdims=True))
        p = p / jnp.sum(p, axis=-1, keepdims=True)
        o_ref[:, hd * X_DH:(hd + 1) * X_DH] = _dot(p.astype(BF16), vh).astype(BF16)
    out_ref[...] = x + _dot(o_ref[...], wo_ref[...])


def _xattn(x1, g, wq, kvmem, wo, S, n_mem, tm):
    T = x1.shape[0]
    per_b = S // tm
    full = lambda i: (0, 0)
    return pl.pallas_call(
        _xattn_kernel,
        grid=(T // tm,),
        in_specs=[
            pl.BlockSpec((tm, D_MODEL), lambda i: (i, 0)),
            pl.BlockSpec((1, D_MODEL), full),
            pl.BlockSpec((D_MODEL, D_MODEL), full),
            pl.BlockSpec((n_mem, 2 * D_MODEL), lambda i: (i // per_b, 0)),
            pl.BlockSpec((D_MODEL, D_MODEL), full),
        ],
        out_specs=pl.BlockSpec((tm, D_MODEL), lambda i: (i, 0)),
        out_shape=jax.ShapeDtypeStruct((T, D_MODEL), F32),
        scratch_shapes=[pltpu.VMEM((tm, D_MODEL), BF16)],
        compiler_params=_cparams(("arbitrary",)),
        name="xattn",
    )(x1, g, wq, kvmem, wo)


def _router_kernel(x_ref, g_ref, wrh_ref, wrl_ref, br_ref, hp_ref, ids_ref, tw_ref, cnt_ref):
    @pl.when(pl.program_id(0) == 0)
    def _():
        cnt_ref[...] = jnp.zeros_like(cnt_ref)

    hn = _rms(x_ref[...], g_ref[...])
    hh, hl = _split_bf16(hn)
    hp_ref[...] = hh
    logits = _dot(hh, wrh_ref[...]) + _dot(hh, wrl_ref[...]) + _dot(hl, wrh_ref[...]) + br_ref[...]
    lane = lax.broadcasted_iota(I32, logits.shape, 1)
    lanef = lane.astype(F32)
    ids = jnp.zeros(logits.shape, F32)
    tw = jnp.zeros(logits.shape, F32)
    onehot = jnp.zeros(logits.shape, F32)
    v0 = None
    den = None
    for kk in range(TOP_K):
        mx = jnp.max(logits, axis=-1, keepdims=True)
        idx = jnp.min(jnp.where(logits == mx, lanef, float(LANES)), axis=-1, keepdims=True)
        sel = lanef == idx
        if kk == 0:
            v0 = mx
        e = jnp.exp(mx - v0)
        den = e if den is None else den + e
        ids = jnp.where(lane == kk, idx, ids)
        tw = jnp.where(lane == kk, e, tw)
        onehot = jnp.where(sel, 1.0, onehot)
        logits = jnp.where(sel, -jnp.inf, logits)
    ids_ref[...] = ids.astype(I32)
    tw_ref[...] = tw / den
    cnt_ref[...] += _ceil_rows(jnp.sum(onehot, axis=0, keepdims=True))


def _router(x2, g, wr_hi, wr_lo, b_r, tm):
    T = x2.shape[0]
    full = lambda i: (0, 0)
    return pl.pallas_call(
        _router_kernel,
        grid=(T // tm,),
        in_specs=[
            pl.BlockSpec((tm, D_MODEL), lambda i: (i, 0)),
            pl.BlockSpec((1, D_MODEL), full),
            pl.BlockSpec((D_MODEL, LANES), full),
            pl.BlockSpec((D_MODEL, LANES), full),
            pl.BlockSpec((1, LANES), full),
        ],
        out_specs=[
            pl.BlockSpec((tm, D_MODEL), lambda i: (i, 0)),
            pl.BlockSpec((tm, LANES), lambda i: (i, 0)),
            pl.BlockSpec((tm, LANES), lambda i: (i, 0)),
            pl.BlockSpec((1, LANES), full),
        ],
        out_shape=[
            jax.ShapeDtypeStruct((T, D_MODEL), BF16),
            jax.ShapeDtypeStruct((T, LANES), I32),
            jax.ShapeDtypeStruct((T, LANES), F32),
            jax.ShapeDtypeStruct((1, LANES), F32),
        ],
        compiler_params=_cparams(("arbitrary",)),
        name="router",
    )(x2, g, wr_hi, wr_lo, b_r)


SEG_WORDS = SUBLANES * LANES


def _slots_kernel(ids_ref, start_ref, ls_ref, lst_ref, seg_ref, run_ref):
    @pl.when(pl.program_id(0) == 0)
    def _():
        run_ref[...] = jnp.zeros_like(run_ref)

    ids = ids_ref[...]
    tm = ids.shape[0]
    lane = lax.broadcasted_iota(I32, ids.shape, 1)
    sels = [lane == ids[:, kk:kk + 1] for kk in range(TOP_K)]
    onehot = jnp.zeros(ids.shape, F32)
    for s in sels:
        onehot = jnp.where(s, 1.0, onehot)
    c8 = _ceil_rows(jnp.sum(onehot, axis=0, keepdims=True))
    er = lax.broadcasted_iota(I32, (LANES, LANES), 0)
    ec = lax.broadcasted_iota(I32, (LANES, LANES), 1)
    before = jnp.where(er < ec, 1.0, 0.0).astype(BF16)
    pieces = jnp.broadcast_to(c8 * (1.0 / SUBLANES), (SUBLANES, LANES)).astype(BF16)
    lo = _dot(pieces, before)[0:1, :] * SUBLANES
    r = lax.broadcasted_iota(I32, (tm, tm), 0)
    c = lax.broadcasted_iota(I32, (tm, tm), 1)
    strict = jnp.where(c < r, 1.0, 0.0).astype(BF16)
    slot = _dot(strict, onehot.astype(BF16)) + lo
    ls = jnp.zeros(ids.shape, F32)
    for kk, s in enumerate(sels):
        pk = jnp.sum(jnp.where(s, slot, 0.0), axis=-1, keepdims=True)
        ls = jnp.where(lane == kk, pk, ls)
    ls_ref[...] = ls
    hi = jnp.floor(ls * (1.0 / 32.0))
    rem = ls - 32.0 * hi
    pick = jnp.where(lax.broadcasted_iota(I32, (SUBLANES, LANES), 0) == lax.broadcasted_iota(I32, (SUBLANES, LANES), 1),
                     1.0, 0.0).astype(BF16)
    lst_ref[...] = 32.0 * _dot_nt(pick, hi.astype(BF16)) + _dot_nt(pick, rem.astype(BF16))
    row = lax.broadcasted_iota(I32, (SUBLANES, LANES), 0)
    off = start_ref[...] + run_ref[...]
    total = jnp.sum(c8, axis=-1, keepdims=True)
    seg = jnp.where(row == 0, c8, jnp.where(row == 1, lo, jnp.where(row == 2, off, jnp.where(row == 3, total, 0.0))))
    seg_ref[...] = seg.astype(I32)
    run_ref[...] += c8


def _slots(ids, starts, tm):
    T = ids.shape[0]
    nt = T // tm
    return pl.pallas_call(
        _slots_kernel,
        grid=(nt,),
        in_specs=[
            pl.BlockSpec((tm, LANES), lambda i: (i, 0)),
            pl.BlockSpec((1, LANES), lambda i: (0, 0)),
        ],
        out_specs=[
            pl.BlockSpec((tm, LANES), lambda i: (i, 0)),
            pl.BlockSpec((SUBLANES, tm), lambda i: (0, i)),
            pl.BlockSpec((SUBLANES, LANES), lambda i: (i, 0)),
        ],
        out_shape=[
            jax.ShapeDtypeStruct((T, LANES), F32),
            jax.ShapeDtypeStruct((SUBLANES, T), F32),
            jax.ShapeDtypeStruct((nt * SUBLANES, LANES), I32),
        ],
        scratch_shapes=[pltpu.VMEM((1, LANES), F32)],
        compiler_params=_cparams(("arbitrary",)),
        name="slots",
    )(ids, starts)


def _local_rows(tm):
    need = tm * TOP_K + N_EXPERTS * (SUBLANES - 1)
    return ((need + LANES - 1) // LANES) * LANES


BIG_PIECE = 4 * SUBLANES


def _segment_starts(seg, make_copy):
    def expert(e, carry):
        cnt = seg(e)
        lo = seg(LANES + e)
        off = seg(2 * LANES + e)
        n_big = lax.shift_right_logical(cnt, 5)
        n_small = lax.shift_right_logical(cnt & (BIG_PIECE - 1), 3)

        def big(j, carry2):
            d = j * BIG_PIECE
            make_copy(pl.multiple_of(lo + d, SUBLANES), pl.multiple_of(off + d, SUBLANES), BIG_PIECE).start()
            return carry2

        def small(j, carry2):
            d = n_big * BIG_PIECE + j * SUBLANES
            make_copy(pl.multiple_of(lo + d, SUBLANES), pl.multiple_of(off + d, SUBLANES), SUBLANES).start()
            return carry2

        lax.fori_loop(0, n_big, big, 0)
        lax.fori_loop(0, n_small, small, 0)
        return carry

    for e in range(N_EXPERTS):
        expert(e, 0)


def _segment_waits(total_rows, make_copy):
    def big(j, carry):
        make_copy(0, 0, BIG_PIECE).wait()
        return carry

    def small(j, carry):
        make_copy(0, 0, SUBLANES).wait()
        return carry

    lax.fori_loop(0, lax.shift_right_logical(total_rows, 5), big, 0)
    lax.fori_loop(0, lax.shift_right_logical(total_rows & (BIG_PIECE - 1), 3), small, 0)


def _dispatch_kernel(h_ref, lst_ref, seg_hbm, xs_ref, sbuf_ref, seg_smem, prev_smem, sem_seg, sem_rows, *, tm):
    i = pl.program_id(0)
    n = pl.num_programs(0)
    slot = i % 2
    rows = sbuf_ref.shape[1]
    cp = pltpu.make_async_copy(seg_hbm.at[i], seg_smem, sem_seg)
    cp.start()
    lst = lst_ref[...].astype(I32)
    rid = lax.broadcasted_iota(I32, (rows, tm), 0)
    perm = jnp.zeros((rows, tm), F32)
    for kk in range(TOP_K):
        perm = perm + jnp.where(rid == lst[kk:kk + 1, :], 1.0, 0.0)
    srt = _dot(perm.astype(BF16), h_ref[...])
    sbuf_ref[slot] = _pack_bf16_pairs(srt)
    cp.wait()

    def copy_from(s):
        def make_copy(lo, off, nrows):
            return pltpu.make_async_copy(sbuf_ref.at[s, pl.ds(lo, nrows), :], xs_ref.at[pl.ds(off, nrows), :],
                                         sem_rows.at[s])
        return make_copy

    _segment_starts(lambda k: seg_smem[k], copy_from(slot))

    @pl.when(i > 0)
    def _():
        _segment_waits(prev_smem[0], copy_from(1 - slot))

    prev_smem[0] = seg_smem[3 * LANES]

    @pl.when(i == n - 1)
    def _():
        _segment_waits(prev_smem[0], copy_from(slot))


def _dispatch(h, lst, seg2d, n_rows, tm):
    T = h.shape[0]
    rows = _local_rows(tm)
    return pl.pallas_call(
        functools.partial(_dispatch_kernel, tm=tm),
        grid=(T // tm,),
        in_specs=[
            pl.BlockSpec((tm, D_MODEL), lambda i: (i, 0)),
            pl.BlockSpec((SUBLANES, tm), lambda i: (0, i)),
            pl.BlockSpec(memory_space=pl.ANY),
        ],
        out_specs=pl.BlockSpec(memory_space=pl.ANY),
        out_shape=jax.ShapeDtypeStruct((n_rows, D_MODEL // 2), U32),
        scratch_shapes=[
            pltpu.VMEM((2, rows, D_MODEL // 2), U32),
            pltpu.SMEM((SEG_WORDS,), I32),
            pltpu.SMEM((1,), I32),
            pltpu.SemaphoreType.DMA,
            pltpu.SemaphoreType.DMA((2,)),
        ],
        compiler_params=_cparams(("arbitrary",)),
        name="dispatch",
    )(h, lst, seg2d)


def _experts_kernel(te_ref, nu_ref, nv_ref, xs_ref, wgu_ref, bgu_ref, wdn_ref, bdn_ref, ys_ref, wgu_bf, wdn_bf):
    i = pl.program_id(0)
    half = D_MODEL // 2

    @pl.when((i == 0) | (te_ref[i] != te_ref[jnp.maximum(i - 1, 0)]))
    def _():
        wgu_bf[...] = wgu_ref[0].astype(BF16)
        wdn_bf[...] = wdn_ref[0].astype(BF16)

    @pl.when(i < nu_ref[0])
    def _():
        live = lax.broadcasted_iota(I32, xs_ref.shape, 0) < nv_ref[i]
        lo, hi = _unpack_bf16_pairs(jnp.where(live, xs_ref[...], jnp.uint32(0)))
        gu = (_dot(lo.astype(BF16), wgu_bf[:half, :]) + _dot(hi.astype(BF16), wgu_bf[half:, :])
              + bgu_ref[0])
        gate = jnp.minimum(gu[:, :D_FF], SWIGLU_LIMIT)
        up = jnp.clip(gu[:, D_FF:], -SWIGLU_LIMIT, SWIGLU_LIMIT)
        act = (up + 1.0) * (gate * _sigmoid(SWIGLU_ALPHA * gate))
        y = _dot(act.astype(BF16), wdn_bf[...]) + bdn_ref[0]
        ys_ref[...] = _pack_bf16_pairs(y)

    @pl.when(i >= nu_ref[0])
    def _():
        ys_ref[...] = jnp.zeros_like(ys_ref)


def _experts(tile_expert, n_used, tile_valid, xs, w_gu, b_gu, w_dn, b_dn, tg):
    P = xs.shape[0]
    half = D_MODEL // 2
    grid_spec = pltpu.PrefetchScalarGridSpec(
        num_scalar_prefetch=3,
        grid=(P // tg,),
        in_specs=[
            pl.BlockSpec((tg, half), lambda i, te, nu, nv: (jnp.minimum(i, jnp.maximum(nu[0] - 1, 0)), 0)),
            pl.BlockSpec((1, D_MODEL, 2 * D_FF), lambda i, te, nu, nv: (te[i], 0, 0)),
            pl.BlockSpec((1, 1, 2 * D_FF), lambda i, te, nu, nv: (te[i], 0, 0)),
            pl.BlockSpec((1, D_FF, D_MODEL), lambda i, te, nu, nv: (te[i], 0, 0)),
            pl.BlockSpec((1, 1, D_MODEL), lambda i, te, nu, nv: (te[i], 0, 0)),
        ],
        out_specs=pl.BlockSpec((tg, half), lambda i, te, nu, nv: (i, 0)),
        scratch_shapes=[pltpu.VMEM((D_MODEL, 2 * D_FF), BF16), pltpu.VMEM((D_FF, D_MODEL), BF16)],
    )
    return pl.pallas_call(
        _experts_kernel,
        grid_spec=grid_spec,
        out_shape=jax.ShapeDtypeStruct((P, half), U32),
        compiler_params=_cparams(("arbitrary",)),
        name="experts",
    )(tile_expert, n_used, tile_valid, xs, w_gu, b_gu, w_dn, b_dn)


def _combine_kernel(seg_hbm, ys_hbm, ls_ref, tw_ref, x_ref, g_ref, out_ref, ybuf_ref, seg_smem, sem_seg, sem_rows, *,
                    tm, final_norm):
    i = pl.program_id(0)
    n = pl.num_programs(0)
    slot = i % 2
    rows = ybuf_ref.shape[1]

    def copy_into(s):
        def make_copy(lo, off, nrows):
            return pltpu.make_async_copy(ys_hbm.at[pl.ds(off, nrows), :], ybuf_ref.at[s, pl.ds(lo, nrows), :],
                                         sem_rows.at[s])
        return make_copy

    def request(step, s):
        cp = pltpu.make_async_copy(seg_hbm.at[step], seg_smem.at[s], sem_seg)
        cp.start()
        cp.wait()
        _segment_starts(lambda k: seg_smem[s, k], copy_into(s))

    @pl.when(i == 0)
    def _():
        ybuf_ref[...] = jnp.zeros_like(ybuf_ref)
        request(0, 0)

    @pl.when(i + 1 < n)
    def _():
        request(i + 1, 1 - slot)

    ls = ls_ref[...].astype(I32)
    tw = tw_ref[...]
    cid = lax.broadcasted_iota(I32, (tm, rows), 1)
    wmat = jnp.zeros((tm, rows), F32)
    for kk in range(TOP_K):
        wmat = wmat + jnp.where(cid == ls[:, kk:kk + 1], tw[:, kk:kk + 1], 0.0)
    wmat = wmat.astype(BF16)
    _segment_waits(seg_smem[slot, 3 * LANES], copy_into(slot))
    lo, hi = _unpack_bf16_pairs(ybuf_ref[slot])
    moe = jnp.concatenate([_dot(wmat, lo.astype(BF16)), _dot(wmat, hi.astype(BF16))], axis=1)
    x3 = x_ref[...] + moe
    out_ref[...] = _rms(x3, g_ref[...]) if final_norm else x3


def _combine(seg2d, ys, ls, tw, x2, g, tm, final_norm):
    T = x2.shape[0]
    rows = _local_rows(tm)
    return pl.pallas_call(
        functools.partial(_combine_kernel, tm=tm, final_norm=final_norm),
        grid=(T // tm,),
        in_specs=[
            pl.BlockSpec(memory_space=pl.ANY),
            pl.BlockSpec(memory_space=pl.ANY),
            pl.BlockSpec((tm, LANES), lambda i: (i, 0)),
            pl.BlockSpec((tm, LANES), lambda i: (i, 0)),
            pl.BlockSpec((tm, D_MODEL), lambda i: (i, 0)),
            pl.BlockSpec((1, D_MODEL), lambda i: (0, 0)),
        ],
        out_specs=pl.BlockSpec((tm, D_MODEL), lambda i: (i, 0)),
        out_shape=jax.ShapeDtypeStruct((T, D_MODEL), F32),
        scratch_shapes=[
            pltpu.VMEM((2, rows, D_MODEL // 2), U32),
            pltpu.SMEM((2, SEG_WORDS), I32),
            pltpu.SemaphoreType.DMA,
            pltpu.SemaphoreType.DMA((2,)),
        ],
        compiler_params=_cparams(("arbitrary",)),
        name="combine",
    )(seg2d, ys, ls, tw, x2, g)


def _tiles(B, S):
    T = B * S
    return dict(
        tm_in=min(1024, T), tn_in=2048,
        ts=min(512, S),
        tq=min(1024, S),
        tm_proj=min(512, S),
        tm_route=min(256, T),
        tg=512,
    )


def _pad_lanes(a, n=LANES, value=0.0):
    return jnp.pad(a, ((0, 0), (0, n - a.shape[1])), constant_values=value)


def kernel(x, mem, norm_mix, w_in, conv_w, b_if, mlstm_gain, diff_lambda, diff_gain, w_branch_m, w_branch_d,
           b_gate, w_out, norm_xattn, norm_mem, wq_x, wkv_x, wo_x, norm_ffn, w_router, b_router, w_gu, b_gu,
           w_dn, b_dn, norm_final):
    B, S, D = x.shape
    n_mem = mem.shape[1]
    T = B * S
    depth = norm_mix.shape[0]
    tl = _tiles(B, S)
    x2d = x.reshape(T, D)
    mem2d = mem.reshape(B * n_mem, D)

    for l in range(depth):
        lam_init = 0.8 - 0.6 * math.exp(-0.3 * l)
        wl = w_in[l]
        if_lo = 2 * M_QK + 2 * M_V
        w_main = jnp.concatenate([wl[:, :if_lo], wl[:, if_lo + 2 * M_HEADS:]], axis=1).astype(BF16)
        w_if = wl[:, if_lo:if_lo + 2 * M_HEADS]
        w_ifp = _pad_lanes(w_if).astype(BF16)
        w_ift = w_if.T.astype(BF16)
        bif = _pad_lanes(b_if[l][None, :])
        bift = jnp.broadcast_to(b_if[l][:, None], (SUBLANES, LANES))

        z, zif, zift = _inproj(x2d, norm_mix[l][None, :], w_main, w_ifp, w_ift, tl["tm_in"], tl["tn_in"])
        hm = _mlstm(z, zif, zift, conv_w[l], bif, bift, mlstm_gain[l].reshape(1, M_V), B, S, tl["ts"])
        hd = _diffattn(z, diff_lambda[l], diff_gain[l][None, :], B, S, tl["tq"], lam_init)
        x1 = _merge(hm, hd, z, x2d, w_branch_m[l].astype(BF16), w_branch_d[l].astype(BF16),
                    w_out[l].astype(BF16), b_gate[l][None, :], tl["tm_proj"])

        kvmem = _memkv(mem2d, norm_mem[l][None, :], wkv_x[l].astype(BF16), n_mem)
        x2 = _xattn(x1, norm_xattn[l][None, :], wq_x[l].astype(BF16), kvmem, wo_x[l].astype(BF16),
                    S, n_mem, tl["tm_proj"])

        wr = _pad_lanes(w_router[l])
        wr_hi = wr.astype(BF16)
        wr_lo = (wr - wr_hi.astype(F32)).astype(BF16)
        br = _pad_lanes(b_router[l][None, :], value=-jnp.inf)
        tm_r = tl["tm_route"]
        tg = tl["tg"]
        hp, ids, tw, cnt = _router(x2, norm_ffn[l][None, :], wr_hi, wr_lo, br, tm_r)

        counts = cnt[0, :N_EXPERTS].astype(I32)
        padded = ((counts + tg - 1) // tg) * tg
        ends = jnp.cumsum(padded)
        starts = ends - padded
        max_rows = T * TOP_K + (T // tm_r) * N_EXPERTS * (SUBLANES - 1)
        n_tiles = -(-max_rows // tg) + N_EXPERTS
        tile_row0 = jnp.arange(n_tiles, dtype=I32) * tg
        tile_expert = jnp.minimum(jnp.sum((tile_row0[:, None] >= ends[None, :]).astype(I32), axis=1), N_EXPERTS - 1)
        n_used = (ends[-1] // tg).astype(I32).reshape(1)
        last_used = tile_expert[jnp.maximum(n_used[0] - 1, 0)]
        tile_expert = jnp.where(tile_row0 < ends[-1], tile_expert, last_used)
        tile_valid = jnp.clip((starts + counts)[tile_expert] - tile_row0, 0, tg).astype(I32)

        ls, lst, seg = _slots(ids, _pad_lanes(starts.astype(F32)[None, :]), tm_r)
        seg2d = seg.reshape(T // tm_r, SEG_WORDS)

        xs = _dispatch(hp, lst, seg2d, n_tiles * tg, tm_r)
        ys = _experts(tile_expert, n_used, tile_valid, xs, w_gu[l], b_gu[l][:, None, :], w_dn[l], b_dn[l][:, None, :],
                      tg)
        x2d = _combine(seg2d, ys, ls, tw, x2, norm_final[None, :], tm_r, final_norm=(l == depth - 1))
    return x2d.reshape(B, S, D)
```

```python
import functools
import math

import jax
import jax.numpy as jnp
from jax import lax
from jax.experimental import pallas as pl
from jax.experimental.pallas import tpu as pltpu

F32 = jnp.float32
BF16 = jnp.bfloat16
U32 = jnp.uint32
I32 = jnp.int32

EPS = 1e-6
CHUNK = 64
D_MODEL = 1024
M_HEADS = 4
M_DK = 128
M_DV = 256
M_QK = M_HEADS * M_DK
M_V = M_HEADS * M_DV
CONV_W = 4
D_HEADS = 8
D_DH = 64
D_HP = 2
D_QK = D_HEADS * 2 * D_DH
D_V = D_HEADS * 2 * D_DH
X_HEADS = 4
X_DH = D_MODEL // X_HEADS
N_EXPERTS = 32
TOP_K = 4
D_FF = D_MODEL
SWIGLU_LIMIT = 7.0
SWIGLU_ALPHA = 1.702

LANES = 128
SUBLANES = 8
N_MAIN = 2 * M_QK + 2 * M_V + 2 * D_QK + D_V + 2 * D_MODEL
OFF_QM, OFF_KM, OFF_VM, OFF_OM = 0, M_QK, 2 * M_QK, 2 * M_QK + M_V
OFF_QD = OFF_OM + M_V
OFF_KD = OFF_QD + D_QK
OFF_VD = OFF_KD + D_QK
OFF_G = OFF_VD + D_V

VMEM_LIMIT = 56 * 1024 * 1024


def _cparams(sem, vmem=VMEM_LIMIT):
    return pltpu.CompilerParams(dimension_semantics=sem, vmem_limit_bytes=vmem)


def _rms(x, g):
    return x * lax.rsqrt(jnp.mean(x * x, axis=-1, keepdims=True) + EPS) * g


def _split_bf16(x):
    hi = x.astype(BF16)
    lo = (x - hi.astype(F32)).astype(BF16)
    return hi, lo


def _dot(a, b):
    return jnp.dot(a, b, preferred_element_type=F32)


def _dot_nt(a, b):
    return lax.dot_general(a, b, (((1,), (1,)), ((), ())), preferred_element_type=F32)


def _sigmoid(x):
    return 1.0 / (1.0 + jnp.exp(-x))


def _log_sigmoid(x):
    return jnp.minimum(x, 0.0) - jnp.log(1.0 + jnp.exp(-jnp.abs(x)))


def _pack_bf16_pairs(x):
    w = x.shape[1] // 2
    u = lax.bitcast_convert_type(x, U32)
    r = (u + jnp.uint32(0x7FFF) + ((u >> 16) & jnp.uint32(1))) >> 16
    return r[:, :w] | (r[:, w:] << 16)


def _pack_exact_bf16_pairs(x):
    w = x.shape[1] // 2
    u = lax.bitcast_convert_type(x, U32)
    return (u[:, :w] >> 16) | (u[:, w:] & jnp.uint32(0xFFFF0000))


def _ceil_rows(x):
    return jnp.floor((x + (SUBLANES - 1)) * (1.0 / SUBLANES)) * SUBLANES


def _unpack_bf16_pairs(p):
    lo = lax.bitcast_convert_type(p << 16, F32)
    hi = lax.bitcast_convert_type(p & jnp.uint32(0xFFFF0000), F32)
    return lo, hi


def _inproj_kernel(x_ref, g_ref, w_ref, wif_ref, wift_ref, z_ref, zif_ref, zift_ref, hn_ref):
    @pl.when(pl.program_id(1) == 0)
    def _():
        hn = _rms(x_ref[...], g_ref[...]).astype(BF16)
        hn_ref[...] = hn
        zif_ref[...] = _dot(hn, wif_ref[...])
        zift_ref[...] = _dot_nt(wift_ref[...], hn)

    z_ref[...] = _dot(hn_ref[...], w_ref[...]).astype(BF16)


def _inproj(x2d, g, w_main, w_if, w_ift, tm, tn):
    T = x2d.shape[0]
    return pl.pallas_call(
        _inproj_kernel,
        grid=(T // tm, N_MAIN // tn),
        in_specs=[
            pl.BlockSpec((tm, D_MODEL), lambda i, j: (i, 0)),
            pl.BlockSpec((1, D_MODEL), lambda i, j: (0, 0)),
            pl.BlockSpec((D_MODEL, tn), lambda i, j: (0, j)),
            pl.BlockSpec((D_MODEL, LANES), lambda i, j: (0, 0)),
            pl.BlockSpec((SUBLANES, D_MODEL), lambda i, j: (0, 0)),
        ],
        out_specs=[
            pl.BlockSpec((tm, tn), lambda i, j: (i, j)),
            pl.BlockSpec((tm, LANES), lambda i, j: (i, 0)),
            pl.BlockSpec((SUBLANES, tm), lambda i, j: (0, i)),
        ],
        out_shape=[
            jax.ShapeDtypeStruct((T, N_MAIN), BF16),
            jax.ShapeDtypeStruct((T, LANES), F32),
            jax.ShapeDtypeStruct((SUBLANES, T), F32),
        ],
        scratch_shapes=[pltpu.VMEM((tm, D_MODEL), BF16)],
        compiler_params=_cparams(("arbitrary", "arbitrary")),
        name="inproj",
    )(x2d, g, w_main, w_if, w_ift)


def _mlstm_kernel(q_ref, k_ref, v_ref, om_ref, zif_ref, zift_ref, cw_ref, bif_ref, bift_ref, mg_ref,
                  out_ref, qc_ref, kc_ref, kt_ref, carry_ref, c_ref, n_ref, m_ref,
                  bd_ref, bdt_ref, brep_ref, grow_ref, brow_ref, *, ts):
    nchunk = ts // CHUNK
    L = CHUNK

    @pl.when(pl.program_id(1) == 0)
    def _():
        carry_ref[...] = jnp.zeros_like(carry_ref)
        c_ref[...] = jnp.zeros_like(c_ref)
        n_ref[...] = jnp.zeros_like(n_ref)
        m_ref[...] = jnp.zeros_like(m_ref)
        rt = lax.broadcasted_iota(I32, (ts, ts), 0)
        ct = lax.broadcasted_iota(I32, (ts, ts), 1)
        same = (rt // L) == (ct // L)
        bd_ref[...] = jnp.where(same, jnp.where(ct <= rt, 1.0, 0.0), 0.0).astype(BF16)
        bdt_ref[...] = jnp.where(same, jnp.where(rt <= ct, 1.0, 0.0), 0.0).astype(BF16)

    row8 = lax.broadcasted_iota(I32, (SUBLANES, M_QK), 0)

    def conv_silu(x, prev8, w):
        acc = w[CONV_W - 1:CONV_W, :] * x
        for s in range(1, CONV_W):
            xs = pltpu.roll(x, s, 0)
            top = jnp.where(row8 < s, pltpu.roll(prev8, s, 0), xs[0:SUBLANES])
            xs = jnp.concatenate([top, xs[SUBLANES:]], axis=0)
            acc = acc + w[CONV_W - 1 - s:CONV_W - s, :] * xs
        return acc * _sigmoid(acc)

    def conv_body(c, carry):
        r0 = pl.multiple_of(c * L, L)
        xq = q_ref[pl.ds(r0, L), :].astype(F32)
        xk = k_ref[pl.ds(r0, L), :].astype(F32)
        yq = conv_silu(xq, carry_ref[:, 0:M_QK], cw_ref[:, 0:M_QK]) * (M_DK ** -0.5)
        yk = conv_silu(xk, carry_ref[:, M_QK:2 * M_QK], cw_ref[:, M_QK:2 * M_QK])
        qc_ref[pl.ds(r0, L), :] = yq.astype(BF16)
        kc_ref[pl.ds(r0, L), :] = yk.astype(BF16)
        for h in range(M_HEADS):
            kt_ref[c, h] = yk[:, h * M_DK:(h + 1) * M_DK].T
        carry_ref[:, 0:M_QK] = xq[L - SUBLANES:L]
        carry_ref[:, M_QK:2 * M_QK] = xk[L - SUBLANES:L]
        return carry

    lax.fori_loop(0, nchunk, conv_body, 0, unroll=2)

    ti = lax.broadcasted_iota(I32, (L, L), 0)
    si = lax.broadcasted_iota(I32, (L, L), 1)
    causal = si <= ti
    lane_row = lax.broadcasted_iota(I32, (LANES, LANES), 0)
    ones_l = jnp.ones((L, LANES), BF16)

    lf_col = _log_sigmoid(zif_ref[...] + bif_ref[...])
    ch, cl = _split_bf16(lf_col)
    b_col_all = _dot(bd_ref[...], ch) + _dot(bd_ref[...], cl)
    bh, bl = _split_bf16(b_col_all)
    for h in range(M_HEADS):
        sel_f = jnp.where(lane_row == M_HEADS + h, 1.0, 0.0).astype(BF16)
        brep_ref[h] = _dot(bh, sel_f) + _dot(bl, sel_f)
    g_row_all = zift_ref[...] + bift_ref[:, 0:1]
    rh, rl = _split_bf16(_log_sigmoid(g_row_all))
    b_row_tile = _dot(rh, bdt_ref[...]) + _dot(rl, bdt_ref[...])
    for cc in range(nchunk):
        grow_ref[cc] = g_row_all[:, cc * L:(cc + 1) * L]
        brow_ref[cc] = b_row_tile[:, cc * L:(cc + 1) * L]

    def chunk_body(c, carry):
        r0 = pl.multiple_of(c * L, L)
        g_row = grow_ref[c]
        b_row_all = brow_ref[c]
        early = []
        for h in range(M_HEADS):
            b_rep = brep_ref[h, pl.ds(r0, L), :]
            i_row = g_row[h:h + 1, :]
            b_row = b_row_all[M_HEADS + h:M_HEADS + h + 1, :]
            b_last = b_rep[L - 1:L, :]
            q = qc_ref[pl.ds(r0, L), h * M_DK:(h + 1) * M_DK]
            k = kc_ref[pl.ds(r0, L), h * M_DK:(h + 1) * M_DK]
            vext = jnp.concatenate([v_ref[pl.ds(r0, L), h * M_DV:(h + 1) * M_DV], ones_l], axis=1)
            dm = jnp.where(causal, b_rep[:, :L] - b_row + i_row, -jnp.inf)
            m_loc = jnp.max(dm, axis=-1, keepdims=True)
            qk = _dot_nt(q, k)
            gk_row = b_last[:, :L] - b_row + i_row
            g_max = jnp.max(gk_row, axis=-1, keepdims=True)
            kwt = (kt_ref[c, h] * jnp.exp(gk_row - g_max)).astype(BF16)
            kv = _dot(kwt, vext)
            c_old = c_ref[h]
            n_old = n_ref[h]
            qcn = _dot(q, jnp.concatenate([c_old, n_old], axis=1).astype(BF16))
            early.append((b_rep, b_last, vext, dm, m_loc, qk, g_max, kv, c_old, n_old, qcn))
        pvs = []
        for h in range(M_HEADS):
            b_rep, b_last, vext, dm, m_loc, qk, g_max, kv, c_old, n_old, qcn = early[h]
            s_loc = qk * jnp.exp(dm - m_loc)
            pvs.append(_dot(s_loc.astype(BF16), vext))
        for h in range(M_HEADS):
            b_rep, b_last, vext, dm, m_loc, qk, g_max, kv, c_old, n_old, qcn = early[h]
            pv = pvs[h]
            m_prev = m_ref[h:h + 1, :]
            inter = b_rep + m_prev
            m_t = jnp.maximum(inter, m_loc)
            w_inter = jnp.exp(inter - m_t)
            r_loc = jnp.exp(m_loc - m_t)
            den = r_loc * pv[:, M_DV:] + w_inter * qcn[:, M_DV:]
            inv = 1.0 / jnp.maximum(jnp.abs(den), jnp.exp(-m_t))
            hv = (jnp.concatenate([r_loc * inv] * 2, axis=1) * pv[:, :M_DV]
                  + jnp.concatenate([w_inter * inv] * 2, axis=1) * qcn[:, :M_DV])
            m_new = jnp.maximum(b_last + m_prev, g_max)
            decay = jnp.exp(b_last + m_prev - m_new)
            sc_loc = jnp.exp(g_max - m_new)
            c_ref[h] = (jnp.concatenate([decay] * 2, axis=1) * c_old
                        + jnp.concatenate([sc_loc] * 2, axis=1) * kv[:, :M_DV])
            n_ref[h] = decay * n_old + sc_loc * kv[:, M_DV:]
            m_ref[h:h + 1, :] = m_new
            hn = _rms(hv, mg_ref[:, h * M_DV:(h + 1) * M_DV])
            og = _sigmoid(om_ref[pl.ds(r0, L), h * M_DV:(h + 1) * M_DV].astype(F32))
            out_ref[pl.ds(r0, L), h * M_DV:(h + 1) * M_DV] = (og * hn).astype(BF16)
        return carry

    lax.fori_loop(0, nchunk, chunk_body, 0, unroll=4)


def _mlstm(z, zif, zift, conv_w, bif, bift, m_gain, B, S, ts):
    T = B * S
    nt = S // ts
    nck = ts // CHUNK
    row = lambda b, t: b * nt + t
    return pl.pallas_call(
        functools.partial(_mlstm_kernel, ts=ts),
        grid=(B, nt),
        in_specs=[
            pl.BlockSpec((ts, M_QK), lambda b, t: (row(b, t), OFF_QM // M_QK)),
            pl.BlockSpec((ts, M_QK), lambda b, t: (row(b, t), OFF_KM // M_QK)),
            pl.BlockSpec((ts, M_V), lambda b, t: (row(b, t), OFF_VM // M_V)),
            pl.BlockSpec((ts, M_V), lambda b, t: (row(b, t), OFF_OM // M_V)),
            pl.BlockSpec((ts, LANES), lambda b, t: (row(b, t), 0)),
            pl.BlockSpec((SUBLANES, ts), lambda b, t: (0, row(b, t))),
            pl.BlockSpec((CONV_W, 2 * M_QK), lambda b, t: (0, 0)),
            pl.BlockSpec((1, LANES), lambda b, t: (0, 0)),
            pl.BlockSpec((SUBLANES, LANES), lambda b, t: (0, 0)),
            pl.BlockSpec((1, M_V), lambda b, t: (0, 0)),
        ],
        out_specs=pl.BlockSpec((ts, M_V), lambda b, t: (row(b, t), 0)),
        out_shape=jax.ShapeDtypeStruct((T, M_V), BF16),
        scratch_shapes=[
            pltpu.VMEM((ts, M_QK), BF16),
            pltpu.VMEM((ts, M_QK), BF16),
            pltpu.VMEM((nck, M_HEADS, M_DK, CHUNK), F32),
            pltpu.VMEM((SUBLANES, 2 * M_QK), F32),
            pltpu.VMEM((M_HEADS, M_DK, M_DV), F32),
            pltpu.VMEM((M_HEADS, M_DK, LANES), F32),
            pltpu.VMEM((SUBLANES, LANES), F32),
            pltpu.VMEM((ts, ts), BF16),
            pltpu.VMEM((ts, ts), BF16),
            pltpu.VMEM((M_HEADS, ts, LANES), F32),
            pltpu.VMEM((nck, SUBLANES, CHUNK), F32),
            pltpu.VMEM((nck, SUBLANES, CHUNK), F32),
        ],
        compiler_params=_cparams(("arbitrary", "arbitrary")),
        name="mlstm",
    )(z, z, z, z, zif, zift, conv_w, bif, bift, m_gain)


def _diffattn_kernel(q_ref, k_ref, v_ref, lam_ref, gain_ref, out_ref, m_ref, a_ref, *, tq, lam_init):
    qi = pl.program_id(2)
    w = 2 * D_DH
    lane = lax.broadcasted_iota(I32, (1, w), 1)
    scale = jnp.asarray(D_DH ** -0.5, BF16)
    qs = []
    for hh in range(D_HP):
        q = q_ref[:, hh * w:(hh + 1) * w]
        qs.append((jnp.where(lane < D_DH, q, jnp.zeros_like(q)) * scale,
                   jnp.where(lane >= D_DH, q, jnp.zeros_like(q)) * scale))
    ones = jnp.ones((tq, w), BF16)

    def block(k0, nk, rows=slice(None), mask=None, first=False):
        n_stream = 2 * D_HP
        scores = []
        for hh in range(D_HP):
            k = k_ref[pl.ds(k0, nk), hh * w:(hh + 1) * w]
            for comp in range(2):
                s = _dot_nt(qs[hh][comp][rows], k)
                scores.append(s if mask is None else jnp.where(mask, s, -jnp.inf))
        m_news, alphas, probs = [], [], []
        for i in range(n_stream):
            s = scores[i]
            s_max = jnp.max(s, axis=-1, keepdims=True)
            if first:
                m_new = jnp.broadcast_to(s_max, (s.shape[0], w))
                alphas.append(None)
            else:
                m_old = m_ref[i, rows, :]
                m_new = jnp.maximum(m_old, s_max)
                alphas.append(jnp.exp(m_old - m_new))
            m_news.append(m_new)
            probs.append(jnp.exp(s - jnp.concatenate([m_new] * (s.shape[1] // w), axis=1)).astype(BF16))
        pvs = []
        for hh in range(D_HP):
            vext = jnp.concatenate([v_ref[pl.ds(k0, nk), hh * w:(hh + 1) * w], ones[:nk]], axis=1)
            for comp in range(2):
                pvs.append(_dot(probs[2 * hh + comp], vext))
        for i in range(n_stream):
            if first:
                a_ref[i, rows, :] = pvs[i]
            else:
                a_ref[i, rows, :] = jnp.concatenate([alphas[i]] * 2, axis=1) * a_ref[i, rows, :] + pvs[i]
            m_ref[i, rows, :] = m_news[i]

    hq = tq // 2
    d0 = pl.multiple_of(qi * tq, tq)

    def chunk_mask(q0, nk):
        rq = (lax.broadcasted_iota(I32, (hq, nk), 0) + q0) // CHUNK
        ck = lax.broadcasted_iota(I32, (hq, nk), 1) // CHUNK
        return ck <= rq

    block(d0, hq, rows=slice(0, hq), mask=chunk_mask(0, hq), first=True)
    block(d0, tq, rows=slice(hq, tq), mask=chunk_mask(hq, tq), first=True)

    def body(jj, carry):
        block(pl.multiple_of(jj * tq, tq), tq)
        return carry

    lax.fori_loop(0, qi, body, 0)

    lp = lam_ref[...]
    lam = (jnp.exp(jnp.sum(lp[0:1, :] * lp[1:2, :], axis=-1, keepdims=True))
           - jnp.exp(jnp.sum(lp[2:3, :] * lp[3:4, :], axis=-1, keepdims=True)) + lam_init)
    for hh in range(D_HP):
        a1 = a_ref[2 * hh]
        a2 = a_ref[2 * hh + 1]
        o = a1[:, :w] / a1[:, w:] - lam * (a2[:, :w] / a2[:, w:])
        out_ref[:, hh * w:(hh + 1) * w] = (_rms(o, gain_ref[...]) * (1.0 - lam_init)).astype(BF16)


def _diffattn(z, lam_p, d_gain, B, S, tq, lam_init):
    T = B * S
    nq = S // tq
    w = 2 * D_DH
    wp = D_HP * w
    return pl.pallas_call(
        functools.partial(_diffattn_kernel, tq=tq, lam_init=lam_init),
        grid=(B, D_HEADS // D_HP, nq),
        in_specs=[
            pl.BlockSpec((tq, wp), lambda b, h, i: (b * nq + i, OFF_QD // wp + h)),
            pl.BlockSpec((S, wp), lambda b, h, i: (b, OFF_KD // wp + h)),
            pl.BlockSpec((S, wp), lambda b, h, i: (b, OFF_VD // wp + h)),
            pl.BlockSpec((4, D_DH), lambda b, h, i: (0, 0)),
            pl.BlockSpec((1, w), lambda b, h, i: (0, 0)),
        ],
        out_specs=pl.BlockSpec((tq, wp), lambda b, h, i: (b * nq + i, h)),
        out_shape=jax.ShapeDtypeStruct((T, D_V), BF16),
        scratch_shapes=[
            pltpu.VMEM((2 * D_HP, tq, w), F32), pltpu.VMEM((2 * D_HP, tq, 2 * w), F32),
        ],
        compiler_params=_cparams(("arbitrary", "arbitrary", "arbitrary")),
        name="diffattn",
    )(z, z, z, lam_p, d_gain)


def _merge_kernel(hm_ref, hd_ref, gz_ref, x_ref, wbm_ref, wbd_ref, wout_ref, bg_ref, out_ref):
    bm = _dot(hm_ref[...], wbm_ref[...])
    bd = _dot(hd_ref[...], wbd_ref[...])
    g = _sigmoid(gz_ref[...].astype(F32) + bg_ref[...])
    merged = g[:, :D_MODEL] * bm + g[:, D_MODEL:] * bd
    out_ref[...] = x_ref[...] + _dot(merged.astype(BF16), wout_ref[...])


def _merge(hm, hd, z, x2d, w_bm, w_bd, w_out, b_gate, tm):
    T = x2d.shape[0]
    full = lambda i: (0, 0)
    return pl.pallas_call(
        _merge_kernel,
        grid=(T // tm,),
        in_specs=[
            pl.BlockSpec((tm, M_V), lambda i: (i, 0)),
            pl.BlockSpec((tm, D_V), lambda i: (i, 0)),
            pl.BlockSpec((tm, 2 * D_MODEL), lambda i: (i, OFF_G // (2 * D_MODEL))),
            pl.BlockSpec((tm, D_MODEL), lambda i: (i, 0)),
            pl.BlockSpec((M_V, D_MODEL), full),
            pl.BlockSpec((D_V, D_MODEL), full),
            pl.BlockSpec((D_MODEL, D_MODEL), full),
            pl.BlockSpec((1, 2 * D_MODEL), full),
        ],
        out_specs=pl.BlockSpec((tm, D_MODEL), lambda i: (i, 0)),
        out_shape=jax.ShapeDtypeStruct((T, D_MODEL), F32),
        compiler_params=_cparams(("arbitrary",)),
        name="merge",
    )(hm, hd, z, x2d, w_bm, w_bd, w_out, b_gate)


def _memkv_kernel(mem_ref, g_ref, w_ref, out_ref):
    out_ref[...] = _dot(_rms(mem_ref[...], g_ref[...]).astype(BF16), w_ref[...]).astype(BF16)


def _memkv(mem2d, g, wkv, n_mem):
    R = mem2d.shape[0]
    return pl.pallas_call(
        _memkv_kernel,
        grid=(R // n_mem,),
        in_specs=[
            pl.BlockSpec((n_mem, D_MODEL), lambda i: (i, 0)),
            pl.BlockSpec((1, D_MODEL), lambda i: (0, 0)),
            pl.BlockSpec((D_MODEL, 2 * D_MODEL), lambda i: (0, 0)),
        ],
        out_specs=pl.BlockSpec((n_mem, 2 * D_MODEL), lambda i: (i, 0)),
        out_shape=jax.ShapeDtypeStruct((R, 2 * D_MODEL), BF16),
        compiler_params=_cparams(("arbitrary",)),
        name="memkv",
    )(mem2d, g, wkv)


def _xattn_kernel(x_ref, g_ref, wq_ref, kv_ref, wo_ref, out_ref, o_ref):
    x = x_ref[...]
    h = _rms(x, g_ref[...]).astype(BF16)
    q = (_dot(h, wq_ref[...]) * (X_DH ** -0.5)).astype(BF16)
    scores = [_dot_nt(q[:, hd * X_DH:(hd + 1) * X_DH], kv_ref[:, hd * X_DH:(hd + 1) * X_DH]) for hd in range(X_HEADS)]
    for hd in range(X_HEADS):
        vh = kv_ref[:, D_MODEL + hd * X_DH:D_MODEL + (hd + 1) * X_DH]
        s = scores[hd]
        p = jnp.exp(s - jnp.max(s, axis=-1, keepdims=True))
        p = p / jnp.sum(p, axis=-1, keepdims=True)
        o_ref[:, hd * X_DH:(hd + 1) * X_DH] = _dot(p.astype(BF16), vh).astype(BF16)
    out_ref[...] = x + _dot(o_ref[...], wo_ref[...])


def _xattn(x1, g, wq, kvmem, wo, S, n_mem, tm):
    T = x1.shape[0]
    per_b = S // tm
    full = lambda i: (0, 0)
    return pl.pallas_call(
        _xattn_kernel,
        grid=(T // tm,),
        in_specs=[
            pl.BlockSpec((tm, D_MODEL), lambda i: (i, 0)),
            pl.BlockSpec((1, D_MODEL), full),
            pl.BlockSpec((D_MODEL, D_MODEL), full),
            pl.BlockSpec((n_mem, 2 * D_MODEL), lambda i: (i // per_b, 0)),
            pl.BlockSpec((D_MODEL, D_MODEL), full),
        ],
        out_specs=pl.BlockSpec((tm, D_MODEL), lambda i: (i, 0)),
        out_shape=jax.ShapeDtypeStruct((T, D_MODEL), F32),
        scratch_shapes=[pltpu.VMEM((tm, D_MODEL), BF16)],
        compiler_params=_cparams(("arbitrary",)),
        name="xattn",
    )(x1, g, wq, kvmem, wo)


def _router_kernel(x_ref, g_ref, wrh_ref, wrl_ref, br_ref, hp_ref, ids_ref, tw_ref, cnt_ref):
    @pl.when(pl.program_id(0) == 0)
    def _():
        cnt_ref[...] = jnp.zeros_like(cnt_ref)

    hn = _rms(x_ref[...], g_ref[...])
    hh, hl = _split_bf16(hn)
    hp_ref[...] = hh
    logits = _dot(hh, wrh_ref[...]) + _dot(hh, wrl_ref[...]) + _dot(hl, wrh_ref[...]) + br_ref[...]
    lane = lax.broadcasted_iota(I32, logits.shape, 1)
    lanef = lane.astype(F32)
    ids = jnp.zeros(logits.shape, F32)
    tw = jnp.zeros(logits.shape, F32)
    onehot = jnp.zeros(logits.shape, F32)
    v0 = None
    den = None
    for kk in range(TOP_K):
        mx = jnp.max(logits, axis=-1, keepdims=True)
        idx = jnp.min(jnp.where(logits == mx, lanef, float(LANES)), axis=-1, keepdims=True)
        sel = lanef == idx
        if kk == 0:
            v0 = mx
        e = jnp.exp(mx - v0)
        den = e if den is None else den + e
        ids = jnp.where(lane == kk, idx, ids)
        tw = jnp.where(lane == kk, e, tw)
        onehot = jnp.where(sel, 1.0, onehot)
        logits = jnp.where(sel, -jnp.inf, logits)
    ids_ref[...] = ids.astype(I32)
    tw_ref[...] = tw / den
    cnt_ref[...] += _ceil_rows(jnp.sum(onehot, axis=0, keepdims=True))


def _router(x2, g, wr_hi, wr_lo, b_r, tm):
    T = x2.shape[0]
    full = lambda i: (0, 0)
    return pl.pallas_call(
        _router_kernel,
        grid=(T // tm,),
        in_specs=[
            pl.BlockSpec((tm, D_MODEL), lambda i: (i, 0)),
            pl.BlockSpec((1, D_MODEL), full),
            pl.BlockSpec((D_MODEL, LANES), full),
            pl.BlockSpec((D_MODEL, LANES), full),
            pl.BlockSpec((1, LANES), full),
        ],
        out_specs=[
            pl.BlockSpec((tm, D_MODEL), lambda i: (i, 0)),
            pl.BlockSpec((tm, LANES), lambda i: (i, 0)),
            pl.BlockSpec((tm, LANES), lambda i: (i, 0)),
            pl.BlockSpec((1, LANES), full),
        ],
        out_shape=[
            jax.ShapeDtypeStruct((T, D_MODEL), BF16),
            jax.ShapeDtypeStruct((T, LANES), I32),
            jax.ShapeDtypeStruct((T, LANES), F32),
            jax.ShapeDtypeStruct((1, LANES), F32),
        ],
        compiler_params=_cparams(("arbitrary",)),
        name="router",
    )(x2, g, wr_hi, wr_lo, b_r)


SEG_WORDS = SUBLANES * LANES


def _slots_kernel(ids_ref, start_ref, ls_ref, lst_ref, seg_ref, run_ref):
    @pl.when(pl.program_id(0) == 0)
    def _():
        run_ref[...] = jnp.zeros_like(run_ref)

    ids = ids_ref[...]
    tm = ids.shape[0]
    lane = lax.broadcasted_iota(I32, ids.shape, 1)
    sels = [lane == ids[:, kk:kk + 1] for kk in range(TOP_K)]
    onehot = jnp.zeros(ids.shape, F32)
    for s in sels:
        onehot = jnp.where(s, 1.0, onehot)
    c8 = _ceil_rows(jnp.sum(onehot, axis=0, keepdims=True))
    er = lax.broadcasted_iota(I32, (LANES, LANES), 0)
    ec = lax.broadcasted_iota(I32, (LANES, LANES), 1)
    before = jnp.where(er < ec, 1.0, 0.0).astype(BF16)
    pieces = jnp.broadcast_to(c8 * (1.0 / SUBLANES), (SUBLANES, LANES)).astype(BF16)
    lo = _dot(pieces, before)[0:1, :] * SUBLANES
    r = lax.broadcasted_iota(I32, (tm, tm), 0)
    c = lax.broadcasted_iota(I32, (tm, tm), 1)
    strict = jnp.where(c < r, 1.0, 0.0).astype(BF16)
    slot = _dot(strict, onehot.astype(BF16)) + lo
    ls = jnp.zeros(ids.shape, F32)
    for kk, s in enumerate(sels):
        pk = jnp.sum(jnp.where(s, slot, 0.0), axis=-1, keepdims=True)
        ls = jnp.where(lane == kk, pk, ls)
    ls_ref[...] = ls
    hi = jnp.floor(ls * (1.0 / 32.0))
    rem = ls - 32.0 * hi
    pick = jnp.where(lax.broadcasted_iota(I32, (SUBLANES, LANES), 0) == lax.broadcasted_iota(I32, (SUBLANES, LANES), 1),
                     1.0, 0.0).astype(BF16)
    lst_ref[...] = 32.0 * _dot_nt(pick, hi.astype(BF16)) + _dot_nt(pick, rem.astype(BF16))
    row = lax.broadcasted_iota(I32, (SUBLANES, LANES), 0)
    off = start_ref[...] + run_ref[...]
    total = jnp.sum(c8, axis=-1, keepdims=True)
    seg = jnp.where(row == 0, c8, jnp.where(row == 1, lo, jnp.where(row == 2, off, jnp.where(row == 3, total, 0.0))))
    seg_ref[...] = seg.astype(I32)
    run_ref[...] += c8


def _slots(ids, starts, tm):
    T = ids.shape[0]
    nt = T // tm
    return pl.pallas_call(
        _slots_kernel,
        grid=(nt,),
        in_specs=[
            pl.BlockSpec((tm, LANES), lambda i: (i, 0)),
            pl.BlockSpec((1, LANES), lambda i: (0, 0)),
        ],
        out_specs=[
            pl.BlockSpec((tm, LANES), lambda i: (i, 0)),
            pl.BlockSpec((SUBLANES, tm), lambda i: (0, i)),
            pl.BlockSpec((SUBLANES, LANES), lambda i: (i, 0)),
        ],
        out_shape=[
            jax.ShapeDtypeStruct((T, LANES), F32),
            jax.ShapeDtypeStruct((SUBLANES, T), F32),
            jax.ShapeDtypeStruct((nt * SUBLANES, LANES), I32),
        ],
        scratch_shapes=[pltpu.VMEM((1, LANES), F32)],
        compiler_params=_cparams(("arbitrary",)),
        name="slots",
    )(ids, starts)


def _local_rows(tm):
    need = tm * TOP_K + N_EXPERTS * (SUBLANES - 1)
    return ((need + LANES - 1) // LANES) * LANES


BIG_PIECE = 4 * SUBLANES


def _segment_starts(seg, make_copy):
    def expert(e, carry):
        cnt = seg(e)
        lo = seg(LANES + e)
        off = seg(2 * LANES + e)
        n_big = lax.shift_right_logical(cnt, 5)
        n_small = lax.shift_right_logical(cnt & (BIG_PIECE - 1), 3)

        def big(j, carry2):
            d = j * BIG_PIECE
            make_copy(pl.multiple_of(lo + d, SUBLANES), pl.multiple_of(off + d, SUBLANES), BIG_PIECE).start()
            return carry2

        def small(j, carry2):
            d = n_big * BIG_PIECE + j * SUBLANES
            make_copy(pl.multiple_of(lo + d, SUBLANES), pl.multiple_of(off + d, SUBLANES), SUBLANES).start()
            return carry2

        lax.fori_loop(0, n_big, big, 0)
        lax.fori_loop(0, n_small, small, 0)
        return carry

    for e in range(N_EXPERTS):
        expert(e, 0)


def _segment_waits(total_rows, make_copy):
    def big(j, carry):
        make_copy(0, 0, BIG_PIECE).wait()
        return carry

    def small(j, carry):
        make_copy(0, 0, SUBLANES).wait()
        return carry

    lax.fori_loop(0, lax.shift_right_logical(total_rows, 5), big, 0)
    lax.fori_loop(0, lax.shift_right_logical(total_rows & (BIG_PIECE - 1), 3), small, 0)


def _dispatch_kernel(h_ref, lst_ref, seg_hbm, xs_ref, sbuf_ref, seg_smem, prev_smem, sem_seg, sem_rows, *, tm):
    i = pl.program_id(0)
    n = pl.num_programs(0)
    slot = i % 2
    rows = sbuf_ref.shape[1]
    cp = pltpu.make_async_copy(seg_hbm.at[i], seg_smem, sem_seg)
    cp.start()
    lst = lst_ref[...].astype(I32)
    rid = lax.broadcasted_iota(I32, (rows, tm), 0)
    perm = jnp.zeros((rows, tm), F32)
    for kk in range(TOP_K):
        perm = jnp.where(rid == lst[kk:kk + 1, :], 1.0, perm)
    srt = _dot(perm.astype(BF16), h_ref[...])
    sbuf_ref[slot] = _pack_exact_bf16_pairs(srt)
    cp.wait()

    def copy_from(s):
        def make_copy(lo, off, nrows):
            return pltpu.make_async_copy(sbuf_ref.at[s, pl.ds(lo, nrows), :], xs_ref.at[pl.ds(off, nrows), :],
                                         sem_rows.at[s])
        return make_copy

    _segment_starts(lambda k: seg_smem[k], copy_from(slot))

    @pl.when(i > 0)
    def _():
        _segment_waits(prev_smem[0], copy_from(1 - slot))

    prev_smem[0] = seg_smem[3 * LANES]

    @pl.when(i == n - 1)
    def _():
        _segment_waits(prev_smem[0], copy_from(slot))


def _dispatch(h, lst, seg2d, n_rows, tm):
    T = h.shape[0]
    rows = _local_rows(tm)
    return pl.pallas_call(
        functools.partial(_dispatch_kernel, tm=tm),
        grid=(T // tm,),
        in_specs=[
            pl.BlockSpec((tm, D_MODEL), lambda i: (i, 0)),
            pl.BlockSpec((SUBLANES, tm), lambda i: (0, i)),
            pl.BlockSpec(memory_space=pl.ANY),
        ],
        out_specs=pl.BlockSpec(memory_space=pl.ANY),
        out_shape=jax.ShapeDtypeStruct((n_rows, D_MODEL // 2), U32),
        scratch_shapes=[
            pltpu.VMEM((2, rows, D_MODEL // 2), U32),
            pltpu.SMEM((SEG_WORDS,), I32),
            pltpu.SMEM((1,), I32),
            pltpu.SemaphoreType.DMA,
            pltpu.SemaphoreType.DMA((2,)),
        ],
        compiler_params=_cparams(("arbitrary",)),
        name="dispatch",
    )(h, lst, seg2d)


def _experts_kernel(te_ref, nu_ref, nv_ref, xs_ref, wgu_ref, bgu_ref, wdn_ref, bdn_ref, ys_ref, wgu_bf, wdn_bf):
    i = pl.program_id(0)
    half = D_MODEL // 2

    @pl.when((i == 0) | (te_ref[i] != te_ref[jnp.maximum(i - 1, 0)]))
    def _():
        wgu_bf[...] = wgu_ref[0].astype(BF16)
        wdn_bf[...] = wdn_ref[0].astype(BF16)

    @pl.when(i < nu_ref[0])
    def _():
        live = lax.broadcasted_iota(I32, xs_ref.shape, 0) < nv_ref[i]
        lo, hi = _unpack_bf16_pairs(jnp.where(live, xs_ref[...], jnp.uint32(0)))
        gu = (_dot(lo.astype(BF16), wgu_bf[:half, :]) + _dot(hi.astype(BF16), wgu_bf[half:, :])
              + bgu_ref[0])
        gate = jnp.minimum(gu[:, :D_FF], SWIGLU_LIMIT)
        up = jnp.clip(gu[:, D_FF:], -SWIGLU_LIMIT, SWIGLU_LIMIT)
        act = (up + 1.0) * (gate * _sigmoid(SWIGLU_ALPHA * gate))
        y = _dot(act.astype(BF16), wdn_bf[...]) + bdn_ref[0]
        ys_ref[...] = _pack_bf16_pairs(y)

    @pl.when(i >= nu_ref[0])
    def _():
        ys_ref[...] = jnp.zeros_like(ys_ref)


def _experts(tile_expert, n_used, tile_valid, xs, w_gu, b_gu, w_dn, b_dn, tg):
    P = xs.shape[0]
    half = D_MODEL // 2
    grid_spec = pltpu.PrefetchScalarGridSpec(
        num_scalar_prefetch=3,
        grid=(P // tg,),
        in_specs=[
            pl.BlockSpec((tg, half), lambda i, te, nu, nv: (jnp.minimum(i, jnp.maximum(nu[0] - 1, 0)), 0)),
            pl.BlockSpec((1, D_MODEL, 2 * D_FF), lambda i, te, nu, nv: (te[i], 0, 0)),
            pl.BlockSpec((1, 1, 2 * D_FF), lambda i, te, nu, nv: (te[i], 0, 0)),
            pl.BlockSpec((1, D_FF, D_MODEL), lambda i, te, nu, nv: (te[i], 0, 0)),
            pl.BlockSpec((1, 1, D_MODEL), lambda i, te, nu, nv: (te[i], 0, 0)),
        ],
        out_specs=pl.BlockSpec((tg, half), lambda i, te, nu, nv: (i, 0)),
        scratch_shapes=[pltpu.VMEM((D_MODEL, 2 * D_FF), BF16), pltpu.VMEM((D_FF, D_MODEL), BF16)],
    )
    return pl.pallas_call(
        _experts_kernel,
        grid_spec=grid_spec,
        out_shape=jax.ShapeDtypeStruct((P, half), U32),
        compiler_params=_cparams(("arbitrary",)),
        name="experts",
    )(tile_expert, n_used, tile_valid, xs, w_gu, b_gu, w_dn, b_dn)


def _combine_kernel(seg_hbm, ys_hbm, ls_ref, tw_ref, x_ref, g_ref, out_ref, ybuf_ref, seg_smem, sem_seg, sem_rows, *,
                    tm, final_norm):
    i = pl.program_id(0)
    n = pl.num_programs(0)
    slot = i % 2
    rows = ybuf_ref.shape[1]

    def copy_into(s):
        def make_copy(lo, off, nrows):
            return pltpu.make_async_copy(ys_hbm.at[pl.ds(off, nrows), :], ybuf_ref.at[s, pl.ds(lo, nrows), :],
                                         sem_rows.at[s])
        return make_copy

    def request(step, s):
        cp = pltpu.make_async_copy(seg_hbm.at[step], seg_smem.at[s], sem_seg)
        cp.start()
        cp.wait()
        _segment_starts(lambda k: seg_smem[s, k], copy_into(s))

    @pl.when(i == 0)
    def _():
        ybuf_ref[...] = jnp.zeros_like(ybuf_ref)
        request(0, 0)

    @pl.when(i + 1 < n)
    def _():
        request(i + 1, 1 - slot)

    ls = ls_ref[...].astype(I32)
    tw = tw_ref[...]
    cid = lax.broadcasted_iota(I32, (tm, rows), 1)
    wmat = jnp.zeros((tm, rows), F32)
    for kk in range(TOP_K):
        wmat = jnp.where(cid == ls[:, kk:kk + 1], tw[:, kk:kk + 1], wmat)
    wmat = wmat.astype(BF16)
    _segment_waits(seg_smem[slot, 3 * LANES], copy_into(slot))
    lo, hi = _unpack_bf16_pairs(ybuf_ref[slot])
    moe = jnp.concatenate([_dot(wmat, lo.astype(BF16)), _dot(wmat, hi.astype(BF16))], axis=1)
    x3 = x_ref[...] + moe
    out_ref[...] = _rms(x3, g_ref[...]) if final_norm else x3


def _combine(seg2d, ys, ls, tw, x2, g, tm, final_norm):
    T = x2.shape[0]
    rows = _local_rows(tm)
    return pl.pallas_call(
        functools.partial(_combine_kernel, tm=tm, final_norm=final_norm),
        grid=(T // tm,),
        in_specs=[
            pl.BlockSpec(memory_space=pl.ANY),
            pl.BlockSpec(memory_space=pl.ANY),
            pl.BlockSpec((tm, LANES), lambda i: (i, 0)),
            pl.BlockSpec((tm, LANES), lambda i: (i, 0)),
            pl.BlockSpec((tm, D_MODEL), lambda i: (i, 0)),
            pl.BlockSpec((1, D_MODEL), lambda i: (0, 0)),
        ],
        out_specs=pl.BlockSpec((tm, D_MODEL), lambda i: (i, 0)),
        out_shape=jax.ShapeDtypeStruct((T, D_MODEL), F32),
        scratch_shapes=[
            pltpu.VMEM((2, rows, D_MODEL // 2), U32),
            pltpu.SMEM((2, SEG_WORDS), I32),
            pltpu.SemaphoreType.DMA,
            pltpu.SemaphoreType.DMA((2,)),
        ],
        compiler_params=_cparams(("arbitrary",)),
        name="combine",
    )(seg2d, ys, ls, tw, x2, g)


def _tiles(B, S):
    T = B * S
    return dict(
        tm_in=min(1024, T), tn_in=2048,
        ts=min(512, S),
        tq=min(1024, S),
        tm_proj=min(512, S),
        tm_route=min(256, T),
        tg=512,
    )


def _pad_lanes(a, n=LANES, value=0.0):
    return jnp.pad(a, ((0, 0), (0, n - a.shape[1])), constant_values=value)


def kernel(x, mem, norm_mix, w_in, conv_w, b_if, mlstm_gain, diff_lambda, diff_gain, w_branch_m, w_branch_d,
           b_gate, w_out, norm_xattn, norm_mem, wq_x, wkv_x, wo_x, norm_ffn, w_router, b_router, w_gu, b_gu,
           w_dn, b_dn, norm_final):
    B, S, D = x.shape
    n_mem = mem.shape[1]
    T = B * S
    depth = norm_mix.shape[0]
    tl = _tiles(B, S)
    x2d = x.reshape(T, D)
    mem2d = mem.reshape(B * n_mem, D)

    for l in range(depth):
        lam_init = 0.8 - 0.6 * math.exp(-0.3 * l)
        wl = w_in[l]
        if_lo = 2 * M_QK + 2 * M_V
        w_main = jnp.concatenate([wl[:, :if_lo], wl[:, if_lo + 2 * M_HEADS:]], axis=1).astype(BF16)
        w_if = wl[:, if_lo:if_lo + 2 * M_HEADS]
        w_ifp = _pad_lanes(w_if).astype(BF16)
        w_ift = w_if.T.astype(BF16)
        bif = _pad_lanes(b_if[l][None, :])
        bift = jnp.broadcast_to(b_if[l][:, None], (SUBLANES, LANES))

        z, zif, zift = _inproj(x2d, norm_mix[l][None, :], w_main, w_ifp, w_ift, tl["tm_in"], tl["tn_in"])
        hm = _mlstm(z, zif, zift, conv_w[l], bif, bift, mlstm_gain[l].reshape(1, M_V), B, S, tl["ts"])
        hd = _diffattn(z, diff_lambda[l], diff_gain[l][None, :], B, S, tl["tq"], lam_init)
        x1 = _merge(hm, hd, z, x2d, w_branch_m[l].astype(BF16), w_branch_d[l].astype(BF16),
                    w_out[l].astype(BF16), b_gate[l][None, :], tl["tm_proj"])

        kvmem = _memkv(mem2d, norm_mem[l][None, :], wkv_x[l].astype(BF16), n_mem)
        x2 = _xattn(x1, norm_xattn[l][None, :], wq_x[l].astype(BF16), kvmem, wo_x[l].astype(BF16),
                    S, n_mem, tl["tm_proj"])

        wr = _pad_lanes(w_router[l])
        wr_hi = wr.astype(BF16)
        wr_lo = (wr - wr_hi.astype(F32)).astype(BF16)
        br = _pad_lanes(b_router[l][None, :], value=-jnp.inf)
        tm_r = tl["tm_route"]
        tg = tl["tg"]
        hp, ids, tw, cnt = _router(x2, norm_ffn[l][None, :], wr_hi, wr_lo, br, tm_r)

        counts = cnt[0, :N_EXPERTS].astype(I32)
        padded = ((counts + tg - 1) // tg) * tg
        ends = jnp.cumsum(padded)
        starts = ends - padded
        max_rows = T * TOP_K + (T // tm_r) * N_EXPERTS * (SUBLANES - 1)
        n_tiles = -(-max_rows // tg) + N_EXPERTS
        tile_row0 = jnp.arange(n_tiles, dtype=I32) * tg
        tile_expert = jnp.minimum(jnp.sum((tile_row0[:, None] >= ends[None, :]).astype(I32), axis=1), N_EXPERTS - 1)
        n_used = (ends[-1] // tg).astype(I32).reshape(1)
        last_used = tile_expert[jnp.maximum(n_used[0] - 1, 0)]
        tile_expert = jnp.where(tile_row0 < ends[-1], tile_expert, last_used)
        tile_valid = jnp.clip((starts + counts)[tile_expert] - tile_row0, 0, tg).astype(I32)

        ls, lst, seg = _slots(ids, _pad_lanes(starts.astype(F32)[None, :]), tm_r)
        seg2d = seg.reshape(T // tm_r, SEG_WORDS)

        xs = _dispatch(hp, lst, seg2d, n_tiles * tg, tm_r)
        ys = _experts(tile_expert, n_used, tile_valid, xs, w_gu[l], b_gu[l][:, None, :], w_dn[l], b_dn[l][:, None, :],
                      tg)
        x2d = _combine(seg2d, ys, ls, tw, x2, norm_final[None, :], tm_r, final_norm=(l == depth - 1))
    return x2d.reshape(B, S, D)
```

```python
import functools
import math

import jax
import jax.numpy as jnp
from jax import lax
from jax.experimental import pallas as pl
from jax.experimental.pallas import tpu as pltpu

F32 = jnp.float32
BF16 = jnp.bfloat16
U32 = jnp.uint32
I32 = jnp.int32

EPS = 1e-6
CHUNK = 64
D_MODEL = 1024
M_HEADS = 4
M_DK = 128
M_DV = 256
M_QK = M_HEADS * M_DK
M_V = M_HEADS * M_DV
CONV_W = 4
D_HEADS = 8
D_DH = 64
D_HP = 2
D_QK = D_HEADS * 2 * D_DH
D_V = D_HEADS * 2 * D_DH
X_HEADS = 4
X_DH = D_MODEL // X_HEADS
N_EXPERTS = 32
TOP_K = 4
D_FF = D_MODEL
SWIGLU_LIMIT = 7.0
SWIGLU_ALPHA = 1.702

LANES = 128
SUBLANES = 8
N_MAIN = 2 * M_QK + 2 * M_V + 2 * D_QK + D_V + 2 * D_MODEL
OFF_QM, OFF_KM, OFF_VM, OFF_OM = 0, M_QK, 2 * M_QK, 2 * M_QK + M_V
OFF_QD = OFF_OM + M_V
OFF_KD = OFF_QD + D_QK
OFF_VD = OFF_KD + D_QK
OFF_G = OFF_VD + D_V

VMEM_LIMIT = 56 * 1024 * 1024


def _cparams(sem, vmem=VMEM_LIMIT):
    return pltpu.CompilerParams(dimension_semantics=sem, vmem_limit_bytes=vmem)


def _rms(x, g):
    return x * lax.rsqrt(jnp.mean(x * x, axis=-1, keepdims=True) + EPS) * g


def _split_bf16(x):
    hi = x.astype(BF16)
    lo = (x - hi.astype(F32)).astype(BF16)
    return hi, lo


def _dot(a, b):
    return jnp.dot(a, b, preferred_element_type=F32)


def _dot_nt(a, b):
    return lax.dot_general(a, b, (((1,), (1,)), ((), ())), preferred_element_type=F32)


def _sigmoid(x):
    return 1.0 / (1.0 + jnp.exp(-x))


def _log_sigmoid(x):
    return jnp.minimum(x, 0.0) - jnp.log(1.0 + jnp.exp(-jnp.abs(x)))


def _pack_bf16_pairs(x):
    w = x.shape[1] // 2
    u = lax.bitcast_convert_type(x, U32)
    r = (u + jnp.uint32(0x7FFF) + ((u >> 16) & jnp.uint32(1))) >> 16
    return r[:, :w] | (r[:, w:] << 16)


def _pack_exact_bf16_pairs(x):
    w = x.shape[1] // 2
    u = lax.bitcast_convert_type(x, U32)
    return (u[:, :w] >> 16) | (u[:, w:] & jnp.uint32(0xFFFF0000))


def _ceil_rows(x):
    return jnp.floor((x + (SUBLANES - 1)) * (1.0 / SUBLANES)) * SUBLANES


def _unpack_bf16_pairs(p):
    lo = lax.bitcast_convert_type(p << 16, F32)
    hi = lax.bitcast_convert_type(p & jnp.uint32(0xFFFF0000), F32)
    return lo, hi


def _inproj_kernel(x_ref, g_ref, w_ref, wif_ref, wift_ref, z_ref, zif_ref, zift_ref, hn_ref):
    @pl.when(pl.program_id(1) == 0)
    def _():
        hn = _rms(x_ref[...], g_ref[...]).astype(BF16)
        hn_ref[...] = hn
        zif_ref[...] = _dot(hn, wif_ref[...])
        zift_ref[...] = _dot_nt(wift_ref[...], hn)

    z_ref[...] = _dot(hn_ref[...], w_ref[...]).astype(BF16)


def _inproj(x2d, g, w_main, w_if, w_ift, tm, tn):
    T = x2d.shape[0]
    return pl.pallas_call(
        _inproj_kernel,
        grid=(T // tm, N_MAIN // tn),
        in_specs=[
            pl.BlockSpec((tm, D_MODEL), lambda i, j: (i, 0)),
            pl.BlockSpec((1, D_MODEL), lambda i, j: (0, 0)),
            pl.BlockSpec((D_MODEL, tn), lambda i, j: (0, j)),
            pl.BlockSpec((D_MODEL, LANES), lambda i, j: (0, 0)),
            pl.BlockSpec((SUBLANES, D_MODEL), lambda i, j: (0, 0)),
        ],
        out_specs=[
            pl.BlockSpec((tm, tn), lambda i, j: (i, j)),
            pl.BlockSpec((tm, LANES), lambda i, j: (i, 0)),
            pl.BlockSpec((SUBLANES, tm), lambda i, j: (0, i)),
        ],
        out_shape=[
            jax.ShapeDtypeStruct((T, N_MAIN), BF16),
            jax.ShapeDtypeStruct((T, LANES), F32),
            jax.ShapeDtypeStruct((SUBLANES, T), F32),
        ],
        scratch_shapes=[pltpu.VMEM((tm, D_MODEL), BF16)],
        compiler_params=_cparams(("arbitrary", "arbitrary")),
        name="inproj",
    )(x2d, g, w_main, w_if, w_ift)


def _mlstm_kernel(q_ref, k_ref, v_ref, om_ref, zif_ref, zift_ref, cw_ref, bif_ref, bift_ref, mg_ref,
                  out_ref, qc_ref, kc_ref, kt_ref, carry_ref, c_ref, n_ref, m_ref,
                  bd_ref, bdt_ref, brep_ref, grow_ref, brow_ref, *, ts):
    nchunk = ts // CHUNK
    L = CHUNK

    @pl.when(pl.program_id(1) == 0)
    def _():
        carry_ref[...] = jnp.zeros_like(carry_ref)
        c_ref[...] = jnp.zeros_like(c_ref)
        n_ref[...] = jnp.zeros_like(n_ref)
        m_ref[...] = jnp.zeros_like(m_ref)
        rt = lax.broadcasted_iota(I32, (ts, ts), 0)
        ct = lax.broadcasted_iota(I32, (ts, ts), 1)
        same = (rt // L) == (ct // L)
        bd_ref[...] = jnp.where(same, jnp.where(ct <= rt, 1.0, 0.0), 0.0).astype(BF16)
        bdt_ref[...] = jnp.where(same, jnp.where(rt <= ct, 1.0, 0.0), 0.0).astype(BF16)

    row8 = lax.broadcasted_iota(I32, (SUBLANES, M_QK), 0)

    def conv_silu(x, prev8, w):
        acc = w[CONV_W - 1:CONV_W, :] * x
        for s in range(1, CONV_W):
            xs = pltpu.roll(x, s, 0)
            top = jnp.where(row8 < s, pltpu.roll(prev8, s, 0), xs[0:SUBLANES])
            xs = jnp.concatenate([top, xs[SUBLANES:]], axis=0)
            acc = acc + w[CONV_W - 1 - s:CONV_W - s, :] * xs
        return acc * _sigmoid(acc)

    def conv_body(c, carry):
        r0 = pl.multiple_of(c * L, L)
        xq = q_ref[pl.ds(r0, L), :].astype(F32)
        xk = k_ref[pl.ds(r0, L), :].astype(F32)
        yq = conv_silu(xq, carry_ref[:, 0:M_QK], cw_ref[:, 0:M_QK]) * (M_DK ** -0.5)
        yk = conv_silu(xk, carry_ref[:, M_QK:2 * M_QK], cw_ref[:, M_QK:2 * M_QK])
        qc_ref[pl.ds(r0, L), :] = yq.astype(BF16)
        kc_ref[pl.ds(r0, L), :] = yk.astype(BF16)
        for h in range(M_HEADS):
            kt_ref[c, h] = yk[:, h * M_DK:(h + 1) * M_DK].T
        carry_ref[:, 0:M_QK] = xq[L - SUBLANES:L]
        carry_ref[:, M_QK:2 * M_QK] = xk[L - SUBLANES:L]
        return carry

    lax.fori_loop(0, nchunk, conv_body, 0, unroll=2)

    ti = lax.broadcasted_iota(I32, (L, L), 0)
    si = lax.broadcasted_iota(I32, (L, L), 1)
    causal = si <= ti
    lane_row = lax.broadcasted_iota(I32, (LANES, LANES), 0)
    ones_l = jnp.ones((L, LANES), BF16)

    lf_col = _log_sigmoid(zif_ref[...] + bif_ref[...])
    ch, cl = _split_bf16(lf_col)
    b_col_all = _dot(bd_ref[...], ch) + _dot(bd_ref[...], cl)
    bh, bl = _split_bf16(b_col_all)
    for h in range(M_HEADS):
        sel_f = jnp.where(lane_row == M_HEADS + h, 1.0, 0.0).astype(BF16)
        brep_ref[h] = _dot(bh, sel_f) + _dot(bl, sel_f)
    g_row_all = zift_ref[...] + bift_ref[:, 0:1]
    rh, rl = _split_bf16(_log_sigmoid(g_row_all))
    b_row_tile = _dot(rh, bdt_ref[...]) + _dot(rl, bdt_ref[...])
    for cc in range(nchunk):
        grow_ref[cc] = g_row_all[:, cc * L:(cc + 1) * L]
        brow_ref[cc] = b_row_tile[:, cc * L:(cc + 1) * L]

    def chunk_body(c, carry):
        r0 = pl.multiple_of(c * L, L)
        g_row = grow_ref[c]
        b_row_all = brow_ref[c]
        early = []
        for h in range(M_HEADS):
            b_rep = brep_ref[h, pl.ds(r0, L), :]
            i_row = g_row[h:h + 1, :]
            b_row = b_row_all[M_HEADS + h:M_HEADS + h + 1, :]
            b_last = b_rep[L - 1:L, :]
            q = qc_ref[pl.ds(r0, L), h * M_DK:(h + 1) * M_DK]
            k = kc_ref[pl.ds(r0, L), h * M_DK:(h + 1) * M_DK]
            vext = jnp.concatenate([v_ref[pl.ds(r0, L), h * M_DV:(h + 1) * M_DV], ones_l], axis=1)
            dm = jnp.where(causal, b_rep[:, :L] - b_row + i_row, -jnp.inf)
            m_loc = jnp.max(dm, axis=-1, keepdims=True)
            qk = _dot_nt(q, k)
            gk_row = b_last[:, :L] - b_row + i_row
            g_max = jnp.max(gk_row, axis=-1, keepdims=True)
            kwt = (kt_ref[c, h] * jnp.exp(gk_row - g_max)).astype(BF16)
            kv = _dot(kwt, vext)
            c_old = c_ref[h]
            n_old = n_ref[h]
            qcn = _dot(q, jnp.concatenate([c_old, n_old], axis=1).astype(BF16))
            early.append((b_rep, b_last, vext, dm, m_loc, qk, g_max, kv, c_old, n_old, qcn))
        pvs = []
        for h in range(M_HEADS):
            b_rep, b_last, vext, dm, m_loc, qk, g_max, kv, c_old, n_old, qcn = early[h]
            s_loc = qk * jnp.exp(dm - m_loc)
            pvs.append(_dot(s_loc.astype(BF16), vext))
        for h in range(M_HEADS):
            b_rep, b_last, vext, dm, m_loc, qk, g_max, kv, c_old, n_old, qcn = early[h]
            pv = pvs[h]
            m_prev = m_ref[h:h + 1, :]
            inter = b_rep + m_prev
            m_t = jnp.maximum(inter, m_loc)
            w_inter = jnp.exp(inter - m_t)
            r_loc = jnp.exp(m_loc - m_t)
            den = r_loc * pv[:, M_DV:] + w_inter * qcn[:, M_DV:]
            inv = 1.0 / jnp.maximum(jnp.abs(den), jnp.exp(-m_t))
            hv = (jnp.concatenate([r_loc * inv] * 2, axis=1) * pv[:, :M_DV]
                  + jnp.concatenate([w_inter * inv] * 2, axis=1) * qcn[:, :M_DV])
            m_new = jnp.maximum(b_last + m_prev, g_max)
            decay = jnp.exp(b_last + m_prev - m_new)
            sc_loc = jnp.exp(g_max - m_new)
            c_ref[h] = (jnp.concatenate([decay] * 2, axis=1) * c_old
                        + jnp.concatenate([sc_loc] * 2, axis=1) * kv[:, :M_DV])
            n_ref[h] = decay * n_old + sc_loc * kv[:, M_DV:]
            m_ref[h:h + 1, :] = m_new
            hn = _rms(hv, mg_ref[:, h * M_DV:(h + 1) * M_DV])
            og = _sigmoid(om_ref[pl.ds(r0, L), h * M_DV:(h + 1) * M_DV].astype(F32))
            out_ref[pl.ds(r0, L), h * M_DV:(h + 1) * M_DV] = (og * hn).astype(BF16)
        return carry

    lax.fori_loop(0, nchunk, chunk_body, 0, unroll=4)


def _mlstm(z, zif, zift, conv_w, bif, bift, m_gain, B, S, ts):
    T = B * S
    nt = S // ts
    nck = ts // CHUNK
    row = lambda b, t: b * nt + t
    return pl.pallas_call(
        functools.partial(_mlstm_kernel, ts=ts),
        grid=(B, nt),
        in_specs=[
            pl.BlockSpec((ts, M_QK), lambda b, t: (row(b, t), OFF_QM // M_QK)),
            pl.BlockSpec((ts, M_QK), lambda b, t: (row(b, t), OFF_KM // M_QK)),
            pl.BlockSpec((ts, M_V), lambda b, t: (row(b, t), OFF_VM // M_V)),
            pl.BlockSpec((ts, M_V), lambda b, t: (row(b, t), OFF_OM // M_V)),
            pl.BlockSpec((ts, LANES), lambda b, t: (row(b, t), 0)),
            pl.BlockSpec((SUBLANES, ts), lambda b, t: (0, row(b, t))),
            pl.BlockSpec((CONV_W, 2 * M_QK), lambda b, t: (0, 0)),
            pl.BlockSpec((1, LANES), lambda b, t: (0, 0)),
            pl.BlockSpec((SUBLANES, LANES), lambda b, t: (0, 0)),
            pl.BlockSpec((1, M_V), lambda b, t: (0, 0)),
        ],
        out_specs=pl.BlockSpec((ts, M_V), lambda b, t: (row(b, t), 0)),
        out_shape=jax.ShapeDtypeStruct((T, M_V), BF16),
        scratch_shapes=[
            pltpu.VMEM((ts, M_QK), BF16),
            pltpu.VMEM((ts, M_QK), BF16),
            pltpu.VMEM((nck, M_HEADS, M_DK, CHUNK), F32),
            pltpu.VMEM((SUBLANES, 2 * M_QK), F32),
            pltpu.VMEM((M_HEADS, M_DK, M_DV), F32),
            pltpu.VMEM((M_HEADS, M_DK, LANES), F32),
            pltpu.VMEM((SUBLANES, LANES), F32),
            pltpu.VMEM((ts, ts), BF16),
            pltpu.VMEM((ts, ts), BF16),
            pltpu.VMEM((M_HEADS, ts, LANES), F32),
            pltpu.VMEM((nck, SUBLANES, CHUNK), F32),
            pltpu.VMEM((nck, SUBLANES, CHUNK), F32),
        ],
        compiler_params=_cparams(("arbitrary", "arbitrary")),
        name="mlstm",
    )(z, z, z, z, zif, zift, conv_w, bif, bift, m_gain)


def _diffattn_kernel(q_ref, k_ref, v_ref, lam_ref, gain_ref, out_ref, m_ref, a_ref, *, tq, lam_init):
    qi = pl.program_id(2)
    w = 2 * D_DH
    lane = lax.broadcasted_iota(I32, (1, w), 1)
    scale = jnp.asarray(D_DH ** -0.5, BF16)
    qs = []
    for hh in range(D_HP):
        q = q_ref[:, hh * w:(hh + 1) * w]
        qs.append((jnp.where(lane < D_DH, q, jnp.zeros_like(q)) * scale,
                   jnp.where(lane >= D_DH, q, jnp.zeros_like(q)) * scale))
    ones = jnp.ones((tq, w), BF16)

    def block(k0, nk, rows=slice(None), mask=None, first=False):
        n_stream = 2 * D_HP
        scores = []
        for hh in range(D_HP):
            k = k_ref[pl.ds(k0, nk), hh * w:(hh + 1) * w]
            for comp in range(2):
                s = _dot_nt(qs[hh][comp][rows], k)
                scores.append(s if mask is None else jnp.where(mask, s, -jnp.inf))
        m_news, alphas, probs = [], [], []
        for i in range(n_stream):
            s = scores[i]
            s_max = jnp.max(s, axis=-1, keepdims=True)
            if first:
                m_new = jnp.broadcast_to(s_max, (s.shape[0], w))
                alphas.append(None)
            else:
                m_old = m_ref[i, rows, :]
                m_new = jnp.maximum(m_old, s_max)
                alphas.append(jnp.exp(m_old - m_new))
            m_news.append(m_new)
            probs.append(jnp.exp(s - jnp.concatenate([m_new] * (s.shape[1] // w), axis=1)).astype(BF16))
        pvs = []
        for hh in range(D_HP):
            vext = jnp.concatenate([v_ref[pl.ds(k0, nk), hh * w:(hh + 1) * w], ones[:nk]], axis=1)
            for comp in range(2):
                pvs.append(_dot(probs[2 * hh + comp], vext))
        for i in range(n_stream):
            if first:
                a_ref[i, rows, :] = pvs[i]
            else:
                a_ref[i, rows, :] = jnp.concatenate([alphas[i]] * 2, axis=1) * a_ref[i, rows, :] + pvs[i]
            m_ref[i, rows, :] = m_news[i]

    hq = tq // 2
    d0 = pl.multiple_of(qi * tq, tq)

    def chunk_mask(q0, nk):
        rq = (lax.broadcasted_iota(I32, (hq, nk), 0) + q0) // CHUNK
        ck = lax.broadcasted_iota(I32, (hq, nk), 1) // CHUNK
        return ck <= rq

    block(d0, hq, rows=slice(0, hq), mask=chunk_mask(0, hq), first=True)
    block(d0, tq, rows=slice(hq, tq), mask=chunk_mask(hq, tq), first=True)

    def body(jj, carry):
        block(pl.multiple_of(jj * tq, tq), tq)
        return carry

    lax.fori_loop(0, qi, body, 0)

    lp = lam_ref[...]
    lam = (jnp.exp(jnp.sum(lp[0:1, :] * lp[1:2, :], axis=-1, keepdims=True))
           - jnp.exp(jnp.sum(lp[2:3, :] * lp[3:4, :], axis=-1, keepdims=True)) + lam_init)
    for hh in range(D_HP):
        a1 = a_ref[2 * hh]
        a2 = a_ref[2 * hh + 1]
        o = a1[:, :w] / a1[:, w:] - lam * (a2[:, :w] / a2[:, w:])
        out_ref[:, hh * w:(hh + 1) * w] = (_rms(o, gain_ref[...]) * (1.0 - lam_init)).astype(BF16)


def _diffattn(z, lam_p, d_gain, B, S, tq, lam_init):
    T = B * S
    nq = S // tq
    w = 2 * D_DH
    wp = D_HP * w
    return pl.pallas_call(
        functools.partial(_diffattn_kernel, tq=tq, lam_init=lam_init),
        grid=(B, D_HEADS // D_HP, nq),
        in_specs=[
            pl.BlockSpec((tq, wp), lambda b, h, i: (b * nq + i, OFF_QD // wp + h)),
            pl.BlockSpec((S, wp), lambda b, h, i: (b, OFF_KD // wp + h)),
            pl.BlockSpec((S, wp), lambda b, h, i: (b, OFF_VD // wp + h)),
            pl.BlockSpec((4, D_DH), lambda b, h, i: (0, 0)),
            pl.BlockSpec((1, w), lambda b, h, i: (0, 0)),
        ],
        out_specs=pl.BlockSpec((tq, wp), lambda b, h, i: (b * nq + i, h)),
        out_shape=jax.ShapeDtypeStruct((T, D_V), BF16),
        scratch_shapes=[
            pltpu.VMEM((2 * D_HP, tq, w), F32), pltpu.VMEM((2 * D_HP, tq, 2 * w), F32),
        ],
        compiler_params=_cparams(("arbitrary", "arbitrary", "arbitrary")),
        name="diffattn",
    )(z, z, z, lam_p, d_gain)


def _merge_kernel(hm_ref, hd_ref, gz_ref, x_ref, wbm_ref, wbd_ref, wout_ref, bg_ref, out_ref):
    bm = _dot(hm_ref[...], wbm_ref[...])
    bd = _dot(hd_ref[...], wbd_ref[...])
    g = _sigmoid(gz_ref[...].astype(F32) + bg_ref[...])
    merged = g[:, :D_MODEL] * bm + g[:, D_MODEL:] * bd
    out_ref[...] = x_ref[...] + _dot(merged.astype(BF16), wout_ref[...])


def _merge(hm, hd, z, x2d, w_bm, w_bd, w_out, b_gate, tm):
    T = x2d.shape[0]
    full = lambda i: (0, 0)
    return pl.pallas_call(
        _merge_kernel,
        grid=(T // tm,),
        in_specs=[
            pl.BlockSpec((tm, M_V), lambda i: (i, 0)),
            pl.BlockSpec((tm, D_V), lambda i: (i, 0)),
            pl.BlockSpec((tm, 2 * D_MODEL), lambda i: (i, OFF_G // (2 * D_MODEL))),
            pl.BlockSpec((tm, D_MODEL), lambda i: (i, 0)),
            pl.BlockSpec((M_V, D_MODEL), full),
            pl.BlockSpec((D_V, D_MODEL), full),
            pl.BlockSpec((D_MODEL, D_MODEL), full),
            pl.BlockSpec((1, 2 * D_MODEL), full),
        ],
        out_specs=pl.BlockSpec((tm, D_MODEL), lambda i: (i, 0)),
        out_shape=jax.ShapeDtypeStruct((T, D_MODEL), F32),
        compiler_params=_cparams(("arbitrary",)),
        name="merge",
    )(hm, hd, z, x2d, w_bm, w_bd, w_out, b_gate)


def _memkv_kernel(mem_ref, g_ref, w_ref, out_ref):
    out_ref[...] = _dot(_rms(mem_ref[...], g_ref[...]).astype(BF16), w_ref[...]).astype(BF16)


def _memkv(mem2d, g, wkv, n_mem):
    R = mem2d.shape[0]
    return pl.pallas_call(
        _memkv_kernel,
        grid=(R // n_mem,),
        in_specs=[
            pl.BlockSpec((n_mem, D_MODEL), lambda i: (i, 0)),
            pl.BlockSpec((1, D_MODEL), lambda i: (0, 0)),
            pl.BlockSpec((D_MODEL, 2 * D_MODEL), lambda i: (0, 0)),
        ],
        out_specs=pl.BlockSpec((n_mem, 2 * D_MODEL), lambda i: (i, 0)),
        out_shape=jax.ShapeDtypeStruct((R, 2 * D_MODEL), BF16),
        compiler_params=_cparams(("arbitrary",)),
        name="memkv",
    )(mem2d, g, wkv)


def _xattn_kernel(x_ref, g_ref, wq_ref, kv_ref, wo_ref, out_ref, o_ref):
    x = x_ref[...]
    h = _rms(x, g_ref[...]).astype(BF16)
    q = (_dot(h, wq_ref[...]) * (X_DH ** -0.5)).astype(BF16)
    scores = [_dot_nt(q[:, hd * X_DH:(hd + 1) * X_DH], kv_ref[:, hd * X_DH:(hd + 1) * X_DH]) for hd in range(X_HEADS)]
    for hd in range(X_HEADS):
        vh = kv_ref[:, D_MODEL + hd * X_DH:D_MODEL + (hd + 1) * X_DH]
        s = scores[hd]
        p = jnp.exp(s - jnp.max(s, axis=-1, keepdims=True))
        p = p / jnp.sum(p, axis=-1, keepdims=True)
        o_ref[:, hd * X_DH:(hd + 1) * X_DH] = _dot(p.astype(BF16), vh).astype(BF16)
    out_ref[...] = x + _dot(o_ref[...], wo_ref[...])


def _xattn(x1, g, wq, kvmem, wo, S, n_mem, tm):
    T = x1.shape[0]
    per_b = S // tm
    full = lambda i: (0, 0)
    return pl.pallas_call(
        _xattn_kernel,
        grid=(T // tm,),
        in_specs=[
            pl.BlockSpec((tm, D_MODEL), lambda i: (i, 0)),
            pl.BlockSpec((1, D_MODEL), full),
            pl.BlockSpec((D_MODEL, D_MODEL), full),
            pl.BlockSpec((n_mem, 2 * D_MODEL), lambda i: (i // per_b, 0)),
            pl.BlockSpec((D_MODEL, D_MODEL), full),
        ],
        out_specs=pl.BlockSpec((tm, D_MODEL), lambda i: (i, 0)),
        out_shape=jax.ShapeDtypeStruct((T, D_MODEL), F32),
        scratch_shapes=[pltpu.VMEM((tm, D_MODEL), BF16)],
        compiler_params=_cparams(("arbitrary",)),
        name="xattn",
    )(x1, g, wq, kvmem, wo)


def _router_kernel(x_ref, g_ref, wrh_ref, wrl_ref, br_ref, hp_ref, ids_ref, tw_ref, cnt_ref):
    @pl.when(pl.program_id(0) == 0)
    def _():
        cnt_ref[...] = jnp.zeros_like(cnt_ref)

    hn = _rms(x_ref[...], g_ref[...])
    hh, hl = _split_bf16(hn)
    hp_ref[...] = hh
    logits = _dot(hh, wrh_ref[...]) + _dot(hh, wrl_ref[...]) + _dot(hl, wrh_ref[...]) + br_ref[...]
    lane = lax.broadcasted_iota(I32, logits.shape, 1)
    lanef = lane.astype(F32)
    ids = jnp.zeros(logits.shape, F32)
    tw = jnp.zeros(logits.shape, F32)
    onehot = jnp.zeros(logits.shape, F32)
    v0 = None
    den = None
    for kk in range(TOP_K):
        mx = jnp.max(logits, axis=-1, keepdims=True)
        idx = jnp.min(jnp.where(logits == mx, lanef, float(LANES)), axis=-1, keepdims=True)
        sel = lanef == idx
        if kk == 0:
            v0 = mx
        e = jnp.exp(mx - v0)
        den = e if den is None else den + e
        ids = jnp.where(lane == kk, idx, ids)
        tw = jnp.where(lane == kk, e, tw)
        onehot = jnp.where(sel, 1.0, onehot)
        logits = jnp.where(sel, -jnp.inf, logits)
    ids_ref[...] = ids.astype(I32)
    tw_ref[...] = tw / den
    cnt_ref[...] += _ceil_rows(jnp.sum(onehot, axis=0, keepdims=True))


def _router(x2, g, wr_hi, wr_lo, b_r, tm):
    T = x2.shape[0]
    full = lambda i: (0, 0)
    return pl.pallas_call(
        _router_kernel,
        grid=(T // tm,),
        in_specs=[
            pl.BlockSpec((tm, D_MODEL), lambda i: (i, 0)),
            pl.BlockSpec((1, D_MODEL), full),
            pl.BlockSpec((D_MODEL, LANES), full),
            pl.BlockSpec((D_MODEL, LANES), full),
            pl.BlockSpec((1, LANES), full),
        ],
        out_specs=[
            pl.BlockSpec((tm, D_MODEL), lambda i: (i, 0)),
            pl.BlockSpec((tm, LANES), lambda i: (i, 0)),
            pl.BlockSpec((tm, LANES), lambda i: (i, 0)),
            pl.BlockSpec((1, LANES), full),
        ],
        out_shape=[
            jax.ShapeDtypeStruct((T, D_MODEL), BF16),
            jax.ShapeDtypeStruct((T, LANES), I32),
            jax.ShapeDtypeStruct((T, LANES), F32),
            jax.ShapeDtypeStruct((1, LANES), F32),
        ],
        compiler_params=_cparams(("arbitrary",)),
        name="router",
    )(x2, g, wr_hi, wr_lo, b_r)


SEG_WORDS = SUBLANES * LANES


def _slots_kernel(ids_ref, start_ref, ls_ref, lst_ref, seg_ref, run_ref):
    @pl.when(pl.program_id(0) == 0)
    def _():
        run_ref[...] = jnp.zeros_like(run_ref)

    ids = ids_ref[...]
    tm = ids.shape[0]
    lane = lax.broadcasted_iota(I32, ids.shape, 1)
    sels = [lane == ids[:, kk:kk + 1] for kk in range(TOP_K)]
    onehot = jnp.zeros(ids.shape, F32)
    for s in sels:
        onehot = jnp.where(s, 1.0, onehot)
    c8 = _ceil_rows(jnp.sum(onehot, axis=0, keepdims=True))
    er = lax.broadcasted_iota(I32, (LANES, LANES), 0)
    ec = lax.broadcasted_iota(I32, (LANES, LANES), 1)
    before = jnp.where(er < ec, 1.0, 0.0).astype(BF16)
    pieces = jnp.broadcast_to(c8 * (1.0 / SUBLANES), (SUBLANES, LANES)).astype(BF16)
    lo = _dot(pieces, before)[0:1, :] * SUBLANES
    r = lax.broadcasted_iota(I32, (tm, tm), 0)
    c = lax.broadcasted_iota(I32, (tm, tm), 1)
    strict = jnp.where(c < r, 1.0, 0.0).astype(BF16)
    slot = _dot(strict, onehot.astype(BF16)) + lo
    ls = jnp.zeros(ids.shape, F32)
    for kk, s in enumerate(sels):
        pk = jnp.sum(jnp.where(s, slot, 0.0), axis=-1, keepdims=True)
        ls = jnp.where(lane == kk, pk, ls)
    ls_ref[...] = ls
    hi = jnp.floor(ls * (1.0 / 32.0))
    rem = ls - 32.0 * hi
    pick = jnp.where(lax.broadcasted_iota(I32, (SUBLANES, LANES), 0) == lax.broadcasted_iota(I32, (SUBLANES, LANES), 1),
                     1.0, 0.0).astype(BF16)
    lst_ref[...] = 32.0 * _dot_nt(pick, hi.astype(BF16)) + _dot_nt(pick, rem.astype(BF16))
    row = lax.broadcasted_iota(I32, (SUBLANES, LANES), 0)
    off = start_ref[...] + run_ref[...]
    total = jnp.sum(c8, axis=-1, keepdims=True)
    seg = jnp.where(row == 0, c8, jnp.where(row == 1, lo, jnp.where(row == 2, off, jnp.where(row == 3, total, 0.0))))
    seg_ref[...] = seg.astype(I32)
    run_ref[...] += c8


def _slots(ids, starts, tm):
    T = ids.shape[0]
    nt = T // tm
    return pl.pallas_call(
        _slots_kernel,
        grid=(nt,),
        in_specs=[
            pl.BlockSpec((tm, LANES), lambda i: (i, 0)),
            pl.BlockSpec((1, LANES), lambda i: (0, 0)),
        ],
        out_specs=[
            pl.BlockSpec((tm, LANES), lambda i: (i, 0)),
            pl.BlockSpec((SUBLANES, tm), lambda i: (0, i)),
            pl.BlockSpec((SUBLANES, LANES), lambda i: (i, 0)),
        ],
        out_shape=[
            jax.ShapeDtypeStruct((T, LANES), F32),
            jax.ShapeDtypeStruct((SUBLANES, T), F32),
            jax.ShapeDtypeStruct((nt * SUBLANES, LANES), I32),
        ],
        scratch_shapes=[pltpu.VMEM((1, LANES), F32)],
        compiler_params=_cparams(("arbitrary",)),
        name="slots",
    )(ids, starts)


def _local_rows(tm):
    need = tm * TOP_K + N_EXPERTS * (SUBLANES - 1)
    return ((need + LANES - 1) // LANES) * LANES


BIG_PIECE = 4 * SUBLANES


def _segment_starts(seg, make_copy):
    def expert(e, carry):
        cnt = seg(e)
        lo = seg(LANES + e)
        off = seg(2 * LANES + e)
        n_big = lax.shift_right_logical(cnt, 5)
        n_small = lax.shift_right_logical(cnt & (BIG_PIECE - 1), 3)

        def big(j, carry2):
            d = j * BIG_PIECE
            make_copy(pl.multiple_of(lo + d, SUBLANES), pl.multiple_of(off + d, SUBLANES), BIG_PIECE).start()
            return carry2

        def small(j, carry2):
            d = n_big * BIG_PIECE + j * SUBLANES
            make_copy(pl.multiple_of(lo + d, SUBLANES), pl.multiple_of(off + d, SUBLANES), SUBLANES).start()
            return carry2

        lax.fori_loop(0, n_big, big, 0)
        lax.fori_loop(0, n_small, small, 0)
        return carry

    for e in range(N_EXPERTS):
        expert(e, 0)


def _segment_waits(total_rows, make_copy):
    def big(j, carry):
        make_copy(0, 0, BIG_PIECE).wait()
        return carry

    def small(j, carry):
        make_copy(0, 0, SUBLANES).wait()
        return carry

    lax.fori_loop(0, lax.shift_right_logical(total_rows, 5), big, 0)
    lax.fori_loop(0, lax.shift_right_logical(total_rows & (BIG_PIECE - 1), 3), small, 0)


def _dispatch_kernel(h_ref, lst_ref, seg_hbm, xs_ref, sbuf_ref, seg_smem, prev_smem, sem_seg, sem_rows, *, tm):
    i = pl.program_id(0)
    n = pl.num_programs(0)
    slot = i % 2
    rows = sbuf_ref.shape[1]
    cp = pltpu.make_async_copy(seg_hbm.at[i], seg_smem, sem_seg)
    cp.start()
    lst = lst_ref[...].astype(I32)
    rid = lax.broadcasted_iota(I32, (rows, tm), 0)
    perm = jnp.zeros((rows, tm), F32)
    for kk in range(TOP_K):
        perm = jnp.where(rid == lst[kk:kk + 1, :], 1.0, perm)
    srt = _dot(perm.astype(BF16), h_ref[...])
    sbuf_ref[slot] = _pack_exact_bf16_pairs(srt)
    cp.wait()

    def copy_from(s):
        def make_copy(lo, off, nrows):
            return pltpu.make_async_copy(sbuf_ref.at[s, pl.ds(lo, nrows), :], xs_ref.at[pl.ds(off, nrows), :],
                                         sem_rows.at[s])
        return make_copy

    _segment_starts(lambda k: seg_smem[k], copy_from(slot))

    @pl.when(i > 0)
    def _():
        _segment_waits(prev_smem[0], copy_from(1 - slot))

    prev_smem[0] = seg_smem[3 * LANES]

    @pl.when(i == n - 1)
    def _():
        _segment_waits(prev_smem[0], copy_from(slot))


def _dispatch(h, lst, seg2d, n_rows, tm):
    T = h.shape[0]
    rows = _local_rows(tm)
    return pl.pallas_call(
        functools.partial(_dispatch_kernel, tm=tm),
        grid=(T // tm,),
        in_specs=[
            pl.BlockSpec((tm, D_MODEL), lambda i: (i, 0)),
            pl.BlockSpec((SUBLANES, tm), lambda i: (0, i)),
            pl.BlockSpec(memory_space=pl.ANY),
        ],
        out_specs=pl.BlockSpec(memory_space=pl.ANY),
        out_shape=jax.ShapeDtypeStruct((n_rows, D_MODEL // 2), U32),
        scratch_shapes=[
            pltpu.VMEM((2, rows, D_MODEL // 2), U32),
            pltpu.SMEM((SEG_WORDS,), I32),
            pltpu.SMEM((1,), I32),
            pltpu.SemaphoreType.DMA,
            pltpu.SemaphoreType.DMA((2,)),
        ],
        compiler_params=_cparams(("arbitrary",)),
        name="dispatch",
    )(h, lst, seg2d)


def _experts_kernel(te_ref, nu_ref, nv_ref, xs_ref, wgu_ref, bgu_ref, wdn_ref, bdn_ref, ys_ref, wgu_bf, wdn_bf):
    i = pl.program_id(0)

    @pl.when((i == 0) | (te_ref[i] != te_ref[jnp.maximum(i - 1, 0)]))
    def _():
        wgu_bf[...] = wgu_ref[0].astype(BF16)
        wdn_bf[...] = wdn_ref[0].astype(BF16)

    @pl.when(i < nu_ref[0])
    def _():
        live = lax.broadcasted_iota(I32, xs_ref.shape, 0) < nv_ref[i]
        lo, hi = _unpack_bf16_pairs(jnp.where(live, xs_ref[...], jnp.uint32(0)))
        xb = jnp.concatenate([lo.astype(BF16), hi.astype(BF16)], axis=1)
        gu = _dot(xb, wgu_bf[...]) + bgu_ref[0]
        gate = jnp.minimum(gu[:, :D_FF], SWIGLU_LIMIT)
        up = jnp.clip(gu[:, D_FF:], -SWIGLU_LIMIT, SWIGLU_LIMIT)
        act = (up + 1.0) * (gate * _sigmoid(SWIGLU_ALPHA * gate))
        y = _dot(act.astype(BF16), wdn_bf[...]) + bdn_ref[0]
        ys_ref[...] = _pack_bf16_pairs(y)

    @pl.when(i >= nu_ref[0])
    def _():
        ys_ref[...] = jnp.zeros_like(ys_ref)


def _experts(tile_expert, n_used, tile_valid, xs, w_gu, b_gu, w_dn, b_dn, tg):
    P = xs.shape[0]
    half = D_MODEL // 2
    grid_spec = pltpu.PrefetchScalarGridSpec(
        num_scalar_prefetch=3,
        grid=(P // tg,),
        in_specs=[
            pl.BlockSpec((tg, half), lambda i, te, nu, nv: (jnp.minimum(i, jnp.maximum(nu[0] - 1, 0)), 0)),
            pl.BlockSpec((1, D_MODEL, 2 * D_FF), lambda i, te, nu, nv: (te[i], 0, 0)),
            pl.BlockSpec((1, 1, 2 * D_FF), lambda i, te, nu, nv: (te[i], 0, 0)),
            pl.BlockSpec((1, D_FF, D_MODEL), lambda i, te, nu, nv: (te[i], 0, 0)),
            pl.BlockSpec((1, 1, D_MODEL), lambda i, te, nu, nv: (te[i], 0, 0)),
        ],
        out_specs=pl.BlockSpec((tg, half), lambda i, te, nu, nv: (i, 0)),
        scratch_shapes=[pltpu.VMEM((D_MODEL, 2 * D_FF), BF16), pltpu.VMEM((D_FF, D_MODEL), BF16)],
    )
    return pl.pallas_call(
        _experts_kernel,
        grid_spec=grid_spec,
        out_shape=jax.ShapeDtypeStruct((P, half), U32),
        compiler_params=_cparams(("arbitrary",)),
        name="experts",
    )(tile_expert, n_used, tile_valid, xs, w_gu, b_gu, w_dn, b_dn)


def _combine_kernel(seg_hbm, ys_hbm, ls_ref, tw_ref, x_ref, g_ref, out_ref, ybuf_ref, seg_smem, sem_seg, sem_rows, *,
                    tm, final_norm):
    i = pl.program_id(0)
    n = pl.num_programs(0)
    slot = i % 2
    rows = ybuf_ref.shape[1]

    def copy_into(s):
        def make_copy(lo, off, nrows):
            return pltpu.make_async_copy(ys_hbm.at[pl.ds(off, nrows), :], ybuf_ref.at[s, pl.ds(lo, nrows), :],
                                         sem_rows.at[s])
        return make_copy

    def request(step, s):
        cp = pltpu.make_async_copy(seg_hbm.at[step], seg_smem.at[s], sem_seg)
        cp.start()
        cp.wait()
        _segment_starts(lambda k: seg_smem[s, k], copy_into(s))

    @pl.when(i == 0)
    def _():
        ybuf_ref[...] = jnp.zeros_like(ybuf_ref)
        request(0, 0)

    @pl.when(i + 1 < n)
    def _():
        request(i + 1, 1 - slot)

    ls = ls_ref[...].astype(I32)
    tw = tw_ref[...]
    cid = lax.broadcasted_iota(I32, (tm, rows), 1)
    wmat = jnp.zeros((tm, rows), F32)
    for kk in range(TOP_K):
        wmat = jnp.where(cid == ls[:, kk:kk + 1], tw[:, kk:kk + 1], wmat)
    wmat = wmat.astype(BF16)
    _segment_waits(seg_smem[slot, 3 * LANES], copy_into(slot))
    lo, hi = _unpack_bf16_pairs(ybuf_ref[slot])
    moe = jnp.concatenate([_dot(wmat, lo.astype(BF16)), _dot(wmat, hi.astype(BF16))], axis=1)
    x3 = x_ref[...] + moe
    out_ref[...] = _rms(x3, g_ref[...]) if final_norm else x3


def _combine(seg2d, ys, ls, tw, x2, g, tm, final_norm):
    T = x2.shape[0]
    rows = _local_rows(tm)
    return pl.pallas_call(
        functools.partial(_combine_kernel, tm=tm, final_norm=final_norm),
        grid=(T // tm,),
        in_specs=[
            pl.BlockSpec(memory_space=pl.ANY),
            pl.BlockSpec(memory_space=pl.ANY),
            pl.BlockSpec((tm, LANES), lambda i: (i, 0)),
            pl.BlockSpec((tm, LANES), lambda i: (i, 0)),
            pl.BlockSpec((tm, D_MODEL), lambda i: (i, 0)),
            pl.BlockSpec((1, D_MODEL), lambda i: (0, 0)),
        ],
        out_specs=pl.BlockSpec((tm, D_MODEL), lambda i: (i, 0)),
        out_shape=jax.ShapeDtypeStruct((T, D_MODEL), F32),
        scratch_shapes=[
            pltpu.VMEM((2, rows, D_MODEL // 2), U32),
            pltpu.SMEM((2, SEG_WORDS), I32),
            pltpu.SemaphoreType.DMA,
            pltpu.SemaphoreType.DMA((2,)),
        ],
        compiler_params=_cparams(("arbitrary",)),
        name="combine",
    )(seg2d, ys, ls, tw, x2, g)


def _tiles(B, S):
    T = B * S
    return dict(
        tm_in=min(1024, T), tn_in=2048,
        ts=min(512, S),
        tq=min(1024, S),
        tm_proj=min(512, S),
        tm_route=min(256, T),
        tg=512,
    )


def _pad_lanes(a, n=LANES, value=0.0):
    return jnp.pad(a, ((0, 0), (0, n - a.shape[1])), constant_values=value)


def kernel(x, mem, norm_mix, w_in, conv_w, b_if, mlstm_gain, diff_lambda, diff_gain, w_branch_m, w_branch_d,
           b_gate, w_out, norm_xattn, norm_mem, wq_x, wkv_x, wo_x, norm_ffn, w_router, b_router, w_gu, b_gu,
           w_dn, b_dn, norm_final):
    B, S, D = x.shape
    n_mem = mem.shape[1]
    T = B * S
    depth = norm_mix.shape[0]
    tl = _tiles(B, S)
    x2d = x.reshape(T, D)
    mem2d = mem.reshape(B * n_mem, D)

    for l in range(depth):
        lam_init = 0.8 - 0.6 * math.exp(-0.3 * l)
        wl = w_in[l]
        if_lo = 2 * M_QK + 2 * M_V
        w_main = jnp.concatenate([wl[:, :if_lo], wl[:, if_lo + 2 * M_HEADS:]], axis=1).astype(BF16)
        w_if = wl[:, if_lo:if_lo + 2 * M_HEADS]
        w_ifp = _pad_lanes(w_if).astype(BF16)
        w_ift = w_if.T.astype(BF16)
        bif = _pad_lanes(b_if[l][None, :])
        bift = jnp.broadcast_to(b_if[l][:, None], (SUBLANES, LANES))

        z, zif, zift = _inproj(x2d, norm_mix[l][None, :], w_main, w_ifp, w_ift, tl["tm_in"], tl["tn_in"])
        hm = _mlstm(z, zif, zift, conv_w[l], bif, bift, mlstm_gain[l].reshape(1, M_V), B, S, tl["ts"])
        hd = _diffattn(z, diff_lambda[l], diff_gain[l][None, :], B, S, tl["tq"], lam_init)
        x1 = _merge(hm, hd, z, x2d, w_branch_m[l].astype(BF16), w_branch_d[l].astype(BF16),
                    w_out[l].astype(BF16), b_gate[l][None, :], tl["tm_proj"])

        kvmem = _memkv(mem2d, norm_mem[l][None, :], wkv_x[l].astype(BF16), n_mem)
        x2 = _xattn(x1, norm_xattn[l][None, :], wq_x[l].astype(BF16), kvmem, wo_x[l].astype(BF16),
                    S, n_mem, tl["tm_proj"])

        wr = _pad_lanes(w_router[l])
        wr_hi = wr.astype(BF16)
        wr_lo = (wr - wr_hi.astype(F32)).astype(BF16)
        br = _pad_lanes(b_router[l][None, :], value=-jnp.inf)
        tm_r = tl["tm_route"]
        tg = tl["tg"]
        hp, ids, tw, cnt = _router(x2, norm_ffn[l][None, :], wr_hi, wr_lo, br, tm_r)

        counts = cnt[0, :N_EXPERTS].astype(I32)
        padded = ((counts + tg - 1) // tg) * tg
        ends = jnp.cumsum(padded)
        starts = ends - padded
        max_rows = T * TOP_K + (T // tm_r) * N_EXPERTS * (SUBLANES - 1)
        n_tiles = -(-max_rows // tg) + N_EXPERTS
        tile_row0 = jnp.arange(n_tiles, dtype=I32) * tg
        tile_expert = jnp.minimum(jnp.sum((tile_row0[:, None] >= ends[None, :]).astype(I32), axis=1), N_EXPERTS - 1)
        n_used = (ends[-1] // tg).astype(I32).reshape(1)
        last_used = tile_expert[jnp.maximum(n_used[0] - 1, 0)]
        tile_expert = jnp.where(tile_row0 < ends[-1], tile_expert, last_used)
        tile_valid = jnp.clip((starts + counts)[tile_expert] - tile_row0, 0, tg).astype(I32)

        ls, lst, seg = _slots(ids, _pad_lanes(starts.astype(F32)[None, :]), tm_r)
        seg2d = seg.reshape(T // tm_r, SEG_WORDS)

        xs = _dispatch(hp, lst, seg2d, n_tiles * tg, tm_r)
        ys = _experts(tile_expert, n_used, tile_valid, xs, w_gu[l], b_gu[l][:, None, :], w_dn[l], b_dn[l][:, None, :],
                      tg)
        x2d = _combine(seg2d, ys, ls, tw, x2, norm_final[None, :], tm_r, final_norm=(l == depth - 1))
    return x2d.reshape(B, S, D)
```

```python
import functools
import math

import jax
import jax.numpy as jnp
from jax import lax
from jax.experimental import pallas as pl
from jax.experimental.pallas import tpu as pltpu

F32 = jnp.float32
BF16 = jnp.bfloat16
U32 = jnp.uint32
I32 = jnp.int32

EPS = 1e-6
CHUNK = 64
D_MODEL = 1024
M_HEADS = 4
M_DK = 128
M_DV = 256
M_QK = M_HEADS * M_DK
M_V = M_HEADS * M_DV
CONV_W = 4
D_HEADS = 8
D_DH = 64
D_HP = 2
D_QK = D_HEADS * 2 * D_DH
D_V = D_HEADS * 2 * D_DH
X_HEADS = 4
X_DH = D_MODEL // X_HEADS
N_EXPERTS = 32
TOP_K = 4
D_FF = D_MODEL
SWIGLU_LIMIT = 7.0
SWIGLU_ALPHA = 1.702

LANES = 128
SUBLANES = 8
N_MAIN = 2 * M_QK + 2 * M_V + 2 * D_QK + D_V + 2 * D_MODEL
OFF_QM, OFF_KM, OFF_VM, OFF_OM = 0, M_QK, 2 * M_QK, 2 * M_QK + M_V
OFF_QD = OFF_OM + M_V
OFF_KD = OFF_QD + D_QK
OFF_VD = OFF_KD + D_QK
OFF_G = OFF_VD + D_V

VMEM_LIMIT = 56 * 1024 * 1024


def _cparams(sem, vmem=VMEM_LIMIT):
    return pltpu.CompilerParams(dimension_semantics=sem, vmem_limit_bytes=vmem)


def _rms(x, g):
    return x * lax.rsqrt(jnp.mean(x * x, axis=-1, keepdims=True) + EPS) * g


def _split_bf16(x):
    hi = x.astype(BF16)
    lo = (x - hi.astype(F32)).astype(BF16)
    return hi, lo


def _dot(a, b):
    return jnp.dot(a, b, preferred_element_type=F32)


def _dot_nt(a, b):
    return lax.dot_general(a, b, (((1,), (1,)), ((), ())), preferred_element_type=F32)


def _sigmoid(x):
    return 1.0 / (1.0 + jnp.exp(-x))


def _log_sigmoid(x):
    return jnp.minimum(x, 0.0) - jnp.log(1.0 + jnp.exp(-jnp.abs(x)))


def _pack_bf16_pairs(x):
    w = x.shape[1] // 2
    u = lax.bitcast_convert_type(x, U32)
    r = (u + jnp.uint32(0x7FFF) + ((u >> 16) & jnp.uint32(1))) >> 16
    return r[:, :w] | (r[:, w:] << 16)


def _pack_exact_bf16_pairs(x):
    w = x.shape[1] // 2
    u = lax.bitcast_convert_type(x, U32)
    return (u[:, :w] >> 16) | (u[:, w:] & jnp.uint32(0xFFFF0000))


def _ceil_rows(x):
    return jnp.floor((x + (SUBLANES - 1)) * (1.0 / SUBLANES)) * SUBLANES


def _unpack_bf16_pairs(p):
    lo = lax.bitcast_convert_type(p << 16, F32)
    hi = lax.bitcast_convert_type(p & jnp.uint32(0xFFFF0000), F32)
    return lo, hi


def _inproj_kernel(x_ref, g_ref, w_ref, wif_ref, wift_ref, z_ref, zif_ref, zift_ref, hn_ref):
    @pl.when(pl.program_id(1) == 0)
    def _():
        hn = _rms(x_ref[...], g_ref[...]).astype(BF16)
        hn_ref[...] = hn
        zif_ref[...] = _dot(hn, wif_ref[...])
        zift_ref[...] = _dot_nt(wift_ref[...], hn)

    z_ref[...] = _dot(hn_ref[...], w_ref[...]).astype(BF16)


def _inproj(x2d, g, w_main, w_if, w_ift, tm, tn):
    T = x2d.shape[0]
    return pl.pallas_call(
        _inproj_kernel,
        grid=(T // tm, N_MAIN // tn),
        in_specs=[
            pl.BlockSpec((tm, D_MODEL), lambda i, j: (i, 0)),
            pl.BlockSpec((1, D_MODEL), lambda i, j: (0, 0)),
            pl.BlockSpec((D_MODEL, tn), lambda i, j: (0, j)),
            pl.BlockSpec((D_MODEL, LANES), lambda i, j: (0, 0)),
            pl.BlockSpec((SUBLANES, D_MODEL), lambda i, j: (0, 0)),
        ],
        out_specs=[
            pl.BlockSpec((tm, tn), lambda i, j: (i, j)),
            pl.BlockSpec((tm, LANES), lambda i, j: (i, 0)),
            pl.BlockSpec((SUBLANES, tm), lambda i, j: (0, i)),
        ],
        out_shape=[
            jax.ShapeDtypeStruct((T, N_MAIN), BF16),
            jax.ShapeDtypeStruct((T, LANES), F32),
            jax.ShapeDtypeStruct((SUBLANES, T), F32),
        ],
        scratch_shapes=[pltpu.VMEM((tm, D_MODEL), BF16)],
        compiler_params=_cparams(("arbitrary", "arbitrary")),
        name="inproj",
    )(x2d, g, w_main, w_if, w_ift)


def _mlstm_kernel(q_ref, k_ref, v_ref, om_ref, zif_ref, zift_ref, cw_ref, bif_ref, bift_ref, mg_ref,
                  out_ref, qc_ref, kc_ref, kt_ref, carry_ref, c_ref, n_ref, m_ref,
                  bd_ref, bdt_ref, brep_ref, grow_ref, brow_ref, *, ts):
    nchunk = ts // CHUNK
    L = CHUNK

    @pl.when(pl.program_id(1) == 0)
    def _():
        carry_ref[...] = jnp.zeros_like(carry_ref)
        c_ref[...] = jnp.zeros_like(c_ref)
        n_ref[...] = jnp.zeros_like(n_ref)
        m_ref[...] = jnp.zeros_like(m_ref)
        rt = lax.broadcasted_iota(I32, (ts, ts), 0)
        ct = lax.broadcasted_iota(I32, (ts, ts), 1)
        same = (rt // L) == (ct // L)
        bd_ref[...] = jnp.where(same, jnp.where(ct <= rt, 1.0, 0.0), 0.0).astype(BF16)
        bdt_ref[...] = jnp.where(same, jnp.where(rt <= ct, 1.0, 0.0), 0.0).astype(BF16)

    row8 = lax.broadcasted_iota(I32, (SUBLANES, M_QK), 0)

    def conv_silu(x, prev8, w):
        acc = w[CONV_W - 1:CONV_W, :] * x
        for s in range(1, CONV_W):
            xs = pltpu.roll(x, s, 0)
            top = jnp.where(row8 < s, pltpu.roll(prev8, s, 0), xs[0:SUBLANES])
            xs = jnp.concatenate([top, xs[SUBLANES:]], axis=0)
            acc = acc + w[CONV_W - 1 - s:CONV_W - s, :] * xs
        return acc * _sigmoid(acc)

    def conv_body(c, carry):
        r0 = pl.multiple_of(c * L, L)
        xq = q_ref[pl.ds(r0, L), :].astype(F32)
        xk = k_ref[pl.ds(r0, L), :].astype(F32)
        yq = conv_silu(xq, carry_ref[:, 0:M_QK], cw_ref[:, 0:M_QK]) * (M_DK ** -0.5)
        yk = conv_silu(xk, carry_ref[:, M_QK:2 * M_QK], cw_ref[:, M_QK:2 * M_QK])
        qc_ref[pl.ds(r0, L), :] = yq.astype(BF16)
        kc_ref[pl.ds(r0, L), :] = yk.astype(BF16)
        for h in range(M_HEADS):
            kt_ref[c, h] = yk[:, h * M_DK:(h + 1) * M_DK].T
        carry_ref[:, 0:M_QK] = xq[L - SUBLANES:L]
        carry_ref[:, M_QK:2 * M_QK] = xk[L - SUBLANES:L]
        return carry

    lax.fori_loop(0, nchunk, conv_body, 0, unroll=4)

    ti = lax.broadcasted_iota(I32, (L, L), 0)
    si = lax.broadcasted_iota(I32, (L, L), 1)
    causal = si <= ti
    lane_row = lax.broadcasted_iota(I32, (LANES, LANES), 0)
    ones_l = jnp.ones((L, LANES), BF16)

    lf_col = _log_sigmoid(zif_ref[...] + bif_ref[...])
    ch, cl = _split_bf16(lf_col)
    bd = bd_ref[...]
    b_col_all = _dot(jnp.concatenate([bd, bd], axis=1), jnp.concatenate([ch, cl], axis=0))
    bhl = jnp.concatenate(_split_bf16(b_col_all), axis=1)
    for h in range(M_HEADS):
        sel_f = jnp.where(lane_row == M_HEADS + h, 1.0, 0.0).astype(BF16)
        brep_ref[h] = _dot(bhl, jnp.concatenate([sel_f, sel_f], axis=0))
    g_row_all = zift_ref[...] + bift_ref[:, 0:1]
    rhl = jnp.concatenate(_split_bf16(_log_sigmoid(g_row_all)), axis=1)
    bdt = bdt_ref[...]
    b_row_tile = _dot(rhl, jnp.concatenate([bdt, bdt], axis=0))
    for cc in range(nchunk):
        grow_ref[cc] = g_row_all[:, cc * L:(cc + 1) * L]
        brow_ref[cc] = b_row_tile[:, cc * L:(cc + 1) * L]

    def chunk_body(c, carry):
        r0 = pl.multiple_of(c * L, L)
        g_row = grow_ref[c]
        b_row_all = brow_ref[c]
        early = []
        for h in range(M_HEADS):
            b_rep = brep_ref[h, pl.ds(r0, L), :]
            i_row = g_row[h:h + 1, :]
            b_row = b_row_all[M_HEADS + h:M_HEADS + h + 1, :]
            b_last = b_rep[L - 1:L, :]
            q = qc_ref[pl.ds(r0, L), h * M_DK:(h + 1) * M_DK]
            k = kc_ref[pl.ds(r0, L), h * M_DK:(h + 1) * M_DK]
            vext = jnp.concatenate([v_ref[pl.ds(r0, L), h * M_DV:(h + 1) * M_DV], ones_l], axis=1)
            dm = jnp.where(causal, b_rep[:, :L] - b_row + i_row, -jnp.inf)
            m_loc = jnp.max(dm, axis=-1, keepdims=True)
            qk = _dot_nt(q, k)
            gk_row = b_last[:, :L] - b_row + i_row
            g_max = jnp.max(gk_row, axis=-1, keepdims=True)
            kwt = (kt_ref[c, h] * jnp.exp(gk_row - g_max)).astype(BF16)
            kv = _dot(kwt, vext)
            c_old = c_ref[h]
            n_old = n_ref[h]
            qcn = _dot(q, jnp.concatenate([c_old, n_old], axis=1).astype(BF16))
            early.append((b_rep, b_last, vext, dm, m_loc, qk, g_max, kv, c_old, n_old, qcn))
        pvs = []
        for h in range(M_HEADS):
            b_rep, b_last, vext, dm, m_loc, qk, g_max, kv, c_old, n_old, qcn = early[h]
            s_loc = qk * jnp.exp(dm - m_loc)
            pvs.append(_dot(s_loc.astype(BF16), vext))
        for h in range(M_HEADS):
            b_rep, b_last, vext, dm, m_loc, qk, g_max, kv, c_old, n_old, qcn = early[h]
            pv = pvs[h]
            m_prev = m_ref[h:h + 1, :]
            inter = b_rep + m_prev
            m_t = jnp.maximum(inter, m_loc)
            w_inter = jnp.exp(inter - m_t)
            r_loc = jnp.exp(m_loc - m_t)
            den = r_loc * pv[:, M_DV:] + w_inter * qcn[:, M_DV:]
            inv = 1.0 / jnp.maximum(jnp.abs(den), jnp.exp(-m_t))
            hv = (jnp.concatenate([r_loc * inv] * 2, axis=1) * pv[:, :M_DV]
                  + jnp.concatenate([w_inter * inv] * 2, axis=1) * qcn[:, :M_DV])
            m_new = jnp.maximum(b_last + m_prev, g_max)
            decay = jnp.exp(b_last + m_prev - m_new)
            sc_loc = jnp.exp(g_max - m_new)
            c_ref[h] = (jnp.concatenate([decay] * 2, axis=1) * c_old
                        + jnp.concatenate([sc_loc] * 2, axis=1) * kv[:, :M_DV])
            n_ref[h] = decay * n_old + sc_loc * kv[:, M_DV:]
            m_ref[h:h + 1, :] = m_new
            hn = _rms(hv, mg_ref[:, h * M_DV:(h + 1) * M_DV])
            og = _sigmoid(om_ref[pl.ds(r0, L), h * M_DV:(h + 1) * M_DV].astype(F32))
            out_ref[pl.ds(r0, L), h * M_DV:(h + 1) * M_DV] = (og * hn).astype(BF16)
        return carry

    lax.fori_loop(0, nchunk, chunk_body, 0, unroll=4)


def _mlstm(z, zif, zift, conv_w, bif, bift, m_gain, B, S, ts):
    T = B * S
    nt = S // ts
    nck = ts // CHUNK
    row = lambda b, t: b * nt + t
    return pl.pallas_call(
        functools.partial(_mlstm_kernel, ts=ts),
        grid=(B, nt),
        in_specs=[
            pl.BlockSpec((ts, M_QK), lambda b, t: (row(b, t), OFF_QM // M_QK)),
            pl.BlockSpec((ts, M_QK), lambda b, t: (row(b, t), OFF_KM // M_QK)),
            pl.BlockSpec((ts, M_V), lambda b, t: (row(b, t), OFF_VM // M_V)),
            pl.BlockSpec((ts, M_V), lambda b, t: (row(b, t), OFF_OM // M_V)),
            pl.BlockSpec((ts, LANES), lambda b, t: (row(b, t), 0)),
            pl.BlockSpec((SUBLANES, ts), lambda b, t: (0, row(b, t))),
            pl.BlockSpec((CONV_W, 2 * M_QK), lambda b, t: (0, 0)),
            pl.BlockSpec((1, LANES), lambda b, t: (0, 0)),
            pl.BlockSpec((SUBLANES, LANES), lambda b, t: (0, 0)),
            pl.BlockSpec((1, M_V), lambda b, t: (0, 0)),
        ],
        out_specs=pl.BlockSpec((ts, M_V), lambda b, t: (row(b, t), 0)),
        out_shape=jax.ShapeDtypeStruct((T, M_V), BF16),
        scratch_shapes=[
            pltpu.VMEM((ts, M_QK), BF16),
            pltpu.VMEM((ts, M_QK), BF16),
            pltpu.VMEM((nck, M_HEADS, M_DK, CHUNK), F32),
            pltpu.VMEM((SUBLANES, 2 * M_QK), F32),
            pltpu.VMEM((M_HEADS, M_DK, M_DV), F32),
            pltpu.VMEM((M_HEADS, M_DK, LANES), F32),
            pltpu.VMEM((SUBLANES, LANES), F32),
            pltpu.VMEM((ts, ts), BF16),
            pltpu.VMEM((ts, ts), BF16),
            pltpu.VMEM((M_HEADS, ts, LANES), F32),
            pltpu.VMEM((nck, SUBLANES, CHUNK), F32),
            pltpu.VMEM((nck, SUBLANES, CHUNK), F32),
        ],
        compiler_params=_cparams(("arbitrary", "arbitrary")),
        name="mlstm",
    )(z, z, z, z, zif, zift, conv_w, bif, bift, m_gain)


def _diffattn_kernel(q_ref, k_ref, v_ref, lam_ref, gain_ref, out_ref, m_ref, a_ref, *, tq, lam_init):
    qi = pl.program_id(2)
    w = 2 * D_DH
    lane = lax.broadcasted_iota(I32, (1, w), 1)
    scale = jnp.asarray(D_DH ** -0.5, BF16)
    qs = []
    for hh in range(D_HP):
        q = q_ref[:, hh * w:(hh + 1) * w]
        qs.append((jnp.where(lane < D_DH, q, jnp.zeros_like(q)) * scale,
                   jnp.where(lane >= D_DH, q, jnp.zeros_like(q)) * scale))
    ones = jnp.ones((tq, w), BF16)

    def block(k0, nk, rows=slice(None), mask=None, first=False):
        n_stream = 2 * D_HP
        scores = []
        for hh in range(D_HP):
            k = k_ref[pl.ds(k0, nk), hh * w:(hh + 1) * w]
            for comp in range(2):
                s = _dot_nt(qs[hh][comp][rows], k)
                scores.append(s if mask is None else jnp.where(mask, s, -jnp.inf))
        m_news, alphas, probs = [], [], []
        for i in range(n_stream):
            s = scores[i]
            s_max = jnp.max(s, axis=-1, keepdims=True)
            if first:
                m_new = jnp.broadcast_to(s_max, (s.shape[0], w))
                alphas.append(None)
            else:
                m_old = m_ref[i, rows, :]
                m_new = jnp.maximum(m_old, s_max)
                alphas.append(jnp.exp(m_old - m_new))
            m_news.append(m_new)
            probs.append(jnp.exp(s - jnp.concatenate([m_new] * (s.shape[1] // w), axis=1)).astype(BF16))
        pvs = []
        for hh in range(D_HP):
            vext = jnp.concatenate([v_ref[pl.ds(k0, nk), hh * w:(hh + 1) * w], ones[:nk]], axis=1)
            for comp in range(2):
                pvs.append(_dot(probs[2 * hh + comp], vext))
        for i in range(n_stream):
            if first:
                a_ref[i, rows, :] = pvs[i]
            else:
                a_ref[i, rows, :] = jnp.concatenate([alphas[i]] * 2, axis=1) * a_ref[i, rows, :] + pvs[i]
            m_ref[i, rows, :] = m_news[i]

    hq = tq // 2
    d0 = pl.multiple_of(qi * tq, tq)

    def chunk_mask(q0, nk):
        rq = (lax.broadcasted_iota(I32, (hq, nk), 0) + q0) // CHUNK
        ck = lax.broadcasted_iota(I32, (hq, nk), 1) // CHUNK
        return ck <= rq

    block(d0, hq, rows=slice(0, hq), mask=chunk_mask(0, hq), first=True)
    block(d0, tq, rows=slice(hq, tq), mask=chunk_mask(hq, tq), first=True)

    def body(jj, carry):
        block(pl.multiple_of(jj * tq, tq), tq)
        return carry

    lax.fori_loop(0, qi, body, 0)

    lp = lam_ref[...]
    lam = (jnp.exp(jnp.sum(lp[0:1, :] * lp[1:2, :], axis=-1, keepdims=True))
           - jnp.exp(jnp.sum(lp[2:3, :] * lp[3:4, :], axis=-1, keepdims=True)) + lam_init)
    for hh in range(D_HP):
        a1 = a_ref[2 * hh]
        a2 = a_ref[2 * hh + 1]
        o = a1[:, :w] / a1[:, w:] - lam * (a2[:, :w] / a2[:, w:])
        out_ref[:, hh * w:(hh + 1) * w] = (_rms(o, gain_ref[...]) * (1.0 - lam_init)).astype(BF16)


def _diffattn(z, lam_p, d_gain, B, S, tq, lam_init):
    T = B * S
    nq = S // tq
    w = 2 * D_DH
    wp = D_HP * w
    return pl.pallas_call(
        functools.partial(_diffattn_kernel, tq=tq, lam_init=lam_init),
        grid=(B, D_HEADS // D_HP, nq),
        in_specs=[
            pl.BlockSpec((tq, wp), lambda b, h, i: (b * nq + i, OFF_QD // wp + h)),
            pl.BlockSpec((S, wp), lambda b, h, i: (b, OFF_KD // wp + h)),
            pl.BlockSpec((S, wp), lambda b, h, i: (b, OFF_VD // wp + h)),
            pl.BlockSpec((4, D_DH), lambda b, h, i: (0, 0)),
            pl.BlockSpec((1, w), lambda b, h, i: (0, 0)),
        ],
        out_specs=pl.BlockSpec((tq, wp), lambda b, h, i: (b * nq + i, h)),
        out_shape=jax.ShapeDtypeStruct((T, D_V), BF16),
        scratch_shapes=[
            pltpu.VMEM((2 * D_HP, tq, w), F32), pltpu.VMEM((2 * D_HP, tq, 2 * w), F32),
        ],
        compiler_params=_cparams(("arbitrary", "arbitrary", "arbitrary")),
        name="diffattn",
    )(z, z, z, lam_p, d_gain)


def _merge_kernel(hm_ref, hd_ref, gz_ref, x_ref, wbm_ref, wbd_ref, wout_ref, bg_ref, out_ref):
    bm = _dot(hm_ref[...], wbm_ref[...])
    bd = _dot(hd_ref[...], wbd_ref[...])
    g = _sigmoid(gz_ref[...].astype(F32) + bg_ref[...])
    merged = g[:, :D_MODEL] * bm + g[:, D_MODEL:] * bd
    out_ref[...] = x_ref[...] + _dot(merged.astype(BF16), wout_ref[...])


def _merge(hm, hd, z, x2d, w_bm, w_bd, w_out, b_gate, tm):
    T = x2d.shape[0]
    full = lambda i: (0, 0)
    return pl.pallas_call(
        _merge_kernel,
        grid=(T // tm,),
        in_specs=[
            pl.BlockSpec((tm, M_V), lambda i: (i, 0)),
            pl.BlockSpec((tm, D_V), lambda i: (i, 0)),
            pl.BlockSpec((tm, 2 * D_MODEL), lambda i: (i, OFF_G // (2 * D_MODEL))),
            pl.BlockSpec((tm, D_MODEL), lambda i: (i, 0)),
            pl.BlockSpec((M_V, D_MODEL), full),
            pl.BlockSpec((D_V, D_MODEL), full),
            pl.BlockSpec((D_MODEL, D_MODEL), full),
            pl.BlockSpec((1, 2 * D_MODEL), full),
        ],
        out_specs=pl.BlockSpec((tm, D_MODEL), lambda i: (i, 0)),
        out_shape=jax.ShapeDtypeStruct((T, D_MODEL), F32),
        compiler_params=_cparams(("arbitrary",)),
        name="merge",
    )(hm, hd, z, x2d, w_bm, w_bd, w_out, b_gate)


def _memkv_kernel(mem_ref, g_ref, w_ref, out_ref):
    out_ref[...] = _dot(_rms(mem_ref[...], g_ref[...]).astype(BF16), w_ref[...]).astype(BF16)


def _memkv(mem2d, g, wkv, n_mem):
    R = mem2d.shape[0]
    return pl.pallas_call(
        _memkv_kernel,
        grid=(R // n_mem,),
        in_specs=[
            pl.BlockSpec((n_mem, D_MODEL), lambda i: (i, 0)),
            pl.BlockSpec((1, D_MODEL), lambda i: (0, 0)),
            pl.BlockSpec((D_MODEL, 2 * D_MODEL), lambda i: (0, 0)),
        ],
        out_specs=pl.BlockSpec((n_mem, 2 * D_MODEL), lambda i: (i, 0)),
        out_shape=jax.ShapeDtypeStruct((R, 2 * D_MODEL), BF16),
        compiler_params=_cparams(("arbitrary",)),
        name="memkv",
    )(mem2d, g, wkv)


def _xattn_kernel(x_ref, g_ref, wq_ref, kv_ref, wo_ref, out_ref, o_ref):
    x = x_ref[...]
    h = _rms(x, g_ref[...]).astype(BF16)
    q = (_dot(h, wq_ref[...]) * (X_DH ** -0.5)).astype(BF16)
    scores = [_dot_nt(q[:, hd * X_DH:(hd + 1) * X_DH], kv_ref[:, hd * X_DH:(hd + 1) * X_DH]) for hd in range(X_HEADS)]
    for hd in range(X_HEADS):
        vh = kv_ref[:, D_MODEL + hd * X_DH:D_MODEL + (hd + 1) * X_DH]
        s = scores[hd]
        p = jnp.exp(s - jnp.max(s, axis=-1, keepdims=True))
        p = p / jnp.sum(p, axis=-1, keepdims=True)
        o_ref[:, hd * X_DH:(hd + 1) * X_DH] = _dot(p.astype(BF16), vh).astype(BF16)
    out_ref[...] = x + _dot(o_ref[...], wo_ref[...])


def _xattn(x1, g, wq, kvmem, wo, S, n_mem, tm):
    T = x1.shape[0]
    per_b = S // tm
    full = lambda i: (0, 0)
    return pl.pallas_call(
        _xattn_kernel,
        grid=(T // tm,),
        in_specs=[
            pl.BlockSpec((tm, D_MODEL), lambda i: (i, 0)),
            pl.BlockSpec((1, D_MODEL), full),
            pl.BlockSpec((D_MODEL, D_MODEL), full),
            pl.BlockSpec((n_mem, 2 * D_MODEL), lambda i: (i // per_b, 0)),
            pl.BlockSpec((D_MODEL, D_MODEL), full),
        ],
        out_specs=pl.BlockSpec((tm, D_MODEL), lambda i: (i, 0)),
        out_shape=jax.ShapeDtypeStruct((T, D_MODEL), F32),
        scratch_shapes=[pltpu.VMEM((tm, D_MODEL), BF16)],
        compiler_params=_cparams(("arbitrary",)),
        name="xattn",
    )(x1, g, wq, kvmem, wo)


def _router_kernel(x_ref, g_ref, wrh_ref, wrl_ref, br_ref, hp_ref, ids_ref, tw_ref, cnt_ref):
    @pl.when(pl.program_id(0) == 0)
    def _():
        cnt_ref[...] = jnp.zeros_like(cnt_ref)

    hn = _rms(x_ref[...], g_ref[...])
    hh, hl = _split_bf16(hn)
    hp_ref[...] = hh
    logits = _dot(hh, wrh_ref[...]) + _dot(hh, wrl_ref[...]) + _dot(hl, wrh_ref[...]) + br_ref[...]
    lane = lax.broadcasted_iota(I32, logits.shape, 1)
    lanef = lane.astype(F32)
    ids = jnp.zeros(logits.shape, F32)
    tw = jnp.zeros(logits.shape, F32)
    onehot = jnp.zeros(logits.shape, F32)
    v0 = None
    den = None
    for kk in range(TOP_K):
        mx = jnp.max(logits, axis=-1, keepdims=True)
        idx = jnp.min(jnp.where(logits == mx, lanef, float(LANES)), axis=-1, keepdims=True)
        sel = lanef == idx
        if kk == 0:
            v0 = mx
        e = jnp.exp(mx - v0)
        den = e if den is None else den + e
        ids = jnp.where(lane == kk, idx, ids)
        tw = jnp.where(lane == kk, e, tw)
        onehot = jnp.where(sel, 1.0, onehot)
        logits = jnp.where(sel, -jnp.inf, logits)
    ids_ref[...] = ids.astype(I32)
    tw_ref[...] = tw / den
    cnt_ref[...] += _ceil_rows(jnp.sum(onehot, axis=0, keepdims=True))


def _router(x2, g, wr_hi, wr_lo, b_r, tm):
    T = x2.shape[0]
    full = lambda i: (0, 0)
    return pl.pallas_call(
        _router_kernel,
        grid=(T // tm,),
        in_specs=[
            pl.BlockSpec((tm, D_MODEL), lambda i: (i, 0)),
            pl.BlockSpec((1, D_MODEL), full),
            pl.BlockSpec((D_MODEL, LANES), full),
            pl.BlockSpec((D_MODEL, LANES), full),
            pl.BlockSpec((1, LANES), full),
        ],
        out_specs=[
            pl.BlockSpec((tm, D_MODEL), lambda i: (i, 0)),
            pl.BlockSpec((tm, LANES), lambda i: (i, 0)),
            pl.BlockSpec((tm, LANES), lambda i: (i, 0)),
            pl.BlockSpec((1, LANES), full),
        ],
        out_shape=[
            jax.ShapeDtypeStruct((T, D_MODEL), BF16),
            jax.ShapeDtypeStruct((T, LANES), I32),
            jax.ShapeDtypeStruct((T, LANES), F32),
            jax.ShapeDtypeStruct((1, LANES), F32),
        ],
        compiler_params=_cparams(("arbitrary",)),
        name="router",
    )(x2, g, wr_hi, wr_lo, b_r)


SEG_WORDS = SUBLANES * LANES


def _slots_kernel(ids_ref, start_ref, ls_ref, lst_ref, seg_ref, run_ref):
    @pl.when(pl.program_id(0) == 0)
    def _():
        run_ref[...] = jnp.zeros_like(run_ref)

    ids = ids_ref[...]
    tm = ids.shape[0]
    lane = lax.broadcasted_iota(I32, ids.shape, 1)
    sels = [lane == ids[:, kk:kk + 1] for kk in range(TOP_K)]
    onehot = jnp.zeros(ids.shape, F32)
    for s in sels:
        onehot = jnp.where(s, 1.0, onehot)
    c8 = _ceil_rows(jnp.sum(onehot, axis=0, keepdims=True))
    er = lax.broadcasted_iota(I32, (LANES, LANES), 0)
    ec = lax.broadcasted_iota(I32, (LANES, LANES), 1)
    before = jnp.where(er < ec, 1.0, 0.0).astype(BF16)
    pieces = jnp.broadcast_to(c8 * (1.0 / SUBLANES), (SUBLANES, LANES)).astype(BF16)
    lo = _dot(pieces, before)[0:1, :] * SUBLANES
    r = lax.broadcasted_iota(I32, (tm, tm), 0)
    c = lax.broadcasted_iota(I32, (tm, tm), 1)
    strict = jnp.where(c < r, 1.0, 0.0).astype(BF16)
    slot = _dot(strict, onehot.astype(BF16)) + lo
    ls = jnp.zeros(ids.shape, F32)
    for kk, s in enumerate(sels):
        pk = jnp.sum(jnp.where(s, slot, 0.0), axis=-1, keepdims=True)
        ls = jnp.where(lane == kk, pk, ls)
    ls_ref[...] = ls
    hi = jnp.floor(ls * (1.0 / 32.0))
    rem = ls - 32.0 * hi
    pick = jnp.where(lax.broadcasted_iota(I32, (SUBLANES, LANES), 0) == lax.broadcasted_iota(I32, (SUBLANES, LANES), 1),
                     1.0, 0.0).astype(BF16)
    lst_ref[...] = 32.0 * _dot_nt(pick, hi.astype(BF16)) + _dot_nt(pick, rem.astype(BF16))
    row = lax.broadcasted_iota(I32, (SUBLANES, LANES), 0)
    off = start_ref[...] + run_ref[...]
    total = jnp.sum(c8, axis=-1, keepdims=True)
    seg = jnp.where(row == 0, c8, jnp.where(row == 1, lo, jnp.where(row == 2, off, jnp.where(row == 3, total, 0.0))))
    seg_ref[...] = seg.astype(I32)
    run_ref[...] += c8


def _slots(ids, starts, tm):
    T = ids.shape[0]
    nt = T // tm
    return pl.pallas_call(
        _slots_kernel,
        grid=(nt,),
        in_specs=[
            pl.BlockSpec((tm, LANES), lambda i: (i, 0)),
            pl.BlockSpec((1, LANES), lambda i: (0, 0)),
        ],
        out_specs=[
            pl.BlockSpec((tm, LANES), lambda i: (i, 0)),
            pl.BlockSpec((SUBLANES, tm), lambda i: (0, i)),
            pl.BlockSpec((SUBLANES, LANES), lambda i: (i, 0)),
        ],
        out_shape=[
            jax.ShapeDtypeStruct((T, LANES), F32),
            jax.ShapeDtypeStruct((SUBLANES, T), F32),
            jax.ShapeDtypeStruct((nt * SUBLANES, LANES), I32),
        ],
        scratch_shapes=[pltpu.VMEM((1, LANES), F32)],
        compiler_params=_cparams(("arbitrary",)),
        name="slots",
    )(ids, starts)


def _local_rows(tm):
    need = tm * TOP_K + N_EXPERTS * (SUBLANES - 1)
    return ((need + LANES - 1) // LANES) * LANES


BIG_PIECE = 4 * SUBLANES


def _segment_starts(seg, make_copy):
    def expert(e, carry):
        cnt = seg(e)
        lo = seg(LANES + e)
        off = seg(2 * LANES + e)
        n_big = lax.shift_right_logical(cnt, 5)
        n_small = lax.shift_right_logical(cnt & (BIG_PIECE - 1), 3)

        def big(j, carry2):
            d = j * BIG_PIECE
            make_copy(pl.multiple_of(lo + d, SUBLANES), pl.multiple_of(off + d, SUBLANES), BIG_PIECE).start()
            return carry2

        def small(j, carry2):
            d = n_big * BIG_PIECE + j * SUBLANES
            make_copy(pl.multiple_of(lo + d, SUBLANES), pl.multiple_of(off + d, SUBLANES), SUBLANES).start()
            return carry2

        lax.fori_loop(0, n_big, big, 0)
        lax.fori_loop(0, n_small, small, 0)
        return carry

    for e in range(N_EXPERTS):
        expert(e, 0)


def _segment_waits(total_rows, make_copy):
    def big(j, carry):
        make_copy(0, 0, BIG_PIECE).wait()
        return carry

    def small(j, carry):
        make_copy(0, 0, SUBLANES).wait()
        return carry

    lax.fori_loop(0, lax.shift_right_logical(total_rows, 5), big, 0)
    lax.fori_loop(0, lax.shift_right_logical(total_rows & (BIG_PIECE - 1), 3), small, 0)


def _dispatch_kernel(h_ref, lst_ref, seg_hbm, xs_ref, sbuf_ref, seg_smem, prev_smem, sem_seg, sem_rows, *, tm):
    i = pl.program_id(0)
    n = pl.num_programs(0)
    slot = i % 2
    rows = sbuf_ref.shape[1]
    cp = pltpu.make_async_copy(seg_hbm.at[i], seg_smem, sem_seg)
    cp.start()
    lst = lst_ref[...].astype(I32)
    rid = lax.broadcasted_iota(I32, (rows, tm), 0)
    perm = jnp.zeros((rows, tm), F32)
    for kk in range(TOP_K):
        perm = jnp.where(rid == lst[kk:kk + 1, :], 1.0, perm)
    srt = _dot(perm.astype(BF16), h_ref[...])
    sbuf_ref[slot] = _pack_exact_bf16_pairs(srt)
    cp.wait()

    def copy_from(s):
        def make_copy(lo, off, nrows):
            return pltpu.make_async_copy(sbuf_ref.at[s, pl.ds(lo, nrows), :], xs_ref.at[pl.ds(off, nrows), :],
                                         sem_rows.at[s])
        return make_copy

    _segment_starts(lambda k: seg_smem[k], copy_from(slot))

    @pl.when(i > 0)
    def _():
        _segment_waits(prev_smem[0], copy_from(1 - slot))

    prev_smem[0] = seg_smem[3 * LANES]

    @pl.when(i == n - 1)
    def _():
        _segment_waits(prev_smem[0], copy_from(slot))


def _dispatch(h, lst, seg2d, n_rows, tm):
    T = h.shape[0]
    rows = _local_rows(tm)
    return pl.pallas_call(
        functools.partial(_dispatch_kernel, tm=tm),
        grid=(T // tm,),
        in_specs=[
            pl.BlockSpec((tm, D_MODEL), lambda i: (i, 0)),
            pl.BlockSpec((SUBLANES, tm), lambda i: (0, i)),
            pl.BlockSpec(memory_space=pl.ANY),
        ],
        out_specs=pl.BlockSpec(memory_space=pl.ANY),
        out_shape=jax.ShapeDtypeStruct((n_rows, D_MODEL // 2), U32),
        scratch_shapes=[
            pltpu.VMEM((2, rows, D_MODEL // 2), U32),
            pltpu.SMEM((SEG_WORDS,), I32),
            pltpu.SMEM((1,), I32),
            pltpu.SemaphoreType.DMA,
            pltpu.SemaphoreType.DMA((2,)),
        ],
        compiler_params=_cparams(("arbitrary",)),
        name="dispatch",
    )(h, lst, seg2d)


def _experts_kernel(te_ref, nu_ref, nv_ref, xs_ref, wgu_ref, bgu_ref, wdn_ref, bdn_ref, ys_ref, wgu_bf, wdn_bf):
    i = pl.program_id(0)

    @pl.when((i == 0) | (te_ref[i] != te_ref[jnp.maximum(i - 1, 0)]))
    def _():
        wgu_bf[...] = wgu_ref[0].astype(BF16)
        wdn_bf[...] = wdn_ref[0].astype(BF16)

    @pl.when(i < nu_ref[0])
    def _():
        live = lax.broadcasted_iota(I32, xs_ref.shape, 0) < nv_ref[i]
        lo, hi = _unpack_bf16_pairs(jnp.where(live, xs_ref[...], jnp.uint32(0)))
        xb = jnp.concatenate([lo.astype(BF16), hi.astype(BF16)], axis=1)
        gu = _dot(xb, wgu_bf[...]) + bgu_ref[0]
        gate = jnp.minimum(gu[:, :D_FF], SWIGLU_LIMIT)
        up = jnp.clip(gu[:, D_FF:], -SWIGLU_LIMIT, SWIGLU_LIMIT)
        act = (up + 1.0) * (gate * _sigmoid(SWIGLU_ALPHA * gate))
        y = _dot(act.astype(BF16), wdn_bf[...]) + bdn_ref[0]
        ys_ref[...] = _pack_bf16_pairs(y)

    @pl.when(i >= nu_ref[0])
    def _():
        ys_ref[...] = jnp.zeros_like(ys_ref)


def _experts(tile_expert, n_used, tile_valid, xs, w_gu, b_gu, w_dn, b_dn, tg):
    P = xs.shape[0]
    half = D_MODEL // 2
    grid_spec = pltpu.PrefetchScalarGridSpec(
        num_scalar_prefetch=3,
        grid=(P // tg,),
        in_specs=[
            pl.BlockSpec((tg, half), lambda i, te, nu, nv: (jnp.minimum(i, jnp.maximum(nu[0] - 1, 0)), 0)),
            pl.BlockSpec((1, D_MODEL, 2 * D_FF), lambda i, te, nu, nv: (te[i], 0, 0)),
            pl.BlockSpec((1, 1, 2 * D_FF), lambda i, te, nu, nv: (te[i], 0, 0)),
            pl.BlockSpec((1, D_FF, D_MODEL), lambda i, te, nu, nv: (te[i], 0, 0)),
            pl.BlockSpec((1, 1, D_MODEL), lambda i, te, nu, nv: (te[i], 0, 0)),
        ],
        out_specs=pl.BlockSpec((tg, half), lambda i, te, nu, nv: (i, 0)),
        scratch_shapes=[pltpu.VMEM((D_MODEL, 2 * D_FF), BF16), pltpu.VMEM((D_FF, D_MODEL), BF16)],
    )
    return pl.pallas_call(
        _experts_kernel,
        grid_spec=grid_spec,
        out_shape=jax.ShapeDtypeStruct((P, half), U32),
        compiler_params=_cparams(("arbitrary",)),
        name="experts",
    )(tile_expert, n_used, tile_valid, xs, w_gu, b_gu, w_dn, b_dn)


def _combine_kernel(seg_hbm, ys_hbm, ls_ref, tw_ref, x_ref, g_ref, out_ref, ybuf_ref, seg_smem, sem_seg, sem_rows, *,
                    tm, final_norm):
    i = pl.program_id(0)
    n = pl.num_programs(0)
    slot = i % 2
    rows = ybuf_ref.shape[1]

    def copy_into(s):
        def make_copy(lo, off, nrows):
            return pltpu.make_async_copy(ys_hbm.at[pl.ds(off, nrows), :], ybuf_ref.at[s, pl.ds(lo, nrows), :],
                                         sem_rows.at[s])
        return make_copy

    def request(step, s):
        cp = pltpu.make_async_copy(seg_hbm.at[step], seg_smem.at[s], sem_seg)
        cp.start()
        cp.wait()
        _segment_starts(lambda k: seg_smem[s, k], copy_into(s))

    @pl.when(i == 0)
    def _():
        ybuf_ref[...] = jnp.zeros_like(ybuf_ref)
        request(0, 0)

    @pl.when(i + 1 < n)
    def _():
        request(i + 1, 1 - slot)

    ls = ls_ref[...].astype(I32)
    tw = tw_ref[...]
    cid = lax.broadcasted_iota(I32, (tm, rows), 1)
    wmat = jnp.zeros((tm, rows), F32)
    for kk in range(TOP_K):
        wmat = jnp.where(cid == ls[:, kk:kk + 1], tw[:, kk:kk + 1], wmat)
    wmat = wmat.astype(BF16)
    _segment_waits(seg_smem[slot, 3 * LANES], copy_into(slot))
    lo, hi = _unpack_bf16_pairs(ybuf_ref[slot])
    moe = jnp.concatenate([_dot(wmat, lo.astype(BF16)), _dot(wmat, hi.astype(BF16))], axis=1)
    x3 = x_ref[...] + moe
    out_ref[...] = _rms(x3, g_ref[...]) if final_norm else x3


def _combine(seg2d, ys, ls, tw, x2, g, tm, final_norm):
    T = x2.shape[0]
    rows = _local_rows(tm)
    return pl.pallas_call(
        functools.partial(_combine_kernel, tm=tm, final_norm=final_norm),
        grid=(T // tm,),
        in_specs=[
            pl.BlockSpec(memory_space=pl.ANY),
            pl.BlockSpec(memory_space=pl.ANY),
            pl.BlockSpec((tm, LANES), lambda i: (i, 0)),
            pl.BlockSpec((tm, LANES), lambda i: (i, 0)),
            pl.BlockSpec((tm, D_MODEL), lambda i: (i, 0)),
            pl.BlockSpec((1, D_MODEL), lambda i: (0, 0)),
        ],
        out_specs=pl.BlockSpec((tm, D_MODEL), lambda i: (i, 0)),
        out_shape=jax.ShapeDtypeStruct((T, D_MODEL), F32),
        scratch_shapes=[
            pltpu.VMEM((2, rows, D_MODEL // 2), U32),
            pltpu.SMEM((2, SEG_WORDS), I32),
            pltpu.SemaphoreType.DMA,
            pltpu.SemaphoreType.DMA((2,)),
        ],
        compiler_params=_cparams(("arbitrary",)),
        name="combine",
    )(seg2d, ys, ls, tw, x2, g)


def _tiles(B, S):
    T = B * S
    return dict(
        tm_in=min(1024, T), tn_in=2048,
        ts=min(512, S),
        tq=min(1024, S),
        tm_proj=min(512, S),
        tm_route=min(256, T),
        tg=512,
    )


def _pad_lanes(a, n=LANES, value=0.0):
    return jnp.pad(a, ((0, 0), (0, n - a.shape[1])), constant_values=value)


def kernel(x, mem, norm_mix, w_in, conv_w, b_if, mlstm_gain, diff_lambda, diff_gain, w_branch_m, w_branch_d,
           b_gate, w_out, norm_xattn, norm_mem, wq_x, wkv_x, wo_x, norm_ffn, w_router, b_router, w_gu, b_gu,
           w_dn, b_dn, norm_final):
    B, S, D = x.shape
    n_mem = mem.shape[1]
    T = B * S
    depth = norm_mix.shape[0]
    tl = _tiles(B, S)
    x2d = x.reshape(T, D)
    mem2d = mem.reshape(B * n_mem, D)

    for l in range(depth):
        lam_init = 0.8 - 0.6 * math.exp(-0.3 * l)
        wl = w_in[l]
        if_lo = 2 * M_QK + 2 * M_V
        w_main = jnp.concatenate([wl[:, :if_lo], wl[:, if_lo + 2 * M_HEADS:]], axis=1).astype(BF16)
        w_if = wl[:, if_lo:if_lo + 2 * M_HEADS]
        w_ifp = _pad_lanes(w_if).astype(BF16)
        w_ift = w_if.T.astype(BF16)
        bif = _pad_lanes(b_if[l][None, :])
        bift = jnp.broadcast_to(b_if[l][:, None], (SUBLANES, LANES))

        z, zif, zift = _inproj(x2d, norm_mix[l][None, :], w_main, w_ifp, w_ift, tl["tm_in"], tl["tn_in"])
        hm = _mlstm(z, zif, zift, conv_w[l], bif, bift, mlstm_gain[l].reshape(1, M_V), B, S, tl["ts"])
        hd = _diffattn(z, diff_lambda[l], diff_gain[l][None, :], B, S, tl["tq"], lam_init)
        x1 = _merge(hm, hd, z, x2d, w_branch_m[l].astype(BF16), w_branch_d[l].astype(BF16),
                    w_out[l].astype(BF16), b_gate[l][None, :], tl["tm_proj"])

        kvmem = _memkv(mem2d, norm_mem[l][None, :], wkv_x[l].astype(BF16), n_mem)
        x2 = _xattn(x1, norm_xattn[l][None, :], wq_x[l].astype(BF16), kvmem, wo_x[l].astype(BF16),
                    S, n_mem, tl["tm_proj"])

        wr = _pad_lanes(w_router[l])
        wr_hi = wr.astype(BF16)
        wr_lo = (wr - wr_hi.astype(F32)).astype(BF16)
        br = _pad_lanes(b_router[l][None, :], value=-jnp.inf)
        tm_r = tl["tm_route"]
        tg = tl["tg"]
        hp, ids, tw, cnt = _router(x2, norm_ffn[l][None, :], wr_hi, wr_lo, br, tm_r)

        counts = cnt[0, :N_EXPERTS].astype(I32)
        padded = ((counts + tg - 1) // tg) * tg
        ends = jnp.cumsum(padded)
        starts = ends - padded
        max_rows = T * TOP_K + (T // tm_r) * N_EXPERTS * (SUBLANES - 1)
        n_tiles = -(-max_rows // tg) + N_EXPERTS
        tile_row0 = jnp.arange(n_tiles, dtype=I32) * tg
        tile_expert = jnp.minimum(jnp.sum((tile_row0[:, None] >= ends[None, :]).astype(I32), axis=1), N_EXPERTS - 1)
        n_used = (ends[-1] // tg).astype(I32).reshape(1)
        last_used = tile_expert[jnp.maximum(n_used[0] - 1, 0)]
        tile_expert = jnp.where(tile_row0 < ends[-1], tile_expert, last_used)
        tile_valid = jnp.clip((starts + counts)[tile_expert] - tile_row0, 0, tg).astype(I32)

        ls, lst, seg = _slots(ids, _pad_lanes(starts.astype(F32)[None, :]), tm_r)
        seg2d = seg.reshape(T // tm_r, SEG_WORDS)

        xs = _dispatch(hp, lst, seg2d, n_tiles * tg, tm_r)
        ys = _experts(tile_expert, n_used, tile_valid, xs, w_gu[l], b_gu[l][:, None, :], w_dn[l], b_dn[l][:, None, :],
                      tg)
        x2d = _combine(seg2d, ys, ls, tw, x2, norm_final[None, :], tm_r, final_norm=(l == depth - 1))
    return x2d.reshape(B, S, D)
```

```python
import functools
import math

import jax
import jax.numpy as jnp
from jax import lax
from jax.experimental import pallas as pl
from jax.experimental.pallas import tpu as pltpu

F32 = jnp.float32
BF16 = jnp.bfloat16
U32 = jnp.uint32
I32 = jnp.int32

EPS = 1e-6
CHUNK = 64
D_MODEL = 1024
M_HEADS = 4
M_DK = 128
M_DV = 256
M_QK = M_HEADS * M_DK
M_V = M_HEADS * M_DV
CONV_W = 4
D_HEADS = 8
D_DH = 64
D_HP = 2
D_QK = D_HEADS * 2 * D_DH
D_V = D_HEADS * 2 * D_DH
X_HEADS = 4
X_DH = D_MODEL // X_HEADS
N_EXPERTS = 32
TOP_K = 4
D_FF = D_MODEL
SWIGLU_LIMIT = 7.0
SWIGLU_ALPHA = 1.702

LANES = 128
SUBLANES = 8
N_MAIN = 2 * M_QK + 2 * M_V + 2 * D_QK + D_V + 2 * D_MODEL
OFF_QM, OFF_KM, OFF_VM, OFF_OM = 0, M_QK, 2 * M_QK, 2 * M_QK + M_V
OFF_QD = OFF_OM + M_V
OFF_KD = OFF_QD + D_QK
OFF_VD = OFF_KD + D_QK
OFF_G = OFF_VD + D_V

VMEM_LIMIT = 56 * 1024 * 1024


def _cparams(sem, vmem=VMEM_LIMIT):
    return pltpu.CompilerParams(dimension_semantics=sem, vmem_limit_bytes=vmem)


def _rms(x, g):
    return x * lax.rsqrt(jnp.mean(x * x, axis=-1, keepdims=True) + EPS) * g


def _split_bf16(x):
    hi = x.astype(BF16)
    lo = (x - hi.astype(F32)).astype(BF16)
    return hi, lo


def _dot(a, b):
    return jnp.dot(a, b, preferred_element_type=F32)


def _dot_nt(a, b):
    return lax.dot_general(a, b, (((1,), (1,)), ((), ())), preferred_element_type=F32)


def _sigmoid(x):
    return 1.0 / (1.0 + jnp.exp(-x))


def _log_sigmoid(x):
    return jnp.minimum(x, 0.0) - jnp.log(1.0 + jnp.exp(-jnp.abs(x)))


def _pack_bf16_pairs(x):
    w = x.shape[1] // 2
    u = lax.bitcast_convert_type(x, U32)
    r = (u + jnp.uint32(0x7FFF) + ((u >> 16) & jnp.uint32(1))) >> 16
    return r[:, :w] | (r[:, w:] << 16)


def _pack_exact_bf16_pairs(x):
    w = x.shape[1] // 2
    u = lax.bitcast_convert_type(x, U32)
    return (u[:, :w] >> 16) | (u[:, w:] & jnp.uint32(0xFFFF0000))


def _ceil_rows(x):
    return jnp.floor((x + (SUBLANES - 1)) * (1.0 / SUBLANES)) * SUBLANES


def _unpack_bf16_pairs(p):
    lo = lax.bitcast_convert_type(p << 16, F32)
    hi = lax.bitcast_convert_type(p & jnp.uint32(0xFFFF0000), F32)
    return lo, hi


def _inproj_kernel(x_ref, g_ref, w_ref, wif_ref, wift_ref, z_ref, zif_ref, zift_ref, hn_ref):
    @pl.when(pl.program_id(1) == 0)
    def _():
        hn = _rms(x_ref[...], g_ref[...]).astype(BF16)
        hn_ref[...] = hn
        zif_ref[...] = _dot(hn, wif_ref[...])
        zift_ref[...] = _dot_nt(wift_ref[...], hn)

    z_ref[...] = _dot(hn_ref[...], w_ref[...]).astype(BF16)


def _inproj(x2d, g, w_main, w_if, w_ift, tm, tn):
    T = x2d.shape[0]
    return pl.pallas_call(
        _inproj_kernel,
        grid=(T // tm, N_MAIN // tn),
        in_specs=[
            pl.BlockSpec((tm, D_MODEL), lambda i, j: (i, 0)),
            pl.BlockSpec((1, D_MODEL), lambda i, j: (0, 0)),
            pl.BlockSpec((D_MODEL, tn), lambda i, j: (0, j)),
            pl.BlockSpec((D_MODEL, LANES), lambda i, j: (0, 0)),
            pl.BlockSpec((SUBLANES, D_MODEL), lambda i, j: (0, 0)),
        ],
        out_specs=[
            pl.BlockSpec((tm, tn), lambda i, j: (i, j)),
            pl.BlockSpec((tm, LANES), lambda i, j: (i, 0)),
            pl.BlockSpec((SUBLANES, tm), lambda i, j: (0, i)),
        ],
        out_shape=[
            jax.ShapeDtypeStruct((T, N_MAIN), BF16),
            jax.ShapeDtypeStruct((T, LANES), F32),
            jax.ShapeDtypeStruct((SUBLANES, T), F32),
        ],
        scratch_shapes=[pltpu.VMEM((tm, D_MODEL), BF16)],
        compiler_params=_cparams(("arbitrary", "arbitrary")),
        name="inproj",
    )(x2d, g, w_main, w_if, w_ift)


def _mlstm_kernel(q_ref, k_ref, v_ref, om_ref, zif_ref, zift_ref, cw_ref, bif_ref, bift_ref, mg_ref,
                  out_ref, qc_ref, kc_ref, kt_ref, carry_ref, c_ref, n_ref, m_ref,
                  bd_ref, bdt_ref, brep_ref, grow_ref, brow_ref, *, ts):
    nchunk = ts // CHUNK
    L = CHUNK

    @pl.when(pl.program_id(1) == 0)
    def _():
        carry_ref[...] = jnp.zeros_like(carry_ref)
        c_ref[...] = jnp.zeros_like(c_ref)
        n_ref[...] = jnp.zeros_like(n_ref)
        m_ref[...] = jnp.zeros_like(m_ref)
        rt = lax.broadcasted_iota(I32, (ts, ts), 0)
        ct = lax.broadcasted_iota(I32, (ts, ts), 1)
        same = (rt // L) == (ct // L)
        bd_ref[...] = jnp.where(same, jnp.where(ct <= rt, 1.0, 0.0), 0.0).astype(BF16)
        bdt_ref[...] = jnp.where(same, jnp.where(rt <= ct, 1.0, 0.0), 0.0).astype(BF16)

    row8 = lax.broadcasted_iota(I32, (SUBLANES, M_QK), 0)

    def conv_silu(x, prev8, w):
        acc = w[CONV_W - 1:CONV_W, :] * x
        for s in range(1, CONV_W):
            xs = pltpu.roll(x, s, 0)
            top = jnp.where(row8 < s, pltpu.roll(prev8, s, 0), xs[0:SUBLANES])
            xs = jnp.concatenate([top, xs[SUBLANES:]], axis=0)
            acc = acc + w[CONV_W - 1 - s:CONV_W - s, :] * xs
        return acc * _sigmoid(acc)

    def conv_body(c, carry):
        r0 = pl.multiple_of(c * L, L)
        xq = q_ref[pl.ds(r0, L), :].astype(F32)
        xk = k_ref[pl.ds(r0, L), :].astype(F32)
        yq = conv_silu(xq, carry_ref[:, 0:M_QK], cw_ref[:, 0:M_QK]) * (M_DK ** -0.5)
        yk = conv_silu(xk, carry_ref[:, M_QK:2 * M_QK], cw_ref[:, M_QK:2 * M_QK])
        qc_ref[pl.ds(r0, L), :] = yq.astype(BF16)
        kc_ref[pl.ds(r0, L), :] = yk.astype(BF16)
        for h in range(M_HEADS):
            kt_ref[c, h] = yk[:, h * M_DK:(h + 1) * M_DK].T
        carry_ref[:, 0:M_QK] = xq[L - SUBLANES:L]
        carry_ref[:, M_QK:2 * M_QK] = xk[L - SUBLANES:L]
        return carry

    lax.fori_loop(0, nchunk, conv_body, 0, unroll=4)

    ti = lax.broadcasted_iota(I32, (L, L), 0)
    si = lax.broadcasted_iota(I32, (L, L), 1)
    causal = si <= ti
    lane_row = lax.broadcasted_iota(I32, (LANES, LANES), 0)
    ones_l = jnp.ones((L, LANES), BF16)

    lf_col = _log_sigmoid(zif_ref[...] + bif_ref[...])
    ch, cl = _split_bf16(lf_col)
    bd = bd_ref[...]
    b_col_all = _dot(jnp.concatenate([bd, bd], axis=1), jnp.concatenate([ch, cl], axis=0))
    bhl = jnp.concatenate(_split_bf16(b_col_all), axis=1)
    for h in range(M_HEADS):
        sel_f = jnp.where(lane_row == M_HEADS + h, 1.0, 0.0).astype(BF16)
        brep_ref[h] = _dot(bhl, jnp.concatenate([sel_f, sel_f], axis=0))
    g_row_all = zift_ref[...] + bift_ref[:, 0:1]
    rhl = jnp.concatenate(_split_bf16(_log_sigmoid(g_row_all)), axis=1)
    bdt = bdt_ref[...]
    b_row_tile = _dot(rhl, jnp.concatenate([bdt, bdt], axis=0))
    for cc in range(nchunk):
        grow_ref[cc] = g_row_all[:, cc * L:(cc + 1) * L]
        brow_ref[cc] = b_row_tile[:, cc * L:(cc + 1) * L]

    def chunk_body(c, carry):
        r0 = pl.multiple_of(c * L, L)
        g_row = grow_ref[c]
        b_row_all = brow_ref[c]
        early = []
        for h in range(M_HEADS):
            b_rep = brep_ref[h, pl.ds(r0, L), :]
            i_row = g_row[h:h + 1, :]
            b_row = b_row_all[M_HEADS + h:M_HEADS + h + 1, :]
            b_last = b_rep[L - 1:L, :]
            q = qc_ref[pl.ds(r0, L), h * M_DK:(h + 1) * M_DK]
            k = kc_ref[pl.ds(r0, L), h * M_DK:(h + 1) * M_DK]
            vext = jnp.concatenate([v_ref[pl.ds(r0, L), h * M_DV:(h + 1) * M_DV], ones_l], axis=1)
            dm = jnp.where(causal, b_rep[:, :L] - b_row + i_row, -jnp.inf)
            m_loc = jnp.max(dm, axis=-1, keepdims=True)
            qk = _dot_nt(q, k)
            gk_row = b_last[:, :L] - b_row + i_row
            g_max = jnp.max(gk_row, axis=-1, keepdims=True)
            kwt = (kt_ref[c, h] * jnp.exp(gk_row - g_max)).astype(BF16)
            kv = _dot(kwt, vext)
            c_old = c_ref[h]
            n_old = n_ref[h]
            qcn = _dot(q, jnp.concatenate([c_old, n_old], axis=1).astype(BF16))
            early.append((b_rep, b_last, vext, dm, m_loc, qk, g_max, kv, c_old, n_old, qcn))
        pvs = []
        for h in range(M_HEADS):
            b_rep, b_last, vext, dm, m_loc, qk, g_max, kv, c_old, n_old, qcn = early[h]
            s_loc = qk * jnp.exp(dm - m_loc)
            pvs.append(_dot(s_loc.astype(BF16), vext))
        for h in range(M_HEADS):
            b_rep, b_last, vext, dm, m_loc, qk, g_max, kv, c_old, n_old, qcn = early[h]
            pv = pvs[h]
            m_prev = m_ref[h:h + 1, :]
            inter = b_rep + m_prev
            m_t = jnp.maximum(inter, m_loc)
            w_inter = jnp.exp(inter - m_t)
            r_loc = jnp.exp(m_loc - m_t)
            den = r_loc * pv[:, M_DV:] + w_inter * qcn[:, M_DV:]
            inv = 1.0 / jnp.maximum(jnp.abs(den), jnp.exp(-m_t))
            hv = (jnp.concatenate([r_loc * inv] * 2, axis=1) * pv[:, :M_DV]
                  + jnp.concatenate([w_inter * inv] * 2, axis=1) * qcn[:, :M_DV])
            m_new = jnp.maximum(b_last + m_prev, g_max)
            decay = jnp.exp(b_last + m_prev - m_new)
            sc_loc = jnp.exp(g_max - m_new)
            c_ref[h] = (jnp.concatenate([decay] * 2, axis=1) * c_old
                        + jnp.concatenate([sc_loc] * 2, axis=1) * kv[:, :M_DV])
            n_ref[h] = decay * n_old + sc_loc * kv[:, M_DV:]
            m_ref[h:h + 1, :] = m_new
            hn = _rms(hv, mg_ref[:, h * M_DV:(h + 1) * M_DV])
            og = _sigmoid(om_ref[pl.ds(r0, L), h * M_DV:(h + 1) * M_DV].astype(F32))
            out_ref[pl.ds(r0, L), h * M_DV:(h + 1) * M_DV] = (og * hn).astype(BF16)
        return carry

    lax.fori_loop(0, nchunk, chunk_body, 0, unroll=4)


def _mlstm(z, zif, zift, conv_w, bif, bift, m_gain, B, S, ts):
    T = B * S
    nt = S // ts
    nck = ts // CHUNK
    row = lambda b, t: b * nt + t
    return pl.pallas_call(
        functools.partial(_mlstm_kernel, ts=ts),
        grid=(B, nt),
        in_specs=[
            pl.BlockSpec((ts, M_QK), lambda b, t: (row(b, t), OFF_QM // M_QK)),
            pl.BlockSpec((ts, M_QK), lambda b, t: (row(b, t), OFF_KM // M_QK)),
            pl.BlockSpec((ts, M_V), lambda b, t: (row(b, t), OFF_VM // M_V)),
            pl.BlockSpec((ts, M_V), lambda b, t: (row(b, t), OFF_OM // M_V)),
            pl.BlockSpec((ts, LANES), lambda b, t: (row(b, t), 0)),
            pl.BlockSpec((SUBLANES, ts), lambda b, t: (0, row(b, t))),
            pl.BlockSpec((CONV_W, 2 * M_QK), lambda b, t: (0, 0)),
            pl.BlockSpec((1, LANES), lambda b, t: (0, 0)),
            pl.BlockSpec((SUBLANES, LANES), lambda b, t: (0, 0)),
            pl.BlockSpec((1, M_V), lambda b, t: (0, 0)),
        ],
        out_specs=pl.BlockSpec((ts, M_V), lambda b, t: (row(b, t), 0)),
        out_shape=jax.ShapeDtypeStruct((T, M_V), BF16),
        scratch_shapes=[
            pltpu.VMEM((ts, M_QK), BF16),
            pltpu.VMEM((ts, M_QK), BF16),
            pltpu.VMEM((nck, M_HEADS, M_DK, CHUNK), F32),
            pltpu.VMEM((SUBLANES, 2 * M_QK), F32),
            pltpu.VMEM((M_HEADS, M_DK, M_DV), F32),
            pltpu.VMEM((M_HEADS, M_DK, LANES), F32),
            pltpu.VMEM((SUBLANES, LANES), F32),
            pltpu.VMEM((ts, ts), BF16),
            pltpu.VMEM((ts, ts), BF16),
            pltpu.VMEM((M_HEADS, ts, LANES), F32),
            pltpu.VMEM((nck, SUBLANES, CHUNK), F32),
            pltpu.VMEM((nck, SUBLANES, CHUNK), F32),
        ],
        compiler_params=_cparams(("arbitrary", "arbitrary")),
        name="mlstm",
    )(z, z, z, z, zif, zift, conv_w, bif, bift, m_gain)


def _diffattn_kernel(q_ref, k_ref, v_ref, lam_ref, gain_ref, out_ref, m_ref, a_ref, *, tq, lam_init):
    qi = pl.program_id(2)
    w = 2 * D_DH
    lane = lax.broadcasted_iota(I32, (1, w), 1)
    scale = jnp.asarray(D_DH ** -0.5, BF16)
    qs = []
    for hh in range(D_HP):
        q = q_ref[:, hh * w:(hh + 1) * w]
        qs.append((jnp.where(lane < D_DH, q, jnp.zeros_like(q)) * scale,
                   jnp.where(lane >= D_DH, q, jnp.zeros_like(q)) * scale))
    ones = jnp.ones((tq, w), BF16)

    def block(k0, nk, rows=slice(None), mask=None, first=False):
        n_stream = 2 * D_HP
        scores = []
        for hh in range(D_HP):
            k = k_ref[pl.ds(k0, nk), hh * w:(hh + 1) * w]
            for comp in range(2):
                s = _dot_nt(qs[hh][comp][rows], k)
                scores.append(s if mask is None else jnp.where(mask, s, -jnp.inf))
        m_news, alphas, probs = [], [], []
        for i in range(n_stream):
            s = scores[i]
            s_max = jnp.max(s, axis=-1, keepdims=True)
            if first:
                m_new = jnp.broadcast_to(s_max, (s.shape[0], w))
                alphas.append(None)
            else:
                m_old = m_ref[i, rows, :]
                m_new = jnp.maximum(m_old, s_max)
                alphas.append(jnp.exp(m_old - m_new))
            m_news.append(m_new)
            probs.append(jnp.exp(s - jnp.concatenate([m_new] * (s.shape[1] // w), axis=1)).astype(BF16))
        pvs = []
        for hh in range(D_HP):
            vext = jnp.concatenate([v_ref[pl.ds(k0, nk), hh * w:(hh + 1) * w], ones[:nk]], axis=1)
            for comp in range(2):
                pvs.append(_dot(probs[2 * hh + comp], vext))
        for i in range(n_stream):
            if first:
                a_ref[i, rows, :] = pvs[i]
            else:
                a_ref[i, rows, :] = jnp.concatenate([alphas[i]] * 2, axis=1) * a_ref[i, rows, :] + pvs[i]
            m_ref[i, rows, :] = m_news[i]

    hq = tq // 2
    d0 = pl.multiple_of(qi * tq, tq)

    def chunk_mask(q0, nk):
        rq = (lax.broadcasted_iota(I32, (hq, nk), 0) + q0) // CHUNK
        ck = lax.broadcasted_iota(I32, (hq, nk), 1) // CHUNK
        return ck <= rq

    block(d0, hq, rows=slice(0, hq), mask=chunk_mask(0, hq), first=True)
    block(d0, tq, rows=slice(hq, tq), mask=chunk_mask(hq, tq), first=True)

    def body(jj, carry):
        block(pl.multiple_of(jj * tq, tq), tq)
        return carry

    lax.fori_loop(0, qi, body, 0)

    lp = lam_ref[...]
    lam = (jnp.exp(jnp.sum(lp[0:1, :] * lp[1:2, :], axis=-1, keepdims=True))
           - jnp.exp(jnp.sum(lp[2:3, :] * lp[3:4, :], axis=-1, keepdims=True)) + lam_init)
    for hh in range(D_HP):
        a1 = a_ref[2 * hh]
        a2 = a_ref[2 * hh + 1]
        o = a1[:, :w] / a1[:, w:] - lam * (a2[:, :w] / a2[:, w:])
        out_ref[:, hh * w:(hh + 1) * w] = (_rms(o, gain_ref[...]) * (1.0 - lam_init)).astype(BF16)


def _diffattn(z, lam_p, d_gain, B, S, tq, lam_init):
    T = B * S
    nq = S // tq
    w = 2 * D_DH
    wp = D_HP * w
    return pl.pallas_call(
        functools.partial(_diffattn_kernel, tq=tq, lam_init=lam_init),
        grid=(B, D_HEADS // D_HP, nq),
        in_specs=[
            pl.BlockSpec((tq, wp), lambda b, h, i: (b * nq + i, OFF_QD // wp + h)),
            pl.BlockSpec((S, wp), lambda b, h, i: (b, OFF_KD // wp + h)),
            pl.BlockSpec((S, wp), lambda b, h, i: (b, OFF_VD // wp + h)),
            pl.BlockSpec((4, D_DH), lambda b, h, i: (0, 0)),
            pl.BlockSpec((1, w), lambda b, h, i: (0, 0)),
        ],
        out_specs=pl.BlockSpec((tq, wp), lambda b, h, i: (b * nq + i, h)),
        out_shape=jax.ShapeDtypeStruct((T, D_V), BF16),
        scratch_shapes=[
            pltpu.VMEM((2 * D_HP, tq, w), F32), pltpu.VMEM((2 * D_HP, tq, 2 * w), F32),
        ],
        compiler_params=_cparams(("arbitrary", "arbitrary", "arbitrary")),
        name="diffattn",
    )(z, z, z, lam_p, d_gain)


def _merge_kernel(hm_ref, hd_ref, gz_ref, x_ref, wbm_ref, wbd_ref, wout_ref, bg_ref, out_ref):
    bm = _dot(hm_ref[...], wbm_ref[...])
    bd = _dot(hd_ref[...], wbd_ref[...])
    g = _sigmoid(gz_ref[...].astype(F32) + bg_ref[...])
    merged = g[:, :D_MODEL] * bm + g[:, D_MODEL:] * bd
    out_ref[...] = x_ref[...] + _dot(merged.astype(BF16), wout_ref[...])


def _merge(hm, hd, z, x2d, w_bm, w_bd, w_out, b_gate, tm):
    T = x2d.shape[0]
    full = lambda i: (0, 0)
    return pl.pallas_call(
        _merge_kernel,
        grid=(T // tm,),
        in_specs=[
            pl.BlockSpec((tm, M_V), lambda i: (i, 0)),
            pl.BlockSpec((tm, D_V), lambda i: (i, 0)),
            pl.BlockSpec((tm, 2 * D_MODEL), lambda i: (i, OFF_G // (2 * D_MODEL))),
            pl.BlockSpec((tm, D_MODEL), lambda i: (i, 0)),
            pl.BlockSpec((M_V, D_MODEL), full),
            pl.BlockSpec((D_V, D_MODEL), full),
            pl.BlockSpec((D_MODEL, D_MODEL), full),
            pl.BlockSpec((1, 2 * D_MODEL), full),
        ],
        out_specs=pl.BlockSpec((tm, D_MODEL), lambda i: (i, 0)),
        out_shape=jax.ShapeDtypeStruct((T, D_MODEL), F32),
        compiler_params=_cparams(("arbitrary",)),
        name="merge",
    )(hm, hd, z, x2d, w_bm, w_bd, w_out, b_gate)


def _memkv_kernel(mem_ref, g_ref, w_ref, out_ref):
    out_ref[...] = _dot(_rms(mem_ref[...], g_ref[...]).astype(BF16), w_ref[...]).astype(BF16)


def _memkv(mem2d, g, wkv, n_mem):
    R = mem2d.shape[0]
    return pl.pallas_call(
        _memkv_kernel,
        grid=(R // n_mem,),
        in_specs=[
            pl.BlockSpec((n_mem, D_MODEL), lambda i: (i, 0)),
            pl.BlockSpec((1, D_MODEL), lambda i: (0, 0)),
            pl.BlockSpec((D_MODEL, 2 * D_MODEL), lambda i: (0, 0)),
        ],
        out_specs=pl.BlockSpec((n_mem, 2 * D_MODEL), lambda i: (i, 0)),
        out_shape=jax.ShapeDtypeStruct((R, 2 * D_MODEL), BF16),
        compiler_params=_cparams(("arbitrary",)),
        name="memkv",
    )(mem2d, g, wkv)


def _xattn_kernel(x_ref, g_ref, wq_ref, kv_ref, wo_ref, out_ref, o_ref):
    x = x_ref[...]
    h = _rms(x, g_ref[...]).astype(BF16)
    q = (_dot(h, wq_ref[...]) * (X_DH ** -0.5)).astype(BF16)
    scores = [_dot_nt(q[:, hd * X_DH:(hd + 1) * X_DH], kv_ref[:, hd * X_DH:(hd + 1) * X_DH]) for hd in range(X_HEADS)]
    for hd in range(X_HEADS):
        vh = kv_ref[:, D_MODEL + hd * X_DH:D_MODEL + (hd + 1) * X_DH]
        s = scores[hd]
        p = jnp.exp(s - jnp.max(s, axis=-1, keepdims=True))
        p = p / jnp.sum(p, axis=-1, keepdims=True)
        o_ref[:, hd * X_DH:(hd + 1) * X_DH] = _dot(p.astype(BF16), vh).astype(BF16)
    out_ref[...] = x + _dot(o_ref[...], wo_ref[...])


def _xattn(x1, g, wq, kvmem, wo, S, n_mem, tm):
    T = x1.shape[0]
    per_b = S // tm
    full = lambda i: (0, 0)
    return pl.pallas_call(
        _xattn_kernel,
        grid=(T // tm,),
        in_specs=[
            pl.BlockSpec((tm, D_MODEL), lambda i: (i, 0)),
            pl.BlockSpec((1, D_MODEL), full),
            pl.BlockSpec((D_MODEL, D_MODEL), full),
            pl.BlockSpec((n_mem, 2 * D_MODEL), lambda i: (i // per_b, 0)),
            pl.BlockSpec((D_MODEL, D_MODEL), full),
        ],
        out_specs=pl.BlockSpec((tm, D_MODEL), lambda i: (i, 0)),
        out_shape=jax.ShapeDtypeStruct((T, D_MODEL), F32),
        scratch_shapes=[pltpu.VMEM((tm, D_MODEL), BF16)],
        compiler_params=_cparams(("arbitrary",)),
        name="xattn",
    )(x1, g, wq, kvmem, wo)


def _router_kernel(x_ref, g_ref, wrh_ref, wrl_ref, br_ref, hp_ref, ids_ref, tw_ref, cnt_ref):
    @pl.when(pl.program_id(0) == 0)
    def _():
        cnt_ref[...] = jnp.zeros_like(cnt_ref)

    hn = _rms(x_ref[...], g_ref[...])
    hh, hl = _split_bf16(hn)
    hp_ref[...] = hh
    logits = _dot(hh, wrh_ref[...]) + _dot(hh, wrl_ref[...]) + _dot(hl, wrh_ref[...]) + br_ref[...]
    lane = lax.broadcasted_iota(I32, logits.shape, 1)
    lanef = lane.astype(F32)
    ids = jnp.zeros(logits.shape, F32)
    tw = jnp.zeros(logits.shape, F32)
    onehot = jnp.zeros(logits.shape, F32)
    v0 = None
    den = None
    for kk in range(TOP_K):
        mx = jnp.max(logits, axis=-1, keepdims=True)
        idx = jnp.min(jnp.where(logits == mx, lanef, float(LANES)), axis=-1, keepdims=True)
        sel = lanef == idx
        if kk == 0:
            v0 = mx
        e = jnp.exp(mx - v0)
        den = e if den is None else den + e
        ids = jnp.where(lane == kk, idx, ids)
        tw = jnp.where(lane == kk, e, tw)
        onehot = jnp.where(sel, 1.0, onehot)
        logits = jnp.where(sel, -jnp.inf, logits)
    ids_ref[...] = ids.astype(I32)
    tw_ref[...] = tw / den
    cnt_ref[...] += _ceil_rows(jnp.sum(onehot, axis=0, keepdims=True))


def _router(x2, g, wr_hi, wr_lo, b_r, tm):
    T = x2.shape[0]
    full = lambda i: (0, 0)
    return pl.pallas_call(
        _router_kernel,
        grid=(T // tm,),
        in_specs=[
            pl.BlockSpec((tm, D_MODEL), lambda i: (i, 0)),
            pl.BlockSpec((1, D_MODEL), full),
            pl.BlockSpec((D_MODEL, LANES), full),
            pl.BlockSpec((D_MODEL, LANES), full),
            pl.BlockSpec((1, LANES), full),
        ],
        out_specs=[
            pl.BlockSpec((tm, D_MODEL), lambda i: (i, 0)),
            pl.BlockSpec((tm, LANES), lambda i: (i, 0)),
            pl.BlockSpec((tm, LANES), lambda i: (i, 0)),
            pl.BlockSpec((1, LANES), full),
        ],
        out_shape=[
            jax.ShapeDtypeStruct((T, D_MODEL), BF16),
            jax.ShapeDtypeStruct((T, LANES), I32),
            jax.ShapeDtypeStruct((T, LANES), F32),
            jax.ShapeDtypeStruct((1, LANES), F32),
        ],
        compiler_params=_cparams(("arbitrary",)),
        name="router",
    )(x2, g, wr_hi, wr_lo, b_r)


SEG_WORDS = SUBLANES * LANES


def _slots_kernel(ids_ref, start_ref, ls_ref, lst_ref, seg_ref, run_ref):
    @pl.when(pl.program_id(0) == 0)
    def _():
        run_ref[...] = jnp.zeros_like(run_ref)

    ids = ids_ref[...]
    tm = ids.shape[0]
    lane = lax.broadcasted_iota(I32, ids.shape, 1)
    sels = [lane == ids[:, kk:kk + 1] for kk in range(TOP_K)]
    onehot = jnp.zeros(ids.shape, F32)
    for s in sels:
        onehot = jnp.where(s, 1.0, onehot)
    c8 = _ceil_rows(jnp.sum(onehot, axis=0, keepdims=True))
    er = lax.broadcasted_iota(I32, (LANES, LANES), 0)
    ec = lax.broadcasted_iota(I32, (LANES, LANES), 1)
    before = jnp.where(er < ec, 1.0, 0.0).astype(BF16)
    pieces = jnp.broadcast_to(c8 * (1.0 / SUBLANES), (SUBLANES, LANES)).astype(BF16)
    lo = _dot(pieces, before)[0:1, :] * SUBLANES
    r = lax.broadcasted_iota(I32, (tm, tm), 0)
    c = lax.broadcasted_iota(I32, (tm, tm), 1)
    strict = jnp.where(c < r, 1.0, 0.0).astype(BF16)
    slot = _dot(strict, onehot.astype(BF16)) + lo
    ls = jnp.zeros(ids.shape, F32)
    for kk, s in enumerate(sels):
        pk = jnp.sum(jnp.where(s, slot, 0.0), axis=-1, keepdims=True)
        ls = jnp.where(lane == kk, pk, ls)
    ls_ref[...] = ls
    hi = jnp.floor(ls * (1.0 / 32.0))
    rem = ls - 32.0 * hi
    pick = jnp.where(lax.broadcasted_iota(I32, (SUBLANES, LANES), 0) == lax.broadcasted_iota(I32, (SUBLANES, LANES), 1),
                     1.0, 0.0).astype(BF16)
    lst_ref[...] = 32.0 * _dot_nt(pick, hi.astype(BF16)) + _dot_nt(pick, rem.astype(BF16))
    row = lax.broadcasted_iota(I32, (SUBLANES, LANES), 0)
    off = start_ref[...] + run_ref[...]
    total = jnp.sum(c8, axis=-1, keepdims=True)
    seg = jnp.where(row == 0, c8, jnp.where(row == 1, lo, jnp.where(row == 2, off, jnp.where(row == 3, total, 0.0))))
    seg_ref[...] = seg.astype(I32)
    run_ref[...] += c8


def _slots(ids, starts, tm):
    T = ids.shape[0]
    nt = T // tm
    return pl.pallas_call(
        _slots_kernel,
        grid=(nt,),
        in_specs=[
            pl.BlockSpec((tm, LANES), lambda i: (i, 0)),
            pl.BlockSpec((1, LANES), lambda i: (0, 0)),
        ],
        out_specs=[
            pl.BlockSpec((tm, LANES), lambda i: (i, 0)),
            pl.BlockSpec((SUBLANES, tm), lambda i: (0, i)),
            pl.BlockSpec((SUBLANES, LANES), lambda i: (i, 0)),
        ],
        out_shape=[
            jax.ShapeDtypeStruct((T, LANES), F32),
            jax.ShapeDtypeStruct((SUBLANES, T), F32),
            jax.ShapeDtypeStruct((nt * SUBLANES, LANES), I32),
        ],
        scratch_shapes=[pltpu.VMEM((1, LANES), F32)],
        compiler_params=_cparams(("arbitrary",)),
        name="slots",
    )(ids, starts)


def _local_rows(tm):
    need = tm * TOP_K + N_EXPERTS * (SUBLANES - 1)
    return ((need + LANES - 1) // LANES) * LANES


BIG_PIECE = 4 * SUBLANES


def _segment_starts(seg, make_copy):
    def expert(e, carry):
        cnt = seg(e)
        lo = seg(LANES + e)
        off = seg(2 * LANES + e)
        n_big = lax.shift_right_logical(cnt, 5)
        n_small = lax.shift_right_logical(cnt & (BIG_PIECE - 1), 3)

        def big(j, carry2):
            d = j * BIG_PIECE
            make_copy(pl.multiple_of(lo + d, SUBLANES), pl.multiple_of(off + d, SUBLANES), BIG_PIECE).start()
            return carry2

        def small(j, carry2):
            d = n_big * BIG_PIECE + j * SUBLANES
            make_copy(pl.multiple_of(lo + d, SUBLANES), pl.multiple_of(off + d, SUBLANES), SUBLANES).start()
            return carry2

        lax.fori_loop(0, n_big, big, 0)
        lax.fori_loop(0, n_small, small, 0)
        return carry

    for e in range(N_EXPERTS):
        expert(e, 0)


def _segment_waits(total_rows, make_copy):
    def big(j, carry):
        make_copy(0, 0, BIG_PIECE).wait()
        return carry

    def small(j, carry):
        make_copy(0, 0, SUBLANES).wait()
        return carry

    lax.fori_loop(0, lax.shift_right_logical(total_rows, 5), big, 0)
    lax.fori_loop(0, lax.shift_right_logical(total_rows & (BIG_PIECE - 1), 3), small, 0)


def _dispatch_kernel(h_ref, lst_ref, seg_hbm, xs_ref, sbuf_ref, seg_smem, prev_smem, sem_seg, sem_rows, *, tm):
    i = pl.program_id(0)
    n = pl.num_programs(0)
    slot = i % 2
    rows = sbuf_ref.shape[1]
    cp = pltpu.make_async_copy(seg_hbm.at[i], seg_smem, sem_seg)
    cp.start()
    lst = lst_ref[...].astype(I32)
    rid = lax.broadcasted_iota(I32, (rows, tm), 0)
    perm = jnp.zeros((rows, tm), F32)
    for kk in range(TOP_K):
        perm = jnp.where(rid == lst[kk:kk + 1, :], 1.0, perm)
    srt = _dot(perm.astype(BF16), h_ref[...])
    sbuf_ref[slot] = _pack_exact_bf16_pairs(srt)
    cp.wait()

    def copy_from(s):
        def make_copy(lo, off, nrows):
            return pltpu.make_async_copy(sbuf_ref.at[s, pl.ds(lo, nrows), :], xs_ref.at[pl.ds(off, nrows), :],
                                         sem_rows.at[s])
        return make_copy

    _segment_starts(lambda k: seg_smem[k], copy_from(slot))

    @pl.when(i > 0)
    def _():
        _segment_waits(prev_smem[0], copy_from(1 - slot))

    prev_smem[0] = seg_smem[3 * LANES]

    @pl.when(i == n - 1)
    def _():
        _segment_waits(prev_smem[0], copy_from(slot))


def _dispatch(h, lst, seg2d, n_rows, tm):
    T = h.shape[0]
    rows = _local_rows(tm)
    return pl.pallas_call(
        functools.partial(_dispatch_kernel, tm=tm),
        grid=(T // tm,),
        in_specs=[
            pl.BlockSpec((tm, D_MODEL), lambda i: (i, 0)),
            pl.BlockSpec((SUBLANES, tm), lambda i: (0, i)),
            pl.BlockSpec(memory_space=pl.ANY),
        ],
        out_specs=pl.BlockSpec(memory_space=pl.ANY),
        out_shape=jax.ShapeDtypeStruct((n_rows, D_MODEL // 2), U32),
        scratch_shapes=[
            pltpu.VMEM((2, rows, D_MODEL // 2), U32),
            pltpu.SMEM((SEG_WORDS,), I32),
            pltpu.SMEM((1,), I32),
            pltpu.SemaphoreType.DMA,
            pltpu.SemaphoreType.DMA((2,)),
        ],
        compiler_params=_cparams(("arbitrary",)),
        name="dispatch",
    )(h, lst, seg2d)


def _experts_kernel(te_ref, nu_ref, nv_ref, xs_ref, wgu_ref, bgu_ref, wdn_ref, bdn_ref, ys_ref, wgu_bf, wdn_bf):
    i = pl.program_id(0)

    @pl.when((i == 0) | (te_ref[i] != te_ref[jnp.maximum(i - 1, 0)]))
    def _():
        wgu_bf[...] = wgu_ref[0].astype(BF16)
        wdn_bf[...] = wdn_ref[0].astype(BF16)

    @pl.when(i < nu_ref[0])
    def _():
        live = lax.broadcasted_iota(I32, xs_ref.shape, 0) < nv_ref[i]
        lo, hi = _unpack_bf16_pairs(jnp.where(live, xs_ref[...], jnp.uint32(0)))
        xb = jnp.concatenate([lo.astype(BF16), hi.astype(BF16)], axis=1)
        gu = _dot(xb, wgu_bf[...]) + bgu_ref[0]
        gate = jnp.minimum(gu[:, :D_FF], SWIGLU_LIMIT)
        up = jnp.clip(gu[:, D_FF:], -SWIGLU_LIMIT, SWIGLU_LIMIT)
        act = (up + 1.0) * (gate * _sigmoid(SWIGLU_ALPHA * gate))
        y = _dot(act.astype(BF16), wdn_bf[...]) + bdn_ref[0]
        ys_ref[...] = _pack_bf16_pairs(y)

    @pl.when(i >= nu_ref[0])
    def _():
        ys_ref[...] = jnp.zeros_like(ys_ref)


def _experts(tile_expert, n_used, tile_valid, xs, w_gu, b_gu, w_dn, b_dn, tg):
    P = xs.shape[0]
    half = D_MODEL // 2
    grid_spec = pltpu.PrefetchScalarGridSpec(
        num_scalar_prefetch=3,
        grid=(P // tg,),
        in_specs=[
            pl.BlockSpec((tg, half), lambda i, te, nu, nv: (jnp.minimum(i, jnp.maximum(nu[0] - 1, 0)), 0)),
            pl.BlockSpec((1, D_MODEL, 2 * D_FF), lambda i, te, nu, nv: (te[i], 0, 0)),
            pl.BlockSpec((1, 1, 2 * D_FF), lambda i, te, nu, nv: (te[i], 0, 0)),
            pl.BlockSpec((1, D_FF, D_MODEL), lambda i, te, nu, nv: (te[i], 0, 0)),
            pl.BlockSpec((1, 1, D_MODEL), lambda i, te, nu, nv: (te[i], 0, 0)),
        ],
        out_specs=pl.BlockSpec((tg, half), lambda i, te, nu, nv: (i, 0)),
        scratch_shapes=[pltpu.VMEM((D_MODEL, 2 * D_FF), BF16), pltpu.VMEM((D_FF, D_MODEL), BF16)],
    )
    return pl.pallas_call(
        _experts_kernel,
        grid_spec=grid_spec,
        out_shape=jax.ShapeDtypeStruct((P, half), U32),
        compiler_params=_cparams(("arbitrary",)),
        name="experts",
    )(tile_expert, n_used, tile_valid, xs, w_gu, b_gu, w_dn, b_dn)


def _combine_kernel(seg_hbm, ys_hbm, ls_ref, tw_ref, x_ref, g_ref, out_ref, ybuf_ref, seg_smem, sem_seg, sem_rows, *,
                    tm, final_norm):
    i = pl.program_id(0)
    n = pl.num_programs(0)
    slot = i % 2
    rows = ybuf_ref.shape[1]

    def copy_into(s):
        def make_copy(lo, off, nrows):
            return pltpu.make_async_copy(ys_hbm.at[pl.ds(off, nrows), :], ybuf_ref.at[s, pl.ds(lo, nrows), :],
                                         sem_rows.at[s])
        return make_copy

    def request(step, s):
        cp = pltpu.make_async_copy(seg_hbm.at[step], seg_smem.at[s], sem_seg)
        cp.start()
        cp.wait()
        _segment_starts(lambda k: seg_smem[s, k], copy_into(s))

    @pl.when(i == 0)
    def _():
        ybuf_ref[...] = jnp.zeros_like(ybuf_ref)
        request(0, 0)

    @pl.when(i + 1 < n)
    def _():
        request(i + 1, 1 - slot)

    ls = ls_ref[...].astype(I32)
    tw = tw_ref[...]
    cid = lax.broadcasted_iota(I32, (tm, rows), 1)
    wmat = jnp.zeros((tm, rows), F32)
    for kk in range(TOP_K):
        wmat = jnp.where(cid == ls[:, kk:kk + 1], tw[:, kk:kk + 1], wmat)
    wmat = wmat.astype(BF16)
    _segment_waits(seg_smem[slot, 3 * LANES], copy_into(slot))
    lo, hi = _unpack_bf16_pairs(ybuf_ref[slot])
    moe = jnp.concatenate([_dot(wmat, lo.astype(BF16)), _dot(wmat, hi.astype(BF16))], axis=1)
    x3 = x_ref[...] + moe
    out_ref[...] = _rms(x3, g_ref[...]) if final_norm else x3


def _combine(seg2d, ys, ls, tw, x2, g, tm, final_norm):
    T = x2.shape[0]
    rows = _local_rows(tm)
    return pl.pallas_call(
        functools.partial(_combine_kernel, tm=tm, final_norm=final_norm),
        grid=(T // tm,),
        in_specs=[
            pl.BlockSpec(memory_space=pl.ANY),
            pl.BlockSpec(memory_space=pl.ANY),
            pl.BlockSpec((tm, LANES), lambda i: (i, 0)),
            pl.BlockSpec((tm, LANES), lambda i: (i, 0)),
            pl.BlockSpec((tm, D_MODEL), lambda i: (i, 0)),
            pl.BlockSpec((1, D_MODEL), lambda i: (0, 0)),
        ],
        out_specs=pl.BlockSpec((tm, D_MODEL), lambda i: (i, 0)),
        out_shape=jax.ShapeDtypeStruct((T, D_MODEL), F32),
        scratch_shapes=[
            pltpu.VMEM((2, rows, D_MODEL // 2), U32),
            pltpu.SMEM((2, SEG_WORDS), I32),
            pltpu.SemaphoreType.DMA,
            pltpu.SemaphoreType.DMA((2,)),
        ],
        compiler_params=_cparams(("arbitrary",)),
        name="combine",
    )(seg2d, ys, ls, tw, x2, g)


def _tiles(B, S):
    T = B * S
    return dict(
        tm_in=min(1024, T), tn_in=2048,
        ts=min(512, S),
        tq=min(1024, S),
        tm_proj=min(512, S),
        tm_route=min(256, T),
        tg=1024,
    )


def _pad_lanes(a, n=LANES, value=0.0):
    return jnp.pad(a, ((0, 0), (0, n - a.shape[1])), constant_values=value)


def kernel(x, mem, norm_mix, w_in, conv_w, b_if, mlstm_gain, diff_lambda, diff_gain, w_branch_m, w_branch_d,
           b_gate, w_out, norm_xattn, norm_mem, wq_x, wkv_x, wo_x, norm_ffn, w_router, b_router, w_gu, b_gu,
           w_dn, b_dn, norm_final):
    B, S, D = x.shape
    n_mem = mem.shape[1]
    T = B * S
    depth = norm_mix.shape[0]
    tl = _tiles(B, S)
    x2d = x.reshape(T, D)
    mem2d = mem.reshape(B * n_mem, D)

    for l in range(depth):
        lam_init = 0.8 - 0.6 * math.exp(-0.3 * l)
        wl = w_in[l]
        if_lo = 2 * M_QK + 2 * M_V
        w_main = jnp.concatenate([wl[:, :if_lo], wl[:, if_lo + 2 * M_HEADS:]], axis=1).astype(BF16)
        w_if = wl[:, if_lo:if_lo + 2 * M_HEADS]
        w_ifp = _pad_lanes(w_if).astype(BF16)
        w_ift = w_if.T.astype(BF16)
        bif = _pad_lanes(b_if[l][None, :])
        bift = jnp.broadcast_to(b_if[l][:, None], (SUBLANES, LANES))

        z, zif, zift = _inproj(x2d, norm_mix[l][None, :], w_main, w_ifp, w_ift, tl["tm_in"], tl["tn_in"])
        hm = _mlstm(z, zif, zift, conv_w[l], bif, bift, mlstm_gain[l].reshape(1, M_V), B, S, tl["ts"])
        hd = _diffattn(z, diff_lambda[l], diff_gain[l][None, :], B, S, tl["tq"], lam_init)
        x1 = _merge(hm, hd, z, x2d, w_branch_m[l].astype(BF16), w_branch_d[l].astype(BF16),
                    w_out[l].astype(BF16), b_gate[l][None, :], tl["tm_proj"])

        kvmem = _memkv(mem2d, norm_mem[l][None, :], wkv_x[l].astype(BF16), n_mem)
        x2 = _xattn(x1, norm_xattn[l][None, :], wq_x[l].astype(BF16), kvmem, wo_x[l].astype(BF16),
                    S, n_mem, tl["tm_proj"])

        wr = _pad_lanes(w_router[l])
        wr_hi = wr.astype(BF16)
        wr_lo = (wr - wr_hi.astype(F32)).astype(BF16)
        br = _pad_lanes(b_router[l][None, :], value=-jnp.inf)
        tm_r = tl["tm_route"]
        tg = tl["tg"]
        hp, ids, tw, cnt = _router(x2, norm_ffn[l][None, :], wr_hi, wr_lo, br, tm_r)

        counts = cnt[0, :N_EXPERTS].astype(I32)
        padded = ((counts + tg - 1) // tg) * tg
        ends = jnp.cumsum(padded)
        starts = ends - padded
        max_rows = T * TOP_K + (T // tm_r) * N_EXPERTS * (SUBLANES - 1)
        n_tiles = -(-max_rows // tg) + N_EXPERTS
        tile_row0 = jnp.arange(n_tiles, dtype=I32) * tg
        tile_expert = jnp.minimum(jnp.sum((tile_row0[:, None] >= ends[None, :]).astype(I32), axis=1), N_EXPERTS - 1)
        n_used = (ends[-1] // tg).astype(I32).reshape(1)
        last_used = tile_expert[jnp.maximum(n_used[0] - 1, 0)]
        tile_expert = jnp.where(tile_row0 < ends[-1], tile_expert, last_used)
        tile_valid = jnp.clip((starts + counts)[tile_expert] - tile_row0, 0, tg).astype(I32)

        ls, lst, seg = _slots(ids, _pad_lanes(starts.astype(F32)[None, :]), tm_r)
        seg2d = seg.reshape(T // tm_r, SEG_WORDS)

        xs = _dispatch(hp, lst, seg2d, n_tiles * tg, tm_r)
        ys = _experts(tile_expert, n_used, tile_valid, xs, w_gu[l], b_gu[l][:, None, :], w_dn[l], b_dn[l][:, None, :],
                      tg)
        x2d = _combine(seg2d, ys, ls, tw, x2, norm_final[None, :], tm_r, final_norm=(l == depth - 1))
    return x2d.reshape(B, S, D)
```

```python
import functools
import math

import jax
import jax.numpy as jnp
from jax import lax
from jax.experimental import pallas as pl
from jax.experimental.pallas import tpu as pltpu

F32 = jnp.float32
BF16 = jnp.bfloat16
U32 = jnp.uint32
I32 = jnp.int32

EPS = 1e-6
CHUNK = 64
D_MODEL = 1024
M_HEADS = 4
M_DK = 128
M_DV = 256
M_QK = M_HEADS * M_DK
M_V = M_HEADS * M_DV
CONV_W = 4
D_HEADS = 8
D_DH = 64
D_HP = 2
D_QK = D_HEADS * 2 * D_DH
D_V = D_HEADS * 2 * D_DH
X_HEADS = 4
X_DH = D_MODEL // X_HEADS
N_EXPERTS = 32
TOP_K = 4
D_FF = D_MODEL
SWIGLU_LIMIT = 7.0
SWIGLU_ALPHA = 1.702

LANES = 128
SUBLANES = 8
N_MAIN = 2 * M_QK + 2 * M_V + 2 * D_QK + D_V + 2 * D_MODEL
OFF_QM, OFF_KM, OFF_VM, OFF_OM = 0, M_QK, 2 * M_QK, 2 * M_QK + M_V
OFF_QD = OFF_OM + M_V
OFF_KD = OFF_QD + D_QK
OFF_VD = OFF_KD + D_QK
OFF_G = OFF_VD + D_V

VMEM_LIMIT = 56 * 1024 * 1024


def _cparams(sem, vmem=VMEM_LIMIT):
    return pltpu.CompilerParams(dimension_semantics=sem, vmem_limit_bytes=vmem)


def _rms(x, g):
    return x * lax.rsqrt(jnp.mean(x * x, axis=-1, keepdims=True) + EPS) * g


def _split_bf16(x):
    hi = x.astype(BF16)
    lo = (x - hi.astype(F32)).astype(BF16)
    return hi, lo


def _dot(a, b):
    return jnp.dot(a, b, preferred_element_type=F32)


def _dot_nt(a, b):
    return lax.dot_general(a, b, (((1,), (1,)), ((), ())), preferred_element_type=F32)


def _sigmoid(x):
    return 1.0 / (1.0 + jnp.exp(-x))


def _log_sigmoid(x):
    return jnp.minimum(x, 0.0) - jnp.log(1.0 + jnp.exp(-jnp.abs(x)))


def _pack_bf16_pairs(x):
    w = x.shape[1] // 2
    u = lax.bitcast_convert_type(x, U32)
    r = (u + jnp.uint32(0x7FFF) + ((u >> 16) & jnp.uint32(1))) >> 16
    return r[:, :w] | (r[:, w:] << 16)


def _pack_exact_bf16_pairs(x):
    w = x.shape[1] // 2
    u = lax.bitcast_convert_type(x, U32)
    return (u[:, :w] >> 16) | (u[:, w:] & jnp.uint32(0xFFFF0000))


def _ceil_rows(x):
    return jnp.floor((x + (SUBLANES - 1)) * (1.0 / SUBLANES)) * SUBLANES


def _unpack_bf16_pairs(p):
    lo = lax.bitcast_convert_type(p << 16, F32)
    hi = lax.bitcast_convert_type(p & jnp.uint32(0xFFFF0000), F32)
    return lo, hi


def _inproj_kernel(x_ref, g_ref, w_ref, wif_ref, wift_ref, z_ref, zif_ref, zift_ref, hn_ref):
    @pl.when(pl.program_id(1) == 0)
    def _():
        hn = _rms(x_ref[...], g_ref[...]).astype(BF16)
        hn_ref[...] = hn
        zif_ref[...] = _dot(hn, wif_ref[...])
        zift_ref[...] = _dot_nt(wift_ref[...], hn)

    z_ref[...] = _dot(hn_ref[...], w_ref[...]).astype(BF16)


def _inproj(x2d, g, w_main, w_if, w_ift, tm, tn):
    T = x2d.shape[0]
    return pl.pallas_call(
        _inproj_kernel,
        grid=(T // tm, N_MAIN // tn),
        in_specs=[
            pl.BlockSpec((tm, D_MODEL), lambda i, j: (i, 0)),
            pl.BlockSpec((1, D_MODEL), lambda i, j: (0, 0)),
            pl.BlockSpec((D_MODEL, tn), lambda i, j: (0, j)),
            pl.BlockSpec((D_MODEL, LANES), lambda i, j: (0, 0)),
            pl.BlockSpec((SUBLANES, D_MODEL), lambda i, j: (0, 0)),
        ],
        out_specs=[
            pl.BlockSpec((tm, tn), lambda i, j: (i, j)),
            pl.BlockSpec((tm, LANES), lambda i, j: (i, 0)),
            pl.BlockSpec((SUBLANES, tm), lambda i, j: (0, i)),
        ],
        out_shape=[
            jax.ShapeDtypeStruct((T, N_MAIN), BF16),
            jax.ShapeDtypeStruct((T, LANES), F32),
            jax.ShapeDtypeStruct((SUBLANES, T), F32),
        ],
        scratch_shapes=[pltpu.VMEM((tm, D_MODEL), BF16)],
        compiler_params=_cparams(("arbitrary", "arbitrary")),
        name="inproj",
    )(x2d, g, w_main, w_if, w_ift)


def _mlstm_kernel(q_ref, k_ref, v_ref, om_ref, zif_ref, zift_ref, cw_ref, bif_ref, bift_ref, mg_ref,
                  out_ref, qc_ref, kc_ref, kt_ref, carry_ref, c_ref, n_ref, m_ref,
                  bd_ref, bdt_ref, brep_ref, grow_ref, brow_ref, *, ts):
    nchunk = ts // CHUNK
    L = CHUNK

    @pl.when(pl.program_id(1) == 0)
    def _():
        carry_ref[...] = jnp.zeros_like(carry_ref)
        c_ref[...] = jnp.zeros_like(c_ref)
        n_ref[...] = jnp.zeros_like(n_ref)
        m_ref[...] = jnp.zeros_like(m_ref)
        rt = lax.broadcasted_iota(I32, (ts, ts), 0)
        ct = lax.broadcasted_iota(I32, (ts, ts), 1)
        same = (rt // L) == (ct // L)
        bd_ref[...] = jnp.where(same, jnp.where(ct <= rt, 1.0, 0.0), 0.0).astype(BF16)
        bdt_ref[...] = jnp.where(same, jnp.where(rt <= ct, 1.0, 0.0), 0.0).astype(BF16)

    row8 = lax.broadcasted_iota(I32, (SUBLANES, M_QK), 0)

    def conv_silu(x, prev8, w):
        acc = w[CONV_W - 1:CONV_W, :] * x
        for s in range(1, CONV_W):
            xs = pltpu.roll(x, s, 0)
            top = jnp.where(row8 < s, pltpu.roll(prev8, s, 0), xs[0:SUBLANES])
            xs = jnp.concatenate([top, xs[SUBLANES:]], axis=0)
            acc = acc + w[CONV_W - 1 - s:CONV_W - s, :] * xs
        return acc * _sigmoid(acc)

    def conv_body(c, carry):
        r0 = pl.multiple_of(c * L, L)
        xq = q_ref[pl.ds(r0, L), :].astype(F32)
        xk = k_ref[pl.ds(r0, L), :].astype(F32)
        yq = conv_silu(xq, carry_ref[:, 0:M_QK], cw_ref[:, 0:M_QK]) * (M_DK ** -0.5)
        yk = conv_silu(xk, carry_ref[:, M_QK:2 * M_QK], cw_ref[:, M_QK:2 * M_QK])
        qc_ref[pl.ds(r0, L), :] = yq.astype(BF16)
        kc_ref[pl.ds(r0, L), :] = yk.astype(BF16)
        for h in range(M_HEADS):
            kt_ref[c, h] = yk[:, h * M_DK:(h + 1) * M_DK].T
        carry_ref[:, 0:M_QK] = xq[L - SUBLANES:L]
        carry_ref[:, M_QK:2 * M_QK] = xk[L - SUBLANES:L]
        return carry

    lax.fori_loop(0, nchunk, conv_body, 0, unroll=4)

    ti = lax.broadcasted_iota(I32, (L, L), 0)
    si = lax.broadcasted_iota(I32, (L, L), 1)
    causal = si <= ti
    lane_row = lax.broadcasted_iota(I32, (LANES, LANES), 0)
    ones_l = jnp.ones((L, LANES), BF16)

    lf_col = _log_sigmoid(zif_ref[...] + bif_ref[...])
    ch, cl = _split_bf16(lf_col)
    bd = bd_ref[...]
    b_col_all = _dot(jnp.concatenate([bd, bd], axis=1), jnp.concatenate([ch, cl], axis=0))
    bhl = jnp.concatenate(_split_bf16(b_col_all), axis=1)
    for h in range(M_HEADS):
        sel_f = jnp.where(lane_row == M_HEADS + h, 1.0, 0.0).astype(BF16)
        brep_ref[h] = _dot(bhl, jnp.concatenate([sel_f, sel_f], axis=0))
    g_row_all = zift_ref[...] + bift_ref[:, 0:1]
    rhl = jnp.concatenate(_split_bf16(_log_sigmoid(g_row_all)), axis=1)
    bdt = bdt_ref[...]
    b_row_tile = _dot(rhl, jnp.concatenate([bdt, bdt], axis=0))
    for cc in range(nchunk):
        grow_ref[cc] = g_row_all[:, cc * L:(cc + 1) * L]
        brow_ref[cc] = b_row_tile[:, cc * L:(cc + 1) * L]

    def chunk_body(c, carry):
        r0 = pl.multiple_of(c * L, L)
        g_row = grow_ref[c]
        b_row_all = brow_ref[c]
        early = []
        for h in range(M_HEADS):
            b_rep = brep_ref[h, pl.ds(r0, L), :]
            i_row = g_row[h:h + 1, :]
            b_row = b_row_all[M_HEADS + h:M_HEADS + h + 1, :]
            b_last = b_rep[L - 1:L, :]
            q = qc_ref[pl.ds(r0, L), h * M_DK:(h + 1) * M_DK]
            k = kc_ref[pl.ds(r0, L), h * M_DK:(h + 1) * M_DK]
            vext = jnp.concatenate([v_ref[pl.ds(r0, L), h * M_DV:(h + 1) * M_DV], ones_l], axis=1)
            dm = jnp.where(causal, b_rep[:, :L] - b_row + i_row, -jnp.inf)
            m_loc = jnp.max(dm, axis=-1, keepdims=True)
            qk = _dot_nt(q, k)
            gk_row = b_last[:, :L] - b_row + i_row
            g_max = jnp.max(gk_row, axis=-1, keepdims=True)
            kwt = (kt_ref[c, h] * jnp.exp(gk_row - g_max)).astype(BF16)
            kv = _dot(kwt, vext)
            c_old = c_ref[h]
            n_old = n_ref[h]
            qcn = _dot(q, jnp.concatenate([c_old, n_old], axis=1).astype(BF16))
            early.append((b_rep, b_last, vext, dm, m_loc, qk, g_max, kv, c_old, n_old, qcn))
        pvs = []
        for h in range(M_HEADS):
            b_rep, b_last, vext, dm, m_loc, qk, g_max, kv, c_old, n_old, qcn = early[h]
            s_loc = qk * jnp.exp(dm - m_loc)
            pvs.append(_dot(s_loc.astype(BF16), vext))
        for h in range(M_HEADS):
            b_rep, b_last, vext, dm, m_loc, qk, g_max, kv, c_old, n_old, qcn = early[h]
            pv = pvs[h]
            m_prev = m_ref[h:h + 1, :]
            inter = b_rep + m_prev
            m_t = jnp.maximum(inter, m_loc)
            w_inter = jnp.exp(inter - m_t)
            r_loc = jnp.exp(m_loc - m_t)
            den = r_loc * pv[:, M_DV:] + w_inter * qcn[:, M_DV:]
            inv = 1.0 / jnp.maximum(jnp.abs(den), jnp.exp(-m_t))
            hv = (jnp.concatenate([r_loc * inv] * 2, axis=1) * pv[:, :M_DV]
                  + jnp.concatenate([w_inter * inv] * 2, axis=1) * qcn[:, :M_DV])
            m_new = jnp.maximum(b_last + m_prev, g_max)
            decay = jnp.exp(b_last + m_prev - m_new)
            sc_loc = jnp.exp(g_max - m_new)
            c_ref[h] = (jnp.concatenate([decay] * 2, axis=1) * c_old
                        + jnp.concatenate([sc_loc] * 2, axis=1) * kv[:, :M_DV])
            n_ref[h] = decay * n_old + sc_loc * kv[:, M_DV:]
            m_ref[h:h + 1, :] = m_new
            hn = _rms(hv, mg_ref[:, h * M_DV:(h + 1) * M_DV])
            og = _sigmoid(om_ref[pl.ds(r0, L), h * M_DV:(h + 1) * M_DV].astype(F32))
            out_ref[pl.ds(r0, L), h * M_DV:(h + 1) * M_DV] = (og * hn).astype(BF16)
        return carry

    lax.fori_loop(0, nchunk, chunk_body, 0, unroll=4)


def _mlstm(z, zif, zift, conv_w, bif, bift, m_gain, B, S, ts):
    T = B * S
    nt = S // ts
    nck = ts // CHUNK
    row = lambda b, t: b * nt + t
    return pl.pallas_call(
        functools.partial(_mlstm_kernel, ts=ts),
        grid=(B, nt),
        in_specs=[
            pl.BlockSpec((ts, M_QK), lambda b, t: (row(b, t), OFF_QM // M_QK)),
            pl.BlockSpec((ts, M_QK), lambda b, t: (row(b, t), OFF_KM // M_QK)),
            pl.BlockSpec((ts, M_V), lambda b, t: (row(b, t), OFF_VM // M_V)),
            pl.BlockSpec((ts, M_V), lambda b, t: (row(b, t), OFF_OM // M_V)),
            pl.BlockSpec((ts, LANES), lambda b, t: (row(b, t), 0)),
            pl.BlockSpec((SUBLANES, ts), lambda b, t: (0, row(b, t))),
            pl.BlockSpec((CONV_W, 2 * M_QK), lambda b, t: (0, 0)),
            pl.BlockSpec((1, LANES), lambda b, t: (0, 0)),
            pl.BlockSpec((SUBLANES, LANES), lambda b, t: (0, 0)),
            pl.BlockSpec((1, M_V), lambda b, t: (0, 0)),
        ],
        out_specs=pl.BlockSpec((ts, M_V), lambda b, t: (row(b, t), 0)),
        out_shape=jax.ShapeDtypeStruct((T, M_V), BF16),
        scratch_shapes=[
            pltpu.VMEM((ts, M_QK), BF16),
            pltpu.VMEM((ts, M_QK), BF16),
            pltpu.VMEM((nck, M_HEADS, M_DK, CHUNK), F32),
            pltpu.VMEM((SUBLANES, 2 * M_QK), F32),
            pltpu.VMEM((M_HEADS, M_DK, M_DV), F32),
            pltpu.VMEM((M_HEADS, M_DK, LANES), F32),
            pltpu.VMEM((SUBLANES, LANES), F32),
            pltpu.VMEM((ts, ts), BF16),
            pltpu.VMEM((ts, ts), BF16),
            pltpu.VMEM((M_HEADS, ts, LANES), F32),
            pltpu.VMEM((nck, SUBLANES, CHUNK), F32),
            pltpu.VMEM((nck, SUBLANES, CHUNK), F32),
        ],
        compiler_params=_cparams(("arbitrary", "arbitrary")),
        name="mlstm",
    )(z, z, z, z, zif, zift, conv_w, bif, bift, m_gain)


def _diffattn_kernel(q_ref, k_ref, v_ref, lam_ref, gain_ref, out_ref, m_ref, a_ref, *, tq, lam_init):
    qi = pl.program_id(2)
    w = 2 * D_DH
    lane = lax.broadcasted_iota(I32, (1, w), 1)
    scale = jnp.asarray(D_DH ** -0.5, BF16)
    qs = []
    for hh in range(D_HP):
        q = q_ref[:, hh * w:(hh + 1) * w]
        qs.append((jnp.where(lane < D_DH, q, jnp.zeros_like(q)) * scale,
                   jnp.where(lane >= D_DH, q, jnp.zeros_like(q)) * scale))
    ones = jnp.ones((tq, w), BF16)

    def block(k0, nk, rows=slice(None), mask=None, first=False):
        n_stream = 2 * D_HP
        scores = []
        for hh in range(D_HP):
            k = k_ref[pl.ds(k0, nk), hh * w:(hh + 1) * w]
            for comp in range(2):
                s = _dot_nt(qs[hh][comp][rows], k)
                scores.append(s if mask is None else jnp.where(mask, s, -jnp.inf))
        m_news, alphas, probs = [], [], []
        for i in range(n_stream):
            s = scores[i]
            s_max = jnp.max(s, axis=-1, keepdims=True)
            if first:
                m_new = jnp.broadcast_to(s_max, (s.shape[0], w))
                alphas.append(None)
            else:
                m_old = m_ref[i, rows, :]
                m_new = jnp.maximum(m_old, s_max)
                alphas.append(jnp.exp(m_old - m_new))
            m_news.append(m_new)
            probs.append(jnp.exp(s - jnp.concatenate([m_new] * (s.shape[1] // w), axis=1)).astype(BF16))
        pvs = []
        for hh in range(D_HP):
            vext = jnp.concatenate([v_ref[pl.ds(k0, nk), hh * w:(hh + 1) * w], ones[:nk]], axis=1)
            for comp in range(2):
                pvs.append(_dot(probs[2 * hh + comp], vext))
        for i in range(n_stream):
            if first:
                a_ref[i, rows, :] = pvs[i]
            else:
                a_ref[i, rows, :] = jnp.concatenate([alphas[i]] * 2, axis=1) * a_ref[i, rows, :] + pvs[i]
            m_ref[i, rows, :] = m_news[i]

    hq = tq // 2
    d0 = pl.multiple_of(qi * tq, tq)

    def chunk_mask(q0, nk):
        rq = (lax.broadcasted_iota(I32, (hq, nk), 0) + q0) // CHUNK
        ck = lax.broadcasted_iota(I32, (hq, nk), 1) // CHUNK
        return ck <= rq

    block(d0, hq, rows=slice(0, hq), mask=chunk_mask(0, hq), first=True)
    block(d0, tq, rows=slice(hq, tq), mask=chunk_mask(hq, tq), first=True)

    def body(jj, carry):
        block(pl.multiple_of(jj * tq, tq), tq)
        return carry

    lax.fori_loop(0, qi, body, 0)

    lp = lam_ref[...]
    lam = (jnp.exp(jnp.sum(lp[0:1, :] * lp[1:2, :], axis=-1, keepdims=True))
           - jnp.exp(jnp.sum(lp[2:3, :] * lp[3:4, :], axis=-1, keepdims=True)) + lam_init)
    for hh in range(D_HP):
        a1 = a_ref[2 * hh]
        a2 = a_ref[2 * hh + 1]
        o = a1[:, :w] / a1[:, w:] - lam * (a2[:, :w] / a2[:, w:])
        out_ref[:, hh * w:(hh + 1) * w] = (_rms(o, gain_ref[...]) * (1.0 - lam_init)).astype(BF16)


def _diffattn(z, lam_p, d_gain, B, S, tq, lam_init):
    T = B * S
    nq = S // tq
    w = 2 * D_DH
    wp = D_HP * w
    return pl.pallas_call(
        functools.partial(_diffattn_kernel, tq=tq, lam_init=lam_init),
        grid=(B, D_HEADS // D_HP, nq),
        in_specs=[
            pl.BlockSpec((tq, wp), lambda b, h, i: (b * nq + i, OFF_QD // wp + h)),
            pl.BlockSpec((S, wp), lambda b, h, i: (b, OFF_KD // wp + h)),
            pl.BlockSpec((S, wp), lambda b, h, i: (b, OFF_VD // wp + h)),
            pl.BlockSpec((4, D_DH), lambda b, h, i: (0, 0)),
            pl.BlockSpec((1, w), lambda b, h, i: (0, 0)),
        ],
        out_specs=pl.BlockSpec((tq, wp), lambda b, h, i: (b * nq + i, h)),
        out_shape=jax.ShapeDtypeStruct((T, D_V), BF16),
        scratch_shapes=[
            pltpu.VMEM((2 * D_HP, tq, w), F32), pltpu.VMEM((2 * D_HP, tq, 2 * w), F32),
        ],
        compiler_params=_cparams(("arbitrary", "arbitrary", "arbitrary")),
        name="diffattn",
    )(z, z, z, lam_p, d_gain)


def _merge_kernel(hm_ref, hd_ref, gz_ref, x_ref, wbm_ref, wbd_ref, wout_ref, bg_ref, out_ref):
    bm = _dot(hm_ref[...], wbm_ref[...])
    bd = _dot(hd_ref[...], wbd_ref[...])
    g = _sigmoid(gz_ref[...].astype(F32) + bg_ref[...])
    merged = g[:, :D_MODEL] * bm + g[:, D_MODEL:] * bd
    out_ref[...] = x_ref[...] + _dot(merged.astype(BF16), wout_ref[...])


def _merge(hm, hd, z, x2d, w_bm, w_bd, w_out, b_gate, tm):
    T = x2d.shape[0]
    full = lambda i: (0, 0)
    return pl.pallas_call(
        _merge_kernel,
        grid=(T // tm,),
        in_specs=[
            pl.BlockSpec((tm, M_V), lambda i: (i, 0)),
            pl.BlockSpec((tm, D_V), lambda i: (i, 0)),
            pl.BlockSpec((tm, 2 * D_MODEL), lambda i: (i, OFF_G // (2 * D_MODEL))),
            pl.BlockSpec((tm, D_MODEL), lambda i: (i, 0)),
            pl.BlockSpec((M_V, D_MODEL), full),
            pl.BlockSpec((D_V, D_MODEL), full),
            pl.BlockSpec((D_MODEL, D_MODEL), full),
            pl.BlockSpec((1, 2 * D_MODEL), full),
        ],
        out_specs=pl.BlockSpec((tm, D_MODEL), lambda i: (i, 0)),
        out_shape=jax.ShapeDtypeStruct((T, D_MODEL), F32),
        compiler_params=_cparams(("arbitrary",)),
        name="merge",
    )(hm, hd, z, x2d, w_bm, w_bd, w_out, b_gate)


def _memkv_kernel(mem_ref, g_ref, w_ref, out_ref):
    out_ref[...] = _dot(_rms(mem_ref[...], g_ref[...]).astype(BF16), w_ref[...]).astype(BF16)


def _memkv(mem2d, g, wkv, n_mem):
    R = mem2d.shape[0]
    return pl.pallas_call(
        _memkv_kernel,
        grid=(R // n_mem,),
        in_specs=[
            pl.BlockSpec((n_mem, D_MODEL), lambda i: (i, 0)),
            pl.BlockSpec((1, D_MODEL), lambda i: (0, 0)),
            pl.BlockSpec((D_MODEL, 2 * D_MODEL), lambda i: (0, 0)),
        ],
        out_specs=pl.BlockSpec((n_mem, 2 * D_MODEL), lambda i: (i, 0)),
        out_shape=jax.ShapeDtypeStruct((R, 2 * D_MODEL), BF16),
        compiler_params=_cparams(("arbitrary",)),
        name="memkv",
    )(mem2d, g, wkv)


def _xattn_kernel(x_ref, g_ref, wq_ref, kv_ref, wo_ref, out_ref, o_ref):
    x = x_ref[...]
    h = _rms(x, g_ref[...]).astype(BF16)
    q = (_dot(h, wq_ref[...]) * (X_DH ** -0.5)).astype(BF16)
    scores = [_dot_nt(q[:, hd * X_DH:(hd + 1) * X_DH], kv_ref[:, hd * X_DH:(hd + 1) * X_DH]) for hd in range(X_HEADS)]
    for hd in range(X_HEADS):
        vh = kv_ref[:, D_MODEL + hd * X_DH:D_MODEL + (hd + 1) * X_DH]
        s = scores[hd]
        p = jnp.exp(s - jnp.max(s, axis=-1, keepdims=True))
        p = p / jnp.sum(p, axis=-1, keepdims=True)
        o_ref[:, hd * X_DH:(hd + 1) * X_DH] = _dot(p.astype(BF16), vh).astype(BF16)
    out_ref[...] = x + _dot(o_ref[...], wo_ref[...])


def _xattn(x1, g, wq, kvmem, wo, S, n_mem, tm):
    T = x1.shape[0]
    per_b = S // tm
    full = lambda i: (0, 0)
    return pl.pallas_call(
        _xattn_kernel,
        grid=(T // tm,),
        in_specs=[
            pl.BlockSpec((tm, D_MODEL), lambda i: (i, 0)),
            pl.BlockSpec((1, D_MODEL), full),
            pl.BlockSpec((D_MODEL, D_MODEL), full),
            pl.BlockSpec((n_mem, 2 * D_MODEL), lambda i: (i // per_b, 0)),
            pl.BlockSpec((D_MODEL, D_MODEL), full),
        ],
        out_specs=pl.BlockSpec((tm, D_MODEL), lambda i: (i, 0)),
        out_shape=jax.ShapeDtypeStruct((T, D_MODEL), F32),
        scratch_shapes=[pltpu.VMEM((tm, D_MODEL), BF16)],
        compiler_params=_cparams(("arbitrary",)),
        name="xattn",
    )(x1, g, wq, kvmem, wo)


def _router_kernel(x_ref, g_ref, wrh_ref, wrl_ref, br_ref, hp_ref, ids_ref, tw_ref, cnt_ref):
    @pl.when(pl.program_id(0) == 0)
    def _():
        cnt_ref[...] = jnp.zeros_like(cnt_ref)

    hn = _rms(x_ref[...], g_ref[...])
    hh, hl = _split_bf16(hn)
    hp_ref[...] = hh
    logits = _dot(hh, wrh_ref[...]) + _dot(hh, wrl_ref[...]) + _dot(hl, wrh_ref[...]) + br_ref[...]
    lane = lax.broadcasted_iota(I32, logits.shape, 1)
    lanef = lane.astype(F32)
    ids = jnp.zeros(logits.shape, F32)
    tw = jnp.zeros(logits.shape, F32)
    onehot = jnp.zeros(logits.shape, F32)
    v0 = None
    den = None
    for kk in range(TOP_K):
        mx = jnp.max(logits, axis=-1, keepdims=True)
        idx = jnp.min(jnp.where(logits == mx, lanef, float(LANES)), axis=-1, keepdims=True)
        sel = lanef == idx
        if kk == 0:
            v0 = mx
        e = jnp.exp(mx - v0)
        den = e if den is None else den + e
        ids = jnp.where(lane == kk, idx, ids)
        tw = jnp.where(lane == kk, e, tw)
        onehot = jnp.where(sel, 1.0, onehot)
        logits = jnp.where(sel, -jnp.inf, logits)
    ids_ref[...] = ids.astype(I32)
    tw_ref[...] = tw / den
    cnt_ref[...] += _ceil_rows(jnp.sum(onehot, axis=0, keepdims=True))


def _router(x2, g, wr_hi, wr_lo, b_r, tm):
    T = x2.shape[0]
    full = lambda i: (0, 0)
    return pl.pallas_call(
        _router_kernel,
        grid=(T // tm,),
        in_specs=[
            pl.BlockSpec((tm, D_MODEL), lambda i: (i, 0)),
            pl.BlockSpec((1, D_MODEL), full),
            pl.BlockSpec((D_MODEL, LANES), full),
            pl.BlockSpec((D_MODEL, LANES), full),
            pl.BlockSpec((1, LANES), full),
        ],
        out_specs=[
            pl.BlockSpec((tm, D_MODEL), lambda i: (i, 0)),
            pl.BlockSpec((tm, LANES), lambda i: (i, 0)),
            pl.BlockSpec((tm, LANES), lambda i: (i, 0)),
            pl.BlockSpec((1, LANES), full),
        ],
        out_shape=[
            jax.ShapeDtypeStruct((T, D_MODEL), BF16),
            jax.ShapeDtypeStruct((T, LANES), I32),
            jax.ShapeDtypeStruct((T, LANES), F32),
            jax.ShapeDtypeStruct((1, LANES), F32),
        ],
        compiler_params=_cparams(("arbitrary",)),
        name="router",
    )(x2, g, wr_hi, wr_lo, b_r)


SEG_WORDS = SUBLANES * LANES


def _slots_kernel(ids_ref, start_ref, ls_ref, lst_ref, seg_ref, run_ref):
    @pl.when(pl.program_id(0) == 0)
    def _():
        run_ref[...] = jnp.zeros_like(run_ref)

    ids = ids_ref[...]
    tm = ids.shape[0]
    lane = lax.broadcasted_iota(I32, ids.shape, 1)
    sels = [lane == ids[:, kk:kk + 1] for kk in range(TOP_K)]
    onehot = jnp.zeros(ids.shape, F32)
    for s in sels:
        onehot = jnp.where(s, 1.0, onehot)
    c8 = _ceil_rows(jnp.sum(onehot, axis=0, keepdims=True))
    er = lax.broadcasted_iota(I32, (LANES, LANES), 0)
    ec = lax.broadcasted_iota(I32, (LANES, LANES), 1)
    before = jnp.where(er < ec, 1.0, 0.0).astype(BF16)
    pieces = jnp.broadcast_to(c8 * (1.0 / SUBLANES), (SUBLANES, LANES)).astype(BF16)
    lo = _dot(pieces, before)[0:1, :] * SUBLANES
    r = lax.broadcasted_iota(I32, (tm, tm), 0)
    c = lax.broadcasted_iota(I32, (tm, tm), 1)
    strict = jnp.where(c < r, 1.0, 0.0).astype(BF16)
    slot = _dot(strict, onehot.astype(BF16)) + lo
    ls = jnp.zeros(ids.shape, F32)
    for kk, s in enumerate(sels):
        pk = jnp.sum(jnp.where(s, slot, 0.0), axis=-1, keepdims=True)
        ls = jnp.where(lane == kk, pk, ls)
    ls_ref[...] = ls
    hi = jnp.floor(ls * (1.0 / 32.0))
    rem = ls - 32.0 * hi
    pick = jnp.where(lax.broadcasted_iota(I32, (SUBLANES, LANES), 0) == lax.broadcasted_iota(I32, (SUBLANES, LANES), 1),
                     1.0, 0.0).astype(BF16)
    lst_ref[...] = 32.0 * _dot_nt(pick, hi.astype(BF16)) + _dot_nt(pick, rem.astype(BF16))
    row = lax.broadcasted_iota(I32, (SUBLANES, LANES), 0)
    off = start_ref[...] + run_ref[...]
    total = jnp.sum(c8, axis=-1, keepdims=True)
    seg = jnp.where(row == 0, c8, jnp.where(row == 1, lo, jnp.where(row == 2, off, jnp.where(row == 3, total, 0.0))))
    seg_ref[...] = seg.astype(I32)
    run_ref[...] += c8


def _slots(ids, starts, tm):
    T = ids.shape[0]
    nt = T // tm
    return pl.pallas_call(
        _slots_kernel,
        grid=(nt,),
        in_specs=[
            pl.BlockSpec((tm, LANES), lambda i: (i, 0)),
            pl.BlockSpec((1, LANES), lambda i: (0, 0)),
        ],
        out_specs=[
            pl.BlockSpec((tm, LANES), lambda i: (i, 0)),
            pl.BlockSpec((SUBLANES, tm), lambda i: (0, i)),
            pl.BlockSpec((SUBLANES, LANES), lambda i: (i, 0)),
        ],
        out_shape=[
            jax.ShapeDtypeStruct((T, LANES), F32),
            jax.ShapeDtypeStruct((SUBLANES, T), F32),
            jax.ShapeDtypeStruct((nt * SUBLANES, LANES), I32),
        ],
        scratch_shapes=[pltpu.VMEM((1, LANES), F32)],
        compiler_params=_cparams(("arbitrary",)),
        name="slots",
    )(ids, starts)


def _local_rows(tm):
    need = tm * TOP_K + N_EXPERTS * (SUBLANES - 1)
    return ((need + LANES - 1) // LANES) * LANES


BIG_PIECE = 4 * SUBLANES


def _segment_starts(seg, make_copy):
    def expert(e, carry):
        cnt = seg(e)
        lo = seg(LANES + e)
        off = seg(2 * LANES + e)
        n_big = lax.shift_right_logical(cnt, 5)
        n_small = lax.shift_right_logical(cnt & (BIG_PIECE - 1), 3)

        def big(j, carry2):
            d = j * BIG_PIECE
            make_copy(pl.multiple_of(lo + d, SUBLANES), pl.multiple_of(off + d, SUBLANES), BIG_PIECE).start()
            return carry2

        def small(j, carry2):
            d = n_big * BIG_PIECE + j * SUBLANES
            make_copy(pl.multiple_of(lo + d, SUBLANES), pl.multiple_of(off + d, SUBLANES), SUBLANES).start()
            return carry2

        lax.fori_loop(0, n_big, big, 0)
        lax.fori_loop(0, n_small, small, 0)
        return carry

    for e in range(N_EXPERTS):
        expert(e, 0)


def _segment_waits(total_rows, make_copy):
    def big(j, carry):
        make_copy(0, 0, BIG_PIECE).wait()
        return carry

    def small(j, carry):
        make_copy(0, 0, SUBLANES).wait()
        return carry

    lax.fori_loop(0, lax.shift_right_logical(total_rows, 5), big, 0)
    lax.fori_loop(0, lax.shift_right_logical(total_rows & (BIG_PIECE - 1), 3), small, 0)


def _dispatch_kernel(h_ref, lst_ref, seg_hbm, xs_ref, sbuf_ref, seg_smem, prev_smem, sem_seg, sem_rows, *, tm):
    i = pl.program_id(0)
    n = pl.num_programs(0)
    slot = i % 2
    rows = sbuf_ref.shape[1]
    cp = pltpu.make_async_copy(seg_hbm.at[i], seg_smem, sem_seg)
    cp.start()
    lst = lst_ref[...].astype(I32)
    rid = lax.broadcasted_iota(I32, (rows, tm), 0)
    perm = jnp.zeros((rows, tm), F32)
    for kk in range(TOP_K):
        perm = jnp.where(rid == lst[kk:kk + 1, :], 1.0, perm)
    srt = _dot(perm.astype(BF16), h_ref[...])
    sbuf_ref[slot] = _pack_exact_bf16_pairs(srt)
    cp.wait()

    def copy_from(s):
        def make_copy(lo, off, nrows):
            return pltpu.make_async_copy(sbuf_ref.at[s, pl.ds(lo, nrows), :], xs_ref.at[pl.ds(off, nrows), :],
                                         sem_rows.at[s])
        return make_copy

    _segment_starts(lambda k: seg_smem[k], copy_from(slot))

    @pl.when(i > 0)
    def _():
        _segment_waits(prev_smem[0], copy_from(1 - slot))

    prev_smem[0] = seg_smem[3 * LANES]

    @pl.when(i == n - 1)
    def _():
        _segment_waits(prev_smem[0], copy_from(slot))


def _dispatch(h, lst, seg2d, n_rows, tm):
    T = h.shape[0]
    rows = _local_rows(tm)
    return pl.pallas_call(
        functools.partial(_dispatch_kernel, tm=tm),
        grid=(T // tm,),
        in_specs=[
            pl.BlockSpec((tm, D_MODEL), lambda i: (i, 0)),
            pl.BlockSpec((SUBLANES, tm), lambda i: (0, i)),
            pl.BlockSpec(memory_space=pl.ANY),
        ],
        out_specs=pl.BlockSpec(memory_space=pl.ANY),
        out_shape=jax.ShapeDtypeStruct((n_rows, D_MODEL // 2), U32),
        scratch_shapes=[
            pltpu.VMEM((2, rows, D_MODEL // 2), U32),
            pltpu.SMEM((SEG_WORDS,), I32),
            pltpu.SMEM((1,), I32),
            pltpu.SemaphoreType.DMA,
            pltpu.SemaphoreType.DMA((2,)),
        ],
        compiler_params=_cparams(("arbitrary",)),
        name="dispatch",
    )(h, lst, seg2d)


def _experts_kernel(te_ref, nu_ref, nv_ref, xs_ref, wgu_ref, bgu_ref, wdn_ref, bdn_ref, ys_ref, wgu_bf, wdn_bf):
    i = pl.program_id(0)

    @pl.when((i == 0) | (te_ref[i] != te_ref[jnp.maximum(i - 1, 0)]))
    def _():
        wgu_bf[...] = wgu_ref[0].astype(BF16)
        wdn_bf[...] = wdn_ref[0].astype(BF16)

    @pl.when(i < nu_ref[0])
    def _():
        live = lax.broadcasted_iota(I32, xs_ref.shape, 0) < nv_ref[i]
        lo, hi = _unpack_bf16_pairs(jnp.where(live, xs_ref[...], jnp.uint32(0)))
        xb = jnp.concatenate([lo.astype(BF16), hi.astype(BF16)], axis=1)
        gu = _dot(xb, wgu_bf[...]) + bgu_ref[0]
        gate = jnp.minimum(gu[:, :D_FF], SWIGLU_LIMIT)
        up = jnp.clip(gu[:, D_FF:], -SWIGLU_LIMIT, SWIGLU_LIMIT)
        act = (up + 1.0) * (gate * _sigmoid(SWIGLU_ALPHA * gate))
        y = _dot(act.astype(BF16), wdn_bf[...]) + bdn_ref[0]
        ys_ref[...] = _pack_bf16_pairs(y)

    @pl.when(i >= nu_ref[0])
    def _():
        ys_ref[...] = jnp.zeros_like(ys_ref)


def _experts(tile_expert, n_used, tile_valid, xs, w_gu, b_gu, w_dn, b_dn, tg):
    P = xs.shape[0]
    half = D_MODEL // 2
    grid_spec = pltpu.PrefetchScalarGridSpec(
        num_scalar_prefetch=3,
        grid=(P // tg,),
        in_specs=[
            pl.BlockSpec((tg, half), lambda i, te, nu, nv: (jnp.minimum(i, jnp.maximum(nu[0] - 1, 0)), 0)),
            pl.BlockSpec((1, D_MODEL, 2 * D_FF), lambda i, te, nu, nv: (te[i], 0, 0)),
            pl.BlockSpec((1, 1, 2 * D_FF), lambda i, te, nu, nv: (te[i], 0, 0)),
            pl.BlockSpec((1, D_FF, D_MODEL), lambda i, te, nu, nv: (te[i], 0, 0)),
            pl.BlockSpec((1, 1, D_MODEL), lambda i, te, nu, nv: (te[i], 0, 0)),
        ],
        out_specs=pl.BlockSpec((tg, half), lambda i, te, nu, nv: (i, 0)),
        scratch_shapes=[pltpu.VMEM((D_MODEL, 2 * D_FF), BF16), pltpu.VMEM((D_FF, D_MODEL), BF16)],
    )
    return pl.pallas_call(
        _experts_kernel,
        grid_spec=grid_spec,
        out_shape=jax.ShapeDtypeStruct((P, half), U32),
        compiler_params=_cparams(("arbitrary",)),
        name="experts",
    )(tile_expert, n_used, tile_valid, xs, w_gu, b_gu, w_dn, b_dn)


def _combine_kernel(seg_hbm, ys_hbm, ls_ref, tw_ref, x_ref, g_ref, out_ref, ybuf_ref, seg_smem, sem_seg, sem_rows, *,
                    tm, final_norm):
    i = pl.program_id(0)
    n = pl.num_programs(0)
    slot = i % 2
    rows = ybuf_ref.shape[1]

    def copy_into(s):
        def make_copy(lo, off, nrows):
            return pltpu.make_async_copy(ys_hbm.at[pl.ds(off, nrows), :], ybuf_ref.at[s, pl.ds(lo, nrows), :],
                                         sem_rows.at[s])
        return make_copy

    def request(step, s):
        cp = pltpu.make_async_copy(seg_hbm.at[step], seg_smem.at[s], sem_seg)
        cp.start()
        cp.wait()
        _segment_starts(lambda k: seg_smem[s, k], copy_into(s))

    @pl.when(i == 0)
    def _():
        ybuf_ref[...] = jnp.zeros_like(ybuf_ref)
        request(0, 0)

    @pl.when(i + 1 < n)
    def _():
        request(i + 1, 1 - slot)

    ls = ls_ref[...].astype(I32)
    tw = tw_ref[...]
    cid = lax.broadcasted_iota(I32, (tm, rows), 1)
    wmat = jnp.zeros((tm, rows), F32)
    for kk in range(TOP_K):
        wmat = jnp.where(cid == ls[:, kk:kk + 1], tw[:, kk:kk + 1], wmat)
    wmat = wmat.astype(BF16)
    _segment_waits(seg_smem[slot, 3 * LANES], copy_into(slot))
    lo, hi = _unpack_bf16_pairs(ybuf_ref[slot])
    moe = jnp.concatenate([_dot(wmat, lo.astype(BF16)), _dot(wmat, hi.astype(BF16))], axis=1)
    x3 = x_ref[...] + moe
    out_ref[...] = _rms(x3, g_ref[...]) if final_norm else x3


def _combine(seg2d, ys, ls, tw, x2, g, tm, final_norm):
    T = x2.shape[0]
    rows = _local_rows(tm)
    return pl.pallas_call(
        functools.partial(_combine_kernel, tm=tm, final_norm=final_norm),
        grid=(T // tm,),
        in_specs=[
            pl.BlockSpec(memory_space=pl.ANY),
            pl.BlockSpec(memory_space=pl.ANY),
            pl.BlockSpec((tm, LANES), lambda i: (i, 0)),
            pl.BlockSpec((tm, LANES), lambda i: (i, 0)),
            pl.BlockSpec((tm, D_MODEL), lambda i: (i, 0)),
            pl.BlockSpec((1, D_MODEL), lambda i: (0, 0)),
        ],
        out_specs=pl.BlockSpec((tm, D_MODEL), lambda i: (i, 0)),
        out_shape=jax.ShapeDtypeStruct((T, D_MODEL), F32),
        scratch_shapes=[
            pltpu.VMEM((2, rows, D_MODEL // 2), U32),
            pltpu.SMEM((2, SEG_WORDS), I32),
            pltpu.SemaphoreType.DMA,
            pltpu.SemaphoreType.DMA((2,)),
        ],
        compiler_params=_cparams(("arbitrary",)),
        name="combine",
    )(seg2d, ys, ls, tw, x2, g)


def _tiles(B, S):
    T = B * S
    return dict(
        tm_in=min(1024, T), tn_in=2048,
        ts=min(512, S),
        tq=min(1024, S),
        tm_proj=min(1024, S),
        tm_route=min(256, T),
        tg=1024,
    )


def _pad_lanes(a, n=LANES, value=0.0):
    return jnp.pad(a, ((0, 0), (0, n - a.shape[1])), constant_values=value)


def kernel(x, mem, norm_mix, w_in, conv_w, b_if, mlstm_gain, diff_lambda, diff_gain, w_branch_m, w_branch_d,
           b_gate, w_out, norm_xattn, norm_mem, wq_x, wkv_x, wo_x, norm_ffn, w_router, b_router, w_gu, b_gu,
           w_dn, b_dn, norm_final):
    B, S, D = x.shape
    n_mem = mem.shape[1]
    T = B * S
    depth = norm_mix.shape[0]
    tl = _tiles(B, S)
    x2d = x.reshape(T, D)
    mem2d = mem.reshape(B * n_mem, D)

    for l in range(depth):
        lam_init = 0.8 - 0.6 * math.exp(-0.3 * l)
        wl = w_in[l]
        if_lo = 2 * M_QK + 2 * M_V
        w_main = jnp.concatenate([wl[:, :if_lo], wl[:, if_lo + 2 * M_HEADS:]], axis=1).astype(BF16)
        w_if = wl[:, if_lo:if_lo + 2 * M_HEADS]
        w_ifp = _pad_lanes(w_if).astype(BF16)
        w_ift = w_if.T.astype(BF16)
        bif = _pad_lanes(b_if[l][None, :])
        bift = jnp.broadcast_to(b_if[l][:, None], (SUBLANES, LANES))

        z, zif, zift = _inproj(x2d, norm_mix[l][None, :], w_main, w_ifp, w_ift, tl["tm_in"], tl["tn_in"])
        hm = _mlstm(z, zif, zift, conv_w[l], bif, bift, mlstm_gain[l].reshape(1, M_V), B, S, tl["ts"])
        hd = _diffattn(z, diff_lambda[l], diff_gain[l][None, :], B, S, tl["tq"], lam_init)
        x1 = _merge(hm, hd, z, x2d, w_branch_m[l].astype(BF16), w_branch_d[l].astype(BF16),
                    w_out[l].astype(BF16), b_gate[l][None, :], tl["tm_proj"])

        kvmem = _memkv(mem2d, norm_mem[l][None, :], wkv_x[l].astype(BF16), n_mem)
        x2 = _xattn(x1, norm_xattn[l][None, :], wq_x[l].astype(BF16), kvmem, wo_x[l].astype(BF16),
                    S, n_mem, tl["tm_proj"])

        wr = _pad_lanes(w_router[l])
        wr_hi = wr.astype(BF16)
        wr_lo = (wr - wr_hi.astype(F32)).astype(BF16)
        br = _pad_lanes(b_router[l][None, :], value=-jnp.inf)
        tm_r = tl["tm_route"]
        tg = tl["tg"]
        hp, ids, tw, cnt = _router(x2, norm_ffn[l][None, :], wr_hi, wr_lo, br, tm_r)

        counts = cnt[0, :N_EXPERTS].astype(I32)
        padded = ((counts + tg - 1) // tg) * tg
        ends = jnp.cumsum(padded)
        starts = ends - padded
        max_rows = T * TOP_K + (T // tm_r) * N_EXPERTS * (SUBLANES - 1)
        n_tiles = -(-max_rows // tg) + N_EXPERTS
        tile_row0 = jnp.arange(n_tiles, dtype=I32) * tg
        tile_expert = jnp.minimum(jnp.sum((tile_row0[:, None] >= ends[None, :]).astype(I32), axis=1), N_EXPERTS - 1)
        n_used = (ends[-1] // tg).astype(I32).reshape(1)
        last_used = tile_expert[jnp.maximum(n_used[0] - 1, 0)]
        tile_expert = jnp.where(tile_row0 < ends[-1], tile_expert, last_used)
        tile_valid = jnp.clip((starts + counts)[tile_expert] - tile_row0, 0, tg).astype(I32)

        ls, lst, seg = _slots(ids, _pad_lanes(starts.astype(F32)[None, :]), tm_r)
        seg2d = seg.reshape(T // tm_r, SEG_WORDS)

        xs = _dispatch(hp, lst, seg2d, n_tiles * tg, tm_r)
        ys = _experts(tile_expert, n_used, tile_valid, xs, w_gu[l], b_gu[l][:, None, :], w_dn[l], b_dn[l][:, None, :],
                      tg)
        x2d = _combine(seg2d, ys, ls, tw, x2, norm_final[None, :], tm_r, final_norm=(l == depth - 1))
    return x2d.reshape(B, S, D)
```

```python
import functools
import math

import jax
import jax.numpy as jnp
from jax import lax
from jax.experimental import pallas as pl
from jax.experimental.pallas import tpu as pltpu

F32 = jnp.float32
BF16 = jnp.bfloat16
U32 = jnp.uint32
I32 = jnp.int32

EPS = 1e-6
CHUNK = 64
D_MODEL = 1024
M_HEADS = 4
M_DK = 128
M_DV = 256
M_QK = M_HEADS * M_DK
M_V = M_HEADS * M_DV
CONV_W = 4
D_HEADS = 8
D_DH = 64
D_HP = 2
D_QK = D_HEADS * 2 * D_DH
D_V = D_HEADS * 2 * D_DH
X_HEADS = 4
X_DH = D_MODEL // X_HEADS
N_EXPERTS = 32
TOP_K = 4
D_FF = D_MODEL
SWIGLU_LIMIT = 7.0
SWIGLU_ALPHA = 1.702

LANES = 128
SUBLANES = 8
N_MAIN = 2 * M_QK + 2 * M_V + 2 * D_QK + D_V + 2 * D_MODEL
OFF_QM, OFF_KM, OFF_VM, OFF_OM = 0, M_QK, 2 * M_QK, 2 * M_QK + M_V
OFF_QD = OFF_OM + M_V
OFF_KD = OFF_QD + D_QK
OFF_VD = OFF_KD + D_QK
OFF_G = OFF_VD + D_V

VMEM_LIMIT = 56 * 1024 * 1024


def _cparams(sem, vmem=VMEM_LIMIT):
    return pltpu.CompilerParams(dimension_semantics=sem, vmem_limit_bytes=vmem)


def _rms(x, g):
    return x * lax.rsqrt(jnp.mean(x * x, axis=-1, keepdims=True) + EPS) * g


def _split_bf16(x):
    hi = x.astype(BF16)
    lo = (x - hi.astype(F32)).astype(BF16)
    return hi, lo


def _dot(a, b):
    return jnp.dot(a, b, preferred_element_type=F32)


def _dot_nt(a, b):
    return lax.dot_general(a, b, (((1,), (1,)), ((), ())), preferred_element_type=F32)


def _sigmoid(x):
    return 1.0 / (1.0 + jnp.exp(-x))


def _log_sigmoid(x):
    return jnp.minimum(x, 0.0) - jnp.log(1.0 + jnp.exp(-jnp.abs(x)))


def _pack_bf16_pairs(x):
    w = x.shape[1] // 2
    u = lax.bitcast_convert_type(x, U32)
    r = (u + jnp.uint32(0x7FFF) + ((u >> 16) & jnp.uint32(1))) >> 16
    return r[:, :w] | (r[:, w:] << 16)


def _pack_exact_bf16_pairs(x):
    w = x.shape[1] // 2
    u = lax.bitcast_convert_type(x, U32)
    return (u[:, :w] >> 16) | (u[:, w:] & jnp.uint32(0xFFFF0000))


def _ceil_rows(x):
    return jnp.floor((x + (SUBLANES - 1)) * (1.0 / SUBLANES)) * SUBLANES


def _unpack_bf16_pairs(p):
    lo = lax.bitcast_convert_type(p << 16, F32)
    hi = lax.bitcast_convert_type(p & jnp.uint32(0xFFFF0000), F32)
    return lo, hi


def _inproj_kernel(x_ref, g_ref, w_ref, wif_ref, wift_ref, z_ref, zif_ref, zift_ref, hn_ref):
    @pl.when(pl.program_id(1) == 0)
    def _():
        hn = _rms(x_ref[...], g_ref[...]).astype(BF16)
        hn_ref[...] = hn
        zif_ref[...] = _dot(hn, wif_ref[...])
        zift_ref[...] = _dot_nt(wift_ref[...], hn)

    z_ref[...] = _dot(hn_ref[...], w_ref[...]).astype(BF16)


def _inproj(x2d, g, w_main, w_if, w_ift, tm, tn):
    T = x2d.shape[0]
    return pl.pallas_call(
        _inproj_kernel,
        grid=(T // tm, N_MAIN // tn),
        in_specs=[
            pl.BlockSpec((tm, D_MODEL), lambda i, j: (i, 0)),
            pl.BlockSpec((1, D_MODEL), lambda i, j: (0, 0)),
            pl.BlockSpec((D_MODEL, tn), lambda i, j: (0, j)),
            pl.BlockSpec((D_MODEL, LANES), lambda i, j: (0, 0)),
            pl.BlockSpec((SUBLANES, D_MODEL), lambda i, j: (0, 0)),
        ],
        out_specs=[
            pl.BlockSpec((tm, tn), lambda i, j: (i, j)),
            pl.BlockSpec((tm, LANES), lambda i, j: (i, 0)),
            pl.BlockSpec((SUBLANES, tm), lambda i, j: (0, i)),
        ],
        out_shape=[
            jax.ShapeDtypeStruct((T, N_MAIN), BF16),
            jax.ShapeDtypeStruct((T, LANES), F32),
            jax.ShapeDtypeStruct((SUBLANES, T), F32),
        ],
        scratch_shapes=[pltpu.VMEM((tm, D_MODEL), BF16)],
        compiler_params=_cparams(("arbitrary", "arbitrary")),
        name="inproj",
    )(x2d, g, w_main, w_if, w_ift)


def _mlstm_kernel(q_ref, k_ref, v_ref, om_ref, zif_ref, zift_ref, cw_ref, bif_ref, bift_ref, mg_ref,
                  out_ref, qc_ref, kc_ref, kt_ref, carry_ref, c_ref, n_ref, m_ref,
                  bd_ref, bdt_ref, brep_ref, grow_ref, brow_ref, *, ts):
    nchunk = ts // CHUNK
    L = CHUNK

    @pl.when(pl.program_id(1) == 0)
    def _():
        carry_ref[...] = jnp.zeros_like(carry_ref)
        c_ref[...] = jnp.zeros_like(c_ref)
        n_ref[...] = jnp.zeros_like(n_ref)
        m_ref[...] = jnp.zeros_like(m_ref)
        rt = lax.broadcasted_iota(I32, (ts, ts), 0)
        ct = lax.broadcasted_iota(I32, (ts, ts), 1)
        same = (rt // L) == (ct // L)
        bd_ref[...] = jnp.where(same, jnp.where(ct <= rt, 1.0, 0.0), 0.0).astype(BF16)
        bdt_ref[...] = jnp.where(same, jnp.where(rt <= ct, 1.0, 0.0), 0.0).astype(BF16)

    row8 = lax.broadcasted_iota(I32, (SUBLANES, M_QK), 0)

    def conv_silu(x, prev8, w):
        acc = w[CONV_W - 1:CONV_W, :] * x
        for s in range(1, CONV_W):
            xs = pltpu.roll(x, s, 0)
            top = jnp.where(row8 < s, pltpu.roll(prev8, s, 0), xs[0:SUBLANES])
            xs = jnp.concatenate([top, xs[SUBLANES:]], axis=0)
            acc = acc + w[CONV_W - 1 - s:CONV_W - s, :] * xs
        return acc * _sigmoid(acc)

    def conv_body(c, carry):
        r0 = pl.multiple_of(c * L, L)
        xq = q_ref[pl.ds(r0, L), :].astype(F32)
        xk = k_ref[pl.ds(r0, L), :].astype(F32)
        yq = conv_silu(xq, carry_ref[:, 0:M_QK], cw_ref[:, 0:M_QK]) * (M_DK ** -0.5)
        yk = conv_silu(xk, carry_ref[:, M_QK:2 * M_QK], cw_ref[:, M_QK:2 * M_QK])
        qc_ref[pl.ds(r0, L), :] = yq.astype(BF16)
        kc_ref[pl.ds(r0, L), :] = yk.astype(BF16)
        for h in range(M_HEADS):
            kt_ref[c, h] = yk[:, h * M_DK:(h + 1) * M_DK].T
        carry_ref[:, 0:M_QK] = xq[L - SUBLANES:L]
        carry_ref[:, M_QK:2 * M_QK] = xk[L - SUBLANES:L]
        return carry

    lax.fori_loop(0, nchunk, conv_body, 0, unroll=4)

    ti = lax.broadcasted_iota(I32, (L, L), 0)
    si = lax.broadcasted_iota(I32, (L, L), 1)
    causal = si <= ti
    lane_row = lax.broadcasted_iota(I32, (LANES, LANES), 0)
    ones_l = jnp.ones((L, LANES), BF16)

    lf_col = _log_sigmoid(zif_ref[...] + bif_ref[...])
    ch, cl = _split_bf16(lf_col)
    bd = bd_ref[...]
    b_col_all = _dot(jnp.concatenate([bd, bd], axis=1), jnp.concatenate([ch, cl], axis=0))
    bhl = jnp.concatenate(_split_bf16(b_col_all), axis=1)
    for h in range(M_HEADS):
        sel_f = jnp.where(lane_row == M_HEADS + h, 1.0, 0.0).astype(BF16)
        brep_ref[h] = _dot(bhl, jnp.concatenate([sel_f, sel_f], axis=0))
    g_row_all = zift_ref[...] + bift_ref[:, 0:1]
    rhl = jnp.concatenate(_split_bf16(_log_sigmoid(g_row_all)), axis=1)
    bdt = bdt_ref[...]
    b_row_tile = _dot(rhl, jnp.concatenate([bdt, bdt], axis=0))
    for cc in range(nchunk):
        grow_ref[cc] = g_row_all[:, cc * L:(cc + 1) * L]
        brow_ref[cc] = b_row_tile[:, cc * L:(cc + 1) * L]

    def chunk_body(c, carry):
        r0 = pl.multiple_of(c * L, L)
        g_row = grow_ref[c]
        b_row_all = brow_ref[c]
        early = []
        for h in range(M_HEADS):
            b_rep = brep_ref[h, pl.ds(r0, L), :]
            i_row = g_row[h:h + 1, :]
            b_row = b_row_all[M_HEADS + h:M_HEADS + h + 1, :]
            b_last = b_rep[L - 1:L, :]
            q = qc_ref[pl.ds(r0, L), h * M_DK:(h + 1) * M_DK]
            k = kc_ref[pl.ds(r0, L), h * M_DK:(h + 1) * M_DK]
            vext = jnp.concatenate([v_ref[pl.ds(r0, L), h * M_DV:(h + 1) * M_DV], ones_l], axis=1)
            dm = jnp.where(causal, b_rep[:, :L] - b_row + i_row, -jnp.inf)
            m_loc = jnp.max(dm, axis=-1, keepdims=True)
            qk = _dot_nt(q, k)
            gk_row = b_last[:, :L] - b_row + i_row
            g_max = jnp.max(gk_row, axis=-1, keepdims=True)
            kwt = (kt_ref[c, h] * jnp.exp(gk_row - g_max)).astype(BF16)
            kv = _dot(kwt, vext)
            c_old = c_ref[h]
            n_old = n_ref[h]
            qcn = _dot(q, jnp.concatenate([c_old, n_old], axis=1).astype(BF16))
            early.append((b_rep, b_last, vext, dm, m_loc, qk, g_max, kv, c_old, n_old, qcn))
        pvs = []
        for h in range(M_HEADS):
            b_rep, b_last, vext, dm, m_loc, qk, g_max, kv, c_old, n_old, qcn = early[h]
            s_loc = qk * jnp.exp(dm - m_loc)
            pvs.append(_dot(s_loc.astype(BF16), vext))
        for h in range(M_HEADS):
            b_rep, b_last, vext, dm, m_loc, qk, g_max, kv, c_old, n_old, qcn = early[h]
            pv = pvs[h]
            m_prev = m_ref[h:h + 1, :]
            inter = b_rep + m_prev
            m_t = jnp.maximum(inter, m_loc)
            w_inter = jnp.exp(inter - m_t)
            r_loc = jnp.exp(m_loc - m_t)
            den = r_loc * pv[:, M_DV:] + w_inter * qcn[:, M_DV:]
            inv = 1.0 / jnp.maximum(jnp.abs(den), jnp.exp(-m_t))
            hv = (jnp.concatenate([r_loc * inv] * 2, axis=1) * pv[:, :M_DV]
                  + jnp.concatenate([w_inter * inv] * 2, axis=1) * qcn[:, :M_DV])
            m_new = jnp.maximum(b_last + m_prev, g_max)
            decay = jnp.exp(b_last + m_prev - m_new)
            sc_loc = jnp.exp(g_max - m_new)
            c_ref[h] = (jnp.concatenate([decay] * 2, axis=1) * c_old
                        + jnp.concatenate([sc_loc] * 2, axis=1) * kv[:, :M_DV])
            n_ref[h] = decay * n_old + sc_loc * kv[:, M_DV:]
            m_ref[h:h + 1, :] = m_new
            hn = _rms(hv, mg_ref[:, h * M_DV:(h + 1) * M_DV])
            og = _sigmoid(om_ref[pl.ds(r0, L), h * M_DV:(h + 1) * M_DV].astype(F32))
            out_ref[pl.ds(r0, L), h * M_DV:(h + 1) * M_DV] = (og * hn).astype(BF16)
        return carry

    lax.fori_loop(0, nchunk, chunk_body, 0, unroll=4)


def _mlstm(z, zif, zift, conv_w, bif, bift, m_gain, B, S, ts):
    T = B * S
    nt = S // ts
    nck = ts // CHUNK
    row = lambda b, t: b * nt + t
    return pl.pallas_call(
        functools.partial(_mlstm_kernel, ts=ts),
        grid=(B, nt),
        in_specs=[
            pl.BlockSpec((ts, M_QK), lambda b, t: (row(b, t), OFF_QM // M_QK)),
            pl.BlockSpec((ts, M_QK), lambda b, t: (row(b, t), OFF_KM // M_QK)),
            pl.BlockSpec((ts, M_V), lambda b, t: (row(b, t), OFF_VM // M_V)),
            pl.BlockSpec((ts, M_V), lambda b, t: (row(b, t), OFF_OM // M_V)),
            pl.BlockSpec((ts, LANES), lambda b, t: (row(b, t), 0)),
            pl.BlockSpec((SUBLANES, ts), lambda b, t: (0, row(b, t))),
            pl.BlockSpec((CONV_W, 2 * M_QK), lambda b, t: (0, 0)),
            pl.BlockSpec((1, LANES), lambda b, t: (0, 0)),
            pl.BlockSpec((SUBLANES, LANES), lambda b, t: (0, 0)),
            pl.BlockSpec((1, M_V), lambda b, t: (0, 0)),
        ],
        out_specs=pl.BlockSpec((ts, M_V), lambda b, t: (row(b, t), 0)),
        out_shape=jax.ShapeDtypeStruct((T, M_V), BF16),
        scratch_shapes=[
            pltpu.VMEM((ts, M_QK), BF16),
            pltpu.VMEM((ts, M_QK), BF16),
            pltpu.VMEM((nck, M_HEADS, M_DK, CHUNK), F32),
            pltpu.VMEM((SUBLANES, 2 * M_QK), F32),
            pltpu.VMEM((M_HEADS, M_DK, M_DV), F32),
            pltpu.VMEM((M_HEADS, M_DK, LANES), F32),
            pltpu.VMEM((SUBLANES, LANES), F32),
            pltpu.VMEM((ts, ts), BF16),
            pltpu.VMEM((ts, ts), BF16),
            pltpu.VMEM((M_HEADS, ts, LANES), F32),
            pltpu.VMEM((nck, SUBLANES, CHUNK), F32),
            pltpu.VMEM((nck, SUBLANES, CHUNK), F32),
        ],
        compiler_params=_cparams(("arbitrary", "arbitrary")),
        name="mlstm",
    )(z, z, z, z, zif, zift, conv_w, bif, bift, m_gain)


def _diffattn_kernel(q_ref, k_ref, v_ref, lam_ref, gain_ref, out_ref, m_ref, a_ref, *, tq, lam_init):
    qi = pl.program_id(2)
    w = 2 * D_DH
    lane = lax.broadcasted_iota(I32, (1, w), 1)
    scale = jnp.asarray(D_DH ** -0.5, BF16)
    qs = []
    for hh in range(D_HP):
        q = q_ref[:, hh * w:(hh + 1) * w]
        qs.append((jnp.where(lane < D_DH, q, jnp.zeros_like(q)) * scale,
                   jnp.where(lane >= D_DH, q, jnp.zeros_like(q)) * scale))
    ones = jnp.ones((tq, w), BF16)

    def block(k0, nk, rows=slice(None), mask=None, first=False):
        n_stream = 2 * D_HP
        scores = []
        for hh in range(D_HP):
            k = k_ref[pl.ds(k0, nk), hh * w:(hh + 1) * w]
            for comp in range(2):
                s = _dot_nt(qs[hh][comp][rows], k)
                scores.append(s if mask is None else jnp.where(mask, s, -jnp.inf))
        m_news, alphas, probs = [], [], []
        for i in range(n_stream):
            s = scores[i]
            s_max = jnp.max(s, axis=-1, keepdims=True)
            if first:
                m_new = jnp.broadcast_to(s_max, (s.shape[0], w))
                alphas.append(None)
            else:
                m_old = m_ref[i, rows, :]
                m_new = jnp.maximum(m_old, s_max)
                alphas.append(jnp.exp(m_old - m_new))
            m_news.append(m_new)
            probs.append(jnp.exp(s - jnp.concatenate([m_new] * (s.shape[1] // w), axis=1)).astype(BF16))
        pvs = []
        for hh in range(D_HP):
            vext = jnp.concatenate([v_ref[pl.ds(k0, nk), hh * w:(hh + 1) * w], ones[:nk]], axis=1)
            for comp in range(2):
                pvs.append(_dot(probs[2 * hh + comp], vext))
        for i in range(n_stream):
            if first:
                a_ref[i, rows, :] = pvs[i]
            else:
                a_ref[i, rows, :] = jnp.concatenate([alphas[i]] * 2, axis=1) * a_ref[i, rows, :] + pvs[i]
            m_ref[i, rows, :] = m_news[i]

    hq = tq // 2
    d0 = pl.multiple_of(qi * tq, tq)

    def chunk_mask(q0, nk):
        rq = (lax.broadcasted_iota(I32, (hq, nk), 0) + q0) // CHUNK
        ck = lax.broadcasted_iota(I32, (hq, nk), 1) // CHUNK
        return ck <= rq

    block(d0, hq, rows=slice(0, hq), mask=chunk_mask(0, hq), first=True)
    block(d0, tq, rows=slice(hq, tq), mask=chunk_mask(hq, tq), first=True)

    def body(jj, carry):
        block(pl.multiple_of(jj * tq, tq), tq)
        return carry

    lax.fori_loop(0, qi, body, 0)

    lp = lam_ref[...]
    lam = (jnp.exp(jnp.sum(lp[0:1, :] * lp[1:2, :], axis=-1, keepdims=True))
           - jnp.exp(jnp.sum(lp[2:3, :] * lp[3:4, :], axis=-1, keepdims=True)) + lam_init)
    for hh in range(D_HP):
        a1 = a_ref[2 * hh]
        a2 = a_ref[2 * hh + 1]
        o = a1[:, :w] / a1[:, w:] - lam * (a2[:, :w] / a2[:, w:])
        out_ref[:, hh * w:(hh + 1) * w] = (_rms(o, gain_ref[...]) * (1.0 - lam_init)).astype(BF16)


def _diffattn(z, lam_p, d_gain, B, S, tq, lam_init):
    T = B * S
    nq = S // tq
    w = 2 * D_DH
    wp = D_HP * w
    return pl.pallas_call(
        functools.partial(_diffattn_kernel, tq=tq, lam_init=lam_init),
        grid=(B, D_HEADS // D_HP, nq),
        in_specs=[
            pl.BlockSpec((tq, wp), lambda b, h, i: (b * nq + i, OFF_QD // wp + h)),
            pl.BlockSpec((S, wp), lambda b, h, i: (b, OFF_KD // wp + h)),
            pl.BlockSpec((S, wp), lambda b, h, i: (b, OFF_VD // wp + h)),
            pl.BlockSpec((4, D_DH), lambda b, h, i: (0, 0)),
            pl.BlockSpec((1, w), lambda b, h, i: (0, 0)),
        ],
        out_specs=pl.BlockSpec((tq, wp), lambda b, h, i: (b * nq + i, h)),
        out_shape=jax.ShapeDtypeStruct((T, D_V), BF16),
        scratch_shapes=[
            pltpu.VMEM((2 * D_HP, tq, w), F32), pltpu.VMEM((2 * D_HP, tq, 2 * w), F32),
        ],
        compiler_params=_cparams(("arbitrary", "arbitrary", "arbitrary")),
        name="diffattn",
    )(z, z, z, lam_p, d_gain)


def _merge_kernel(hm_ref, hd_ref, gz_ref, x_ref, wbm_ref, wbd_ref, wout_ref, bg_ref, out_ref):
    bm = _dot(hm_ref[...], wbm_ref[...])
    bd = _dot(hd_ref[...], wbd_ref[...])
    g = _sigmoid(gz_ref[...].astype(F32) + bg_ref[...])
    merged = g[:, :D_MODEL] * bm + g[:, D_MODEL:] * bd
    out_ref[...] = x_ref[...] + _dot(merged.astype(BF16), wout_ref[...])


def _merge(hm, hd, z, x2d, w_bm, w_bd, w_out, b_gate, tm):
    T = x2d.shape[0]
    full = lambda i: (0, 0)
    return pl.pallas_call(
        _merge_kernel,
        grid=(T // tm,),
        in_specs=[
            pl.BlockSpec((tm, M_V), lambda i: (i, 0)),
            pl.BlockSpec((tm, D_V), lambda i: (i, 0)),
            pl.BlockSpec((tm, 2 * D_MODEL), lambda i: (i, OFF_G // (2 * D_MODEL))),
            pl.BlockSpec((tm, D_MODEL), lambda i: (i, 0)),
            pl.BlockSpec((M_V, D_MODEL), full),
            pl.BlockSpec((D_V, D_MODEL), full),
            pl.BlockSpec((D_MODEL, D_MODEL), full),
            pl.BlockSpec((1, 2 * D_MODEL), full),
        ],
        out_specs=pl.BlockSpec((tm, D_MODEL), lambda i: (i, 0)),
        out_shape=jax.ShapeDtypeStruct((T, D_MODEL), F32),
        compiler_params=_cparams(("arbitrary",)),
        name="merge",
    )(hm, hd, z, x2d, w_bm, w_bd, w_out, b_gate)


def _memkv_kernel(mem_ref, g_ref, w_ref, out_ref):
    out_ref[...] = _dot(_rms(mem_ref[...], g_ref[...]).astype(BF16), w_ref[...]).astype(BF16)


def _memkv(mem2d, g, wkv, n_mem):
    R = mem2d.shape[0]
    return pl.pallas_call(
        _memkv_kernel,
        grid=(R // n_mem,),
        in_specs=[
            pl.BlockSpec((n_mem, D_MODEL), lambda i: (i, 0)),
            pl.BlockSpec((1, D_MODEL), lambda i: (0, 0)),
            pl.BlockSpec((D_MODEL, 2 * D_MODEL), lambda i: (0, 0)),
        ],
        out_specs=pl.BlockSpec((n_mem, 2 * D_MODEL), lambda i: (i, 0)),
        out_shape=jax.ShapeDtypeStruct((R, 2 * D_MODEL), BF16),
        compiler_params=_cparams(("arbitrary",)),
        name="memkv",
    )(mem2d, g, wkv)


def _xattn_kernel(x_ref, g_ref, wq_ref, kv_ref, wo_ref, out_ref, o_ref):
    x = x_ref[...]
    h = _rms(x, g_ref[...]).astype(BF16)
    q = (_dot(h, wq_ref[...]) * (X_DH ** -0.5)).astype(BF16)
    scores = [_dot_nt(q[:, hd * X_DH:(hd + 1) * X_DH], kv_ref[:, hd * X_DH:(hd + 1) * X_DH]) for hd in range(X_HEADS)]
    for hd in range(X_HEADS):
        vh = kv_ref[:, D_MODEL + hd * X_DH:D_MODEL + (hd + 1) * X_DH]
        s = scores[hd]
        p = jnp.exp(s - jnp.max(s, axis=-1, keepdims=True))
        p = p / jnp.sum(p, axis=-1, keepdims=True)
        o_ref[:, hd * X_DH:(hd + 1) * X_DH] = _dot(p.astype(BF16), vh).astype(BF16)
    out_ref[...] = x + _dot(o_ref[...], wo_ref[...])


def _xattn(x1, g, wq, kvmem, wo, S, n_mem, tm):
    T = x1.shape[0]
    per_b = S // tm
    full = lambda i: (0, 0)
    return pl.pallas_call(
        _xattn_kernel,
        grid=(T // tm,),
        in_specs=[
            pl.BlockSpec((tm, D_MODEL), lambda i: (i, 0)),
            pl.BlockSpec((1, D_MODEL), full),
            pl.BlockSpec((D_MODEL, D_MODEL), full),
            pl.BlockSpec((n_mem, 2 * D_MODEL), lambda i: (i // per_b, 0)),
            pl.BlockSpec((D_MODEL, D_MODEL), full),
        ],
        out_specs=pl.BlockSpec((tm, D_MODEL), lambda i: (i, 0)),
        out_shape=jax.ShapeDtypeStruct((T, D_MODEL), F32),
        scratch_shapes=[pltpu.VMEM((tm, D_MODEL), BF16)],
        compiler_params=_cparams(("arbitrary",)),
        name="xattn",
    )(x1, g, wq, kvmem, wo)


def _router_kernel(x_ref, g_ref, wrh_ref, wrl_ref, br_ref, hp_ref, ids_ref, tw_ref, cnt_ref):
    @pl.when(pl.program_id(0) == 0)
    def _():
        cnt_ref[...] = jnp.zeros_like(cnt_ref)

    hn = _rms(x_ref[...], g_ref[...])
    hh, hl = _split_bf16(hn)
    hp_ref[...] = hh
    logits = _dot(hh, wrh_ref[...]) + _dot(hh, wrl_ref[...]) + _dot(hl, wrh_ref[...]) + br_ref[...]
    lane = lax.broadcasted_iota(I32, logits.shape, 1)
    lanef = lane.astype(F32)
    ids = jnp.zeros(logits.shape, F32)
    tw = jnp.zeros(logits.shape, F32)
    onehot = jnp.zeros(logits.shape, F32)
    v0 = None
    den = None
    for kk in range(TOP_K):
        mx = jnp.max(logits, axis=-1, keepdims=True)
        idx = jnp.min(jnp.where(logits == mx, lanef, float(LANES)), axis=-1, keepdims=True)
        sel = lanef == idx
        if kk == 0:
            v0 = mx
        e = jnp.exp(mx - v0)
        den = e if den is None else den + e
        ids = jnp.where(lane == kk, idx, ids)
        tw = jnp.where(lane == kk, e, tw)
        onehot = jnp.where(sel, 1.0, onehot)
        logits = jnp.where(sel, -jnp.inf, logits)
    ids_ref[...] = ids.astype(I32)
    tw_ref[...] = tw / den
    cnt_ref[...] += _ceil_rows(jnp.sum(onehot, axis=0, keepdims=True))


def _router(x2, g, wr_hi, wr_lo, b_r, tm):
    T = x2.shape[0]
    full = lambda i: (0, 0)
    return pl.pallas_call(
        _router_kernel,
        grid=(T // tm,),
        in_specs=[
            pl.BlockSpec((tm, D_MODEL), lambda i: (i, 0)),
            pl.BlockSpec((1, D_MODEL), full),
            pl.BlockSpec((D_MODEL, LANES), full),
            pl.BlockSpec((D_MODEL, LANES), full),
            pl.BlockSpec((1, LANES), full),
        ],
        out_specs=[
            pl.BlockSpec((tm, D_MODEL), lambda i: (i, 0)),
            pl.BlockSpec((tm, LANES), lambda i: (i, 0)),
            pl.BlockSpec((tm, LANES), lambda i: (i, 0)),
            pl.BlockSpec((1, LANES), full),
        ],
        out_shape=[
            jax.ShapeDtypeStruct((T, D_MODEL), BF16),
            jax.ShapeDtypeStruct((T, LANES), I32),
            jax.ShapeDtypeStruct((T, LANES), F32),
            jax.ShapeDtypeStruct((1, LANES), F32),
        ],
        compiler_params=_cparams(("arbitrary",)),
        name="router",
    )(x2, g, wr_hi, wr_lo, b_r)


SEG_WORDS = SUBLANES * LANES


def _slots_kernel(ids_ref, start_ref, ls_ref, lst_ref, seg_ref, run_ref):
    @pl.when(pl.program_id(0) == 0)
    def _():
        run_ref[...] = jnp.zeros_like(run_ref)

    ids = ids_ref[...]
    tm = ids.shape[0]
    lane = lax.broadcasted_iota(I32, ids.shape, 1)
    sels = [lane == ids[:, kk:kk + 1] for kk in range(TOP_K)]
    onehot = jnp.zeros(ids.shape, F32)
    for s in sels:
        onehot = jnp.where(s, 1.0, onehot)
    c8 = _ceil_rows(jnp.sum(onehot, axis=0, keepdims=True))
    er = lax.broadcasted_iota(I32, (LANES, LANES), 0)
    ec = lax.broadcasted_iota(I32, (LANES, LANES), 1)
    before = jnp.where(er < ec, 1.0, 0.0).astype(BF16)
    pieces = jnp.broadcast_to(c8 * (1.0 / SUBLANES), (SUBLANES, LANES)).astype(BF16)
    lo = _dot(pieces, before)[0:1, :] * SUBLANES
    r = lax.broadcasted_iota(I32, (tm, tm), 0)
    c = lax.broadcasted_iota(I32, (tm, tm), 1)
    strict = jnp.where(c < r, 1.0, 0.0).astype(BF16)
    slot = _dot(strict, onehot.astype(BF16)) + lo
    ls = jnp.zeros(ids.shape, F32)
    for kk, s in enumerate(sels):
        pk = jnp.sum(jnp.where(s, slot, 0.0), axis=-1, keepdims=True)
        ls = jnp.where(lane == kk, pk, ls)
    ls_ref[...] = ls
    hi = jnp.floor(ls * (1.0 / 32.0))
    rem = ls - 32.0 * hi
    pick = jnp.where(lax.broadcasted_iota(I32, (SUBLANES, LANES), 0) == lax.broadcasted_iota(I32, (SUBLANES, LANES), 1),
                     1.0, 0.0).astype(BF16)
    lst_ref[...] = 32.0 * _dot_nt(pick, hi.astype(BF16)) + _dot_nt(pick, rem.astype(BF16))
    row = lax.broadcasted_iota(I32, (SUBLANES, LANES), 0)
    off = start_ref[...] + run_ref[...]
    total = jnp.sum(c8, axis=-1, keepdims=True)
    seg = jnp.where(row == 0, c8, jnp.where(row == 1, lo, jnp.where(row == 2, off, jnp.where(row == 3, total, 0.0))))
    seg_ref[...] = seg.astype(I32)
    run_ref[...] += c8


def _slots(ids, starts, tm):
    T = ids.shape[0]
    nt = T // tm
    return pl.pallas_call(
        _slots_kernel,
        grid=(nt,),
        in_specs=[
            pl.BlockSpec((tm, LANES), lambda i: (i, 0)),
            pl.BlockSpec((1, LANES), lambda i: (0, 0)),
        ],
        out_specs=[
            pl.BlockSpec((tm, LANES), lambda i: (i, 0)),
            pl.BlockSpec((SUBLANES, tm), lambda i: (0, i)),
            pl.BlockSpec((SUBLANES, LANES), lambda i: (i, 0)),
        ],
        out_shape=[
            jax.ShapeDtypeStruct((T, LANES), F32),
            jax.ShapeDtypeStruct((SUBLANES, T), F32),
            jax.ShapeDtypeStruct((nt * SUBLANES, LANES), I32),
        ],
        scratch_shapes=[pltpu.VMEM((1, LANES), F32)],
        compiler_params=_cparams(("arbitrary",)),
        name="slots",
    )(ids, starts)


def _local_rows(tm):
    need = tm * TOP_K + N_EXPERTS * (SUBLANES - 1)
    return ((need + LANES - 1) // LANES) * LANES


BIG_PIECE = 4 * SUBLANES


def _segment_starts(seg, make_copy):
    def expert(e, carry):
        cnt = seg(e)
        lo = seg(LANES + e)
        off = seg(2 * LANES + e)
        n_big = lax.shift_right_logical(cnt, 5)

        def big(j, carry2):
            d = j * BIG_PIECE
            make_copy(pl.multiple_of(lo + d, SUBLANES), pl.multiple_of(off + d, SUBLANES), BIG_PIECE).start()
            return carry2

        lax.fori_loop(0, n_big, big, 0)
        for size in (2 * SUBLANES, SUBLANES):
            d = cnt & ~(2 * size - 1)

            @pl.when((cnt & size) != 0)
            def _(d=d, size=size):
                make_copy(pl.multiple_of(lo + d, SUBLANES), pl.multiple_of(off + d, SUBLANES), size).start()
        return carry

    for e in range(N_EXPERTS):
        expert(e, 0)


def _segment_waits(total_rows, make_copy):
    def big(j, carry):
        make_copy(0, 0, BIG_PIECE).wait()
        return carry

    lax.fori_loop(0, lax.shift_right_logical(total_rows, 5), big, 0)
    for size in (2 * SUBLANES, SUBLANES):
        @pl.when((total_rows & size) != 0)
        def _(size=size):
            make_copy(0, 0, size).wait()


def _dispatch_kernel(h_ref, lst_ref, seg_hbm, xs_ref, sbuf_ref, seg_smem, prev_smem, sem_seg, sem_rows, *, tm):
    i = pl.program_id(0)
    n = pl.num_programs(0)
    slot = i % 2
    rows = sbuf_ref.shape[1]
    cp = pltpu.make_async_copy(seg_hbm.at[i], seg_smem, sem_seg)
    cp.start()
    lst = lst_ref[...].astype(I32)
    rid = lax.broadcasted_iota(I32, (rows, tm), 0)
    perm = jnp.zeros((rows, tm), F32)
    for kk in range(TOP_K):
        perm = jnp.where(rid == lst[kk:kk + 1, :], 1.0, perm)
    srt = _dot(perm.astype(BF16), h_ref[...])
    sbuf_ref[slot] = _pack_exact_bf16_pairs(srt)
    cp.wait()

    def copy_from(s):
        def make_copy(lo, off, nrows):
            return pltpu.make_async_copy(sbuf_ref.at[s, pl.ds(lo, nrows), :], xs_ref.at[pl.ds(off, nrows), :],
                                         sem_rows.at[s])
        return make_copy

    _segment_starts(lambda k: seg_smem[k], copy_from(slot))

    @pl.when(i > 0)
    def _():
        _segment_waits(prev_smem[0], copy_from(1 - slot))

    prev_smem[0] = seg_smem[3 * LANES]

    @pl.when(i == n - 1)
    def _():
        _segment_waits(prev_smem[0], copy_from(slot))


def _dispatch(h, lst, seg2d, n_rows, tm):
    T = h.shape[0]
    rows = _local_rows(tm)
    return pl.pallas_call(
        functools.partial(_dispatch_kernel, tm=tm),
        grid=(T // tm,),
        in_specs=[
            pl.BlockSpec((tm, D_MODEL), lambda i: (i, 0)),
            pl.BlockSpec((SUBLANES, tm), lambda i: (0, i)),
            pl.BlockSpec(memory_space=pl.ANY),
        ],
        out_specs=pl.BlockSpec(memory_space=pl.ANY),
        out_shape=jax.ShapeDtypeStruct((n_rows, D_MODEL // 2), U32),
        scratch_shapes=[
            pltpu.VMEM((2, rows, D_MODEL // 2), U32),
            pltpu.SMEM((SEG_WORDS,), I32),
            pltpu.SMEM((1,), I32),
            pltpu.SemaphoreType.DMA,
            pltpu.SemaphoreType.DMA((2,)),
        ],
        compiler_params=_cparams(("arbitrary",)),
        name="dispatch",
    )(h, lst, seg2d)


def _experts_kernel(te_ref, nu_ref, nv_ref, xs_ref, wgu_ref, bgu_ref, wdn_ref, bdn_ref, ys_ref, wgu_bf, wdn_bf):
    i = pl.program_id(0)

    @pl.when((i == 0) | (te_ref[i] != te_ref[jnp.maximum(i - 1, 0)]))
    def _():
        wgu_bf[...] = wgu_ref[0].astype(BF16)
        wdn_bf[...] = wdn_ref[0].astype(BF16)

    @pl.when(i < nu_ref[0])
    def _():
        live = lax.broadcasted_iota(I32, xs_ref.shape, 0) < nv_ref[i]
        lo, hi = _unpack_bf16_pairs(jnp.where(live, xs_ref[...], jnp.uint32(0)))
        xb = jnp.concatenate([lo.astype(BF16), hi.astype(BF16)], axis=1)
        gu = _dot(xb, wgu_bf[...]) + bgu_ref[0]
        gate = jnp.minimum(gu[:, :D_FF], SWIGLU_LIMIT)
        up = jnp.clip(gu[:, D_FF:], -SWIGLU_LIMIT, SWIGLU_LIMIT)
        act = (up + 1.0) * (gate * _sigmoid(SWIGLU_ALPHA * gate))
        y = _dot(act.astype(BF16), wdn_bf[...]) + bdn_ref[0]
        ys_ref[...] = _pack_bf16_pairs(y)

    @pl.when(i >= nu_ref[0])
    def _():
        ys_ref[...] = jnp.zeros_like(ys_ref)


def _experts(tile_expert, n_used, tile_valid, xs, w_gu, b_gu, w_dn, b_dn, tg):
    P = xs.shape[0]
    half = D_MODEL // 2
    grid_spec = pltpu.PrefetchScalarGridSpec(
        num_scalar_prefetch=3,
        grid=(P // tg,),
        in_specs=[
            pl.BlockSpec((tg, half), lambda i, te, nu, nv: (jnp.minimum(i, jnp.maximum(nu[0] - 1, 0)), 0)),
            pl.BlockSpec((1, D_MODEL, 2 * D_FF), lambda i, te, nu, nv: (te[i], 0, 0)),
            pl.BlockSpec((1, 1, 2 * D_FF), lambda i, te, nu, nv: (te[i], 0, 0)),
            pl.BlockSpec((1, D_FF, D_MODEL), lambda i, te, nu, nv: (te[i], 0, 0)),
            pl.BlockSpec((1, 1, D_MODEL), lambda i, te, nu, nv: (te[i], 0, 0)),
        ],
        out_specs=pl.BlockSpec((tg, half), lambda i, te, nu, nv: (i, 0)),
        scratch_shapes=[pltpu.VMEM((D_MODEL, 2 * D_FF), BF16), pltpu.VMEM((D_FF, D_MODEL), BF16)],
    )
    return pl.pallas_call(
        _experts_kernel,
        grid_spec=grid_spec,
        out_shape=jax.ShapeDtypeStruct((P, half), U32),
        compiler_params=_cparams(("arbitrary",)),
        name="experts",
    )(tile_expert, n_used, tile_valid, xs, w_gu, b_gu, w_dn, b_dn)


def _combine_kernel(seg_hbm, ys_hbm, ls_ref, tw_ref, x_ref, g_ref, out_ref, ybuf_ref, seg_smem, sem_seg, sem_rows, *,
                    tm, final_norm):
    i = pl.program_id(0)
    n = pl.num_programs(0)
    slot = i % 2
    rows = ybuf_ref.shape[1]

    def copy_into(s):
        def make_copy(lo, off, nrows):
            return pltpu.make_async_copy(ys_hbm.at[pl.ds(off, nrows), :], ybuf_ref.at[s, pl.ds(lo, nrows), :],
                                         sem_rows.at[s])
        return make_copy

    def request(step, s):
        cp = pltpu.make_async_copy(seg_hbm.at[step], seg_smem.at[s], sem_seg)
        cp.start()
        cp.wait()
        _segment_starts(lambda k: seg_smem[s, k], copy_into(s))

    @pl.when(i == 0)
    def _():
        ybuf_ref[...] = jnp.zeros_like(ybuf_ref)
        request(0, 0)

    @pl.when(i + 1 < n)
    def _():
        request(i + 1, 1 - slot)

    ls = ls_ref[...].astype(I32)
    tw = tw_ref[...]
    cid = lax.broadcasted_iota(I32, (tm, rows), 1)
    wmat = jnp.zeros((tm, rows), F32)
    for kk in range(TOP_K):
        wmat = jnp.where(cid == ls[:, kk:kk + 1], tw[:, kk:kk + 1], wmat)
    wmat = wmat.astype(BF16)
    _segment_waits(seg_smem[slot, 3 * LANES], copy_into(slot))
    lo, hi = _unpack_bf16_pairs(ybuf_ref[slot])
    moe = jnp.concatenate([_dot(wmat, lo.astype(BF16)), _dot(wmat, hi.astype(BF16))], axis=1)
    x3 = x_ref[...] + moe
    out_ref[...] = _rms(x3, g_ref[...]) if final_norm else x3


def _combine(seg2d, ys, ls, tw, x2, g, tm, final_norm):
    T = x2.shape[0]
    rows = _local_rows(tm)
    return pl.pallas_call(
        functools.partial(_combine_kernel, tm=tm, final_norm=final_norm),
        grid=(T // tm,),
        in_specs=[
            pl.BlockSpec(memory_space=pl.ANY),
            pl.BlockSpec(memory_space=pl.ANY),
            pl.BlockSpec((tm, LANES), lambda i: (i, 0)),
            pl.BlockSpec((tm, LANES), lambda i: (i, 0)),
            pl.BlockSpec((tm, D_MODEL), lambda i: (i, 0)),
            pl.BlockSpec((1, D_MODEL), lambda i: (0, 0)),
        ],
        out_specs=pl.BlockSpec((tm, D_MODEL), lambda i: (i, 0)),
        out_shape=jax.ShapeDtypeStruct((T, D_MODEL), F32),
        scratch_shapes=[
            pltpu.VMEM((2, rows, D_MODEL // 2), U32),
            pltpu.SMEM((2, SEG_WORDS), I32),
            pltpu.SemaphoreType.DMA,
            pltpu.SemaphoreType.DMA((2,)),
        ],
        compiler_params=_cparams(("arbitrary",)),
        name="combine",
    )(seg2d, ys, ls, tw, x2, g)


def _tiles(B, S):
    T = B * S
    return dict(
        tm_in=min(1024, T), tn_in=2048,
        ts=min(512, S),
        tq=min(1024, S),
        tm_proj=min(1024, S),
        tm_route=min(256, T),
        tg=1024,
    )


def _pad_lanes(a, n=LANES, value=0.0):
    return jnp.pad(a, ((0, 0), (0, n - a.shape[1])), constant_values=value)


def kernel(x, mem, norm_mix, w_in, conv_w, b_if, mlstm_gain, diff_lambda, diff_gain, w_branch_m, w_branch_d,
           b_gate, w_out, norm_xattn, norm_mem, wq_x, wkv_x, wo_x, norm_ffn, w_router, b_router, w_gu, b_gu,
           w_dn, b_dn, norm_final):
    B, S, D = x.shape
    n_mem = mem.shape[1]
    T = B * S
    depth = norm_mix.shape[0]
    tl = _tiles(B, S)
    x2d = x.reshape(T, D)
    mem2d = mem.reshape(B * n_mem, D)

    for l in range(depth):
        lam_init = 0.8 - 0.6 * math.exp(-0.3 * l)
        wl = w_in[l]
        if_lo = 2 * M_QK + 2 * M_V
        w_main = jnp.concatenate([wl[:, :if_lo], wl[:, if_lo + 2 * M_HEADS:]], axis=1).astype(BF16)
        w_if = wl[:, if_lo:if_lo + 2 * M_HEADS]
        w_ifp = _pad_lanes(w_if).astype(BF16)
        w_ift = w_if.T.astype(BF16)
        bif = _pad_lanes(b_if[l][None, :])
        bift = jnp.broadcast_to(b_if[l][:, None], (SUBLANES, LANES))

        z, zif, zift = _inproj(x2d, norm_mix[l][None, :], w_main, w_ifp, w_ift, tl["tm_in"], tl["tn_in"])
        hm = _mlstm(z, zif, zift, conv_w[l], bif, bift, mlstm_gain[l].reshape(1, M_V), B, S, tl["ts"])
        hd = _diffattn(z, diff_lambda[l], diff_gain[l][None, :], B, S, tl["tq"], lam_init)
        x1 = _merge(hm, hd, z, x2d, w_branch_m[l].astype(BF16), w_branch_d[l].astype(BF16),
                    w_out[l].astype(BF16), b_gate[l][None, :], tl["tm_proj"])

        kvmem = _memkv(mem2d, norm_mem[l][None, :], wkv_x[l].astype(BF16), n_mem)
        x2 = _xattn(x1, norm_xattn[l][None, :], wq_x[l].astype(BF16), kvmem, wo_x[l].astype(BF16),
                    S, n_mem, tl["tm_proj"])

        wr = _pad_lanes(w_router[l])
        wr_hi = wr.astype(BF16)
        wr_lo = (wr - wr_hi.astype(F32)).astype(BF16)
        br = _pad_lanes(b_router[l][None, :], value=-jnp.inf)
        tm_r = tl["tm_route"]
        tg = tl["tg"]
        hp, ids, tw, cnt = _router(x2, norm_ffn[l][None, :], wr_hi, wr_lo, br, tm_r)

        counts = cnt[0, :N_EXPERTS].astype(I32)
        padded = ((counts + tg - 1) // tg) * tg
        ends = jnp.cumsum(padded)
        starts = ends - padded
        max_rows = T * TOP_K + (T // tm_r) * N_EXPERTS * (SUBLANES - 1)
        n_tiles = -(-max_rows // tg) + N_EXPERTS
        tile_row0 = jnp.arange(n_tiles, dtype=I32) * tg
        tile_expert = jnp.minimum(jnp.sum((tile_row0[:, None] >= ends[None, :]).astype(I32), axis=1), N_EXPERTS - 1)
        n_used = (ends[-1] // tg).astype(I32).reshape(1)
        last_used = tile_expert[jnp.maximum(n_used[0] - 1, 0)]
        tile_expert = jnp.where(tile_row0 < ends[-1], tile_expert, last_used)
        tile_valid = jnp.clip((starts + counts)[tile_expert] - tile_row0, 0, tg).astype(I32)

        ls, lst, seg = _slots(ids, _pad_lanes(starts.astype(F32)[None, :]), tm_r)
        seg2d = seg.reshape(T // tm_r, SEG_WORDS)

        xs = _dispatch(hp, lst, seg2d, n_tiles * tg, tm_r)
        ys = _experts(tile_expert, n_used, tile_valid, xs, w_gu[l], b_gu[l][:, None, :], w_dn[l], b_dn[l][:, None, :],
                      tg)
        x2d = _combine(seg2d, ys, ls, tw, x2, norm_final[None, :], tm_r, final_norm=(l == depth - 1))
    return x2d.reshape(B, S, D)
```

```python
import functools
import math

import jax
import jax.numpy as jnp
from jax import lax
from jax.experimental import pallas as pl
from jax.experimental.pallas import tpu as pltpu

F32 = jnp.float32
BF16 = jnp.bfloat16
U32 = jnp.uint32
I32 = jnp.int32

EPS = 1e-6
CHUNK = 64
D_MODEL = 1024
M_HEADS = 4
M_DK = 128
M_DV = 256
M_QK = M_HEADS * M_DK
M_V = M_HEADS * M_DV
CONV_W = 4
D_HEADS = 8
D_DH = 64
D_HP = 2
D_QK = D_HEADS * 2 * D_DH
D_V = D_HEADS * 2 * D_DH
X_HEADS = 4
X_DH = D_MODEL // X_HEADS
N_EXPERTS = 32
TOP_K = 4
D_FF = D_MODEL
SWIGLU_LIMIT = 7.0
SWIGLU_ALPHA = 1.702

LANES = 128
SUBLANES = 8
N_MAIN = 2 * M_QK + 2 * M_V + 2 * D_QK + D_V + 2 * D_MODEL
OFF_QM, OFF_KM, OFF_VM, OFF_OM = 0, M_QK, 2 * M_QK, 2 * M_QK + M_V
OFF_QD = OFF_OM + M_V
OFF_KD = OFF_QD + D_QK
OFF_VD = OFF_KD + D_QK
OFF_G = OFF_VD + D_V

VMEM_LIMIT = 56 * 1024 * 1024


def _cparams(sem, vmem=VMEM_LIMIT):
    return pltpu.CompilerParams(dimension_semantics=sem, vmem_limit_bytes=vmem)


def _rms(x, g):
    return x * lax.rsqrt(jnp.mean(x * x, axis=-1, keepdims=True) + EPS) * g


def _split_bf16(x):
    hi = x.astype(BF16)
    lo = (x - hi.astype(F32)).astype(BF16)
    return hi, lo


def _dot(a, b):
    return jnp.dot(a, b, preferred_element_type=F32)


def _dot_nt(a, b):
    return lax.dot_general(a, b, (((1,), (1,)), ((), ())), preferred_element_type=F32)


def _sigmoid(x):
    return 1.0 / (1.0 + jnp.exp(-x))


def _log_sigmoid(x):
    return jnp.minimum(x, 0.0) - jnp.log(1.0 + jnp.exp(-jnp.abs(x)))


def _pack_bf16_pairs(x):
    w = x.shape[1] // 2
    u = lax.bitcast_convert_type(x, U32)
    r = (u + jnp.uint32(0x7FFF) + ((u >> 16) & jnp.uint32(1))) >> 16
    return r[:, :w] | (r[:, w:] << 16)


def _pack_exact_bf16_pairs(x):
    w = x.shape[1] // 2
    u = lax.bitcast_convert_type(x, U32)
    return (u[:, :w] >> 16) | (u[:, w:] & jnp.uint32(0xFFFF0000))


def _ceil_rows(x):
    return jnp.floor((x + (SUBLANES - 1)) * (1.0 / SUBLANES)) * SUBLANES


def _unpack_bf16_pairs(p):
    lo = lax.bitcast_convert_type(p << 16, F32)
    hi = lax.bitcast_convert_type(p & jnp.uint32(0xFFFF0000), F32)
    return lo, hi


def _inproj_kernel(x_ref, g_ref, w_ref, wif_ref, wift_ref, z_ref, zif_ref, zift_ref, hn_ref):
    @pl.when(pl.program_id(1) == 0)
    def _():
        hn = _rms(x_ref[...], g_ref[...]).astype(BF16)
        hn_ref[...] = hn
        zif_ref[...] = _dot(hn, wif_ref[...])
        zift_ref[...] = _dot_nt(wift_ref[...], hn)

    z_ref[...] = _dot(hn_ref[...], w_ref[...]).astype(BF16)


def _inproj(x2d, g, w_main, w_if, w_ift, tm, tn):
    T = x2d.shape[0]
    return pl.pallas_call(
        _inproj_kernel,
        grid=(T // tm, N_MAIN // tn),
        in_specs=[
            pl.BlockSpec((tm, D_MODEL), lambda i, j: (i, 0)),
            pl.BlockSpec((1, D_MODEL), lambda i, j: (0, 0)),
            pl.BlockSpec((D_MODEL, tn), lambda i, j: (0, j)),
            pl.BlockSpec((D_MODEL, LANES), lambda i, j: (0, 0)),
            pl.BlockSpec((SUBLANES, D_MODEL), lambda i, j: (0, 0)),
        ],
        out_specs=[
            pl.BlockSpec((tm, tn), lambda i, j: (i, j)),
            pl.BlockSpec((tm, LANES), lambda i, j: (i, 0)),
            pl.BlockSpec((SUBLANES, tm), lambda i, j: (0, i)),
        ],
        out_shape=[
            jax.ShapeDtypeStruct((T, N_MAIN), BF16),
            jax.ShapeDtypeStruct((T, LANES), F32),
            jax.ShapeDtypeStruct((SUBLANES, T), F32),
        ],
        scratch_shapes=[pltpu.VMEM((tm, D_MODEL), BF16)],
        compiler_params=_cparams(("arbitrary", "arbitrary")),
        name="inproj",
    )(x2d, g, w_main, w_if, w_ift)


def _mlstm_kernel(q_ref, k_ref, v_ref, om_ref, zif_ref, zift_ref, cw_ref, bif_ref, bift_ref, mg_ref,
                  out_ref, qc_ref, kc_ref, kt_ref, carry_ref, c_ref, n_ref, m_ref,
                  bd_ref, bdt_ref, brep_ref, grow_ref, brow_ref, *, ts):
    nchunk = ts // CHUNK
    L = CHUNK

    @pl.when(pl.program_id(1) == 0)
    def _():
        carry_ref[...] = jnp.zeros_like(carry_ref)
        c_ref[...] = jnp.zeros_like(c_ref)
        n_ref[...] = jnp.zeros_like(n_ref)
        m_ref[...] = jnp.zeros_like(m_ref)
        rt = lax.broadcasted_iota(I32, (ts, ts), 0)
        ct = lax.broadcasted_iota(I32, (ts, ts), 1)
        same = (rt // L) == (ct // L)
        bd_ref[...] = jnp.where(same, jnp.where(ct <= rt, 1.0, 0.0), 0.0).astype(BF16)
        bdt_ref[...] = jnp.where(same, jnp.where(rt <= ct, 1.0, 0.0), 0.0).astype(BF16)

    row8 = lax.broadcasted_iota(I32, (SUBLANES, M_QK), 0)

    def conv_silu(x, prev8, w):
        acc = w[CONV_W - 1:CONV_W, :] * x
        for s in range(1, CONV_W):
            xs = pltpu.roll(x, s, 0)
            top = jnp.where(row8 < s, pltpu.roll(prev8, s, 0), xs[0:SUBLANES])
            xs = jnp.concatenate([top, xs[SUBLANES:]], axis=0)
            acc = acc + w[CONV_W - 1 - s:CONV_W - s, :] * xs
        return acc * _sigmoid(acc)

    def conv_body(c, carry):
        r0 = pl.multiple_of(c * L, L)
        xq = q_ref[pl.ds(r0, L), :].astype(F32)
        xk = k_ref[pl.ds(r0, L), :].astype(F32)
        yq = conv_silu(xq, carry_ref[:, 0:M_QK], cw_ref[:, 0:M_QK]) * (M_DK ** -0.5)
        yk = conv_silu(xk, carry_ref[:, M_QK:2 * M_QK], cw_ref[:, M_QK:2 * M_QK])
        qc_ref[pl.ds(r0, L), :] = yq.astype(BF16)
        kc_ref[pl.ds(r0, L), :] = yk.astype(BF16)
        for h in range(M_HEADS):
            kt_ref[c, h] = yk[:, h * M_DK:(h + 1) * M_DK].T
        carry_ref[:, 0:M_QK] = xq[L - SUBLANES:L]
        carry_ref[:, M_QK:2 * M_QK] = xk[L - SUBLANES:L]
        return carry

    lax.fori_loop(0, nchunk, conv_body, 0, unroll=4)

    ti = lax.broadcasted_iota(I32, (L, L), 0)
    si = lax.broadcasted_iota(I32, (L, L), 1)
    causal = si <= ti
    lane_row = lax.broadcasted_iota(I32, (LANES, LANES), 0)
    ones_l = jnp.ones((L, LANES), BF16)

    lf_col = _log_sigmoid(zif_ref[...] + bif_ref[...])
    ch, cl = _split_bf16(lf_col)
    bd = bd_ref[...]
    b_col_all = _dot(jnp.concatenate([bd, bd], axis=1), jnp.concatenate([ch, cl], axis=0))
    bhl = jnp.concatenate(_split_bf16(b_col_all), axis=1)
    for h in range(M_HEADS):
        sel_f = jnp.where(lane_row == M_HEADS + h, 1.0, 0.0).astype(BF16)
        brep_ref[h] = _dot(bhl, jnp.concatenate([sel_f, sel_f], axis=0))
    g_row_all = zift_ref[...] + bift_ref[:, 0:1]
    rhl = jnp.concatenate(_split_bf16(_log_sigmoid(g_row_all)), axis=1)
    bdt = bdt_ref[...]
    b_row_tile = _dot(rhl, jnp.concatenate([bdt, bdt], axis=0))
    for cc in range(nchunk):
        grow_ref[cc] = g_row_all[:, cc * L:(cc + 1) * L]
        brow_ref[cc] = b_row_tile[:, cc * L:(cc + 1) * L]

    def chunk_body(c, carry):
        r0 = pl.multiple_of(c * L, L)
        g_row = grow_ref[c]
        b_row_all = brow_ref[c]
        early = []
        for h in range(M_HEADS):
            b_rep = brep_ref[h, pl.ds(r0, L), :]
            i_row = g_row[h:h + 1, :]
            b_row = b_row_all[M_HEADS + h:M_HEADS + h + 1, :]
            b_last = b_rep[L - 1:L, :]
            q = qc_ref[pl.ds(r0, L), h * M_DK:(h + 1) * M_DK]
            k = kc_ref[pl.ds(r0, L), h * M_DK:(h + 1) * M_DK]
            vext = jnp.concatenate([v_ref[pl.ds(r0, L), h * M_DV:(h + 1) * M_DV], ones_l], axis=1)
            dm = jnp.where(causal, b_rep[:, :L] - b_row + i_row, -jnp.inf)
            m_loc = jnp.max(dm, axis=-1, keepdims=True)
            qk = _dot_nt(q, k)
            gk_row = b_last[:, :L] - b_row + i_row
            g_max = jnp.max(gk_row, axis=-1, keepdims=True)
            kwt = (kt_ref[c, h] * jnp.exp(gk_row - g_max)).astype(BF16)
            kv = _dot(kwt, vext)
            c_old = c_ref[h]
            n_old = n_ref[h]
            qcn = _dot(q, jnp.concatenate([c_old, n_old], axis=1).astype(BF16))
            early.append((b_rep, b_last, vext, dm, m_loc, qk, g_max, kv, c_old, n_old, qcn))
        pvs = []
        for h in range(M_HEADS):
            b_rep, b_last, vext, dm, m_loc, qk, g_max, kv, c_old, n_old, qcn = early[h]
            s_loc = qk * jnp.exp(dm - m_loc)
            pvs.append(_dot(s_loc.astype(BF16), vext))
        for h in range(M_HEADS):
            b_rep, b_last, vext, dm, m_loc, qk, g_max, kv, c_old, n_old, qcn = early[h]
            pv = pvs[h]
            m_prev = m_ref[h:h + 1, :]
            inter = b_rep + m_prev
            m_t = jnp.maximum(inter, m_loc)
            w_inter = jnp.exp(inter - m_t)
            r_loc = jnp.exp(m_loc - m_t)
            den = r_loc * pv[:, M_DV:] + w_inter * qcn[:, M_DV:]
            inv = 1.0 / jnp.maximum(jnp.abs(den), jnp.exp(-m_t))
            hv = (jnp.concatenate([r_loc * inv] * 2, axis=1) * pv[:, :M_DV]
                  + jnp.concatenate([w_inter * inv] * 2, axis=1) * qcn[:, :M_DV])
            m_new = jnp.maximum(b_last + m_prev, g_max)
            decay = jnp.exp(b_last + m_prev - m_new)
            sc_loc = jnp.exp(g_max - m_new)
            c_ref[h] = (jnp.concatenate([decay] * 2, axis=1) * c_old
                        + jnp.concatenate([sc_loc] * 2, axis=1) * kv[:, :M_DV])
            n_ref[h] = decay * n_old + sc_loc * kv[:, M_DV:]
            m_ref[h:h + 1, :] = m_new
            hn = _rms(hv, mg_ref[:, h * M_DV:(h + 1) * M_DV])
            og = _sigmoid(om_ref[pl.ds(r0, L), h * M_DV:(h + 1) * M_DV].astype(F32))
            out_ref[pl.ds(r0, L), h * M_DV:(h + 1) * M_DV] = (og * hn).astype(BF16)
        return carry

    lax.fori_loop(0, nchunk, chunk_body, 0, unroll=4)


def _mlstm(z, zif, zift, conv_w, bif, bift, m_gain, B, S, ts):
    T = B * S
    nt = S // ts
    nck = ts // CHUNK
    row = lambda b, t: b * nt + t
    return pl.pallas_call(
        functools.partial(_mlstm_kernel, ts=ts),
        grid=(B, nt),
        in_specs=[
            pl.BlockSpec((ts, M_QK), lambda b, t: (row(b, t), OFF_QM // M_QK)),
            pl.BlockSpec((ts, M_QK), lambda b, t: (row(b, t), OFF_KM // M_QK)),
            pl.BlockSpec((ts, M_V), lambda b, t: (row(b, t), OFF_VM // M_V)),
            pl.BlockSpec((ts, M_V), lambda b, t: (row(b, t), OFF_OM // M_V)),
            pl.BlockSpec((ts, LANES), lambda b, t: (row(b, t), 0)),
            pl.BlockSpec((SUBLANES, ts), lambda b, t: (0, row(b, t))),
            pl.BlockSpec((CONV_W, 2 * M_QK), lambda b, t: (0, 0)),
            pl.BlockSpec((1, LANES), lambda b, t: (0, 0)),
            pl.BlockSpec((SUBLANES, LANES), lambda b, t: (0, 0)),
            pl.BlockSpec((1, M_V), lambda b, t: (0, 0)),
        ],
        out_specs=pl.BlockSpec((ts, M_V), lambda b, t: (row(b, t), 0)),
        out_shape=jax.ShapeDtypeStruct((T, M_V), BF16),
        scratch_shapes=[
            pltpu.VMEM((ts, M_QK), BF16),
            pltpu.VMEM((ts, M_QK), BF16),
            pltpu.VMEM((nck, M_HEADS, M_DK, CHUNK), F32),
            pltpu.VMEM((SUBLANES, 2 * M_QK), F32),
            pltpu.VMEM((M_HEADS, M_DK, M_DV), F32),
            pltpu.VMEM((M_HEADS, M_DK, LANES), F32),
            pltpu.VMEM((SUBLANES, LANES), F32),
            pltpu.VMEM((ts, ts), BF16),
            pltpu.VMEM((ts, ts), BF16),
            pltpu.VMEM((M_HEADS, ts, LANES), F32),
            pltpu.VMEM((nck, SUBLANES, CHUNK), F32),
            pltpu.VMEM((nck, SUBLANES, CHUNK), F32),
        ],
        compiler_params=_cparams(("arbitrary", "arbitrary")),
        name="mlstm",
    )(z, z, z, z, zif, zift, conv_w, bif, bift, m_gain)


def _diffattn_kernel(q_ref, k_ref, v_ref, lam_ref, gain_ref, out_ref, m_ref, a_ref, *, tq, lam_init):
    qi = pl.program_id(2)
    w = 2 * D_DH
    lane = lax.broadcasted_iota(I32, (1, w), 1)
    scale = jnp.asarray(D_DH ** -0.5, BF16)
    qs = []
    for hh in range(D_HP):
        q = q_ref[:, hh * w:(hh + 1) * w]
        qs.append((jnp.where(lane < D_DH, q, jnp.zeros_like(q)) * scale,
                   jnp.where(lane >= D_DH, q, jnp.zeros_like(q)) * scale))
    ones = jnp.ones((tq, w), BF16)

    def block(k0, nk, rows=slice(None), mask=None, first=False):
        n_stream = 2 * D_HP
        scores = []
        for hh in range(D_HP):
            k = k_ref[pl.ds(k0, nk), hh * w:(hh + 1) * w]
            for comp in range(2):
                s = _dot_nt(qs[hh][comp][rows], k)
                scores.append(s if mask is None else jnp.where(mask, s, -jnp.inf))
        m_news, alphas, probs = [], [], []
        for i in range(n_stream):
            s = scores[i]
            s_max = jnp.max(s, axis=-1, keepdims=True)
            if first:
                m_new = jnp.broadcast_to(s_max, (s.shape[0], w))
                alphas.append(None)
            else:
                m_old = m_ref[i, rows, :]
                m_new = jnp.maximum(m_old, s_max)
                alphas.append(jnp.exp(m_old - m_new))
            m_news.append(m_new)
            probs.append(jnp.exp(s - jnp.concatenate([m_new] * (s.shape[1] // w), axis=1)).astype(BF16))
        pvs = []
        for hh in range(D_HP):
            vext = jnp.concatenate([v_ref[pl.ds(k0, nk), hh * w:(hh + 1) * w], ones[:nk]], axis=1)
            for comp in range(2):
                pvs.append(_dot(probs[2 * hh + comp], vext))
        for i in range(n_stream):
            if first:
                a_ref[i, rows, :] = pvs[i]
            else:
                a_ref[i, rows, :] = jnp.concatenate([alphas[i]] * 2, axis=1) * a_ref[i, rows, :] + pvs[i]
            m_ref[i, rows, :] = m_news[i]

    hq = tq // 2
    d0 = pl.multiple_of(qi * tq, tq)

    def chunk_mask(q0, nk):
        rq = (lax.broadcasted_iota(I32, (hq, nk), 0) + q0) // CHUNK
        ck = lax.broadcasted_iota(I32, (hq, nk), 1) // CHUNK
        return ck <= rq

    block(d0, hq, rows=slice(0, hq), mask=chunk_mask(0, hq), first=True)
    block(d0, tq, rows=slice(hq, tq), mask=chunk_mask(hq, tq), first=True)

    def body(jj, carry):
        block(pl.multiple_of(jj * tq, tq), tq)
        return carry

    lax.fori_loop(0, qi, body, 0)

    lp = lam_ref[...]
    lam = (jnp.exp(jnp.sum(lp[0:1, :] * lp[1:2, :], axis=-1, keepdims=True))
           - jnp.exp(jnp.sum(lp[2:3, :] * lp[3:4, :], axis=-1, keepdims=True)) + lam_init)
    for hh in range(D_HP):
        a1 = a_ref[2 * hh]
        a2 = a_ref[2 * hh + 1]
        o = a1[:, :w] / a1[:, w:] - lam * (a2[:, :w] / a2[:, w:])
        out_ref[:, hh * w:(hh + 1) * w] = (_rms(o, gain_ref[...]) * (1.0 - lam_init)).astype(BF16)


def _diffattn(z, lam_p, d_gain, B, S, tq, lam_init):
    T = B * S
    nq = S // tq
    w = 2 * D_DH
    wp = D_HP * w
    return pl.pallas_call(
        functools.partial(_diffattn_kernel, tq=tq, lam_init=lam_init),
        grid=(B, D_HEADS // D_HP, nq),
        in_specs=[
            pl.BlockSpec((tq, wp), lambda b, h, i: (b * nq + i, OFF_QD // wp + h)),
            pl.BlockSpec((S, wp), lambda b, h, i: (b, OFF_KD // wp + h)),
            pl.BlockSpec((S, wp), lambda b, h, i: (b, OFF_VD // wp + h)),
            pl.BlockSpec((4, D_DH), lambda b, h, i: (0, 0)),
            pl.BlockSpec((1, w), lambda b, h, i: (0, 0)),
        ],
        out_specs=pl.BlockSpec((tq, wp), lambda b, h, i: (b * nq + i, h)),
        out_shape=jax.ShapeDtypeStruct((T, D_V), BF16),
        scratch_shapes=[
            pltpu.VMEM((2 * D_HP, tq, w), F32), pltpu.VMEM((2 * D_HP, tq, 2 * w), F32),
        ],
        compiler_params=_cparams(("arbitrary", "arbitrary", "arbitrary")),
        name="diffattn",
    )(z, z, z, lam_p, d_gain)


def _merge_kernel(hm_ref, hd_ref, gz_ref, x_ref, wbm_ref, wbd_ref, wout_ref, bg_ref, out_ref):
    bm = _dot(hm_ref[...], wbm_ref[...])
    bd = _dot(hd_ref[...], wbd_ref[...])
    g = _sigmoid(gz_ref[...].astype(F32) + bg_ref[...])
    merged = g[:, :D_MODEL] * bm + g[:, D_MODEL:] * bd
    out_ref[...] = x_ref[...] + _dot(merged.astype(BF16), wout_ref[...])


def _merge(hm, hd, z, x2d, w_bm, w_bd, w_out, b_gate, tm):
    T = x2d.shape[0]
    full = lambda i: (0, 0)
    return pl.pallas_call(
        _merge_kernel,
        grid=(T // tm,),
        in_specs=[
            pl.BlockSpec((tm, M_V), lambda i: (i, 0)),
            pl.BlockSpec((tm, D_V), lambda i: (i, 0)),
            pl.BlockSpec((tm, 2 * D_MODEL), lambda i: (i, OFF_G // (2 * D_MODEL))),
            pl.BlockSpec((tm, D_MODEL), lambda i: (i, 0)),
            pl.BlockSpec((M_V, D_MODEL), full),
            pl.BlockSpec((D_V, D_MODEL), full),
            pl.BlockSpec((D_MODEL, D_MODEL), full),
            pl.BlockSpec((1, 2 * D_MODEL), full),
        ],
        out_specs=pl.BlockSpec((tm, D_MODEL), lambda i: (i, 0)),
        out_shape=jax.ShapeDtypeStruct((T, D_MODEL), F32),
        compiler_params=_cparams(("arbitrary",)),
        name="merge",
    )(hm, hd, z, x2d, w_bm, w_bd, w_out, b_gate)


def _memkv_kernel(mem_ref, g_ref, w_ref, out_ref):
    out_ref[...] = _dot(_rms(mem_ref[...], g_ref[...]).astype(BF16), w_ref[...]).astype(BF16)


def _memkv(mem2d, g, wkv, n_mem):
    R = mem2d.shape[0]
    return pl.pallas_call(
        _memkv_kernel,
        grid=(R // n_mem,),
        in_specs=[
            pl.BlockSpec((n_mem, D_MODEL), lambda i: (i, 0)),
            pl.BlockSpec((1, D_MODEL), lambda i: (0, 0)),
            pl.BlockSpec((D_MODEL, 2 * D_MODEL), lambda i: (0, 0)),
        ],
        out_specs=pl.BlockSpec((n_mem, 2 * D_MODEL), lambda i: (i, 0)),
        out_shape=jax.ShapeDtypeStruct((R, 2 * D_MODEL), BF16),
        compiler_params=_cparams(("arbitrary",)),
        name="memkv",
    )(mem2d, g, wkv)


def _xattn_kernel(x_ref, g_ref, wq_ref, kv_ref, wo_ref, out_ref, o_ref):
    x = x_ref[...]
    h = _rms(x, g_ref[...]).astype(BF16)
    q = (_dot(h, wq_ref[...]) * (X_DH ** -0.5)).astype(BF16)
    scores = [_dot_nt(q[:, hd * X_DH:(hd + 1) * X_DH], kv_ref[:, hd * X_DH:(hd + 1) * X_DH]) for hd in range(X_HEADS)]
    for hd in range(X_HEADS):
        vh = kv_ref[:, D_MODEL + hd * X_DH:D_MODEL + (hd + 1) * X_DH]
        s = scores[hd]
        p = jnp.exp(s - jnp.max(s, axis=-1, keepdims=True))
        p = p / jnp.sum(p, axis=-1, keepdims=True)
        o_ref[:, hd * X_DH:(hd + 1) * X_DH] = _dot(p.astype(BF16), vh).astype(BF16)
    out_ref[...] = x + _dot(o_ref[...], wo_ref[...])


def _xattn(x1, g, wq, kvmem, wo, S, n_mem, tm):
    T = x1.shape[0]
    per_b = S // tm
    full = lambda i: (0, 0)
    return pl.pallas_call(
        _xattn_kernel,
        grid=(T // tm,),
        in_specs=[
            pl.BlockSpec((tm, D_MODEL), lambda i: (i, 0)),
            pl.BlockSpec((1, D_MODEL), full),
            pl.BlockSpec((D_MODEL, D_MODEL), full),
            pl.BlockSpec((n_mem, 2 * D_MODEL), lambda i: (i // per_b, 0)),
            pl.BlockSpec((D_MODEL, D_MODEL), full),
        ],
        out_specs=pl.BlockSpec((tm, D_MODEL), lambda i: (i, 0)),
        out_shape=jax.ShapeDtypeStruct((T, D_MODEL), F32),
        scratch_shapes=[pltpu.VMEM((tm, D_MODEL), BF16)],
        compiler_params=_cparams(("arbitrary",)),
        name="xattn",
    )(x1, g, wq, kvmem, wo)


def _router_kernel(x_ref, g_ref, wrh_ref, wrl_ref, br_ref, hp_ref, ids_ref, tw_ref, cnt_ref):
    @pl.when(pl.program_id(0) == 0)
    def _():
        cnt_ref[...] = jnp.zeros_like(cnt_ref)

    hn = _rms(x_ref[...], g_ref[...])
    hh, hl = _split_bf16(hn)
    hp_ref[...] = hh
    logits = _dot(hh, wrh_ref[...]) + _dot(hh, wrl_ref[...]) + _dot(hl, wrh_ref[...]) + br_ref[...]
    lane = lax.broadcasted_iota(I32, logits.shape, 1)
    lanef = lane.astype(F32)
    ids = jnp.zeros(logits.shape, F32)
    tw = jnp.zeros(logits.shape, F32)
    onehot = jnp.zeros(logits.shape, F32)
    v0 = None
    den = None
    for kk in range(TOP_K):
        mx = jnp.max(logits, axis=-1, keepdims=True)
        idx = jnp.min(jnp.where(logits == mx, lanef, float(LANES)), axis=-1, keepdims=True)
        sel = lanef == idx
        if kk == 0:
            v0 = mx
        e = jnp.exp(mx - v0)
        den = e if den is None else den + e
        ids = jnp.where(lane == kk, idx, ids)
        tw = jnp.where(lane == kk, e, tw)
        onehot = jnp.where(sel, 1.0, onehot)
        logits = jnp.where(sel, -jnp.inf, logits)
    ids_ref[...] = ids.astype(I32)
    tw_ref[...] = tw / den
    cnt_ref[...] += _ceil_rows(jnp.sum(onehot, axis=0, keepdims=True))


def _router(x2, g, wr_hi, wr_lo, b_r, tm):
    T = x2.shape[0]
    full = lambda i: (0, 0)
    return pl.pallas_call(
        _router_kernel,
        grid=(T // tm,),
        in_specs=[
            pl.BlockSpec((tm, D_MODEL), lambda i: (i, 0)),
            pl.BlockSpec((1, D_MODEL), full),
            pl.BlockSpec((D_MODEL, LANES), full),
            pl.BlockSpec((D_MODEL, LANES), full),
            pl.BlockSpec((1, LANES), full),
        ],
        out_specs=[
            pl.BlockSpec((tm, D_MODEL), lambda i: (i, 0)),
            pl.BlockSpec((tm, LANES), lambda i: (i, 0)),
            pl.BlockSpec((tm, LANES), lambda i: (i, 0)),
            pl.BlockSpec((1, LANES), full),
        ],
        out_shape=[
            jax.ShapeDtypeStruct((T, D_MODEL), BF16),
            jax.ShapeDtypeStruct((T, LANES), I32),
            jax.ShapeDtypeStruct((T, LANES), F32),
            jax.ShapeDtypeStruct((1, LANES), F32),
        ],
        compiler_params=_cparams(("arbitrary",)),
        name="router",
    )(x2, g, wr_hi, wr_lo, b_r)


SEG_WORDS = SUBLANES * LANES


def _slots_kernel(ids_ref, start_ref, ls_ref, lst_ref, seg_ref, run_ref):
    @pl.when(pl.program_id(0) == 0)
    def _():
        run_ref[...] = jnp.zeros_like(run_ref)

    ids = ids_ref[...]
    tm = ids.shape[0]
    lane = lax.broadcasted_iota(I32, ids.shape, 1)
    sels = [lane == ids[:, kk:kk + 1] for kk in range(TOP_K)]
    onehot = jnp.zeros(ids.shape, F32)
    for s in sels:
        onehot = jnp.where(s, 1.0, onehot)
    c8 = _ceil_rows(jnp.sum(onehot, axis=0, keepdims=True))
    er = lax.broadcasted_iota(I32, (LANES, LANES), 0)
    ec = lax.broadcasted_iota(I32, (LANES, LANES), 1)
    before = jnp.where(er < ec, 1.0, 0.0).astype(BF16)
    pieces = jnp.broadcast_to(c8 * (1.0 / SUBLANES), (SUBLANES, LANES)).astype(BF16)
    lo = _dot(pieces, before)[0:1, :] * SUBLANES
    r = lax.broadcasted_iota(I32, (tm, tm), 0)
    c = lax.broadcasted_iota(I32, (tm, tm), 1)
    strict = jnp.where(c < r, 1.0, 0.0).astype(BF16)
    slot = _dot(strict, onehot.astype(BF16)) + lo
    ls = jnp.zeros(ids.shape, F32)
    for kk, s in enumerate(sels):
        pk = jnp.sum(jnp.where(s, slot, 0.0), axis=-1, keepdims=True)
        ls = jnp.where(lane == kk, pk, ls)
    ls_ref[...] = ls
    hi = jnp.floor(ls * (1.0 / 32.0))
    rem = ls - 32.0 * hi
    pick = jnp.where(lax.broadcasted_iota(I32, (SUBLANES, LANES), 0) == lax.broadcasted_iota(I32, (SUBLANES, LANES), 1),
                     1.0, 0.0).astype(BF16)
    lst_ref[...] = 32.0 * _dot_nt(pick, hi.astype(BF16)) + _dot_nt(pick, rem.astype(BF16))
    row = lax.broadcasted_iota(I32, (SUBLANES, LANES), 0)
    off = start_ref[...] + run_ref[...]
    total = jnp.sum(c8, axis=-1, keepdims=True)
    seg = jnp.where(row == 0, c8, jnp.where(row == 1, lo, jnp.where(row == 2, off, jnp.where(row == 3, total, 0.0))))
    seg_ref[...] = seg.astype(I32)
    run_ref[...] += c8


def _slots(ids, starts, tm):
    T = ids.shape[0]
    nt = T // tm
    return pl.pallas_call(
        _slots_kernel,
        grid=(nt,),
        in_specs=[
            pl.BlockSpec((tm, LANES), lambda i: (i, 0)),
            pl.BlockSpec((1, LANES), lambda i: (0, 0)),
        ],
        out_specs=[
            pl.BlockSpec((tm, LANES), lambda i: (i, 0)),
            pl.BlockSpec((SUBLANES, tm), lambda i: (0, i)),
            pl.BlockSpec((SUBLANES, LANES), lambda i: (i, 0)),
        ],
        out_shape=[
            jax.ShapeDtypeStruct((T, LANES), F32),
            jax.ShapeDtypeStruct((SUBLANES, T), F32),
            jax.ShapeDtypeStruct((nt * SUBLANES, LANES), I32),
        ],
        scratch_shapes=[pltpu.VMEM((1, LANES), F32)],
        compiler_params=_cparams(("arbitrary",)),
        name="slots",
    )(ids, starts)


def _local_rows(tm):
    need = tm * TOP_K + N_EXPERTS * (SUBLANES - 1)
    return ((need + LANES - 1) // LANES) * LANES


BIG_SHIFT = 5
BIG_PIECE = 1 << BIG_SHIFT
SMALL_PIECES = tuple(1 << b for b in range(BIG_SHIFT - 1, 2, -1))


def _segment_starts(seg, make_copy):
    def expert(e, carry):
        cnt = seg(e)
        lo = seg(LANES + e)
        off = seg(2 * LANES + e)
        n_big = lax.shift_right_logical(cnt, BIG_SHIFT)

        def big(j, carry2):
            d = j * BIG_PIECE
            make_copy(pl.multiple_of(lo + d, SUBLANES), pl.multiple_of(off + d, SUBLANES), BIG_PIECE).start()
            return carry2

        lax.fori_loop(0, n_big, big, 0)
        for size in SMALL_PIECES:
            d = cnt & ~(2 * size - 1)

            @pl.when((cnt & size) != 0)
            def _(d=d, size=size):
                make_copy(pl.multiple_of(lo + d, SUBLANES), pl.multiple_of(off + d, SUBLANES), size).start()
        return carry

    for e in range(N_EXPERTS):
        expert(e, 0)


def _segment_waits(total_rows, make_copy):
    def big(j, carry):
        make_copy(0, 0, BIG_PIECE).wait()
        return carry

    lax.fori_loop(0, lax.shift_right_logical(total_rows, BIG_SHIFT), big, 0)
    for size in SMALL_PIECES:
        @pl.when((total_rows & size) != 0)
        def _(size=size):
            make_copy(0, 0, size).wait()


def _dispatch_kernel(h_ref, lst_ref, seg_hbm, xs_ref, sbuf_ref, seg_smem, prev_smem, sem_seg, sem_rows, *, tm):
    i = pl.program_id(0)
    n = pl.num_programs(0)
    slot = i % 2
    rows = sbuf_ref.shape[1]
    cp = pltpu.make_async_copy(seg_hbm.at[i], seg_smem, sem_seg)
    cp.start()
    lst = lst_ref[...].astype(I32)
    rid = lax.broadcasted_iota(I32, (rows, tm), 0)
    perm = jnp.zeros((rows, tm), F32)
    for kk in range(TOP_K):
        perm = jnp.where(rid == lst[kk:kk + 1, :], 1.0, perm)
    srt = _dot(perm.astype(BF16), h_ref[...])
    sbuf_ref[slot] = _pack_exact_bf16_pairs(srt)
    cp.wait()

    def copy_from(s):
        def make_copy(lo, off, nrows):
            return pltpu.make_async_copy(sbuf_ref.at[s, pl.ds(lo, nrows), :], xs_ref.at[pl.ds(off, nrows), :],
                                         sem_rows.at[s])
        return make_copy

    _segment_starts(lambda k: seg_smem[k], copy_from(slot))

    @pl.when(i > 0)
    def _():
        _segment_waits(prev_smem[0], copy_from(1 - slot))

    prev_smem[0] = seg_smem[3 * LANES]

    @pl.when(i == n - 1)
    def _():
        _segment_waits(prev_smem[0], copy_from(slot))


def _dispatch(h, lst, seg2d, n_rows, tm):
    T = h.shape[0]
    rows = _local_rows(tm)
    return pl.pallas_call(
        functools.partial(_dispatch_kernel, tm=tm),
        grid=(T // tm,),
        in_specs=[
            pl.BlockSpec((tm, D_MODEL), lambda i: (i, 0)),
            pl.BlockSpec((SUBLANES, tm), lambda i: (0, i)),
            pl.BlockSpec(memory_space=pl.ANY),
        ],
        out_specs=pl.BlockSpec(memory_space=pl.ANY),
        out_shape=jax.ShapeDtypeStruct((n_rows, D_MODEL // 2), U32),
        scratch_shapes=[
            pltpu.VMEM((2, rows, D_MODEL // 2), U32),
            pltpu.SMEM((SEG_WORDS,), I32),
            pltpu.SMEM((1,), I32),
            pltpu.SemaphoreType.DMA,
            pltpu.SemaphoreType.DMA((2,)),
        ],
        compiler_params=_cparams(("arbitrary",)),
        name="dispatch",
    )(h, lst, seg2d)


def _experts_kernel(te_ref, nu_ref, nv_ref, xs_ref, wgu_ref, bgu_ref, wdn_ref, bdn_ref, ys_ref, wgu_bf, wdn_bf):
    i = pl.program_id(0)

    @pl.when((i == 0) | (te_ref[i] != te_ref[jnp.maximum(i - 1, 0)]))
    def _():
        wgu_bf[...] = wgu_ref[0].astype(BF16)
        wdn_bf[...] = wdn_ref[0].astype(BF16)

    @pl.when(i < nu_ref[0])
    def _():
        live = lax.broadcasted_iota(I32, xs_ref.shape, 0) < nv_ref[i]
        lo, hi = _unpack_bf16_pairs(jnp.where(live, xs_ref[...], jnp.uint32(0)))
        xb = jnp.concatenate([lo.astype(BF16), hi.astype(BF16)], axis=1)
        gu = _dot(xb, wgu_bf[...]) + bgu_ref[0]
        gate = jnp.minimum(gu[:, :D_FF], SWIGLU_LIMIT)
        up = jnp.clip(gu[:, D_FF:], -SWIGLU_LIMIT, SWIGLU_LIMIT)
        act = (up + 1.0) * (gate * _sigmoid(SWIGLU_ALPHA * gate))
        y = _dot(act.astype(BF16), wdn_bf[...]) + bdn_ref[0]
        ys_ref[...] = _pack_bf16_pairs(y)

    @pl.when(i >= nu_ref[0])
    def _():
        ys_ref[...] = jnp.zeros_like(ys_ref)


def _experts(tile_expert, n_used, tile_valid, xs, w_gu, b_gu, w_dn, b_dn, tg):
    P = xs.shape[0]
    half = D_MODEL // 2
    grid_spec = pltpu.PrefetchScalarGridSpec(
        num_scalar_prefetch=3,
        grid=(P // tg,),
        in_specs=[
            pl.BlockSpec((tg, half), lambda i, te, nu, nv: (jnp.minimum(i, jnp.maximum(nu[0] - 1, 0)), 0)),
            pl.BlockSpec((1, D_MODEL, 2 * D_FF), lambda i, te, nu, nv: (te[i], 0, 0)),
            pl.BlockSpec((1, 1, 2 * D_FF), lambda i, te, nu, nv: (te[i], 0, 0)),
            pl.BlockSpec((1, D_FF, D_MODEL), lambda i, te, nu, nv: (te[i], 0, 0)),
            pl.BlockSpec((1, 1, D_MODEL), lambda i, te, nu, nv: (te[i], 0, 0)),
        ],
        out_specs=pl.BlockSpec((tg, half), lambda i, te, nu, nv: (i, 0)),
        scratch_shapes=[pltpu.VMEM((D_MODEL, 2 * D_FF), BF16), pltpu.VMEM((D_FF, D_MODEL), BF16)],
    )
    return pl.pallas_call(
        _experts_kernel,
        grid_spec=grid_spec,
        out_shape=jax.ShapeDtypeStruct((P, half), U32),
        compiler_params=_cparams(("arbitrary",)),
        name="experts",
    )(tile_expert, n_used, tile_valid, xs, w_gu, b_gu, w_dn, b_dn)


def _combine_kernel(seg_hbm, ys_hbm, ls_ref, tw_ref, x_ref, g_ref, out_ref, ybuf_ref, seg_smem, sem_seg, sem_rows, *,
                    tm, final_norm):
    i = pl.program_id(0)
    n = pl.num_programs(0)
    slot = i % 2
    rows = ybuf_ref.shape[1]

    def copy_into(s):
        def make_copy(lo, off, nrows):
            return pltpu.make_async_copy(ys_hbm.at[pl.ds(off, nrows), :], ybuf_ref.at[s, pl.ds(lo, nrows), :],
                                         sem_rows.at[s])
        return make_copy

    def request(step, s):
        cp = pltpu.make_async_copy(seg_hbm.at[step], seg_smem.at[s], sem_seg)
        cp.start()
        cp.wait()
        _segment_starts(lambda k: seg_smem[s, k], copy_into(s))

    @pl.when(i == 0)
    def _():
        ybuf_ref[...] = jnp.zeros_like(ybuf_ref)
        request(0, 0)

    @pl.when(i + 1 < n)
    def _():
        request(i + 1, 1 - slot)

    ls = ls_ref[...].astype(I32)
    tw = tw_ref[...]
    cid = lax.broadcasted_iota(I32, (tm, rows), 1)
    wmat = jnp.zeros((tm, rows), F32)
    for kk in range(TOP_K):
        wmat = jnp.where(cid == ls[:, kk:kk + 1], tw[:, kk:kk + 1], wmat)
    wmat = wmat.astype(BF16)
    _segment_waits(seg_smem[slot, 3 * LANES], copy_into(slot))
    lo, hi = _unpack_bf16_pairs(ybuf_ref[slot])
    moe = jnp.concatenate([_dot(wmat, lo.astype(BF16)), _dot(wmat, hi.astype(BF16))], axis=1)
    x3 = x_ref[...] + moe
    out_ref[...] = _rms(x3, g_ref[...]) if final_norm else x3


def _combine(seg2d, ys, ls, tw, x2, g, tm, final_norm):
    T = x2.shape[0]
    rows = _local_rows(tm)
    return pl.pallas_call(
        functools.partial(_combine_kernel, tm=tm, final_norm=final_norm),
        grid=(T // tm,),
        in_specs=[
            pl.BlockSpec(memory_space=pl.ANY),
            pl.BlockSpec(memory_space=pl.ANY),
            pl.BlockSpec((tm, LANES), lambda i: (i, 0)),
            pl.BlockSpec((tm, LANES), lambda i: (i, 0)),
            pl.BlockSpec((tm, D_MODEL), lambda i: (i, 0)),
            pl.BlockSpec((1, D_MODEL), lambda i: (0, 0)),
        ],
        out_specs=pl.BlockSpec((tm, D_MODEL), lambda i: (i, 0)),
        out_shape=jax.ShapeDtypeStruct((T, D_MODEL), F32),
        scratch_shapes=[
            pltpu.VMEM((2, rows, D_MODEL // 2), U32),
            pltpu.SMEM((2, SEG_WORDS), I32),
            pltpu.SemaphoreType.DMA,
            pltpu.SemaphoreType.DMA((2,)),
        ],
        compiler_params=_cparams(("arbitrary",)),
        name="combine",
    )(seg2d, ys, ls, tw, x2, g)


def _tiles(B, S):
    T = B * S
    return dict(
        tm_in=min(1024, T), tn_in=2048,
        ts=min(512, S),
        tq=min(1024, S),
        tm_proj=min(1024, S),
        tm_route=min(256, T),
        tg=1024,
    )


def _pad_lanes(a, n=LANES, value=0.0):
    return jnp.pad(a, ((0, 0), (0, n - a.shape[1])), constant_values=value)


def kernel(x, mem, norm_mix, w_in, conv_w, b_if, mlstm_gain, diff_lambda, diff_gain, w_branch_m, w_branch_d,
           b_gate, w_out, norm_xattn, norm_mem, wq_x, wkv_x, wo_x, norm_ffn, w_router, b_router, w_gu, b_gu,
           w_dn, b_dn, norm_final):
    B, S, D = x.shape
    n_mem = mem.shape[1]
    T = B * S
    depth = norm_mix.shape[0]
    tl = _tiles(B, S)
    x2d = x.reshape(T, D)
    mem2d = mem.reshape(B * n_mem, D)

    for l in range(depth):
        lam_init = 0.8 - 0.6 * math.exp(-0.3 * l)
        wl = w_in[l]
        if_lo = 2 * M_QK + 2 * M_V
        w_main = jnp.concatenate([wl[:, :if_lo], wl[:, if_lo + 2 * M_HEADS:]], axis=1).astype(BF16)
        w_if = wl[:, if_lo:if_lo + 2 * M_HEADS]
        w_ifp = _pad_lanes(w_if).astype(BF16)
        w_ift = w_if.T.astype(BF16)
        bif = _pad_lanes(b_if[l][None, :])
        bift = jnp.broadcast_to(b_if[l][:, None], (SUBLANES, LANES))

        z, zif, zift = _inproj(x2d, norm_mix[l][None, :], w_main, w_ifp, w_ift, tl["tm_in"], tl["tn_in"])
        hm = _mlstm(z, zif, zift, conv_w[l], bif, bift, mlstm_gain[l].reshape(1, M_V), B, S, tl["ts"])
        hd = _diffattn(z, diff_lambda[l], diff_gain[l][None, :], B, S, tl["tq"], lam_init)
        x1 = _merge(hm, hd, z, x2d, w_branch_m[l].astype(BF16), w_branch_d[l].astype(BF16),
                    w_out[l].astype(BF16), b_gate[l][None, :], tl["tm_proj"])

        kvmem = _memkv(mem2d, norm_mem[l][None, :], wkv_x[l].astype(BF16), n_mem)
        x2 = _xattn(x1, norm_xattn[l][None, :], wq_x[l].astype(BF16), kvmem, wo_x[l].astype(BF16),
                    S, n_mem, tl["tm_proj"])

        wr = _pad_lanes(w_router[l])
        wr_hi = wr.astype(BF16)
        wr_lo = (wr - wr_hi.astype(F32)).astype(BF16)
        br = _pad_lanes(b_router[l][None, :], value=-jnp.inf)
        tm_r = tl["tm_route"]
        tg = tl["tg"]
        hp, ids, tw, cnt = _router(x2, norm_ffn[l][None, :], wr_hi, wr_lo, br, tm_r)

        counts = cnt[0, :N_EXPERTS].astype(I32)
        padded = ((counts + tg - 1) // tg) * tg
        ends = jnp.cumsum(padded)
        starts = ends - padded
        max_rows = T * TOP_K + (T // tm_r) * N_EXPERTS * (SUBLANES - 1)
        n_tiles = -(-max_rows // tg) + N_EXPERTS
        tile_row0 = jnp.arange(n_tiles, dtype=I32) * tg
        tile_expert = jnp.minimum(jnp.sum((tile_row0[:, None] >= ends[None, :]).astype(I32), axis=1), N_EXPERTS - 1)
        n_used = (ends[-1] // tg).astype(I32).reshape(1)
        last_used = tile_expert[jnp.maximum(n_used[0] - 1, 0)]
        tile_expert = jnp.where(tile_row0 < ends[-1], tile_expert, last_used)
        tile_valid = jnp.clip((starts + counts)[tile_expert] - tile_row0, 0, tg).astype(I32)

        ls, lst, seg = _slots(ids, _pad_lanes(starts.astype(F32)[None, :]), tm_r)
        seg2d = seg.reshape(T // tm_r, SEG_WORDS)

        xs = _dispatch(hp, lst, seg2d, n_tiles * tg, tm_r)
        ys = _experts(tile_expert, n_used, tile_valid, xs, w_gu[l], b_gu[l][:, None, :], w_dn[l], b_dn[l][:, None, :],
                      tg)
        x2d = _combine(seg2d, ys, ls, tw, x2, norm_final[None, :], tm_r, final_norm=(l == depth - 1))
    return x2d.reshape(B, S, D)
```

```python
import functools
import math

import jax
import jax.numpy as jnp
from jax import lax
from jax.experimental import pallas as pl
from jax.experimental.pallas import tpu as pltpu

F32 = jnp.float32
BF16 = jnp.bfloat16
U32 = jnp.uint32
I32 = jnp.int32

EPS = 1e-6
CHUNK = 64
D_MODEL = 1024
M_HEADS = 4
M_DK = 128
M_DV = 256
M_QK = M_HEADS * M_DK
M_V = M_HEADS * M_DV
CONV_W = 4
D_HEADS = 8
D_DH = 64
D_HP = 2
D_QK = D_HEADS * 2 * D_DH
D_V = D_HEADS * 2 * D_DH
X_HEADS = 4
X_DH = D_MODEL // X_HEADS
N_EXPERTS = 32
TOP_K = 4
D_FF = D_MODEL
SWIGLU_LIMIT = 7.0
SWIGLU_ALPHA = 1.702

LANES = 128
SUBLANES = 8
N_MAIN = 2 * M_QK + 2 * M_V + 2 * D_QK + D_V + 2 * D_MODEL
OFF_QM, OFF_KM, OFF_VM, OFF_OM = 0, M_QK, 2 * M_QK, 2 * M_QK + M_V
OFF_QD = OFF_OM + M_V
OFF_KD = OFF_QD + D_QK
OFF_VD = OFF_KD + D_QK
OFF_G = OFF_VD + D_V

VMEM_LIMIT = 56 * 1024 * 1024


def _cparams(sem, vmem=VMEM_LIMIT):
    return pltpu.CompilerParams(dimension_semantics=sem, vmem_limit_bytes=vmem)


def _rms(x, g):
    return x * lax.rsqrt(jnp.mean(x * x, axis=-1, keepdims=True) + EPS) * g


def _split_bf16(x):
    hi = x.astype(BF16)
    lo = (x - hi.astype(F32)).astype(BF16)
    return hi, lo


def _dot(a, b):
    return jnp.dot(a, b, preferred_element_type=F32)


def _dot_nt(a, b):
    return lax.dot_general(a, b, (((1,), (1,)), ((), ())), preferred_element_type=F32)


def _sigmoid(x):
    return 1.0 / (1.0 + jnp.exp(-x))


def _log_sigmoid(x):
    return jnp.minimum(x, 0.0) - jnp.log(1.0 + jnp.exp(-jnp.abs(x)))


def _pack_bf16_pairs(x):
    w = x.shape[1] // 2
    u = lax.bitcast_convert_type(x, U32)
    r = (u + jnp.uint32(0x7FFF) + ((u >> 16) & jnp.uint32(1))) >> 16
    return r[:, :w] | (r[:, w:] << 16)


def _pack_exact_bf16_pairs(x):
    w = x.shape[1] // 2
    u = lax.bitcast_convert_type(x, U32)
    return (u[:, :w] >> 16) | (u[:, w:] & jnp.uint32(0xFFFF0000))


def _ceil_rows(x):
    return jnp.floor((x + (SUBLANES - 1)) * (1.0 / SUBLANES)) * SUBLANES


def _unpack_bf16_pairs(p):
    lo = lax.bitcast_convert_type(p << 16, F32)
    hi = lax.bitcast_convert_type(p & jnp.uint32(0xFFFF0000), F32)
    return lo, hi


def _inproj_kernel(x_ref, g_ref, w_ref, wif_ref, wift_ref, z_ref, zif_ref, zift_ref, hn_ref):
    @pl.when(pl.program_id(1) == 0)
    def _():
        hn = _rms(x_ref[...], g_ref[...]).astype(BF16)
        hn_ref[...] = hn
        zif_ref[...] = _dot(hn, wif_ref[...])
        zift_ref[...] = _dot_nt(wift_ref[...], hn)

    z_ref[...] = _dot(hn_ref[...], w_ref[...]).astype(BF16)


def _inproj(x2d, g, w_main, w_if, w_ift, tm, tn):
    T = x2d.shape[0]
    return pl.pallas_call(
        _inproj_kernel,
        grid=(T // tm, N_MAIN // tn),
        in_specs=[
            pl.BlockSpec((tm, D_MODEL), lambda i, j: (i, 0)),
            pl.BlockSpec((1, D_MODEL), lambda i, j: (0, 0)),
            pl.BlockSpec((D_MODEL, tn), lambda i, j: (0, j)),
            pl.BlockSpec((D_MODEL, LANES), lambda i, j: (0, 0)),
            pl.BlockSpec((SUBLANES, D_MODEL), lambda i, j: (0, 0)),
        ],
        out_specs=[
            pl.BlockSpec((tm, tn), lambda i, j: (i, j)),
            pl.BlockSpec((tm, LANES), lambda i, j: (i, 0)),
            pl.BlockSpec((SUBLANES, tm), lambda i, j: (0, i)),
        ],
        out_shape=[
            jax.ShapeDtypeStruct((T, N_MAIN), BF16),
            jax.ShapeDtypeStruct((T, LANES), F32),
            jax.ShapeDtypeStruct((SUBLANES, T), F32),
        ],
        scratch_shapes=[pltpu.VMEM((tm, D_MODEL), BF16)],
        compiler_params=_cparams(("arbitrary", "arbitrary")),
        name="inproj",
    )(x2d, g, w_main, w_if, w_ift)


def _mlstm_kernel(q_ref, k_ref, v_ref, om_ref, zif_ref, zift_ref, cw_ref, bif_ref, bift_ref, mg_ref,
                  out_ref, qc_ref, kc_ref, kt_ref, carry_ref, c_ref, n_ref, m_ref,
                  bd_ref, bdt_ref, brep_ref, grow_ref, brow_ref, *, ts):
    nchunk = ts // CHUNK
    L = CHUNK

    @pl.when(pl.program_id(1) == 0)
    def _():
        carry_ref[...] = jnp.zeros_like(carry_ref)
        c_ref[...] = jnp.zeros_like(c_ref)
        n_ref[...] = jnp.zeros_like(n_ref)
        m_ref[...] = jnp.zeros_like(m_ref)
        rt = lax.broadcasted_iota(I32, (ts, ts), 0)
        ct = lax.broadcasted_iota(I32, (ts, ts), 1)
        same = (rt // L) == (ct // L)
        bd_ref[...] = jnp.where(same, jnp.where(ct <= rt, 1.0, 0.0), 0.0).astype(BF16)
        bdt_ref[...] = jnp.where(same, jnp.where(rt <= ct, 1.0, 0.0), 0.0).astype(BF16)

    row8 = lax.broadcasted_iota(I32, (SUBLANES, M_QK), 0)

    def conv_silu(x, prev8, w):
        acc = w[CONV_W - 1:CONV_W, :] * x
        for s in range(1, CONV_W):
            xs = pltpu.roll(x, s, 0)
            top = jnp.where(row8 < s, pltpu.roll(prev8, s, 0), xs[0:SUBLANES])
            xs = jnp.concatenate([top, xs[SUBLANES:]], axis=0)
            acc = acc + w[CONV_W - 1 - s:CONV_W - s, :] * xs
        return acc * _sigmoid(acc)

    def conv_body(c, carry):
        r0 = pl.multiple_of(c * L, L)
        xq = q_ref[pl.ds(r0, L), :].astype(F32)
        xk = k_ref[pl.ds(r0, L), :].astype(F32)
        yq = conv_silu(xq, carry_ref[:, 0:M_QK], cw_ref[:, 0:M_QK]) * (M_DK ** -0.5)
        yk = conv_silu(xk, carry_ref[:, M_QK:2 * M_QK], cw_ref[:, M_QK:2 * M_QK])
        qc_ref[pl.ds(r0, L), :] = yq.astype(BF16)
        kc_ref[pl.ds(r0, L), :] = yk.astype(BF16)
        for h in range(M_HEADS):
            kt_ref[c, h] = yk[:, h * M_DK:(h + 1) * M_DK].T
        carry_ref[:, 0:M_QK] = xq[L - SUBLANES:L]
        carry_ref[:, M_QK:2 * M_QK] = xk[L - SUBLANES:L]
        return carry

    lax.fori_loop(0, nchunk, conv_body, 0, unroll=4)

    ti = lax.broadcasted_iota(I32, (L, L), 0)
    si = lax.broadcasted_iota(I32, (L, L), 1)
    causal = si <= ti
    lane_row = lax.broadcasted_iota(I32, (LANES, LANES), 0)
    ones_l = jnp.ones((L, LANES), BF16)

    lf_col = _log_sigmoid(zif_ref[...] + bif_ref[...])
    ch, cl = _split_bf16(lf_col)
    bd = bd_ref[...]
    b_col_all = _dot(jnp.concatenate([bd, bd], axis=1), jnp.concatenate([ch, cl], axis=0))
    bhl = jnp.concatenate(_split_bf16(b_col_all), axis=1)
    for h in range(M_HEADS):
        sel_f = jnp.where(lane_row == M_HEADS + h, 1.0, 0.0).astype(BF16)
        brep_ref[h] = _dot(bhl, jnp.concatenate([sel_f, sel_f], axis=0))
    g_row_all = zift_ref[...] + bift_ref[:, 0:1]
    rhl = jnp.concatenate(_split_bf16(_log_sigmoid(g_row_all)), axis=1)
    bdt = bdt_ref[...]
    b_row_tile = _dot(rhl, jnp.concatenate([bdt, bdt], axis=0))
    for cc in range(nchunk):
        grow_ref[cc] = g_row_all[:, cc * L:(cc + 1) * L]
        brow_ref[cc] = b_row_tile[:, cc * L:(cc + 1) * L]

    def chunk_body(c, carry):
        r0 = pl.multiple_of(c * L, L)
        g_row = grow_ref[c]
        b_row_all = brow_ref[c]
        early = []
        for h in range(M_HEADS):
            b_rep = brep_ref[h, pl.ds(r0, L), :]
            i_row = g_row[h:h + 1, :]
            b_row = b_row_all[M_HEADS + h:M_HEADS + h + 1, :]
            b_last = b_rep[L - 1:L, :]
            q = qc_ref[pl.ds(r0, L), h * M_DK:(h + 1) * M_DK]
            k = kc_ref[pl.ds(r0, L), h * M_DK:(h + 1) * M_DK]
            vext = jnp.concatenate([v_ref[pl.ds(r0, L), h * M_DV:(h + 1) * M_DV], ones_l], axis=1)
            dm = jnp.where(causal, b_rep[:, :L] - b_row + i_row, -jnp.inf)
            m_loc = jnp.max(dm, axis=-1, keepdims=True)
            qk = _dot_nt(q, k)
            gk_row = b_last[:, :L] - b_row + i_row
            g_max = jnp.max(gk_row, axis=-1, keepdims=True)
            kwt = (kt_ref[c, h] * jnp.exp(gk_row - g_max)).astype(BF16)
            kv = _dot(kwt, vext)
            c_old = c_ref[h]
            n_old = n_ref[h]
            qcn = _dot(q, jnp.concatenate([c_old, n_old], axis=1).astype(BF16))
            early.append((b_rep, b_last, vext, dm, m_loc, qk, g_max, kv, c_old, n_old, qcn))
        pvs = []
        for h in range(M_HEADS):
            b_rep, b_last, vext, dm, m_loc, qk, g_max, kv, c_old, n_old, qcn = early[h]
            s_loc = qk * jnp.exp(dm - m_loc)
            pvs.append(_dot(s_loc.astype(BF16), vext))
        for h in range(M_HEADS):
            b_rep, b_last, vext, dm, m_loc, qk, g_max, kv, c_old, n_old, qcn = early[h]
            pv = pvs[h]
            m_prev = m_ref[h:h + 1, :]
            inter = b_rep + m_prev
            m_t = jnp.maximum(inter, m_loc)
            w_inter = jnp.exp(inter - m_t)
            r_loc = jnp.exp(m_loc - m_t)
            den = r_loc * pv[:, M_DV:] + w_inter * qcn[:, M_DV:]
            inv = 1.0 / jnp.maximum(jnp.abs(den), jnp.exp(-m_t))
            hv = (jnp.concatenate([r_loc * inv] * 2, axis=1) * pv[:, :M_DV]
                  + jnp.concatenate([w_inter * inv] * 2, axis=1) * qcn[:, :M_DV])
            m_new = jnp.maximum(b_last + m_prev, g_max)
            decay = jnp.exp(b_last + m_prev - m_new)
            sc_loc = jnp.exp(g_max - m_new)
            c_ref[h] = (jnp.concatenate([decay] * 2, axis=1) * c_old
                        + jnp.concatenate([sc_loc] * 2, axis=1) * kv[:, :M_DV])
            n_ref[h] = decay * n_old + sc_loc * kv[:, M_DV:]
            m_ref[h:h + 1, :] = m_new
            hn = _rms(hv, mg_ref[:, h * M_DV:(h + 1) * M_DV])
            og = _sigmoid(om_ref[pl.ds(r0, L), h * M_DV:(h + 1) * M_DV].astype(F32))
            out_ref[pl.ds(r0, L), h * M_DV:(h + 1) * M_DV] = (og * hn).astype(BF16)
        return carry

    lax.fori_loop(0, nchunk, chunk_body, 0, unroll=4)


def _mlstm(z, zif, zift, conv_w, bif, bift, m_gain, B, S, ts):
    T = B * S
    nt = S // ts
    nck = ts // CHUNK
    row = lambda b, t: b * nt + t
    return pl.pallas_call(
        functools.partial(_mlstm_kernel, ts=ts),
        grid=(B, nt),
        in_specs=[
            pl.BlockSpec((ts, M_QK), lambda b, t: (row(b, t), OFF_QM // M_QK)),
            pl.BlockSpec((ts, M_QK), lambda b, t: (row(b, t), OFF_KM // M_QK)),
            pl.BlockSpec((ts, M_V), lambda b, t: (row(b, t), OFF_VM // M_V)),
            pl.BlockSpec((ts, M_V), lambda b, t: (row(b, t), OFF_OM // M_V)),
            pl.BlockSpec((ts, LANES), lambda b, t: (row(b, t), 0)),
            pl.BlockSpec((SUBLANES, ts), lambda b, t: (0, row(b, t))),
            pl.BlockSpec((CONV_W, 2 * M_QK), lambda b, t: (0, 0)),
            pl.BlockSpec((1, LANES), lambda b, t: (0, 0)),
            pl.BlockSpec((SUBLANES, LANES), lambda b, t: (0, 0)),
            pl.BlockSpec((1, M_V), lambda b, t: (0, 0)),
        ],
        out_specs=pl.BlockSpec((ts, M_V), lambda b, t: (row(b, t), 0)),
        out_shape=jax.ShapeDtypeStruct((T, M_V), BF16),
        scratch_shapes=[
            pltpu.VMEM((ts, M_QK), BF16),
            pltpu.VMEM((ts, M_QK), BF16),
            pltpu.VMEM((nck, M_HEADS, M_DK, CHUNK), F32),
            pltpu.VMEM((SUBLANES, 2 * M_QK), F32),
            pltpu.VMEM((M_HEADS, M_DK, M_DV), F32),
            pltpu.VMEM((M_HEADS, M_DK, LANES), F32),
            pltpu.VMEM((SUBLANES, LANES), F32),
            pltpu.VMEM((ts, ts), BF16),
            pltpu.VMEM((ts, ts), BF16),
            pltpu.VMEM((M_HEADS, ts, LANES), F32),
            pltpu.VMEM((nck, SUBLANES, CHUNK), F32),
            pltpu.VMEM((nck, SUBLANES, CHUNK), F32),
        ],
        compiler_params=_cparams(("arbitrary", "arbitrary")),
        name="mlstm",
    )(z, z, z, z, zif, zift, conv_w, bif, bift, m_gain)


def _diffattn_kernel(q_ref, k_ref, v_ref, lam_ref, gain_ref, out_ref, m_ref, a_ref, *, tq, lam_init):
    qi = pl.program_id(2)
    w = 2 * D_DH
    lane = lax.broadcasted_iota(I32, (1, w), 1)
    scale = jnp.asarray(D_DH ** -0.5, BF16)
    qs = []
    for hh in range(D_HP):
        q = q_ref[:, hh * w:(hh + 1) * w]
        qs.append((jnp.where(lane < D_DH, q, jnp.zeros_like(q)) * scale,
                   jnp.where(lane >= D_DH, q, jnp.zeros_like(q)) * scale))
    ones = jnp.ones((tq, w), BF16)

    def block(k0, nk, rows=slice(None), mask=None, first=False):
        n_stream = 2 * D_HP
        scores = []
        for hh in range(D_HP):
            k = k_ref[pl.ds(k0, nk), hh * w:(hh + 1) * w]
            for comp in range(2):
                s = _dot_nt(qs[hh][comp][rows], k)
                scores.append(s if mask is None else jnp.where(mask, s, -jnp.inf))
        m_news, alphas, probs = [], [], []
        for i in range(n_stream):
            s = scores[i]
            s_max = jnp.max(s, axis=-1, keepdims=True)
            if first:
                m_new = jnp.broadcast_to(s_max, (s.shape[0], w))
                alphas.append(None)
            else:
                m_old = m_ref[i, rows, :]
                m_new = jnp.maximum(m_old, s_max)
                alphas.append(jnp.exp(m_old - m_new))
            m_news.append(m_new)
            probs.append(jnp.exp(s - jnp.concatenate([m_new] * (s.shape[1] // w), axis=1)).astype(BF16))
        pvs = []
        for hh in range(D_HP):
            vext = jnp.concatenate([v_ref[pl.ds(k0, nk), hh * w:(hh + 1) * w], ones[:nk]], axis=1)
            for comp in range(2):
                pvs.append(_dot(probs[2 * hh + comp], vext))
        for i in range(n_stream):
            if first:
                a_ref[i, rows, :] = pvs[i]
            else:
                a_ref[i, rows, :] = jnp.concatenate([alphas[i]] * 2, axis=1) * a_ref[i, rows, :] + pvs[i]
            m_ref[i, rows, :] = m_news[i]

    hq = tq // 2
    d0 = pl.multiple_of(qi * tq, tq)

    def chunk_mask(q0, nk):
        rq = (lax.broadcasted_iota(I32, (hq, nk), 0) + q0) // CHUNK
        ck = lax.broadcasted_iota(I32, (hq, nk), 1) // CHUNK
        return ck <= rq

    block(d0, hq, rows=slice(0, hq), mask=chunk_mask(0, hq), first=True)
    block(d0, tq, rows=slice(hq, tq), mask=chunk_mask(hq, tq), first=True)

    def body(jj, carry):
        block(pl.multiple_of(jj * tq, tq), tq)
        return carry

    lax.fori_loop(0, qi, body, 0)

    lp = lam_ref[...]
    lam = (jnp.exp(jnp.sum(lp[0:1, :] * lp[1:2, :], axis=-1, keepdims=True))
           - jnp.exp(jnp.sum(lp[2:3, :] * lp[3:4, :], axis=-1, keepdims=True)) + lam_init)
    for hh in range(D_HP):
        a1 = a_ref[2 * hh]
        a2 = a_ref[2 * hh + 1]
        o = a1[:, :w] / a1[:, w:] - lam * (a2[:, :w] / a2[:, w:])
        out_ref[:, hh * w:(hh + 1) * w] = (_rms(o, gain_ref[...]) * (1.0 - lam_init)).astype(BF16)


def _diffattn(z, lam_p, d_gain, B, S, tq, lam_init):
    T = B * S
    nq = S // tq
    w = 2 * D_DH
    wp = D_HP * w
    return pl.pallas_call(
        functools.partial(_diffattn_kernel, tq=tq, lam_init=lam_init),
        grid=(B, D_HEADS // D_HP, nq),
        in_specs=[
            pl.BlockSpec((tq, wp), lambda b, h, i: (b * nq + i, OFF_QD // wp + h)),
            pl.BlockSpec((S, wp), lambda b, h, i: (b, OFF_KD // wp + h)),
            pl.BlockSpec((S, wp), lambda b, h, i: (b, OFF_VD // wp + h)),
            pl.BlockSpec((4, D_DH), lambda b, h, i: (0, 0)),
            pl.BlockSpec((1, w), lambda b, h, i: (0, 0)),
        ],
        out_specs=pl.BlockSpec((tq, wp), lambda b, h, i: (b * nq + i, h)),
        out_shape=jax.ShapeDtypeStruct((T, D_V), BF16),
        scratch_shapes=[
            pltpu.VMEM((2 * D_HP, tq, w), F32), pltpu.VMEM((2 * D_HP, tq, 2 * w), F32),
        ],
        compiler_params=_cparams(("arbitrary", "arbitrary", "arbitrary")),
        name="diffattn",
    )(z, z, z, lam_p, d_gain)


def _merge_kernel(hm_ref, hd_ref, gz_ref, x_ref, wbm_ref, wbd_ref, wout_ref, bg_ref, out_ref):
    bm = _dot(hm_ref[...], wbm_ref[...])
    bd = _dot(hd_ref[...], wbd_ref[...])
    g = _sigmoid(gz_ref[...].astype(F32) + bg_ref[...])
    merged = g[:, :D_MODEL] * bm + g[:, D_MODEL:] * bd
    out_ref[...] = x_ref[...] + _dot(merged.astype(BF16), wout_ref[...])


def _merge(hm, hd, z, x2d, w_bm, w_bd, w_out, b_gate, tm):
    T = x2d.shape[0]
    full = lambda i: (0, 0)
    return pl.pallas_call(
        _merge_kernel,
        grid=(T // tm,),
        in_specs=[
            pl.BlockSpec((tm, M_V), lambda i: (i, 0)),
            pl.BlockSpec((tm, D_V), lambda i: (i, 0)),
            pl.BlockSpec((tm, 2 * D_MODEL), lambda i: (i, OFF_G // (2 * D_MODEL))),
            pl.BlockSpec((tm, D_MODEL), lambda i: (i, 0)),
            pl.BlockSpec((M_V, D_MODEL), full),
            pl.BlockSpec((D_V, D_MODEL), full),
            pl.BlockSpec((D_MODEL, D_MODEL), full),
            pl.BlockSpec((1, 2 * D_MODEL), full),
        ],
        out_specs=pl.BlockSpec((tm, D_MODEL), lambda i: (i, 0)),
        out_shape=jax.ShapeDtypeStruct((T, D_MODEL), F32),
        compiler_params=_cparams(("arbitrary",)),
        name="merge",
    )(hm, hd, z, x2d, w_bm, w_bd, w_out, b_gate)


def _memkv_kernel(mem_ref, g_ref, w_ref, out_ref):
    out_ref[...] = _dot(_rms(mem_ref[...], g_ref[...]).astype(BF16), w_ref[...]).astype(BF16)


def _memkv(mem2d, g, wkv, n_mem):
    R = mem2d.shape[0]
    return pl.pallas_call(
        _memkv_kernel,
        grid=(R // n_mem,),
        in_specs=[
            pl.BlockSpec((n_mem, D_MODEL), lambda i: (i, 0)),
            pl.BlockSpec((1, D_MODEL), lambda i: (0, 0)),
            pl.BlockSpec((D_MODEL, 2 * D_MODEL), lambda i: (0, 0)),
        ],
        out_specs=pl.BlockSpec((n_mem, 2 * D_MODEL), lambda i: (i, 0)),
        out_shape=jax.ShapeDtypeStruct((R, 2 * D_MODEL), BF16),
        compiler_params=_cparams(("arbitrary",)),
        name="memkv",
    )(mem2d, g, wkv)


def _xattn_kernel(x_ref, g_ref, wq_ref, kv_ref, wo_ref, out_ref, o_ref):
    x = x_ref[...]
    h = _rms(x, g_ref[...]).astype(BF16)
    q = (_dot(h, wq_ref[...]) * (X_DH ** -0.5)).astype(BF16)
    scores = [_dot_nt(q[:, hd * X_DH:(hd + 1) * X_DH], kv_ref[:, hd * X_DH:(hd + 1) * X_DH]) for hd in range(X_HEADS)]
    for hd in range(X_HEADS):
        vh = kv_ref[:, D_MODEL + hd * X_DH:D_MODEL + (hd + 1) * X_DH]
        s = scores[hd]
        p = jnp.exp(s - jnp.max(s, axis=-1, keepdims=True))
        p = p / jnp.sum(p, axis=-1, keepdims=True)
        o_ref[:, hd * X_DH:(hd + 1) * X_DH] = _dot(p.astype(BF16), vh).astype(BF16)
    out_ref[...] = x + _dot(o_ref[...], wo_ref[...])


def _xattn(x1, g, wq, kvmem, wo, S, n_mem, tm):
    T = x1.shape[0]
    per_b = S // tm
    full = lambda i: (0, 0)
    return pl.pallas_call(
        _xattn_kernel,
        grid=(T // tm,),
        in_specs=[
            pl.BlockSpec((tm, D_MODEL), lambda i: (i, 0)),
            pl.BlockSpec((1, D_MODEL), full),
            pl.BlockSpec((D_MODEL, D_MODEL), full),
            pl.BlockSpec((n_mem, 2 * D_MODEL), lambda i: (i // per_b, 0)),
            pl.BlockSpec((D_MODEL, D_MODEL), full),
        ],
        out_specs=pl.BlockSpec((tm, D_MODEL), lambda i: (i, 0)),
        out_shape=jax.ShapeDtypeStruct((T, D_MODEL), F32),
        scratch_shapes=[pltpu.VMEM((tm, D_MODEL), BF16)],
        compiler_params=_cparams(("arbitrary",)),
        name="xattn",
    )(x1, g, wq, kvmem, wo)


def _router_kernel(x_ref, g_ref, wrh_ref, wrl_ref, br_ref, hp_ref, ids_ref, tw_ref, cnt_ref):
    @pl.when(pl.program_id(0) == 0)
    def _():
        cnt_ref[...] = jnp.zeros_like(cnt_ref)

    hn = _rms(x_ref[...], g_ref[...])
    hh, hl = _split_bf16(hn)
    hp_ref[...] = hh
    logits = _dot(hh, wrh_ref[...]) + _dot(hh, wrl_ref[...]) + _dot(hl, wrh_ref[...]) + br_ref[...]
    lane = lax.broadcasted_iota(I32, logits.shape, 1)
    lanef = lane.astype(F32)
    ids = jnp.zeros(logits.shape, F32)
    tw = jnp.zeros(logits.shape, F32)
    onehot = jnp.zeros(logits.shape, F32)
    v0 = None
    den = None
    for kk in range(TOP_K):
        mx = jnp.max(logits, axis=-1, keepdims=True)
        idx = jnp.min(jnp.where(logits == mx, lanef, float(LANES)), axis=-1, keepdims=True)
        sel = lanef == idx
        if kk == 0:
            v0 = mx
        e = jnp.exp(mx - v0)
        den = e if den is None else den + e
        ids = jnp.where(lane == kk, idx, ids)
        tw = jnp.where(lane == kk, e, tw)
        onehot = jnp.where(sel, 1.0, onehot)
        logits = jnp.where(sel, -jnp.inf, logits)
    ids_ref[...] = ids.astype(I32)
    tw_ref[...] = tw / den
    cnt_ref[...] += _ceil_rows(jnp.sum(onehot, axis=0, keepdims=True))


def _router(x2, g, wr_hi, wr_lo, b_r, tm):
    T = x2.shape[0]
    full = lambda i: (0, 0)
    return pl.pallas_call(
        _router_kernel,
        grid=(T // tm,),
        in_specs=[
            pl.BlockSpec((tm, D_MODEL), lambda i: (i, 0)),
            pl.BlockSpec((1, D_MODEL), full),
            pl.BlockSpec((D_MODEL, LANES), full),
            pl.BlockSpec((D_MODEL, LANES), full),
            pl.BlockSpec((1, LANES), full),
        ],
        out_specs=[
            pl.BlockSpec((tm, D_MODEL), lambda i: (i, 0)),
            pl.BlockSpec((tm, LANES), lambda i: (i, 0)),
            pl.BlockSpec((tm, LANES), lambda i: (i, 0)),
            pl.BlockSpec((1, LANES), full),
        ],
        out_shape=[
            jax.ShapeDtypeStruct((T, D_MODEL), BF16),
            jax.ShapeDtypeStruct((T, LANES), I32),
            jax.ShapeDtypeStruct((T, LANES), F32),
            jax.ShapeDtypeStruct((1, LANES), F32),
        ],
        compiler_params=_cparams(("arbitrary",)),
        name="router",
    )(x2, g, wr_hi, wr_lo, b_r)


SEG_WORDS = SUBLANES * LANES


def _slots_kernel(ids_ref, start_ref, ls_ref, lst_ref, seg_ref, run_ref):
    @pl.when(pl.program_id(0) == 0)
    def _():
        run_ref[...] = jnp.zeros_like(run_ref)

    ids = ids_ref[...]
    tm = ids.shape[0]
    lane = lax.broadcasted_iota(I32, ids.shape, 1)
    sels = [lane == ids[:, kk:kk + 1] for kk in range(TOP_K)]
    onehot = jnp.zeros(ids.shape, F32)
    for s in sels:
        onehot = jnp.where(s, 1.0, onehot)
    c8 = _ceil_rows(jnp.sum(onehot, axis=0, keepdims=True))
    er = lax.broadcasted_iota(I32, (LANES, LANES), 0)
    ec = lax.broadcasted_iota(I32, (LANES, LANES), 1)
    before = jnp.where(er < ec, 1.0, 0.0).astype(BF16)
    pieces = jnp.broadcast_to(c8 * (1.0 / SUBLANES), (SUBLANES, LANES)).astype(BF16)
    lo = _dot(pieces, before)[0:1, :] * SUBLANES
    r = lax.broadcasted_iota(I32, (tm, tm), 0)
    c = lax.broadcasted_iota(I32, (tm, tm), 1)
    strict = jnp.where(c < r, 1.0, 0.0).astype(BF16)
    slot = _dot(strict, onehot.astype(BF16)) + lo
    ls = jnp.zeros(ids.shape, F32)
    for kk, s in enumerate(sels):
        pk = jnp.sum(jnp.where(s, slot, 0.0), axis=-1, keepdims=True)
        ls = jnp.where(lane == kk, pk, ls)
    ls_ref[...] = ls
    hi = jnp.floor(ls * (1.0 / 32.0))
    rem = ls - 32.0 * hi
    pick = jnp.where(lax.broadcasted_iota(I32, (SUBLANES, LANES), 0) == lax.broadcasted_iota(I32, (SUBLANES, LANES), 1),
                     1.0, 0.0).astype(BF16)
    lst_ref[...] = 32.0 * _dot_nt(pick, hi.astype(BF16)) + _dot_nt(pick, rem.astype(BF16))
    row = lax.broadcasted_iota(I32, (SUBLANES, LANES), 0)
    off = start_ref[...] + run_ref[...]
    total = jnp.sum(c8, axis=-1, keepdims=True)
    seg = jnp.where(row == 0, c8, jnp.where(row == 1, lo, jnp.where(row == 2, off, jnp.where(row == 3, total, 0.0))))
    seg_ref[...] = seg.astype(I32)
    run_ref[...] += c8


def _slots(ids, starts, tm):
    T = ids.shape[0]
    nt = T // tm
    return pl.pallas_call(
        _slots_kernel,
        grid=(nt,),
        in_specs=[
            pl.BlockSpec((tm, LANES), lambda i: (i, 0)),
            pl.BlockSpec((1, LANES), lambda i: (0, 0)),
        ],
        out_specs=[
            pl.BlockSpec((tm, LANES), lambda i: (i, 0)),
            pl.BlockSpec((SUBLANES, tm), lambda i: (0, i)),
            pl.BlockSpec((SUBLANES, LANES), lambda i: (i, 0)),
        ],
        out_shape=[
            jax.ShapeDtypeStruct((T, LANES), F32),
            jax.ShapeDtypeStruct((SUBLANES, T), F32),
            jax.ShapeDtypeStruct((nt * SUBLANES, LANES), I32),
        ],
        scratch_shapes=[pltpu.VMEM((1, LANES), F32)],
        compiler_params=_cparams(("arbitrary",)),
        name="slots",
    )(ids, starts)


def _local_rows(tm):
    need = tm * TOP_K + N_EXPERTS * (SUBLANES - 1)
    return ((need + LANES - 1) // LANES) * LANES


BIG_SHIFT = 5
BIG_PIECE = 1 << BIG_SHIFT
SMALL_PIECES = tuple(1 << b for b in range(BIG_SHIFT - 1, 2, -1))


def _segment_starts(seg, make_copy):
    def expert(e, carry):
        cnt = seg(e)
        lo = seg(LANES + e)
        off = seg(2 * LANES + e)
        n_big = lax.shift_right_logical(cnt, BIG_SHIFT)

        def big(j, carry2):
            d = j * BIG_PIECE
            make_copy(pl.multiple_of(lo + d, SUBLANES), pl.multiple_of(off + d, SUBLANES), BIG_PIECE).start()
            return carry2

        lax.fori_loop(0, n_big, big, 0)
        for size in SMALL_PIECES:
            d = cnt & ~(2 * size - 1)

            @pl.when((cnt & size) != 0)
            def _(d=d, size=size):
                make_copy(pl.multiple_of(lo + d, SUBLANES), pl.multiple_of(off + d, SUBLANES), size).start()
        return carry

    for e in range(N_EXPERTS):
        expert(e, 0)


def _segment_waits(total_rows, make_copy):
    def big(j, carry):
        make_copy(0, 0, BIG_PIECE).wait()
        return carry

    lax.fori_loop(0, lax.shift_right_logical(total_rows, BIG_SHIFT), big, 0)
    for size in SMALL_PIECES:
        @pl.when((total_rows & size) != 0)
        def _(size=size):
            make_copy(0, 0, size).wait()


def _dispatch_kernel(h_ref, lst_ref, seg_hbm, xs_ref, sbuf_ref, seg_smem, prev_smem, sem_seg, sem_rows, *, tm):
    i = pl.program_id(0)
    n = pl.num_programs(0)
    slot = i % 2
    rows = sbuf_ref.shape[1]
    cp = pltpu.make_async_copy(seg_hbm.at[i], seg_smem, sem_seg)
    cp.start()
    lst = lst_ref[...].astype(I32)
    rid = lax.broadcasted_iota(I32, (rows, tm), 0)
    perm = jnp.zeros((rows, tm), F32)
    for kk in range(TOP_K):
        perm = jnp.where(rid == lst[kk:kk + 1, :], 1.0, perm)
    srt = _dot(perm.astype(BF16), h_ref[...])
    sbuf_ref[slot] = _pack_exact_bf16_pairs(srt)
    cp.wait()

    def copy_from(s):
        def make_copy(lo, off, nrows):
            return pltpu.make_async_copy(sbuf_ref.at[s, pl.ds(lo, nrows), :], xs_ref.at[pl.ds(off, nrows), :],
                                         sem_rows.at[s])
        return make_copy

    _segment_starts(lambda k: seg_smem[k], copy_from(slot))

    @pl.when(i > 0)
    def _():
        _segment_waits(prev_smem[0], copy_from(1 - slot))

    prev_smem[0] = seg_smem[3 * LANES]

    @pl.when(i == n - 1)
    def _():
        _segment_waits(prev_smem[0], copy_from(slot))


def _dispatch(h, lst, seg2d, n_rows, tm):
    T = h.shape[0]
    rows = _local_rows(tm)
    return pl.pallas_call(
        functools.partial(_dispatch_kernel, tm=tm),
        grid=(T // tm,),
        in_specs=[
            pl.BlockSpec((tm, D_MODEL), lambda i: (i, 0)),
            pl.BlockSpec((SUBLANES, tm), lambda i: (0, i)),
            pl.BlockSpec(memory_space=pl.ANY),
        ],
        out_specs=pl.BlockSpec(memory_space=pl.ANY),
        out_shape=jax.ShapeDtypeStruct((n_rows, D_MODEL // 2), U32),
        scratch_shapes=[
            pltpu.VMEM((2, rows, D_MODEL // 2), U32),
            pltpu.SMEM((SEG_WORDS,), I32),
            pltpu.SMEM((1,), I32),
            pltpu.SemaphoreType.DMA,
            pltpu.SemaphoreType.DMA((2,)),
        ],
        compiler_params=_cparams(("arbitrary",)),
        name="dispatch",
    )(h, lst, seg2d)


def _experts_kernel(te_ref, nu_ref, nv_ref, xs_ref, wgu_ref, bgu_ref, wdn_ref, bdn_ref, ys_ref, wgu_bf, wdn_bf):
    i = pl.program_id(0)

    @pl.when((i == 0) | (te_ref[i] != te_ref[jnp.maximum(i - 1, 0)]))
    def _():
        wgu_bf[...] = wgu_ref[0].astype(BF16)
        wdn_bf[...] = wdn_ref[0].astype(BF16)

    @pl.when(i < nu_ref[0])
    def _():
        live = lax.broadcasted_iota(I32, xs_ref.shape, 0) < nv_ref[i]
        lo, hi = _unpack_bf16_pairs(jnp.where(live, xs_ref[...], jnp.uint32(0)))
        xb = jnp.concatenate([lo.astype(BF16), hi.astype(BF16)], axis=1)
        gu = _dot(xb, wgu_bf[...]) + bgu_ref[0]
        gate = jnp.minimum(gu[:, :D_FF], SWIGLU_LIMIT)
        up = jnp.clip(gu[:, D_FF:], -SWIGLU_LIMIT, SWIGLU_LIMIT)
        act = (up + 1.0) * (gate * _sigmoid(SWIGLU_ALPHA * gate))
        y = _dot(act.astype(BF16), wdn_bf[...]) + bdn_ref[0]
        ys_ref[...] = _pack_bf16_pairs(y)

    @pl.when(i >= nu_ref[0])
    def _():
        ys_ref[...] = jnp.zeros_like(ys_ref)


def _experts(tile_expert, n_used, tile_valid, xs, w_gu, b_gu, w_dn, b_dn, tg):
    P = xs.shape[0]
    half = D_MODEL // 2
    grid_spec = pltpu.PrefetchScalarGridSpec(
        num_scalar_prefetch=3,
        grid=(P // tg,),
        in_specs=[
            pl.BlockSpec((tg, half), lambda i, te, nu, nv: (jnp.minimum(i, jnp.maximum(nu[0] - 1, 0)), 0)),
            pl.BlockSpec((1, D_MODEL, 2 * D_FF), lambda i, te, nu, nv: (te[i], 0, 0)),
            pl.BlockSpec((1, 1, 2 * D_FF), lambda i, te, nu, nv: (te[i], 0, 0)),
            pl.BlockSpec((1, D_FF, D_MODEL), lambda i, te, nu, nv: (te[i], 0, 0)),
            pl.BlockSpec((1, 1, D_MODEL), lambda i, te, nu, nv: (te[i], 0, 0)),
        ],
        out_specs=pl.BlockSpec((tg, half), lambda i, te, nu, nv: (i, 0)),
        scratch_shapes=[pltpu.VMEM((D_MODEL, 2 * D_FF), BF16), pltpu.VMEM((D_FF, D_MODEL), BF16)],
    )
    return pl.pallas_call(
        _experts_kernel,
        grid_spec=grid_spec,
        out_shape=jax.ShapeDtypeStruct((P, half), U32),
        compiler_params=_cparams(("arbitrary",)),
        name="experts",
    )(tile_expert, n_used, tile_valid, xs, w_gu, b_gu, w_dn, b_dn)


def _combine_kernel(seg_hbm, ys_hbm, ls_ref, tw_ref, x_ref, g_ref, out_ref, ybuf_ref, seg_smem, sem_seg, sem_rows, *,
                    tm, final_norm):
    i = pl.program_id(0)
    n = pl.num_programs(0)
    slot = i % 2
    rows = ybuf_ref.shape[1]

    def copy_into(s):
        def make_copy(lo, off, nrows):
            return pltpu.make_async_copy(ys_hbm.at[pl.ds(off, nrows), :], ybuf_ref.at[s, pl.ds(lo, nrows), :],
                                         sem_rows.at[s])
        return make_copy

    def request(step, s):
        cp = pltpu.make_async_copy(seg_hbm.at[step], seg_smem.at[s], sem_seg)
        cp.start()
        cp.wait()
        _segment_starts(lambda k: seg_smem[s, k], copy_into(s))

    @pl.when(i == 0)
    def _():
        ybuf_ref[...] = jnp.zeros_like(ybuf_ref)
        request(0, 0)

    @pl.when(i + 1 < n)
    def _():
        request(i + 1, 1 - slot)

    ls = ls_ref[...].astype(I32)
    tw = tw_ref[...]
    cid = lax.broadcasted_iota(I32, (tm, rows), 1)
    wmat = jnp.zeros((tm, rows), F32)
    for kk in range(TOP_K):
        wmat = jnp.where(cid == ls[:, kk:kk + 1], tw[:, kk:kk + 1], wmat)
    wmat = wmat.astype(BF16)
    _segment_waits(seg_smem[slot, 3 * LANES], copy_into(slot))
    lo, hi = _unpack_bf16_pairs(ybuf_ref[slot])
    moe = jnp.concatenate([_dot(wmat, lo.astype(BF16)), _dot(wmat, hi.astype(BF16))], axis=1)
    x3 = x_ref[...] + moe
    out_ref[...] = _rms(x3, g_ref[...]) if final_norm else x3


def _combine(seg2d, ys, ls, tw, x2, g, tm, final_norm):
    T = x2.shape[0]
    rows = _local_rows(tm)
    return pl.pallas_call(
        functools.partial(_combine_kernel, tm=tm, final_norm=final_norm),
        grid=(T // tm,),
        in_specs=[
            pl.BlockSpec(memory_space=pl.ANY),
            pl.BlockSpec(memory_space=pl.ANY),
            pl.BlockSpec((tm, LANES), lambda i: (i, 0)),
            pl.BlockSpec((tm, LANES), lambda i: (i, 0)),
            pl.BlockSpec((tm, D_MODEL), lambda i: (i, 0)),
            pl.BlockSpec((1, D_MODEL), lambda i: (0, 0)),
        ],
        out_specs=pl.BlockSpec((tm, D_MODEL), lambda i: (i, 0)),
        out_shape=jax.ShapeDtypeStruct((T, D_MODEL), F32),
        scratch_shapes=[
            pltpu.VMEM((2, rows, D_MODEL // 2), U32),
            pltpu.SMEM((2, SEG_WORDS), I32),
            pltpu.SemaphoreType.DMA,
            pltpu.SemaphoreType.DMA((2,)),
        ],
        compiler_params=_cparams(("arbitrary",)),
        name="combine",
    )(seg2d, ys, ls, tw, x2, g)


def _tiles(B, S):
    T = B * S
    return dict(
        tm_in=min(1024, T), tn_in=2048,
        ts=min(256, S),
        tq=min(1024, S),
        tm_proj=min(1024, S),
        tm_route=min(256, T),
        tg=1024,
    )


def _pad_lanes(a, n=LANES, value=0.0):
    return jnp.pad(a, ((0, 0), (0, n - a.shape[1])), constant_values=value)


def kernel(x, mem, norm_mix, w_in, conv_w, b_if, mlstm_gain, diff_lambda, diff_gain, w_branch_m, w_branch_d,
           b_gate, w_out, norm_xattn, norm_mem, wq_x, wkv_x, wo_x, norm_ffn, w_router, b_router, w_gu, b_gu,
           w_dn, b_dn, norm_final):
    B, S, D = x.shape
    n_mem = mem.shape[1]
    T = B * S
    depth = norm_mix.shape[0]
    tl = _tiles(B, S)
    x2d = x.reshape(T, D)
    mem2d = mem.reshape(B * n_mem, D)

    for l in range(depth):
        lam_init = 0.8 - 0.6 * math.exp(-0.3 * l)
        wl = w_in[l]
        if_lo = 2 * M_QK + 2 * M_V
        w_main = jnp.concatenate([wl[:, :if_lo], wl[:, if_lo + 2 * M_HEADS:]], axis=1).astype(BF16)
        w_if = wl[:, if_lo:if_lo + 2 * M_HEADS]
        w_ifp = _pad_lanes(w_if).astype(BF16)
        w_ift = w_if.T.astype(BF16)
        bif = _pad_lanes(b_if[l][None, :])
        bift = jnp.broadcast_to(b_if[l][:, None], (SUBLANES, LANES))

        z, zif, zift = _inproj(x2d, norm_mix[l][None, :], w_main, w_ifp, w_ift, tl["tm_in"], tl["tn_in"])
        hm = _mlstm(z, zif, zift, conv_w[l], bif, bift, mlstm_gain[l].reshape(1, M_V), B, S, tl["ts"])
        hd = _diffattn(z, diff_lambda[l], diff_gain[l][None, :], B, S, tl["tq"], lam_init)
        x1 = _merge(hm, hd, z, x2d, w_branch_m[l].astype(BF16), w_branch_d[l].astype(BF16),
                    w_out[l].astype(BF16), b_gate[l][None, :], tl["tm_proj"])

        kvmem = _memkv(mem2d, norm_mem[l][None, :], wkv_x[l].astype(BF16), n_mem)
        x2 = _xattn(x1, norm_xattn[l][None, :], wq_x[l].astype(BF16), kvmem, wo_x[l].astype(BF16),
                    S, n_mem, tl["tm_proj"])

        wr = _pad_lanes(w_router[l])
        wr_hi = wr.astype(BF16)
        wr_lo = (wr - wr_hi.astype(F32)).astype(BF16)
        br = _pad_lanes(b_router[l][None, :], value=-jnp.inf)
        tm_r = tl["tm_route"]
        tg = tl["tg"]
        hp, ids, tw, cnt = _router(x2, norm_ffn[l][None, :], wr_hi, wr_lo, br, tm_r)

        counts = cnt[0, :N_EXPERTS].astype(I32)
        padded = ((counts + tg - 1) // tg) * tg
        ends = jnp.cumsum(padded)
        starts = ends - padded
        max_rows = T * TOP_K + (T // tm_r) * N_EXPERTS * (SUBLANES - 1)
        n_tiles = -(-max_rows // tg) + N_EXPERTS
        tile_row0 = jnp.arange(n_tiles, dtype=I32) * tg
        tile_expert = jnp.minimum(jnp.sum((tile_row0[:, None] >= ends[None, :]).astype(I32), axis=1), N_EXPERTS - 1)
        n_used = (ends[-1] // tg).astype(I32).reshape(1)
        last_used = tile_expert[jnp.maximum(n_used[0] - 1, 0)]
        tile_expert = jnp.where(tile_row0 < ends[-1], tile_expert, last_used)
        tile_valid = jnp.clip((starts + counts)[tile_expert] - tile_row0, 0, tg).astype(I32)

        ls, lst, seg = _slots(ids, _pad_lanes(starts.astype(F32)[None, :]), tm_r)
        seg2d = seg.reshape(T // tm_r, SEG_WORDS)

        xs = _dispatch(hp, lst, seg2d, n_tiles * tg, tm_r)
        ys = _experts(tile_expert, n_used, tile_valid, xs, w_gu[l], b_gu[l][:, None, :], w_dn[l], b_dn[l][:, None, :],
                      tg)
        x2d = _combine(seg2d, ys, ls, tw, x2, norm_final[None, :], tm_r, final_norm=(l == depth - 1))
    return x2d.reshape(B, S, D)
```
